```python
import jax, jax.numpy as jnp
from jax import lax
import numpy as np

D_MODEL = 1024
BATCH = 8
SEQ = 4096
DEPTH = 1

HEAD_DIM = 128
N_ATTN_HEADS = D_MODEL // HEAD_DIM
D_ATTN = N_ATTN_HEADS * HEAD_DIM
D_LRU = D_MODEL
N_LRU_BLOCKS = 8
LRU_BLOCK = D_LRU // N_LRU_BLOCKS
CONV_WIDTH = 4
LRU_C = 8.0
D_MIX = D_ATTN + D_LRU
D_PLE = 256
Q_BLOCK = 128
RMS_EPS = 1e-6
D_IN = 4 * D_ATTN + N_ATTN_HEADS + 2 * D_LRU
SPLIT_POINTS = [D_ATTN, 2 * D_ATTN, 3 * D_ATTN, 3 * D_ATTN + N_ATTN_HEADS,
                4 * D_ATTN + N_ATTN_HEADS, 4 * D_ATTN + N_ATTN_HEADS + D_LRU]

kernel_name = "hymba_fox_rglru_sandwich_ple"


def rmsnorm(x, g):
    xf = x.astype(jnp.float32)
    y = xf * lax.rsqrt(jnp.mean(xf * xf, axis=-1, keepdims=True) + RMS_EPS) * g.astype(jnp.float32)
    return y.astype(x.dtype)


def forgetting_attention(q, k, v, f_logit):
    B, S, H, Dh = q.shape
    nblk = S // Q_BLOCK
    scale = HEAD_DIM ** -0.5
    q = q.transpose(0, 2, 1, 3)
    k = k.transpose(0, 2, 1, 3)
    v = v.transpose(0, 2, 1, 3)
    c = jnp.cumsum(jax.nn.log_sigmoid(f_logit.astype(jnp.float32)), axis=1).transpose(0, 2, 1)
    qb = q.reshape(B, H, nblk, Q_BLOCK, Dh).transpose(2, 0, 1, 3, 4)
    cb = c.reshape(B, H, nblk, Q_BLOCK).transpose(2, 0, 1, 3)
    kpos = jnp.arange(S)

    def one_block(args):
        qi, ci, bi = args
        qpos = bi * Q_BLOCK + jnp.arange(Q_BLOCK)
        s = jnp.einsum('bhqd,bhkd->bhqk', qi, k, preferred_element_type=jnp.float32) * scale
        s = s + (ci[:, :, :, None] - c[:, :, None, :])
        s = jnp.where(kpos[None, :] <= qpos[:, None], s, -jnp.inf)
        w = jax.nn.softmax(s, axis=-1)
        return jnp.einsum('bhqk,bhkd->bhqd', w.astype(v.dtype), v)

    ob = lax.map(one_block, (qb, cb, jnp.arange(nblk)))
    return ob.transpose(1, 0, 3, 2, 4).reshape(B, S, H * Dh)


def causal_depthwise_conv(x, w, b):
    y = lax.conv_general_dilated(
        x, w[:, None, :].astype(x.dtype), window_strides=(1,),
        padding=[(CONV_WIDTH - 1, 0)], dimension_numbers=('NWC', 'WIO', 'NWC'),
        feature_group_count=x.shape[-1])
    return y + b


def rg_lru(xc, w_r, b_r, w_i, b_i, lam):
    B, S, _ = xc.shape
    xb = xc.reshape(B, S, N_LRU_BLOCKS, LRU_BLOCK)
    r = jax.nn.sigmoid((jnp.einsum('bsnj,njk->bsnk', xb, w_r).reshape(B, S, D_LRU) + b_r).astype(jnp.float32))
    i = jax.nn.sigmoid((jnp.einsum('bsnj,njk->bsnk', xb, w_i).reshape(B, S, D_LRU) + b_i).astype(jnp.float32))
    log_a = -LRU_C * r * jax.nn.softplus(-lam.astype(jnp.float32))
    a = jnp.exp(log_a)
    u = jnp.sqrt(-jnp.expm1(2.0 * log_a)) * (i * xc.astype(jnp.float32))

    def combine(left, right):
        a_l, b_l = left
        a_r, b_r2 = right
        return a_l * a_r, a_r * b_l + b_r2

    _, h = lax.associative_scan(combine, (a, u), axis=1)
    return h.astype(xc.dtype)


def _fwd_setup_inputs(seed: int = 0) -> dict:
    key = jax.random.key(seed)
    ks = jax.random.split(key, 24)
    f32 = jnp.float32
    nrm = lambda k, shape, s: jax.random.normal(k, shape, f32) * s
    x = jax.random.normal(ks[0], (BATCH, SEQ, D_MODEL), f32)
    p = jax.random.normal(ks[1], (DEPTH, BATCH, SEQ, D_PLE), f32)
    w_in = nrm(ks[2], (DEPTH, D_MODEL, D_IN), D_MODEL ** -0.5)
    b_f = jnp.linspace(1.0, 6.0, N_ATTN_HEADS, dtype=f32)[None, :] + nrm(ks[3], (DEPTH, N_ATTN_HEADS), 0.1)
    pre_gain = 1.0 + nrm(ks[4], (DEPTH, D_MODEL), 0.05)
    post_gain = 1.0 + nrm(ks[5], (DEPTH, D_MODEL), 0.05)
    conv_w = nrm(ks[6], (DEPTH, CONV_WIDTH, D_LRU), CONV_WIDTH ** -0.5)
    conv_b = nrm(ks[7], (DEPTH, D_LRU), 0.01)
    w_rgate = nrm(ks[8], (DEPTH, N_LRU_BLOCKS, LRU_BLOCK, LRU_BLOCK), LRU_BLOCK ** -0.5)
    b_rgate = nrm(ks[9], (DEPTH, D_LRU), 0.01)
    w_igate = nrm(ks[10], (DEPTH, N_LRU_BLOCKS, LRU_BLOCK, LRU_BLOCK), LRU_BLOCK ** -0.5)
    b_igate = nrm(ks[11], (DEPTH, D_LRU), 0.01)
    a_pow = jax.random.uniform(ks[12], (DEPTH, D_LRU), f32, minval=0.9, maxval=0.999)
    a0 = a_pow ** (1.0 / LRU_C)
    lru_lambda = jnp.log(a0) - jnp.log1p(-a0)
    attn_out_gain = 1.0 + nrm(ks[13], (DEPTH, D_ATTN), 0.05)
    lru_out_gain = 1.0 + nrm(ks[14], (DEPTH, D_LRU), 0.05)
    w_out = nrm(ks[15], (DEPTH, D_MIX, D_MODEL), D_MIX ** -0.5)
    w_ple = nrm(ks[16], (DEPTH, D_PLE, D_MODEL), D_PLE ** -0.5)
    ple_gain = 1.0 + nrm(ks[17], (DEPTH, D_MODEL), 0.05)
    w_ple_gate = nrm(ks[18], (DEPTH, D_MODEL, D_MODEL), D_MODEL ** -0.5)
    b_ple_gate = nrm(ks[19], (DEPTH, D_MODEL), 0.01)
    return {"x": x, "p": p, "w_in": w_in, "b_f": b_f, "pre_gain": pre_gain,
            "post_gain": post_gain, "conv_w": conv_w, "conv_b": conv_b,
            "w_rgate": w_rgate, "b_rgate": b_rgate, "w_igate": w_igate, "b_igate": b_igate,
            "lru_lambda": lru_lambda, "attn_out_gain": attn_out_gain, "lru_out_gain": lru_out_gain,
            "w_out": w_out, "w_ple": w_ple, "ple_gain": ple_gain,
            "w_ple_gate": w_ple_gate, "b_ple_gate": b_ple_gate}


def _fwd_reference(x, p, w_in, b_f, pre_gain, post_gain, conv_w, conv_b, w_rgate, b_rgate,
              w_igate, b_igate, lru_lambda, attn_out_gain, lru_out_gain, w_out,
              w_ple, ple_gain, w_ple_gate, b_ple_gate):
    B, S, _ = x.shape
    h = x
    for i in range(DEPTH):
        xn = rmsnorm(h, pre_gain[i])
        z = xn @ w_in[i]
        q, k, v, fl, g_attn, x_lru, g_lru = jnp.split(z, SPLIT_POINTS, axis=-1)
        fl = fl + b_f[i]
        o_attn = forgetting_attention(q.reshape(B, S, N_ATTN_HEADS, HEAD_DIM),
                                      k.reshape(B, S, N_ATTN_HEADS, HEAD_DIM),
                                      v.reshape(B, S, N_ATTN_HEADS, HEAD_DIM), fl)
        y_attn = rmsnorm(o_attn, attn_out_gain[i]) * jax.nn.silu(g_attn)
        xc = causal_depthwise_conv(x_lru, conv_w[i], conv_b[i])
        o_lru = rg_lru(xc, w_rgate[i], b_rgate[i], w_igate[i], b_igate[i], lru_lambda[i])
        y_lru = rmsnorm(o_lru, lru_out_gain[i]) * jax.nn.silu(g_lru)
        mix = jnp.concatenate([y_attn, y_lru], axis=-1) @ w_out[i]
        h = h + rmsnorm(mix, post_gain[i])
        e = rmsnorm(p[i] @ w_ple[i], ple_gain[i])
        gate = jax.nn.sigmoid(h @ w_ple_gate[i] + b_ple_gate[i])
        h = h + gate * e
    return h


import jax as _jax
import jax.numpy as _jnp

TWIN_FORMAT = 'train_step'
FWD_PARAMS = ['x', 'p', 'w_in', 'b_f', 'pre_gain', 'post_gain', 'conv_w', 'conv_b', 'w_rgate', 'b_rgate', 'w_igate', 'b_igate', 'lru_lambda', 'attn_out_gain', 'lru_out_gain', 'w_out', 'w_ple', 'ple_gain', 'w_ple_gate', 'b_ple_gate']
TWIN_WEIGHTS = ['w_in', 'b_f', 'pre_gain', 'post_gain', 'conv_w', 'conv_b', 'w_rgate', 'b_rgate', 'w_igate', 'b_igate', 'lru_lambda', 'attn_out_gain', 'lru_out_gain', 'w_out', 'w_ple', 'ple_gain', 'w_ple_gate', 'b_ple_gate']
TWIN_DIFF_INPUT = 'x'
TWIN_INPUTS = ['x', 'p', 'w_in', 'b_f', 'pre_gain', 'post_gain', 'conv_w', 'conv_b', 'w_rgate', 'b_rgate', 'w_igate', 'b_igate', 'lru_lambda', 'attn_out_gain', 'lru_out_gain', 'w_out', 'w_ple', 'ple_gain', 'w_ple_gate', 'b_ple_gate', 'loss_target', 'm_w_in', 'm_b_f', 'm_pre_gain', 'm_post_gain', 'm_conv_w', 'm_conv_b', 'm_w_rgate', 'm_b_rgate', 'm_w_igate', 'm_b_igate', 'm_lru_lambda', 'm_attn_out_gain', 'm_lru_out_gain', 'm_w_out', 'm_w_ple', 'm_ple_gain', 'm_w_ple_gate', 'm_b_ple_gate', 'v_w_in', 'v_b_f', 'v_pre_gain', 'v_post_gain', 'v_conv_w', 'v_conv_b', 'v_w_rgate', 'v_b_rgate', 'v_w_igate', 'v_b_igate', 'v_lru_lambda', 'v_attn_out_gain', 'v_lru_out_gain', 'v_w_out', 'v_w_ple', 'v_ple_gain', 'v_w_ple_gate', 'v_b_ple_gate']
TWIN_OUTPUTS = ['loss', 'grad_x', 'grad_w_in', 'grad_b_f', 'grad_pre_gain', 'grad_post_gain', 'grad_conv_w', 'grad_conv_b', 'grad_w_rgate', 'grad_b_rgate', 'grad_w_igate', 'grad_b_igate', 'grad_lru_lambda', 'grad_attn_out_gain', 'grad_lru_out_gain', 'grad_w_out', 'grad_w_ple', 'grad_ple_gain', 'grad_w_ple_gate', 'grad_b_ple_gate', 'delta_w_in', 'delta_b_f', 'delta_pre_gain', 'delta_post_gain', 'delta_conv_w', 'delta_conv_b', 'delta_w_rgate', 'delta_b_rgate', 'delta_w_igate', 'delta_b_igate', 'delta_lru_lambda', 'delta_attn_out_gain', 'delta_lru_out_gain', 'delta_w_out', 'delta_w_ple', 'delta_ple_gain', 'delta_w_ple_gate', 'delta_b_ple_gate', 'new_m_w_in', 'new_m_b_f', 'new_m_pre_gain', 'new_m_post_gain', 'new_m_conv_w', 'new_m_conv_b', 'new_m_w_rgate', 'new_m_b_rgate', 'new_m_w_igate', 'new_m_b_igate', 'new_m_lru_lambda', 'new_m_attn_out_gain', 'new_m_lru_out_gain', 'new_m_w_out', 'new_m_w_ple', 'new_m_ple_gain', 'new_m_w_ple_gate', 'new_m_b_ple_gate', 'new_v_w_in', 'new_v_b_f', 'new_v_pre_gain', 'new_v_post_gain', 'new_v_conv_w', 'new_v_conv_b', 'new_v_w_rgate', 'new_v_b_rgate', 'new_v_w_igate', 'new_v_b_igate', 'new_v_lru_lambda', 'new_v_attn_out_gain', 'new_v_lru_out_gain', 'new_v_w_out', 'new_v_w_ple', 'new_v_ple_gain', 'new_v_w_ple_gate', 'new_v_b_ple_gate']
TWIN_LEAF_KINDS = {'loss': 'loss', 'grad_x': 'grad_x', 'grad_w_in': 'grad_w', 'grad_b_f': 'grad_w', 'grad_pre_gain': 'grad_w', 'grad_post_gain': 'grad_w', 'grad_conv_w': 'grad_w', 'grad_conv_b': 'grad_w', 'grad_w_rgate': 'grad_w', 'grad_b_rgate': 'grad_w', 'grad_w_igate': 'grad_w', 'grad_b_igate': 'grad_w', 'grad_lru_lambda': 'grad_w', 'grad_attn_out_gain': 'grad_w', 'grad_lru_out_gain': 'grad_w', 'grad_w_out': 'grad_w', 'grad_w_ple': 'grad_w', 'grad_ple_gain': 'grad_w', 'grad_w_ple_gate': 'grad_w', 'grad_b_ple_gate': 'grad_w', 'delta_w_in': 'delta_w', 'delta_b_f': 'delta_w', 'delta_pre_gain': 'delta_w', 'delta_post_gain': 'delta_w', 'delta_conv_w': 'delta_w', 'delta_conv_b': 'delta_w', 'delta_w_rgate': 'delta_w', 'delta_b_rgate': 'delta_w', 'delta_w_igate': 'delta_w', 'delta_b_igate': 'delta_w', 'delta_lru_lambda': 'delta_w', 'delta_attn_out_gain': 'delta_w', 'delta_lru_out_gain': 'delta_w', 'delta_w_out': 'delta_w', 'delta_w_ple': 'delta_w', 'delta_ple_gain': 'delta_w', 'delta_w_ple_gate': 'delta_w', 'delta_b_ple_gate': 'delta_w', 'new_m_w_in': 'new_m', 'new_m_b_f': 'new_m', 'new_m_pre_gain': 'new_m', 'new_m_post_gain': 'new_m', 'new_m_conv_w': 'new_m', 'new_m_conv_b': 'new_m', 'new_m_w_rgate': 'new_m', 'new_m_b_rgate': 'new_m', 'new_m_w_igate': 'new_m', 'new_m_b_igate': 'new_m', 'new_m_lru_lambda': 'new_m', 'new_m_attn_out_gain': 'new_m', 'new_m_lru_out_gain': 'new_m', 'new_m_w_out': 'new_m', 'new_m_w_ple': 'new_m', 'new_m_ple_gain': 'new_m', 'new_m_w_ple_gate': 'new_m', 'new_m_b_ple_gate': 'new_m', 'new_v_w_in': 'new_v', 'new_v_b_f': 'new_v', 'new_v_pre_gain': 'new_v', 'new_v_post_gain': 'new_v', 'new_v_conv_w': 'new_v', 'new_v_conv_b': 'new_v', 'new_v_w_rgate': 'new_v', 'new_v_b_rgate': 'new_v', 'new_v_w_igate': 'new_v', 'new_v_b_igate': 'new_v', 'new_v_lru_lambda': 'new_v', 'new_v_attn_out_gain': 'new_v', 'new_v_lru_out_gain': 'new_v', 'new_v_w_out': 'new_v', 'new_v_w_ple': 'new_v', 'new_v_ple_gain': 'new_v', 'new_v_w_ple_gate': 'new_v', 'new_v_b_ple_gate': 'new_v'}


def _forward(args):
    return _fwd_reference(*[args[k] for k in FWD_PARAMS])


def _output_shape():
    out = _jax.eval_shape(lambda: _forward(_fwd_setup_inputs(0)))
    return out.shape, out.dtype

N_MICROBATCH = 1
ADAM_LR = 0.001
ADAM_B1 = 0.9
ADAM_B2 = 0.999
ADAM_EPS = 1e-08
ADAM_WD = 0.01
ADAM_STEP = 10
PER_EXAMPLE_BATCH_AXIS = {'x': 0, 'p': 1, 'loss_target': 0}
SHARED_INPUTS = []
_WEIGHT_DTYPES = {'w_in': _jnp.float32, 'b_f': _jnp.float32, 'pre_gain': _jnp.float32, 'post_gain': _jnp.float32, 'conv_w': _jnp.float32, 'conv_b': _jnp.float32, 'w_rgate': _jnp.float32, 'b_rgate': _jnp.float32, 'w_igate': _jnp.float32, 'b_igate': _jnp.float32, 'lru_lambda': _jnp.float32, 'attn_out_gain': _jnp.float32, 'lru_out_gain': _jnp.float32, 'w_out': _jnp.float32, 'w_ple': _jnp.float32, 'ple_gain': _jnp.float32, 'w_ple_gate': _jnp.float32, 'b_ple_gate': _jnp.float32}
MOMENT_SCALE = {'w_in': 2.013872e-01, 'b_f': 1.053560e+00, 'pre_gain': 5.335581e-01, 'post_gain': 3.269542e+01, 'conv_w': 3.044744e-01, 'conv_b': 9.646098e+00, 'w_rgate': 1.681577e-01, 'b_rgate': 8.949160e-02, 'w_igate': 3.122116e-01, 'b_igate': 8.741782e-02, 'lru_lambda': 1.432622e-01, 'attn_out_gain': 2.304927e-01, 'lru_out_gain': 3.378312e-01, 'w_out': 3.830251e-01, 'w_ple': 1.853396e-01, 'ple_gain': 1.029163e+01, 'w_ple_gate': 2.382437e-01, 'b_ple_gate': 2.936514e+00}


def _to_microbatches(a, axis):
    t = _jnp.moveaxis(a, axis, 0)
    t = t.reshape((N_MICROBATCH, t.shape[0] // N_MICROBATCH) + t.shape[1:])
    return _jnp.moveaxis(t, 1, axis + 1)


def setup_inputs(seed: int = 0) -> dict:
    inp = _fwd_setup_inputs(seed)
    key = _jax.random.fold_in(_jax.random.key(seed), 7919)
    shape, _ = _output_shape()
    out = dict(inp)
    out["loss_target"] = _jax.random.normal(_jax.random.fold_in(key, 0), shape, _jnp.float32)
    for i, name in enumerate(TWIN_WEIGHTS):
        w = inp[name].astype(_jnp.float32)
        if MOMENT_SCALE is None:
            s = _jnp.sqrt(_jnp.mean(_jnp.square(w)) + 1e-30)
        else:
            s = MOMENT_SCALE[name]
        km, kv = _jax.random.split(_jax.random.fold_in(key, i + 1))
        out[name] = w
        out["m_" + name] = s * _jax.random.normal(km, w.shape, _jnp.float32)
        out["v_" + name] = (s * s) * _jax.random.uniform(kv, w.shape, _jnp.float32, 0.5, 1.5)
    if N_MICROBATCH > 1:
        for name, axis in PER_EXAMPLE_BATCH_AXIS.items():
            out[name] = _to_microbatches(out[name], axis)
    return {'x': out['x'], 'p': out['p'], 'w_in': out['w_in'], 'b_f': out['b_f'], 'pre_gain': out['pre_gain'], 'post_gain': out['post_gain'], 'conv_w': out['conv_w'], 'conv_b': out['conv_b'], 'w_rgate': out['w_rgate'], 'b_rgate': out['b_rgate'], 'w_igate': out['w_igate'], 'b_igate': out['b_igate'], 'lru_lambda': out['lru_lambda'], 'attn_out_gain': out['attn_out_gain'], 'lru_out_gain': out['lru_out_gain'], 'w_out': out['w_out'], 'w_ple': out['w_ple'], 'ple_gain': out['ple_gain'], 'w_ple_gate': out['w_ple_gate'], 'b_ple_gate': out['b_ple_gate'], 'loss_target': out['loss_target'], 'm_w_in': out['m_w_in'], 'm_b_f': out['m_b_f'], 'm_pre_gain': out['m_pre_gain'], 'm_post_gain': out['m_post_gain'], 'm_conv_w': out['m_conv_w'], 'm_conv_b': out['m_conv_b'], 'm_w_rgate': out['m_w_rgate'], 'm_b_rgate': out['m_b_rgate'], 'm_w_igate': out['m_w_igate'], 'm_b_igate': out['m_b_igate'], 'm_lru_lambda': out['m_lru_lambda'], 'm_attn_out_gain': out['m_attn_out_gain'], 'm_lru_out_gain': out['m_lru_out_gain'], 'm_w_out': out['m_w_out'], 'm_w_ple': out['m_w_ple'], 'm_ple_gain': out['m_ple_gain'], 'm_w_ple_gate': out['m_w_ple_gate'], 'm_b_ple_gate': out['m_b_ple_gate'], 'v_w_in': out['v_w_in'], 'v_b_f': out['v_b_f'], 'v_pre_gain': out['v_pre_gain'], 'v_post_gain': out['v_post_gain'], 'v_conv_w': out['v_conv_w'], 'v_conv_b': out['v_conv_b'], 'v_w_rgate': out['v_w_rgate'], 'v_b_rgate': out['v_b_rgate'], 'v_w_igate': out['v_w_igate'], 'v_b_igate': out['v_b_igate'], 'v_lru_lambda': out['v_lru_lambda'], 'v_attn_out_gain': out['v_attn_out_gain'], 'v_lru_out_gain': out['v_lru_out_gain'], 'v_w_out': out['v_w_out'], 'v_w_ple': out['v_w_ple'], 'v_ple_gain': out['v_ple_gain'], 'v_w_ple_gate': out['v_w_ple_gate'], 'v_b_ple_gate': out['v_b_ple_gate']}


def _loss(weights, diff, rest, loss_target):
    with _jax.named_scope("forward"):
        args = {**rest, TWIN_DIFF_INPUT: diff, **{k: w.astype(_WEIGHT_DTYPES[k]) for k, w in weights.items()}}
        y = _forward(args)
    with _jax.named_scope("loss_head"):
        err = _jnp.square(y.astype(_jnp.float32) - loss_target)
        return 0.5 * _jnp.sum(_jnp.mean(err, axis=-1)) if err.ndim else 0.5 * err


def _adamw(w, g, m, v):
    m = ADAM_B1 * m + (1.0 - ADAM_B1) * g
    v = ADAM_B2 * v + (1.0 - ADAM_B2) * _jnp.square(g)
    m_hat = m / (1.0 - ADAM_B1 ** ADAM_STEP)
    v_hat = v / (1.0 - ADAM_B2 ** ADAM_STEP)
    delta = -ADAM_LR * (m_hat / (_jnp.sqrt(v_hat) + ADAM_EPS) + ADAM_WD * w)
    return delta, m, v


def reference(x, p, w_in, b_f, pre_gain, post_gain, conv_w, conv_b, w_rgate, b_rgate, w_igate, b_igate, lru_lambda, attn_out_gain, lru_out_gain, w_out, w_ple, ple_gain, w_ple_gate, b_ple_gate, loss_target, m_w_in, m_b_f, m_pre_gain, m_post_gain, m_conv_w, m_conv_b, m_w_rgate, m_b_rgate, m_w_igate, m_b_igate, m_lru_lambda, m_attn_out_gain, m_lru_out_gain, m_w_out, m_w_ple, m_ple_gain, m_w_ple_gate, m_b_ple_gate, v_w_in, v_b_f, v_pre_gain, v_post_gain, v_conv_w, v_conv_b, v_w_rgate, v_b_rgate, v_w_igate, v_b_igate, v_lru_lambda, v_attn_out_gain, v_lru_out_gain, v_w_out, v_w_ple, v_ple_gain, v_w_ple_gate, v_b_ple_gate):
    given = dict(x=x, p=p, w_in=w_in, b_f=b_f, pre_gain=pre_gain, post_gain=post_gain, conv_w=conv_w, conv_b=conv_b, w_rgate=w_rgate, b_rgate=b_rgate, w_igate=w_igate, b_igate=b_igate, lru_lambda=lru_lambda, attn_out_gain=attn_out_gain, lru_out_gain=lru_out_gain, w_out=w_out, w_ple=w_ple, ple_gain=ple_gain, w_ple_gate=w_ple_gate, b_ple_gate=b_ple_gate, loss_target=loss_target, m_w_in=m_w_in, m_b_f=m_b_f, m_pre_gain=m_pre_gain, m_post_gain=m_post_gain, m_conv_w=m_conv_w, m_conv_b=m_conv_b, m_w_rgate=m_w_rgate, m_b_rgate=m_b_rgate, m_w_igate=m_w_igate, m_b_igate=m_b_igate, m_lru_lambda=m_lru_lambda, m_attn_out_gain=m_attn_out_gain, m_lru_out_gain=m_lru_out_gain, m_w_out=m_w_out, m_w_ple=m_w_ple, m_ple_gain=m_ple_gain, m_w_ple_gate=m_w_ple_gate, m_b_ple_gate=m_b_ple_gate, v_w_in=v_w_in, v_b_f=v_b_f, v_pre_gain=v_pre_gain, v_post_gain=v_post_gain, v_conv_w=v_conv_w, v_conv_b=v_conv_b, v_w_rgate=v_w_rgate, v_b_rgate=v_b_rgate, v_w_igate=v_w_igate, v_b_igate=v_b_igate, v_lru_lambda=v_lru_lambda, v_attn_out_gain=v_attn_out_gain, v_lru_out_gain=v_lru_out_gain, v_w_out=v_w_out, v_w_ple=v_w_ple, v_ple_gain=v_ple_gain, v_w_ple_gate=v_w_ple_gate, v_b_ple_gate=v_b_ple_gate)
    weights = {n: given[n] for n in TWIN_WEIGHTS}
    shared = {n: given[n] for n in SHARED_INPUTS}
    per_example = {n: given[n] for n in ['x', 'p']}
    grad_fn = _jax.value_and_grad(_loss, argnums=(0, 1))

    def one_microbatch(ex, loss_target):
        ex = dict(ex)
        diff = ex.pop(TWIN_DIFF_INPUT)
        return grad_fn(weights, diff, {**shared, **ex}, loss_target)

    if N_MICROBATCH == 1:
        loss, (grad_w, grad_x) = one_microbatch(per_example, given["loss_target"])
    else:
        def body(carry, xs):
            loss_sum, grad_sum = carry
            l_k, (gw_k, gx_k) = one_microbatch(xs[0], xs[1])
            with _jax.named_scope("update"):
                return (loss_sum + l_k, _jax.tree.map(_jnp.add, grad_sum, gw_k)), gx_k

        init = (_jnp.zeros((), _jnp.float32), _jax.tree.map(_jnp.zeros_like, weights))
        (loss, grad_w), grad_x = _jax.lax.scan(body, init, (per_example, given["loss_target"]))
    with _jax.named_scope("update"):
        delta_w, new_m, new_v = {}, {}, {}
        for n in TWIN_WEIGHTS:
            delta_w[n], new_m[n], new_v[n] = _adamw(weights[n], grad_w[n], given["m_" + n], given["v_" + n])
    return (loss, grad_x, *[grad_w[n] for n in TWIN_WEIGHTS], *[delta_w[n] for n in TWIN_WEIGHTS],
            *[new_m[n] for n in TWIN_WEIGHTS], *[new_v[n] for n in TWIN_WEIGHTS])
```

```python
import functools

import jax
import jax.numpy as jnp
from jax import lax
from jax.experimental import pallas as pl
from jax.experimental.pallas import tpu as pltpu

F32 = jnp.float32
BF16 = jnp.bfloat16

N_DEV = 8
D = 1024
HD = 128
NH = 8
D_IN = 6152
D_IN_SHARD = D_IN // N_DEV
D_QKV = 3 * D
D_REST = 3 * D + HD
FL_COL = 3 * D
D_PLE = 256
CONV_W = 4
LRU_C = 8.0
RMS_EPS = 1e-6
SCALE = HD ** -0.5
NEG = -1e30

ADAM_LR = 0.001
ADAM_B1 = 0.9
ADAM_B2 = 0.999
ADAM_EPS = 1e-08
ADAM_WD = 0.01
ADAM_STEP = 10

TS = 256
TQ = 256
VMEM_LIMIT = 48 * 1024 * 1024

NT_DIMS = (((1,), (1,)), ((), ()))
TN_DIMS = (((0,), (0,)), ((), ()))


def _params(**kw):
    return pltpu.CompilerParams(vmem_limit_bytes=VMEM_LIMIT, **kw)


def _sigmoid(v):
    return 1.0 / (1.0 + jnp.exp(-v))


def _rms_fwd(v, gain):
    rstd = lax.rsqrt(jnp.mean(v * v, axis=-1, keepdims=True) + RMS_EPS)
    return v * rstd, rstd


def _rms_bwd(vhat, rstd, dvhat):
    return rstd * (dvhat - vhat * jnp.mean(dvhat * vhat, axis=-1, keepdims=True))


def _colsum(v):
    return jnp.sum(v, axis=0, keepdims=True)


def _rows_iota(t):
    return lax.broadcasted_iota(jnp.int32, (t, 1), 0)


def _scan_fwd(a, u):
    t = a.shape[0]
    rows = _rows_iota(t)
    d = 1
    while d < t:
        valid = rows >= d
        u = jnp.where(valid, u + a * pltpu.roll(u, d, 0), u)
        a = jnp.where(valid, a * pltpu.roll(a, d, 0), a)
        d *= 2
    return a, u


def _scan_bwd(b, g):
    t = b.shape[0]
    rows = _rows_iota(t)
    d = 1
    while d < t:
        valid = rows < t - d
        g = jnp.where(valid, g + b * pltpu.roll(g, t - d, 0), g)
        b = jnp.where(valid, b * pltpu.roll(b, t - d, 0), b)
        d *= 2
    return g


def _cumsum_fwd(v):
    t = v.shape[0]
    rows = _rows_iota(t)
    d = 1
    while d < t:
        v = jnp.where(rows >= d, v + pltpu.roll(v, d, 0), v)
        d *= 2
    return v


def _cumsum_bwd(v):
    t = v.shape[0]
    rows = _rows_iota(t)
    d = 1
    while d < t:
        v = jnp.where(rows < t - d, v + pltpu.roll(v, t - d, 0), v)
        d *= 2
    return v


def _shift_down(ext, k, t):
    return pltpu.roll(ext, k, 0)[8:, :] if k else ext[8:, :]


def _shift_up(ext, k, t):
    return pltpu.roll(ext, t + 8 - k, 0)[:t, :] if k else ext[:t, :]


def _exchange(name, arrs, kinds):
    n = len(arrs)
    out_shape = []
    for a, kind in zip(arrs, kinds):
        shp = a.shape if kind == "scatter" else (N_DEV,) + a.shape
        out_shape.append(jax.ShapeDtypeStruct(shp, a.dtype))

    def body(*refs):
        ins, outs = refs[:n], refs[n:2 * n]
        send_sems, recv_sems, local_sems = refs[2 * n:]
        x, y, c = lax.axis_index("x"), lax.axis_index("y"), lax.axis_index("c")
        me = 4 * x + 2 * y + c
        copies = []
        for i in range(n):
            scatter = kinds[i] == "scatter"
            mine = pltpu.make_async_copy(ins[i].at[me] if scatter else ins[i], outs[i].at[me], local_sems.at[i])
            mine.start()
            copies.append(mine)
            for m in range(1, N_DEV):
                px = 1 - x if m & 4 else x
                py = 1 - y if m & 2 else y
                pc = 1 - c if m & 1 else c
                peer = 4 * px + 2 * py + pc
                cp = pltpu.make_async_remote_copy(
                    src_ref=ins[i].at[peer] if scatter else ins[i],
                    dst_ref=outs[i].at[me],
                    send_sem=send_sems.at[i, m - 1],
                    recv_sem=recv_sems.at[i, m - 1],
                    device_id=(px, py, pc),
                    device_id_type=pl.DeviceIdType.MESH,
                )
                cp.start()
                copies.append(cp)
        for cp in copies:
            cp.wait()

    any_spec = pl.BlockSpec(memory_space=pl.ANY)
    return pl.pallas_call(
        body,
        name=name,
        out_shape=out_shape,
        in_specs=[any_spec] * n,
        out_specs=[any_spec] * n,
        scratch_shapes=[
            pltpu.SemaphoreType.DMA((n, N_DEV - 1)),
            pltpu.SemaphoreType.DMA((n, N_DEV - 1)),
            pltpu.SemaphoreType.DMA((n,)),
        ],
        compiler_params=pltpu.CompilerParams(has_side_effects=True),
    )(*arrs)


def _pick(n, cands):
    for t in cands:
        if n % t == 0:
            return t
    raise ValueError(f"no tile for {n}")


def _mm(name, a, b, mode, out_dtype):
    if mode == "nn":
        (m, k), (k2, n) = a.shape, b.shape
    elif mode == "nt":
        (m, k), (n, k2) = a.shape, b.shape
    else:
        (k, m), (k2, n) = a.shape, b.shape
    assert k == k2, (name, a.shape, b.shape)
    tm = _pick(m, (1024, 512, 256) if mode == "tn" else (512, 256))
    tn = _pick(n, (1024, 640, 512, 256, 128))
    tk = _pick(k, (512, 256) if mode == "tn" else (1024, 640, 512, 256))
    nk = k // tk

    def body(a_ref, b_ref, o_ref, acc_ref):
        kk = pl.program_id(2)
        av = a_ref[...].astype(BF16)
        bv = b_ref[...].astype(BF16)
        if mode == "nn":
            part = jnp.dot(av, bv, preferred_element_type=F32)
        elif mode == "nt":
            part = lax.dot_general(av, bv, NT_DIMS, preferred_element_type=F32)
        else:
            part = lax.dot_general(av, bv, TN_DIMS, preferred_element_type=F32)

        @pl.when(kk == 0)
        def _():
            acc_ref[...] = part

        @pl.when(kk > 0)
        def _():
            acc_ref[...] += part

        @pl.when(kk == nk - 1)
        def _():
            o_ref[...] = acc_ref[...].astype(out_dtype)

    if mode == "tn":
        a_spec = pl.BlockSpec((tk, tm), lambda i, j, kk: (kk, i))
    else:
        a_spec = pl.BlockSpec((tm, tk), lambda i, j, kk: (i, kk))
    if mode == "nt":
        b_spec = pl.BlockSpec((tn, tk), lambda i, j, kk: (j, kk))
    else:
        b_spec = pl.BlockSpec((tk, tn), lambda i, j, kk: (kk, j))
    return pl.pallas_call(
        body,
        name=name,
        grid=(m // tm, n // tn, nk),
        in_specs=[a_spec, b_spec],
        out_specs=pl.BlockSpec((tm, tn), lambda i, j, kk: (i, j)),
        out_shape=jax.ShapeDtypeStruct((m, n), out_dtype),
        scratch_shapes=[pltpu.VMEM((tm, tn), F32)],
        compiler_params=_params(dimension_semantics=("parallel", "parallel", "arbitrary")),
    )(a, b)


def _row(c, col=0):
    return pl.BlockSpec((TS, c), lambda i: (i, col))


def _vec(r, c):
    return pl.BlockSpec((r, c), lambda i: (0, 0))


def _prenorm(x, pre_gain):
    s = x.shape[0]

    def body(x_ref, g_ref, o_ref):
        xhat, _ = _rms_fwd(x_ref[...], g_ref[...])
        o_ref[...] = (xhat * g_ref[...]).astype(BF16)

    return pl.pallas_call(
        body, name="prenorm", grid=(s // TS,),
        in_specs=[_row(D), _vec(1, D)], out_specs=_row(D),
        out_shape=jax.ShapeDtypeStruct((s, D), BF16),
        compiler_params=_params(dimension_semantics=("parallel",)),
    )(x, pre_gain)


def _forget_fwd(zr, bf_pad):
    s = zr.shape[0]
    n = s // TS

    def body(fl_ref, b_ref, ccol_ref, crow_ref, c_buf, carry):
        i = pl.program_id(0)

        @pl.when(i == 0)
        def _():
            carry[...] = jnp.zeros_like(carry)

        fl = fl_ref[...] + b_ref[...]
        ls = jnp.minimum(fl, 0.0) - jnp.log(1.0 + jnp.exp(-jnp.abs(fl)))
        c_buf[...] = _cumsum_fwd(ls) + carry[0:1, :]
        carry[0:1, :] = c_buf[TS - 1:TS, :]
        cv = c_buf[...]
        for h in range(NH):
            ccol_ref[h] = jnp.broadcast_to(cv[:, 8 * h:8 * h + 1], (TS, HD))
        crow_ref[0] = cv.T

    return pl.pallas_call(
        body, name="forget_fwd", grid=(n,),
        in_specs=[_row(HD, FL_COL // HD), _vec(1, HD)],
        out_specs=[pl.BlockSpec((NH, TS, HD), lambda i: (0, i, 0)), pl.BlockSpec((1, HD, TS), lambda i: (i, 0, 0))],
        out_shape=[jax.ShapeDtypeStruct((NH, s, HD), F32), jax.ShapeDtypeStruct((n, HD, TS), F32)],
        scratch_shapes=[pltpu.VMEM((TS, HD), F32), pltpu.VMEM((8, HD), F32)],
        compiler_params=_params(dimension_semantics=("arbitrary",)),
    )(zr, bf_pad)


def _attn_fwd(zq, ccol, crow):
    s = zq.shape[0]
    n = s // TQ

    def body(q_ref, k_ref, v_ref, cc_ref, cr_ref, o_ref, aq_ref):
        i = pl.program_id(1)
        q = q_ref[...]
        cq = cc_ref[0][:, 0:1]
        tri = lax.broadcasted_iota(jnp.int32, (TQ, TQ), 0) >= lax.broadcasted_iota(jnp.int32, (TQ, TQ), 1)

        def step(j, carry, masked):
            m, l, acc = carry
            rows = pl.ds(pl.multiple_of(j * TQ, TQ), TQ)
            sc = lax.dot_general(q, k_ref[rows, :], NT_DIMS, preferred_element_type=F32) * SCALE
            sc = sc + (cq - cr_ref[j][0:1, :])
            if masked:
                sc = jnp.where(tri, sc, NEG)
            m_new = jnp.maximum(m, jnp.max(sc, axis=1, keepdims=True))
            alpha = jnp.exp(m - m_new)
            pr = jnp.exp(sc - m_new)
            l = alpha * l + jnp.sum(pr, axis=1, keepdims=True)
            acc = alpha * acc + jnp.dot(pr.astype(BF16), v_ref[rows, :], preferred_element_type=F32)
            return m_new, l, acc

        init = (jnp.full((TQ, 1), NEG, F32), jnp.zeros((TQ, 1), F32), jnp.zeros((TQ, HD), F32))
        carry = lax.fori_loop(0, i, lambda j, cr: step(j, cr, False), init)
        m, l, acc = step(i, carry, True)
        o_ref[...] = acc / l
        aq_ref[0] = jnp.broadcast_to(cq - (m + jnp.log(l)), (TQ, HD))

    return pl.pallas_call(
        body, name="attn_fwd", grid=(NH, n),
        in_specs=[
            pl.BlockSpec((TQ, HD), lambda h, i: (i, h)),
            pl.BlockSpec((s, HD), lambda h, i: (0, NH + h)),
            pl.BlockSpec((s, HD), lambda h, i: (0, 2 * NH + h)),
            pl.BlockSpec((1, TQ, HD), lambda h, i: (h, i, 0)),
            pl.BlockSpec((n, 8, TQ), lambda h, i: (0, h, 0)),
        ],
        out_specs=[pl.BlockSpec((TQ, HD), lambda h, i: (i, h)), pl.BlockSpec((1, TQ, HD), lambda h, i: (h, i, 0))],
        out_shape=[jax.ShapeDtypeStruct((s, D), F32), jax.ShapeDtypeStruct((NH, s, HD), F32)],
        compiler_params=_params(dimension_semantics=("parallel", "parallel")),
    )(zq, zq, zq, ccol, crow)


def _attn_bwd(zq, do, aq, delta, crow):
    s = zq.shape[0]
    n = s // TQ

    def body(k_ref, v_ref, q_ref, do_ref, aq_ref, dl_ref, cr_ref, dq_ref, dk_ref, dv_ref, dcs_ref, drs_ref):
        j = pl.program_id(1)

        @pl.when(j == 0)
        def _():
            dq_ref[...] = jnp.zeros_like(dq_ref)
            drs_ref[...] = jnp.zeros_like(drs_ref)

        k = k_ref[...]
        v = v_ref[...]
        ck = cr_ref[0][0:1, :]
        tri = lax.broadcasted_iota(jnp.int32, (TQ, TQ), 0) >= lax.broadcasted_iota(jnp.int32, (TQ, TQ), 1)

        def step(i, carry, masked):
            dk, dv, dcs = carry
            rows = pl.ds(pl.multiple_of(i * TQ, TQ), TQ)
            q = q_ref[rows, :]
            dout = do_ref[rows, :]
            aqv = aq_ref[0, rows, :][:, 0:1]
            dlv = dl_ref[0, rows, :][:, 0:1]
            sc = lax.dot_general(q, k, NT_DIMS, preferred_element_type=F32) * SCALE + (aqv - ck)
            if masked:
                sc = jnp.where(tri, sc, NEG)
            pr = jnp.exp(sc)
            dv = dv + lax.dot_general(pr.astype(BF16), dout, TN_DIMS, preferred_element_type=F32)
            dp = lax.dot_general(dout, v, NT_DIMS, preferred_element_type=F32)
            ds = pr * (dp - dlv)
            dcs = dcs + _colsum(ds)
            drs_ref[0, rows, :] += jnp.broadcast_to(jnp.sum(ds, axis=1, keepdims=True), (TQ, HD))
            dsb = ds.astype(BF16)
            dk = dk + lax.dot_general(dsb, q, TN_DIMS, preferred_element_type=F32)
            dq_ref[rows, :] += jnp.dot(dsb, k, preferred_element_type=F32) * SCALE
            return dk, dv, dcs

        init = (jnp.zeros((TQ, HD), F32), jnp.zeros((TQ, HD), F32), jnp.zeros((1, TQ), F32))
        carry = step(j, init, True)
        dk, dv, dcs = lax.fori_loop(j + 1, n, lambda i, cr: step(i, cr, False), carry)
        dk_ref[...] = (dk * SCALE).astype(BF16)
        dv_ref[...] = dv.astype(BF16)
        dcs_ref[0] = jnp.broadcast_to(dcs, (8, TQ))

    return pl.pallas_call(
        body, name="attn_bwd", grid=(NH, n),
        in_specs=[
            pl.BlockSpec((TQ, HD), lambda h, j: (j, NH + h)),
            pl.BlockSpec((TQ, HD), lambda h, j: (j, 2 * NH + h)),
            pl.BlockSpec((s, HD), lambda h, j: (0, h)),
            pl.BlockSpec((s, HD), lambda h, j: (0, h)),
            pl.BlockSpec((1, s, HD), lambda h, j: (h, 0, 0)),
            pl.BlockSpec((1, s, HD), lambda h, j: (h, 0, 0)),
            pl.BlockSpec((1, 8, TQ), lambda h, j: (j, h, 0)),
        ],
        out_specs=[
            pl.BlockSpec((s, HD), lambda h, j: (0, h)),
            pl.BlockSpec((TQ, HD), lambda h, j: (j, h)),
            pl.BlockSpec((TQ, HD), lambda h, j: (j, h)),
            pl.BlockSpec((1, 8, TQ), lambda h, j: (j, h, 0)),
            pl.BlockSpec((1, s, HD), lambda h, j: (h, 0, 0)),
        ],
        out_shape=[
            jax.ShapeDtypeStruct((s, D), F32),
            jax.ShapeDtypeStruct((s, D), BF16),
            jax.ShapeDtypeStruct((s, D), BF16),
            jax.ShapeDtypeStruct((n, 8 * NH, TQ), F32),
            jax.ShapeDtypeStruct((NH, s, HD), F32),
        ],
        compiler_params=_params(dimension_semantics=("parallel", "arbitrary")),
    )(zq, zq, zq, do, aq, delta, crow)


def _forget_bwd(dcs, drs, zr, bf_pad):
    n = dcs.shape[0]
    s = n * TS

    def body(dcs_ref, drs_ref, fl_ref, b_ref, dfl_ref, gb_ref, buf, carry):
        i = pl.program_id(0)

        @pl.when(i == 0)
        def _():
            carry[...] = jnp.zeros_like(carry)
            gb_ref[...] = jnp.zeros_like(gb_ref)

        dc_t = jnp.concatenate([dcs_ref[0], jnp.zeros((HD - 8 * NH, TS), F32)], axis=0)
        lane = lax.broadcasted_iota(jnp.int32, (TS, HD), 1)
        dc = -dc_t.T
        for hh in range(NH):
            dc = dc + jnp.where(lane == 8 * hh, drs_ref[hh], 0.0)
        buf[...] = _cumsum_bwd(dc) + carry[0:1, :]
        carry[0:1, :] = buf[0:1, :]
        fl = fl_ref[...] + b_ref[...]
        dfl = buf[...] * _sigmoid(-fl)
        dfl_ref[...] = dfl.astype(BF16)
        gb_ref[...] += _colsum(dfl)

    return pl.pallas_call(
        body, name="forget_bwd", grid=(n,),
        in_specs=[
            pl.BlockSpec((1, 8 * NH, TS), lambda i: (n - 1 - i, 0, 0)),
            pl.BlockSpec((NH, TS, HD), lambda i: (0, n - 1 - i, 0)),
            pl.BlockSpec((TS, HD), lambda i: (n - 1 - i, FL_COL // HD)),
            _vec(1, HD),
        ],
        out_specs=[pl.BlockSpec((TS, HD), lambda i: (n - 1 - i, 0)), _vec(1, HD)],
        out_shape=[jax.ShapeDtypeStruct((s, HD), BF16), jax.ShapeDtypeStruct((1, HD), F32)],
        scratch_shapes=[pltpu.VMEM((TS, HD), F32), pltpu.VMEM((8, HD), F32)],
        compiler_params=_params(dimension_semantics=("arbitrary",)),
    )(dcs, drs, zr, bf_pad)


def _gates(xc, w_ref, b):
    xb = xc.astype(BF16)
    pre = jnp.concatenate(
        [jnp.dot(xb[:, HD * g:HD * (g + 1)], w_ref[g], preferred_element_type=F32) for g in range(NH)], axis=1)
    return _sigmoid(pre + b)


def _lru_coeffs(r, lam):
    sp = jnp.maximum(-lam, 0.0) + jnp.log(1.0 + jnp.exp(-jnp.abs(lam)))
    log_a = -LRU_C * r * sp
    a = jnp.exp(log_a)
    y = 2.0 * log_a
    em1 = jnp.where(jnp.abs(y) < 0.01, y * (1.0 + y * (0.5 + y * (1.0 / 6.0))), jnp.exp(y) - 1.0)
    return sp, a, jnp.sqrt(-em1)


def _conv_taps(ext, t):
    return [_shift_down(ext, CONV_W - 1 - jj, t) for jj in range(CONV_W)]


def _lru_fwd(zr, conv_w8, conv_b, w_r, b_r, w_i, b_i, lam):
    s = zr.shape[0]
    n = s // TS
    xl_col = 1

    def body(xl_ref, halo_ref, cw_ref, cb_ref, wr_ref, br_ref, wi_ref, bi_ref, lam_ref, xc_ref, h_ref, carry):
        i = pl.program_id(0)

        @pl.when(i == 0)
        def _():
            carry[...] = jnp.zeros_like(carry)

        halo = jnp.where(i == 0, 0.0, halo_ref[...])
        taps = _conv_taps(jnp.concatenate([halo, xl_ref[...]], axis=0), TS)
        xc = cb_ref[...] + sum(cw_ref[jj:jj + 1, :] * taps[jj] for jj in range(CONV_W))
        xc_ref[...] = xc
        r = _gates(xc, wr_ref, br_ref[...])
        ig = _gates(xc, wi_ref, bi_ref[...])
        _, a, gam = _lru_coeffs(r, lam_ref[...])
        a_cum, h_loc = _scan_fwd(a, gam * (ig * xc))
        h_ref[...] = h_loc + a_cum * carry[0:1, :]
        carry[0:1, :] = h_ref[TS - 1:TS, :]

    return pl.pallas_call(
        body, name="lru_fwd", grid=(n,),
        in_specs=[
            _row(D, xl_col),
            pl.BlockSpec((8, D), lambda i: (jnp.maximum(i * (TS // 8) - 1, 0), xl_col)),
            _vec(8, D), _vec(1, D),
            pl.BlockSpec((NH, HD, HD), lambda i: (0, 0, 0)), _vec(1, D),
            pl.BlockSpec((NH, HD, HD), lambda i: (0, 0, 0)), _vec(1, D),
            _vec(1, D),
        ],
        out_specs=[_row(D), _row(D)],
        out_shape=[jax.ShapeDtypeStruct((s, D), F32), jax.ShapeDtypeStruct((s, D), F32)],
        scratch_shapes=[pltpu.VMEM((8, D), F32)],
        compiler_params=_params(dimension_semantics=("arbitrary",)),
    )(zr, zr, conv_w8, conv_b, w_r, b_r, w_i, b_i, lam)


def _lru_bwd(zr, xc, h, dh, conv_w8, w_r, b_r, w_i, b_i, lam):
    s = zr.shape[0]
    n = s // TS
    xl_col = 1

    def rev(i):
        return n - 1 - i

    def body(xl_ref, xlh_ref, xc_ref, h_ref, hh_ref, dh_ref, cw_ref, wr_ref, br_ref, wi_ref, bi_ref, lam_ref,
             dxl_ref, gwr_ref, gwi_ref, gbr_ref, gbi_ref, glam_ref, gcb_ref, gcw_ref, l_buf, dxc_buf, carry_g, carry_dxc):
        i = pl.program_id(0)
        first = rev(i) == 0

        @pl.when(i == 0)
        def _():
            carry_g[...] = jnp.zeros_like(carry_g)
            carry_dxc[...] = jnp.zeros_like(carry_dxc)
            for ref in (gwr_ref, gwi_ref, gbr_ref, gbi_ref, glam_ref, gcb_ref, gcw_ref):
                ref[...] = jnp.zeros_like(ref)

        rows = _rows_iota(TS)
        xc = xc_ref[...]
        lam = lam_ref[...]
        r = _gates(xc, wr_ref, br_ref[...])
        ig = _gates(xc, wi_ref, bi_ref[...])
        sp, a, gam = _lru_coeffs(r, lam)
        g = dh_ref[...] + jnp.where(rows == TS - 1, carry_g[0:1, :], 0.0)
        b = jnp.where(rows == TS - 1, 0.0, pltpu.roll(a, TS - 1, 0))
        l_buf[...] = _scan_bwd(b, g)
        lv = l_buf[...]
        carry_g[0:1, :] = l_buf[0:1, :] * a[0:1, :]
        h_prev_row = jnp.where(first, 0.0, hh_ref[7:8, :])
        h_prev = jnp.where(rows == 0, h_prev_row, pltpu.roll(h_ref[...], 1, 0))
        dgam = lv * ig * xc
        dig = lv * gam * xc
        dxc = lv * gam * ig
        dla = lv * h_prev * a - dgam * (a * a) / gam
        dr = dla * (-LRU_C) * sp
        glam_ref[...] += _colsum(dla * r) * (LRU_C * _sigmoid(-lam))
        dpr = dr * r * (1.0 - r)
        dpi = dig * ig * (1.0 - ig)
        gbr_ref[...] += _colsum(dpr)
        gbi_ref[...] += _colsum(dpi)
        xb = xc.astype(BF16)
        dprb = dpr.astype(BF16)
        dpib = dpi.astype(BF16)
        back = []
        for gi in range(NH):
            cs = slice(HD * gi, HD * (gi + 1))
            gwr_ref[gi] += lax.dot_general(xb[:, cs], dprb[:, cs], TN_DIMS, preferred_element_type=F32)
            gwi_ref[gi] += lax.dot_general(xb[:, cs], dpib[:, cs], TN_DIMS, preferred_element_type=F32)
            back.append(lax.dot_general(dprb[:, cs], wr_ref[gi], NT_DIMS, preferred_element_type=F32)
                        + lax.dot_general(dpib[:, cs], wi_ref[gi], NT_DIMS, preferred_element_type=F32))
        dxc = dxc + jnp.concatenate(back, axis=1)
        dxc_buf[...] = dxc
        gcb_ref[...] += _colsum(dxc)
        halo = jnp.where(first, 0.0, xlh_ref[...])
        taps = _conv_taps(jnp.concatenate([halo, xl_ref[...]], axis=0), TS)
        for jj in range(CONV_W):
            gcw_ref[jj:jj + 1, :] += _colsum(dxc * taps[jj])
        ext = jnp.concatenate([dxc, carry_dxc[...]], axis=0)
        dxl = sum(cw_ref[jj:jj + 1, :] * _shift_up(ext, CONV_W - 1 - jj, TS) for jj in range(CONV_W))
        dxl_ref[...] = dxl.astype(BF16)
        carry_dxc[...] = dxc_buf[0:8, :]

    rowr = lambda c, col=0: pl.BlockSpec((TS, c), lambda i: (rev(i), col))
    halo = lambda col: pl.BlockSpec((8, D), lambda i: (jnp.maximum(rev(i) * (TS // 8) - 1, 0), col))
    gate_w = pl.BlockSpec((NH, HD, HD), lambda i: (0, 0, 0))
    return pl.pallas_call(
        body, name="lru_bwd", grid=(n,),
        in_specs=[rowr(D, xl_col), halo(xl_col), rowr(D), rowr(D), halo(0), rowr(D),
                  _vec(8, D), gate_w, _vec(1, D), gate_w, _vec(1, D), _vec(1, D)],
        out_specs=[rowr(D), gate_w, gate_w, _vec(1, D), _vec(1, D), _vec(1, D), _vec(1, D), _vec(8, D)],
        out_shape=[
            jax.ShapeDtypeStruct((s, D), BF16),
            jax.ShapeDtypeStruct((NH, HD, HD), F32), jax.ShapeDtypeStruct((NH, HD, HD), F32),
            jax.ShapeDtypeStruct((1, D), F32), jax.ShapeDtypeStruct((1, D), F32), jax.ShapeDtypeStruct((1, D), F32),
            jax.ShapeDtypeStruct((1, D), F32), jax.ShapeDtypeStruct((8, D), F32),
        ],
        scratch_shapes=[pltpu.VMEM((TS, D), F32), pltpu.VMEM((TS, D), F32), pltpu.VMEM((8, D), F32), pltpu.VMEM((8, D), F32)],
        compiler_params=_params(dimension_semantics=("arbitrary",)),
    )(zr, zr, xc, h, h, dh, conv_w8, w_r, b_r, w_i, b_i, lam)


def _silu_parts(g):
    sg = _sigmoid(g)
    return g * sg, sg * (1.0 + g * (1.0 - sg))


def _branch_out(o, h, zr, gain_a, gain_l):
    s = o.shape[0]

    def body(o_ref, ga_ref, h_ref, gl_ref, ka_ref, kl_ref, y_ref):
        ohat, _ = _rms_fwd(o_ref[...], None)
        y_ref[:, 0:D] = (ohat * ka_ref[...] * _silu_parts(ga_ref[...])[0]).astype(BF16)
        hhat, _ = _rms_fwd(h_ref[...], None)
        y_ref[:, D:2 * D] = (hhat * kl_ref[...] * _silu_parts(gl_ref[...])[0]).astype(BF16)

    return pl.pallas_call(
        body, name="branch_out", grid=(s // TS,),
        in_specs=[_row(D), _row(D, 0), _row(D), _row(D, 2), _vec(1, D), _vec(1, D)],
        out_specs=_row(2 * D),
        out_shape=jax.ShapeDtypeStruct((s, 2 * D), BF16),
        compiler_params=_params(dimension_semantics=("parallel",)),
    )(o, zr, h, zr, gain_a, gain_l)


def _branch_out_bwd(o, h, zr, dycat, gain_a, gain_l):
    s = o.shape[0]

    def body(o_ref, ga_ref, h_ref, gl_ref, dya_ref, dyl_ref, ka_ref, kl_ref,
             do_ref, dl_ref, dga_ref, dh_ref, dgl_ref, gka_ref, gkl_ref):
        @pl.when(pl.program_id(0) == 0)
        def _():
            gka_ref[...] = jnp.zeros_like(gka_ref)
            gkl_ref[...] = jnp.zeros_like(gkl_ref)

        def one(v, g, dy, gain):
            vhat, rstd = _rms_fwd(v, None)
            sg, dsg = _silu_parts(g)
            dn = dy * sg
            dg = dy * (vhat * gain) * dsg
            return _rms_bwd(vhat, rstd, dn * gain), dg, _colsum(dn * vhat)

        o = o_ref[...]
        dout, dga, gka = one(o, ga_ref[...], dya_ref[...], ka_ref[...])
        do_ref[...] = dout.astype(BF16)
        dga_ref[...] = dga.astype(BF16)
        gka_ref[...] += gka
        prod = dout * o
        for hh in range(NH):
            dl_ref[hh] = jnp.broadcast_to(jnp.sum(prod[:, HD * hh:HD * (hh + 1)], axis=1, keepdims=True), (TS, HD))
        dh, dgl, gkl = one(h_ref[...], gl_ref[...], dyl_ref[...], kl_ref[...])
        dh_ref[...] = dh
        dgl_ref[...] = dgl.astype(BF16)
        gkl_ref[...] += gkl

    return pl.pallas_call(
        body, name="branch_out_bwd", grid=(s // TS,),
        in_specs=[_row(D), _row(D, 0), _row(D), _row(D, 2), _row(D, 0), _row(D, 1), _vec(1, D), _vec(1, D)],
        out_specs=[_row(D), pl.BlockSpec((NH, TS, HD), lambda i: (0, i, 0)), _row(D), _row(D), _row(D), _vec(1, D), _vec(1, D)],
        out_shape=[
            jax.ShapeDtypeStruct((s, D), BF16), jax.ShapeDtypeStruct((NH, s, HD), F32), jax.ShapeDtypeStruct((s, D), BF16),
            jax.ShapeDtypeStruct((s, D), F32), jax.ShapeDtypeStruct((s, D), BF16),
            jax.ShapeDtypeStruct((1, D), F32), jax.ShapeDtypeStruct((1, D), F32),
        ],
        compiler_params=_params(dimension_semantics=("arbitrary",)),
    )(o, zr, h, zr, dycat, dycat, gain_a, gain_l)


def _residual(x, mix, post_gain):
    s = x.shape[0]

    def body(x_ref, m_ref, g_ref, h_ref, hb_ref):
        mhat, _ = _rms_fwd(m_ref[...], None)
        h1 = x_ref[...] + mhat * g_ref[...]
        h_ref[...] = h1
        hb_ref[...] = h1.astype(BF16)

    return pl.pallas_call(
        body, name="residual", grid=(s // TS,),
        in_specs=[_row(D), _row(D), _vec(1, D)], out_specs=[_row(D), _row(D)],
        out_shape=[jax.ShapeDtypeStruct((s, D), F32), jax.ShapeDtypeStruct((s, D), BF16)],
        compiler_params=_params(dimension_semantics=("parallel",)),
    )(x, mix, post_gain)


def _head(h1, pe, gp, tgt, ple_gain, b_gate):
    s = h1.shape[0]

    def body(h_ref, pe_ref, gp_ref, t_ref, kg_ref, b_ref, loss_ref, dy_ref, dgp_ref, dpe_ref, gk_ref, gb_ref):
        @pl.when(pl.program_id(0) == 0)
        def _():
            loss_ref[...] = jnp.zeros_like(loss_ref)
            gk_ref[...] = jnp.zeros_like(gk_ref)
            gb_ref[...] = jnp.zeros_like(gb_ref)

        ehat, rstd = _rms_fwd(pe_ref[...], None)
        e = ehat * kg_ref[...]
        gate = _sigmoid(gp_ref[...] + b_ref[...])
        diff = (h_ref[...] + gate * e) - t_ref[...]
        per_row = jnp.mean(diff * diff, axis=-1, keepdims=True)
        loss_ref[...] += 0.5 * jnp.sum(per_row, axis=0, keepdims=True)
        dy = diff * (1.0 / D)
        dy_ref[...] = dy
        dgp = dy * e * gate * (1.0 - gate)
        dgp_ref[...] = dgp.astype(BF16)
        gb_ref[...] += _colsum(dgp)
        de = dy * gate
        gk_ref[...] += _colsum(de * ehat)
        dpe_ref[...] = _rms_bwd(ehat, rstd, de * kg_ref[...]).astype(BF16)

    return pl.pallas_call(
        body, name="head", grid=(s // TS,),
        in_specs=[_row(D), _row(D), _row(D), _row(D), _vec(1, D), _vec(1, D)],
        out_specs=[_vec(1, 1), _row(D), _row(D), _row(D), _vec(1, D), _vec(1, D)],
        out_shape=[
            jax.ShapeDtypeStruct((1, 1), F32), jax.ShapeDtypeStruct((s, D), F32), jax.ShapeDtypeStruct((s, D), BF16),
            jax.ShapeDtypeStruct((s, D), BF16), jax.ShapeDtypeStruct((1, D), F32), jax.ShapeDtypeStruct((1, D), F32),
        ],
        compiler_params=_params(dimension_semantics=("arbitrary",)),
    )(h1, pe, gp, tgt, ple_gain, b_gate)


def _residual_bwd(dy, t, mix, post_gain):
    s = dy.shape[0]

    def body(dy_ref, t_ref, m_ref, g_ref, dh_ref, dm_ref, gg_ref):
        @pl.when(pl.program_id(0) == 0)
        def _():
            gg_ref[...] = jnp.zeros_like(gg_ref)

        dh1 = dy_ref[...] + t_ref[...]
        dh_ref[...] = dh1
        mhat, rstd = _rms_fwd(m_ref[...], None)
        gg_ref[...] += _colsum(dh1 * mhat)
        dm_ref[...] = _rms_bwd(mhat, rstd, dh1 * g_ref[...]).astype(BF16)

    return pl.pallas_call(
        body, name="residual_bwd", grid=(s // TS,),
        in_specs=[_row(D), _row(D), _row(D), _vec(1, D)], out_specs=[_row(D), _row(D), _vec(1, D)],
        out_shape=[jax.ShapeDtypeStruct((s, D), F32), jax.ShapeDtypeStruct((s, D), BF16), jax.ShapeDtypeStruct((1, D), F32)],
        compiler_params=_params(dimension_semantics=("arbitrary",)),
    )(dy, t, mix, post_gain)


def _prenorm_bwd(x, dxn_a, dxn_b, dh1, pre_gain):
    s = x.shape[0]

    def body(x_ref, da_ref, db_ref, dh_ref, g_ref, dx_ref, gg_ref):
        @pl.when(pl.program_id(0) == 0)
        def _():
            gg_ref[...] = jnp.zeros_like(gg_ref)

        xhat, rstd = _rms_fwd(x_ref[...], None)
        dxn = da_ref[...] + db_ref[...]
        gg_ref[...] += _colsum(dxn * xhat)
        dx_ref[...] = dh_ref[...] + _rms_bwd(xhat, rstd, dxn * g_ref[...])

    return pl.pallas_call(
        body, name="prenorm_bwd", grid=(s // TS,),
        in_specs=[_row(D), _row(D), _row(D), _row(D), _vec(1, D)], out_specs=[_row(D), _vec(1, D)],
        out_shape=[jax.ShapeDtypeStruct((s, D), F32), jax.ShapeDtypeStruct((1, D), F32)],
        compiler_params=_params(dimension_semantics=("arbitrary",)),
    )(x, dxn_a, dxn_b, dh1, pre_gain)


def _adamw(name, parts, w, m, v):
    r, c = w.shape
    tr = _pick(r, (256, 128, 16, 8)) if r % 8 == 0 else r
    blk = pl.BlockSpec((tr, c), lambda i: (i, 0))

    def body(p_ref, w_ref, m_ref, v_ref, g_ref, d_ref, nm_ref, nv_ref):
        g = p_ref[0].astype(F32)
        for j in range(1, N_DEV):
            g = g + p_ref[j].astype(F32)
        g_ref[...] = g
        nm = ADAM_B1 * m_ref[...] + (1.0 - ADAM_B1) * g
        nv = ADAM_B2 * v_ref[...] + (1.0 - ADAM_B2) * (g * g)
        nm_ref[...] = nm
        nv_ref[...] = nv
        m_hat = nm / (1.0 - ADAM_B1 ** ADAM_STEP)
        v_hat = nv / (1.0 - ADAM_B2 ** ADAM_STEP)
        d_ref[...] = -ADAM_LR * (m_hat / (jnp.sqrt(v_hat) + ADAM_EPS) + ADAM_WD * w_ref[...])

    return pl.pallas_call(
        body, name=name, grid=(r // tr,),
        in_specs=[pl.BlockSpec((N_DEV, tr, c), lambda i: (0, i, 0)), blk, blk, blk],
        out_specs=[blk] * 4,
        out_shape=[jax.ShapeDtypeStruct((r, c), F32)] * 4,
        compiler_params=_params(dimension_semantics=("parallel",)),
    )(parts, w, m, v)


def _spread8(v):
    r = v.shape[0]
    return jnp.pad(jnp.pad(v[:, :, None], ((0, 0), (0, 0), (0, 7))).reshape(r, 8 * NH), ((0, 0), (0, HD - 8 * NH)))


def _gather8(v):
    return v[:, :8 * NH].reshape(v.shape[0], NH, 8)[:, :, 0]


def _cols_to_shards(g):
    r, c8 = g.shape
    return g.reshape(r, N_DEV, c8 // N_DEV).transpose(1, 0, 2)


def _shards_to_cols(g):
    n, r, c = g.shape
    return g.transpose(1, 0, 2).reshape(r, n * c)


def kernel(x, p, w_in, b_f, pre_gain, post_gain, conv_w, conv_b, w_rgate, b_rgate, w_igate, b_igate, lru_lambda, attn_out_gain, lru_out_gain, w_out, w_ple, ple_gain, w_ple_gate, b_ple_gate, loss_target, m_w_in, m_b_f, m_pre_gain, m_post_gain, m_conv_w, m_conv_b, m_w_rgate, m_b_rgate, m_w_igate, m_b_igate, m_lru_lambda, m_attn_out_gain, m_lru_out_gain, m_w_out, m_w_ple, m_ple_gain, m_w_ple_gate, m_b_ple_gate, v_w_in, v_b_f, v_pre_gain, v_post_gain, v_conv_w, v_conv_b, v_w_rgate, v_b_rgate, v_w_igate, v_b_igate, v_lru_lambda, v_attn_out_gain, v_lru_out_gain, v_w_out, v_w_ple, v_ple_gain, v_w_ple_gate, v_b_ple_gate):
    me = 4 * lax.axis_index("x") + 2 * lax.axis_index("y") + lax.axis_index("c")
    x2, p2, tgt = x[0], p[0, 0], loss_target[0]

    conv_w_shard8 = jnp.pad(conv_w[0], ((0, 8 - CONV_W), (0, 0)))
    g_win, g_wout, g_wple, g_wpg, g_conv = _exchange(
        "gather_weights",
        [w_in[0].astype(BF16), w_out[0].astype(BF16), w_ple[0].astype(BF16), w_ple_gate[0].astype(BF16), conv_w_shard8],
        ["bcast"] * 5)
    win_full = _shards_to_cols(g_win)
    w_qkv = win_full[:, :D_QKV]
    w_rest = jnp.concatenate([win_full[:, D_QKV + NH:], _spread8(win_full[:, D_QKV:D_QKV + NH])], axis=1)
    wout_full = g_wout.reshape(2 * D, D)
    wple_full = _shards_to_cols(g_wple)
    wpg_full = g_wpg.reshape(D, D)
    conv_w8 = _shards_to_cols(g_conv)
    bf_pad = _spread8(b_f)
    w_r, w_i = w_rgate[0].astype(BF16), w_igate[0].astype(BF16)

    xn = _prenorm(x2, pre_gain)
    zq = _mm("proj_qkv", xn, w_qkv, "nn", BF16)
    zr = _mm("proj_rest", xn, w_rest, "nn", F32)
    ccol, crow = _forget_fwd(zr, bf_pad)
    o, aq = _attn_fwd(zq, ccol, crow)
    xc, h = _lru_fwd(zr, conv_w8, conv_b, w_r, b_rgate, w_i, b_igate, lru_lambda)
    ycat = _branch_out(o, h, zr, attn_out_gain, lru_out_gain)
    mix = _mm("proj_out", ycat, wout_full, "nn", F32)
    h1, h1b = _residual(x2, mix, post_gain)
    pe = _mm("proj_ple", p2, wple_full, "nn", F32)
    gp = _mm("proj_gate", h1b, wpg_full, "nn", F32)
    loss_part, dy, dgp, dpe, g_ple_gain, g_b_gate = _head(h1, pe, gp, tgt, ple_gain, b_ple_gate)
    loss = lax.psum(loss_part[0, 0], ("x", "y", "c"))

    t = _mm("bwd_gate_x", dgp, wpg_full, "nt", F32)
    gw_pg = _mm("bwd_gate_w", h1b, dgp, "tn", BF16)
    gw_ple = _mm("bwd_ple_w", p2, dpe, "tn", BF16)
    dh1, dmix, g_post_gain = _residual_bwd(dy, t, mix, post_gain)
    dycat = _mm("bwd_out_x", dmix, wout_full, "nt", F32)
    gw_out = _mm("bwd_out_w", ycat, dmix, "tn", BF16)
    do, delta, dga, dh, dgl, g_aog, g_log = _branch_out_bwd(o, h, zr, dycat, attn_out_gain, lru_out_gain)
    dq, dk, dv, dcs, drs = _attn_bwd(zq, do, aq, delta, crow)
    dxl, g_wr, g_wi, g_br, g_bi, g_lam, g_cb, g_cw8 = _lru_bwd(
        zr, xc, h, dh, conv_w8, w_r, b_rgate, w_i, b_igate, lru_lambda)
    dfl, g_bf_pad = _forget_bwd(dcs, drs, zr, bf_pad)
    dzq = jnp.concatenate([dq.astype(BF16), dk, dv], axis=1)
    dzr = jnp.concatenate([dga, dxl, dgl, dfl], axis=1)
    dxn_a = _mm("bwd_qkv_x", dzq, w_qkv, "nt", F32)
    dxn_b = _mm("bwd_rest_x", dzr, w_rest, "nt", F32)
    gw_qkv = _mm("bwd_qkv_w", xn, dzq, "tn", BF16)
    gw_rest = _mm("bwd_rest_w", xn, dzr, "tn", BF16)
    grad_x, g_pre_gain = _prenorm_bwd(x2, dxn_a, dxn_b, dh1, pre_gain)

    gw_in = jnp.concatenate([gw_qkv, _gather8(gw_rest[:, FL_COL:]), gw_rest[:, :FL_COL]], axis=1)
    small = jnp.concatenate(
        [jnp.pad(_gather8(g_bf_pad), ((0, 0), (0, D - NH))), g_pre_gain, g_post_gain, g_cb, g_br, g_bi, g_lam, g_aog, g_log,
         g_ple_gain, g_b_gate, g_cw8[:CONV_W], jnp.zeros((1, D), F32)], axis=0)
    gates = jnp.concatenate([g_wr.reshape(D, HD), g_wi.reshape(D, HD)], axis=0).astype(BF16)
    r_win, r_wout, r_wple, r_wpg, r_gates, r_small = _exchange(
        "exchange_grads",
        [_cols_to_shards(gw_in), gw_out.reshape(N_DEV, 2 * D // N_DEV, D), _cols_to_shards(gw_ple),
         gw_pg.reshape(N_DEV, D // N_DEV, D), gates, small],
        ["scatter"] * 4 + ["bcast"] * 2)

    upd = {}
    upd["w_in"] = _adamw("adamw_w_in", r_win, w_in[0], m_w_in[0], v_w_in[0])
    upd["w_out"] = _adamw("adamw_w_out", r_wout, w_out[0], m_w_out[0], v_w_out[0])
    upd["w_ple"] = _adamw("adamw_w_ple", r_wple, w_ple[0], m_w_ple[0], v_w_ple[0])
    upd["w_ple_gate"] = _adamw("adamw_w_ple_gate", r_wpg, w_ple_gate[0], m_w_ple_gate[0], v_w_ple_gate[0])
    gates_of = lambda a, b: jnp.concatenate([a[0].reshape(D, HD), b[0].reshape(D, HD)], axis=0)
    g_gates = _adamw("adamw_gates", r_gates, gates_of(w_rgate, w_igate), gates_of(m_w_rgate, m_w_igate),
                     gates_of(v_w_rgate, v_w_igate))
    upd["w_rgate"] = [a[:D].reshape(1, NH, HD, HD) for a in g_gates]
    upd["w_igate"] = [a[D:].reshape(1, NH, HD, HD) for a in g_gates]
    vec_names = ["b_f", "pre_gain", "post_gain", "conv_b", "b_rgate", "b_igate", "lru_lambda", "attn_out_gain",
                 "lru_out_gain", "ple_gain", "b_ple_gate"]
    vec_w = dict(b_f=(b_f, m_b_f, v_b_f), pre_gain=(pre_gain, m_pre_gain, v_pre_gain),
                 post_gain=(post_gain, m_post_gain, v_post_gain), conv_b=(conv_b, m_conv_b, v_conv_b),
                 b_rgate=(b_rgate, m_b_rgate, v_b_rgate), b_igate=(b_igate, m_b_igate, v_b_igate),
                 lru_lambda=(lru_lambda, m_lru_lambda, v_lru_lambda),
                 attn_out_gain=(attn_out_gain, m_attn_out_gain, v_attn_out_gain),
                 lru_out_gain=(lru_out_gain, m_lru_out_gain, v_lru_out_gain), ple_gain=(ple_gain, m_ple_gain, v_ple_gain),
                 b_ple_gate=(b_ple_gate, m_b_ple_gate, v_b_ple_gate))
    conv_mine = lambda a: lax.dynamic_slice_in_dim(a, me * HD, HD, axis=1)

    def small_rows(k):
        rows = [jnp.pad(vec_w[nm][k], ((0, 0), (0, D - vec_w[nm][k].shape[1]))) for nm in vec_names]
        cw = (conv_w, m_conv_w, v_conv_w)[k][0]
        full = lax.dynamic_update_slice_in_dim(jnp.ones((CONV_W, D), F32), cw, me * HD, axis=1)
        return jnp.concatenate(rows + [full, jnp.ones((1, D), F32)], axis=0)

    g_small = _adamw("adamw_small", r_small, small_rows(0), small_rows(1), small_rows(2))
    for idx, nm in enumerate(vec_names):
        width = vec_w[nm][0].shape[1]
        upd[nm] = [a[idx:idx + 1, :width] for a in g_small]
    base = len(vec_names)
    upd["conv_w"] = [conv_mine(a[base:base + CONV_W])[None] for a in g_small]
    for nm in ("w_in", "w_out", "w_ple", "w_ple_gate"):
        upd[nm] = [a[None] for a in upd[nm]]

    order = ["w_in", "b_f", "pre_gain", "post_gain", "conv_w", "conv_b", "w_rgate", "b_rgate", "w_igate", "b_igate",
             "lru_lambda", "attn_out_gain", "lru_out_gain", "w_out", "w_ple", "ple_gain", "w_ple_gate", "b_ple_gate"]
    outs = [loss, grad_x[None]]
    for k in range(4):
        outs += [upd[nm][k] for nm in order]
    return tuple(outs)
```

```python
import functools

import jax
import jax.numpy as jnp
from jax import lax
from jax.experimental import pallas as pl
from jax.experimental.pallas import tpu as pltpu

F32 = jnp.float32
BF16 = jnp.bfloat16

N_DEV = 8
D = 1024
HD = 128
NH = 8
D_IN = 6152
D_IN_SHARD = D_IN // N_DEV
D_QKV = 3 * D
D_REST = 3 * D + HD
FL_COL = 3 * D
D_PLE = 256
CONV_W = 4
LRU_C = 8.0
RMS_EPS = 1e-6
SCALE = HD ** -0.5
EXP2_SCALE = SCALE * 1.4426950408889634
NEG = -1e30

ADAM_LR = 0.001
ADAM_B1 = 0.9
ADAM_B2 = 0.999
ADAM_EPS = 1e-08
ADAM_WD = 0.01
ADAM_STEP = 10

TS = 256
TQ = 512
VMEM_LIMIT = 48 * 1024 * 1024

NT_DIMS = (((1,), (1,)), ((), ()))
TN_DIMS = (((0,), (0,)), ((), ()))


def _params(**kw):
    return pltpu.CompilerParams(vmem_limit_bytes=VMEM_LIMIT, **kw)


def _sigmoid(v):
    return 1.0 / (1.0 + jnp.exp(-v))


def _rms_fwd(v, gain):
    rstd = lax.rsqrt(jnp.mean(v * v, axis=-1, keepdims=True) + RMS_EPS)
    return v * rstd, rstd


def _rms_bwd(vhat, rstd, dvhat):
    return rstd * (dvhat - vhat * jnp.mean(dvhat * vhat, axis=-1, keepdims=True))


def _colsum(v):
    return jnp.sum(v, axis=0, keepdims=True)


def _rows_iota(t):
    return lax.broadcasted_iota(jnp.int32, (t, 1), 0)


def _scan_fwd(a, u):
    t = a.shape[0]
    rows = _rows_iota(t)
    d = 1
    while d < t:
        valid = rows >= d
        u = jnp.where(valid, u + a * pltpu.roll(u, d, 0), u)
        a = jnp.where(valid, a * pltpu.roll(a, d, 0), a)
        d *= 2
    return a, u


def _scan_bwd(b, g):
    t = b.shape[0]
    rows = _rows_iota(t)
    d = 1
    while d < t:
        valid = rows < t - d
        g = jnp.where(valid, g + b * pltpu.roll(g, t - d, 0), g)
        b = jnp.where(valid, b * pltpu.roll(b, t - d, 0), b)
        d *= 2
    return g


def _cumsum_fwd(v):
    t = v.shape[0]
    rows = _rows_iota(t)
    d = 1
    while d < t:
        v = jnp.where(rows >= d, v + pltpu.roll(v, d, 0), v)
        d *= 2
    return v


def _cumsum_bwd(v):
    t = v.shape[0]
    rows = _rows_iota(t)
    d = 1
    while d < t:
        v = jnp.where(rows < t - d, v + pltpu.roll(v, t - d, 0), v)
        d *= 2
    return v


def _shift_down(ext, k, t):
    return pltpu.roll(ext, k, 0)[8:, :] if k else ext[8:, :]


def _shift_up(ext, k, t):
    return pltpu.roll(ext, t + 8 - k, 0)[:t, :] if k else ext[:t, :]


def _exchange(name, arrs, kinds):
    n = len(arrs)
    out_shape = []
    for a, kind in zip(arrs, kinds):
        shp = a.shape if kind == "scatter" else (N_DEV,) + a.shape
        out_shape.append(jax.ShapeDtypeStruct(shp, a.dtype))

    def body(*refs):
        ins, outs = refs[:n], refs[n:2 * n]
        send_sems, recv_sems, local_sems = refs[2 * n:]
        x, y, c = lax.axis_index("x"), lax.axis_index("y"), lax.axis_index("c")
        me = 4 * x + 2 * y + c
        copies = []
        for i in range(n):
            scatter = kinds[i] == "scatter"
            mine = pltpu.make_async_copy(ins[i].at[me] if scatter else ins[i], outs[i].at[me], local_sems.at[i])
            mine.start()
            copies.append(mine)
            for m in range(1, N_DEV):
                px = 1 - x if m & 4 else x
                py = 1 - y if m & 2 else y
                pc = 1 - c if m & 1 else c
                peer = 4 * px + 2 * py + pc
                cp = pltpu.make_async_remote_copy(
                    src_ref=ins[i].at[peer] if scatter else ins[i],
                    dst_ref=outs[i].at[me],
                    send_sem=send_sems.at[i, m - 1],
                    recv_sem=recv_sems.at[i, m - 1],
                    device_id=(px, py, pc),
                    device_id_type=pl.DeviceIdType.MESH,
                )
                cp.start()
                copies.append(cp)
        for cp in copies:
            cp.wait()

    any_spec = pl.BlockSpec(memory_space=pl.ANY)
    return pl.pallas_call(
        body,
        name=name,
        out_shape=out_shape,
        in_specs=[any_spec] * n,
        out_specs=[any_spec] * n,
        scratch_shapes=[
            pltpu.SemaphoreType.DMA((n, N_DEV - 1)),
            pltpu.SemaphoreType.DMA((n, N_DEV - 1)),
            pltpu.SemaphoreType.DMA((n,)),
        ],
        compiler_params=pltpu.CompilerParams(has_side_effects=True),
    )(*arrs)


def _pick(n, cands):
    for t in cands:
        if n % t == 0:
            return t
    raise ValueError(f"no tile for {n}")


def _mm(name, a, b, mode, out_dtype):
    if mode == "nn":
        (m, k), (k2, n) = a.shape, b.shape
    elif mode == "nt":
        (m, k), (n, k2) = a.shape, b.shape
    else:
        (k, m), (k2, n) = a.shape, b.shape
    assert k == k2, (name, a.shape, b.shape)
    tm = _pick(m, (1024, 512, 256) if mode == "tn" else (512, 256))
    tn = _pick(n, (1024, 640, 512, 256, 128))
    tk = _pick(k, (512, 256) if mode == "tn" else (1024, 640, 512, 256))
    nk = k // tk

    def body(a_ref, b_ref, o_ref, acc_ref):
        kk = pl.program_id(2)
        av = a_ref[...].astype(BF16)
        bv = b_ref[...].astype(BF16)
        if mode == "nn":
            part = jnp.dot(av, bv, preferred_element_type=F32)
        elif mode == "nt":
            part = lax.dot_general(av, bv, NT_DIMS, preferred_element_type=F32)
        else:
            part = lax.dot_general(av, bv, TN_DIMS, preferred_element_type=F32)

        @pl.when(kk == 0)
        def _():
            acc_ref[...] = part

        @pl.when(kk > 0)
        def _():
            acc_ref[...] += part

        @pl.when(kk == nk - 1)
        def _():
            o_ref[...] = acc_ref[...].astype(out_dtype)

    if mode == "tn":
        a_spec = pl.BlockSpec((tk, tm), lambda i, j, kk: (kk, i))
    else:
        a_spec = pl.BlockSpec((tm, tk), lambda i, j, kk: (i, kk))
    if mode == "nt":
        b_spec = pl.BlockSpec((tn, tk), lambda i, j, kk: (j, kk))
    else:
        b_spec = pl.BlockSpec((tk, tn), lambda i, j, kk: (kk, j))
    return pl.pallas_call(
        body,
        name=name,
        grid=(m // tm, n // tn, nk),
        in_specs=[a_spec, b_spec],
        out_specs=pl.BlockSpec((tm, tn), lambda i, j, kk: (i, j)),
        out_shape=jax.ShapeDtypeStruct((m, n), out_dtype),
        scratch_shapes=[pltpu.VMEM((tm, tn), F32)],
        compiler_params=_params(dimension_semantics=("parallel", "parallel", "arbitrary")),
    )(a, b)


def _row(c, col=0):
    return pl.BlockSpec((TS, c), lambda i: (i, col))


def _vec(r, c):
    return pl.BlockSpec((r, c), lambda i: (0, 0))


def _prenorm(x, pre_gain):
    s = x.shape[0]

    def body(x_ref, g_ref, o_ref):
        xhat, _ = _rms_fwd(x_ref[...], g_ref[...])
        o_ref[...] = (xhat * g_ref[...]).astype(BF16)

    return pl.pallas_call(
        body, name="prenorm", grid=(s // TS,),
        in_specs=[_row(D), _vec(1, D)], out_specs=_row(D),
        out_shape=jax.ShapeDtypeStruct((s, D), BF16),
        compiler_params=_params(dimension_semantics=("parallel",)),
    )(x, pre_gain)


def _forget_fwd(zr, bf_pad):
    s = zr.shape[0]
    n = s // TQ

    def body(fl_ref, b_ref, ccol_ref, crow_ref, c_buf, carry):
        i = pl.program_id(0)

        @pl.when(i == 0)
        def _():
            carry[...] = jnp.zeros_like(carry)

        fl = fl_ref[...] + b_ref[...]
        ls = jnp.minimum(fl, 0.0) - jnp.log(1.0 + jnp.exp(-jnp.abs(fl)))
        c_buf[...] = _cumsum_fwd(ls) + carry[0:1, :]
        carry[0:1, :] = c_buf[TQ - 1:TQ, :]
        cv = c_buf[...]
        for h in range(NH):
            ccol_ref[h] = jnp.broadcast_to(cv[:, 8 * h:8 * h + 1], (TQ, HD))
        crow_ref[0] = cv.T

    return pl.pallas_call(
        body, name="forget_fwd", grid=(n,),
        in_specs=[pl.BlockSpec((TQ, HD), lambda i: (i, FL_COL // HD)), _vec(1, HD)],
        out_specs=[pl.BlockSpec((NH, TQ, HD), lambda i: (0, i, 0)), pl.BlockSpec((1, HD, TQ), lambda i: (i, 0, 0))],
        out_shape=[jax.ShapeDtypeStruct((NH, s, HD), F32), jax.ShapeDtypeStruct((n, HD, TQ), F32)],
        scratch_shapes=[pltpu.VMEM((TQ, HD), F32), pltpu.VMEM((8, HD), F32)],
        compiler_params=_params(dimension_semantics=("arbitrary",)),
    )(zr, bf_pad)


def _attn_fwd(zq, ccol, crow):
    s = zq.shape[0]
    n = s // TQ
    nb = TQ // HD

    def body(q_ref, k_ref, v_ref, cc_ref, cr_ref, o_ref, aq_ref):
        i = pl.program_id(1)
        q = q_ref[...]
        cq = cc_ref[0] * (1.0 / SCALE)
        row = lax.broadcasted_iota(jnp.int32, (TQ, HD), 0)
        col = lax.broadcasted_iota(jnp.int32, (TQ, HD), 1)

        def step(j, carry, masked):
            m, l, acc = carry
            rows = pl.ds(pl.multiple_of(j * TQ, TQ), TQ)
            u = lax.dot_general(q, k_ref[rows, :], NT_DIMS, preferred_element_type=F32)
            ck = cr_ref[j][0:1, :] * (1.0 / SCALE)
            us = [u[:, HD * b:HD * (b + 1)] + (cq - ck[:, HD * b:HD * (b + 1)]) for b in range(nb)]
            if masked:
                us = [jnp.where(row >= col + HD * b, us[b], NEG) for b in range(nb)]
            bm = functools.reduce(jnp.maximum, us)
            m_new = jnp.maximum(m, jnp.max(bm, axis=1, keepdims=True))
            alpha = jnp.exp2((m - m_new) * EXP2_SCALE)
            ps = [jnp.exp2((ub - m_new) * EXP2_SCALE) for ub in us]
            l = alpha * l + functools.reduce(jnp.add, ps)
            pr = jnp.concatenate(ps, axis=1).astype(BF16)
            acc = alpha * acc + jnp.dot(pr, v_ref[rows, :], preferred_element_type=F32)
            return m_new, l, acc

        init = (jnp.full((TQ, HD), NEG, F32), jnp.zeros((TQ, HD), F32), jnp.zeros((TQ, HD), F32))
        carry = lax.fori_loop(0, i, lambda j, cr: step(j, cr, False), init)
        m, l, acc = step(i, carry, True)
        l_row = jnp.sum(l, axis=1, keepdims=True)
        o_ref[...] = acc / l_row
        aq_ref[0] = cq - (m + jnp.log(l_row) * (1.0 / SCALE))

    return pl.pallas_call(
        body, name="attn_fwd", grid=(NH, n),
        in_specs=[
            pl.BlockSpec((TQ, HD), lambda h, i: (i, h)),
            pl.BlockSpec((s, HD), lambda h, i: (0, NH + h)),
            pl.BlockSpec((s, HD), lambda h, i: (0, 2 * NH + h)),
            pl.BlockSpec((1, TQ, HD), lambda h, i: (h, i, 0)),
            pl.BlockSpec((n, 8, TQ), lambda h, i: (0, h, 0)),
        ],
        out_specs=[pl.BlockSpec((TQ, HD), lambda h, i: (i, h)), pl.BlockSpec((1, TQ, HD), lambda h, i: (h, i, 0))],
        out_shape=[jax.ShapeDtypeStruct((s, D), F32), jax.ShapeDtypeStruct((NH, s, HD), F32)],
        compiler_params=_params(dimension_semantics=("parallel", "parallel")),
    )(zq, zq, zq, ccol, crow)


def _attn_bwd(zq, do, aq, delta, crow):
    s = zq.shape[0]
    n = s // TQ
    nb = TQ // HD

    def body(k_ref, v_ref, q_ref, do_ref, aq_ref, dl_ref, cr_ref, dq_ref, dk_ref, dv_ref, dcs_ref, drs_ref):
        j = pl.program_id(1)

        @pl.when(j == 0)
        def _():
            dq_ref[...] = jnp.zeros_like(dq_ref)
            drs_ref[...] = jnp.zeros_like(drs_ref)

        k = k_ref[...]
        v = v_ref[...]
        ck = cr_ref[0][0:1, :] * (1.0 / SCALE)
        row = lax.broadcasted_iota(jnp.int32, (TQ, HD), 0)
        col = lax.broadcasted_iota(jnp.int32, (TQ, HD), 1)

        def step(i, carry, masked):
            dk, dv, dcs = carry
            rows = pl.ds(pl.multiple_of(i * TQ, TQ), TQ)
            q = q_ref[rows, :]
            dout = do_ref[rows, :]
            aqv = aq_ref[0, rows, :]
            dlv = dl_ref[0, rows, :]
            u = lax.dot_general(q, k, NT_DIMS, preferred_element_type=F32)
            dp = lax.dot_general(dout, v, NT_DIMS, preferred_element_type=F32)
            prs, dss = [], []
            for b in range(nb):
                cs = slice(HD * b, HD * (b + 1))
                ub = u[:, cs] + (aqv - ck[:, cs])
                if masked:
                    ub = jnp.where(row >= col + HD * b, ub, NEG)
                pb = jnp.exp2(ub * EXP2_SCALE)
                prs.append(pb)
                dss.append(pb * (dp[:, cs] - dlv))
            drs_ref[0, rows, :] += functools.reduce(jnp.add, dss)
            ds = jnp.concatenate(dss, axis=1)
            dcs = dcs + jnp.sum(ds.reshape(TQ // 8, 8, TQ), axis=0)
            dsb = ds.astype(BF16)
            dv = dv + lax.dot_general(jnp.concatenate(prs, axis=1).astype(BF16), dout, TN_DIMS, preferred_element_type=F32)
            dk = dk + lax.dot_general(dsb, q, TN_DIMS, preferred_element_type=F32)
            dq_ref[rows, :] += jnp.dot(dsb, k, preferred_element_type=F32) * SCALE
            return dk, dv, dcs

        init = (jnp.zeros((TQ, HD), F32), jnp.zeros((TQ, HD), F32), jnp.zeros((8, TQ), F32))
        carry = step(j, init, True)
        dk, dv, dcs = lax.fori_loop(j + 1, n, lambda i, cr: step(i, cr, False), carry)
        dk_ref[...] = (dk * SCALE).astype(BF16)
        dv_ref[...] = dv.astype(BF16)
        dcs_ref[0] = jnp.broadcast_to(_colsum(dcs), (8, TQ))

    return pl.pallas_call(
        body, name="attn_bwd", grid=(NH, n),
        in_specs=[
            pl.BlockSpec((TQ, HD), lambda h, j: (j, NH + h)),
            pl.BlockSpec((TQ, HD), lambda h, j: (j, 2 * NH + h)),
            pl.BlockSpec((s, HD), lambda h, j: (0, h)),
            pl.BlockSpec((s, HD), lambda h, j: (0, h)),
            pl.BlockSpec((1, s, HD), lambda h, j: (h, 0, 0)),
            pl.BlockSpec((1, s, HD), lambda h, j: (h, 0, 0)),
            pl.BlockSpec((1, 8, TQ), lambda h, j: (j, h, 0)),
        ],
        out_specs=[
            pl.BlockSpec((s, HD), lambda h, j: (0, h)),
            pl.BlockSpec((TQ, HD), lambda h, j: (j, h)),
            pl.BlockSpec((TQ, HD), lambda h, j: (j, h)),
            pl.BlockSpec((1, 8, TQ), lambda h, j: (j, h, 0)),
            pl.BlockSpec((1, s, HD), lambda h, j: (h, 0, 0)),
        ],
        out_shape=[
            jax.ShapeDtypeStruct((s, D), F32),
            jax.ShapeDtypeStruct((s, D), BF16),
            jax.ShapeDtypeStruct((s, D), BF16),
            jax.ShapeDtypeStruct((n, 8 * NH, TQ), F32),
            jax.ShapeDtypeStruct((NH, s, HD), F32),
        ],
        compiler_params=_params(dimension_semantics=("parallel", "arbitrary")),
    )(zq, zq, zq, do, aq, delta, crow)


def _forget_bwd(dcs, drs, zr, bf_pad):
    n = dcs.shape[0]
    s = n * TQ

    def body(dcs_ref, drs_ref, fl_ref, b_ref, dfl_ref, gb_ref, buf, carry):
        i = pl.program_id(0)

        @pl.when(i == 0)
        def _():
            carry[...] = jnp.zeros_like(carry)
            gb_ref[...] = jnp.zeros_like(gb_ref)

        dc_t = jnp.concatenate([dcs_ref[0], jnp.zeros((HD - 8 * NH, TQ), F32)], axis=0)
        lane = lax.broadcasted_iota(jnp.int32, (TQ, HD), 1)
        dc = -dc_t.T
        for hh in range(NH):
            dc = dc + jnp.where(lane == 8 * hh, jnp.sum(drs_ref[hh], axis=1, keepdims=True), 0.0)
        buf[...] = _cumsum_bwd(dc) + carry[0:1, :]
        carry[0:1, :] = buf[0:1, :]
        fl = fl_ref[...] + b_ref[...]
        dfl = buf[...] * _sigmoid(-fl)
        dfl_ref[...] = dfl.astype(BF16)
        gb_ref[...] += _colsum(dfl)

    return pl.pallas_call(
        body, name="forget_bwd", grid=(n,),
        in_specs=[
            pl.BlockSpec((1, 8 * NH, TQ), lambda i: (n - 1 - i, 0, 0)),
            pl.BlockSpec((NH, TQ, HD), lambda i: (0, n - 1 - i, 0)),
            pl.BlockSpec((TQ, HD), lambda i: (n - 1 - i, FL_COL // HD)),
            _vec(1, HD),
        ],
        out_specs=[pl.BlockSpec((TQ, HD), lambda i: (n - 1 - i, 0)), _vec(1, HD)],
        out_shape=[jax.ShapeDtypeStruct((s, HD), BF16), jax.ShapeDtypeStruct((1, HD), F32)],
        scratch_shapes=[pltpu.VMEM((TQ, HD), F32), pltpu.VMEM((8, HD), F32)],
        compiler_params=_params(dimension_semantics=("arbitrary",)),
    )(dcs, drs, zr, bf_pad)


def _gates(xc, w_ref, b):
    xb = xc.astype(BF16)
    pre = jnp.concatenate(
        [jnp.dot(xb[:, HD * g:HD * (g + 1)], w_ref[g], preferred_element_type=F32) for g in range(NH)], axis=1)
    return _sigmoid(pre + b)


def _lru_coeffs(r, lam):
    sp = jnp.maximum(-lam, 0.0) + jnp.log(1.0 + jnp.exp(-jnp.abs(lam)))
    log_a = -LRU_C * r * sp
    a = jnp.exp(log_a)
    y = 2.0 * log_a
    em1 = jnp.where(jnp.abs(y) < 0.01, y * (1.0 + y * (0.5 + y * (1.0 / 6.0))), jnp.exp(y) - 1.0)
    return sp, a, jnp.sqrt(-em1)


def _conv_taps(ext, t):
    return [_shift_down(ext, CONV_W - 1 - jj, t) for jj in range(CONV_W)]


def _lru_fwd(zr, conv_w8, conv_b, w_r, b_r, w_i, b_i, lam):
    s = zr.shape[0]
    n = s // TS
    xl_col = 1

    def body(xl_ref, halo_ref, cw_ref, cb_ref, wr_ref, br_ref, wi_ref, bi_ref, lam_ref, xc_ref, h_ref, carry):
        i = pl.program_id(0)

        @pl.when(i == 0)
        def _():
            carry[...] = jnp.zeros_like(carry)

        halo = jnp.where(i == 0, 0.0, halo_ref[...])
        taps = _conv_taps(jnp.concatenate([halo, xl_ref[...]], axis=0), TS)
        xc = cb_ref[...] + sum(cw_ref[jj:jj + 1, :] * taps[jj] for jj in range(CONV_W))
        xc_ref[...] = xc
        r = _gates(xc, wr_ref, br_ref[...])
        ig = _gates(xc, wi_ref, bi_ref[...])
        _, a, gam = _lru_coeffs(r, lam_ref[...])
        a_cum, h_loc = _scan_fwd(a, gam * (ig * xc))
        h_ref[...] = h_loc + a_cum * carry[0:1, :]
        carry[0:1, :] = h_ref[TS - 1:TS, :]

    return pl.pallas_call(
        body, name="lru_fwd", grid=(n,),
        in_specs=[
            _row(D, xl_col),
            pl.BlockSpec((8, D), lambda i: (jnp.maximum(i * (TS // 8) - 1, 0), xl_col)),
            _vec(8, D), _vec(1, D),
            pl.BlockSpec((NH, HD, HD), lambda i: (0, 0, 0)), _vec(1, D),
            pl.BlockSpec((NH, HD, HD), lambda i: (0, 0, 0)), _vec(1, D),
            _vec(1, D),
        ],
        out_specs=[_row(D), _row(D)],
        out_shape=[jax.ShapeDtypeStruct((s, D), F32), jax.ShapeDtypeStruct((s, D), F32)],
        scratch_shapes=[pltpu.VMEM((8, D), F32)],
        compiler_params=_params(dimension_semantics=("arbitrary",)),
    )(zr, zr, conv_w8, conv_b, w_r, b_r, w_i, b_i, lam)


def _lru_bwd(zr, xc, h, dh, conv_w8, w_r, b_r, w_i, b_i, lam):
    s = zr.shape[0]
    n = s // TS
    xl_col = 1

    def rev(i):
        return n - 1 - i

    def body(xl_ref, xlh_ref, xc_ref, h_ref, hh_ref, dh_ref, cw_ref, wr_ref, br_ref, wi_ref, bi_ref, lam_ref,
             dxl_ref, gwr_ref, gwi_ref, gbr_ref, gbi_ref, glam_ref, gcb_ref, gcw_ref, l_buf, dxc_buf, carry_g, carry_dxc):
        i = pl.program_id(0)
        first = rev(i) == 0

        @pl.when(i == 0)
        def _():
            carry_g[...] = jnp.zeros_like(carry_g)
            carry_dxc[...] = jnp.zeros_like(carry_dxc)
            for ref in (gwr_ref, gwi_ref, gbr_ref, gbi_ref, glam_ref, gcb_ref, gcw_ref):
                ref[...] = jnp.zeros_like(ref)

        rows = _rows_iota(TS)
        xc = xc_ref[...]
        lam = lam_ref[...]
        r = _gates(xc, wr_ref, br_ref[...])
        ig = _gates(xc, wi_ref, bi_ref[...])
        sp, a, gam = _lru_coeffs(r, lam)
        g = dh_ref[...] + jnp.where(rows == TS - 1, carry_g[0:1, :], 0.0)
        b = jnp.where(rows == TS - 1, 0.0, pltpu.roll(a, TS - 1, 0))
        l_buf[...] = _scan_bwd(b, g)
        lv = l_buf[...]
        carry_g[0:1, :] = l_buf[0:1, :] * a[0:1, :]
        h_prev_row = jnp.where(first, 0.0, hh_ref[7:8, :])
        h_prev = jnp.where(rows == 0, h_prev_row, pltpu.roll(h_ref[...], 1, 0))
        dgam = lv * ig * xc
        dig = lv * gam * xc
        dxc = lv * gam * ig
        dla = lv * h_prev * a - dgam * (a * a) / gam
        dr = dla * (-LRU_C) * sp
        glam_ref[...] += _colsum(dla * r) * (LRU_C * _sigmoid(-lam))
        dpr = dr * r * (1.0 - r)
        dpi = dig * ig * (1.0 - ig)
        gbr_ref[...] += _colsum(dpr)
        gbi_ref[...] += _colsum(dpi)
        xb = xc.astype(BF16)
        dprb = dpr.astype(BF16)
        dpib = dpi.astype(BF16)
        back = []
        for gi in range(NH):
            cs = slice(HD * gi, HD * (gi + 1))
            gwr_ref[gi] += lax.dot_general(xb[:, cs], dprb[:, cs], TN_DIMS, preferred_element_type=F32)
            gwi_ref[gi] += lax.dot_general(xb[:, cs], dpib[:, cs], TN_DIMS, preferred_element_type=F32)
            back.append(lax.dot_general(dprb[:, cs], wr_ref[gi], NT_DIMS, preferred_element_type=F32)
                        + lax.dot_general(dpib[:, cs], wi_ref[gi], NT_DIMS, preferred_element_type=F32))
        dxc = dxc + jnp.concatenate(back, axis=1)
        dxc_buf[...] = dxc
        gcb_ref[...] += _colsum(dxc)
        halo = jnp.where(first, 0.0, xlh_ref[...])
        taps = _conv_taps(jnp.concatenate([halo, xl_ref[...]], axis=0), TS)
        for jj in range(CONV_W):
            gcw_ref[jj:jj + 1, :] += _colsum(dxc * taps[jj])
        ext = jnp.concatenate([dxc, carry_dxc[...]], axis=0)
        dxl = sum(cw_ref[jj:jj + 1, :] * _shift_up(ext, CONV_W - 1 - jj, TS) for jj in range(CONV_W))
        dxl_ref[...] = dxl.astype(BF16)
        carry_dxc[...] = dxc_buf[0:8, :]

    rowr = lambda c, col=0: pl.BlockSpec((TS, c), lambda i: (rev(i), col))
    halo = lambda col: pl.BlockSpec((8, D), lambda i: (jnp.maximum(rev(i) * (TS // 8) - 1, 0), col))
    gate_w = pl.BlockSpec((NH, HD, HD), lambda i: (0, 0, 0))
    return pl.pallas_call(
        body, name="lru_bwd", grid=(n,),
        in_specs=[rowr(D, xl_col), halo(xl_col), rowr(D), rowr(D), halo(0), rowr(D),
                  _vec(8, D), gate_w, _vec(1, D), gate_w, _vec(1, D), _vec(1, D)],
        out_specs=[rowr(D), gate_w, gate_w, _vec(1, D), _vec(1, D), _vec(1, D), _vec(1, D), _vec(8, D)],
        out_shape=[
            jax.ShapeDtypeStruct((s, D), BF16),
            jax.ShapeDtypeStruct((NH, HD, HD), F32), jax.ShapeDtypeStruct((NH, HD, HD), F32),
            jax.ShapeDtypeStruct((1, D), F32), jax.ShapeDtypeStruct((1, D), F32), jax.ShapeDtypeStruct((1, D), F32),
            jax.ShapeDtypeStruct((1, D), F32), jax.ShapeDtypeStruct((8, D), F32),
        ],
        scratch_shapes=[pltpu.VMEM((TS, D), F32), pltpu.VMEM((TS, D), F32), pltpu.VMEM((8, D), F32), pltpu.VMEM((8, D), F32)],
        compiler_params=_params(dimension_semantics=("arbitrary",)),
    )(zr, zr, xc, h, h, dh, conv_w8, w_r, b_r, w_i, b_i, lam)


def _silu_parts(g):
    sg = _sigmoid(g)
    return g * sg, sg * (1.0 + g * (1.0 - sg))


def _branch_out(o, h, zr, gain_a, gain_l):
    s = o.shape[0]

    def body(o_ref, ga_ref, h_ref, gl_ref, ka_ref, kl_ref, y_ref):
        ohat, _ = _rms_fwd(o_ref[...], None)
        y_ref[:, 0:D] = (ohat * ka_ref[...] * _silu_parts(ga_ref[...])[0]).astype(BF16)
        hhat, _ = _rms_fwd(h_ref[...], None)
        y_ref[:, D:2 * D] = (hhat * kl_ref[...] * _silu_parts(gl_ref[...])[0]).astype(BF16)

    return pl.pallas_call(
        body, name="branch_out", grid=(s // TS,),
        in_specs=[_row(D), _row(D, 0), _row(D), _row(D, 2), _vec(1, D), _vec(1, D)],
        out_specs=_row(2 * D),
        out_shape=jax.ShapeDtypeStruct((s, 2 * D), BF16),
        compiler_params=_params(dimension_semantics=("parallel",)),
    )(o, zr, h, zr, gain_a, gain_l)


def _branch_out_bwd(o, h, zr, dycat, gain_a, gain_l):
    s = o.shape[0]

    def body(o_ref, ga_ref, h_ref, gl_ref, dya_ref, dyl_ref, ka_ref, kl_ref,
             do_ref, dl_ref, dga_ref, dh_ref, dgl_ref, gka_ref, gkl_ref):
        @pl.when(pl.program_id(0) == 0)
        def _():
            gka_ref[...] = jnp.zeros_like(gka_ref)
            gkl_ref[...] = jnp.zeros_like(gkl_ref)

        def one(v, g, dy, gain):
            vhat, rstd = _rms_fwd(v, None)
            sg, dsg = _silu_parts(g)
            dn = dy * sg
            dg = dy * (vhat * gain) * dsg
            return _rms_bwd(vhat, rstd, dn * gain), dg, _colsum(dn * vhat)

        o = o_ref[...]
        dout, dga, gka = one(o, ga_ref[...], dya_ref[...], ka_ref[...])
        do_ref[...] = dout.astype(BF16)
        dga_ref[...] = dga.astype(BF16)
        gka_ref[...] += gka
        prod = dout * o
        for hh in range(NH):
            dl_ref[hh] = jnp.broadcast_to(jnp.sum(prod[:, HD * hh:HD * (hh + 1)], axis=1, keepdims=True), (TS, HD))
        dh, dgl, gkl = one(h_ref[...], gl_ref[...], dyl_ref[...], kl_ref[...])
        dh_ref[...] = dh
        dgl_ref[...] = dgl.astype(BF16)
        gkl_ref[...] += gkl

    return pl.pallas_call(
        body, name="branch_out_bwd", grid=(s // TS,),
        in_specs=[_row(D), _row(D, 0), _row(D), _row(D, 2), _row(D, 0), _row(D, 1), _vec(1, D), _vec(1, D)],
        out_specs=[_row(D), pl.BlockSpec((NH, TS, HD), lambda i: (0, i, 0)), _row(D), _row(D), _row(D), _vec(1, D), _vec(1, D)],
        out_shape=[
            jax.ShapeDtypeStruct((s, D), BF16), jax.ShapeDtypeStruct((NH, s, HD), F32), jax.ShapeDtypeStruct((s, D), BF16),
            jax.ShapeDtypeStruct((s, D), F32), jax.ShapeDtypeStruct((s, D), BF16),
            jax.ShapeDtypeStruct((1, D), F32), jax.ShapeDtypeStruct((1, D), F32),
        ],
        compiler_params=_params(dimension_semantics=("arbitrary",)),
    )(o, zr, h, zr, dycat, dycat, gain_a, gain_l)


def _residual(x, mix, post_gain):
    s = x.shape[0]

    def body(x_ref, m_ref, g_ref, h_ref, hb_ref):
        mhat, _ = _rms_fwd(m_ref[...], None)
        h1 = x_ref[...] + mhat * g_ref[...]
        h_ref[...] = h1
        hb_ref[...] = h1.astype(BF16)

    return pl.pallas_call(
        body, name="residual", grid=(s // TS,),
        in_specs=[_row(D), _row(D), _vec(1, D)], out_specs=[_row(D), _row(D)],
        out_shape=[jax.ShapeDtypeStruct((s, D), F32), jax.ShapeDtypeStruct((s, D), BF16)],
        compiler_params=_params(dimension_semantics=("parallel",)),
    )(x, mix, post_gain)


def _head(h1, pe, gp, tgt, ple_gain, b_gate):
    s = h1.shape[0]

    def body(h_ref, pe_ref, gp_ref, t_ref, kg_ref, b_ref, loss_ref, dy_ref, dgp_ref, dpe_ref, gk_ref, gb_ref):
        @pl.when(pl.program_id(0) == 0)
        def _():
            loss_ref[...] = jnp.zeros_like(loss_ref)
            gk_ref[...] = jnp.zeros_like(gk_ref)
            gb_ref[...] = jnp.zeros_like(gb_ref)

        ehat, rstd = _rms_fwd(pe_ref[...], None)
        e = ehat * kg_ref[...]
        gate = _sigmoid(gp_ref[...] + b_ref[...])
        diff = (h_ref[...] + gate * e) - t_ref[...]
        per_row = jnp.mean(diff * diff, axis=-1, keepdims=True)
        loss_ref[...] += 0.5 * jnp.sum(per_row, axis=0, keepdims=True)
        dy = diff * (1.0 / D)
        dy_ref[...] = dy
        dgp = dy * e * gate * (1.0 - gate)
        dgp_ref[...] = dgp.astype(BF16)
        gb_ref[...] += _colsum(dgp)
        de = dy * gate
        gk_ref[...] += _colsum(de * ehat)
        dpe_ref[...] = _rms_bwd(ehat, rstd, de * kg_ref[...]).astype(BF16)

    return pl.pallas_call(
        body, name="head", grid=(s // TS,),
        in_specs=[_row(D), _row(D), _row(D), _row(D), _vec(1, D), _vec(1, D)],
        out_specs=[_vec(1, 1), _row(D), _row(D), _row(D), _vec(1, D), _vec(1, D)],
        out_shape=[
            jax.ShapeDtypeStruct((1, 1), F32), jax.ShapeDtypeStruct((s, D), F32), jax.ShapeDtypeStruct((s, D), BF16),
            jax.ShapeDtypeStruct((s, D), BF16), jax.ShapeDtypeStruct((1, D), F32), jax.ShapeDtypeStruct((1, D), F32),
        ],
        compiler_params=_params(dimension_semantics=("arbitrary",)),
    )(h1, pe, gp, tgt, ple_gain, b_gate)


def _residual_bwd(dy, t, mix, post_gain):
    s = dy.shape[0]

    def body(dy_ref, t_ref, m_ref, g_ref, dh_ref, dm_ref, gg_ref):
        @pl.when(pl.program_id(0) == 0)
        def _():
            gg_ref[...] = jnp.zeros_like(gg_ref)

        dh1 = dy_ref[...] + t_ref[...]
        dh_ref[...] = dh1
        mhat, rstd = _rms_fwd(m_ref[...], None)
        gg_ref[...] += _colsum(dh1 * mhat)
        dm_ref[...] = _rms_bwd(mhat, rstd, dh1 * g_ref[...]).astype(BF16)

    return pl.pallas_call(
        body, name="residual_bwd", grid=(s // TS,),
        in_specs=[_row(D), _row(D), _row(D), _vec(1, D)], out_specs=[_row(D), _row(D), _vec(1, D)],
        out_shape=[jax.ShapeDtypeStruct((s, D), F32), jax.ShapeDtypeStruct((s, D), BF16), jax.ShapeDtypeStruct((1, D), F32)],
        compiler_params=_params(dimension_semantics=("arbitrary",)),
    )(dy, t, mix, post_gain)


def _prenorm_bwd(x, dxn_a, dxn_b, dh1, pre_gain):
    s = x.shape[0]

    def body(x_ref, da_ref, db_ref, dh_ref, g_ref, dx_ref, gg_ref):
        @pl.when(pl.program_id(0) == 0)
        def _():
            gg_ref[...] = jnp.zeros_like(gg_ref)

        xhat, rstd = _rms_fwd(x_ref[...], None)
        dxn = da_ref[...] + db_ref[...]
        gg_ref[...] += _colsum(dxn * xhat)
        dx_ref[...] = dh_ref[...] + _rms_bwd(xhat, rstd, dxn * g_ref[...])

    return pl.pallas_call(
        body, name="prenorm_bwd", grid=(s // TS,),
        in_specs=[_row(D), _row(D), _row(D), _row(D), _vec(1, D)], out_specs=[_row(D), _vec(1, D)],
        out_shape=[jax.ShapeDtypeStruct((s, D), F32), jax.ShapeDtypeStruct((1, D), F32)],
        compiler_params=_params(dimension_semantics=("arbitrary",)),
    )(x, dxn_a, dxn_b, dh1, pre_gain)


def _adamw(name, parts, w, m, v):
    r, c = w.shape
    tr = _pick(r, (256, 128, 16, 8)) if r % 8 == 0 else r
    blk = pl.BlockSpec((tr, c), lambda i: (i, 0))

    def body(p_ref, w_ref, m_ref, v_ref, g_ref, d_ref, nm_ref, nv_ref):
        g = p_ref[0].astype(F32)
        for j in range(1, N_DEV):
            g = g + p_ref[j].astype(F32)
        g_ref[...] = g
        nm = ADAM_B1 * m_ref[...] + (1.0 - ADAM_B1) * g
        nv = ADAM_B2 * v_ref[...] + (1.0 - ADAM_B2) * (g * g)
        nm_ref[...] = nm
        nv_ref[...] = nv
        m_hat = nm / (1.0 - ADAM_B1 ** ADAM_STEP)
        v_hat = nv / (1.0 - ADAM_B2 ** ADAM_STEP)
        d_ref[...] = -ADAM_LR * (m_hat / (jnp.sqrt(v_hat) + ADAM_EPS) + ADAM_WD * w_ref[...])

    return pl.pallas_call(
        body, name=name, grid=(r // tr,),
        in_specs=[pl.BlockSpec((N_DEV, tr, c), lambda i: (0, i, 0)), blk, blk, blk],
        out_specs=[blk] * 4,
        out_shape=[jax.ShapeDtypeStruct((r, c), F32)] * 4,
        compiler_params=_params(dimension_semantics=("parallel",)),
    )(parts, w, m, v)


def _spread8(v):
    r = v.shape[0]
    return jnp.pad(jnp.pad(v[:, :, None], ((0, 0), (0, 0), (0, 7))).reshape(r, 8 * NH), ((0, 0), (0, HD - 8 * NH)))


def _gather8(v):
    return v[:, :8 * NH].reshape(v.shape[0], NH, 8)[:, :, 0]


def _cols_to_shards(g):
    r, c8 = g.shape
    return g.reshape(r, N_DEV, c8 // N_DEV).transpose(1, 0, 2)


def _shards_to_cols(g):
    n, r, c = g.shape
    return g.transpose(1, 0, 2).reshape(r, n * c)


def kernel(x, p, w_in, b_f, pre_gain, post_gain, conv_w, conv_b, w_rgate, b_rgate, w_igate, b_igate, lru_lambda, attn_out_gain, lru_out_gain, w_out, w_ple, ple_gain, w_ple_gate, b_ple_gate, loss_target, m_w_in, m_b_f, m_pre_gain, m_post_gain, m_conv_w, m_conv_b, m_w_rgate, m_b_rgate, m_w_igate, m_b_igate, m_lru_lambda, m_attn_out_gain, m_lru_out_gain, m_w_out, m_w_ple, m_ple_gain, m_w_ple_gate, m_b_ple_gate, v_w_in, v_b_f, v_pre_gain, v_post_gain, v_conv_w, v_conv_b, v_w_rgate, v_b_rgate, v_w_igate, v_b_igate, v_lru_lambda, v_attn_out_gain, v_lru_out_gain, v_w_out, v_w_ple, v_ple_gain, v_w_ple_gate, v_b_ple_gate):
    me = 4 * lax.axis_index("x") + 2 * lax.axis_index("y") + lax.axis_index("c")
    x2, p2, tgt = x[0], p[0, 0], loss_target[0]

    conv_w_shard8 = jnp.pad(conv_w[0], ((0, 8 - CONV_W), (0, 0)))
    g_win, g_wout, g_wple, g_wpg, g_conv = _exchange(
        "gather_weights",
        [w_in[0].astype(BF16), w_out[0].astype(BF16), w_ple[0].astype(BF16), w_ple_gate[0].astype(BF16), conv_w_shard8],
        ["bcast"] * 5)
    win_full = _shards_to_cols(g_win)
    w_qkv = win_full[:, :D_QKV]
    w_rest = jnp.concatenate([win_full[:, D_QKV + NH:], _spread8(win_full[:, D_QKV:D_QKV + NH])], axis=1)
    wout_full = g_wout.reshape(2 * D, D)
    wple_full = _shards_to_cols(g_wple)
    wpg_full = g_wpg.reshape(D, D)
    conv_w8 = _shards_to_cols(g_conv)
    bf_pad = _spread8(b_f)
    w_r, w_i = w_rgate[0].astype(BF16), w_igate[0].astype(BF16)

    xn = _prenorm(x2, pre_gain)
    zq = _mm("proj_qkv", xn, w_qkv, "nn", BF16)
    zr = _mm("proj_rest", xn, w_rest, "nn", F32)
    ccol, crow = _forget_fwd(zr, bf_pad)
    o, aq = _attn_fwd(zq, ccol, crow)
    xc, h = _lru_fwd(zr, conv_w8, conv_b, w_r, b_rgate, w_i, b_igate, lru_lambda)
    ycat = _branch_out(o, h, zr, attn_out_gain, lru_out_gain)
    mix = _mm("proj_out", ycat, wout_full, "nn", F32)
    h1, h1b = _residual(x2, mix, post_gain)
    pe = _mm("proj_ple", p2, wple_full, "nn", F32)
    gp = _mm("proj_gate", h1b, wpg_full, "nn", F32)
    loss_part, dy, dgp, dpe, g_ple_gain, g_b_gate = _head(h1, pe, gp, tgt, ple_gain, b_ple_gate)
    loss = lax.psum(loss_part[0, 0], ("x", "y", "c"))

    t = _mm("bwd_gate_x", dgp, wpg_full, "nt", F32)
    gw_pg = _mm("bwd_gate_w", h1b, dgp, "tn", BF16)
    gw_ple = _mm("bwd_ple_w", p2, dpe, "tn", BF16)
    dh1, dmix, g_post_gain = _residual_bwd(dy, t, mix, post_gain)
    dycat = _mm("bwd_out_x", dmix, wout_full, "nt", F32)
    gw_out = _mm("bwd_out_w", ycat, dmix, "tn", BF16)
    do, delta, dga, dh, dgl, g_aog, g_log = _branch_out_bwd(o, h, zr, dycat, attn_out_gain, lru_out_gain)
    dq, dk, dv, dcs, drs = _attn_bwd(zq, do, aq, delta, crow)
    dxl, g_wr, g_wi, g_br, g_bi, g_lam, g_cb, g_cw8 = _lru_bwd(
        zr, xc, h, dh, conv_w8, w_r, b_rgate, w_i, b_igate, lru_lambda)
    dfl, g_bf_pad = _forget_bwd(dcs, drs, zr, bf_pad)
    dzq = jnp.concatenate([dq.astype(BF16), dk, dv], axis=1)
    dzr = jnp.concatenate([dga, dxl, dgl, dfl], axis=1)
    dxn_a = _mm("bwd_qkv_x", dzq, w_qkv, "nt", F32)
    dxn_b = _mm("bwd_rest_x", dzr, w_rest, "nt", F32)
    gw_qkv = _mm("bwd_qkv_w", xn, dzq, "tn", BF16)
    gw_rest = _mm("bwd_rest_w", xn, dzr, "tn", BF16)
    grad_x, g_pre_gain = _prenorm_bwd(x2, dxn_a, dxn_b, dh1, pre_gain)

    gw_in = jnp.concatenate([gw_qkv, _gather8(gw_rest[:, FL_COL:]), gw_rest[:, :FL_COL]], axis=1)
    small = jnp.concatenate(
        [jnp.pad(_gather8(g_bf_pad), ((0, 0), (0, D - NH))), g_pre_gain, g_post_gain, g_cb, g_br, g_bi, g_lam, g_aog, g_log,
         g_ple_gain, g_b_gate, g_cw8[:CONV_W], jnp.zeros((1, D), F32)], axis=0)
    gates = jnp.concatenate([g_wr.reshape(D, HD), g_wi.reshape(D, HD)], axis=0).astype(BF16)
    r_win, r_wout, r_wple, r_wpg, r_gates, r_small = _exchange(
        "exchange_grads",
        [_cols_to_shards(gw_in), gw_out.reshape(N_DEV, 2 * D // N_DEV, D), _cols_to_shards(gw_ple),
         gw_pg.reshape(N_DEV, D // N_DEV, D), gates, small],
        ["scatter"] * 4 + ["bcast"] * 2)

    upd = {}
    upd["w_in"] = _adamw("adamw_w_in", r_win, w_in[0], m_w_in[0], v_w_in[0])
    upd["w_out"] = _adamw("adamw_w_out", r_wout, w_out[0], m_w_out[0], v_w_out[0])
    upd["w_ple"] = _adamw("adamw_w_ple", r_wple, w_ple[0], m_w_ple[0], v_w_ple[0])
    upd["w_ple_gate"] = _adamw("adamw_w_ple_gate", r_wpg, w_ple_gate[0], m_w_ple_gate[0], v_w_ple_gate[0])
    gates_of = lambda a, b: jnp.concatenate([a[0].reshape(D, HD), b[0].reshape(D, HD)], axis=0)
    g_gates = _adamw("adamw_gates", r_gates, gates_of(w_rgate, w_igate), gates_of(m_w_rgate, m_w_igate),
                     gates_of(v_w_rgate, v_w_igate))
    upd["w_rgate"] = [a[:D].reshape(1, NH, HD, HD) for a in g_gates]
    upd["w_igate"] = [a[D:].reshape(1, NH, HD, HD) for a in g_gates]
    vec_names = ["b_f", "pre_gain", "post_gain", "conv_b", "b_rgate", "b_igate", "lru_lambda", "attn_out_gain",
                 "lru_out_gain", "ple_gain", "b_ple_gate"]
    vec_w = dict(b_f=(b_f, m_b_f, v_b_f), pre_gain=(pre_gain, m_pre_gain, v_pre_gain),
                 post_gain=(post_gain, m_post_gain, v_post_gain), conv_b=(conv_b, m_conv_b, v_conv_b),
                 b_rgate=(b_rgate, m_b_rgate, v_b_rgate), b_igate=(b_igate, m_b_igate, v_b_igate),
                 lru_lambda=(lru_lambda, m_lru_lambda, v_lru_lambda),
                 attn_out_gain=(attn_out_gain, m_attn_out_gain, v_attn_out_gain),
                 lru_out_gain=(lru_out_gain, m_lru_out_gain, v_lru_out_gain), ple_gain=(ple_gain, m_ple_gain, v_ple_gain),
                 b_ple_gate=(b_ple_gate, m_b_ple_gate, v_b_ple_gate))
    conv_mine = lambda a: lax.dynamic_slice_in_dim(a, me * HD, HD, axis=1)

    def small_rows(k):
        rows = [jnp.pad(vec_w[nm][k], ((0, 0), (0, D - vec_w[nm][k].shape[1]))) for nm in vec_names]
        cw = (conv_w, m_conv_w, v_conv_w)[k][0]
        full = lax.dynamic_update_slice_in_dim(jnp.ones((CONV_W, D), F32), cw, me * HD, axis=1)
        return jnp.concatenate(rows + [full, jnp.ones((1, D), F32)], axis=0)

    g_small = _adamw("adamw_small", r_small, small_rows(0), small_rows(1), small_rows(2))
    for idx, nm in enumerate(vec_names):
        width = vec_w[nm][0].shape[1]
        upd[nm] = [a[idx:idx + 1, :width] for a in g_small]
    base = len(vec_names)
    upd["conv_w"] = [conv_mine(a[base:base + CONV_W])[None] for a in g_small]
    for nm in ("w_in", "w_out", "w_ple", "w_ple_gate"):
        upd[nm] = [a[None] for a in upd[nm]]

    order = ["w_in", "b_f", "pre_gain", "post_gain", "conv_w", "conv_b", "w_rgate", "b_rgate", "w_igate", "b_igate",
             "lru_lambda", "attn_out_gain", "lru_out_gain", "w_out", "w_ple", "ple_gain", "w_ple_gate", "b_ple_gate"]
    outs = [loss, grad_x[None]]
    for k in range(4):
        outs += [upd[nm][k] for nm in order]
    return tuple(outs)
```

```python
import functools

import jax
import jax.numpy as jnp
from jax import lax
from jax.experimental import pallas as pl
from jax.experimental.pallas import tpu as pltpu

F32 = jnp.float32
BF16 = jnp.bfloat16

N_DEV = 8
D = 1024
HD = 128
NH = 8
D_IN = 6152
D_IN_SHARD = D_IN // N_DEV
D_QKV = 3 * D
D_REST = 3 * D + HD
FL_COL = 3 * D
D_PLE = 256
CONV_W = 4
LRU_C = 8.0
RMS_EPS = 1e-6
SCALE = HD ** -0.5
EXP2_SCALE = SCALE * 1.4426950408889634
NEG = -1e30

ADAM_LR = 0.001
ADAM_B1 = 0.9
ADAM_B2 = 0.999
ADAM_EPS = 1e-08
ADAM_WD = 0.01
ADAM_STEP = 10

TS = 256
TQ = 512
VMEM_LIMIT = 48 * 1024 * 1024

NT_DIMS = (((1,), (1,)), ((), ()))
TN_DIMS = (((0,), (0,)), ((), ()))


def _params(**kw):
    return pltpu.CompilerParams(vmem_limit_bytes=VMEM_LIMIT, **kw)


def _sigmoid(v):
    return 1.0 / (1.0 + jnp.exp(-v))


def _rms_fwd(v, gain):
    rstd = lax.rsqrt(jnp.mean(v * v, axis=-1, keepdims=True) + RMS_EPS)
    return v * rstd, rstd


def _rms_bwd(vhat, rstd, dvhat):
    return rstd * (dvhat - vhat * jnp.mean(dvhat * vhat, axis=-1, keepdims=True))


def _colsum(v):
    return jnp.sum(v, axis=0, keepdims=True)


def _rows_iota(t):
    return lax.broadcasted_iota(jnp.int32, (t, 1), 0)


def _scan_fwd(a, u):
    t = a.shape[0]
    rows = _rows_iota(t)
    d = 1
    while d < t:
        valid = rows >= d
        u = jnp.where(valid, u + a * pltpu.roll(u, d, 0), u)
        a = jnp.where(valid, a * pltpu.roll(a, d, 0), a)
        d *= 2
    return a, u


def _scan_bwd(b, g):
    t = b.shape[0]
    rows = _rows_iota(t)
    d = 1
    while d < t:
        valid = rows < t - d
        g = jnp.where(valid, g + b * pltpu.roll(g, t - d, 0), g)
        b = jnp.where(valid, b * pltpu.roll(b, t - d, 0), b)
        d *= 2
    return g


def _cumsum_fwd(v):
    t = v.shape[0]
    rows = _rows_iota(t)
    d = 1
    while d < t:
        v = jnp.where(rows >= d, v + pltpu.roll(v, d, 0), v)
        d *= 2
    return v


def _cumsum_bwd(v):
    t = v.shape[0]
    rows = _rows_iota(t)
    d = 1
    while d < t:
        v = jnp.where(rows < t - d, v + pltpu.roll(v, t - d, 0), v)
        d *= 2
    return v


def _shift_down(ext, k, t):
    return pltpu.roll(ext, k, 0)[8:, :] if k else ext[8:, :]


def _shift_up(ext, k, t):
    return pltpu.roll(ext, t + 8 - k, 0)[:t, :] if k else ext[:t, :]


def _exchange(name, arrs, kinds):
    n = len(arrs)
    out_shape = []
    for a, kind in zip(arrs, kinds):
        shp = a.shape if kind == "scatter" else (N_DEV,) + a.shape
        out_shape.append(jax.ShapeDtypeStruct(shp, a.dtype))

    def body(*refs):
        ins, outs = refs[:n], refs[n:2 * n]
        send_sems, recv_sems, local_sems = refs[2 * n:]
        x, y, c = lax.axis_index("x"), lax.axis_index("y"), lax.axis_index("c")
        me = 4 * x + 2 * y + c
        copies = []
        for i in range(n):
            scatter = kinds[i] == "scatter"
            mine = pltpu.make_async_copy(ins[i].at[me] if scatter else ins[i], outs[i].at[me], local_sems.at[i])
            mine.start()
            copies.append(mine)
            for m in range(1, N_DEV):
                px = 1 - x if m & 4 else x
                py = 1 - y if m & 2 else y
                pc = 1 - c if m & 1 else c
                peer = 4 * px + 2 * py + pc
                cp = pltpu.make_async_remote_copy(
                    src_ref=ins[i].at[peer] if scatter else ins[i],
                    dst_ref=outs[i].at[me],
                    send_sem=send_sems.at[i, m - 1],
                    recv_sem=recv_sems.at[i, m - 1],
                    device_id=(px, py, pc),
                    device_id_type=pl.DeviceIdType.MESH,
                )
                cp.start()
                copies.append(cp)
        for cp in copies:
            cp.wait()

    any_spec = pl.BlockSpec(memory_space=pl.ANY)
    return pl.pallas_call(
        body,
        name=name,
        out_shape=out_shape,
        in_specs=[any_spec] * n,
        out_specs=[any_spec] * n,
        scratch_shapes=[
            pltpu.SemaphoreType.DMA((n, N_DEV - 1)),
            pltpu.SemaphoreType.DMA((n, N_DEV - 1)),
            pltpu.SemaphoreType.DMA((n,)),
        ],
        compiler_params=pltpu.CompilerParams(has_side_effects=True),
    )(*arrs)


def _peers(x, y, c):
    out = []
    for m in range(1, N_DEV):
        px = 1 - x if m & 4 else x
        py = 1 - y if m & 2 else y
        pc = 1 - c if m & 1 else c
        out.append((m, (px, py, pc), 4 * px + 2 * py + pc))
    return out


def _split_copies(kinds, src_refs, land_refs, send_sems, recv_sems):
    x, y, c = lax.axis_index("x"), lax.axis_index("y"), lax.axis_index("c")
    me = 4 * x + 2 * y + c
    copies = []
    for i, kind in enumerate(kinds):
        for m, peer, pidx in _peers(x, y, c):
            copies.append(pltpu.make_async_remote_copy(
                src_ref=src_refs[i].at[pidx] if kind == "scatter" else src_refs[i],
                dst_ref=land_refs[i].at[me],
                send_sem=send_sems.at[i * (N_DEV - 1) + m - 1],
                recv_sem=recv_sems.at[i * (N_DEV - 1) + m - 1],
                device_id=peer,
                device_id_type=pl.DeviceIdType.MESH,
            ))
    return copies


_HBM_SPEC = pl.BlockSpec(memory_space=pltpu.HBM)
_SEM_SPEC = pl.BlockSpec(memory_space=pltpu.SEMAPHORE)
_DATAFLOW = pltpu.SideEffectType.DATAFLOW_SIDE_EFFECTING


def _exchange_start(name, arrs, kinds):
    n = len(arrs)
    lands = []
    for a, kind in zip(arrs, kinds):
        shp = a.shape if kind == "scatter" else (N_DEV,) + a.shape
        lands.append(lax.empty(shp, a.dtype))

    def body(*refs):
        src_refs, land_refs = refs[:n], refs[n:2 * n]
        send_sems, recv_sems = refs[2 * n:2 * n + 2]
        token = refs[-1]
        for cp in _split_copies(kinds, src_refs, land_refs, send_sems, recv_sems):
            cp.start()
        token[...] = jnp.zeros_like(token)

    n_sem = n * (N_DEV - 1)
    hbm = lambda a: pltpu.HBM(a.shape, a.dtype)
    res = pl.pallas_call(
        body, name=name,
        out_shape=(pltpu.SemaphoreType.DMA((n_sem,)), pltpu.SemaphoreType.DMA((n_sem,)),
                   *[hbm(a) for a in arrs], *[hbm(a) for a in lands], jax.ShapeDtypeStruct((8, HD), F32)),
        in_specs=[_HBM_SPEC] * (2 * n),
        out_specs=(_SEM_SPEC, _SEM_SPEC, *[_HBM_SPEC] * (2 * n), pl.BlockSpec(memory_space=pltpu.VMEM)),
        input_output_aliases={i: 2 + i for i in range(2 * n)},
        compiler_params=pltpu.CompilerParams(has_side_effects=_DATAFLOW),
    )(*[pltpu.with_memory_space_constraint(a, pltpu.HBM) for a in arrs],
      *[pltpu.with_memory_space_constraint(a, pltpu.HBM) for a in lands])
    return (kinds, res[0], res[1], res[2:2 + n], res[2 + n:2 + 2 * n]), res[-1]


def _exchange_wait(name, state, after):
    kinds, send_sems, recv_sems, srcs, lands = state
    n = len(srcs)

    def body(*refs):
        src_refs, land_refs = refs[:n], refs[n:2 * n]
        send_sems_ref, recv_sems_ref = refs[2 * n:2 * n + 2]
        for cp in _split_copies(kinds, src_refs, land_refs, send_sems_ref, recv_sems_ref):
            cp.wait_send()
            cp.wait_recv()

    res = pl.pallas_call(
        body, name=name,
        out_shape=tuple(pltpu.HBM(a.shape, a.dtype) for a in (*srcs, *lands)),
        in_specs=[_HBM_SPEC] * (2 * n) + [_SEM_SPEC, _SEM_SPEC, pl.BlockSpec(memory_space=pl.ANY)],
        out_specs=tuple([_HBM_SPEC] * (2 * n)),
        input_output_aliases={i: i for i in range(2 * n)},
        compiler_params=pltpu.CompilerParams(has_side_effects=_DATAFLOW),
    )(*srcs, *lands, send_sems, recv_sems, after)
    me = 4 * lax.axis_index("x") + 2 * lax.axis_index("y") + lax.axis_index("c")
    outs = []
    for kind, src, land in zip(kinds, res[:n], res[n:]):
        own = lax.dynamic_index_in_dim(src, me, 0, keepdims=False) if kind == "scatter" else src
        outs.append(lax.dynamic_update_index_in_dim(land, own, me, 0))
    return outs


def _pick(n, cands):
    for t in cands:
        if n % t == 0:
            return t
    raise ValueError(f"no tile for {n}")


def _mm(name, a, b, mode, out_dtype):
    if mode == "nn":
        (m, k), (k2, n) = a.shape, b.shape
    elif mode == "nt":
        (m, k), (n, k2) = a.shape, b.shape
    else:
        (k, m), (k2, n) = a.shape, b.shape
    assert k == k2, (name, a.shape, b.shape)
    tm = _pick(m, (1024, 512, 256) if mode == "tn" else (512, 256))
    tn = _pick(n, (1024, 640, 512, 256, 128))
    tk = _pick(k, (512, 256) if mode == "tn" else (1024, 640, 512, 256))
    nk = k // tk

    def body(a_ref, b_ref, o_ref, acc_ref):
        kk = pl.program_id(2)
        av = a_ref[...].astype(BF16)
        bv = b_ref[...].astype(BF16)
        if mode == "nn":
            part = jnp.dot(av, bv, preferred_element_type=F32)
        elif mode == "nt":
            part = lax.dot_general(av, bv, NT_DIMS, preferred_element_type=F32)
        else:
            part = lax.dot_general(av, bv, TN_DIMS, preferred_element_type=F32)

        @pl.when(kk == 0)
        def _():
            acc_ref[...] = part

        @pl.when(kk > 0)
        def _():
            acc_ref[...] += part

        @pl.when(kk == nk - 1)
        def _():
            o_ref[...] = acc_ref[...].astype(out_dtype)

    if mode == "tn":
        a_spec = pl.BlockSpec((tk, tm), lambda i, j, kk: (kk, i))
    else:
        a_spec = pl.BlockSpec((tm, tk), lambda i, j, kk: (i, kk))
    if mode == "nt":
        b_spec = pl.BlockSpec((tn, tk), lambda i, j, kk: (j, kk))
    else:
        b_spec = pl.BlockSpec((tk, tn), lambda i, j, kk: (kk, j))
    return pl.pallas_call(
        body,
        name=name,
        grid=(m // tm, n // tn, nk),
        in_specs=[a_spec, b_spec],
        out_specs=pl.BlockSpec((tm, tn), lambda i, j, kk: (i, j)),
        out_shape=jax.ShapeDtypeStruct((m, n), out_dtype),
        scratch_shapes=[pltpu.VMEM((tm, tn), F32)],
        compiler_params=_params(dimension_semantics=("parallel", "parallel", "arbitrary")),
    )(a, b)


def _row(c, col=0):
    return pl.BlockSpec((TS, c), lambda i: (i, col))


def _vec(r, c):
    return pl.BlockSpec((r, c), lambda i: (0, 0))


def _prenorm(x, pre_gain):
    s = x.shape[0]

    def body(x_ref, g_ref, o_ref):
        xhat, _ = _rms_fwd(x_ref[...], g_ref[...])
        o_ref[...] = (xhat * g_ref[...]).astype(BF16)

    return pl.pallas_call(
        body, name="prenorm", grid=(s // TS,),
        in_specs=[_row(D), _vec(1, D)], out_specs=_row(D),
        out_shape=jax.ShapeDtypeStruct((s, D), BF16),
        compiler_params=_params(dimension_semantics=("parallel",)),
    )(x, pre_gain)


def _forget_fwd(zr, bf_pad):
    s = zr.shape[0]
    n = s // TQ

    def body(fl_ref, b_ref, ccol_ref, crow_ref, c_buf, carry):
        i = pl.program_id(0)

        @pl.when(i == 0)
        def _():
            carry[...] = jnp.zeros_like(carry)

        fl = fl_ref[...] + b_ref[...]
        ls = jnp.minimum(fl, 0.0) - jnp.log(1.0 + jnp.exp(-jnp.abs(fl)))
        c_buf[...] = _cumsum_fwd(ls) + carry[0:1, :]
        carry[0:1, :] = c_buf[TQ - 1:TQ, :]
        cv = c_buf[...]
        for h in range(NH):
            ccol_ref[h] = jnp.broadcast_to(cv[:, 8 * h:8 * h + 1], (TQ, HD))
        crow_ref[0] = cv.T

    return pl.pallas_call(
        body, name="forget_fwd", grid=(n,),
        in_specs=[pl.BlockSpec((TQ, HD), lambda i: (i, FL_COL // HD)), _vec(1, HD)],
        out_specs=[pl.BlockSpec((NH, TQ, HD), lambda i: (0, i, 0)), pl.BlockSpec((1, HD, TQ), lambda i: (i, 0, 0))],
        out_shape=[jax.ShapeDtypeStruct((NH, s, HD), F32), jax.ShapeDtypeStruct((n, HD, TQ), F32)],
        scratch_shapes=[pltpu.VMEM((TQ, HD), F32), pltpu.VMEM((8, HD), F32)],
        compiler_params=_params(dimension_semantics=("arbitrary",)),
    )(zr, bf_pad)


def _attn_fwd(zq, ccol, crow):
    s = zq.shape[0]
    n = s // TQ
    nb = TQ // HD

    def body(q_ref, k_ref, v_ref, cc_ref, cr_ref, o_ref, aq_ref):
        i = pl.program_id(1)
        q = q_ref[...]
        cq = cc_ref[0] * (1.0 / SCALE)
        row = lax.broadcasted_iota(jnp.int32, (TQ, HD), 0)
        col = lax.broadcasted_iota(jnp.int32, (TQ, HD), 1)

        def step(j, carry, masked):
            m, l, acc = carry
            rows = pl.ds(pl.multiple_of(j * TQ, TQ), TQ)
            u = lax.dot_general(q, k_ref[rows, :], NT_DIMS, preferred_element_type=F32)
            ck = cr_ref[j][0:1, :] * (1.0 / SCALE)
            us = [u[:, HD * b:HD * (b + 1)] + (cq - ck[:, HD * b:HD * (b + 1)]) for b in range(nb)]
            if masked:
                us = [jnp.where(row >= col + HD * b, us[b], NEG) for b in range(nb)]
            bm = functools.reduce(jnp.maximum, us)
            m_new = jnp.maximum(m, jnp.max(bm, axis=1, keepdims=True))
            alpha = jnp.exp2((m - m_new) * EXP2_SCALE)
            ps = [jnp.exp2((ub - m_new) * EXP2_SCALE) for ub in us]
            l = alpha * l + functools.reduce(jnp.add, ps)
            pr = jnp.concatenate(ps, axis=1).astype(BF16)
            acc = alpha * acc + jnp.dot(pr, v_ref[rows, :], preferred_element_type=F32)
            return m_new, l, acc

        init = (jnp.full((TQ, HD), NEG, F32), jnp.zeros((TQ, HD), F32), jnp.zeros((TQ, HD), F32))
        carry = lax.fori_loop(0, i, lambda j, cr: step(j, cr, False), init)
        m, l, acc = step(i, carry, True)
        l_row = jnp.sum(l, axis=1, keepdims=True)
        o_ref[...] = acc / l_row
        aq_ref[0] = cq - (m + jnp.log(l_row) * (1.0 / SCALE))

    return pl.pallas_call(
        body, name="attn_fwd", grid=(NH, n),
        in_specs=[
            pl.BlockSpec((TQ, HD), lambda h, i: (i, h)),
            pl.BlockSpec((s, HD), lambda h, i: (0, NH + h)),
            pl.BlockSpec((s, HD), lambda h, i: (0, 2 * NH + h)),
            pl.BlockSpec((1, TQ, HD), lambda h, i: (h, i, 0)),
            pl.BlockSpec((n, 8, TQ), lambda h, i: (0, h, 0)),
        ],
        out_specs=[pl.BlockSpec((TQ, HD), lambda h, i: (i, h)), pl.BlockSpec((1, TQ, HD), lambda h, i: (h, i, 0))],
        out_shape=[jax.ShapeDtypeStruct((s, D), F32), jax.ShapeDtypeStruct((NH, s, HD), F32)],
        compiler_params=_params(dimension_semantics=("parallel", "parallel")),
    )(zq, zq, zq, ccol, crow)


def _attn_bwd(zq, do, aq, delta, crow):
    s = zq.shape[0]
    n = s // TQ
    nb = TQ // HD

    def body(k_ref, v_ref, q_ref, do_ref, aq_ref, dl_ref, cr_ref, dq_ref, dk_ref, dv_ref, dcs_ref, drs_ref):
        j = pl.program_id(1)

        @pl.when(j == 0)
        def _():
            dq_ref[...] = jnp.zeros_like(dq_ref)
            drs_ref[...] = jnp.zeros_like(drs_ref)

        k = k_ref[...]
        v = v_ref[...]
        ck = cr_ref[0][0:1, :] * (1.0 / SCALE)
        row = lax.broadcasted_iota(jnp.int32, (TQ, HD), 0)
        col = lax.broadcasted_iota(jnp.int32, (TQ, HD), 1)

        def step(i, carry, masked):
            dk, dv, dcs = carry
            rows = pl.ds(pl.multiple_of(i * TQ, TQ), TQ)
            q = q_ref[rows, :]
            dout = do_ref[rows, :]
            aqv = aq_ref[0, rows, :]
            dlv = dl_ref[0, rows, :]
            u = lax.dot_general(q, k, NT_DIMS, preferred_element_type=F32)
            dp = lax.dot_general(dout, v, NT_DIMS, preferred_element_type=F32)
            prs, dss = [], []
            for b in range(nb):
                cs = slice(HD * b, HD * (b + 1))
                ub = u[:, cs] + (aqv - ck[:, cs])
                if masked:
                    ub = jnp.where(row >= col + HD * b, ub, NEG)
                pb = jnp.exp2(ub * EXP2_SCALE)
                prs.append(pb)
                dss.append(pb * (dp[:, cs] - dlv))
            drs_ref[0, rows, :] += functools.reduce(jnp.add, dss)
            ds = jnp.concatenate(dss, axis=1)
            dcs = dcs + jnp.sum(ds.reshape(TQ // 8, 8, TQ), axis=0)
            dsb = ds.astype(BF16)
            dv = dv + lax.dot_general(jnp.concatenate(prs, axis=1).astype(BF16), dout, TN_DIMS, preferred_element_type=F32)
            dk = dk + lax.dot_general(dsb, q, TN_DIMS, preferred_element_type=F32)
            dq_ref[rows, :] += jnp.dot(dsb, k, preferred_element_type=F32) * SCALE
            return dk, dv, dcs

        init = (jnp.zeros((TQ, HD), F32), jnp.zeros((TQ, HD), F32), jnp.zeros((8, TQ), F32))
        carry = step(j, init, True)
        dk, dv, dcs = lax.fori_loop(j + 1, n, lambda i, cr: step(i, cr, False), carry)
        dk_ref[...] = (dk * SCALE).astype(BF16)
        dv_ref[...] = dv.astype(BF16)
        dcs_ref[0] = jnp.broadcast_to(_colsum(dcs), (8, TQ))

    return pl.pallas_call(
        body, name="attn_bwd", grid=(NH, n),
        in_specs=[
            pl.BlockSpec((TQ, HD), lambda h, j: (j, NH + h)),
            pl.BlockSpec((TQ, HD), lambda h, j: (j, 2 * NH + h)),
            pl.BlockSpec((s, HD), lambda h, j: (0, h)),
            pl.BlockSpec((s, HD), lambda h, j: (0, h)),
            pl.BlockSpec((1, s, HD), lambda h, j: (h, 0, 0)),
            pl.BlockSpec((1, s, HD), lambda h, j: (h, 0, 0)),
            pl.BlockSpec((1, 8, TQ), lambda h, j: (j, h, 0)),
        ],
        out_specs=[
            pl.BlockSpec((s, HD), lambda h, j: (0, h)),
            pl.BlockSpec((TQ, HD), lambda h, j: (j, h)),
            pl.BlockSpec((TQ, HD), lambda h, j: (j, h)),
            pl.BlockSpec((1, 8, TQ), lambda h, j: (j, h, 0)),
            pl.BlockSpec((1, s, HD), lambda h, j: (h, 0, 0)),
        ],
        out_shape=[
            jax.ShapeDtypeStruct((s, D), F32),
            jax.ShapeDtypeStruct((s, D), BF16),
            jax.ShapeDtypeStruct((s, D), BF16),
            jax.ShapeDtypeStruct((n, 8 * NH, TQ), F32),
            jax.ShapeDtypeStruct((NH, s, HD), F32),
        ],
        compiler_params=_params(dimension_semantics=("parallel", "arbitrary")),
    )(zq, zq, zq, do, aq, delta, crow)


def _forget_bwd(dcs, drs, zr, bf_pad):
    n = dcs.shape[0]
    s = n * TQ

    def body(dcs_ref, drs_ref, fl_ref, b_ref, dfl_ref, gb_ref, buf, carry):
        i = pl.program_id(0)

        @pl.when(i == 0)
        def _():
            carry[...] = jnp.zeros_like(carry)
            gb_ref[...] = jnp.zeros_like(gb_ref)

        dc_t = jnp.concatenate([dcs_ref[0], jnp.zeros((HD - 8 * NH, TQ), F32)], axis=0)
        lane = lax.broadcasted_iota(jnp.int32, (TQ, HD), 1)
        dc = -dc_t.T
        for hh in range(NH):
            dc = dc + jnp.where(lane == 8 * hh, jnp.sum(drs_ref[hh], axis=1, keepdims=True), 0.0)
        buf[...] = _cumsum_bwd(dc) + carry[0:1, :]
        carry[0:1, :] = buf[0:1, :]
        fl = fl_ref[...] + b_ref[...]
        dfl = buf[...] * _sigmoid(-fl)
        dfl_ref[...] = dfl.astype(BF16)
        gb_ref[...] += _colsum(dfl)

    return pl.pallas_call(
        body, name="forget_bwd", grid=(n,),
        in_specs=[
            pl.BlockSpec((1, 8 * NH, TQ), lambda i: (n - 1 - i, 0, 0)),
            pl.BlockSpec((NH, TQ, HD), lambda i: (0, n - 1 - i, 0)),
            pl.BlockSpec((TQ, HD), lambda i: (n - 1 - i, FL_COL // HD)),
            _vec(1, HD),
        ],
        out_specs=[pl.BlockSpec((TQ, HD), lambda i: (n - 1 - i, 0)), _vec(1, HD)],
        out_shape=[jax.ShapeDtypeStruct((s, HD), BF16), jax.ShapeDtypeStruct((1, HD), F32)],
        scratch_shapes=[pltpu.VMEM((TQ, HD), F32), pltpu.VMEM((8, HD), F32)],
        compiler_params=_params(dimension_semantics=("arbitrary",)),
    )(dcs, drs, zr, bf_pad)


def _gates(xc, w_ref, b):
    xb = xc.astype(BF16)
    pre = jnp.concatenate(
        [jnp.dot(xb[:, HD * g:HD * (g + 1)], w_ref[g], preferred_element_type=F32) for g in range(NH)], axis=1)
    return _sigmoid(pre + b)


def _lru_coeffs(r, lam):
    sp = jnp.maximum(-lam, 0.0) + jnp.log(1.0 + jnp.exp(-jnp.abs(lam)))
    log_a = -LRU_C * r * sp
    a = jnp.exp(log_a)
    y = 2.0 * log_a
    em1 = jnp.where(jnp.abs(y) < 0.01, y * (1.0 + y * (0.5 + y * (1.0 / 6.0))), jnp.exp(y) - 1.0)
    return sp, a, jnp.sqrt(-em1)


def _conv_taps(ext, t):
    return [_shift_down(ext, CONV_W - 1 - jj, t) for jj in range(CONV_W)]


def _lru_fwd(zr, conv_w8, conv_b, w_r, b_r, w_i, b_i, lam):
    s = zr.shape[0]
    n = s // TS
    xl_col = 1

    def body(xl_ref, halo_ref, cw_ref, cb_ref, wr_ref, br_ref, wi_ref, bi_ref, lam_ref, xc_ref, h_ref, carry):
        i = pl.program_id(0)

        @pl.when(i == 0)
        def _():
            carry[...] = jnp.zeros_like(carry)

        halo = jnp.where(i == 0, 0.0, halo_ref[...])
        taps = _conv_taps(jnp.concatenate([halo, xl_ref[...]], axis=0), TS)
        xc = cb_ref[...] + sum(cw_ref[jj:jj + 1, :] * taps[jj] for jj in range(CONV_W))
        xc_ref[...] = xc
        r = _gates(xc, wr_ref, br_ref[...])
        ig = _gates(xc, wi_ref, bi_ref[...])
        _, a, gam = _lru_coeffs(r, lam_ref[...])
        a_cum, h_loc = _scan_fwd(a, gam * (ig * xc))
        h_ref[...] = h_loc + a_cum * carry[0:1, :]
        carry[0:1, :] = h_ref[TS - 1:TS, :]

    return pl.pallas_call(
        body, name="lru_fwd", grid=(n,),
        in_specs=[
            _row(D, xl_col),
            pl.BlockSpec((8, D), lambda i: (jnp.maximum(i * (TS // 8) - 1, 0), xl_col)),
            _vec(8, D), _vec(1, D),
            pl.BlockSpec((NH, HD, HD), lambda i: (0, 0, 0)), _vec(1, D),
            pl.BlockSpec((NH, HD, HD), lambda i: (0, 0, 0)), _vec(1, D),
            _vec(1, D),
        ],
        out_specs=[_row(D), _row(D)],
        out_shape=[jax.ShapeDtypeStruct((s, D), F32), jax.ShapeDtypeStruct((s, D), F32)],
        scratch_shapes=[pltpu.VMEM((8, D), F32)],
        compiler_params=_params(dimension_semantics=("arbitrary",)),
    )(zr, zr, conv_w8, conv_b, w_r, b_r, w_i, b_i, lam)


def _lru_bwd(zr, xc, h, dh, conv_w8, w_r, b_r, w_i, b_i, lam):
    s = zr.shape[0]
    n = s // TS
    xl_col = 1

    def rev(i):
        return n - 1 - i

    def body(xl_ref, xlh_ref, xc_ref, h_ref, hh_ref, dh_ref, cw_ref, wr_ref, br_ref, wi_ref, bi_ref, lam_ref,
             dxl_ref, gwr_ref, gwi_ref, gbr_ref, gbi_ref, glam_ref, gcb_ref, gcw_ref, l_buf, dxc_buf, carry_g, carry_dxc):
        i = pl.program_id(0)
        first = rev(i) == 0

        @pl.when(i == 0)
        def _():
            carry_g[...] = jnp.zeros_like(carry_g)
            carry_dxc[...] = jnp.zeros_like(carry_dxc)
            for ref in (gwr_ref, gwi_ref, gbr_ref, gbi_ref, glam_ref, gcb_ref, gcw_ref):
                ref[...] = jnp.zeros_like(ref)

        rows = _rows_iota(TS)
        xc = xc_ref[...]
        lam = lam_ref[...]
        r = _gates(xc, wr_ref, br_ref[...])
        ig = _gates(xc, wi_ref, bi_ref[...])
        sp, a, gam = _lru_coeffs(r, lam)
        g = dh_ref[...] + jnp.where(rows == TS - 1, carry_g[0:1, :], 0.0)
        b = jnp.where(rows == TS - 1, 0.0, pltpu.roll(a, TS - 1, 0))
        l_buf[...] = _scan_bwd(b, g)
        lv = l_buf[...]
        carry_g[0:1, :] = l_buf[0:1, :] * a[0:1, :]
        h_prev_row = jnp.where(first, 0.0, hh_ref[7:8, :])
        h_prev = jnp.where(rows == 0, h_prev_row, pltpu.roll(h_ref[...], 1, 0))
        dgam = lv * ig * xc
        dig = lv * gam * xc
        dxc = lv * gam * ig
        dla = lv * h_prev * a - dgam * (a * a) / gam
        dr = dla * (-LRU_C) * sp
        glam_ref[...] += _colsum(dla * r) * (LRU_C * _sigmoid(-lam))
        dpr = dr * r * (1.0 - r)
        dpi = dig * ig * (1.0 - ig)
        gbr_ref[...] += _colsum(dpr)
        gbi_ref[...] += _colsum(dpi)
        xb = xc.astype(BF16)
        dprb = dpr.astype(BF16)
        dpib = dpi.astype(BF16)
        back = []
        for gi in range(NH):
            cs = slice(HD * gi, HD * (gi + 1))
            gwr_ref[gi] += lax.dot_general(xb[:, cs], dprb[:, cs], TN_DIMS, preferred_element_type=F32)
            gwi_ref[gi] += lax.dot_general(xb[:, cs], dpib[:, cs], TN_DIMS, preferred_element_type=F32)
            back.append(lax.dot_general(dprb[:, cs], wr_ref[gi], NT_DIMS, preferred_element_type=F32)
                        + lax.dot_general(dpib[:, cs], wi_ref[gi], NT_DIMS, preferred_element_type=F32))
        dxc = dxc + jnp.concatenate(back, axis=1)
        dxc_buf[...] = dxc
        gcb_ref[...] += _colsum(dxc)
        halo = jnp.where(first, 0.0, xlh_ref[...])
        taps = _conv_taps(jnp.concatenate([halo, xl_ref[...]], axis=0), TS)
        for jj in range(CONV_W):
            gcw_ref[jj:jj + 1, :] += _colsum(dxc * taps[jj])
        ext = jnp.concatenate([dxc, carry_dxc[...]], axis=0)
        dxl = sum(cw_ref[jj:jj + 1, :] * _shift_up(ext, CONV_W - 1 - jj, TS) for jj in range(CONV_W))
        dxl_ref[...] = dxl.astype(BF16)
        carry_dxc[...] = dxc_buf[0:8, :]

    rowr = lambda c, col=0: pl.BlockSpec((TS, c), lambda i: (rev(i), col))
    halo = lambda col: pl.BlockSpec((8, D), lambda i: (jnp.maximum(rev(i) * (TS // 8) - 1, 0), col))
    gate_w = pl.BlockSpec((NH, HD, HD), lambda i: (0, 0, 0))
    return pl.pallas_call(
        body, name="lru_bwd", grid=(n,),
        in_specs=[rowr(D, xl_col), halo(xl_col), rowr(D), rowr(D), halo(0), rowr(D),
                  _vec(8, D), gate_w, _vec(1, D), gate_w, _vec(1, D), _vec(1, D)],
        out_specs=[rowr(D), gate_w, gate_w, _vec(1, D), _vec(1, D), _vec(1, D), _vec(1, D), _vec(8, D)],
        out_shape=[
            jax.ShapeDtypeStruct((s, D), BF16),
            jax.ShapeDtypeStruct((NH, HD, HD), F32), jax.ShapeDtypeStruct((NH, HD, HD), F32),
            jax.ShapeDtypeStruct((1, D), F32), jax.ShapeDtypeStruct((1, D), F32), jax.ShapeDtypeStruct((1, D), F32),
            jax.ShapeDtypeStruct((1, D), F32), jax.ShapeDtypeStruct((8, D), F32),
        ],
        scratch_shapes=[pltpu.VMEM((TS, D), F32), pltpu.VMEM((TS, D), F32), pltpu.VMEM((8, D), F32), pltpu.VMEM((8, D), F32)],
        compiler_params=_params(dimension_semantics=("arbitrary",)),
    )(zr, zr, xc, h, h, dh, conv_w8, w_r, b_r, w_i, b_i, lam)


def _silu_parts(g):
    sg = _sigmoid(g)
    return g * sg, sg * (1.0 + g * (1.0 - sg))


def _branch_out(o, h, zr, gain_a, gain_l):
    s = o.shape[0]

    def body(o_ref, ga_ref, h_ref, gl_ref, ka_ref, kl_ref, y_ref):
        ohat, _ = _rms_fwd(o_ref[...], None)
        y_ref[:, 0:D] = (ohat * ka_ref[...] * _silu_parts(ga_ref[...])[0]).astype(BF16)
        hhat, _ = _rms_fwd(h_ref[...], None)
        y_ref[:, D:2 * D] = (hhat * kl_ref[...] * _silu_parts(gl_ref[...])[0]).astype(BF16)

    return pl.pallas_call(
        body, name="branch_out", grid=(s // TS,),
        in_specs=[_row(D), _row(D, 0), _row(D), _row(D, 2), _vec(1, D), _vec(1, D)],
        out_specs=_row(2 * D),
        out_shape=jax.ShapeDtypeStruct((s, 2 * D), BF16),
        compiler_params=_params(dimension_semantics=("parallel",)),
    )(o, zr, h, zr, gain_a, gain_l)


def _branch_out_bwd(o, h, zr, dycat, gain_a, gain_l):
    s = o.shape[0]

    def body(o_ref, ga_ref, h_ref, gl_ref, dya_ref, dyl_ref, ka_ref, kl_ref,
             do_ref, dl_ref, dga_ref, dh_ref, dgl_ref, gka_ref, gkl_ref):
        @pl.when(pl.program_id(0) == 0)
        def _():
            gka_ref[...] = jnp.zeros_like(gka_ref)
            gkl_ref[...] = jnp.zeros_like(gkl_ref)

        def one(v, g, dy, gain):
            vhat, rstd = _rms_fwd(v, None)
            sg, dsg = _silu_parts(g)
            dn = dy * sg
            dg = dy * (vhat * gain) * dsg
            return _rms_bwd(vhat, rstd, dn * gain), dg, _colsum(dn * vhat)

        o = o_ref[...]
        dout, dga, gka = one(o, ga_ref[...], dya_ref[...], ka_ref[...])
        do_ref[...] = dout.astype(BF16)
        dga_ref[...] = dga.astype(BF16)
        gka_ref[...] += gka
        prod = dout * o
        for hh in range(NH):
            dl_ref[hh] = jnp.broadcast_to(jnp.sum(prod[:, HD * hh:HD * (hh + 1)], axis=1, keepdims=True), (TS, HD))
        dh, dgl, gkl = one(h_ref[...], gl_ref[...], dyl_ref[...], kl_ref[...])
        dh_ref[...] = dh
        dgl_ref[...] = dgl.astype(BF16)
        gkl_ref[...] += gkl

    return pl.pallas_call(
        body, name="branch_out_bwd", grid=(s // TS,),
        in_specs=[_row(D), _row(D, 0), _row(D), _row(D, 2), _row(D, 0), _row(D, 1), _vec(1, D), _vec(1, D)],
        out_specs=[_row(D), pl.BlockSpec((NH, TS, HD), lambda i: (0, i, 0)), _row(D), _row(D), _row(D), _vec(1, D), _vec(1, D)],
        out_shape=[
            jax.ShapeDtypeStruct((s, D), BF16), jax.ShapeDtypeStruct((NH, s, HD), F32), jax.ShapeDtypeStruct((s, D), BF16),
            jax.ShapeDtypeStruct((s, D), F32), jax.ShapeDtypeStruct((s, D), BF16),
            jax.ShapeDtypeStruct((1, D), F32), jax.ShapeDtypeStruct((1, D), F32),
        ],
        compiler_params=_params(dimension_semantics=("arbitrary",)),
    )(o, zr, h, zr, dycat, dycat, gain_a, gain_l)


def _residual(x, mix, post_gain):
    s = x.shape[0]

    def body(x_ref, m_ref, g_ref, h_ref, hb_ref):
        mhat, _ = _rms_fwd(m_ref[...], None)
        h1 = x_ref[...] + mhat * g_ref[...]
        h_ref[...] = h1
        hb_ref[...] = h1.astype(BF16)

    return pl.pallas_call(
        body, name="residual", grid=(s // TS,),
        in_specs=[_row(D), _row(D), _vec(1, D)], out_specs=[_row(D), _row(D)],
        out_shape=[jax.ShapeDtypeStruct((s, D), F32), jax.ShapeDtypeStruct((s, D), BF16)],
        compiler_params=_params(dimension_semantics=("parallel",)),
    )(x, mix, post_gain)


def _head(h1, pe, gp, tgt, ple_gain, b_gate):
    s = h1.shape[0]

    def body(h_ref, pe_ref, gp_ref, t_ref, kg_ref, b_ref, loss_ref, dy_ref, dgp_ref, dpe_ref, gk_ref, gb_ref):
        @pl.when(pl.program_id(0) == 0)
        def _():
            loss_ref[...] = jnp.zeros_like(loss_ref)
            gk_ref[...] = jnp.zeros_like(gk_ref)
            gb_ref[...] = jnp.zeros_like(gb_ref)

        ehat, rstd = _rms_fwd(pe_ref[...], None)
        e = ehat * kg_ref[...]
        gate = _sigmoid(gp_ref[...] + b_ref[...])
        diff = (h_ref[...] + gate * e) - t_ref[...]
        per_row = jnp.mean(diff * diff, axis=-1, keepdims=True)
        loss_ref[...] += 0.5 * jnp.sum(per_row, axis=0, keepdims=True)
        dy = diff * (1.0 / D)
        dy_ref[...] = dy
        dgp = dy * e * gate * (1.0 - gate)
        dgp_ref[...] = dgp.astype(BF16)
        gb_ref[...] += _colsum(dgp)
        de = dy * gate
        gk_ref[...] += _colsum(de * ehat)
        dpe_ref[...] = _rms_bwd(ehat, rstd, de * kg_ref[...]).astype(BF16)

    return pl.pallas_call(
        body, name="head", grid=(s // TS,),
        in_specs=[_row(D), _row(D), _row(D), _row(D), _vec(1, D), _vec(1, D)],
        out_specs=[_vec(1, 1), _row(D), _row(D), _row(D), _vec(1, D), _vec(1, D)],
        out_shape=[
            jax.ShapeDtypeStruct((1, 1), F32), jax.ShapeDtypeStruct((s, D), F32), jax.ShapeDtypeStruct((s, D), BF16),
            jax.ShapeDtypeStruct((s, D), BF16), jax.ShapeDtypeStruct((1, D), F32), jax.ShapeDtypeStruct((1, D), F32),
        ],
        compiler_params=_params(dimension_semantics=("arbitrary",)),
    )(h1, pe, gp, tgt, ple_gain, b_gate)


def _residual_bwd(dy, t, mix, post_gain):
    s = dy.shape[0]

    def body(dy_ref, t_ref, m_ref, g_ref, dh_ref, dm_ref, gg_ref):
        @pl.when(pl.program_id(0) == 0)
        def _():
            gg_ref[...] = jnp.zeros_like(gg_ref)

        dh1 = dy_ref[...] + t_ref[...]
        dh_ref[...] = dh1
        mhat, rstd = _rms_fwd(m_ref[...], None)
        gg_ref[...] += _colsum(dh1 * mhat)
        dm_ref[...] = _rms_bwd(mhat, rstd, dh1 * g_ref[...]).astype(BF16)

    return pl.pallas_call(
        body, name="residual_bwd", grid=(s // TS,),
        in_specs=[_row(D), _row(D), _row(D), _vec(1, D)], out_specs=[_row(D), _row(D), _vec(1, D)],
        out_shape=[jax.ShapeDtypeStruct((s, D), F32), jax.ShapeDtypeStruct((s, D), BF16), jax.ShapeDtypeStruct((1, D), F32)],
        compiler_params=_params(dimension_semantics=("arbitrary",)),
    )(dy, t, mix, post_gain)


def _prenorm_bwd(x, dxn_a, dxn_b, dh1, pre_gain):
    s = x.shape[0]

    def body(x_ref, da_ref, db_ref, dh_ref, g_ref, dx_ref, gg_ref):
        @pl.when(pl.program_id(0) == 0)
        def _():
            gg_ref[...] = jnp.zeros_like(gg_ref)

        xhat, rstd = _rms_fwd(x_ref[...], None)
        dxn = da_ref[...] + db_ref[...]
        gg_ref[...] += _colsum(dxn * xhat)
        dx_ref[...] = dh_ref[...] + _rms_bwd(xhat, rstd, dxn * g_ref[...])

    return pl.pallas_call(
        body, name="prenorm_bwd", grid=(s // TS,),
        in_specs=[_row(D), _row(D), _row(D), _row(D), _vec(1, D)], out_specs=[_row(D), _vec(1, D)],
        out_shape=[jax.ShapeDtypeStruct((s, D), F32), jax.ShapeDtypeStruct((1, D), F32)],
        compiler_params=_params(dimension_semantics=("arbitrary",)),
    )(x, dxn_a, dxn_b, dh1, pre_gain)


def _adamw(name, parts, w, m, v):
    r, c = w.shape
    tr = _pick(r, (256, 128, 16, 8)) if r % 8 == 0 else r
    blk = pl.BlockSpec((tr, c), lambda i: (i, 0))

    def body(p_ref, w_ref, m_ref, v_ref, g_ref, d_ref, nm_ref, nv_ref):
        g = p_ref[0].astype(F32)
        for j in range(1, N_DEV):
            g = g + p_ref[j].astype(F32)
        g_ref[...] = g
        nm = ADAM_B1 * m_ref[...] + (1.0 - ADAM_B1) * g
        nv = ADAM_B2 * v_ref[...] + (1.0 - ADAM_B2) * (g * g)
        nm_ref[...] = nm
        nv_ref[...] = nv
        m_hat = nm / (1.0 - ADAM_B1 ** ADAM_STEP)
        v_hat = nv / (1.0 - ADAM_B2 ** ADAM_STEP)
        d_ref[...] = -ADAM_LR * (m_hat / (jnp.sqrt(v_hat) + ADAM_EPS) + ADAM_WD * w_ref[...])

    return pl.pallas_call(
        body, name=name, grid=(r // tr,),
        in_specs=[pl.BlockSpec((N_DEV, tr, c), lambda i: (0, i, 0)), blk, blk, blk],
        out_specs=[blk] * 4,
        out_shape=[jax.ShapeDtypeStruct((r, c), F32)] * 4,
        compiler_params=_params(dimension_semantics=("parallel",)),
    )(parts, w, m, v)


def _spread8(v):
    r = v.shape[0]
    return jnp.pad(jnp.pad(v[:, :, None], ((0, 0), (0, 0), (0, 7))).reshape(r, 8 * NH), ((0, 0), (0, HD - 8 * NH)))


def _gather8(v):
    return v[:, :8 * NH].reshape(v.shape[0], NH, 8)[:, :, 0]


def _cols_to_shards(g):
    r, c8 = g.shape
    return g.reshape(r, N_DEV, c8 // N_DEV).transpose(1, 0, 2)


def _shards_to_cols(g):
    n, r, c = g.shape
    return g.transpose(1, 0, 2).reshape(r, n * c)


def kernel(x, p, w_in, b_f, pre_gain, post_gain, conv_w, conv_b, w_rgate, b_rgate, w_igate, b_igate, lru_lambda, attn_out_gain, lru_out_gain, w_out, w_ple, ple_gain, w_ple_gate, b_ple_gate, loss_target, m_w_in, m_b_f, m_pre_gain, m_post_gain, m_conv_w, m_conv_b, m_w_rgate, m_b_rgate, m_w_igate, m_b_igate, m_lru_lambda, m_attn_out_gain, m_lru_out_gain, m_w_out, m_w_ple, m_ple_gain, m_w_ple_gate, m_b_ple_gate, v_w_in, v_b_f, v_pre_gain, v_post_gain, v_conv_w, v_conv_b, v_w_rgate, v_b_rgate, v_w_igate, v_b_igate, v_lru_lambda, v_attn_out_gain, v_lru_out_gain, v_w_out, v_w_ple, v_ple_gain, v_w_ple_gate, v_b_ple_gate):
    me = 4 * lax.axis_index("x") + 2 * lax.axis_index("y") + lax.axis_index("c")
    x2, p2, tgt = x[0], p[0, 0], loss_target[0]

    conv_w_shard8 = jnp.pad(conv_w[0], ((0, 8 - CONV_W), (0, 0)))
    g_win, g_conv = _exchange("gather_w_in", [w_in[0].astype(BF16), conv_w_shard8], ["bcast"] * 2)
    rest_state, rest_token = _exchange_start(
        "gather_rest_start", [w_out[0].astype(BF16), w_ple[0].astype(BF16), w_ple_gate[0].astype(BF16)], ["bcast"] * 3)
    win_full = _shards_to_cols(g_win)
    w_qkv = win_full[:, :D_QKV]
    w_rest = jnp.concatenate([win_full[:, D_QKV + NH:], _spread8(win_full[:, D_QKV:D_QKV + NH])], axis=1)
    conv_w8 = _shards_to_cols(g_conv)
    bf_pad = _spread8(b_f)
    w_r, w_i = w_rgate[0].astype(BF16), w_igate[0].astype(BF16)

    xn = _prenorm(x2, pre_gain + rest_token[0:1, 0:1])
    zq = _mm("proj_qkv", xn, w_qkv, "nn", BF16)
    zr = _mm("proj_rest", xn, w_rest, "nn", F32)
    ccol, crow = _forget_fwd(zr, bf_pad)
    o, aq = _attn_fwd(zq, ccol, crow)
    xc, h = _lru_fwd(zr, conv_w8, conv_b, w_r, b_rgate, w_i, b_igate, lru_lambda)
    ycat = _branch_out(o, h, zr, attn_out_gain, lru_out_gain)
    g_wout, g_wple, g_wpg = _exchange_wait("gather_rest_wait", rest_state, ycat)
    wout_full = g_wout.reshape(2 * D, D)
    wple_full = _shards_to_cols(g_wple)
    wpg_full = g_wpg.reshape(D, D)
    mix = _mm("proj_out", ycat, wout_full, "nn", F32)
    h1, h1b = _residual(x2, mix, post_gain)
    pe = _mm("proj_ple", p2, wple_full, "nn", F32)
    gp = _mm("proj_gate", h1b, wpg_full, "nn", F32)
    loss_part, dy, dgp, dpe, g_ple_gain, g_b_gate = _head(h1, pe, gp, tgt, ple_gain, b_ple_gate)
    loss = lax.psum(loss_part[0, 0], ("x", "y", "c"))

    t = _mm("bwd_gate_x", dgp, wpg_full, "nt", F32)
    gw_pg = _mm("bwd_gate_w", h1b, dgp, "tn", BF16)
    gw_ple = _mm("bwd_ple_w", p2, dpe, "tn", BF16)
    dh1, dmix, g_post_gain = _residual_bwd(dy, t, mix, post_gain)
    dycat = _mm("bwd_out_x", dmix, wout_full, "nt", F32)
    gw_out = _mm("bwd_out_w", ycat, dmix, "tn", BF16)
    outw_state, outw_token = _exchange_start(
        "exchange_outw_start",
        [gw_out.reshape(N_DEV, 2 * D // N_DEV, D), _cols_to_shards(gw_ple), gw_pg.reshape(N_DEV, D // N_DEV, D)],
        ["scatter"] * 3)
    do, delta, dga, dh, dgl, g_aog, g_log = _branch_out_bwd(
        o, h, zr, dycat, attn_out_gain + outw_token[0:1, 0:1], lru_out_gain)
    dq, dk, dv, dcs, drs = _attn_bwd(zq, do, aq, delta, crow)
    dxl, g_wr, g_wi, g_br, g_bi, g_lam, g_cb, g_cw8 = _lru_bwd(
        zr, xc, h, dh, conv_w8, w_r, b_rgate, w_i, b_igate, lru_lambda)
    dfl, g_bf_pad = _forget_bwd(dcs, drs, zr, bf_pad)
    dzq = jnp.concatenate([dq.astype(BF16), dk, dv], axis=1)
    dzr = jnp.concatenate([dga, dxl, dgl, dfl], axis=1)
    gw_qkv = _mm("bwd_qkv_w", xn, dzq, "tn", BF16)
    gw_rest = _mm("bwd_rest_w", xn, dzr, "tn", BF16)
    gw_in = jnp.concatenate([gw_qkv, _gather8(gw_rest[:, FL_COL:]), gw_rest[:, :FL_COL]], axis=1)
    gates = jnp.concatenate([g_wr.reshape(D, HD), g_wi.reshape(D, HD)], axis=0).astype(BF16)
    inw_state, inw_token = _exchange_start(
        "exchange_inw_start", [_cols_to_shards(gw_in), gates], ["scatter", "bcast"])
    dxn_a = _mm("bwd_qkv_x", dzq, w_qkv, "nt", F32)
    dxn_b = _mm("bwd_rest_x", dzr, w_rest, "nt", F32)
    grad_x, g_pre_gain = _prenorm_bwd(x2, dxn_a, dxn_b, dh1, pre_gain + inw_token[0:1, 0:1])
    small = jnp.concatenate(
        [jnp.pad(_gather8(g_bf_pad), ((0, 0), (0, D - NH))), g_pre_gain, g_post_gain, g_cb, g_br, g_bi, g_lam, g_aog, g_log,
         g_ple_gain, g_b_gate, g_cw8[:CONV_W], jnp.zeros((1, D), F32)], axis=0)
    (r_small,) = _exchange("exchange_small", [small], ["bcast"])
    r_wout, r_wple, r_wpg = _exchange_wait("exchange_outw_wait", outw_state, r_small)
    r_win, r_gates = _exchange_wait("exchange_inw_wait", inw_state, r_small)

    upd = {}
    upd["w_in"] = _adamw("adamw_w_in", r_win, w_in[0], m_w_in[0], v_w_in[0])
    upd["w_out"] = _adamw("adamw_w_out", r_wout, w_out[0], m_w_out[0], v_w_out[0])
    upd["w_ple"] = _adamw("adamw_w_ple", r_wple, w_ple[0], m_w_ple[0], v_w_ple[0])
    upd["w_ple_gate"] = _adamw("adamw_w_ple_gate", r_wpg, w_ple_gate[0], m_w_ple_gate[0], v_w_ple_gate[0])
    gates_of = lambda a, b: jnp.concatenate([a[0].reshape(D, HD), b[0].reshape(D, HD)], axis=0)
    g_gates = _adamw("adamw_gates", r_gates, gates_of(w_rgate, w_igate), gates_of(m_w_rgate, m_w_igate),
                     gates_of(v_w_rgate, v_w_igate))
    upd["w_rgate"] = [a[:D].reshape(1, NH, HD, HD) for a in g_gates]
    upd["w_igate"] = [a[D:].reshape(1, NH, HD, HD) for a in g_gates]
    vec_names = ["b_f", "pre_gain", "post_gain", "conv_b", "b_rgate", "b_igate", "lru_lambda", "attn_out_gain",
                 "lru_out_gain", "ple_gain", "b_ple_gate"]
    vec_w = dict(b_f=(b_f, m_b_f, v_b_f), pre_gain=(pre_gain, m_pre_gain, v_pre_gain),
                 post_gain=(post_gain, m_post_gain, v_post_gain), conv_b=(conv_b, m_conv_b, v_conv_b),
                 b_rgate=(b_rgate, m_b_rgate, v_b_rgate), b_igate=(b_igate, m_b_igate, v_b_igate),
                 lru_lambda=(lru_lambda, m_lru_lambda, v_lru_lambda),
                 attn_out_gain=(attn_out_gain, m_attn_out_gain, v_attn_out_gain),
                 lru_out_gain=(lru_out_gain, m_lru_out_gain, v_lru_out_gain), ple_gain=(ple_gain, m_ple_gain, v_ple_gain),
                 b_ple_gate=(b_ple_gate, m_b_ple_gate, v_b_ple_gate))
    conv_mine = lambda a: lax.dynamic_slice_in_dim(a, me * HD, HD, axis=1)

    def small_rows(k):
        rows = [jnp.pad(vec_w[nm][k], ((0, 0), (0, D - vec_w[nm][k].shape[1]))) for nm in vec_names]
        cw = (conv_w, m_conv_w, v_conv_w)[k][0]
        full = lax.dynamic_update_slice_in_dim(jnp.ones((CONV_W, D), F32), cw, me * HD, axis=1)
        return jnp.concatenate(rows + [full, jnp.ones((1, D), F32)], axis=0)

    g_small = _adamw("adamw_small", r_small, small_rows(0), small_rows(1), small_rows(2))
    for idx, nm in enumerate(vec_names):
        width = vec_w[nm][0].shape[1]
        upd[nm] = [a[idx:idx + 1, :width] for a in g_small]
    base = len(vec_names)
    upd["conv_w"] = [conv_mine(a[base:base + CONV_W])[None] for a in g_small]
    for nm in ("w_in", "w_out", "w_ple", "w_ple_gate"):
        upd[nm] = [a[None] for a in upd[nm]]

    order = ["w_in", "b_f", "pre_gain", "post_gain", "conv_w", "conv_b", "w_rgate", "b_rgate", "w_igate", "b_igate",
             "lru_lambda", "attn_out_gain", "lru_out_gain", "w_out", "w_ple", "ple_gain", "w_ple_gate", "b_ple_gate"]
    outs = [loss, grad_x[None]]
    for k in range(4):
        outs += [upd[nm][k] for nm in order]
    return tuple(outs)
```

```python
import functools

import jax
import jax.numpy as jnp
from jax import lax
from jax.experimental import pallas as pl
from jax.experimental.pallas import tpu as pltpu

F32 = jnp.float32
BF16 = jnp.bfloat16

N_DEV = 8
D = 1024
HD = 128
NH = 8
D_IN = 6152
D_IN_SHARD = D_IN // N_DEV
D_QKV = 3 * D
D_REST = 3 * D + HD
FL_COL = 3 * D
D_PLE = 256
CONV_W = 4
LRU_C = 8.0
RMS_EPS = 1e-6
SCALE = HD ** -0.5
EXP2_SCALE = SCALE * 1.4426950408889634
NEG = -1e30

ADAM_LR = 0.001
ADAM_B1 = 0.9
ADAM_B2 = 0.999
ADAM_EPS = 1e-08
ADAM_WD = 0.01
ADAM_STEP = 10

TS = 256
TQ = 512
VMEM_LIMIT = 48 * 1024 * 1024

NT_DIMS = (((1,), (1,)), ((), ()))
TN_DIMS = (((0,), (0,)), ((), ()))


def _params(**kw):
    return pltpu.CompilerParams(vmem_limit_bytes=VMEM_LIMIT, **kw)


def _sigmoid(v):
    return 1.0 / (1.0 + jnp.exp(-v))


def _rms_fwd(v, gain):
    rstd = lax.rsqrt(jnp.mean(v * v, axis=-1, keepdims=True) + RMS_EPS)
    return v * rstd, rstd


def _rms_bwd(vhat, rstd, dvhat):
    return rstd * (dvhat - vhat * jnp.mean(dvhat * vhat, axis=-1, keepdims=True))


def _colsum(v):
    return jnp.sum(v, axis=0, keepdims=True)


def _rows_iota(t):
    return lax.broadcasted_iota(jnp.int32, (t, 1), 0)


def _scan_fwd(a, u):
    t = a.shape[0]
    rows = _rows_iota(t)
    d = 1
    while d < t:
        valid = rows >= d
        u = jnp.where(valid, u + a * pltpu.roll(u, d, 0), u)
        a = jnp.where(valid, a * pltpu.roll(a, d, 0), a)
        d *= 2
    return a, u


def _scan_bwd(b, g):
    t = b.shape[0]
    rows = _rows_iota(t)
    d = 1
    while d < t:
        valid = rows < t - d
        g = jnp.where(valid, g + b * pltpu.roll(g, t - d, 0), g)
        b = jnp.where(valid, b * pltpu.roll(b, t - d, 0), b)
        d *= 2
    return g


def _cumsum_fwd(v):
    t = v.shape[0]
    rows = _rows_iota(t)
    d = 1
    while d < t:
        v = jnp.where(rows >= d, v + pltpu.roll(v, d, 0), v)
        d *= 2
    return v


def _cumsum_bwd(v):
    t = v.shape[0]
    rows = _rows_iota(t)
    d = 1
    while d < t:
        v = jnp.where(rows < t - d, v + pltpu.roll(v, t - d, 0), v)
        d *= 2
    return v


def _shift_down(ext, k, t):
    return pltpu.roll(ext, k, 0)[8:, :] if k else ext[8:, :]


def _shift_up(ext, k, t):
    return pltpu.roll(ext, t + 8 - k, 0)[:t, :] if k else ext[:t, :]


def _exchange(name, arrs, kinds):
    n = len(arrs)
    out_shape = []
    for a, kind in zip(arrs, kinds):
        shp = a.shape if kind == "scatter" else (N_DEV,) + a.shape
        out_shape.append(jax.ShapeDtypeStruct(shp, a.dtype))

    def body(*refs):
        ins, outs = refs[:n], refs[n:2 * n]
        send_sems, recv_sems, local_sems = refs[2 * n:]
        x, y, c = lax.axis_index("x"), lax.axis_index("y"), lax.axis_index("c")
        me = 4 * x + 2 * y + c
        copies = []
        for i in range(n):
            scatter = kinds[i] == "scatter"
            mine = pltpu.make_async_copy(ins[i].at[me] if scatter else ins[i], outs[i].at[me], local_sems.at[i])
            mine.start()
            copies.append(mine)
            for m in range(1, N_DEV):
                px = 1 - x if m & 4 else x
                py = 1 - y if m & 2 else y
                pc = 1 - c if m & 1 else c
                peer = 4 * px + 2 * py + pc
                cp = pltpu.make_async_remote_copy(
                    src_ref=ins[i].at[peer] if scatter else ins[i],
                    dst_ref=outs[i].at[me],
                    send_sem=send_sems.at[i, m - 1],
                    recv_sem=recv_sems.at[i, m - 1],
                    device_id=(px, py, pc),
                    device_id_type=pl.DeviceIdType.MESH,
                )
                cp.start()
                copies.append(cp)
        for cp in copies:
            cp.wait()

    any_spec = pl.BlockSpec(memory_space=pl.ANY)
    return pl.pallas_call(
        body,
        name=name,
        out_shape=out_shape,
        in_specs=[any_spec] * n,
        out_specs=[any_spec] * n,
        scratch_shapes=[
            pltpu.SemaphoreType.DMA((n, N_DEV - 1)),
            pltpu.SemaphoreType.DMA((n, N_DEV - 1)),
            pltpu.SemaphoreType.DMA((n,)),
        ],
        compiler_params=pltpu.CompilerParams(has_side_effects=True),
    )(*arrs)


def _gather_two_level(name, arrs):
    n = len(arrs)

    def body(*refs):
        ins, outs = refs[:n], refs[n:2 * n]
        send_sems, recv_sems, local_sems = refs[2 * n:]
        x, y, c = lax.axis_index("x"), lax.axis_index("y"), lax.axis_index("c")
        me, sibling = (x, y, c), (x, y, 1 - c)
        chips = [(1 - x, y), (x, 1 - y), (1 - x, 1 - y)]

        def slot(i, dev):
            return outs[i].at[4 * dev[0] + 2 * dev[1] + dev[2]]

        def copy(i, k, block, to, src=None):
            return pltpu.make_async_remote_copy(
                src_ref=slot(i, block) if src is None else src, dst_ref=slot(i, block),
                send_sem=send_sems.at[i, k], recv_sem=recv_sems.at[i, k],
                device_id=to, device_id_type=pl.DeviceIdType.MESH)

        own, sent = [], []
        for i in range(n):
            mine = pltpu.make_async_copy(ins[i], slot(i, me), local_sems.at[i])
            mine.start()
            own.append(mine)
            first = [copy(i, 1 + j, me, (*chip, c), src=ins[i]) for j, chip in enumerate(chips)]
            first.append(copy(i, 0, me, sibling, src=ins[i]))
            for cp in first:
                cp.start()
            sent += first
        for j, chip in enumerate(chips):
            for i in range(n):
                copy(i, 1 + j, (*chip, c), me).wait_recv()
                fwd = copy(i, 4 + j, (*chip, c), sibling)
                fwd.start()
                sent.append(fwd)
        for i in range(n):
            copy(i, 0, sibling, me).wait_recv()
            for j, chip in enumerate(chips):
                copy(i, 4 + j, (*chip, 1 - c), me).wait_recv()
        for cp in sent:
            cp.wait_send()
        for cp in own:
            cp.wait()

    any_spec = pl.BlockSpec(memory_space=pl.ANY)
    return pl.pallas_call(
        body, name=name,
        out_shape=[jax.ShapeDtypeStruct((N_DEV,) + a.shape, a.dtype) for a in arrs],
        in_specs=[any_spec] * n, out_specs=[any_spec] * n,
        scratch_shapes=[pltpu.SemaphoreType.DMA((n, 7)), pltpu.SemaphoreType.DMA((n, 7)), pltpu.SemaphoreType.DMA((n,))],
        compiler_params=pltpu.CompilerParams(has_side_effects=True),
    )(*arrs)


def _peers(x, y, c):
    out = []
    for m in range(1, N_DEV):
        px = 1 - x if m & 4 else x
        py = 1 - y if m & 2 else y
        pc = 1 - c if m & 1 else c
        out.append((m, (px, py, pc), 4 * px + 2 * py + pc))
    return out


def _split_copies(kinds, src_refs, land_refs, send_sems, recv_sems):
    x, y, c = lax.axis_index("x"), lax.axis_index("y"), lax.axis_index("c")
    me = 4 * x + 2 * y + c
    copies = []
    for i, kind in enumerate(kinds):
        for m, peer, pidx in _peers(x, y, c):
            copies.append(pltpu.make_async_remote_copy(
                src_ref=src_refs[i].at[pidx] if kind == "scatter" else src_refs[i],
                dst_ref=land_refs[i].at[me],
                send_sem=send_sems.at[i * (N_DEV - 1) + m - 1],
                recv_sem=recv_sems.at[i * (N_DEV - 1) + m - 1],
                device_id=peer,
                device_id_type=pl.DeviceIdType.MESH,
            ))
    return copies


_HBM_SPEC = pl.BlockSpec(memory_space=pltpu.HBM)
_SEM_SPEC = pl.BlockSpec(memory_space=pltpu.SEMAPHORE)
_DATAFLOW = pltpu.SideEffectType.DATAFLOW_SIDE_EFFECTING


def _exchange_start(name, arrs, kinds):
    n = len(arrs)
    lands = []
    for a, kind in zip(arrs, kinds):
        shp = a.shape if kind == "scatter" else (N_DEV,) + a.shape
        lands.append(lax.empty(shp, a.dtype))

    def body(*refs):
        src_refs, land_refs = refs[:n], refs[n:2 * n]
        send_sems, recv_sems = refs[2 * n:2 * n + 2]
        token = refs[-1]
        for cp in _split_copies(kinds, src_refs, land_refs, send_sems, recv_sems):
            cp.start()
        token[...] = jnp.zeros_like(token)

    n_sem = n * (N_DEV - 1)
    hbm = lambda a: pltpu.HBM(a.shape, a.dtype)
    res = pl.pallas_call(
        body, name=name,
        out_shape=(pltpu.SemaphoreType.DMA((n_sem,)), pltpu.SemaphoreType.DMA((n_sem,)),
                   *[hbm(a) for a in arrs], *[hbm(a) for a in lands], jax.ShapeDtypeStruct((8, HD), F32)),
        in_specs=[_HBM_SPEC] * (2 * n),
        out_specs=(_SEM_SPEC, _SEM_SPEC, *[_HBM_SPEC] * (2 * n), pl.BlockSpec(memory_space=pltpu.VMEM)),
        input_output_aliases={i: 2 + i for i in range(2 * n)},
        compiler_params=pltpu.CompilerParams(has_side_effects=_DATAFLOW),
    )(*[pltpu.with_memory_space_constraint(a, pltpu.HBM) for a in arrs],
      *[pltpu.with_memory_space_constraint(a, pltpu.HBM) for a in lands])
    return (kinds, res[0], res[1], res[2:2 + n], res[2 + n:2 + 2 * n]), res[-1]


def _exchange_wait(name, state, after):
    kinds, send_sems, recv_sems, srcs, lands = state
    n = len(srcs)

    def body(*refs):
        src_refs, land_refs = refs[:n], refs[n:2 * n]
        send_sems_ref, recv_sems_ref = refs[2 * n:2 * n + 2]
        for cp in _split_copies(kinds, src_refs, land_refs, send_sems_ref, recv_sems_ref):
            cp.wait_send()
            cp.wait_recv()

    res = pl.pallas_call(
        body, name=name,
        out_shape=tuple(pltpu.HBM(a.shape, a.dtype) for a in (*srcs, *lands)),
        in_specs=[_HBM_SPEC] * (2 * n) + [_SEM_SPEC, _SEM_SPEC, pl.BlockSpec(memory_space=pl.ANY)],
        out_specs=tuple([_HBM_SPEC] * (2 * n)),
        input_output_aliases={i: i for i in range(2 * n)},
        compiler_params=pltpu.CompilerParams(has_side_effects=_DATAFLOW),
    )(*srcs, *lands, send_sems, recv_sems, after)
    me = 4 * lax.axis_index("x") + 2 * lax.axis_index("y") + lax.axis_index("c")
    outs = []
    for kind, src, land in zip(kinds, res[:n], res[n:]):
        own = lax.dynamic_index_in_dim(src, me, 0, keepdims=False) if kind == "scatter" else src
        outs.append(lax.dynamic_update_index_in_dim(land, own, me, 0))
    return outs


def _pick(n, cands):
    for t in cands:
        if n % t == 0:
            return t
    raise ValueError(f"no tile for {n}")


def _mm(name, a, b, mode, out_dtype, after=None):
    if mode == "nn":
        (m, k), (k2, n) = a.shape, b.shape
    elif mode == "nt":
        (m, k), (n, k2) = a.shape, b.shape
    else:
        (k, m), (k2, n) = a.shape, b.shape
    assert k == k2, (name, a.shape, b.shape)
    tm = _pick(m, (1024, 640, 512, 256) if mode == "tn" else (512, 256))
    tn = _pick(n, (1024, 640, 512, 256, 128))
    tk = _pick(k, (512, 256) if mode == "tn" else (1024, 640, 512, 256))
    nk = k // tk

    def body(a_ref, b_ref, *rest):
        o_ref, acc_ref = rest[-2:]
        kk = pl.program_id(2)
        av = a_ref[...].astype(BF16)
        bv = b_ref[...].astype(BF16)
        if mode == "nn":
            part = jnp.dot(av, bv, preferred_element_type=F32)
        elif mode == "nt":
            part = lax.dot_general(av, bv, NT_DIMS, preferred_element_type=F32)
        else:
            part = lax.dot_general(av, bv, TN_DIMS, preferred_element_type=F32)

        @pl.when(kk == 0)
        def _():
            acc_ref[...] = part

        @pl.when(kk > 0)
        def _():
            acc_ref[...] += part

        @pl.when(kk == nk - 1)
        def _():
            o_ref[...] = acc_ref[...].astype(out_dtype)

    if mode == "tn":
        a_spec = pl.BlockSpec((tk, tm), lambda i, j, kk: (kk, i))
    else:
        a_spec = pl.BlockSpec((tm, tk), lambda i, j, kk: (i, kk))
    if mode == "nt":
        b_spec = pl.BlockSpec((tn, tk), lambda i, j, kk: (j, kk))
    else:
        b_spec = pl.BlockSpec((tk, tn), lambda i, j, kk: (kk, j))
    in_specs, args = [a_spec, b_spec], [a, b]
    if after is not None:
        in_specs.append(pl.BlockSpec((8, HD), lambda i, j, kk: (0, 0)))
        args.append(after)
    return pl.pallas_call(
        body,
        name=name,
        grid=(m // tm, n // tn, nk),
        in_specs=in_specs,
        out_specs=pl.BlockSpec((tm, tn), lambda i, j, kk: (i, j)),
        out_shape=jax.ShapeDtypeStruct((m, n), out_dtype),
        scratch_shapes=[pltpu.VMEM((tm, tn), F32)],
        compiler_params=_params(dimension_semantics=("parallel", "parallel", "arbitrary")),
    )(*args)


def _row(c, col=0):
    return pl.BlockSpec((TS, c), lambda i: (i, col))


def _vec(r, c):
    return pl.BlockSpec((r, c), lambda i: (0, 0))


def _prenorm(x, pre_gain):
    s = x.shape[0]

    def body(x_ref, g_ref, o_ref):
        xhat, _ = _rms_fwd(x_ref[...], g_ref[...])
        o_ref[...] = (xhat * g_ref[...]).astype(BF16)

    return pl.pallas_call(
        body, name="prenorm", grid=(s // TS,),
        in_specs=[_row(D), _vec(1, D)], out_specs=_row(D),
        out_shape=jax.ShapeDtypeStruct((s, D), BF16),
        compiler_params=_params(dimension_semantics=("parallel",)),
    )(x, pre_gain)


def _forget_fwd(zr, bf_pad):
    s = zr.shape[0]
    n = s // TQ

    def body(fl_ref, b_ref, ccol_ref, crow_ref, c_buf, carry):
        i = pl.program_id(0)

        @pl.when(i == 0)
        def _():
            carry[...] = jnp.zeros_like(carry)

        fl = fl_ref[...] + b_ref[...]
        ls = jnp.minimum(fl, 0.0) - jnp.log(1.0 + jnp.exp(-jnp.abs(fl)))
        c_buf[...] = _cumsum_fwd(ls) + carry[0:1, :]
        carry[0:1, :] = c_buf[TQ - 1:TQ, :]
        cv = c_buf[...]
        for h in range(NH):
            ccol_ref[h] = jnp.broadcast_to(cv[:, 8 * h:8 * h + 1], (TQ, HD))
        crow_ref[0] = cv.T

    return pl.pallas_call(
        body, name="forget_fwd", grid=(n,),
        in_specs=[pl.BlockSpec((TQ, HD), lambda i: (i, FL_COL // HD)), _vec(1, HD)],
        out_specs=[pl.BlockSpec((NH, TQ, HD), lambda i: (0, i, 0)), pl.BlockSpec((1, HD, TQ), lambda i: (i, 0, 0))],
        out_shape=[jax.ShapeDtypeStruct((NH, s, HD), F32), jax.ShapeDtypeStruct((n, HD, TQ), F32)],
        scratch_shapes=[pltpu.VMEM((TQ, HD), F32), pltpu.VMEM((8, HD), F32)],
        compiler_params=_params(dimension_semantics=("arbitrary",)),
    )(zr, bf_pad)


def _attn_fwd(zq, ccol, crow):
    s = zq.shape[0]
    n = s // TQ
    nb = TQ // HD

    def body(q_ref, k_ref, v_ref, cc_ref, cr_ref, o_ref, aq_ref):
        i = pl.program_id(1)
        q = q_ref[...]
        cq = cc_ref[0] * (1.0 / SCALE)
        row = lax.broadcasted_iota(jnp.int32, (TQ, HD), 0)
        col = lax.broadcasted_iota(jnp.int32, (TQ, HD), 1)

        def step(j, carry, masked):
            m, l, acc = carry
            rows = pl.ds(pl.multiple_of(j * TQ, TQ), TQ)
            u = lax.dot_general(q, k_ref[rows, :], NT_DIMS, preferred_element_type=F32)
            ck = cr_ref[j][0:1, :] * (1.0 / SCALE)
            us = [u[:, HD * b:HD * (b + 1)] + (cq - ck[:, HD * b:HD * (b + 1)]) for b in range(nb)]
            if masked:
                us = [jnp.where(row >= col + HD * b, us[b], NEG) for b in range(nb)]
            bm = functools.reduce(jnp.maximum, us)
            m_new = jnp.maximum(m, jnp.max(bm, axis=1, keepdims=True))
            alpha = jnp.exp2((m - m_new) * EXP2_SCALE)
            ps = [jnp.exp2((ub - m_new) * EXP2_SCALE) for ub in us]
            l = alpha * l + functools.reduce(jnp.add, ps)
            pr = jnp.concatenate(ps, axis=1).astype(BF16)
            acc = alpha * acc + jnp.dot(pr, v_ref[rows, :], preferred_element_type=F32)
            return m_new, l, acc

        init = (jnp.full((TQ, HD), NEG, F32), jnp.zeros((TQ, HD), F32), jnp.zeros((TQ, HD), F32))
        carry = lax.fori_loop(0, i, lambda j, cr: step(j, cr, False), init)
        m, l, acc = step(i, carry, True)
        l_row = jnp.sum(l, axis=1, keepdims=True)
        o_ref[...] = acc / l_row
        aq_ref[0] = cq - (m + jnp.log(l_row) * (1.0 / SCALE))

    return pl.pallas_call(
        body, name="attn_fwd", grid=(NH, n),
        in_specs=[
            pl.BlockSpec((TQ, HD), lambda h, i: (i, h)),
            pl.BlockSpec((s, HD), lambda h, i: (0, NH + h)),
            pl.BlockSpec((s, HD), lambda h, i: (0, 2 * NH + h)),
            pl.BlockSpec((1, TQ, HD), lambda h, i: (h, i, 0)),
            pl.BlockSpec((n, 8, TQ), lambda h, i: (0, h, 0)),
        ],
        out_specs=[pl.BlockSpec((TQ, HD), lambda h, i: (i, h)), pl.BlockSpec((1, TQ, HD), lambda h, i: (h, i, 0))],
        out_shape=[jax.ShapeDtypeStruct((s, D), F32), jax.ShapeDtypeStruct((NH, s, HD), F32)],
        compiler_params=_params(dimension_semantics=("parallel", "parallel")),
    )(zq, zq, zq, ccol, crow)


def _attn_bwd(zq, do, aq, delta, crow, after):
    s = zq.shape[0]
    n = s // TQ
    nb = TQ // HD

    def body(k_ref, v_ref, q_ref, do_ref, aq_ref, dl_ref, cr_ref, after_ref, dq_ref, dk_ref, dv_ref, dcs_ref, drs_ref):
        j = pl.program_id(1)

        @pl.when(j == 0)
        def _():
            dq_ref[...] = jnp.zeros_like(dq_ref)
            drs_ref[...] = jnp.zeros_like(drs_ref)

        k = k_ref[...]
        v = v_ref[...]
        ck = cr_ref[0][0:1, :] * (1.0 / SCALE)
        row = lax.broadcasted_iota(jnp.int32, (TQ, HD), 0)
        col = lax.broadcasted_iota(jnp.int32, (TQ, HD), 1)

        def step(i, carry, masked):
            dk, dv, dcs = carry
            rows = pl.ds(pl.multiple_of(i * TQ, TQ), TQ)
            q = q_ref[rows, :]
            dout = do_ref[rows, :]
            aqv = aq_ref[0, rows, :]
            dlv = dl_ref[0, rows, :]
            u = lax.dot_general(q, k, NT_DIMS, preferred_element_type=F32)
            dp = lax.dot_general(dout, v, NT_DIMS, preferred_element_type=F32)
            prs, dss = [], []
            for b in range(nb):
                cs = slice(HD * b, HD * (b + 1))
                ub = u[:, cs] + (aqv - ck[:, cs])
                if masked:
                    ub = jnp.where(row >= col + HD * b, ub, NEG)
                pb = jnp.exp2(ub * EXP2_SCALE)
                prs.append(pb)
                dss.append(pb * (dp[:, cs] - dlv))
            drs_ref[0, rows, :] += functools.reduce(jnp.add, dss)
            ds = jnp.concatenate(dss, axis=1)
            dcs = dcs + jnp.sum(ds.reshape(TQ // 8, 8, TQ), axis=0)
            dsb = ds.astype(BF16)
            dv = dv + lax.dot_general(jnp.concatenate(prs, axis=1).astype(BF16), dout, TN_DIMS, preferred_element_type=F32)
            dk = dk + lax.dot_general(dsb, q, TN_DIMS, preferred_element_type=F32)
            dq_ref[rows, :] += jnp.dot(dsb, k, preferred_element_type=F32) * SCALE
            return dk, dv, dcs

        init = (jnp.zeros((TQ, HD), F32), jnp.zeros((TQ, HD), F32), jnp.zeros((8, TQ), F32))
        carry = step(j, init, True)
        dk, dv, dcs = lax.fori_loop(j + 1, n, lambda i, cr: step(i, cr, False), carry)
        dk_ref[...] = (dk * SCALE).astype(BF16)
        dv_ref[...] = dv.astype(BF16)
        dcs_ref[0] = jnp.broadcast_to(_colsum(dcs), (8, TQ))

    return pl.pallas_call(
        body, name="attn_bwd", grid=(NH, n),
        in_specs=[
            pl.BlockSpec((TQ, HD), lambda h, j: (j, NH + h)),
            pl.BlockSpec((TQ, HD), lambda h, j: (j, 2 * NH + h)),
            pl.BlockSpec((s, HD), lambda h, j: (0, h)),
            pl.BlockSpec((s, HD), lambda h, j: (0, h)),
            pl.BlockSpec((1, s, HD), lambda h, j: (h, 0, 0)),
            pl.BlockSpec((1, s, HD), lambda h, j: (h, 0, 0)),
            pl.BlockSpec((1, 8, TQ), lambda h, j: (j, h, 0)),
            pl.BlockSpec((8, HD), lambda h, j: (0, 0)),
        ],
        out_specs=[
            pl.BlockSpec((s, HD), lambda h, j: (0, h)),
            pl.BlockSpec((TQ, HD), lambda h, j: (j, h)),
            pl.BlockSpec((TQ, HD), lambda h, j: (j, h)),
            pl.BlockSpec((1, 8, TQ), lambda h, j: (j, h, 0)),
            pl.BlockSpec((1, s, HD), lambda h, j: (h, 0, 0)),
        ],
        out_shape=[
            jax.ShapeDtypeStruct((s, D), F32),
            jax.ShapeDtypeStruct((s, D), BF16),
            jax.ShapeDtypeStruct((s, D), BF16),
            jax.ShapeDtypeStruct((n, 8 * NH, TQ), F32),
            jax.ShapeDtypeStruct((NH, s, HD), F32),
        ],
        compiler_params=_params(dimension_semantics=("parallel", "arbitrary")),
    )(zq, zq, zq, do, aq, delta, crow, after)


def _forget_bwd(dcs, drs, zr, bf_pad):
    n = dcs.shape[0]
    s = n * TQ

    def body(dcs_ref, drs_ref, fl_ref, b_ref, dfl_ref, gb_ref, buf, carry):
        i = pl.program_id(0)

        @pl.when(i == 0)
        def _():
            carry[...] = jnp.zeros_like(carry)
            gb_ref[...] = jnp.zeros_like(gb_ref)

        dc_t = jnp.concatenate([dcs_ref[0], jnp.zeros((HD - 8 * NH, TQ), F32)], axis=0)
        lane = lax.broadcasted_iota(jnp.int32, (TQ, HD), 1)
        dc = -dc_t.T
        for hh in range(NH):
            dc = dc + jnp.where(lane == 8 * hh, jnp.sum(drs_ref[hh], axis=1, keepdims=True), 0.0)
        buf[...] = _cumsum_bwd(dc) + carry[0:1, :]
        carry[0:1, :] = buf[0:1, :]
        fl = fl_ref[...] + b_ref[...]
        dfl = buf[...] * _sigmoid(-fl)
        dfl_ref[...] = dfl.astype(BF16)
        gb_ref[...] += _colsum(dfl)

    return pl.pallas_call(
        body, name="forget_bwd", grid=(n,),
        in_specs=[
            pl.BlockSpec((1, 8 * NH, TQ), lambda i: (n - 1 - i, 0, 0)),
            pl.BlockSpec((NH, TQ, HD), lambda i: (0, n - 1 - i, 0)),
            pl.BlockSpec((TQ, HD), lambda i: (n - 1 - i, FL_COL // HD)),
            _vec(1, HD),
        ],
        out_specs=[pl.BlockSpec((TQ, HD), lambda i: (n - 1 - i, 0)), _vec(1, HD)],
        out_shape=[jax.ShapeDtypeStruct((s, HD), BF16), jax.ShapeDtypeStruct((1, HD), F32)],
        scratch_shapes=[pltpu.VMEM((TQ, HD), F32), pltpu.VMEM((8, HD), F32)],
        compiler_params=_params(dimension_semantics=("arbitrary",)),
    )(dcs, drs, zr, bf_pad)


def _gates(xc, w_ref, b):
    xb = xc.astype(BF16)
    pre = jnp.concatenate(
        [jnp.dot(xb[:, HD * g:HD * (g + 1)], w_ref[g], preferred_element_type=F32) for g in range(NH)], axis=1)
    return _sigmoid(pre + b)


def _lru_coeffs(r, lam):
    sp = jnp.maximum(-lam, 0.0) + jnp.log(1.0 + jnp.exp(-jnp.abs(lam)))
    log_a = -LRU_C * r * sp
    a = jnp.exp(log_a)
    y = 2.0 * log_a
    em1 = jnp.where(jnp.abs(y) < 0.01, y * (1.0 + y * (0.5 + y * (1.0 / 6.0))), jnp.exp(y) - 1.0)
    return sp, a, jnp.sqrt(-em1)


def _conv_taps(ext, t):
    return [_shift_down(ext, CONV_W - 1 - jj, t) for jj in range(CONV_W)]


def _lru_fwd(zr, conv_w8, conv_b, w_r, b_r, w_i, b_i, lam):
    s = zr.shape[0]
    n = s // TS
    xl_col = 1

    def body(xl_ref, halo_ref, cw_ref, cb_ref, wr_ref, br_ref, wi_ref, bi_ref, lam_ref, xc_ref, h_ref, carry):
        i = pl.program_id(0)

        @pl.when(i == 0)
        def _():
            carry[...] = jnp.zeros_like(carry)

        halo = jnp.where(i == 0, 0.0, halo_ref[...])
        taps = _conv_taps(jnp.concatenate([halo, xl_ref[...]], axis=0), TS)
        xc = cb_ref[...] + sum(cw_ref[jj:jj + 1, :] * taps[jj] for jj in range(CONV_W))
        xc_ref[...] = xc
        r = _gates(xc, wr_ref, br_ref[...])
        ig = _gates(xc, wi_ref, bi_ref[...])
        _, a, gam = _lru_coeffs(r, lam_ref[...])
        a_cum, h_loc = _scan_fwd(a, gam * (ig * xc))
        h_ref[...] = h_loc + a_cum * carry[0:1, :]
        carry[0:1, :] = h_ref[TS - 1:TS, :]

    return pl.pallas_call(
        body, name="lru_fwd", grid=(n,),
        in_specs=[
            _row(D, xl_col),
            pl.BlockSpec((8, D), lambda i: (jnp.maximum(i * (TS // 8) - 1, 0), xl_col)),
            _vec(8, D), _vec(1, D),
            pl.BlockSpec((NH, HD, HD), lambda i: (0, 0, 0)), _vec(1, D),
            pl.BlockSpec((NH, HD, HD), lambda i: (0, 0, 0)), _vec(1, D),
            _vec(1, D),
        ],
        out_specs=[_row(D), _row(D)],
        out_shape=[jax.ShapeDtypeStruct((s, D), F32), jax.ShapeDtypeStruct((s, D), F32)],
        scratch_shapes=[pltpu.VMEM((8, D), F32)],
        compiler_params=_params(dimension_semantics=("arbitrary",)),
    )(zr, zr, conv_w8, conv_b, w_r, b_r, w_i, b_i, lam)


def _lru_bwd(zr, xc, h, dh, conv_w8, w_r, b_r, w_i, b_i, lam):
    s = zr.shape[0]
    n = s // TS
    xl_col = 1

    def rev(i):
        return n - 1 - i

    def body(xl_ref, xlh_ref, xc_ref, h_ref, hh_ref, dh_ref, cw_ref, wr_ref, br_ref, wi_ref, bi_ref, lam_ref,
             dxl_ref, gwr_ref, gwi_ref, gbr_ref, gbi_ref, glam_ref, gcb_ref, gcw_ref, l_buf, dxc_buf, carry_g, carry_dxc):
        i = pl.program_id(0)
        first = rev(i) == 0

        @pl.when(i == 0)
        def _():
            carry_g[...] = jnp.zeros_like(carry_g)
            carry_dxc[...] = jnp.zeros_like(carry_dxc)
            for ref in (gwr_ref, gwi_ref, gbr_ref, gbi_ref, glam_ref, gcb_ref, gcw_ref):
                ref[...] = jnp.zeros_like(ref)

        rows = _rows_iota(TS)
        xc = xc_ref[...]
        lam = lam_ref[...]
        r = _gates(xc, wr_ref, br_ref[...])
        ig = _gates(xc, wi_ref, bi_ref[...])
        sp, a, gam = _lru_coeffs(r, lam)
        g = dh_ref[...] + jnp.where(rows == TS - 1, carry_g[0:1, :], 0.0)
        b = jnp.where(rows == TS - 1, 0.0, pltpu.roll(a, TS - 1, 0))
        l_buf[...] = _scan_bwd(b, g)
        lv = l_buf[...]
        carry_g[0:1, :] = l_buf[0:1, :] * a[0:1, :]
        h_prev_row = jnp.where(first, 0.0, hh_ref[7:8, :])
        h_prev = jnp.where(rows == 0, h_prev_row, pltpu.roll(h_ref[...], 1, 0))
        dgam = lv * ig * xc
        dig = lv * gam * xc
        dxc = lv * gam * ig
        dla = lv * h_prev * a - dgam * (a * a) / gam
        dr = dla * (-LRU_C) * sp
        glam_ref[...] += _colsum(dla * r) * (LRU_C * _sigmoid(-lam))
        dpr = dr * r * (1.0 - r)
        dpi = dig * ig * (1.0 - ig)
        gbr_ref[...] += _colsum(dpr)
        gbi_ref[...] += _colsum(dpi)
        xb = xc.astype(BF16)
        dprb = dpr.astype(BF16)
        dpib = dpi.astype(BF16)
        back = []
        for gi in range(NH):
            cs = slice(HD * gi, HD * (gi + 1))
            gwr_ref[gi] += lax.dot_general(xb[:, cs], dprb[:, cs], TN_DIMS, preferred_element_type=F32)
            gwi_ref[gi] += lax.dot_general(xb[:, cs], dpib[:, cs], TN_DIMS, preferred_element_type=F32)
            back.append(lax.dot_general(dprb[:, cs], wr_ref[gi], NT_DIMS, preferred_element_type=F32)
                        + lax.dot_general(dpib[:, cs], wi_ref[gi], NT_DIMS, preferred_element_type=F32))
        dxc = dxc + jnp.concatenate(back, axis=1)
        dxc_buf[...] = dxc
        gcb_ref[...] += _colsum(dxc)
        halo = jnp.where(first, 0.0, xlh_ref[...])
        taps = _conv_taps(jnp.concatenate([halo, xl_ref[...]], axis=0), TS)
        for jj in range(CONV_W):
            gcw_ref[jj:jj + 1, :] += _colsum(dxc * taps[jj])
        ext = jnp.concatenate([dxc, carry_dxc[...]], axis=0)
        dxl = sum(cw_ref[jj:jj + 1, :] * _shift_up(ext, CONV_W - 1 - jj, TS) for jj in range(CONV_W))
        dxl_ref[...] = dxl.astype(BF16)
        carry_dxc[...] = dxc_buf[0:8, :]

    rowr = lambda c, col=0: pl.BlockSpec((TS, c), lambda i: (rev(i), col))
    halo = lambda col: pl.BlockSpec((8, D), lambda i: (jnp.maximum(rev(i) * (TS // 8) - 1, 0), col))
    gate_w = pl.BlockSpec((NH, HD, HD), lambda i: (0, 0, 0))
    return pl.pallas_call(
        body, name="lru_bwd", grid=(n,),
        in_specs=[rowr(D, xl_col), halo(xl_col), rowr(D), rowr(D), halo(0), rowr(D),
                  _vec(8, D), gate_w, _vec(1, D), gate_w, _vec(1, D), _vec(1, D)],
        out_specs=[rowr(D), gate_w, gate_w, _vec(1, D), _vec(1, D), _vec(1, D), _vec(1, D), _vec(8, D)],
        out_shape=[
            jax.ShapeDtypeStruct((s, D), BF16),
            jax.ShapeDtypeStruct((NH, HD, HD), F32), jax.ShapeDtypeStruct((NH, HD, HD), F32),
            jax.ShapeDtypeStruct((1, D), F32), jax.ShapeDtypeStruct((1, D), F32), jax.ShapeDtypeStruct((1, D), F32),
            jax.ShapeDtypeStruct((1, D), F32), jax.ShapeDtypeStruct((8, D), F32),
        ],
        scratch_shapes=[pltpu.VMEM((TS, D), F32), pltpu.VMEM((TS, D), F32), pltpu.VMEM((8, D), F32), pltpu.VMEM((8, D), F32)],
        compiler_params=_params(dimension_semantics=("arbitrary",)),
    )(zr, zr, xc, h, h, dh, conv_w8, w_r, b_r, w_i, b_i, lam)


def _silu_parts(g):
    sg = _sigmoid(g)
    return g * sg, sg * (1.0 + g * (1.0 - sg))


def _branch_out(o, h, zr, gain_a, gain_l):
    s = o.shape[0]

    def body(o_ref, ga_ref, h_ref, gl_ref, ka_ref, kl_ref, y_ref):
        ohat, _ = _rms_fwd(o_ref[...], None)
        y_ref[:, 0:D] = (ohat * ka_ref[...] * _silu_parts(ga_ref[...])[0]).astype(BF16)
        hhat, _ = _rms_fwd(h_ref[...], None)
        y_ref[:, D:2 * D] = (hhat * kl_ref[...] * _silu_parts(gl_ref[...])[0]).astype(BF16)

    return pl.pallas_call(
        body, name="branch_out", grid=(s // TS,),
        in_specs=[_row(D), _row(D, 0), _row(D), _row(D, 2), _vec(1, D), _vec(1, D)],
        out_specs=_row(2 * D),
        out_shape=jax.ShapeDtypeStruct((s, 2 * D), BF16),
        compiler_params=_params(dimension_semantics=("parallel",)),
    )(o, zr, h, zr, gain_a, gain_l)


def _branch_out_bwd(o, h, zr, dycat, gain_a, gain_l):
    s = o.shape[0]

    def body(o_ref, ga_ref, h_ref, gl_ref, dya_ref, dyl_ref, ka_ref, kl_ref,
             do_ref, dl_ref, dga_ref, dh_ref, dgl_ref, gka_ref, gkl_ref):
        @pl.when(pl.program_id(0) == 0)
        def _():
            gka_ref[...] = jnp.zeros_like(gka_ref)
            gkl_ref[...] = jnp.zeros_like(gkl_ref)

        def one(v, g, dy, gain):
            vhat, rstd = _rms_fwd(v, None)
            sg, dsg = _silu_parts(g)
            dn = dy * sg
            dg = dy * (vhat * gain) * dsg
            return _rms_bwd(vhat, rstd, dn * gain), dg, _colsum(dn * vhat)

        o = o_ref[...]
        dout, dga, gka = one(o, ga_ref[...], dya_ref[...], ka_ref[...])
        do_ref[...] = dout.astype(BF16)
        dga_ref[...] = dga.astype(BF16)
        gka_ref[...] += gka
        prod = dout * o
        for hh in range(NH):
            dl_ref[hh] = jnp.broadcast_to(jnp.sum(prod[:, HD * hh:HD * (hh + 1)], axis=1, keepdims=True), (TS, HD))
        dh, dgl, gkl = one(h_ref[...], gl_ref[...], dyl_ref[...], kl_ref[...])
        dh_ref[...] = dh
        dgl_ref[...] = dgl.astype(BF16)
        gkl_ref[...] += gkl

    return pl.pallas_call(
        body, name="branch_out_bwd", grid=(s // TS,),
        in_specs=[_row(D), _row(D, 0), _row(D), _row(D, 2), _row(D, 0), _row(D, 1), _vec(1, D), _vec(1, D)],
        out_specs=[_row(D), pl.BlockSpec((NH, TS, HD), lambda i: (0, i, 0)), _row(D), _row(D), _row(D), _vec(1, D), _vec(1, D)],
        out_shape=[
            jax.ShapeDtypeStruct((s, D), BF16), jax.ShapeDtypeStruct((NH, s, HD), F32), jax.ShapeDtypeStruct((s, D), BF16),
            jax.ShapeDtypeStruct((s, D), F32), jax.ShapeDtypeStruct((s, D), BF16),
            jax.ShapeDtypeStruct((1, D), F32), jax.ShapeDtypeStruct((1, D), F32),
        ],
        compiler_params=_params(dimension_semantics=("arbitrary",)),
    )(o, zr, h, zr, dycat, dycat, gain_a, gain_l)


def _residual(x, mix, post_gain):
    s = x.shape[0]

    def body(x_ref, m_ref, g_ref, h_ref, hb_ref):
        mhat, _ = _rms_fwd(m_ref[...], None)
        h1 = x_ref[...] + mhat * g_ref[...]
        h_ref[...] = h1
        hb_ref[...] = h1.astype(BF16)

    return pl.pallas_call(
        body, name="residual", grid=(s // TS,),
        in_specs=[_row(D), _row(D), _vec(1, D)], out_specs=[_row(D), _row(D)],
        out_shape=[jax.ShapeDtypeStruct((s, D), F32), jax.ShapeDtypeStruct((s, D), BF16)],
        compiler_params=_params(dimension_semantics=("parallel",)),
    )(x, mix, post_gain)


def _head(h1, pe, gp, tgt, ple_gain, b_gate):
    s = h1.shape[0]

    def body(h_ref, pe_ref, gp_ref, t_ref, kg_ref, b_ref, loss_ref, dy_ref, dgp_ref, dpe_ref, gk_ref, gb_ref):
        @pl.when(pl.program_id(0) == 0)
        def _():
            loss_ref[...] = jnp.zeros_like(loss_ref)
            gk_ref[...] = jnp.zeros_like(gk_ref)
            gb_ref[...] = jnp.zeros_like(gb_ref)

        ehat, rstd = _rms_fwd(pe_ref[...], None)
        e = ehat * kg_ref[...]
        gate = _sigmoid(gp_ref[...] + b_ref[...])
        diff = (h_ref[...] + gate * e) - t_ref[...]
        per_row = jnp.mean(diff * diff, axis=-1, keepdims=True)
        loss_ref[...] += 0.5 * jnp.sum(per_row, axis=0, keepdims=True)
        dy = diff * (1.0 / D)
        dy_ref[...] = dy
        dgp = dy * e * gate * (1.0 - gate)
        dgp_ref[...] = dgp.astype(BF16)
        gb_ref[...] += _colsum(dgp)
        de = dy * gate
        gk_ref[...] += _colsum(de * ehat)
        dpe_ref[...] = _rms_bwd(ehat, rstd, de * kg_ref[...]).astype(BF16)

    return pl.pallas_call(
        body, name="head", grid=(s // TS,),
        in_specs=[_row(D), _row(D), _row(D), _row(D), _vec(1, D), _vec(1, D)],
        out_specs=[_vec(1, 1), _row(D), _row(D), _row(D), _vec(1, D), _vec(1, D)],
        out_shape=[
            jax.ShapeDtypeStruct((1, 1), F32), jax.ShapeDtypeStruct((s, D), F32), jax.ShapeDtypeStruct((s, D), BF16),
            jax.ShapeDtypeStruct((s, D), BF16), jax.ShapeDtypeStruct((1, D), F32), jax.ShapeDtypeStruct((1, D), F32),
        ],
        compiler_params=_params(dimension_semantics=("arbitrary",)),
    )(h1, pe, gp, tgt, ple_gain, b_gate)


def _residual_bwd(dy, t, mix, post_gain):
    s = dy.shape[0]

    def body(dy_ref, t_ref, m_ref, g_ref, dh_ref, dm_ref, gg_ref):
        @pl.when(pl.program_id(0) == 0)
        def _():
            gg_ref[...] = jnp.zeros_like(gg_ref)

        dh1 = dy_ref[...] + t_ref[...]
        dh_ref[...] = dh1
        mhat, rstd = _rms_fwd(m_ref[...], None)
        gg_ref[...] += _colsum(dh1 * mhat)
        dm_ref[...] = _rms_bwd(mhat, rstd, dh1 * g_ref[...]).astype(BF16)

    return pl.pallas_call(
        body, name="residual_bwd", grid=(s // TS,),
        in_specs=[_row(D), _row(D), _row(D), _vec(1, D)], out_specs=[_row(D), _row(D), _vec(1, D)],
        out_shape=[jax.ShapeDtypeStruct((s, D), F32), jax.ShapeDtypeStruct((s, D), BF16), jax.ShapeDtypeStruct((1, D), F32)],
        compiler_params=_params(dimension_semantics=("arbitrary",)),
    )(dy, t, mix, post_gain)


def _prenorm_bwd(x, dxn_a, dxn_b, dh1, pre_gain):
    s = x.shape[0]

    def body(x_ref, da_ref, db_ref, dh_ref, g_ref, dx_ref, gg_ref):
        @pl.when(pl.program_id(0) == 0)
        def _():
            gg_ref[...] = jnp.zeros_like(gg_ref)

        xhat, rstd = _rms_fwd(x_ref[...], None)
        dxn = da_ref[...] + db_ref[...]
        gg_ref[...] += _colsum(dxn * xhat)
        dx_ref[...] = dh_ref[...] + _rms_bwd(xhat, rstd, dxn * g_ref[...])

    return pl.pallas_call(
        body, name="prenorm_bwd", grid=(s // TS,),
        in_specs=[_row(D), _row(D), _row(D), _row(D), _vec(1, D)], out_specs=[_row(D), _vec(1, D)],
        out_shape=[jax.ShapeDtypeStruct((s, D), F32), jax.ShapeDtypeStruct((1, D), F32)],
        compiler_params=_params(dimension_semantics=("arbitrary",)),
    )(x, dxn_a, dxn_b, dh1, pre_gain)


def _adamw(name, parts, w, m, v):
    r, c = w.shape
    if r % 8 == 0:
        tr = _pick(r, (256, 128, 16, 8))
        grid = (r // tr,)
        blk = pl.BlockSpec((tr, c), lambda i: (i, 0))
        parts_blk = pl.BlockSpec((N_DEV, tr, c), lambda i: (0, i, 0))
    else:
        tc = _pick(c, (256, 128))
        grid = (c // tc,)
        blk = pl.BlockSpec((r, tc), lambda i: (0, i))
        parts_blk = pl.BlockSpec((N_DEV, r, tc), lambda i: (0, 0, i))

    def body(p_ref, w_ref, m_ref, v_ref, g_ref, d_ref, nm_ref, nv_ref):
        g = p_ref[0].astype(F32)
        for j in range(1, N_DEV):
            g = g + p_ref[j].astype(F32)
        g_ref[...] = g
        nm = ADAM_B1 * m_ref[...] + (1.0 - ADAM_B1) * g
        nv = ADAM_B2 * v_ref[...] + (1.0 - ADAM_B2) * (g * g)
        nm_ref[...] = nm
        nv_ref[...] = nv
        m_hat = nm / (1.0 - ADAM_B1 ** ADAM_STEP)
        v_hat = nv / (1.0 - ADAM_B2 ** ADAM_STEP)
        d_ref[...] = -ADAM_LR * (m_hat / (jnp.sqrt(v_hat) + ADAM_EPS) + ADAM_WD * w_ref[...])

    return pl.pallas_call(
        body, name=name, grid=grid,
        in_specs=[parts_blk, blk, blk, blk],
        out_specs=[blk] * 4,
        out_shape=[jax.ShapeDtypeStruct((r, c), F32)] * 4,
        compiler_params=_params(dimension_semantics=("parallel",)),
    )(parts, w, m, v)


def _spread8(v):
    r = v.shape[0]
    return jnp.pad(jnp.pad(v[:, :, None], ((0, 0), (0, 0), (0, 7))).reshape(r, 8 * NH), ((0, 0), (0, HD - 8 * NH)))


def _gather8(v):
    return v[:, :8 * NH].reshape(v.shape[0], NH, 8)[:, :, 0]


def _cols_to_shards(g):
    r, c8 = g.shape
    return g.reshape(r, N_DEV, c8 // N_DEV).transpose(1, 0, 2)


def _shards_to_cols(g):
    n, r, c = g.shape
    return g.transpose(1, 0, 2).reshape(r, n * c)


def kernel(x, p, w_in, b_f, pre_gain, post_gain, conv_w, conv_b, w_rgate, b_rgate, w_igate, b_igate, lru_lambda, attn_out_gain, lru_out_gain, w_out, w_ple, ple_gain, w_ple_gate, b_ple_gate, loss_target, m_w_in, m_b_f, m_pre_gain, m_post_gain, m_conv_w, m_conv_b, m_w_rgate, m_b_rgate, m_w_igate, m_b_igate, m_lru_lambda, m_attn_out_gain, m_lru_out_gain, m_w_out, m_w_ple, m_ple_gain, m_w_ple_gate, m_b_ple_gate, v_w_in, v_b_f, v_pre_gain, v_post_gain, v_conv_w, v_conv_b, v_w_rgate, v_b_rgate, v_w_igate, v_b_igate, v_lru_lambda, v_attn_out_gain, v_lru_out_gain, v_w_out, v_w_ple, v_ple_gain, v_w_ple_gate, v_b_ple_gate):
    me = 4 * lax.axis_index("x") + 2 * lax.axis_index("y") + lax.axis_index("c")
    x2, p2, tgt = x[0], p[0, 0], loss_target[0]

    conv_w_shard8 = jnp.pad(conv_w[0], ((0, 8 - CONV_W), (0, 0)))
    wt, m_wt, v_wt = w_in[0].T, m_w_in[0].T, v_w_in[0].T
    g_wint, g_conv = _gather_two_level("gather_w_in", [wt.astype(BF16), conv_w_shard8])
    rest_state, rest_token = _exchange_start(
        "gather_rest_start", [w_out[0].astype(BF16), w_ple[0].astype(BF16), w_ple_gate[0].astype(BF16)], ["bcast"] * 3)
    win_t = g_wint.reshape(D_IN, D)
    w_qkv_t = win_t[:D_QKV]
    w_rest_t = jnp.concatenate([win_t[D_QKV + NH:], _spread8(win_t[D_QKV:D_QKV + NH].T).T], axis=0)
    conv_w8 = _shards_to_cols(g_conv)
    bf_pad = _spread8(b_f)
    w_r, w_i = w_rgate[0].astype(BF16), w_igate[0].astype(BF16)

    xn = _prenorm(x2, pre_gain + rest_token[0:1, 0:1])
    zq = _mm("proj_qkv", xn, w_qkv_t, "nt", BF16)
    zr = _mm("proj_rest", xn, w_rest_t, "nt", F32)
    ccol, crow = _forget_fwd(zr, bf_pad)
    o, aq = _attn_fwd(zq, ccol, crow)
    xc, h = _lru_fwd(zr, conv_w8, conv_b, w_r, b_rgate, w_i, b_igate, lru_lambda)
    ycat = _branch_out(o, h, zr, attn_out_gain, lru_out_gain)
    g_wout, g_wple, g_wpg = _exchange_wait("gather_rest_wait", rest_state, ycat)
    wout_full = g_wout.reshape(2 * D, D)
    wple_full = _shards_to_cols(g_wple)
    wpg_full = g_wpg.reshape(D, D)
    mix = _mm("proj_out", ycat, wout_full, "nn", F32)
    h1, h1b = _residual(x2, mix, post_gain)
    pe = _mm("proj_ple", p2, wple_full, "nn", F32)
    gp = _mm("proj_gate", h1b, wpg_full, "nn", F32)
    loss_part, dy, dgp, dpe, g_ple_gain, g_b_gate = _head(h1, pe, gp, tgt, ple_gain, b_ple_gate)
    loss = lax.psum(loss_part[0, 0], ("x", "y", "c"))

    t = _mm("bwd_gate_x", dgp, wpg_full, "nt", F32)
    gw_pg = _mm("bwd_gate_w", h1b, dgp, "tn", BF16)
    gw_ple = _mm("bwd_ple_w", p2, dpe, "tn", BF16)
    dh1, dmix, g_post_gain = _residual_bwd(dy, t, mix, post_gain)
    dycat = _mm("bwd_out_x", dmix, wout_full, "nt", F32)
    gw_out = _mm("bwd_out_w", ycat, dmix, "tn", BF16)
    do, delta, dga, dh, dgl, g_aog, g_log = _branch_out_bwd(o, h, zr, dycat, attn_out_gain, lru_out_gain)
    dxl, g_wr, g_wi, g_br, g_bi, g_lam, g_cb, g_cw8 = _lru_bwd(
        zr, xc, h, dh, conv_w8, w_r, b_rgate, w_i, b_igate, lru_lambda)
    gates = jnp.concatenate([g_wr.reshape(D, HD), g_wi.reshape(D, HD)], axis=0).astype(BF16)
    outw_state, outw_token = _exchange_start(
        "exchange_outw_start",
        [gw_out.reshape(N_DEV, 2 * D // N_DEV, D), _cols_to_shards(gw_ple), gw_pg.reshape(N_DEV, D // N_DEV, D), gates],
        ["scatter"] * 3 + ["bcast"])
    dq, dk, dv, dcs, drs = _attn_bwd(zq, do, aq, delta, crow, outw_token)
    dfl, g_bf_pad = _forget_bwd(dcs, drs, zr, bf_pad)
    dzq = jnp.concatenate([dq.astype(BF16), dk, dv], axis=1)
    dzr = jnp.concatenate([dga, dxl, dgl, dfl], axis=1)
    gw_qkv_t = _mm("bwd_qkv_w", dzq, xn, "tn", BF16)
    gw_rest_t = _mm("bwd_rest_w", dzr, xn, "tn", BF16)
    gw_in_t = jnp.concatenate([gw_qkv_t, _gather8(gw_rest_t[FL_COL:].T).T, gw_rest_t[:FL_COL]], axis=0)
    inw_state, inw_token = _exchange_start(
        "exchange_inw_start", [gw_in_t.reshape(N_DEV, D_IN_SHARD, D)], ["scatter"])
    dxn_a = _mm("bwd_qkv_x", dzq, w_qkv_t, "nn", F32, after=inw_token)
    dxn_b = _mm("bwd_rest_x", dzr, w_rest_t, "nn", F32, after=inw_token)
    grad_x, g_pre_gain = _prenorm_bwd(x2, dxn_a, dxn_b, dh1, pre_gain)

    upd = {}
    r_wout, r_wple, r_wpg, r_gates = _exchange_wait("exchange_outw_wait", outw_state, grad_x)
    upd["w_out"] = _adamw("adamw_w_out", r_wout, w_out[0], m_w_out[0], v_w_out[0])
    upd["w_ple"] = _adamw("adamw_w_ple", r_wple, w_ple[0], m_w_ple[0], v_w_ple[0])
    upd["w_ple_gate"] = _adamw("adamw_w_ple_gate", r_wpg, w_ple_gate[0], m_w_ple_gate[0], v_w_ple_gate[0])
    gates_of = lambda a, b: jnp.concatenate([a[0].reshape(D, HD), b[0].reshape(D, HD)], axis=0)
    g_gates = _adamw("adamw_gates", r_gates, gates_of(w_rgate, w_igate), gates_of(m_w_rgate, m_w_igate),
                     gates_of(v_w_rgate, v_w_igate))
    upd["w_rgate"] = [a[:D].reshape(1, NH, HD, HD) for a in g_gates]
    upd["w_igate"] = [a[D:].reshape(1, NH, HD, HD) for a in g_gates]
    small = jnp.concatenate(
        [jnp.pad(_gather8(g_bf_pad), ((0, 0), (0, D - NH))), g_pre_gain, g_post_gain, g_cb, g_br, g_bi, g_lam, g_aog, g_log,
         g_ple_gain, g_b_gate, g_cw8[:CONV_W], upd["w_out"][0][0:1]], axis=0)
    (r_small,) = _exchange("exchange_small", [small], ["bcast"])
    vec_names = ["b_f", "pre_gain", "post_gain", "conv_b", "b_rgate", "b_igate", "lru_lambda", "attn_out_gain",
                 "lru_out_gain", "ple_gain", "b_ple_gate"]
    vec_w = dict(b_f=(b_f, m_b_f, v_b_f), pre_gain=(pre_gain, m_pre_gain, v_pre_gain),
                 post_gain=(post_gain, m_post_gain, v_post_gain), conv_b=(conv_b, m_conv_b, v_conv_b),
                 b_rgate=(b_rgate, m_b_rgate, v_b_rgate), b_igate=(b_igate, m_b_igate, v_b_igate),
                 lru_lambda=(lru_lambda, m_lru_lambda, v_lru_lambda),
                 attn_out_gain=(attn_out_gain, m_attn_out_gain, v_attn_out_gain),
                 lru_out_gain=(lru_out_gain, m_lru_out_gain, v_lru_out_gain), ple_gain=(ple_gain, m_ple_gain, v_ple_gain),
                 b_ple_gate=(b_ple_gate, m_b_ple_gate, v_b_ple_gate))
    conv_mine = lambda a: lax.dynamic_slice_in_dim(a, me * HD, HD, axis=1)

    def small_rows(k):
        rows = [jnp.pad(vec_w[nm][k], ((0, 0), (0, D - vec_w[nm][k].shape[1]))) for nm in vec_names]
        cw = (conv_w, m_conv_w, v_conv_w)[k][0]
        full = lax.dynamic_update_slice_in_dim(jnp.ones((CONV_W, D), F32), cw, me * HD, axis=1)
        return jnp.concatenate(rows + [full, jnp.ones((1, D), F32)], axis=0)

    g_small = _adamw("adamw_small", r_small, small_rows(0), small_rows(1), small_rows(2))
    for idx, nm in enumerate(vec_names):
        width = vec_w[nm][0].shape[1]
        upd[nm] = [a[idx:idx + 1, :width] for a in g_small]
    base = len(vec_names)
    upd["conv_w"] = [conv_mine(a[base:base + CONV_W])[None] for a in g_small]
    (r_win,) = _exchange_wait("exchange_inw_wait", inw_state, g_small[0])
    upd["w_in"] = [a.T for a in _adamw("adamw_w_in", r_win, wt, m_wt, v_wt)]
    for nm in ("w_in", "w_out", "w_ple", "w_ple_gate"):
        upd[nm] = [a[None] for a in upd[nm]]

    order = ["w_in", "b_f", "pre_gain", "post_gain", "conv_w", "conv_b", "w_rgate", "b_rgate", "w_igate", "b_igate",
             "lru_lambda", "attn_out_gain", "lru_out_gain", "w_out", "w_ple", "ple_gain", "w_ple_gate", "b_ple_gate"]
    outs = [loss, grad_x[None]]
    for k in range(4):
        outs += [upd[nm][k] for nm in order]
    return tuple(outs)
```

```python
import functools

import jax
import jax.numpy as jnp
from jax import lax
from jax.experimental import pallas as pl
from jax.experimental.pallas import tpu as pltpu

F32 = jnp.float32
BF16 = jnp.bfloat16

N_DEV = 8
D = 1024
HD = 128
NH = 8
D_IN = 6152
D_IN_SHARD = D_IN // N_DEV
D_QKV = 3 * D
D_REST = 3 * D + HD
FL_COL = 3 * D
D_PLE = 256
CONV_W = 4
LRU_C = 8.0
RMS_EPS = 1e-6
SCALE = HD ** -0.5
EXP2_SCALE = SCALE * 1.4426950408889634
NEG = -1e30

ADAM_LR = 0.001
ADAM_B1 = 0.9
ADAM_B2 = 0.999
ADAM_EPS = 1e-08
ADAM_WD = 0.01
ADAM_STEP = 10

TS = 256
TQ = 1024
VMEM_LIMIT = 48 * 1024 * 1024

NT_DIMS = (((1,), (1,)), ((), ()))
TN_DIMS = (((0,), (0,)), ((), ()))


def _params(**kw):
    return pltpu.CompilerParams(vmem_limit_bytes=VMEM_LIMIT, **kw)


def _sigmoid(v):
    return 1.0 / (1.0 + jnp.exp(-v))


def _rms_fwd(v, gain):
    rstd = lax.rsqrt(jnp.mean(v * v, axis=-1, keepdims=True) + RMS_EPS)
    return v * rstd, rstd


def _rms_bwd(vhat, rstd, dvhat):
    return rstd * (dvhat - vhat * jnp.mean(dvhat * vhat, axis=-1, keepdims=True))


def _colsum(v):
    return jnp.sum(v, axis=0, keepdims=True)


def _rows_iota(t):
    return lax.broadcasted_iota(jnp.int32, (t, 1), 0)


def _scan_fwd(a, u):
    t = a.shape[0]
    rows = _rows_iota(t)
    d = 1
    while d < t:
        valid = rows >= d
        u = jnp.where(valid, u + a * pltpu.roll(u, d, 0), u)
        a = jnp.where(valid, a * pltpu.roll(a, d, 0), a)
        d *= 2
    return a, u


def _scan_bwd(b, g):
    t = b.shape[0]
    rows = _rows_iota(t)
    d = 1
    while d < t:
        valid = rows < t - d
        g = jnp.where(valid, g + b * pltpu.roll(g, t - d, 0), g)
        b = jnp.where(valid, b * pltpu.roll(b, t - d, 0), b)
        d *= 2
    return g


def _cumsum_fwd(v):
    t = v.shape[0]
    rows = _rows_iota(t)
    d = 1
    while d < t:
        v = jnp.where(rows >= d, v + pltpu.roll(v, d, 0), v)
        d *= 2
    return v


def _cumsum_bwd(v):
    t = v.shape[0]
    rows = _rows_iota(t)
    d = 1
    while d < t:
        v = jnp.where(rows < t - d, v + pltpu.roll(v, t - d, 0), v)
        d *= 2
    return v


def _bias_lanes(v, at, ones_at):
    lane = lax.broadcasted_iota(jnp.int32, v.shape, 1)
    hi = v.astype(BF16).astype(F32)
    mid = (v - hi).astype(BF16).astype(F32)
    lo = ((v - hi) - mid).astype(BF16).astype(F32)
    out = jnp.where((lane >= ones_at) & (lane < ones_at + 3), 1.0, 0.0)
    for k, piece in enumerate((hi, mid, lo)):
        out = jnp.where(lane == at + k, piece, out)
    return out.astype(BF16)


def _shift_down(ext, k, t):
    return pltpu.roll(ext, k, 0)[8:, :] if k else ext[8:, :]


def _shift_up(ext, k, t):
    return pltpu.roll(ext, t + 8 - k, 0)[:t, :] if k else ext[:t, :]


def _exchange(name, arrs, kinds):
    n = len(arrs)
    out_shape = []
    for a, kind in zip(arrs, kinds):
        shp = a.shape if kind == "scatter" else (N_DEV,) + a.shape
        out_shape.append(jax.ShapeDtypeStruct(shp, a.dtype))

    def body(*refs):
        ins, outs = refs[:n], refs[n:2 * n]
        send_sems, recv_sems, local_sems = refs[2 * n:]
        x, y, c = lax.axis_index("x"), lax.axis_index("y"), lax.axis_index("c")
        me = 4 * x + 2 * y + c
        copies = []
        for i in range(n):
            scatter = kinds[i] == "scatter"
            mine = pltpu.make_async_copy(ins[i].at[me] if scatter else ins[i], outs[i].at[me], local_sems.at[i])
            mine.start()
            copies.append(mine)
            for m in range(1, N_DEV):
                px = 1 - x if m & 4 else x
                py = 1 - y if m & 2 else y
                pc = 1 - c if m & 1 else c
                peer = 4 * px + 2 * py + pc
                cp = pltpu.make_async_remote_copy(
                    src_ref=ins[i].at[peer] if scatter else ins[i],
                    dst_ref=outs[i].at[me],
                    send_sem=send_sems.at[i, m - 1],
                    recv_sem=recv_sems.at[i, m - 1],
                    device_id=(px, py, pc),
                    device_id_type=pl.DeviceIdType.MESH,
                )
                cp.start()
                copies.append(cp)
        for cp in copies:
            cp.wait()

    any_spec = pl.BlockSpec(memory_space=pl.ANY)
    return pl.pallas_call(
        body,
        name=name,
        out_shape=out_shape,
        in_specs=[any_spec] * n,
        out_specs=[any_spec] * n,
        scratch_shapes=[
            pltpu.SemaphoreType.DMA((n, N_DEV - 1)),
            pltpu.SemaphoreType.DMA((n, N_DEV - 1)),
            pltpu.SemaphoreType.DMA((n,)),
        ],
        compiler_params=pltpu.CompilerParams(has_side_effects=True),
    )(*arrs)


def _gather_two_level(name, arrs):
    n = len(arrs)

    def body(*refs):
        ins, outs = refs[:n], refs[n:2 * n]
        send_sems, recv_sems, local_sems = refs[2 * n:]
        x, y, c = lax.axis_index("x"), lax.axis_index("y"), lax.axis_index("c")
        me, sibling = (x, y, c), (x, y, 1 - c)
        chips = [(1 - x, y), (x, 1 - y), (1 - x, 1 - y)]

        def slot(i, dev):
            return outs[i].at[4 * dev[0] + 2 * dev[1] + dev[2]]

        def copy(i, k, block, to, src=None):
            return pltpu.make_async_remote_copy(
                src_ref=slot(i, block) if src is None else src, dst_ref=slot(i, block),
                send_sem=send_sems.at[i, k], recv_sem=recv_sems.at[i, k],
                device_id=to, device_id_type=pl.DeviceIdType.MESH)

        own, sent = [], []
        for i in range(n):
            mine = pltpu.make_async_copy(ins[i], slot(i, me), local_sems.at[i])
            mine.start()
            own.append(mine)
            first = [copy(i, 1 + j, me, (*chip, c), src=ins[i]) for j, chip in enumerate(chips)]
            first.append(copy(i, 0, me, sibling, src=ins[i]))
            for cp in first:
                cp.start()
            sent += first
        for j, chip in enumerate(chips):
            for i in range(n):
                copy(i, 1 + j, (*chip, c), me).wait_recv()
                fwd = copy(i, 4 + j, (*chip, c), sibling)
                fwd.start()
                sent.append(fwd)
        for i in range(n):
            copy(i, 0, sibling, me).wait_recv()
            for j, chip in enumerate(chips):
                copy(i, 4 + j, (*chip, 1 - c), me).wait_recv()
        for cp in sent:
            cp.wait_send()
        for cp in own:
            cp.wait()

    any_spec = pl.BlockSpec(memory_space=pl.ANY)
    return pl.pallas_call(
        body, name=name,
        out_shape=[jax.ShapeDtypeStruct((N_DEV,) + a.shape, a.dtype) for a in arrs],
        in_specs=[any_spec] * n, out_specs=[any_spec] * n,
        scratch_shapes=[pltpu.SemaphoreType.DMA((n, 7)), pltpu.SemaphoreType.DMA((n, 7)), pltpu.SemaphoreType.DMA((n,))],
        compiler_params=pltpu.CompilerParams(has_side_effects=True),
    )(*arrs)


def _peers(x, y, c):
    out = []
    for m in range(1, N_DEV):
        px = 1 - x if m & 4 else x
        py = 1 - y if m & 2 else y
        pc = 1 - c if m & 1 else c
        out.append((m, (px, py, pc), 4 * px + 2 * py + pc))
    return out


def _split_copies(kinds, src_refs, land_refs, send_sems, recv_sems):
    x, y, c = lax.axis_index("x"), lax.axis_index("y"), lax.axis_index("c")
    me = 4 * x + 2 * y + c
    copies = []
    for i, kind in enumerate(kinds):
        for m, peer, pidx in _peers(x, y, c):
            copies.append(pltpu.make_async_remote_copy(
                src_ref=src_refs[i].at[pidx] if kind == "scatter" else src_refs[i],
                dst_ref=land_refs[i].at[me],
                send_sem=send_sems.at[i * (N_DEV - 1) + m - 1],
                recv_sem=recv_sems.at[i * (N_DEV - 1) + m - 1],
                device_id=peer,
                device_id_type=pl.DeviceIdType.MESH,
            ))
    return copies


_HBM_SPEC = pl.BlockSpec(memory_space=pltpu.HBM)
_SEM_SPEC = pl.BlockSpec(memory_space=pltpu.SEMAPHORE)
_DATAFLOW = pltpu.SideEffectType.DATAFLOW_SIDE_EFFECTING


def _exchange_start(name, arrs, kinds):
    n = len(arrs)
    lands = []
    for a, kind in zip(arrs, kinds):
        shp = a.shape if kind == "scatter" else (N_DEV,) + a.shape
        lands.append(lax.empty(shp, a.dtype))

    def body(*refs):
        src_refs, land_refs = refs[:n], refs[n:2 * n]
        send_sems, recv_sems = refs[2 * n:2 * n + 2]
        token = refs[-1]
        for cp in _split_copies(kinds, src_refs, land_refs, send_sems, recv_sems):
            cp.start()
        token[...] = jnp.zeros_like(token)

    n_sem = n * (N_DEV - 1)
    hbm = lambda a: pltpu.HBM(a.shape, a.dtype)
    res = pl.pallas_call(
        body, name=name,
        out_shape=(pltpu.SemaphoreType.DMA((n_sem,)), pltpu.SemaphoreType.DMA((n_sem,)),
                   *[hbm(a) for a in arrs], *[hbm(a) for a in lands], jax.ShapeDtypeStruct((8, HD), F32)),
        in_specs=[_HBM_SPEC] * (2 * n),
        out_specs=(_SEM_SPEC, _SEM_SPEC, *[_HBM_SPEC] * (2 * n), pl.BlockSpec(memory_space=pltpu.VMEM)),
        input_output_aliases={i: 2 + i for i in range(2 * n)},
        compiler_params=pltpu.CompilerParams(has_side_effects=_DATAFLOW),
    )(*[pltpu.with_memory_space_constraint(a, pltpu.HBM) for a in arrs],
      *[pltpu.with_memory_space_constraint(a, pltpu.HBM) for a in lands])
    return (kinds, res[0], res[1], res[2:2 + n], res[2 + n:2 + 2 * n]), res[-1]


def _exchange_wait(name, state, after):
    kinds, send_sems, recv_sems, srcs, lands = state
    n = len(srcs)

    def body(*refs):
        src_refs, land_refs = refs[:n], refs[n:2 * n]
        send_sems_ref, recv_sems_ref = refs[2 * n:2 * n + 2]
        for cp in _split_copies(kinds, src_refs, land_refs, send_sems_ref, recv_sems_ref):
            cp.wait_send()
            cp.wait_recv()

    res = pl.pallas_call(
        body, name=name,
        out_shape=tuple(pltpu.HBM(a.shape, a.dtype) for a in (*srcs, *lands)),
        in_specs=[_HBM_SPEC] * (2 * n) + [_SEM_SPEC, _SEM_SPEC, pl.BlockSpec(memory_space=pl.ANY)],
        out_specs=tuple([_HBM_SPEC] * (2 * n)),
        input_output_aliases={i: i for i in range(2 * n)},
        compiler_params=pltpu.CompilerParams(has_side_effects=_DATAFLOW),
    )(*srcs, *lands, send_sems, recv_sems, after)
    me = 4 * lax.axis_index("x") + 2 * lax.axis_index("y") + lax.axis_index("c")
    outs = []
    for kind, src, land in zip(kinds, res[:n], res[n:]):
        own = lax.dynamic_index_in_dim(src, me, 0, keepdims=False) if kind == "scatter" else src
        outs.append(lax.dynamic_update_index_in_dim(land, own, me, 0))
    return outs


def _pick(n, cands):
    for t in cands:
        if n % t == 0:
            return t
    raise ValueError(f"no tile for {n}")


def _mm(name, a, b, mode, out_dtype, after=None):
    if mode == "nn":
        (m, k), (k2, n) = a.shape, b.shape
    elif mode == "nt":
        (m, k), (n, k2) = a.shape, b.shape
    else:
        (k, m), (k2, n) = a.shape, b.shape
    assert k == k2, (name, a.shape, b.shape)
    if mode == "tn":
        tm = _pick(m, (1024, 640, 512, 256))
        tn = _pick(n, (1024, 640, 512, 256, 128))
        tk = _pick(k, (2048, 1024, 512, 256))
    else:
        tm, tn, tk = _pick(m, (512, 256)), n, k
    nk = k // tk

    def body(a_ref, b_ref, *rest):
        o_ref = rest[-2] if nk > 1 else rest[-1]
        av = a_ref[...].astype(BF16)
        bv = b_ref[...].astype(BF16)
        if mode == "nn":
            part = jnp.dot(av, bv, preferred_element_type=F32)
        elif mode == "nt":
            part = lax.dot_general(av, bv, NT_DIMS, preferred_element_type=F32)
        else:
            part = lax.dot_general(av, bv, TN_DIMS, preferred_element_type=F32)
        if nk == 1:
            o_ref[...] = part.astype(out_dtype)
            return
        acc_ref = rest[-1]
        kk = pl.program_id(2)

        @pl.when(kk == 0)
        def _():
            acc_ref[...] = part

        @pl.when(kk > 0)
        def _():
            acc_ref[...] += part

        @pl.when(kk == nk - 1)
        def _():
            o_ref[...] = acc_ref[...].astype(out_dtype)

    if mode == "tn":
        a_spec = pl.BlockSpec((tk, tm), lambda j, i, kk: (kk, i))
    else:
        a_spec = pl.BlockSpec((tm, tk), lambda j, i, kk: (i, kk))
    if mode == "nt":
        b_spec = pl.BlockSpec((tn, tk), lambda j, i, kk: (j, kk))
    else:
        b_spec = pl.BlockSpec((tk, tn), lambda j, i, kk: (kk, j))
    in_specs, args = [a_spec, b_spec], [a, b]
    if after is not None:
        in_specs.append(pl.BlockSpec((8, HD), lambda j, i, kk: (0, 0)))
        args.append(after)
    return pl.pallas_call(
        body,
        name=name,
        grid=(n // tn, m // tm, nk),
        in_specs=in_specs,
        out_specs=pl.BlockSpec((tm, tn), lambda j, i, kk: (i, j)),
        out_shape=jax.ShapeDtypeStruct((m, n), out_dtype),
        scratch_shapes=[pltpu.VMEM((tm, tn), F32)] if nk > 1 else [],
        compiler_params=_params(dimension_semantics=("parallel", "parallel", "arbitrary")),
    )(*args)


def _row(c, col=0):
    return pl.BlockSpec((TS, c), lambda i: (i, col))


def _vec(r, c):
    return pl.BlockSpec((r, c), lambda i: (0, 0))


def _prenorm(x, pre_gain):
    s = x.shape[0]

    def body(x_ref, g_ref, o_ref):
        xhat, _ = _rms_fwd(x_ref[...], g_ref[...])
        o_ref[...] = (xhat * g_ref[...]).astype(BF16)

    return pl.pallas_call(
        body, name="prenorm", grid=(s // TS,),
        in_specs=[_row(D), _vec(1, D)], out_specs=_row(D),
        out_shape=jax.ShapeDtypeStruct((s, D), BF16),
        compiler_params=_params(dimension_semantics=("parallel",)),
    )(x, pre_gain)


def _forget_fwd(zr, bf_pad):
    s = zr.shape[0]
    n = s // TQ

    def body(fl_ref, b_ref, kx_ref, c_buf, carry):
        i = pl.program_id(0)

        @pl.when(i == 0)
        def _():
            carry[...] = jnp.zeros_like(carry)

        fl = fl_ref[...] + b_ref[...]
        ls = jnp.minimum(fl, 0.0) - jnp.log(1.0 + jnp.exp(-jnp.abs(fl)))
        c_buf[...] = _cumsum_fwd(ls) + carry[0:1, :]
        carry[0:1, :] = c_buf[TQ - 1:TQ, :]
        cv = c_buf[...]
        for h in range(NH):
            kx_ref[h] = _bias_lanes(jnp.broadcast_to(cv[:, 8 * h:8 * h + 1], (TQ, HD)) * (-1.0 / SCALE), 0, 3)

    return pl.pallas_call(
        body, name="forget_fwd", grid=(n,),
        in_specs=[pl.BlockSpec((TQ, HD), lambda i: (i, FL_COL // HD)), _vec(1, HD)],
        out_specs=pl.BlockSpec((NH, TQ, HD), lambda i: (0, i, 0)),
        out_shape=jax.ShapeDtypeStruct((NH, s, HD), BF16),
        scratch_shapes=[pltpu.VMEM((TQ, HD), F32), pltpu.VMEM((8, HD), F32)],
        compiler_params=_params(dimension_semantics=("arbitrary",)),
    )(zr, bf_pad)


def _attn_fwd(zq, kx):
    s = zq.shape[0]
    n = s // TQ
    nb = TQ // HD

    def body(q_ref, k_ref, v_ref, kx_ref, o_ref, ax_ref):
        i = pl.program_id(1)
        lane = lax.broadcasted_iota(jnp.int32, (TQ, HD), 1)
        row = lax.broadcasted_iota(jnp.int32, (TQ, HD), 0)
        qa = jnp.concatenate([q_ref[...], jnp.where(lane < 3, 1.0, 0.0).astype(BF16)], axis=1)

        def qk(j):
            rows = pl.ds(pl.multiple_of(j * TQ, TQ), TQ)
            ka = jnp.concatenate([k_ref[rows, :], kx_ref[0, rows, :]], axis=1)
            return lax.dot_general(qa, ka, NT_DIMS, preferred_element_type=F32)

        def step(j, carry, u, masked):
            m, l, acc = carry
            rows = pl.ds(pl.multiple_of(j * TQ, TQ), TQ)
            us = [u[:, HD * b:HD * (b + 1)] for b in range(nb)]
            if masked:
                us = [jnp.where(row >= lane + HD * b, us[b], NEG) for b in range(nb)]
            bm = functools.reduce(jnp.maximum, us)
            m_new = jnp.maximum(m, jnp.max(bm, axis=1, keepdims=True))
            alpha = jnp.exp2((m - m_new) * EXP2_SCALE)
            shift = m_new * EXP2_SCALE
            ps = [jnp.exp2(ub * EXP2_SCALE - shift) for ub in us]
            l = alpha * l + functools.reduce(jnp.add, ps)
            pr = jnp.concatenate(ps, axis=1).astype(BF16)
            acc = alpha * acc + jnp.dot(pr, v_ref[rows, :], preferred_element_type=F32)
            return m_new, l, acc

        init = (jnp.full((TQ, HD), NEG, F32), jnp.zeros((TQ, HD), F32), jnp.zeros((TQ, HD), F32))
        carry = lax.fori_loop(0, i, lambda j, cr: step(j, cr, qk(j), False), init)
        m, l, acc = step(i, carry, qk(i), True)
        l_row = jnp.sum(l, axis=1, keepdims=True)
        o_ref[...] = acc / l_row
        ax_ref[0] = _bias_lanes(-(m + jnp.log(l_row) * (1.0 / SCALE)), 3, 0)

    return pl.pallas_call(
        body, name="attn_fwd", grid=(NH, n),
        in_specs=[
            pl.BlockSpec((TQ, HD), lambda h, i: (i, h)),
            pl.BlockSpec((s, HD), lambda h, i: (0, NH + h)),
            pl.BlockSpec((s, HD), lambda h, i: (0, 2 * NH + h)),
            pl.BlockSpec((1, s, HD), lambda h, i: (h, 0, 0)),
        ],
        out_specs=[pl.BlockSpec((TQ, HD), lambda h, i: (i, h)), pl.BlockSpec((1, TQ, HD), lambda h, i: (h, i, 0))],
        out_shape=[jax.ShapeDtypeStruct((s, D), F32), jax.ShapeDtypeStruct((NH, s, HD), BF16)],
        compiler_params=_params(dimension_semantics=("parallel", "parallel")),
    )(zq, zq, zq, kx)


def _attn_bwd(zq, do, ax, delta, kx, after):
    s = zq.shape[0]
    n = s // TQ
    nb = TQ // HD

    def body(k_ref, v_ref, kx_ref, q_ref, ax_ref, do_ref, dl_ref, after_ref, dq_ref, dk_ref, dv_ref, dcs_ref, drs_ref):
        j = pl.program_id(1)

        @pl.when(j == 0)
        def _():
            dq_ref[...] = jnp.zeros_like(dq_ref)
            drs_ref[...] = jnp.zeros_like(drs_ref)

        k = k_ref[...]
        v = v_ref[...]
        ka = jnp.concatenate([k, kx_ref[0]], axis=1)
        row = lax.broadcasted_iota(jnp.int32, (TQ, HD), 0)
        lane = lax.broadcasted_iota(jnp.int32, (TQ, HD), 1)

        def products(i):
            rows = pl.ds(pl.multiple_of(i * TQ, TQ), TQ)
            qa = jnp.concatenate([q_ref[rows, :], ax_ref[0, rows, :]], axis=1)
            return (lax.dot_general(qa, ka, NT_DIMS, preferred_element_type=F32),
                    lax.dot_general(do_ref[rows, :], v, NT_DIMS, preferred_element_type=F32))

        def step(i, carry, u, dp, masked):
            dk, dv, dcs = carry
            rows = pl.ds(pl.multiple_of(i * TQ, TQ), TQ)
            q = q_ref[rows, :]
            dout = do_ref[rows, :]
            dlv = dl_ref[0, rows, :]
            prs, dss = [], []
            for b in range(nb):
                cs = slice(HD * b, HD * (b + 1))
                ub = u[:, cs]
                if masked:
                    ub = jnp.where(row >= lane + HD * b, ub, NEG)
                pb = jnp.exp2(ub * EXP2_SCALE)
                prs.append(pb)
                dss.append(pb * (dp[:, cs] - dlv))
            drs_ref[0, rows, :] += functools.reduce(jnp.add, dss)
            ds = jnp.concatenate(dss, axis=1)
            dcs = dcs + jnp.sum(ds.reshape(TQ // 8, 8, TQ), axis=0)
            dsb = ds.astype(BF16)
            dv = dv + lax.dot_general(jnp.concatenate(prs, axis=1).astype(BF16), dout, TN_DIMS, preferred_element_type=F32)
            dk = dk + lax.dot_general(dsb, q, TN_DIMS, preferred_element_type=F32)
            dq_ref[rows, :] += jnp.dot(dsb, k, preferred_element_type=F32) * SCALE
            return dk, dv, dcs

        init = (jnp.zeros((TQ, HD), F32), jnp.zeros((TQ, HD), F32), jnp.zeros((8, TQ), F32))
        carry = step(j, init, *products(j), True)
        dk, dv, dcs = lax.fori_loop(j + 1, n, lambda i, cr: step(i, cr, *products(i), False), carry)
        dk_ref[...] = (dk * SCALE).astype(BF16)
        dv_ref[...] = dv.astype(BF16)
        dcs_ref[0] = jnp.broadcast_to(_colsum(dcs), (8, TQ))

    return pl.pallas_call(
        body, name="attn_bwd", grid=(NH, n),
        in_specs=[
            pl.BlockSpec((TQ, HD), lambda h, j: (j, NH + h)),
            pl.BlockSpec((TQ, HD), lambda h, j: (j, 2 * NH + h)),
            pl.BlockSpec((1, TQ, HD), lambda h, j: (h, j, 0)),
            pl.BlockSpec((s, HD), lambda h, j: (0, h)),
            pl.BlockSpec((1, s, HD), lambda h, j: (h, 0, 0)),
            pl.BlockSpec((s, HD), lambda h, j: (0, h)),
            pl.BlockSpec((1, s, HD), lambda h, j: (h, 0, 0)),
            pl.BlockSpec((8, HD), lambda h, j: (0, 0)),
        ],
        out_specs=[
            pl.BlockSpec((s, HD), lambda h, j: (0, h)),
            pl.BlockSpec((TQ, HD), lambda h, j: (j, h)),
            pl.BlockSpec((TQ, HD), lambda h, j: (j, h)),
            pl.BlockSpec((1, 8, TQ), lambda h, j: (j, h, 0)),
            pl.BlockSpec((1, s, HD), lambda h, j: (h, 0, 0)),
        ],
        out_shape=[
            jax.ShapeDtypeStruct((s, D), F32),
            jax.ShapeDtypeStruct((s, D), BF16),
            jax.ShapeDtypeStruct((s, D), BF16),
            jax.ShapeDtypeStruct((n, 8 * NH, TQ), F32),
            jax.ShapeDtypeStruct((NH, s, HD), F32),
        ],
        compiler_params=_params(dimension_semantics=("parallel", "arbitrary")),
    )(zq, zq, kx, zq, ax, do, delta, after)


def _forget_bwd(dcs, drs, zr, bf_pad):
    n = dcs.shape[0]
    s = n * TQ

    def body(dcs_ref, drs_ref, fl_ref, b_ref, dfl_ref, gb_ref, buf, carry):
        i = pl.program_id(0)

        @pl.when(i == 0)
        def _():
            carry[...] = jnp.zeros_like(carry)
            gb_ref[...] = jnp.zeros_like(gb_ref)

        dc_t = jnp.concatenate([dcs_ref[0], jnp.zeros((HD - 8 * NH, TQ), F32)], axis=0)
        lane = lax.broadcasted_iota(jnp.int32, (TQ, HD), 1)
        dc = -dc_t.T
        for hh in range(NH):
            dc = dc + jnp.where(lane == 8 * hh, jnp.sum(drs_ref[hh], axis=1, keepdims=True), 0.0)
        buf[...] = _cumsum_bwd(dc) + carry[0:1, :]
        carry[0:1, :] = buf[0:1, :]
        fl = fl_ref[...] + b_ref[...]
        dfl = buf[...] * _sigmoid(-fl)
        dfl_ref[...] = dfl.astype(BF16)
        gb_ref[...] += _colsum(dfl)

    return pl.pallas_call(
        body, name="forget_bwd", grid=(n,),
        in_specs=[
            pl.BlockSpec((1, 8 * NH, TQ), lambda i: (n - 1 - i, 0, 0)),
            pl.BlockSpec((NH, TQ, HD), lambda i: (0, n - 1 - i, 0)),
            pl.BlockSpec((TQ, HD), lambda i: (n - 1 - i, FL_COL // HD)),
            _vec(1, HD),
        ],
        out_specs=[pl.BlockSpec((TQ, HD), lambda i: (n - 1 - i, 0)), _vec(1, HD)],
        out_shape=[jax.ShapeDtypeStruct((s, HD), BF16), jax.ShapeDtypeStruct((1, HD), F32)],
        scratch_shapes=[pltpu.VMEM((TQ, HD), F32), pltpu.VMEM((8, HD), F32)],
        compiler_params=_params(dimension_semantics=("arbitrary",)),
    )(dcs, drs, zr, bf_pad)


def _gates(xc, w_ref, b):
    xb = xc.astype(BF16)
    pre = jnp.concatenate(
        [jnp.dot(xb[:, HD * g:HD * (g + 1)], w_ref[g], preferred_element_type=F32) for g in range(NH)], axis=1)
    return _sigmoid(pre + b)


def _lru_coeffs(r, lam):
    sp = jnp.maximum(-lam, 0.0) + jnp.log(1.0 + jnp.exp(-jnp.abs(lam)))
    log_a = -LRU_C * r * sp
    a = jnp.exp(log_a)
    y = 2.0 * log_a
    em1 = jnp.where(jnp.abs(y) < 0.01, y * (1.0 + y * (0.5 + y * (1.0 / 6.0))), jnp.exp(y) - 1.0)
    return sp, a, jnp.sqrt(-em1)


def _conv_taps(ext, t):
    return [_shift_down(ext, CONV_W - 1 - jj, t) for jj in range(CONV_W)]


def _lru_fwd(zr, conv_w8, conv_b, w_r, b_r, w_i, b_i, lam):
    s = zr.shape[0]
    n = s // TS
    xl_col = 1

    def body(xl_ref, halo_ref, cw_ref, cb_ref, wr_ref, br_ref, wi_ref, bi_ref, lam_ref, xc_ref, h_ref, carry):
        i = pl.program_id(0)

        @pl.when(i == 0)
        def _():
            carry[...] = jnp.zeros_like(carry)

        halo = jnp.where(i == 0, 0.0, halo_ref[...])
        taps = _conv_taps(jnp.concatenate([halo, xl_ref[...]], axis=0), TS)
        xc = cb_ref[...] + sum(cw_ref[jj:jj + 1, :] * taps[jj] for jj in range(CONV_W))
        xc_ref[...] = xc
        r = _gates(xc, wr_ref, br_ref[...])
        ig = _gates(xc, wi_ref, bi_ref[...])
        _, a, gam = _lru_coeffs(r, lam_ref[...])
        a_cum, h_loc = _scan_fwd(a, gam * (ig * xc))
        h_ref[...] = h_loc + a_cum * carry[0:1, :]
        carry[0:1, :] = h_ref[TS - 1:TS, :]

    return pl.pallas_call(
        body, name="lru_fwd", grid=(n,),
        in_specs=[
            _row(D, xl_col),
            pl.BlockSpec((8, D), lambda i: (jnp.maximum(i * (TS // 8) - 1, 0), xl_col)),
            _vec(8, D), _vec(1, D),
            pl.BlockSpec((NH, HD, HD), lambda i: (0, 0, 0)), _vec(1, D),
            pl.BlockSpec((NH, HD, HD), lambda i: (0, 0, 0)), _vec(1, D),
            _vec(1, D),
        ],
        out_specs=[_row(D), _row(D)],
        out_shape=[jax.ShapeDtypeStruct((s, D), F32), jax.ShapeDtypeStruct((s, D), F32)],
        scratch_shapes=[pltpu.VMEM((8, D), F32)],
        compiler_params=_params(dimension_semantics=("arbitrary",)),
    )(zr, zr, conv_w8, conv_b, w_r, b_r, w_i, b_i, lam)


def _lru_bwd(zr, xc, h, dh, conv_w8, w_r, b_r, w_i, b_i, lam):
    s = zr.shape[0]
    n = s // TS
    xl_col = 1

    def rev(i):
        return n - 1 - i

    def body(xl_ref, xlh_ref, xc_ref, h_ref, hh_ref, dh_ref, cw_ref, wr_ref, br_ref, wi_ref, bi_ref, lam_ref,
             dxl_ref, gwr_ref, gwi_ref, gbr_ref, gbi_ref, glam_ref, gcb_ref, gcw_ref, l_buf, dxc_buf, carry_g, carry_dxc):
        i = pl.program_id(0)
        first = rev(i) == 0

        @pl.when(i == 0)
        def _():
            carry_g[...] = jnp.zeros_like(carry_g)
            carry_dxc[...] = jnp.zeros_like(carry_dxc)
            for ref in (gwr_ref, gwi_ref, gbr_ref, gbi_ref, glam_ref, gcb_ref, gcw_ref):
                ref[...] = jnp.zeros_like(ref)

        rows = _rows_iota(TS)
        xc = xc_ref[...]
        lam = lam_ref[...]
        r = _gates(xc, wr_ref, br_ref[...])
        ig = _gates(xc, wi_ref, bi_ref[...])
        sp, a, gam = _lru_coeffs(r, lam)
        g = dh_ref[...] + jnp.where(rows == TS - 1, carry_g[0:1, :], 0.0)
        b = jnp.where(rows == TS - 1, 0.0, pltpu.roll(a, TS - 1, 0))
        l_buf[...] = _scan_bwd(b, g)
        lv = l_buf[...]
        carry_g[0:1, :] = l_buf[0:1, :] * a[0:1, :]
        h_prev_row = jnp.where(first, 0.0, hh_ref[7:8, :])
        h_prev = jnp.where(rows == 0, h_prev_row, pltpu.roll(h_ref[...], 1, 0))
        dgam = lv * ig * xc
        dig = lv * gam * xc
        dxc = lv * gam * ig
        dla = lv * h_prev * a - dgam * (a * a) / gam
        dr = dla * (-LRU_C) * sp
        glam_ref[...] += _colsum(dla * r) * (LRU_C * _sigmoid(-lam))
        dpr = dr * r * (1.0 - r)
        dpi = dig * ig * (1.0 - ig)
        gbr_ref[...] += _colsum(dpr)
        gbi_ref[...] += _colsum(dpi)
        xb = xc.astype(BF16)
        dprb = dpr.astype(BF16)
        dpib = dpi.astype(BF16)
        back = []
        for gi in range(NH):
            cs = slice(HD * gi, HD * (gi + 1))
            gwr_ref[gi] += lax.dot_general(xb[:, cs], dprb[:, cs], TN_DIMS, preferred_element_type=F32)
            gwi_ref[gi] += lax.dot_general(xb[:, cs], dpib[:, cs], TN_DIMS, preferred_element_type=F32)
            back.append(lax.dot_general(dprb[:, cs], wr_ref[gi], NT_DIMS, preferred_element_type=F32)
                        + lax.dot_general(dpib[:, cs], wi_ref[gi], NT_DIMS, preferred_element_type=F32))
        dxc = dxc + jnp.concatenate(back, axis=1)
        dxc_buf[...] = dxc
        gcb_ref[...] += _colsum(dxc)
        halo = jnp.where(first, 0.0, xlh_ref[...])
        taps = _conv_taps(jnp.concatenate([halo, xl_ref[...]], axis=0), TS)
        for jj in range(CONV_W):
            gcw_ref[jj:jj + 1, :] += _colsum(dxc * taps[jj])
        ext = jnp.concatenate([dxc, carry_dxc[...]], axis=0)
        dxl = sum(cw_ref[jj:jj + 1, :] * _shift_up(ext, CONV_W - 1 - jj, TS) for jj in range(CONV_W))
        dxl_ref[...] = dxl.astype(BF16)
        carry_dxc[...] = dxc_buf[0:8, :]

    rowr = lambda c, col=0: pl.BlockSpec((TS, c), lambda i: (rev(i), col))
    halo = lambda col: pl.BlockSpec((8, D), lambda i: (jnp.maximum(rev(i) * (TS // 8) - 1, 0), col))
    gate_w = pl.BlockSpec((NH, HD, HD), lambda i: (0, 0, 0))
    return pl.pallas_call(
        body, name="lru_bwd", grid=(n,),
        in_specs=[rowr(D, xl_col), halo(xl_col), rowr(D), rowr(D), halo(0), rowr(D),
                  _vec(8, D), gate_w, _vec(1, D), gate_w, _vec(1, D), _vec(1, D)],
        out_specs=[rowr(D), gate_w, gate_w, _vec(1, D), _vec(1, D), _vec(1, D), _vec(1, D), _vec(8, D)],
        out_shape=[
            jax.ShapeDtypeStruct((s, D), BF16),
            jax.ShapeDtypeStruct((NH, HD, HD), F32), jax.ShapeDtypeStruct((NH, HD, HD), F32),
            jax.ShapeDtypeStruct((1, D), F32), jax.ShapeDtypeStruct((1, D), F32), jax.ShapeDtypeStruct((1, D), F32),
            jax.ShapeDtypeStruct((1, D), F32), jax.ShapeDtypeStruct((8, D), F32),
        ],
        scratch_shapes=[pltpu.VMEM((TS, D), F32), pltpu.VMEM((TS, D), F32), pltpu.VMEM((8, D), F32), pltpu.VMEM((8, D), F32)],
        compiler_params=_params(dimension_semantics=("arbitrary",)),
    )(zr, zr, xc, h, h, dh, conv_w8, w_r, b_r, w_i, b_i, lam)


def _silu_parts(g):
    sg = _sigmoid(g)
    return g * sg, sg * (1.0 + g * (1.0 - sg))


def _branch_out(o, h, zr, gain_a, gain_l):
    s = o.shape[0]

    def body(o_ref, ga_ref, h_ref, gl_ref, ka_ref, kl_ref, y_ref):
        ohat, _ = _rms_fwd(o_ref[...], None)
        y_ref[:, 0:D] = (ohat * ka_ref[...] * _silu_parts(ga_ref[...])[0]).astype(BF16)
        hhat, _ = _rms_fwd(h_ref[...], None)
        y_ref[:, D:2 * D] = (hhat * kl_ref[...] * _silu_parts(gl_ref[...])[0]).astype(BF16)

    return pl.pallas_call(
        body, name="branch_out", grid=(s // TS,),
        in_specs=[_row(D), _row(D, 0), _row(D), _row(D, 2), _vec(1, D), _vec(1, D)],
        out_specs=_row(2 * D),
        out_shape=jax.ShapeDtypeStruct((s, 2 * D), BF16),
        compiler_params=_params(dimension_semantics=("parallel",)),
    )(o, zr, h, zr, gain_a, gain_l)


def _branch_out_bwd(o, h, zr, dycat, gain_a, gain_l):
    s = o.shape[0]

    def body(o_ref, ga_ref, h_ref, gl_ref, dya_ref, dyl_ref, ka_ref, kl_ref,
             do_ref, dl_ref, dga_ref, dh_ref, dgl_ref, gka_ref, gkl_ref):
        @pl.when(pl.program_id(0) == 0)
        def _():
            gka_ref[...] = jnp.zeros_like(gka_ref)
            gkl_ref[...] = jnp.zeros_like(gkl_ref)

        def one(v, g, dy, gain):
            vhat, rstd = _rms_fwd(v, None)
            sg, dsg = _silu_parts(g)
            dn = dy * sg
            dg = dy * (vhat * gain) * dsg
            return _rms_bwd(vhat, rstd, dn * gain), dg, _colsum(dn * vhat)

        o = o_ref[...]
        dout, dga, gka = one(o, ga_ref[...], dya_ref[...], ka_ref[...])
        do_ref[...] = dout.astype(BF16)
        dga_ref[...] = dga.astype(BF16)
        gka_ref[...] += gka
        prod = dout * o
        for hh in range(NH):
            dl_ref[hh] = jnp.broadcast_to(jnp.sum(prod[:, HD * hh:HD * (hh + 1)], axis=1, keepdims=True), (TS, HD))
        dh, dgl, gkl = one(h_ref[...], gl_ref[...], dyl_ref[...], kl_ref[...])
        dh_ref[...] = dh
        dgl_ref[...] = dgl.astype(BF16)
        gkl_ref[...] += gkl

    return pl.pallas_call(
        body, name="branch_out_bwd", grid=(s // TS,),
        in_specs=[_row(D), _row(D, 0), _row(D), _row(D, 2), _row(D, 0), _row(D, 1), _vec(1, D), _vec(1, D)],
        out_specs=[_row(D), pl.BlockSpec((NH, TS, HD), lambda i: (0, i, 0)), _row(D), _row(D), _row(D), _vec(1, D), _vec(1, D)],
        out_shape=[
            jax.ShapeDtypeStruct((s, D), BF16), jax.ShapeDtypeStruct((NH, s, HD), F32), jax.ShapeDtypeStruct((s, D), BF16),
            jax.ShapeDtypeStruct((s, D), F32), jax.ShapeDtypeStruct((s, D), BF16),
            jax.ShapeDtypeStruct((1, D), F32), jax.ShapeDtypeStruct((1, D), F32),
        ],
        compiler_params=_params(dimension_semantics=("arbitrary",)),
    )(o, zr, h, zr, dycat, dycat, gain_a, gain_l)


def _residual(x, mix, post_gain):
    s = x.shape[0]

    def body(x_ref, m_ref, g_ref, h_ref, hb_ref):
        mhat, _ = _rms_fwd(m_ref[...], None)
        h1 = x_ref[...] + mhat * g_ref[...]
        h_ref[...] = h1
        hb_ref[...] = h1.astype(BF16)

    return pl.pallas_call(
        body, name="residual", grid=(s // TS,),
        in_specs=[_row(D), _row(D), _vec(1, D)], out_specs=[_row(D), _row(D)],
        out_shape=[jax.ShapeDtypeStruct((s, D), F32), jax.ShapeDtypeStruct((s, D), BF16)],
        compiler_params=_params(dimension_semantics=("parallel",)),
    )(x, mix, post_gain)


def _head(h1, pe, gp, tgt, ple_gain, b_gate):
    s = h1.shape[0]

    def body(h_ref, pe_ref, gp_ref, t_ref, kg_ref, b_ref, loss_ref, dy_ref, dgp_ref, dpe_ref, gk_ref, gb_ref):
        @pl.when(pl.program_id(0) == 0)
        def _():
            loss_ref[...] = jnp.zeros_like(loss_ref)
            gk_ref[...] = jnp.zeros_like(gk_ref)
            gb_ref[...] = jnp.zeros_like(gb_ref)

        ehat, rstd = _rms_fwd(pe_ref[...], None)
        e = ehat * kg_ref[...]
        gate = _sigmoid(gp_ref[...] + b_ref[...])
        diff = (h_ref[...] + gate * e) - t_ref[...]
        per_row = jnp.mean(diff * diff, axis=-1, keepdims=True)
        loss_ref[...] += 0.5 * jnp.sum(per_row, axis=0, keepdims=True)
        dy = diff * (1.0 / D)
        dy_ref[...] = dy
        dgp = dy * e * gate * (1.0 - gate)
        dgp_ref[...] = dgp.astype(BF16)
        gb_ref[...] += _colsum(dgp)
        de = dy * gate
        gk_ref[...] += _colsum(de * ehat)
        dpe_ref[...] = _rms_bwd(ehat, rstd, de * kg_ref[...]).astype(BF16)

    return pl.pallas_call(
        body, name="head", grid=(s // TS,),
        in_specs=[_row(D), _row(D), _row(D), _row(D), _vec(1, D), _vec(1, D)],
        out_specs=[_vec(1, 1), _row(D), _row(D), _row(D), _vec(1, D), _vec(1, D)],
        out_shape=[
            jax.ShapeDtypeStruct((1, 1), F32), jax.ShapeDtypeStruct((s, D), F32), jax.ShapeDtypeStruct((s, D), BF16),
            jax.ShapeDtypeStruct((s, D), BF16), jax.ShapeDtypeStruct((1, D), F32), jax.ShapeDtypeStruct((1, D), F32),
        ],
        compiler_params=_params(dimension_semantics=("arbitrary",)),
    )(h1, pe, gp, tgt, ple_gain, b_gate)


def _residual_bwd(dy, t, mix, post_gain):
    s = dy.shape[0]

    def body(dy_ref, t_ref, m_ref, g_ref, dh_ref, dm_ref, gg_ref):
        @pl.when(pl.program_id(0) == 0)
        def _():
            gg_ref[...] = jnp.zeros_like(gg_ref)

        dh1 = dy_ref[...] + t_ref[...]
        dh_ref[...] = dh1
        mhat, rstd = _rms_fwd(m_ref[...], None)
        gg_ref[...] += _colsum(dh1 * mhat)
        dm_ref[...] = _rms_bwd(mhat, rstd, dh1 * g_ref[...]).astype(BF16)

    return pl.pallas_call(
        body, name="residual_bwd", grid=(s // TS,),
        in_specs=[_row(D), _row(D), _row(D), _vec(1, D)], out_specs=[_row(D), _row(D), _vec(1, D)],
        out_shape=[jax.ShapeDtypeStruct((s, D), F32), jax.ShapeDtypeStruct((s, D), BF16), jax.ShapeDtypeStruct((1, D), F32)],
        compiler_params=_params(dimension_semantics=("arbitrary",)),
    )(dy, t, mix, post_gain)


def _prenorm_bwd(x, dxn_a, dxn_b, dh1, pre_gain):
    s = x.shape[0]

    def body(x_ref, da_ref, db_ref, dh_ref, g_ref, dx_ref, gg_ref):
        @pl.when(pl.program_id(0) == 0)
        def _():
            gg_ref[...] = jnp.zeros_like(gg_ref)

        xhat, rstd = _rms_fwd(x_ref[...], None)
        dxn = da_ref[...] + db_ref[...]
        gg_ref[...] += _colsum(dxn * xhat)
        dx_ref[...] = dh_ref[...] + _rms_bwd(xhat, rstd, dxn * g_ref[...])

    return pl.pallas_call(
        body, name="prenorm_bwd", grid=(s // TS,),
        in_specs=[_row(D), _row(D), _row(D), _row(D), _vec(1, D)], out_specs=[_row(D), _vec(1, D)],
        out_shape=[jax.ShapeDtypeStruct((s, D), F32), jax.ShapeDtypeStruct((1, D), F32)],
        compiler_params=_params(dimension_semantics=("arbitrary",)),
    )(x, dxn_a, dxn_b, dh1, pre_gain)


def _adamw(name, parts, w, m, v):
    r, c = w.shape
    if r % 8 == 0:
        tr = _pick(r, (256, 128, 16, 8))
        grid = (r // tr,)
        blk = pl.BlockSpec((tr, c), lambda i: (i, 0))
        parts_blk = pl.BlockSpec((N_DEV, tr, c), lambda i: (0, i, 0))
    else:
        tc = _pick(c, (256, 128))
        grid = (c // tc,)
        blk = pl.BlockSpec((r, tc), lambda i: (0, i))
        parts_blk = pl.BlockSpec((N_DEV, r, tc), lambda i: (0, 0, i))

    def body(p_ref, w_ref, m_ref, v_ref, g_ref, d_ref, nm_ref, nv_ref):
        g = p_ref[0].astype(F32)
        for j in range(1, N_DEV):
            g = g + p_ref[j].astype(F32)
        g_ref[...] = g
        nm = ADAM_B1 * m_ref[...] + (1.0 - ADAM_B1) * g
        nv = ADAM_B2 * v_ref[...] + (1.0 - ADAM_B2) * (g * g)
        nm_ref[...] = nm
        nv_ref[...] = nv
        m_hat = nm / (1.0 - ADAM_B1 ** ADAM_STEP)
        v_hat = nv / (1.0 - ADAM_B2 ** ADAM_STEP)
        d_ref[...] = -ADAM_LR * (m_hat / (jnp.sqrt(v_hat) + ADAM_EPS) + ADAM_WD * w_ref[...])

    return pl.pallas_call(
        body, name=name, grid=grid,
        in_specs=[parts_blk, blk, blk, blk],
        out_specs=[blk] * 4,
        out_shape=[jax.ShapeDtypeStruct((r, c), F32)] * 4,
        compiler_params=_params(dimension_semantics=("parallel",)),
    )(parts, w, m, v)


def _spread8(v):
    r = v.shape[0]
    return jnp.pad(jnp.pad(v[:, :, None], ((0, 0), (0, 0), (0, 7))).reshape(r, 8 * NH), ((0, 0), (0, HD - 8 * NH)))


def _gather8(v):
    return v[:, :8 * NH].reshape(v.shape[0], NH, 8)[:, :, 0]


def _cols_to_shards(g):
    r, c8 = g.shape
    return g.reshape(r, N_DEV, c8 // N_DEV).transpose(1, 0, 2)


def _shards_to_cols(g):
    n, r, c = g.shape
    return g.transpose(1, 0, 2).reshape(r, n * c)


def kernel(x, p, w_in, b_f, pre_gain, post_gain, conv_w, conv_b, w_rgate, b_rgate, w_igate, b_igate, lru_lambda, attn_out_gain, lru_out_gain, w_out, w_ple, ple_gain, w_ple_gate, b_ple_gate, loss_target, m_w_in, m_b_f, m_pre_gain, m_post_gain, m_conv_w, m_conv_b, m_w_rgate, m_b_rgate, m_w_igate, m_b_igate, m_lru_lambda, m_attn_out_gain, m_lru_out_gain, m_w_out, m_w_ple, m_ple_gain, m_w_ple_gate, m_b_ple_gate, v_w_in, v_b_f, v_pre_gain, v_post_gain, v_conv_w, v_conv_b, v_w_rgate, v_b_rgate, v_w_igate, v_b_igate, v_lru_lambda, v_attn_out_gain, v_lru_out_gain, v_w_out, v_w_ple, v_ple_gain, v_w_ple_gate, v_b_ple_gate):
    me = 4 * lax.axis_index("x") + 2 * lax.axis_index("y") + lax.axis_index("c")
    x2, p2, tgt = x[0], p[0, 0], loss_target[0]

    conv_w_shard8 = jnp.pad(conv_w[0], ((0, 8 - CONV_W), (0, 0)))
    wt, m_wt, v_wt = w_in[0].T, m_w_in[0].T, v_w_in[0].T
    g_wint, g_conv = _gather_two_level("gather_w_in", [wt.astype(BF16), conv_w_shard8])
    rest_state, rest_token = _exchange_start(
        "gather_rest_start", [w_out[0].astype(BF16), w_ple[0].astype(BF16), w_ple_gate[0].astype(BF16)], ["bcast"] * 3)
    win_t = g_wint.reshape(D_IN, D)
    w_qkv_t = win_t[:D_QKV]
    w_rest_t = jnp.concatenate([win_t[D_QKV + NH:], _spread8(win_t[D_QKV:D_QKV + NH].T).T], axis=0)
    conv_w8 = _shards_to_cols(g_conv)
    bf_pad = _spread8(b_f)
    w_r, w_i = w_rgate[0].astype(BF16), w_igate[0].astype(BF16)

    xn = _prenorm(x2, pre_gain + rest_token[0:1, 0:1])
    zq = _mm("proj_qkv", xn, w_qkv_t, "nt", BF16)
    zr = _mm("proj_rest", xn, w_rest_t, "nt", F32)
    kx = _forget_fwd(zr, bf_pad)
    o, ax = _attn_fwd(zq, kx)
    xc, h = _lru_fwd(zr, conv_w8, conv_b, w_r, b_rgate, w_i, b_igate, lru_lambda)
    ycat = _branch_out(o, h, zr, attn_out_gain, lru_out_gain)
    g_wout, g_wple, g_wpg = _exchange_wait("gather_rest_wait", rest_state, ycat)
    wout_full = g_wout.reshape(2 * D, D)
    wple_full = _shards_to_cols(g_wple)
    wpg_full = g_wpg.reshape(D, D)
    mix = _mm("proj_out", ycat, wout_full, "nn", F32)
    h1, h1b = _residual(x2, mix, post_gain)
    pe = _mm("proj_ple", p2, wple_full, "nn", F32)
    gp = _mm("proj_gate", h1b, wpg_full, "nn", F32)
    loss_part, dy, dgp, dpe, g_ple_gain, g_b_gate = _head(h1, pe, gp, tgt, ple_gain, b_ple_gate)
    loss = lax.psum(loss_part[0, 0], ("x", "y", "c"))

    t = _mm("bwd_gate_x", dgp, wpg_full, "nt", F32)
    gw_pg = _mm("bwd_gate_w", h1b, dgp, "tn", BF16)
    gw_ple = _mm("bwd_ple_w", p2, dpe, "tn", BF16)
    dh1, dmix, g_post_gain = _residual_bwd(dy, t, mix, post_gain)
    dycat = _mm("bwd_out_x", dmix, wout_full, "nt", F32)
    gw_out = _mm("bwd_out_w", ycat, dmix, "tn", BF16)
    do, delta, dga, dh, dgl, g_aog, g_log = _branch_out_bwd(o, h, zr, dycat, attn_out_gain, lru_out_gain)
    dxl, g_wr, g_wi, g_br, g_bi, g_lam, g_cb, g_cw8 = _lru_bwd(
        zr, xc, h, dh, conv_w8, w_r, b_rgate, w_i, b_igate, lru_lambda)
    gates = jnp.concatenate([g_wr.reshape(D, HD), g_wi.reshape(D, HD)], axis=0).astype(BF16)
    outw_state, outw_token = _exchange_start(
        "exchange_outw_start",
        [gw_out.reshape(N_DEV, 2 * D // N_DEV, D), _cols_to_shards(gw_ple), gw_pg.reshape(N_DEV, D // N_DEV, D), gates],
        ["scatter"] * 3 + ["bcast"])
    dq, dk, dv, dcs, drs = _attn_bwd(zq, do, ax, delta, kx, outw_token)
    dfl, g_bf_pad = _forget_bwd(dcs, drs, zr, bf_pad)
    dzq = jnp.concatenate([dq.astype(BF16), dk, dv], axis=1)
    dzr = jnp.concatenate([dga, dxl, dgl, dfl], axis=1)
    gw_qkv_t = _mm("bwd_qkv_w", dzq, xn, "tn", BF16)
    gw_rest_t = _mm("bwd_rest_w", dzr, xn, "tn", BF16)
    gw_in_t = jnp.concatenate([gw_qkv_t, _gather8(gw_rest_t[FL_COL:].T).T, gw_rest_t[:FL_COL]], axis=0)
    inw_state, inw_token = _exchange_start(
        "exchange_inw_start", [gw_in_t.reshape(N_DEV, D_IN_SHARD, D)], ["scatter"])
    dxn_a = _mm("bwd_qkv_x", dzq, w_qkv_t, "nn", F32, after=inw_token)
    dxn_b = _mm("bwd_rest_x", dzr, w_rest_t, "nn", F32, after=inw_token)
    grad_x, g_pre_gain = _prenorm_bwd(x2, dxn_a, dxn_b, dh1, pre_gain)

    upd = {}
    r_wout, r_wple, r_wpg, r_gates = _exchange_wait("exchange_outw_wait", outw_state, grad_x)
    upd["w_out"] = _adamw("adamw_w_out", r_wout, w_out[0], m_w_out[0], v_w_out[0])
    upd["w_ple"] = _adamw("adamw_w_ple", r_wple, w_ple[0], m_w_ple[0], v_w_ple[0])
    upd["w_ple_gate"] = _adamw("adamw_w_ple_gate", r_wpg, w_ple_gate[0], m_w_ple_gate[0], v_w_ple_gate[0])
    gates_of = lambda a, b: jnp.concatenate([a[0].reshape(D, HD), b[0].reshape(D, HD)], axis=0)
    g_gates = _adamw("adamw_gates", r_gates, gates_of(w_rgate, w_igate), gates_of(m_w_rgate, m_w_igate),
                     gates_of(v_w_rgate, v_w_igate))
    upd["w_rgate"] = [a[:D].reshape(1, NH, HD, HD) for a in g_gates]
    upd["w_igate"] = [a[D:].reshape(1, NH, HD, HD) for a in g_gates]
    small = jnp.concatenate(
        [jnp.pad(_gather8(g_bf_pad), ((0, 0), (0, D - NH))), g_pre_gain, g_post_gain, g_cb, g_br, g_bi, g_lam, g_aog, g_log,
         g_ple_gain, g_b_gate, g_cw8[:CONV_W], upd["w_out"][0][0:1]], axis=0)
    (r_small,) = _exchange("exchange_small", [small], ["bcast"])
    vec_names = ["b_f", "pre_gain", "post_gain", "conv_b", "b_rgate", "b_igate", "lru_lambda", "attn_out_gain",
                 "lru_out_gain", "ple_gain", "b_ple_gate"]
    vec_w = dict(b_f=(b_f, m_b_f, v_b_f), pre_gain=(pre_gain, m_pre_gain, v_pre_gain),
                 post_gain=(post_gain, m_post_gain, v_post_gain), conv_b=(conv_b, m_conv_b, v_conv_b),
                 b_rgate=(b_rgate, m_b_rgate, v_b_rgate), b_igate=(b_igate, m_b_igate, v_b_igate),
                 lru_lambda=(lru_lambda, m_lru_lambda, v_lru_lambda),
                 attn_out_gain=(attn_out_gain, m_attn_out_gain, v_attn_out_gain),
                 lru_out_gain=(lru_out_gain, m_lru_out_gain, v_lru_out_gain), ple_gain=(ple_gain, m_ple_gain, v_ple_gain),
                 b_ple_gate=(b_ple_gate, m_b_ple_gate, v_b_ple_gate))
    conv_mine = lambda a: lax.dynamic_slice_in_dim(a, me * HD, HD, axis=1)

    def small_rows(k):
        rows = [jnp.pad(vec_w[nm][k], ((0, 0), (0, D - vec_w[nm][k].shape[1]))) for nm in vec_names]
        cw = (conv_w, m_conv_w, v_conv_w)[k][0]
        full = lax.dynamic_update_slice_in_dim(jnp.ones((CONV_W, D), F32), cw, me * HD, axis=1)
        return jnp.concatenate(rows + [full, jnp.ones((1, D), F32)], axis=0)

    g_small = _adamw("adamw_small", r_small, small_rows(0), small_rows(1), small_rows(2))
    for idx, nm in enumerate(vec_names):
        width = vec_w[nm][0].shape[1]
        upd[nm] = [a[idx:idx + 1, :width] for a in g_small]
    base = len(vec_names)
    upd["conv_w"] = [conv_mine(a[base:base + CONV_W])[None] for a in g_small]
    (r_win,) = _exchange_wait("exchange_inw_wait", inw_state, g_small[0])
    upd["w_in"] = [a.T for a in _adamw("adamw_w_in", r_win, wt, m_wt, v_wt)]
    for nm in ("w_in", "w_out", "w_ple", "w_ple_gate"):
        upd[nm] = [a[None] for a in upd[nm]]

    order = ["w_in", "b_f", "pre_gain", "post_gain", "conv_w", "conv_b", "w_rgate", "b_rgate", "w_igate", "b_igate",
             "lru_lambda", "attn_out_gain", "lru_out_gain", "w_out", "w_ple", "ple_gain", "w_ple_gate", "b_ple_gate"]
    outs = [loss, grad_x[None]]
    for k in range(4):
        outs += [upd[nm][k] for nm in order]
    return tuple(outs)
```

```python
import functools

import jax
import jax.numpy as jnp
from jax import lax
from jax.experimental import pallas as pl
from jax.experimental.pallas import tpu as pltpu

F32 = jnp.float32
BF16 = jnp.bfloat16

N_DEV = 8
D = 1024
HD = 128
NH = 8
D_IN = 6152
D_IN_SHARD = D_IN // N_DEV
D_QKV = 3 * D
D_REST = 3 * D + HD
FL_COL = 3 * D
D_PLE = 256
CONV_W = 4
LRU_C = 8.0
RMS_EPS = 1e-6
SCALE = HD ** -0.5
EXP2_SCALE = SCALE * 1.4426950408889634
NEG = -1e30

ADAM_LR = 0.001
ADAM_B1 = 0.9
ADAM_B2 = 0.999
ADAM_EPS = 1e-08
ADAM_WD = 0.01
ADAM_STEP = 10

TS = 256
TQ = 1024
VMEM_LIMIT = 48 * 1024 * 1024

NT_DIMS = (((1,), (1,)), ((), ()))
TN_DIMS = (((0,), (0,)), ((), ()))


def _params(**kw):
    return pltpu.CompilerParams(vmem_limit_bytes=VMEM_LIMIT, **kw)


def _sigmoid(v):
    return 0.5 * jnp.tanh(0.5 * v) + 0.5


def _sigmoid_rel(v):
    return 1.0 / (1.0 + jnp.exp(-v))


def _rms_fwd(v, gain):
    rstd = lax.rsqrt(jnp.mean(v * v, axis=-1, keepdims=True) + RMS_EPS)
    return v * rstd, rstd


def _rms_bwd(vhat, rstd, dvhat):
    return rstd * (dvhat - vhat * jnp.mean(dvhat * vhat, axis=-1, keepdims=True))


def _colsum(v):
    return jnp.sum(v, axis=0, keepdims=True)


def _rows_iota(t):
    return lax.broadcasted_iota(jnp.int32, (t, 1), 0)


def _scan(a, u, reverse):
    t, c = a.shape
    rows = _rows_iota(t)
    d = 1
    while d < t:
        if d < 8:
            valid = rows < t - d if reverse else rows >= d
            shift = t - d if reverse else d
            u = jnp.where(valid, u + a * pltpu.roll(u, shift, 0), u)
            a = jnp.where(valid, a * pltpu.roll(a, shift, 0), a)
        else:
            zeros, ones = jnp.zeros((d, c), F32), jnp.ones((d, c), F32)
            if reverse:
                u_far, a_far = jnp.concatenate([u[d:], zeros], axis=0), jnp.concatenate([a[d:], ones], axis=0)
            else:
                u_far, a_far = jnp.concatenate([zeros, u[:t - d]], axis=0), jnp.concatenate([ones, a[:t - d]], axis=0)
            u = u + a * u_far
            a = a * a_far
        d *= 2
    return a, u


def _cumsum_fwd(v):
    t = v.shape[0]
    rows = _rows_iota(t)
    d = 1
    while d < t:
        v = jnp.where(rows >= d, v + pltpu.roll(v, d, 0), v)
        d *= 2
    return v


def _cumsum_bwd(v):
    t = v.shape[0]
    rows = _rows_iota(t)
    d = 1
    while d < t:
        v = jnp.where(rows < t - d, v + pltpu.roll(v, t - d, 0), v)
        d *= 2
    return v


def _bias_lanes(v, at, ones_at):
    lane = lax.broadcasted_iota(jnp.int32, v.shape, 1)
    hi = v.astype(BF16).astype(F32)
    mid = (v - hi).astype(BF16).astype(F32)
    lo = ((v - hi) - mid).astype(BF16).astype(F32)
    out = jnp.where((lane >= ones_at) & (lane < ones_at + 3), 1.0, 0.0)
    for k, piece in enumerate((hi, mid, lo)):
        out = jnp.where(lane == at + k, piece, out)
    return out.astype(BF16)


def _shift_down(ext, k, t):
    return pltpu.roll(ext, k, 0)[8:, :] if k else ext[8:, :]


def _shift_up(ext, k, t):
    return pltpu.roll(ext, t + 8 - k, 0)[:t, :] if k else ext[:t, :]


def _exchange(name, arrs, kinds):
    n = len(arrs)
    out_shape = []
    for a, kind in zip(arrs, kinds):
        shp = a.shape if kind == "scatter" else (N_DEV,) + a.shape
        out_shape.append(jax.ShapeDtypeStruct(shp, a.dtype))

    def body(*refs):
        ins, outs = refs[:n], refs[n:2 * n]
        send_sems, recv_sems, local_sems = refs[2 * n:]
        x, y, c = lax.axis_index("x"), lax.axis_index("y"), lax.axis_index("c")
        me = 4 * x + 2 * y + c
        copies = []
        for i in range(n):
            scatter = kinds[i] == "scatter"
            mine = pltpu.make_async_copy(ins[i].at[me] if scatter else ins[i], outs[i].at[me], local_sems.at[i])
            mine.start()
            copies.append(mine)
            for m in range(1, N_DEV):
                px = 1 - x if m & 4 else x
                py = 1 - y if m & 2 else y
                pc = 1 - c if m & 1 else c
                peer = 4 * px + 2 * py + pc
                cp = pltpu.make_async_remote_copy(
                    src_ref=ins[i].at[peer] if scatter else ins[i],
                    dst_ref=outs[i].at[me],
                    send_sem=send_sems.at[i, m - 1],
                    recv_sem=recv_sems.at[i, m - 1],
                    device_id=(px, py, pc),
                    device_id_type=pl.DeviceIdType.MESH,
                )
                cp.start()
                copies.append(cp)
        for cp in copies:
            cp.wait()

    any_spec = pl.BlockSpec(memory_space=pl.ANY)
    return pl.pallas_call(
        body,
        name=name,
        out_shape=out_shape,
        in_specs=[any_spec] * n,
        out_specs=[any_spec] * n,
        scratch_shapes=[
            pltpu.SemaphoreType.DMA((n, N_DEV - 1)),
            pltpu.SemaphoreType.DMA((n, N_DEV - 1)),
            pltpu.SemaphoreType.DMA((n,)),
        ],
        compiler_params=pltpu.CompilerParams(has_side_effects=True),
    )(*arrs)


def _gather_two_level(name, arrs, pieces=1):
    n = len(arrs)
    items = []
    for i, a in enumerate(arrs):
        rows = a.shape[0]
        if pieces > 1 and rows >= 512:
            step = -(-rows // (16 * pieces)) * 16
            items += [(i, r0, min(step, rows - r0)) for r0 in range(0, rows, step)]
        else:
            items.append((i, 0, rows))
    n_items = len(items)

    def body(*refs):
        ins, outs = refs[:n], refs[n:2 * n]
        send_sems, recv_sems, local_sems = refs[2 * n:]
        x, y, c = lax.axis_index("x"), lax.axis_index("y"), lax.axis_index("c")
        me, sibling = (x, y, c), (x, y, 1 - c)
        chips = [(1 - x, y), (x, 1 - y), (1 - x, 1 - y)]

        def rows_of(ref, t):
            i, r0, rn = items[t]
            return ref if rn == arrs[i].shape[0] else ref.at[pl.ds(r0, rn)]

        def slot(t, dev):
            return rows_of(outs[items[t][0]].at[4 * dev[0] + 2 * dev[1] + dev[2]], t)

        def copy(t, k, block, to, from_input=False):
            return pltpu.make_async_remote_copy(
                src_ref=rows_of(ins[items[t][0]], t) if from_input else slot(t, block), dst_ref=slot(t, block),
                send_sem=send_sems.at[t, k], recv_sem=recv_sems.at[t, k],
                device_id=to, device_id_type=pl.DeviceIdType.MESH)

        own, sent = [], []
        for t in range(n_items):
            mine = pltpu.make_async_copy(rows_of(ins[items[t][0]], t), slot(t, me), local_sems.at[t])
            mine.start()
            own.append(mine)
            first = [copy(t, 1 + j, me, (*chip, c), from_input=True) for j, chip in enumerate(chips)]
            first.append(copy(t, 0, me, sibling, from_input=True))
            for cp in first:
                cp.start()
            sent += first
        for t in range(n_items):
            for j, chip in enumerate(chips):
                copy(t, 1 + j, (*chip, c), me).wait_recv()
                fwd = copy(t, 4 + j, (*chip, c), sibling)
                fwd.start()
                sent.append(fwd)
        for t in range(n_items):
            copy(t, 0, sibling, me).wait_recv()
            for j, chip in enumerate(chips):
                copy(t, 4 + j, (*chip, 1 - c), me).wait_recv()
        for cp in sent:
            cp.wait_send()
        for cp in own:
            cp.wait()

    any_spec = pl.BlockSpec(memory_space=pl.ANY)
    return pl.pallas_call(
        body, name=name,
        out_shape=[jax.ShapeDtypeStruct((N_DEV,) + a.shape, a.dtype) for a in arrs],
        in_specs=[any_spec] * n, out_specs=[any_spec] * n,
        scratch_shapes=[pltpu.SemaphoreType.DMA((n_items, 7)), pltpu.SemaphoreType.DMA((n_items, 7)),
                        pltpu.SemaphoreType.DMA((n_items,))],
        compiler_params=pltpu.CompilerParams(has_side_effects=True),
    )(*arrs)


def _peers(x, y, c):
    out = []
    for m in range(1, N_DEV):
        px = 1 - x if m & 4 else x
        py = 1 - y if m & 2 else y
        pc = 1 - c if m & 1 else c
        out.append((m, (px, py, pc), 4 * px + 2 * py + pc))
    return out


def _split_copies(kinds, src_refs, land_refs, send_sems, recv_sems):
    x, y, c = lax.axis_index("x"), lax.axis_index("y"), lax.axis_index("c")
    me = 4 * x + 2 * y + c
    copies = []
    for i, kind in enumerate(kinds):
        for m, peer, pidx in _peers(x, y, c):
            copies.append(pltpu.make_async_remote_copy(
                src_ref=src_refs[i].at[pidx] if kind == "scatter" else src_refs[i],
                dst_ref=land_refs[i].at[me],
                send_sem=send_sems.at[i * (N_DEV - 1) + m - 1],
                recv_sem=recv_sems.at[i * (N_DEV - 1) + m - 1],
                device_id=peer,
                device_id_type=pl.DeviceIdType.MESH,
            ))
    return copies


_HBM_SPEC = pl.BlockSpec(memory_space=pltpu.HBM)
_SEM_SPEC = pl.BlockSpec(memory_space=pltpu.SEMAPHORE)
_DATAFLOW = pltpu.SideEffectType.DATAFLOW_SIDE_EFFECTING


def _exchange_start(name, arrs, kinds):
    n = len(arrs)
    lands = []
    for a, kind in zip(arrs, kinds):
        shp = a.shape if kind == "scatter" else (N_DEV,) + a.shape
        lands.append(lax.empty(shp, a.dtype))

    def body(*refs):
        src_refs, land_refs = refs[:n], refs[n:2 * n]
        send_sems, recv_sems = refs[2 * n:2 * n + 2]
        token = refs[-1]
        for cp in _split_copies(kinds, src_refs, land_refs, send_sems, recv_sems):
            cp.start()
        token[...] = jnp.zeros_like(token)

    n_sem = n * (N_DEV - 1)
    hbm = lambda a: pltpu.HBM(a.shape, a.dtype)
    res = pl.pallas_call(
        body, name=name,
        out_shape=(pltpu.SemaphoreType.DMA((n_sem,)), pltpu.SemaphoreType.DMA((n_sem,)),
                   *[hbm(a) for a in arrs], *[hbm(a) for a in lands], jax.ShapeDtypeStruct((8, HD), F32)),
        in_specs=[_HBM_SPEC] * (2 * n),
        out_specs=(_SEM_SPEC, _SEM_SPEC, *[_HBM_SPEC] * (2 * n), pl.BlockSpec(memory_space=pltpu.VMEM)),
        input_output_aliases={i: 2 + i for i in range(2 * n)},
        compiler_params=pltpu.CompilerParams(has_side_effects=_DATAFLOW),
    )(*[pltpu.with_memory_space_constraint(a, pltpu.HBM) for a in arrs],
      *[pltpu.with_memory_space_constraint(a, pltpu.HBM) for a in lands])
    return (kinds, res[0], res[1], res[2:2 + n], res[2 + n:2 + 2 * n]), res[-1]


def _exchange_wait(name, state, after):
    kinds, send_sems, recv_sems, srcs, lands = state
    n = len(srcs)

    def body(*refs):
        src_refs, land_refs = refs[:n], refs[n:2 * n]
        send_sems_ref, recv_sems_ref = refs[2 * n:2 * n + 2]
        for cp in _split_copies(kinds, src_refs, land_refs, send_sems_ref, recv_sems_ref):
            cp.wait_send()
            cp.wait_recv()

    res = pl.pallas_call(
        body, name=name,
        out_shape=tuple(pltpu.HBM(a.shape, a.dtype) for a in (*srcs, *lands)),
        in_specs=[_HBM_SPEC] * (2 * n) + [_SEM_SPEC, _SEM_SPEC, pl.BlockSpec(memory_space=pl.ANY)],
        out_specs=tuple([_HBM_SPEC] * (2 * n)),
        input_output_aliases={i: i for i in range(2 * n)},
        compiler_params=pltpu.CompilerParams(has_side_effects=_DATAFLOW),
    )(*srcs, *lands, send_sems, recv_sems, after)
    me = 4 * lax.axis_index("x") + 2 * lax.axis_index("y") + lax.axis_index("c")
    outs = []
    for kind, src, land in zip(kinds, res[:n], res[n:]):
        own = lax.dynamic_index_in_dim(src, me, 0, keepdims=False) if kind == "scatter" else src
        outs.append(lax.dynamic_update_index_in_dim(land, own, me, 0))
    return outs


def _pick(n, cands):
    for t in cands:
        if n % t == 0:
            return t
    raise ValueError(f"no tile for {n}")


def _mm(name, a, b, mode, out_dtype, after=None):
    if mode == "nn":
        (m, k), (k2, n) = a.shape, b.shape
    elif mode == "nt":
        (m, k), (n, k2) = a.shape, b.shape
    else:
        (k, m), (k2, n) = a.shape, b.shape
    assert k == k2, (name, a.shape, b.shape)
    if mode == "tn":
        tm = _pick(m, (1024, 640, 512, 256, 128))
        tn = _pick(n, (1024, 640, 512, 256, 128))
        tk = _pick(k, (2048, 1024, 512, 256))
    else:
        tm, tn, tk = _pick(m, (512, 256)), n, k
    nk = k // tk

    def body(a_ref, b_ref, *rest):
        o_ref = rest[-2] if nk > 1 else rest[-1]
        av = a_ref[...].astype(BF16)
        bv = b_ref[...].astype(BF16)
        if mode == "nn":
            part = jnp.dot(av, bv, preferred_element_type=F32)
        elif mode == "nt":
            part = lax.dot_general(av, bv, NT_DIMS, preferred_element_type=F32)
        else:
            part = lax.dot_general(av, bv, TN_DIMS, preferred_element_type=F32)
        if nk == 1:
            o_ref[...] = part.astype(out_dtype)
            return
        acc_ref = rest[-1]
        kk = pl.program_id(2)

        @pl.when(kk == 0)
        def _():
            acc_ref[...] = part

        @pl.when(kk > 0)
        def _():
            acc_ref[...] += part

        @pl.when(kk == nk - 1)
        def _():
            o_ref[...] = acc_ref[...].astype(out_dtype)

    if mode == "tn":
        a_spec = pl.BlockSpec((tk, tm), lambda j, i, kk: (kk, i))
    else:
        a_spec = pl.BlockSpec((tm, tk), lambda j, i, kk: (i, kk))
    if mode == "nt":
        b_spec = pl.BlockSpec((tn, tk), lambda j, i, kk: (j, kk))
    else:
        b_spec = pl.BlockSpec((tk, tn), lambda j, i, kk: (kk, j))
    in_specs, args = [a_spec, b_spec], [a, b]
    if after is not None:
        in_specs.append(pl.BlockSpec((8, HD), lambda j, i, kk: (0, 0)))
        args.append(after)
    return pl.pallas_call(
        body,
        name=name,
        grid=(n // tn, m // tm, nk),
        in_specs=in_specs,
        out_specs=pl.BlockSpec((tm, tn), lambda j, i, kk: (i, j)),
        out_shape=jax.ShapeDtypeStruct((m, n), out_dtype),
        scratch_shapes=[pltpu.VMEM((tm, tn), F32)] if nk > 1 else [],
        compiler_params=_params(dimension_semantics=("parallel", "parallel", "arbitrary")),
    )(*args)


def _mm_cat(name, a_list, b, out_dtype, after=None):
    m = a_list[0].shape[0]
    ks = [a.shape[1] for a in a_list]
    n = b.shape[1]
    assert sum(ks) == b.shape[0], (name, ks, b.shape)
    tm = _pick(m, (512, 256))
    na = len(a_list)

    def body(*refs):
        b_ref, o_ref = refs[na], refs[-1]
        k0, acc = 0, None
        for a_ref, kw in zip(refs[:na], ks):
            part = jnp.dot(a_ref[...].astype(BF16), b_ref[k0:k0 + kw, :], preferred_element_type=F32)
            acc = part if acc is None else acc + part
            k0 += kw
        o_ref[...] = acc.astype(out_dtype)

    in_specs = [pl.BlockSpec((tm, kw), lambda i: (i, 0)) for kw in ks] + [pl.BlockSpec(b.shape, lambda i: (0, 0))]
    args = [*a_list, b]
    if after is not None:
        in_specs.append(pl.BlockSpec((8, HD), lambda i: (0, 0)))
        args.append(after)
    return pl.pallas_call(
        body, name=name, grid=(m // tm,),
        in_specs=in_specs, out_specs=pl.BlockSpec((tm, n), lambda i: (i, 0)),
        out_shape=jax.ShapeDtypeStruct((m, n), out_dtype),
        compiler_params=_params(dimension_semantics=("parallel",)),
    )(*args)


def _row(c, col=0):
    return pl.BlockSpec((TS, c), lambda i: (i, col))


def _vec(r, c):
    return pl.BlockSpec((r, c), lambda i: (0, 0))


def _prenorm(x, pre_gain):
    s = x.shape[0]

    def body(x_ref, g_ref, o_ref):
        xhat, _ = _rms_fwd(x_ref[...], g_ref[...])
        o_ref[...] = (xhat * g_ref[...]).astype(BF16)

    return pl.pallas_call(
        body, name="prenorm", grid=(s // TS,),
        in_specs=[_row(D), _vec(1, D)], out_specs=_row(D),
        out_shape=jax.ShapeDtypeStruct((s, D), BF16),
        compiler_params=_params(dimension_semantics=("parallel",)),
    )(x, pre_gain)


def _forget_fwd(zr, bf_pad):
    s = zr.shape[0]
    n = s // TQ

    def body(fl_ref, b_ref, kx_ref, c_buf, carry):
        i = pl.program_id(0)

        @pl.when(i == 0)
        def _():
            carry[...] = jnp.zeros_like(carry)

        fl = fl_ref[...] + b_ref[...]
        ls = jnp.minimum(fl, 0.0) - jnp.log(1.0 + jnp.exp(-jnp.abs(fl)))
        c_buf[...] = _cumsum_fwd(ls) + carry[0:1, :]
        carry[0:1, :] = c_buf[TQ - 1:TQ, :]
        cv = c_buf[...]
        for h in range(NH):
            kx_ref[h] = _bias_lanes(jnp.broadcast_to(cv[:, 8 * h:8 * h + 1], (TQ, HD)) * (-1.0 / SCALE), 0, 3)

    return pl.pallas_call(
        body, name="forget_fwd", grid=(n,),
        in_specs=[pl.BlockSpec((TQ, HD), lambda i: (i, FL_COL // HD)), _vec(1, HD)],
        out_specs=pl.BlockSpec((NH, TQ, HD), lambda i: (0, i, 0)),
        out_shape=jax.ShapeDtypeStruct((NH, s, HD), BF16),
        scratch_shapes=[pltpu.VMEM((TQ, HD), F32), pltpu.VMEM((8, HD), F32)],
        compiler_params=_params(dimension_semantics=("arbitrary",)),
    )(zr, bf_pad)


def _attn_fwd(zq, kx):
    s = zq.shape[0]
    n = s // TQ
    nb = TQ // HD

    def body(q_ref, k_ref, v_ref, kx_ref, o_ref, ax_ref):
        i = pl.program_id(1)
        lane = lax.broadcasted_iota(jnp.int32, (TQ, HD), 1)
        row = lax.broadcasted_iota(jnp.int32, (TQ, HD), 0)
        qa = jnp.concatenate([q_ref[...], jnp.where(lane < 3, 1.0, 0.0).astype(BF16)], axis=1)

        def qk(j):
            rows = pl.ds(pl.multiple_of(j * TQ, TQ), TQ)
            ka = jnp.concatenate([k_ref[rows, :], kx_ref[0, rows, :]], axis=1)
            return lax.dot_general(qa, ka, NT_DIMS, preferred_element_type=F32)

        def step(j, carry, u, masked):
            m, l, acc = carry
            rows = pl.ds(pl.multiple_of(j * TQ, TQ), TQ)
            us = [u[:, HD * b:HD * (b + 1)] for b in range(nb)]
            if masked:
                us = [jnp.where(row >= lane + HD * b, us[b], NEG) for b in range(nb)]
            bm = functools.reduce(jnp.maximum, us)
            m_new = jnp.maximum(m, jnp.max(bm, axis=1, keepdims=True))
            alpha = jnp.exp2((m - m_new) * EXP2_SCALE)
            shift = m_new * EXP2_SCALE
            ps = [jnp.exp2(ub * EXP2_SCALE - shift) for ub in us]
            l = alpha * l + functools.reduce(jnp.add, ps)
            pr = jnp.concatenate(ps, axis=1).astype(BF16)
            acc = alpha * acc + jnp.dot(pr, v_ref[rows, :], preferred_element_type=F32)
            return m_new, l, acc

        init = (jnp.full((TQ, HD), NEG, F32), jnp.zeros((TQ, HD), F32), jnp.zeros((TQ, HD), F32))
        carry = lax.fori_loop(0, i, lambda j, cr: step(j, cr, qk(j), False), init)
        m, l, acc = step(i, carry, qk(i), True)
        l_row = jnp.sum(l, axis=1, keepdims=True)
        o_ref[...] = acc / l_row
        ax_ref[0] = _bias_lanes(-(m + jnp.log(l_row) * (1.0 / SCALE)), 3, 0)

    return pl.pallas_call(
        body, name="attn_fwd", grid=(NH, n),
        in_specs=[
            pl.BlockSpec((TQ, HD), lambda h, i: (i, h)),
            pl.BlockSpec((s, HD), lambda h, i: (0, NH + h)),
            pl.BlockSpec((s, HD), lambda h, i: (0, 2 * NH + h)),
            pl.BlockSpec((1, s, HD), lambda h, i: (h, 0, 0)),
        ],
        out_specs=[pl.BlockSpec((TQ, HD), lambda h, i: (i, h)), pl.BlockSpec((1, TQ, HD), lambda h, i: (h, i, 0))],
        out_shape=[jax.ShapeDtypeStruct((s, D), F32), jax.ShapeDtypeStruct((NH, s, HD), BF16)],
        compiler_params=_params(dimension_semantics=("parallel", "parallel")),
    )(zq, zq, zq, kx)


def _attn_bwd(zq, do, ax, delta, kx, after):
    s = zq.shape[0]
    n = s // TQ
    nb = TQ // HD

    def body(k_ref, v_ref, kx_ref, q_ref, ax_ref, do_ref, dl_ref, after_ref, dq_ref, dk_ref, dv_ref, dcs_ref, drs_ref):
        j = pl.program_id(1)

        @pl.when(j == 0)
        def _():
            dq_ref[...] = jnp.zeros_like(dq_ref)
            drs_ref[...] = jnp.zeros_like(drs_ref)

        k = k_ref[...]
        v = v_ref[...]
        ka = jnp.concatenate([k, kx_ref[0]], axis=1)
        row = lax.broadcasted_iota(jnp.int32, (TQ, HD), 0)
        lane = lax.broadcasted_iota(jnp.int32, (TQ, HD), 1)

        def products(i):
            rows = pl.ds(pl.multiple_of(i * TQ, TQ), TQ)
            qa = jnp.concatenate([q_ref[rows, :], ax_ref[0, rows, :]], axis=1)
            return (lax.dot_general(qa, ka, NT_DIMS, preferred_element_type=F32),
                    lax.dot_general(do_ref[rows, :], v, NT_DIMS, preferred_element_type=F32))

        def step(i, carry, u, dp, masked):
            dk, dv, dcs = carry
            rows = pl.ds(pl.multiple_of(i * TQ, TQ), TQ)
            q = q_ref[rows, :]
            dout = do_ref[rows, :]
            dlv = dl_ref[0, rows, :]
            prs, dss = [], []
            for b in range(nb):
                cs = slice(HD * b, HD * (b + 1))
                ub = u[:, cs]
                if masked:
                    ub = jnp.where(row >= lane + HD * b, ub, NEG)
                pb = jnp.exp2(ub * EXP2_SCALE)
                prs.append(pb)
                dss.append(pb * (dp[:, cs] - dlv))
            drs_ref[0, rows, :] += functools.reduce(jnp.add, dss)
            ds = jnp.concatenate(dss, axis=1)
            dcs = dcs + jnp.sum(ds.reshape(TQ // 8, 8, TQ), axis=0)
            dsb = ds.astype(BF16)
            dv = dv + lax.dot_general(jnp.concatenate(prs, axis=1).astype(BF16), dout, TN_DIMS, preferred_element_type=F32)
            dk = dk + lax.dot_general(dsb, q, TN_DIMS, preferred_element_type=F32)
            dq_ref[rows, :] += jnp.dot(dsb, k, preferred_element_type=F32) * SCALE
            return dk, dv, dcs

        init = (jnp.zeros((TQ, HD), F32), jnp.zeros((TQ, HD), F32), jnp.zeros((8, TQ), F32))
        carry = step(j, init, *products(j), True)
        dk, dv, dcs = lax.fori_loop(j + 1, n, lambda i, cr: step(i, cr, *products(i), False), carry)
        dk_ref[...] = (dk * SCALE).astype(BF16)
        dv_ref[...] = dv.astype(BF16)
        dcs_ref[0] = jnp.broadcast_to(_colsum(dcs), (8, TQ))

    return pl.pallas_call(
        body, name="attn_bwd", grid=(NH, n),
        in_specs=[
            pl.BlockSpec((TQ, HD), lambda h, j: (j, NH + h)),
            pl.BlockSpec((TQ, HD), lambda h, j: (j, 2 * NH + h)),
            pl.BlockSpec((1, TQ, HD), lambda h, j: (h, j, 0)),
            pl.BlockSpec((s, HD), lambda h, j: (0, h)),
            pl.BlockSpec((1, s, HD), lambda h, j: (h, 0, 0)),
            pl.BlockSpec((s, HD), lambda h, j: (0, h)),
            pl.BlockSpec((1, s, HD), lambda h, j: (h, 0, 0)),
            pl.BlockSpec((8, HD), lambda h, j: (0, 0)),
        ],
        out_specs=[
            pl.BlockSpec((s, HD), lambda h, j: (0, h)),
            pl.BlockSpec((TQ, HD), lambda h, j: (j, h)),
            pl.BlockSpec((TQ, HD), lambda h, j: (j, h)),
            pl.BlockSpec((1, 8, TQ), lambda h, j: (j, h, 0)),
            pl.BlockSpec((1, s, HD), lambda h, j: (h, 0, 0)),
        ],
        out_shape=[
            jax.ShapeDtypeStruct((s, D), F32),
            jax.ShapeDtypeStruct((s, D), BF16),
            jax.ShapeDtypeStruct((s, D), BF16),
            jax.ShapeDtypeStruct((n, 8 * NH, TQ), F32),
            jax.ShapeDtypeStruct((NH, s, HD), F32),
        ],
        compiler_params=_params(dimension_semantics=("parallel", "arbitrary")),
    )(zq, zq, kx, zq, ax, do, delta, after)


def _forget_bwd(dcs, drs, zr, bf_pad):
    n = dcs.shape[0]
    s = n * TQ

    def body(dcs_ref, drs_ref, fl_ref, b_ref, dfl_ref, gb_ref, buf, carry):
        i = pl.program_id(0)

        @pl.when(i == 0)
        def _():
            carry[...] = jnp.zeros_like(carry)
            gb_ref[...] = jnp.zeros_like(gb_ref)

        dc_t = jnp.concatenate([dcs_ref[0], jnp.zeros((HD - 8 * NH, TQ), F32)], axis=0)
        lane = lax.broadcasted_iota(jnp.int32, (TQ, HD), 1)
        dc = -dc_t.T
        for hh in range(NH):
            dc = dc + jnp.where(lane == 8 * hh, jnp.sum(drs_ref[hh], axis=1, keepdims=True), 0.0)
        buf[...] = _cumsum_bwd(dc) + carry[0:1, :]
        carry[0:1, :] = buf[0:1, :]
        fl = fl_ref[...] + b_ref[...]
        dfl = buf[...] * _sigmoid_rel(-fl)
        dfl_ref[...] = dfl.astype(BF16)
        gb_ref[...] += _colsum(dfl)

    return pl.pallas_call(
        body, name="forget_bwd", grid=(n,),
        in_specs=[
            pl.BlockSpec((1, 8 * NH, TQ), lambda i: (n - 1 - i, 0, 0)),
            pl.BlockSpec((NH, TQ, HD), lambda i: (0, n - 1 - i, 0)),
            pl.BlockSpec((TQ, HD), lambda i: (n - 1 - i, FL_COL // HD)),
            _vec(1, HD),
        ],
        out_specs=[pl.BlockSpec((TQ, HD), lambda i: (n - 1 - i, 0)), _vec(1, HD)],
        out_shape=[jax.ShapeDtypeStruct((s, HD), BF16), jax.ShapeDtypeStruct((1, HD), F32)],
        scratch_shapes=[pltpu.VMEM((TQ, HD), F32), pltpu.VMEM((8, HD), F32)],
        compiler_params=_params(dimension_semantics=("arbitrary",)),
    )(dcs, drs, zr, bf_pad)


def _gates(xc, w_ref, b, sigmoid):
    xb = xc.astype(BF16)
    pre = jnp.concatenate(
        [jnp.dot(xb[:, HD * g:HD * (g + 1)], w_ref[g], preferred_element_type=F32) for g in range(NH)], axis=1)
    return sigmoid(pre + b)


def _lru_coeffs(r, lam):
    sp = jnp.maximum(-lam, 0.0) + jnp.log(1.0 + jnp.exp(-jnp.abs(lam)))
    log_a = -LRU_C * r * sp
    a = jnp.exp(log_a)
    y = 2.0 * log_a
    em1 = jnp.where(jnp.abs(y) < 0.01, y * (1.0 + y * (0.5 + y * (1.0 / 6.0))), jnp.exp(y) - 1.0)
    em = -em1
    return sp, a, em * lax.rsqrt(jnp.maximum(em, 1e-37))


def _conv_taps(ext, t):
    return [_shift_down(ext, CONV_W - 1 - jj, t) for jj in range(CONV_W)]


def _lru_fwd(zr, conv_w8, conv_b, w_r, b_r, w_i, b_i, lam):
    s = zr.shape[0]
    n = s // TS
    xl_col = 1

    def body(xl_ref, halo_ref, cw_ref, cb_ref, wr_ref, br_ref, wi_ref, bi_ref, lam_ref, xc_ref, h_ref, carry):
        i = pl.program_id(0)

        @pl.when(i == 0)
        def _():
            carry[...] = jnp.zeros_like(carry)

        halo = jnp.where(i == 0, 0.0, halo_ref[...])
        taps = _conv_taps(jnp.concatenate([halo, xl_ref[...]], axis=0), TS)
        xc = cb_ref[...] + sum(cw_ref[jj:jj + 1, :] * taps[jj] for jj in range(CONV_W))
        xc_ref[...] = xc
        r = _gates(xc, wr_ref, br_ref[...], _sigmoid_rel)
        ig = _gates(xc, wi_ref, bi_ref[...], _sigmoid)
        _, a, gam = _lru_coeffs(r, lam_ref[...])
        a_cum, h_loc = _scan(a, gam * (ig * xc), False)
        h_ref[...] = h_loc + a_cum * carry[0:1, :]
        carry[0:1, :] = h_ref[TS - 1:TS, :]

    return pl.pallas_call(
        body, name="lru_fwd", grid=(n,),
        in_specs=[
            _row(D, xl_col),
            pl.BlockSpec((8, D), lambda i: (jnp.maximum(i * (TS // 8) - 1, 0), xl_col)),
            _vec(8, D), _vec(1, D),
            pl.BlockSpec((NH, HD, HD), lambda i: (0, 0, 0)), _vec(1, D),
            pl.BlockSpec((NH, HD, HD), lambda i: (0, 0, 0)), _vec(1, D),
            _vec(1, D),
        ],
        out_specs=[_row(D), _row(D)],
        out_shape=[jax.ShapeDtypeStruct((s, D), F32), jax.ShapeDtypeStruct((s, D), F32)],
        scratch_shapes=[pltpu.VMEM((8, D), F32)],
        compiler_params=_params(dimension_semantics=("arbitrary",)),
    )(zr, zr, conv_w8, conv_b, w_r, b_r, w_i, b_i, lam)


def _lru_bwd(zr, xc, h, dh, conv_w8, w_r, b_r, w_i, b_i, lam):
    s = zr.shape[0]
    n = s // TS
    xl_col = 1

    def rev(i):
        return n - 1 - i

    def body(xl_ref, xlh_ref, xc_ref, h_ref, hh_ref, dh_ref, cw_ref, wr_ref, br_ref, wi_ref, bi_ref, lam_ref,
             dxl_ref, gwr_ref, gwi_ref, gbr_ref, gbi_ref, glam_ref, gcb_ref, gcw_ref, l_buf, dxc_buf, carry_g, carry_dxc):
        i = pl.program_id(0)
        first = rev(i) == 0

        @pl.when(i == 0)
        def _():
            carry_g[...] = jnp.zeros_like(carry_g)
            carry_dxc[...] = jnp.zeros_like(carry_dxc)
            for ref in (gwr_ref, gwi_ref, gbr_ref, gbi_ref, glam_ref, gcb_ref, gcw_ref):
                ref[...] = jnp.zeros_like(ref)

        rows = _rows_iota(TS)
        xc = xc_ref[...]
        lam = lam_ref[...]
        r = _gates(xc, wr_ref, br_ref[...], _sigmoid_rel)
        ig = _gates(xc, wi_ref, bi_ref[...], _sigmoid)
        sp, a, gam = _lru_coeffs(r, lam)
        g = dh_ref[...] + jnp.where(rows == TS - 1, carry_g[0:1, :], 0.0)
        b = jnp.where(rows == TS - 1, 0.0, pltpu.roll(a, TS - 1, 0))
        l_buf[...] = _scan(b, g, True)[1]
        lv = l_buf[...]
        carry_g[0:1, :] = l_buf[0:1, :] * a[0:1, :]
        h_prev_row = jnp.where(first, 0.0, hh_ref[7:8, :])
        h_prev = jnp.where(rows == 0, h_prev_row, pltpu.roll(h_ref[...], 1, 0))
        dgam = lv * ig * xc
        dig = lv * gam * xc
        dxc = lv * gam * ig
        dla = lv * h_prev * a - dgam * (a * a) / gam
        dr = dla * (-LRU_C) * sp
        glam_ref[...] += _colsum(dla * r) * (LRU_C * _sigmoid_rel(-lam))
        dpr = dr * r * (1.0 - r)
        dpi = dig * ig * (1.0 - ig)
        gbr_ref[...] += _colsum(dpr)
        gbi_ref[...] += _colsum(dpi)
        xb = xc.astype(BF16)
        dprb = dpr.astype(BF16)
        dpib = dpi.astype(BF16)
        back = []
        for gi in range(NH):
            cs = slice(HD * gi, HD * (gi + 1))
            gwr_ref[gi] += lax.dot_general(xb[:, cs], dprb[:, cs], TN_DIMS, preferred_element_type=F32)
            gwi_ref[gi] += lax.dot_general(xb[:, cs], dpib[:, cs], TN_DIMS, preferred_element_type=F32)
            back.append(lax.dot_general(dprb[:, cs], wr_ref[gi], NT_DIMS, preferred_element_type=F32)
                        + lax.dot_general(dpib[:, cs], wi_ref[gi], NT_DIMS, preferred_element_type=F32))
        dxc = dxc + jnp.concatenate(back, axis=1)
        dxc_buf[...] = dxc
        gcb_ref[...] += _colsum(dxc)
        halo = jnp.where(first, 0.0, xlh_ref[...])
        taps = _conv_taps(jnp.concatenate([halo, xl_ref[...]], axis=0), TS)
        for jj in range(CONV_W):
            gcw_ref[jj:jj + 1, :] += _colsum(dxc * taps[jj])
        ext = jnp.concatenate([dxc, carry_dxc[...]], axis=0)
        dxl = sum(cw_ref[jj:jj + 1, :] * _shift_up(ext, CONV_W - 1 - jj, TS) for jj in range(CONV_W))
        dxl_ref[...] = dxl.astype(BF16)
        carry_dxc[...] = dxc_buf[0:8, :]

    rowr = lambda c, col=0: pl.BlockSpec((TS, c), lambda i: (rev(i), col))
    halo = lambda col: pl.BlockSpec((8, D), lambda i: (jnp.maximum(rev(i) * (TS // 8) - 1, 0), col))
    gate_w = pl.BlockSpec((NH, HD, HD), lambda i: (0, 0, 0))
    return pl.pallas_call(
        body, name="lru_bwd", grid=(n,),
        in_specs=[rowr(D, xl_col), halo(xl_col), rowr(D), rowr(D), halo(0), rowr(D),
                  _vec(8, D), gate_w, _vec(1, D), gate_w, _vec(1, D), _vec(1, D)],
        out_specs=[rowr(D), gate_w, gate_w, _vec(1, D), _vec(1, D), _vec(1, D), _vec(1, D), _vec(8, D)],
        out_shape=[
            jax.ShapeDtypeStruct((s, D), BF16),
            jax.ShapeDtypeStruct((NH, HD, HD), F32), jax.ShapeDtypeStruct((NH, HD, HD), F32),
            jax.ShapeDtypeStruct((1, D), F32), jax.ShapeDtypeStruct((1, D), F32), jax.ShapeDtypeStruct((1, D), F32),
            jax.ShapeDtypeStruct((1, D), F32), jax.ShapeDtypeStruct((8, D), F32),
        ],
        scratch_shapes=[pltpu.VMEM((TS, D), F32), pltpu.VMEM((TS, D), F32), pltpu.VMEM((8, D), F32), pltpu.VMEM((8, D), F32)],
        compiler_params=_params(dimension_semantics=("arbitrary",)),
    )(zr, zr, xc, h, h, dh, conv_w8, w_r, b_r, w_i, b_i, lam)


def _silu_parts(g):
    sg = _sigmoid(g)
    return g * sg, sg * (1.0 + g * (1.0 - sg))


def _branch_out(o, h, zr, gain_a, gain_l):
    s = o.shape[0]

    def body(o_ref, ga_ref, h_ref, gl_ref, ka_ref, kl_ref, y_ref):
        ohat, _ = _rms_fwd(o_ref[...], None)
        y_ref[:, 0:D] = (ohat * ka_ref[...] * _silu_parts(ga_ref[...])[0]).astype(BF16)
        hhat, _ = _rms_fwd(h_ref[...], None)
        y_ref[:, D:2 * D] = (hhat * kl_ref[...] * _silu_parts(gl_ref[...])[0]).astype(BF16)

    return pl.pallas_call(
        body, name="branch_out", grid=(s // TS,),
        in_specs=[_row(D), _row(D, 0), _row(D), _row(D, 2), _vec(1, D), _vec(1, D)],
        out_specs=_row(2 * D),
        out_shape=jax.ShapeDtypeStruct((s, 2 * D), BF16),
        compiler_params=_params(dimension_semantics=("parallel",)),
    )(o, zr, h, zr, gain_a, gain_l)


def _branch_out_bwd(o, h, zr, dycat, gain_a, gain_l):
    s = o.shape[0]

    def body(o_ref, ga_ref, h_ref, gl_ref, dya_ref, dyl_ref, ka_ref, kl_ref,
             do_ref, dl_ref, dga_ref, dh_ref, dgl_ref, gka_ref, gkl_ref):
        @pl.when(pl.program_id(0) == 0)
        def _():
            gka_ref[...] = jnp.zeros_like(gka_ref)
            gkl_ref[...] = jnp.zeros_like(gkl_ref)

        def one(v, g, dy, gain):
            vhat, rstd = _rms_fwd(v, None)
            sg, dsg = _silu_parts(g)
            dn = dy * sg
            dg = dy * (vhat * gain) * dsg
            return _rms_bwd(vhat, rstd, dn * gain), dg, _colsum(dn * vhat)

        o = o_ref[...]
        dout, dga, gka = one(o, ga_ref[...], dya_ref[...], ka_ref[...])
        do_ref[...] = dout.astype(BF16)
        dga_ref[...] = dga.astype(BF16)
        gka_ref[...] += gka
        prod = dout * o
        for hh in range(NH):
            dl_ref[hh] = jnp.broadcast_to(jnp.sum(prod[:, HD * hh:HD * (hh + 1)], axis=1, keepdims=True), (TS, HD))
        dh, dgl, gkl = one(h_ref[...], gl_ref[...], dyl_ref[...], kl_ref[...])
        dh_ref[...] = dh
        dgl_ref[...] = dgl.astype(BF16)
        gkl_ref[...] += gkl

    return pl.pallas_call(
        body, name="branch_out_bwd", grid=(s // TS,),
        in_specs=[_row(D), _row(D, 0), _row(D), _row(D, 2), _row(D, 0), _row(D, 1), _vec(1, D), _vec(1, D)],
        out_specs=[_row(D), pl.BlockSpec((NH, TS, HD), lambda i: (0, i, 0)), _row(D), _row(D), _row(D), _vec(1, D), _vec(1, D)],
        out_shape=[
            jax.ShapeDtypeStruct((s, D), BF16), jax.ShapeDtypeStruct((NH, s, HD), F32), jax.ShapeDtypeStruct((s, D), BF16),
            jax.ShapeDtypeStruct((s, D), F32), jax.ShapeDtypeStruct((s, D), BF16),
            jax.ShapeDtypeStruct((1, D), F32), jax.ShapeDtypeStruct((1, D), F32),
        ],
        compiler_params=_params(dimension_semantics=("arbitrary",)),
    )(o, zr, h, zr, dycat, dycat, gain_a, gain_l)


def _residual(x, mix, post_gain):
    s = x.shape[0]

    def body(x_ref, m_ref, g_ref, h_ref, hb_ref):
        mhat, _ = _rms_fwd(m_ref[...], None)
        h1 = x_ref[...] + mhat * g_ref[...]
        h_ref[...] = h1
        hb_ref[...] = h1.astype(BF16)

    return pl.pallas_call(
        body, name="residual", grid=(s // TS,),
        in_specs=[_row(D), _row(D), _vec(1, D)], out_specs=[_row(D), _row(D)],
        out_shape=[jax.ShapeDtypeStruct((s, D), F32), jax.ShapeDtypeStruct((s, D), BF16)],
        compiler_params=_params(dimension_semantics=("parallel",)),
    )(x, mix, post_gain)


def _head(h1, pe, gp, tgt, ple_gain, b_gate):
    s = h1.shape[0]

    def body(h_ref, pe_ref, gp_ref, t_ref, kg_ref, b_ref, loss_ref, dy_ref, dgp_ref, dpe_ref, gk_ref, gb_ref):
        @pl.when(pl.program_id(0) == 0)
        def _():
            loss_ref[...] = jnp.zeros_like(loss_ref)
            gk_ref[...] = jnp.zeros_like(gk_ref)
            gb_ref[...] = jnp.zeros_like(gb_ref)

        ehat, rstd = _rms_fwd(pe_ref[...], None)
        e = ehat * kg_ref[...]
        gate = _sigmoid(gp_ref[...] + b_ref[...])
        diff = (h_ref[...] + gate * e) - t_ref[...]
        per_row = jnp.mean(diff * diff, axis=-1, keepdims=True)
        loss_ref[...] += 0.5 * jnp.sum(per_row, axis=0, keepdims=True)
        dy = diff * (1.0 / D)
        dy_ref[...] = dy
        dgp = dy * e * gate * (1.0 - gate)
        dgp_ref[...] = dgp.astype(BF16)
        gb_ref[...] += _colsum(dgp)
        de = dy * gate
        gk_ref[...] += _colsum(de * ehat)
        dpe_ref[...] = _rms_bwd(ehat, rstd, de * kg_ref[...]).astype(BF16)

    return pl.pallas_call(
        body, name="head", grid=(s // TS,),
        in_specs=[_row(D), _row(D), _row(D), _row(D), _vec(1, D), _vec(1, D)],
        out_specs=[_vec(1, 1), _row(D), _row(D), _row(D), _vec(1, D), _vec(1, D)],
        out_shape=[
            jax.ShapeDtypeStruct((1, 1), F32), jax.ShapeDtypeStruct((s, D), F32), jax.ShapeDtypeStruct((s, D), BF16),
            jax.ShapeDtypeStruct((s, D), BF16), jax.ShapeDtypeStruct((1, D), F32), jax.ShapeDtypeStruct((1, D), F32),
        ],
        compiler_params=_params(dimension_semantics=("arbitrary",)),
    )(h1, pe, gp, tgt, ple_gain, b_gate)


def _residual_bwd(dy, t, mix, post_gain):
    s = dy.shape[0]

    def body(dy_ref, t_ref, m_ref, g_ref, dh_ref, dm_ref, gg_ref):
        @pl.when(pl.program_id(0) == 0)
        def _():
            gg_ref[...] = jnp.zeros_like(gg_ref)

        dh1 = dy_ref[...] + t_ref[...]
        dh_ref[...] = dh1
        mhat, rstd = _rms_fwd(m_ref[...], None)
        gg_ref[...] += _colsum(dh1 * mhat)
        dm_ref[...] = _rms_bwd(mhat, rstd, dh1 * g_ref[...]).astype(BF16)

    return pl.pallas_call(
        body, name="residual_bwd", grid=(s // TS,),
        in_specs=[_row(D), _row(D), _row(D), _vec(1, D)], out_specs=[_row(D), _row(D), _vec(1, D)],
        out_shape=[jax.ShapeDtypeStruct((s, D), F32), jax.ShapeDtypeStruct((s, D), BF16), jax.ShapeDtypeStruct((1, D), F32)],
        compiler_params=_params(dimension_semantics=("arbitrary",)),
    )(dy, t, mix, post_gain)


def _prenorm_bwd(x, dxn_a, dxn_b, dh1, pre_gain):
    s = x.shape[0]

    def body(x_ref, da_ref, db_ref, dh_ref, g_ref, dx_ref, gg_ref):
        @pl.when(pl.program_id(0) == 0)
        def _():
            gg_ref[...] = jnp.zeros_like(gg_ref)

        xhat, rstd = _rms_fwd(x_ref[...], None)
        dxn = da_ref[...] + db_ref[...]
        gg_ref[...] += _colsum(dxn * xhat)
        dx_ref[...] = dh_ref[...] + _rms_bwd(xhat, rstd, dxn * g_ref[...])

    return pl.pallas_call(
        body, name="prenorm_bwd", grid=(s // TS,),
        in_specs=[_row(D), _row(D), _row(D), _row(D), _vec(1, D)], out_specs=[_row(D), _vec(1, D)],
        out_shape=[jax.ShapeDtypeStruct((s, D), F32), jax.ShapeDtypeStruct((1, D), F32)],
        compiler_params=_params(dimension_semantics=("arbitrary",)),
    )(x, dxn_a, dxn_b, dh1, pre_gain)


def _adamw(name, parts, w, m, v):
    r, c = w.shape
    if r % 8 == 0:
        tr = _pick(r, (256, 128, 16, 8))
        grid = (r // tr,)
        blk = pl.BlockSpec((tr, c), lambda i: (i, 0))
        parts_blk = pl.BlockSpec((N_DEV, tr, c), lambda i: (0, i, 0))
    else:
        tc = _pick(c, (256, 128))
        grid = (c // tc,)
        blk = pl.BlockSpec((r, tc), lambda i: (0, i))
        parts_blk = pl.BlockSpec((N_DEV, r, tc), lambda i: (0, 0, i))

    def body(p_ref, w_ref, m_ref, v_ref, g_ref, d_ref, nm_ref, nv_ref):
        g = p_ref[0].astype(F32)
        for j in range(1, N_DEV):
            g = g + p_ref[j].astype(F32)
        g_ref[...] = g
        nm = ADAM_B1 * m_ref[...] + (1.0 - ADAM_B1) * g
        nv = ADAM_B2 * v_ref[...] + (1.0 - ADAM_B2) * (g * g)
        nm_ref[...] = nm
        nv_ref[...] = nv
        m_hat = nm / (1.0 - ADAM_B1 ** ADAM_STEP)
        v_hat = nv / (1.0 - ADAM_B2 ** ADAM_STEP)
        d_ref[...] = -ADAM_LR * (m_hat / (jnp.sqrt(v_hat) + ADAM_EPS) + ADAM_WD * w_ref[...])

    return pl.pallas_call(
        body, name=name, grid=grid,
        in_specs=[parts_blk, blk, blk, blk],
        out_specs=[blk] * 4,
        out_shape=[jax.ShapeDtypeStruct((r, c), F32)] * 4,
        compiler_params=_params(dimension_semantics=("parallel",)),
    )(parts, w, m, v)


def _spread8(v):
    r = v.shape[0]
    return jnp.pad(jnp.pad(v[:, :, None], ((0, 0), (0, 0), (0, 7))).reshape(r, 8 * NH), ((0, 0), (0, HD - 8 * NH)))


def _gather8(v):
    return v[:, :8 * NH].reshape(v.shape[0], NH, 8)[:, :, 0]


def _cols_to_shards(g):
    r, c8 = g.shape
    return g.reshape(r, N_DEV, c8 // N_DEV).transpose(1, 0, 2)


def _shards_to_cols(g):
    n, r, c = g.shape
    return g.transpose(1, 0, 2).reshape(r, n * c)


def kernel(x, p, w_in, b_f, pre_gain, post_gain, conv_w, conv_b, w_rgate, b_rgate, w_igate, b_igate, lru_lambda, attn_out_gain, lru_out_gain, w_out, w_ple, ple_gain, w_ple_gate, b_ple_gate, loss_target, m_w_in, m_b_f, m_pre_gain, m_post_gain, m_conv_w, m_conv_b, m_w_rgate, m_b_rgate, m_w_igate, m_b_igate, m_lru_lambda, m_attn_out_gain, m_lru_out_gain, m_w_out, m_w_ple, m_ple_gain, m_w_ple_gate, m_b_ple_gate, v_w_in, v_b_f, v_pre_gain, v_post_gain, v_conv_w, v_conv_b, v_w_rgate, v_b_rgate, v_w_igate, v_b_igate, v_lru_lambda, v_attn_out_gain, v_lru_out_gain, v_w_out, v_w_ple, v_ple_gain, v_w_ple_gate, v_b_ple_gate):
    me = 4 * lax.axis_index("x") + 2 * lax.axis_index("y") + lax.axis_index("c")
    x2, p2, tgt = x[0], p[0, 0], loss_target[0]

    conv_w_shard8 = jnp.pad(conv_w[0], ((0, 8 - CONV_W), (0, 0)))
    wt, m_wt, v_wt = w_in[0].T, m_w_in[0].T, v_w_in[0].T
    g_wint, g_conv = _gather_two_level("gather_w_in", [wt.astype(BF16), conv_w_shard8], pieces=4)
    rest_state, rest_token = _exchange_start(
        "gather_rest_start", [w_out[0].astype(BF16), w_ple[0].astype(BF16), w_ple_gate[0].astype(BF16)], ["bcast"] * 3)
    win_t = g_wint.reshape(D_IN, D)
    w_qkv_t = win_t[:D_QKV]
    w_rest_t = jnp.concatenate([win_t[D_QKV + NH:], _spread8(win_t[D_QKV:D_QKV + NH].T).T], axis=0)
    conv_w8 = _shards_to_cols(g_conv)
    bf_pad = _spread8(b_f)
    w_r, w_i = w_rgate[0].astype(BF16), w_igate[0].astype(BF16)

    xn = _prenorm(x2, pre_gain + rest_token[0:1, 0:1])
    zq = _mm("proj_qkv", xn, w_qkv_t, "nt", BF16)
    zr = _mm("proj_rest", xn, w_rest_t, "nt", F32)
    kx = _forget_fwd(zr, bf_pad)
    o, ax = _attn_fwd(zq, kx)
    xc, h = _lru_fwd(zr, conv_w8, conv_b, w_r, b_rgate, w_i, b_igate, lru_lambda)
    ycat = _branch_out(o, h, zr, attn_out_gain, lru_out_gain)
    g_wout, g_wple, g_wpg = _exchange_wait("gather_rest_wait", rest_state, ycat)
    wout_full = g_wout.reshape(2 * D, D)
    wple_full = _shards_to_cols(g_wple)
    wpg_full = g_wpg.reshape(D, D)
    mix = _mm("proj_out", ycat, wout_full, "nn", F32)
    h1, h1b = _residual(x2, mix, post_gain)
    pe = _mm("proj_ple", p2, wple_full, "nn", F32)
    gp = _mm("proj_gate", h1b, wpg_full, "nn", F32)
    loss_part, dy, dgp, dpe, g_ple_gain, g_b_gate = _head(h1, pe, gp, tgt, ple_gain, b_ple_gate)
    loss = lax.psum(loss_part[0, 0], ("x", "y", "c"))

    t = _mm("bwd_gate_x", dgp, wpg_full, "nt", F32)
    gw_pg = _mm("bwd_gate_w", h1b, dgp, "tn", BF16)
    gw_ple = _mm("bwd_ple_w", p2, dpe, "tn", BF16)
    dh1, dmix, g_post_gain = _residual_bwd(dy, t, mix, post_gain)
    dycat = _mm("bwd_out_x", dmix, wout_full, "nt", F32)
    gw_out = _mm("bwd_out_w", ycat, dmix, "tn", BF16)
    do, delta, dga, dh, dgl, g_aog, g_log = _branch_out_bwd(o, h, zr, dycat, attn_out_gain, lru_out_gain)
    dxl, g_wr, g_wi, g_br, g_bi, g_lam, g_cb, g_cw8 = _lru_bwd(
        zr, xc, h, dh, conv_w8, w_r, b_rgate, w_i, b_igate, lru_lambda)
    gates = jnp.concatenate([g_wr.reshape(D, HD), g_wi.reshape(D, HD)], axis=0).astype(BF16)
    outw_state, outw_token = _exchange_start(
        "exchange_outw_start",
        [gw_out.reshape(N_DEV, 2 * D // N_DEV, D), _cols_to_shards(gw_ple), gw_pg.reshape(N_DEV, D // N_DEV, D), gates],
        ["scatter"] * 3 + ["bcast"])
    dq, dk, dv, dcs, drs = _attn_bwd(zq, do, ax, delta, kx, outw_token)
    dfl, g_bf_pad = _forget_bwd(dcs, drs, zr, bf_pad)
    gw_pieces = [_mm("bwd_w_" + nm, dz, xn, "tn", BF16) for nm, dz in
                 (("q", dq), ("k", dk), ("v", dv), ("fl", dfl), ("ga", dga), ("xl", dxl), ("gl", dgl))]
    gw_pieces[3] = _gather8(gw_pieces[3].T).T
    gw_in_t = jnp.concatenate(gw_pieces, axis=0)
    inw_state, inw_token = _exchange_start(
        "exchange_inw_start", [gw_in_t.reshape(N_DEV, D_IN_SHARD, D)], ["scatter"])
    dxn_a = _mm_cat("bwd_qkv_x", [dq, dk, dv], w_qkv_t, F32, after=inw_token)
    dxn_b = _mm_cat("bwd_rest_x", [dga, dxl, dgl, dfl], w_rest_t, F32, after=inw_token)
    grad_x, g_pre_gain = _prenorm_bwd(x2, dxn_a, dxn_b, dh1, pre_gain)

    upd = {}
    r_wout, r_wple, r_wpg, r_gates = _exchange_wait("exchange_outw_wait", outw_state, grad_x)
    upd["w_out"] = _adamw("adamw_w_out", r_wout, w_out[0], m_w_out[0], v_w_out[0])
    upd["w_ple"] = _adamw("adamw_w_ple", r_wple, w_ple[0], m_w_ple[0], v_w_ple[0])
    upd["w_ple_gate"] = _adamw("adamw_w_ple_gate", r_wpg, w_ple_gate[0], m_w_ple_gate[0], v_w_ple_gate[0])
    gates_of = lambda a, b: jnp.concatenate([a[0].reshape(D, HD), b[0].reshape(D, HD)], axis=0)
    g_gates = _adamw("adamw_gates", r_gates, gates_of(w_rgate, w_igate), gates_of(m_w_rgate, m_w_igate),
                     gates_of(v_w_rgate, v_w_igate))
    upd["w_rgate"] = [a[:D].reshape(1, NH, HD, HD) for a in g_gates]
    upd["w_igate"] = [a[D:].reshape(1, NH, HD, HD) for a in g_gates]
    small = jnp.concatenate(
        [jnp.pad(_gather8(g_bf_pad), ((0, 0), (0, D - NH))), g_pre_gain, g_post_gain, g_cb, g_br, g_bi, g_lam, g_aog, g_log,
         g_ple_gain, g_b_gate, g_cw8[:CONV_W], upd["w_out"][0][0:1]], axis=0)
    (r_small,) = _exchange("exchange_small", [small], ["bcast"])
    vec_names = ["b_f", "pre_gain", "post_gain", "conv_b", "b_rgate", "b_igate", "lru_lambda", "attn_out_gain",
                 "lru_out_gain", "ple_gain", "b_ple_gate"]
    vec_w = dict(b_f=(b_f, m_b_f, v_b_f), pre_gain=(pre_gain, m_pre_gain, v_pre_gain),
                 post_gain=(post_gain, m_post_gain, v_post_gain), conv_b=(conv_b, m_conv_b, v_conv_b),
                 b_rgate=(b_rgate, m_b_rgate, v_b_rgate), b_igate=(b_igate, m_b_igate, v_b_igate),
                 lru_lambda=(lru_lambda, m_lru_lambda, v_lru_lambda),
                 attn_out_gain=(attn_out_gain, m_attn_out_gain, v_attn_out_gain),
                 lru_out_gain=(lru_out_gain, m_lru_out_gain, v_lru_out_gain), ple_gain=(ple_gain, m_ple_gain, v_ple_gain),
                 b_ple_gate=(b_ple_gate, m_b_ple_gate, v_b_ple_gate))
    conv_mine = lambda a: lax.dynamic_slice_in_dim(a, me * HD, HD, axis=1)

    def small_rows(k):
        rows = [jnp.pad(vec_w[nm][k], ((0, 0), (0, D - vec_w[nm][k].shape[1]))) for nm in vec_names]
        cw = (conv_w, m_conv_w, v_conv_w)[k][0]
        full = lax.dynamic_update_slice_in_dim(jnp.ones((CONV_W, D), F32), cw, me * HD, axis=1)
        return jnp.concatenate(rows + [full, jnp.ones((1, D), F32)], axis=0)

    g_small = _adamw("adamw_small", r_small, small_rows(0), small_rows(1), small_rows(2))
    for idx, nm in enumerate(vec_names):
        width = vec_w[nm][0].shape[1]
        upd[nm] = [a[idx:idx + 1, :width] for a in g_small]
    base = len(vec_names)
    upd["conv_w"] = [conv_mine(a[base:base + CONV_W])[None] for a in g_small]
    (r_win,) = _exchange_wait("exchange_inw_wait", inw_state, g_small[0])
    upd["w_in"] = [a.T for a in _adamw("adamw_w_in", r_win, wt, m_wt, v_wt)]
    for nm in ("w_in", "w_out", "w_ple", "w_ple_gate"):
        upd[nm] = [a[None] for a in upd[nm]]

    order = ["w_in", "b_f", "pre_gain", "post_gain", "conv_w", "conv_b", "w_rgate", "b_rgate", "w_igate", "b_igate",
             "lru_lambda", "attn_out_gain", "lru_out_gain", "w_out", "w_ple", "ple_gain", "w_ple_gate", "b_ple_gate"]
    outs = [loss, grad_x[None]]
    for k in range(4):
        outs += [upd[nm][k] for nm in order]
    return tuple(outs)
```

```python
import functools

import jax
import jax.numpy as jnp
from jax import lax
from jax.experimental import pallas as pl
from jax.experimental.pallas import tpu as pltpu

F32 = jnp.float32
BF16 = jnp.bfloat16

N_DEV = 8
D = 1024
HD = 128
NH = 8
D_IN = 6152
D_IN_SHARD = D_IN // N_DEV
D_QKV = 3 * D
D_REST = 3 * D + HD
FL_COL = 3 * D
D_PLE = 256
CONV_W = 4
LRU_C = 8.0
RMS_EPS = 1e-6
SCALE = HD ** -0.5
EXP2_SCALE = SCALE * 1.4426950408889634
NEG = -1e30

ADAM_LR = 0.001
ADAM_B1 = 0.9
ADAM_B2 = 0.999
ADAM_EPS = 1e-08
ADAM_WD = 0.01
ADAM_STEP = 10

TS = 256
TQ = 1024
VMEM_LIMIT = 48 * 1024 * 1024

NT_DIMS = (((1,), (1,)), ((), ()))
TN_DIMS = (((0,), (0,)), ((), ()))


def _params(**kw):
    return pltpu.CompilerParams(vmem_limit_bytes=VMEM_LIMIT, **kw)


def _sigmoid(v):
    return 0.5 * jnp.tanh(0.5 * v) + 0.5


def _sigmoid_rel(v):
    return 1.0 / (1.0 + jnp.exp(-v))


def _rms_fwd(v, gain):
    rstd = lax.rsqrt(jnp.mean(v * v, axis=-1, keepdims=True) + RMS_EPS)
    return v * rstd, rstd


def _rms_bwd(vhat, rstd, dvhat):
    return rstd * (dvhat - vhat * jnp.mean(dvhat * vhat, axis=-1, keepdims=True))


def _colsum(v):
    return jnp.sum(v, axis=0, keepdims=True)


def _rows_iota(t):
    return lax.broadcasted_iota(jnp.int32, (t, 1), 0)


def _scan(a, u, reverse):
    t, c = a.shape
    rows = _rows_iota(t)
    d = 1
    while d < t:
        if d < 8:
            valid = rows < t - d if reverse else rows >= d
            shift = t - d if reverse else d
            u = jnp.where(valid, u + a * pltpu.roll(u, shift, 0), u)
            a = jnp.where(valid, a * pltpu.roll(a, shift, 0), a)
        else:
            zeros, ones = jnp.zeros((d, c), F32), jnp.ones((d, c), F32)
            if reverse:
                u_far, a_far = jnp.concatenate([u[d:], zeros], axis=0), jnp.concatenate([a[d:], ones], axis=0)
            else:
                u_far, a_far = jnp.concatenate([zeros, u[:t - d]], axis=0), jnp.concatenate([ones, a[:t - d]], axis=0)
            u = u + a * u_far
            a = a * a_far
        d *= 2
    return a, u


def _cumsum_fwd(v):
    t = v.shape[0]
    rows = _rows_iota(t)
    d = 1
    while d < t:
        v = jnp.where(rows >= d, v + pltpu.roll(v, d, 0), v)
        d *= 2
    return v


def _cumsum_bwd(v):
    t = v.shape[0]
    rows = _rows_iota(t)
    d = 1
    while d < t:
        v = jnp.where(rows < t - d, v + pltpu.roll(v, t - d, 0), v)
        d *= 2
    return v


def _bias_lanes(v, at, ones_at):
    lane = lax.broadcasted_iota(jnp.int32, v.shape, 1)
    hi = v.astype(BF16).astype(F32)
    mid = (v - hi).astype(BF16).astype(F32)
    lo = ((v - hi) - mid).astype(BF16).astype(F32)
    out = jnp.where((lane >= ones_at) & (lane < ones_at + 3), 1.0, 0.0)
    for k, piece in enumerate((hi, mid, lo)):
        out = jnp.where(lane == at + k, piece, out)
    return out.astype(BF16)


def _shift_down(ext, k, t):
    return pltpu.roll(ext, k, 0)[8:, :] if k else ext[8:, :]


def _shift_up(ext, k, t):
    return pltpu.roll(ext, t + 8 - k, 0)[:t, :] if k else ext[:t, :]


def _exchange(name, arrs, kinds):
    n = len(arrs)
    out_shape = []
    for a, kind in zip(arrs, kinds):
        shp = a.shape if kind == "scatter" else (N_DEV,) + a.shape
        out_shape.append(jax.ShapeDtypeStruct(shp, a.dtype))

    def body(*refs):
        ins, outs = refs[:n], refs[n:2 * n]
        send_sems, recv_sems, local_sems = refs[2 * n:]
        x, y, c = lax.axis_index("x"), lax.axis_index("y"), lax.axis_index("c")
        me = 4 * x + 2 * y + c
        copies = []
        for i in range(n):
            scatter = kinds[i] == "scatter"
            mine = pltpu.make_async_copy(ins[i].at[me] if scatter else ins[i], outs[i].at[me], local_sems.at[i])
            mine.start()
            copies.append(mine)
            for m in range(1, N_DEV):
                px = 1 - x if m & 4 else x
                py = 1 - y if m & 2 else y
                pc = 1 - c if m & 1 else c
                peer = 4 * px + 2 * py + pc
                cp = pltpu.make_async_remote_copy(
                    src_ref=ins[i].at[peer] if scatter else ins[i],
                    dst_ref=outs[i].at[me],
                    send_sem=send_sems.at[i, m - 1],
                    recv_sem=recv_sems.at[i, m - 1],
                    device_id=(px, py, pc),
                    device_id_type=pl.DeviceIdType.MESH,
                )
                cp.start()
                copies.append(cp)
        for cp in copies:
            cp.wait()

    any_spec = pl.BlockSpec(memory_space=pl.ANY)
    return pl.pallas_call(
        body,
        name=name,
        out_shape=out_shape,
        in_specs=[any_spec] * n,
        out_specs=[any_spec] * n,
        scratch_shapes=[
            pltpu.SemaphoreType.DMA((n, N_DEV - 1)),
            pltpu.SemaphoreType.DMA((n, N_DEV - 1)),
            pltpu.SemaphoreType.DMA((n,)),
        ],
        compiler_params=pltpu.CompilerParams(has_side_effects=True),
    )(*arrs)


def _gather_two_level(name, arrs, pieces=1):
    n = len(arrs)
    items = []
    for i, a in enumerate(arrs):
        rows = a.shape[0]
        if pieces > 1 and rows >= 512:
            step = -(-rows // (16 * pieces)) * 16
            items += [(i, r0, min(step, rows - r0)) for r0 in range(0, rows, step)]
        else:
            items.append((i, 0, rows))
    n_items = len(items)

    def body(*refs):
        ins, outs = refs[:n], refs[n:2 * n]
        send_sems, recv_sems, local_sems = refs[2 * n:]
        x, y, c = lax.axis_index("x"), lax.axis_index("y"), lax.axis_index("c")
        me, sibling = (x, y, c), (x, y, 1 - c)
        chips = [(1 - x, y), (x, 1 - y), (1 - x, 1 - y)]

        def rows_of(ref, t):
            i, r0, rn = items[t]
            return ref if rn == arrs[i].shape[0] else ref.at[pl.ds(r0, rn)]

        def slot(t, dev):
            return rows_of(outs[items[t][0]].at[4 * dev[0] + 2 * dev[1] + dev[2]], t)

        def copy(t, k, block, to, from_input=False):
            return pltpu.make_async_remote_copy(
                src_ref=rows_of(ins[items[t][0]], t) if from_input else slot(t, block), dst_ref=slot(t, block),
                send_sem=send_sems.at[t, k], recv_sem=recv_sems.at[t, k],
                device_id=to, device_id_type=pl.DeviceIdType.MESH)

        own, sent = [], []
        for t in range(n_items):
            mine = pltpu.make_async_copy(rows_of(ins[items[t][0]], t), slot(t, me), local_sems.at[t])
            mine.start()
            own.append(mine)
            first = [copy(t, 1 + j, me, (*chip, c), from_input=True) for j, chip in enumerate(chips)]
            first.append(copy(t, 0, me, sibling, from_input=True))
            for cp in first:
                cp.start()
            sent += first
        for t in range(n_items):
            for j, chip in enumerate(chips):
                copy(t, 1 + j, (*chip, c), me).wait_recv()
                fwd = copy(t, 4 + j, (*chip, c), sibling)
                fwd.start()
                sent.append(fwd)
        for t in range(n_items):
            copy(t, 0, sibling, me).wait_recv()
            for j, chip in enumerate(chips):
                copy(t, 4 + j, (*chip, 1 - c), me).wait_recv()
        for cp in sent:
            cp.wait_send()
        for cp in own:
            cp.wait()

    any_spec = pl.BlockSpec(memory_space=pl.ANY)
    return pl.pallas_call(
        body, name=name,
        out_shape=[jax.ShapeDtypeStruct((N_DEV,) + a.shape, a.dtype) for a in arrs],
        in_specs=[any_spec] * n, out_specs=[any_spec] * n,
        scratch_shapes=[pltpu.SemaphoreType.DMA((n_items, 7)), pltpu.SemaphoreType.DMA((n_items, 7)),
                        pltpu.SemaphoreType.DMA((n_items,))],
        compiler_params=pltpu.CompilerParams(has_side_effects=True),
    )(*arrs)


def _peers(x, y, c):
    out = []
    for m in range(1, N_DEV):
        px = 1 - x if m & 4 else x
        py = 1 - y if m & 2 else y
        pc = 1 - c if m & 1 else c
        out.append((m, (px, py, pc), 4 * px + 2 * py + pc))
    return out


def _split_copies(kinds, src_refs, land_refs, send_sems, recv_sems):
    x, y, c = lax.axis_index("x"), lax.axis_index("y"), lax.axis_index("c")
    me = 4 * x + 2 * y + c
    copies = []
    for i, kind in enumerate(kinds):
        for m, peer, pidx in _peers(x, y, c):
            copies.append(pltpu.make_async_remote_copy(
                src_ref=src_refs[i].at[pidx] if kind == "scatter" else src_refs[i],
                dst_ref=land_refs[i].at[me],
                send_sem=send_sems.at[i * (N_DEV - 1) + m - 1],
                recv_sem=recv_sems.at[i * (N_DEV - 1) + m - 1],
                device_id=peer,
                device_id_type=pl.DeviceIdType.MESH,
            ))
    return copies


_HBM_SPEC = pl.BlockSpec(memory_space=pltpu.HBM)
_SEM_SPEC = pl.BlockSpec(memory_space=pltpu.SEMAPHORE)
_DATAFLOW = pltpu.SideEffectType.DATAFLOW_SIDE_EFFECTING


def _exchange_start(name, arrs, kinds):
    n = len(arrs)
    lands = []
    for a, kind in zip(arrs, kinds):
        shp = a.shape if kind == "scatter" else (N_DEV,) + a.shape
        lands.append(lax.empty(shp, a.dtype))

    def body(*refs):
        src_refs, land_refs = refs[:n], refs[n:2 * n]
        send_sems, recv_sems = refs[2 * n:2 * n + 2]
        token = refs[-1]
        for cp in _split_copies(kinds, src_refs, land_refs, send_sems, recv_sems):
            cp.start()
        token[...] = jnp.zeros_like(token)

    n_sem = n * (N_DEV - 1)
    hbm = lambda a: pltpu.HBM(a.shape, a.dtype)
    res = pl.pallas_call(
        body, name=name,
        out_shape=(pltpu.SemaphoreType.DMA((n_sem,)), pltpu.SemaphoreType.DMA((n_sem,)),
                   *[hbm(a) for a in arrs], *[hbm(a) for a in lands], jax.ShapeDtypeStruct((8, HD), F32)),
        in_specs=[_HBM_SPEC] * (2 * n),
        out_specs=(_SEM_SPEC, _SEM_SPEC, *[_HBM_SPEC] * (2 * n), pl.BlockSpec(memory_space=pltpu.VMEM)),
        input_output_aliases={i: 2 + i for i in range(2 * n)},
        compiler_params=pltpu.CompilerParams(has_side_effects=_DATAFLOW),
    )(*[pltpu.with_memory_space_constraint(a, pltpu.HBM) for a in arrs],
      *[pltpu.with_memory_space_constraint(a, pltpu.HBM) for a in lands])
    return (kinds, res[0], res[1], res[2:2 + n], res[2 + n:2 + 2 * n]), res[-1]


def _exchange_wait(name, state, after, fill_own=True):
    kinds, send_sems, recv_sems, srcs, lands = state
    n = len(srcs)

    def body(*refs):
        src_refs, land_refs = refs[:n], refs[n:2 * n]
        send_sems_ref, recv_sems_ref = refs[2 * n:2 * n + 2]
        for cp in _split_copies(kinds, src_refs, land_refs, send_sems_ref, recv_sems_ref):
            cp.wait_send()
            cp.wait_recv()

    res = pl.pallas_call(
        body, name=name,
        out_shape=tuple(pltpu.HBM(a.shape, a.dtype) for a in (*srcs, *lands)),
        in_specs=[_HBM_SPEC] * (2 * n) + [_SEM_SPEC, _SEM_SPEC, pl.BlockSpec(memory_space=pl.ANY)],
        out_specs=tuple([_HBM_SPEC] * (2 * n)),
        input_output_aliases={i: i for i in range(2 * n)},
        compiler_params=pltpu.CompilerParams(has_side_effects=_DATAFLOW),
    )(*srcs, *lands, send_sems, recv_sems, after)
    me = 4 * lax.axis_index("x") + 2 * lax.axis_index("y") + lax.axis_index("c")
    outs = []
    for kind, src, land in zip(kinds, res[:n], res[n:]):
        own = lax.dynamic_index_in_dim(src, me, 0, keepdims=False) if kind == "scatter" else src
        outs.append(lax.dynamic_update_index_in_dim(land, own, me, 0) if fill_own else (land, own))
    return outs


def _pick(n, cands):
    for t in cands:
        if n % t == 0:
            return t
    raise ValueError(f"no tile for {n}")


def _mm(name, a, b, mode, out_dtype, after=None):
    if mode == "nn":
        (m, k), (k2, n) = a.shape, b.shape
    elif mode == "nt":
        (m, k), (n, k2) = a.shape, b.shape
    else:
        (k, m), (k2, n) = a.shape, b.shape
    assert k == k2, (name, a.shape, b.shape)
    if mode == "tn":
        tm = _pick(m, (1024, 640, 512, 256, 128))
        tn = _pick(n, (1024, 640, 512, 256, 128))
        tk = _pick(k, (2048, 1024, 512, 256))
    else:
        tm, tn, tk = _pick(m, (512, 256)), n, k
    nk = k // tk

    def body(a_ref, b_ref, *rest):
        o_ref = rest[-2] if nk > 1 else rest[-1]
        av = a_ref[...].astype(BF16)
        bv = b_ref[...].astype(BF16)
        if mode == "nn":
            part = jnp.dot(av, bv, preferred_element_type=F32)
        elif mode == "nt":
            part = lax.dot_general(av, bv, NT_DIMS, preferred_element_type=F32)
        else:
            part = lax.dot_general(av, bv, TN_DIMS, preferred_element_type=F32)
        if nk == 1:
            o_ref[...] = part.astype(out_dtype)
            return
        acc_ref = rest[-1]
        kk = pl.program_id(2)

        @pl.when(kk == 0)
        def _():
            acc_ref[...] = part

        @pl.when(kk > 0)
        def _():
            acc_ref[...] += part

        @pl.when(kk == nk - 1)
        def _():
            o_ref[...] = acc_ref[...].astype(out_dtype)

    if mode == "tn":
        a_spec = pl.BlockSpec((tk, tm), lambda j, i, kk: (kk, i))
    else:
        a_spec = pl.BlockSpec((tm, tk), lambda j, i, kk: (i, kk))
    if mode == "nt":
        b_spec = pl.BlockSpec((tn, tk), lambda j, i, kk: (j, kk))
    else:
        b_spec = pl.BlockSpec((tk, tn), lambda j, i, kk: (kk, j))
    in_specs, args = [a_spec, b_spec], [a, b]
    if after is not None:
        in_specs.append(pl.BlockSpec((8, HD), lambda j, i, kk: (0, 0)))
        args.append(after)
    return pl.pallas_call(
        body,
        name=name,
        grid=(n // tn, m // tm, nk),
        in_specs=in_specs,
        out_specs=pl.BlockSpec((tm, tn), lambda j, i, kk: (i, j)),
        out_shape=jax.ShapeDtypeStruct((m, n), out_dtype),
        scratch_shapes=[pltpu.VMEM((tm, tn), F32)] if nk > 1 else [],
        compiler_params=_params(dimension_semantics=("parallel", "parallel", "arbitrary")),
    )(*args)


def _mm_cat(name, a_list, b, out_dtype, after=None):
    m = a_list[0].shape[0]
    ks = [a.shape[1] for a in a_list]
    n = b.shape[1]
    assert sum(ks) == b.shape[0], (name, ks, b.shape)
    tm = _pick(m, (512, 256))
    na = len(a_list)

    def body(*refs):
        b_ref, o_ref = refs[na], refs[-1]
        k0, acc = 0, None
        for a_ref, kw in zip(refs[:na], ks):
            part = jnp.dot(a_ref[...].astype(BF16), b_ref[k0:k0 + kw, :], preferred_element_type=F32)
            acc = part if acc is None else acc + part
            k0 += kw
        o_ref[...] = acc.astype(out_dtype)

    in_specs = [pl.BlockSpec((tm, kw), lambda i: (i, 0)) for kw in ks] + [pl.BlockSpec(b.shape, lambda i: (0, 0))]
    args = [*a_list, b]
    if after is not None:
        in_specs.append(pl.BlockSpec((8, HD), lambda i: (0, 0)))
        args.append(after)
    return pl.pallas_call(
        body, name=name, grid=(m // tm,),
        in_specs=in_specs, out_specs=pl.BlockSpec((tm, n), lambda i: (i, 0)),
        out_shape=jax.ShapeDtypeStruct((m, n), out_dtype),
        compiler_params=_params(dimension_semantics=("parallel",)),
    )(*args)


def _row(c, col=0):
    return pl.BlockSpec((TS, c), lambda i: (i, col))


def _vec(r, c):
    return pl.BlockSpec((r, c), lambda i: (0, 0))


def _prenorm(x, pre_gain):
    s = x.shape[0]

    def body(x_ref, g_ref, o_ref):
        xhat, _ = _rms_fwd(x_ref[...], g_ref[...])
        o_ref[...] = (xhat * g_ref[...]).astype(BF16)

    return pl.pallas_call(
        body, name="prenorm", grid=(s // TS,),
        in_specs=[_row(D), _vec(1, D)], out_specs=_row(D),
        out_shape=jax.ShapeDtypeStruct((s, D), BF16),
        compiler_params=_params(dimension_semantics=("parallel",)),
    )(x, pre_gain)


def _forget_fwd(zr, bf_pad):
    s = zr.shape[0]
    n = s // TQ

    def body(fl_ref, b_ref, kx_ref, c_buf, carry):
        i = pl.program_id(0)

        @pl.when(i == 0)
        def _():
            carry[...] = jnp.zeros_like(carry)

        fl = fl_ref[...] + b_ref[...]
        ls = jnp.minimum(fl, 0.0) - jnp.log(1.0 + jnp.exp(-jnp.abs(fl)))
        c_buf[...] = _cumsum_fwd(ls) + carry[0:1, :]
        carry[0:1, :] = c_buf[TQ - 1:TQ, :]
        cv = c_buf[...]
        for h in range(NH):
            kx_ref[h] = _bias_lanes(jnp.broadcast_to(cv[:, 8 * h:8 * h + 1], (TQ, HD)) * (-1.0 / SCALE), 0, 3)

    return pl.pallas_call(
        body, name="forget_fwd", grid=(n,),
        in_specs=[pl.BlockSpec((TQ, HD), lambda i: (i, FL_COL // HD)), _vec(1, HD)],
        out_specs=pl.BlockSpec((NH, TQ, HD), lambda i: (0, i, 0)),
        out_shape=jax.ShapeDtypeStruct((NH, s, HD), BF16),
        scratch_shapes=[pltpu.VMEM((TQ, HD), F32), pltpu.VMEM((8, HD), F32)],
        compiler_params=_params(dimension_semantics=("arbitrary",)),
    )(zr, bf_pad)


def _attn_fwd(zq, kx):
    s = zq.shape[0]
    n = s // TQ
    nb = TQ // HD

    def body(q_ref, k_ref, v_ref, kx_ref, o_ref, ax_ref):
        i = pl.program_id(1)
        lane = lax.broadcasted_iota(jnp.int32, (TQ, HD), 1)
        row = lax.broadcasted_iota(jnp.int32, (TQ, HD), 0)
        qa = jnp.concatenate([q_ref[...], jnp.where(lane < 3, 1.0, 0.0).astype(BF16)], axis=1)

        def qk(j):
            rows = pl.ds(pl.multiple_of(j * TQ, TQ), TQ)
            ka = jnp.concatenate([k_ref[rows, :], kx_ref[0, rows, :]], axis=1)
            return lax.dot_general(qa, ka, NT_DIMS, preferred_element_type=F32)

        def step(j, carry, u, masked):
            m, l, acc = carry
            rows = pl.ds(pl.multiple_of(j * TQ, TQ), TQ)
            us = [u[:, HD * b:HD * (b + 1)] for b in range(nb)]
            if masked:
                us = [jnp.where(row >= lane + HD * b, us[b], NEG) for b in range(nb)]
            bm = functools.reduce(jnp.maximum, us)
            m_new = jnp.maximum(m, jnp.max(bm, axis=1, keepdims=True))
            alpha = jnp.exp2((m - m_new) * EXP2_SCALE)
            shift = m_new * EXP2_SCALE
            ps = [jnp.exp2(ub * EXP2_SCALE - shift) for ub in us]
            l = alpha * l + functools.reduce(jnp.add, ps)
            pr = jnp.concatenate(ps, axis=1).astype(BF16)
            acc = alpha * acc + jnp.dot(pr, v_ref[rows, :], preferred_element_type=F32)
            return m_new, l, acc

        init = (jnp.full((TQ, HD), NEG, F32), jnp.zeros((TQ, HD), F32), jnp.zeros((TQ, HD), F32))
        carry = lax.fori_loop(0, i, lambda j, cr: step(j, cr, qk(j), False), init)
        m, l, acc = step(i, carry, qk(i), True)
        l_row = jnp.sum(l, axis=1, keepdims=True)
        o_ref[...] = acc / l_row
        ax_ref[0] = _bias_lanes(-(m + jnp.log(l_row) * (1.0 / SCALE)), 3, 0)

    return pl.pallas_call(
        body, name="attn_fwd", grid=(NH, n),
        in_specs=[
            pl.BlockSpec((TQ, HD), lambda h, i: (i, h)),
            pl.BlockSpec((s, HD), lambda h, i: (0, NH + h)),
            pl.BlockSpec((s, HD), lambda h, i: (0, 2 * NH + h)),
            pl.BlockSpec((1, s, HD), lambda h, i: (h, 0, 0)),
        ],
        out_specs=[pl.BlockSpec((TQ, HD), lambda h, i: (i, h)), pl.BlockSpec((1, TQ, HD), lambda h, i: (h, i, 0))],
        out_shape=[jax.ShapeDtypeStruct((s, D), F32), jax.ShapeDtypeStruct((NH, s, HD), BF16)],
        compiler_params=_params(dimension_semantics=("parallel", "parallel")),
    )(zq, zq, zq, kx)


def _attn_bwd(zq, do, ax, delta, kx, after):
    s = zq.shape[0]
    n = s // TQ
    nb = TQ // HD

    def body(k_ref, v_ref, kx_ref, q_ref, ax_ref, do_ref, dl_ref, after_ref, dq_ref, dk_ref, dv_ref, dcs_ref, drs_ref):
        j = pl.program_id(1)

        @pl.when(j == 0)
        def _():
            dq_ref[...] = jnp.zeros_like(dq_ref)
            drs_ref[...] = jnp.zeros_like(drs_ref)

        k = k_ref[...]
        v = v_ref[...]
        ka = jnp.concatenate([k, kx_ref[0]], axis=1)
        row = lax.broadcasted_iota(jnp.int32, (TQ, HD), 0)
        lane = lax.broadcasted_iota(jnp.int32, (TQ, HD), 1)

        def products(i):
            rows = pl.ds(pl.multiple_of(i * TQ, TQ), TQ)
            qa = jnp.concatenate([q_ref[rows, :], ax_ref[0, rows, :]], axis=1)
            return (lax.dot_general(qa, ka, NT_DIMS, preferred_element_type=F32),
                    lax.dot_general(do_ref[rows, :], v, NT_DIMS, preferred_element_type=F32))

        def step(i, carry, u, dp, masked):
            dk, dv, dcs = carry
            rows = pl.ds(pl.multiple_of(i * TQ, TQ), TQ)
            q = q_ref[rows, :]
            dout = do_ref[rows, :]
            dlv = dl_ref[0, rows, :]
            prs, dss = [], []
            for b in range(nb):
                cs = slice(HD * b, HD * (b + 1))
                ub = u[:, cs]
                if masked:
                    ub = jnp.where(row >= lane + HD * b, ub, NEG)
                pb = jnp.exp2(ub * EXP2_SCALE)
                prs.append(pb)
                dss.append(pb * (dp[:, cs] - dlv))
            drs_ref[0, rows, :] += functools.reduce(jnp.add, dss)
            ds = jnp.concatenate(dss, axis=1)
            dcs = dcs + jnp.sum(ds.reshape(TQ // 8, 8, TQ), axis=0)
            dsb = ds.astype(BF16)
            dv = dv + lax.dot_general(jnp.concatenate(prs, axis=1).astype(BF16), dout, TN_DIMS, preferred_element_type=F32)
            dk = dk + lax.dot_general(dsb, q, TN_DIMS, preferred_element_type=F32)
            dq_ref[rows, :] += jnp.dot(dsb, k, preferred_element_type=F32) * SCALE
            return dk, dv, dcs

        init = (jnp.zeros((TQ, HD), F32), jnp.zeros((TQ, HD), F32), jnp.zeros((8, TQ), F32))
        carry = step(j, init, *products(j), True)
        dk, dv, dcs = lax.fori_loop(j + 1, n, lambda i, cr: step(i, cr, *products(i), False), carry)
        dk_ref[...] = (dk * SCALE).astype(BF16)
        dv_ref[...] = dv.astype(BF16)
        dcs_ref[0] = jnp.broadcast_to(_colsum(dcs), (8, TQ))

    return pl.pallas_call(
        body, name="attn_bwd", grid=(NH, n),
        in_specs=[
            pl.BlockSpec((TQ, HD), lambda h, j: (j, NH + h)),
            pl.BlockSpec((TQ, HD), lambda h, j: (j, 2 * NH + h)),
            pl.BlockSpec((1, TQ, HD), lambda h, j: (h, j, 0)),
            pl.BlockSpec((s, HD), lambda h, j: (0, h)),
            pl.BlockSpec((1, s, HD), lambda h, j: (h, 0, 0)),
            pl.BlockSpec((s, HD), lambda h, j: (0, h)),
            pl.BlockSpec((1, s, HD), lambda h, j: (h, 0, 0)),
            pl.BlockSpec((8, HD), lambda h, j: (0, 0)),
        ],
        out_specs=[
            pl.BlockSpec((s, HD), lambda h, j: (0, h)),
            pl.BlockSpec((TQ, HD), lambda h, j: (j, h)),
            pl.BlockSpec((TQ, HD), lambda h, j: (j, h)),
            pl.BlockSpec((1, 8, TQ), lambda h, j: (j, h, 0)),
            pl.BlockSpec((1, s, HD), lambda h, j: (h, 0, 0)),
        ],
        out_shape=[
            jax.ShapeDtypeStruct((s, D), F32),
            jax.ShapeDtypeStruct((s, D), BF16),
            jax.ShapeDtypeStruct((s, D), BF16),
            jax.ShapeDtypeStruct((n, 8 * NH, TQ), F32),
            jax.ShapeDtypeStruct((NH, s, HD), F32),
        ],
        compiler_params=_params(dimension_semantics=("parallel", "arbitrary")),
    )(zq, zq, kx, zq, ax, do, delta, after)


def _forget_bwd(dcs, drs, zr, bf_pad):
    n = dcs.shape[0]
    s = n * TQ

    def body(dcs_ref, drs_ref, fl_ref, b_ref, dfl_ref, gb_ref, buf, carry):
        i = pl.program_id(0)

        @pl.when(i == 0)
        def _():
            carry[...] = jnp.zeros_like(carry)
            gb_ref[...] = jnp.zeros_like(gb_ref)

        dc_t = jnp.concatenate([dcs_ref[0], jnp.zeros((HD - 8 * NH, TQ), F32)], axis=0)
        lane = lax.broadcasted_iota(jnp.int32, (TQ, HD), 1)
        dc = -dc_t.T
        for hh in range(NH):
            dc = dc + jnp.where(lane == 8 * hh, jnp.sum(drs_ref[hh], axis=1, keepdims=True), 0.0)
        buf[...] = _cumsum_bwd(dc) + carry[0:1, :]
        carry[0:1, :] = buf[0:1, :]
        fl = fl_ref[...] + b_ref[...]
        dfl = buf[...] * _sigmoid_rel(-fl)
        dfl_ref[...] = dfl.astype(BF16)
        gb_ref[...] += _colsum(dfl)

    return pl.pallas_call(
        body, name="forget_bwd", grid=(n,),
        in_specs=[
            pl.BlockSpec((1, 8 * NH, TQ), lambda i: (n - 1 - i, 0, 0)),
            pl.BlockSpec((NH, TQ, HD), lambda i: (0, n - 1 - i, 0)),
            pl.BlockSpec((TQ, HD), lambda i: (n - 1 - i, FL_COL // HD)),
            _vec(1, HD),
        ],
        out_specs=[pl.BlockSpec((TQ, HD), lambda i: (n - 1 - i, 0)), _vec(1, HD)],
        out_shape=[jax.ShapeDtypeStruct((s, HD), BF16), jax.ShapeDtypeStruct((1, HD), F32)],
        scratch_shapes=[pltpu.VMEM((TQ, HD), F32), pltpu.VMEM((8, HD), F32)],
        compiler_params=_params(dimension_semantics=("arbitrary",)),
    )(dcs, drs, zr, bf_pad)


def _gates(xc, w_ref, b, sigmoid):
    xb = xc.astype(BF16)
    pre = jnp.concatenate(
        [jnp.dot(xb[:, HD * g:HD * (g + 1)], w_ref[g], preferred_element_type=F32) for g in range(NH)], axis=1)
    return sigmoid(pre + b)


def _lru_coeffs(r, lam):
    sp = jnp.maximum(-lam, 0.0) + jnp.log(1.0 + jnp.exp(-jnp.abs(lam)))
    log_a = -LRU_C * r * sp
    a = jnp.exp(log_a)
    y = 2.0 * log_a
    em1 = jnp.where(jnp.abs(y) < 0.01, y * (1.0 + y * (0.5 + y * (1.0 / 6.0))), jnp.exp(y) - 1.0)
    em = -em1
    return sp, a, em * lax.rsqrt(jnp.maximum(em, 1e-37))


def _conv_taps(ext, t):
    return [_shift_down(ext, CONV_W - 1 - jj, t) for jj in range(CONV_W)]


def _lru_fwd(zr, conv_w8, conv_b, w_r, b_r, w_i, b_i, lam):
    s = zr.shape[0]
    n = s // TS
    xl_col = 1

    def body(xl_ref, halo_ref, cw_ref, cb_ref, wr_ref, br_ref, wi_ref, bi_ref, lam_ref, xc_ref, h_ref, carry):
        i = pl.program_id(0)

        @pl.when(i == 0)
        def _():
            carry[...] = jnp.zeros_like(carry)

        halo = jnp.where(i == 0, 0.0, halo_ref[...])
        taps = _conv_taps(jnp.concatenate([halo, xl_ref[...]], axis=0), TS)
        xc = cb_ref[...] + sum(cw_ref[jj:jj + 1, :] * taps[jj] for jj in range(CONV_W))
        xc_ref[...] = xc
        r = _gates(xc, wr_ref, br_ref[...], _sigmoid_rel)
        ig = _gates(xc, wi_ref, bi_ref[...], _sigmoid)
        _, a, gam = _lru_coeffs(r, lam_ref[...])
        a_cum, h_loc = _scan(a, gam * (ig * xc), False)
        h_ref[...] = h_loc + a_cum * carry[0:1, :]
        carry[0:1, :] = h_ref[TS - 1:TS, :]

    return pl.pallas_call(
        body, name="lru_fwd", grid=(n,),
        in_specs=[
            _row(D, xl_col),
            pl.BlockSpec((8, D), lambda i: (jnp.maximum(i * (TS // 8) - 1, 0), xl_col)),
            _vec(8, D), _vec(1, D),
            pl.BlockSpec((NH, HD, HD), lambda i: (0, 0, 0)), _vec(1, D),
            pl.BlockSpec((NH, HD, HD), lambda i: (0, 0, 0)), _vec(1, D),
            _vec(1, D),
        ],
        out_specs=[_row(D), _row(D)],
        out_shape=[jax.ShapeDtypeStruct((s, D), F32), jax.ShapeDtypeStruct((s, D), F32)],
        scratch_shapes=[pltpu.VMEM((8, D), F32)],
        compiler_params=_params(dimension_semantics=("arbitrary",)),
    )(zr, zr, conv_w8, conv_b, w_r, b_r, w_i, b_i, lam)


def _lru_bwd(zr, xc, h, dh, conv_w8, w_r, b_r, w_i, b_i, lam):
    s = zr.shape[0]
    n = s // TS
    xl_col = 1

    def rev(i):
        return n - 1 - i

    def body(xl_ref, xlh_ref, xc_ref, h_ref, hh_ref, dh_ref, cw_ref, wr_ref, br_ref, wi_ref, bi_ref, lam_ref,
             dxl_ref, gwr_ref, gwi_ref, gbr_ref, gbi_ref, glam_ref, gcb_ref, gcw_ref, l_buf, dxc_buf, carry_g, carry_dxc):
        i = pl.program_id(0)
        first = rev(i) == 0

        @pl.when(i == 0)
        def _():
            carry_g[...] = jnp.zeros_like(carry_g)
            carry_dxc[...] = jnp.zeros_like(carry_dxc)
            for ref in (gwr_ref, gwi_ref, gbr_ref, gbi_ref, glam_ref, gcb_ref, gcw_ref):
                ref[...] = jnp.zeros_like(ref)

        rows = _rows_iota(TS)
        xc = xc_ref[...]
        lam = lam_ref[...]
        r = _gates(xc, wr_ref, br_ref[...], _sigmoid_rel)
        ig = _gates(xc, wi_ref, bi_ref[...], _sigmoid)
        sp, a, gam = _lru_coeffs(r, lam)
        g = dh_ref[...] + jnp.where(rows == TS - 1, carry_g[0:1, :], 0.0)
        b = jnp.where(rows == TS - 1, 0.0, pltpu.roll(a, TS - 1, 0))
        l_buf[...] = _scan(b, g, True)[1]
        lv = l_buf[...]
        carry_g[0:1, :] = l_buf[0:1, :] * a[0:1, :]
        h_prev_row = jnp.where(first, 0.0, hh_ref[7:8, :])
        h_prev = jnp.where(rows == 0, h_prev_row, pltpu.roll(h_ref[...], 1, 0))
        dgam = lv * ig * xc
        dig = lv * gam * xc
        dxc = lv * gam * ig
        dla = lv * h_prev * a - dgam * (a * a) / gam
        dr = dla * (-LRU_C) * sp
        glam_ref[...] += _colsum(dla * r) * (LRU_C * _sigmoid_rel(-lam))
        dpr = dr * r * (1.0 - r)
        dpi = dig * ig * (1.0 - ig)
        gbr_ref[...] += _colsum(dpr)
        gbi_ref[...] += _colsum(dpi)
        xb = xc.astype(BF16)
        dprb = dpr.astype(BF16)
        dpib = dpi.astype(BF16)
        back = []
        for gi in range(NH):
            cs = slice(HD * gi, HD * (gi + 1))
            gwr_ref[gi] += lax.dot_general(xb[:, cs], dprb[:, cs], TN_DIMS, preferred_element_type=F32)
            gwi_ref[gi] += lax.dot_general(xb[:, cs], dpib[:, cs], TN_DIMS, preferred_element_type=F32)
            back.append(lax.dot_general(dprb[:, cs], wr_ref[gi], NT_DIMS, preferred_element_type=F32)
                        + lax.dot_general(dpib[:, cs], wi_ref[gi], NT_DIMS, preferred_element_type=F32))
        dxc = dxc + jnp.concatenate(back, axis=1)
        dxc_buf[...] = dxc
        gcb_ref[...] += _colsum(dxc)
        halo = jnp.where(first, 0.0, xlh_ref[...])
        taps = _conv_taps(jnp.concatenate([halo, xl_ref[...]], axis=0), TS)
        for jj in range(CONV_W):
            gcw_ref[jj:jj + 1, :] += _colsum(dxc * taps[jj])
        ext = jnp.concatenate([dxc, carry_dxc[...]], axis=0)
        dxl = sum(cw_ref[jj:jj + 1, :] * _shift_up(ext, CONV_W - 1 - jj, TS) for jj in range(CONV_W))
        dxl_ref[...] = dxl.astype(BF16)
        carry_dxc[...] = dxc_buf[0:8, :]

    rowr = lambda c, col=0: pl.BlockSpec((TS, c), lambda i: (rev(i), col))
    halo = lambda col: pl.BlockSpec((8, D), lambda i: (jnp.maximum(rev(i) * (TS // 8) - 1, 0), col))
    gate_w = pl.BlockSpec((NH, HD, HD), lambda i: (0, 0, 0))
    return pl.pallas_call(
        body, name="lru_bwd", grid=(n,),
        in_specs=[rowr(D, xl_col), halo(xl_col), rowr(D), rowr(D), halo(0), rowr(D),
                  _vec(8, D), gate_w, _vec(1, D), gate_w, _vec(1, D), _vec(1, D)],
        out_specs=[rowr(D), gate_w, gate_w, _vec(1, D), _vec(1, D), _vec(1, D), _vec(1, D), _vec(8, D)],
        out_shape=[
            jax.ShapeDtypeStruct((s, D), BF16),
            jax.ShapeDtypeStruct((NH, HD, HD), F32), jax.ShapeDtypeStruct((NH, HD, HD), F32),
            jax.ShapeDtypeStruct((1, D), F32), jax.ShapeDtypeStruct((1, D), F32), jax.ShapeDtypeStruct((1, D), F32),
            jax.ShapeDtypeStruct((1, D), F32), jax.ShapeDtypeStruct((8, D), F32),
        ],
        scratch_shapes=[pltpu.VMEM((TS, D), F32), pltpu.VMEM((TS, D), F32), pltpu.VMEM((8, D), F32), pltpu.VMEM((8, D), F32)],
        compiler_params=_params(dimension_semantics=("arbitrary",)),
    )(zr, zr, xc, h, h, dh, conv_w8, w_r, b_r, w_i, b_i, lam)


def _silu_parts(g):
    sg = _sigmoid(g)
    return g * sg, sg * (1.0 + g * (1.0 - sg))


def _branch_out(o, h, zr, gain_a, gain_l):
    s = o.shape[0]

    def body(o_ref, ga_ref, h_ref, gl_ref, ka_ref, kl_ref, y_ref):
        ohat, _ = _rms_fwd(o_ref[...], None)
        y_ref[:, 0:D] = (ohat * ka_ref[...] * _silu_parts(ga_ref[...])[0]).astype(BF16)
        hhat, _ = _rms_fwd(h_ref[...], None)
        y_ref[:, D:2 * D] = (hhat * kl_ref[...] * _silu_parts(gl_ref[...])[0]).astype(BF16)

    return pl.pallas_call(
        body, name="branch_out", grid=(s // TS,),
        in_specs=[_row(D), _row(D, 0), _row(D), _row(D, 2), _vec(1, D), _vec(1, D)],
        out_specs=_row(2 * D),
        out_shape=jax.ShapeDtypeStruct((s, 2 * D), BF16),
        compiler_params=_params(dimension_semantics=("parallel",)),
    )(o, zr, h, zr, gain_a, gain_l)


def _branch_out_bwd(o, h, zr, dmix, w_out, gain_a, gain_l):
    s = o.shape[0]

    def body(o_ref, ga_ref, h_ref, gl_ref, dm_ref, w_ref, ka_ref, kl_ref,
             do_ref, dl_ref, dga_ref, dh_ref, dgl_ref, gka_ref, gkl_ref):
        @pl.when(pl.program_id(0) == 0)
        def _():
            gka_ref[...] = jnp.zeros_like(gka_ref)
            gkl_ref[...] = jnp.zeros_like(gkl_ref)

        dycat = lax.dot_general(dm_ref[...], w_ref[...], NT_DIMS, preferred_element_type=F32)

        def one(v, g, dy, gain):
            vhat, rstd = _rms_fwd(v, None)
            sg, dsg = _silu_parts(g)
            dn = dy * sg
            dg = dy * (vhat * gain) * dsg
            return _rms_bwd(vhat, rstd, dn * gain), dg, _colsum(dn * vhat)

        o = o_ref[...]
        dout, dga, gka = one(o, ga_ref[...], dycat[:, :D], ka_ref[...])
        do_ref[...] = dout.astype(BF16)
        dga_ref[...] = dga.astype(BF16)
        gka_ref[...] += gka
        prod = dout * o
        for hh in range(NH):
            dl_ref[hh] = jnp.broadcast_to(jnp.sum(prod[:, HD * hh:HD * (hh + 1)], axis=1, keepdims=True), (TS, HD))
        dh, dgl, gkl = one(h_ref[...], gl_ref[...], dycat[:, D:], kl_ref[...])
        dh_ref[...] = dh
        dgl_ref[...] = dgl.astype(BF16)
        gkl_ref[...] += gkl

    return pl.pallas_call(
        body, name="branch_out_bwd", grid=(s // TS,),
        in_specs=[_row(D), _row(D, 0), _row(D), _row(D, 2), _row(D), _vec(2 * D, D), _vec(1, D), _vec(1, D)],
        out_specs=[_row(D), pl.BlockSpec((NH, TS, HD), lambda i: (0, i, 0)), _row(D), _row(D), _row(D), _vec(1, D), _vec(1, D)],
        out_shape=[
            jax.ShapeDtypeStruct((s, D), BF16), jax.ShapeDtypeStruct((NH, s, HD), F32), jax.ShapeDtypeStruct((s, D), BF16),
            jax.ShapeDtypeStruct((s, D), F32), jax.ShapeDtypeStruct((s, D), BF16),
            jax.ShapeDtypeStruct((1, D), F32), jax.ShapeDtypeStruct((1, D), F32),
        ],
        compiler_params=_params(dimension_semantics=("arbitrary",)),
    )(o, zr, h, zr, dmix, w_out, gain_a, gain_l)


def _residual(x, ycat, w_out, post_gain):
    s = x.shape[0]

    def body(x_ref, y_ref, w_ref, g_ref, m_ref, h_ref, hb_ref):
        mix = jnp.dot(y_ref[...], w_ref[...], preferred_element_type=F32)
        m_ref[...] = mix
        mhat, _ = _rms_fwd(mix, None)
        h1 = x_ref[...] + mhat * g_ref[...]
        h_ref[...] = h1
        hb_ref[...] = h1.astype(BF16)

    return pl.pallas_call(
        body, name="residual", grid=(s // TS,),
        in_specs=[_row(D), _row(2 * D), _vec(2 * D, D), _vec(1, D)], out_specs=[_row(D), _row(D), _row(D)],
        out_shape=[jax.ShapeDtypeStruct((s, D), F32), jax.ShapeDtypeStruct((s, D), F32), jax.ShapeDtypeStruct((s, D), BF16)],
        compiler_params=_params(dimension_semantics=("parallel",)),
    )(x, ycat, w_out, post_gain)


def _head(h1, p, tgt, mix, w_gate, w_ple, ple_gain, b_gate, post_gain):
    s = h1.shape[0]

    def body(h_ref, p_ref, t_ref, m_ref, wg_ref, wp_ref, kg_ref, b_ref, pg_ref,
             loss_ref, dgp_ref, dpe_ref, dh_ref, dm_ref, gk_ref, gb_ref, gg_ref):
        @pl.when(pl.program_id(0) == 0)
        def _():
            for ref in (loss_ref, gk_ref, gb_ref, gg_ref):
                ref[...] = jnp.zeros_like(ref)

        h1 = h_ref[...]
        pe = jnp.dot(p_ref[...].astype(BF16), wp_ref[...], preferred_element_type=F32)
        gp = jnp.dot(h1.astype(BF16), wg_ref[...], preferred_element_type=F32)
        ehat, rstd = _rms_fwd(pe, None)
        e = ehat * kg_ref[...]
        gate = _sigmoid(gp + b_ref[...])
        diff = (h1 + gate * e) - t_ref[...]
        per_row = jnp.mean(diff * diff, axis=-1, keepdims=True)
        loss_ref[...] += 0.5 * jnp.sum(per_row, axis=0, keepdims=True)
        dy = diff * (1.0 / D)
        dgp = dy * e * gate * (1.0 - gate)
        dgpb = dgp.astype(BF16)
        dgp_ref[...] = dgpb
        gb_ref[...] += _colsum(dgp)
        de = dy * gate
        gk_ref[...] += _colsum(de * ehat)
        dpe_ref[...] = _rms_bwd(ehat, rstd, de * kg_ref[...]).astype(BF16)
        dh1 = dy + lax.dot_general(dgpb, wg_ref[...], NT_DIMS, preferred_element_type=F32)
        dh_ref[...] = dh1
        mhat, rstd_m = _rms_fwd(m_ref[...], None)
        gg_ref[...] += _colsum(dh1 * mhat)
        dm_ref[...] = _rms_bwd(mhat, rstd_m, dh1 * pg_ref[...]).astype(BF16)

    return pl.pallas_call(
        body, name="head", grid=(s // TS,),
        in_specs=[_row(D), _row(D_PLE), _row(D), _row(D), _vec(D, D), _vec(D_PLE, D), _vec(1, D), _vec(1, D), _vec(1, D)],
        out_specs=[_vec(1, 1), _row(D), _row(D), _row(D), _row(D), _vec(1, D), _vec(1, D), _vec(1, D)],
        out_shape=[
            jax.ShapeDtypeStruct((1, 1), F32), jax.ShapeDtypeStruct((s, D), BF16), jax.ShapeDtypeStruct((s, D), BF16),
            jax.ShapeDtypeStruct((s, D), F32), jax.ShapeDtypeStruct((s, D), BF16),
            jax.ShapeDtypeStruct((1, D), F32), jax.ShapeDtypeStruct((1, D), F32), jax.ShapeDtypeStruct((1, D), F32),
        ],
        compiler_params=_params(dimension_semantics=("arbitrary",)),
    )(h1, p, tgt, mix, w_gate, w_ple, ple_gain, b_gate, post_gain)


def _prenorm_bwd(x, dxn_a, dxn_b, dh1, pre_gain):
    s = x.shape[0]

    def body(x_ref, da_ref, db_ref, dh_ref, g_ref, dx_ref, gg_ref):
        @pl.when(pl.program_id(0) == 0)
        def _():
            gg_ref[...] = jnp.zeros_like(gg_ref)

        xhat, rstd = _rms_fwd(x_ref[...], None)
        dxn = da_ref[...] + db_ref[...]
        gg_ref[...] += _colsum(dxn * xhat)
        dx_ref[...] = dh_ref[...] + _rms_bwd(xhat, rstd, dxn * g_ref[...])

    return pl.pallas_call(
        body, name="prenorm_bwd", grid=(s // TS,),
        in_specs=[_row(D), _row(D), _row(D), _row(D), _vec(1, D)], out_specs=[_row(D), _vec(1, D)],
        out_shape=[jax.ShapeDtypeStruct((s, D), F32), jax.ShapeDtypeStruct((1, D), F32)],
        compiler_params=_params(dimension_semantics=("arbitrary",)),
    )(x, dxn_a, dxn_b, dh1, pre_gain)


def _adamw(name, parts, w, m, v, own=None, me=None):
    r, c = w.shape
    if r % 8 == 0:
        tr = _pick(r, (256, 128, 16, 8))
        grid = (r // tr,)
        blk = pl.BlockSpec((tr, c), lambda i: (i, 0))
        parts_blk = pl.BlockSpec((N_DEV, tr, c), lambda i: (0, i, 0))
    else:
        tc = _pick(c, (256, 128))
        grid = (c // tc,)
        blk = pl.BlockSpec((r, tc), lambda i: (0, i))
        parts_blk = pl.BlockSpec((N_DEV, r, tc), lambda i: (0, 0, i))

    def body(*refs):
        p_ref, w_ref, m_ref, v_ref = refs[:4]
        g_ref, d_ref, nm_ref, nv_ref = refs[-4:]
        if own is None:
            g = p_ref[0].astype(F32)
            for j in range(1, N_DEV):
                g = g + p_ref[j].astype(F32)
            g_ref[...] = g
        else:
            own_ref, me_ref = refs[4:6]
            g_ref[...] = jnp.zeros_like(g_ref)
            for j in range(N_DEV):
                @pl.when(me_ref[0] == j)
                def _():
                    g_ref[...] += own_ref[...].astype(F32)

                @pl.when(me_ref[0] != j)
                def _():
                    g_ref[...] += p_ref[j].astype(F32)
            g = g_ref[...]
        nm = ADAM_B1 * m_ref[...] + (1.0 - ADAM_B1) * g
        nv = ADAM_B2 * v_ref[...] + (1.0 - ADAM_B2) * (g * g)
        nm_ref[...] = nm
        nv_ref[...] = nv
        m_hat = nm / (1.0 - ADAM_B1 ** ADAM_STEP)
        v_hat = nv / (1.0 - ADAM_B2 ** ADAM_STEP)
        d_ref[...] = -ADAM_LR * (m_hat / (jnp.sqrt(v_hat) + ADAM_EPS) + ADAM_WD * w_ref[...])

    in_specs, args = [parts_blk, blk, blk, blk], [parts, w, m, v]
    if own is not None:
        in_specs += [blk, pl.BlockSpec(memory_space=pltpu.SMEM)]
        args += [own, me]
    return pl.pallas_call(
        body, name=name, grid=grid,
        in_specs=in_specs,
        out_specs=[blk] * 4,
        out_shape=[jax.ShapeDtypeStruct((r, c), F32)] * 4,
        compiler_params=_params(dimension_semantics=("parallel",)),
    )(*args)


def _spread8(v):
    r = v.shape[0]
    return jnp.pad(jnp.pad(v[:, :, None], ((0, 0), (0, 0), (0, 7))).reshape(r, 8 * NH), ((0, 0), (0, HD - 8 * NH)))


def _gather8(v):
    return v[:, :8 * NH].reshape(v.shape[0], NH, 8)[:, :, 0]


def _cols_to_shards(g):
    r, c8 = g.shape
    return g.reshape(r, N_DEV, c8 // N_DEV).transpose(1, 0, 2)


def _shards_to_cols(g):
    n, r, c = g.shape
    return g.transpose(1, 0, 2).reshape(r, n * c)


def kernel(x, p, w_in, b_f, pre_gain, post_gain, conv_w, conv_b, w_rgate, b_rgate, w_igate, b_igate, lru_lambda, attn_out_gain, lru_out_gain, w_out, w_ple, ple_gain, w_ple_gate, b_ple_gate, loss_target, m_w_in, m_b_f, m_pre_gain, m_post_gain, m_conv_w, m_conv_b, m_w_rgate, m_b_rgate, m_w_igate, m_b_igate, m_lru_lambda, m_attn_out_gain, m_lru_out_gain, m_w_out, m_w_ple, m_ple_gain, m_w_ple_gate, m_b_ple_gate, v_w_in, v_b_f, v_pre_gain, v_post_gain, v_conv_w, v_conv_b, v_w_rgate, v_b_rgate, v_w_igate, v_b_igate, v_lru_lambda, v_attn_out_gain, v_lru_out_gain, v_w_out, v_w_ple, v_ple_gain, v_w_ple_gate, v_b_ple_gate):
    me = 4 * lax.axis_index("x") + 2 * lax.axis_index("y") + lax.axis_index("c")
    x2, p2, tgt = x[0], p[0, 0], loss_target[0]

    conv_w_shard8 = jnp.pad(conv_w[0], ((0, 8 - CONV_W), (0, 0)))
    wt, m_wt, v_wt = w_in[0].T, m_w_in[0].T, v_w_in[0].T
    g_wint, g_conv = _gather_two_level("gather_w_in", [wt.astype(BF16), conv_w_shard8])
    win_t = g_wint.reshape(D_IN, D)
    rest_state, rest_token = _exchange_start(
        "gather_rest_start", [w_out[0].astype(BF16), w_ple[0].astype(BF16), w_ple_gate[0].astype(BF16)], ["bcast"] * 3)
    w_qkv_t = win_t[:D_QKV]
    w_rest_t = jnp.concatenate([win_t[D_QKV + NH:], _spread8(win_t[D_QKV:D_QKV + NH].T).T], axis=0)
    conv_w8 = _shards_to_cols(g_conv)
    bf_pad = _spread8(b_f)
    w_r, w_i = w_rgate[0].astype(BF16), w_igate[0].astype(BF16)

    xn = _prenorm(x2, pre_gain + rest_token[0:1, 0:1])
    zq = _mm("proj_qkv", xn, w_qkv_t, "nt", BF16)
    zr = _mm("proj_rest", xn, w_rest_t, "nt", F32)
    kx = _forget_fwd(zr, bf_pad)
    o, ax = _attn_fwd(zq, kx)
    xc, h = _lru_fwd(zr, conv_w8, conv_b, w_r, b_rgate, w_i, b_igate, lru_lambda)
    ycat = _branch_out(o, h, zr, attn_out_gain, lru_out_gain)
    g_wout, g_wple, g_wpg = _exchange_wait("gather_rest_wait", rest_state, ycat)
    wout_full = g_wout.reshape(2 * D, D)
    wple_full = _shards_to_cols(g_wple)
    wpg_full = g_wpg.reshape(D, D)
    mix, h1, h1b = _residual(x2, ycat, wout_full, post_gain)

    loss_part, dgp, dpe, dh1, dmix, g_ple_gain, g_b_gate, g_post_gain = _head(
        h1, p2, tgt, mix, wpg_full, wple_full, ple_gain, b_ple_gate, post_gain)
    loss = lax.psum(loss_part[0, 0], ("x", "y", "c"))
    gw_pg = _mm("bwd_gate_w", h1b, dgp, "tn", BF16)
    gw_ple = _mm("bwd_ple_w", p2, dpe, "tn", BF16)
    gw_out = _mm("bwd_out_w", ycat, dmix, "tn", BF16)
    do, delta, dga, dh, dgl, g_aog, g_log = _branch_out_bwd(o, h, zr, dmix, wout_full, attn_out_gain, lru_out_gain)
    dxl, g_wr, g_wi, g_br, g_bi, g_lam, g_cb, g_cw8 = _lru_bwd(
        zr, xc, h, dh, conv_w8, w_r, b_rgate, w_i, b_igate, lru_lambda)
    gates = jnp.concatenate([g_wr.reshape(D, HD), g_wi.reshape(D, HD)], axis=0).astype(BF16)
    outw_state, outw_token = _exchange_start(
        "exchange_outw_start",
        [gw_out.reshape(N_DEV, 2 * D // N_DEV, D), _cols_to_shards(gw_ple), gw_pg.reshape(N_DEV, D // N_DEV, D), gates],
        ["scatter"] * 3 + ["bcast"])
    dq, dk, dv, dcs, drs = _attn_bwd(zq, do, ax, delta, kx, outw_token)
    dfl, g_bf_pad = _forget_bwd(dcs, drs, zr, bf_pad)
    gw_pieces = [_mm("bwd_w_" + nm, dz, xn, "tn", BF16) for nm, dz in
                 (("q", dq), ("k", dk), ("v", dv), ("fl", dfl), ("ga", dga), ("xl", dxl), ("gl", dgl))]
    gw_pieces[3] = _gather8(gw_pieces[3].T).T
    gw_in_t = jnp.concatenate(gw_pieces, axis=0)
    inw_state, inw_token = _exchange_start(
        "exchange_inw_start", [gw_in_t.reshape(N_DEV, D_IN_SHARD, D)], ["scatter"])
    dxn_a = _mm_cat("bwd_qkv_x", [dq, dk, dv], w_qkv_t, F32, after=inw_token)
    dxn_b = _mm_cat("bwd_rest_x", [dga, dxl, dgl, dfl], w_rest_t, F32, after=inw_token)
    grad_x, g_pre_gain = _prenorm_bwd(x2, dxn_a, dxn_b, dh1, pre_gain)

    upd = {}
    me1 = me.reshape(1).astype(jnp.int32)
    r_wout, r_wple, r_wpg, r_gates = _exchange_wait("exchange_outw_wait", outw_state, grad_x, fill_own=False)
    upd["w_out"] = _adamw("adamw_w_out", r_wout[0], w_out[0], m_w_out[0], v_w_out[0], r_wout[1], me1)
    upd["w_ple"] = _adamw("adamw_w_ple", r_wple[0], w_ple[0], m_w_ple[0], v_w_ple[0], r_wple[1], me1)
    upd["w_ple_gate"] = _adamw("adamw_w_ple_gate", r_wpg[0], w_ple_gate[0], m_w_ple_gate[0], v_w_ple_gate[0], r_wpg[1], me1)
    gates_of = lambda a, b: jnp.concatenate([a[0].reshape(D, HD), b[0].reshape(D, HD)], axis=0)
    g_gates = _adamw("adamw_gates", r_gates[0], gates_of(w_rgate, w_igate), gates_of(m_w_rgate, m_w_igate),
                     gates_of(v_w_rgate, v_w_igate), r_gates[1], me1)
    upd["w_rgate"] = [a[:D].reshape(1, NH, HD, HD) for a in g_gates]
    upd["w_igate"] = [a[D:].reshape(1, NH, HD, HD) for a in g_gates]
    behind = upd["w_out"][0][0:1] + upd["w_ple_gate"][0][0:1] + jnp.pad(g_gates[0][0:1], ((0, 0), (0, D - HD)))
    small = jnp.concatenate(
        [jnp.pad(_gather8(g_bf_pad), ((0, 0), (0, D - NH))), g_pre_gain, g_post_gain, g_cb, g_br, g_bi, g_lam, g_aog, g_log,
         g_ple_gain, g_b_gate, g_cw8[:CONV_W], behind], axis=0)
    (r_small,) = _exchange("exchange_small", [small], ["bcast"])
    vec_names = ["b_f", "pre_gain", "post_gain", "conv_b", "b_rgate", "b_igate", "lru_lambda", "attn_out_gain",
                 "lru_out_gain", "ple_gain", "b_ple_gate"]
    vec_w = dict(b_f=(b_f, m_b_f, v_b_f), pre_gain=(pre_gain, m_pre_gain, v_pre_gain),
                 post_gain=(post_gain, m_post_gain, v_post_gain), conv_b=(conv_b, m_conv_b, v_conv_b),
                 b_rgate=(b_rgate, m_b_rgate, v_b_rgate), b_igate=(b_igate, m_b_igate, v_b_igate),
                 lru_lambda=(lru_lambda, m_lru_lambda, v_lru_lambda),
                 attn_out_gain=(attn_out_gain, m_attn_out_gain, v_attn_out_gain),
                 lru_out_gain=(lru_out_gain, m_lru_out_gain, v_lru_out_gain), ple_gain=(ple_gain, m_ple_gain, v_ple_gain),
                 b_ple_gate=(b_ple_gate, m_b_ple_gate, v_b_ple_gate))
    conv_mine = lambda a: lax.dynamic_slice_in_dim(a, me * HD, HD, axis=1)

    def small_rows(k):
        rows = [jnp.pad(vec_w[nm][k], ((0, 0), (0, D - vec_w[nm][k].shape[1]))) for nm in vec_names]
        cw = (conv_w, m_conv_w, v_conv_w)[k][0]
        full = lax.dynamic_update_slice_in_dim(jnp.ones((CONV_W, D), F32), cw, me * HD, axis=1)
        return jnp.concatenate(rows + [full, jnp.ones((1, D), F32)], axis=0)

    g_small = _adamw("adamw_small", r_small, small_rows(0), small_rows(1), small_rows(2))
    for idx, nm in enumerate(vec_names):
        width = vec_w[nm][0].shape[1]
        upd[nm] = [a[idx:idx + 1, :width] for a in g_small]
    base = len(vec_names)
    upd["conv_w"] = [conv_mine(a[base:base + CONV_W])[None] for a in g_small]
    (r_win,) = _exchange_wait("exchange_inw_wait", inw_state, g_small[0], fill_own=False)
    upd["w_in"] = [a.T for a in _adamw("adamw_w_in", r_win[0], wt, m_wt, v_wt, r_win[1], me1)]
    for nm in ("w_in", "w_out", "w_ple", "w_ple_gate"):
        upd[nm] = [a[None] for a in upd[nm]]

    order = ["w_in", "b_f", "pre_gain", "post_gain", "conv_w", "conv_b", "w_rgate", "b_rgate", "w_igate", "b_igate",
             "lru_lambda", "attn_out_gain", "lru_out_gain", "w_out", "w_ple", "ple_gain", "w_ple_gate", "b_ple_gate"]
    outs = [loss, grad_x[None]]
    for k in range(4):
        outs += [upd[nm][k] for nm in order]
    return tuple(outs)
```

```python
import functools

import jax
import jax.numpy as jnp
from jax import lax
from jax.experimental import pallas as pl
from jax.experimental.pallas import tpu as pltpu

F32 = jnp.float32
BF16 = jnp.bfloat16

N_DEV = 8
D = 1024
HD = 128
NH = 8
D_IN = 6152
D_IN_SHARD = D_IN // N_DEV
D_QKV = 3 * D
D_REST = 3 * D + HD
FL_COL = 3 * D
D_PLE = 256
CONV_W = 4
LRU_C = 8.0
RMS_EPS = 1e-6
SCALE = HD ** -0.5
EXP2_SCALE = SCALE * 1.4426950408889634
NEG = -1e30

ADAM_LR = 0.001
ADAM_B1 = 0.9
ADAM_B2 = 0.999
ADAM_EPS = 1e-08
ADAM_WD = 0.01
ADAM_STEP = 10

TS = 256
TQ = 1024
ROW_PARTS = 2
VMEM_LIMIT = 48 * 1024 * 1024

NT_DIMS = (((1,), (1,)), ((), ()))
TN_DIMS = (((0,), (0,)), ((), ()))


def _params(**kw):
    return pltpu.CompilerParams(vmem_limit_bytes=VMEM_LIMIT, **kw)


def _sigmoid(v):
    return 0.5 * jnp.tanh(0.5 * v) + 0.5


def _sigmoid_rel(v):
    return 1.0 / (1.0 + jnp.exp(-v))


def _rms_fwd(v, gain):
    rstd = lax.rsqrt(jnp.mean(v * v, axis=-1, keepdims=True) + RMS_EPS)
    return v * rstd, rstd


def _rms_bwd(vhat, rstd, dvhat):
    return rstd * (dvhat - vhat * jnp.mean(dvhat * vhat, axis=-1, keepdims=True))


def _colsum(v):
    return jnp.sum(v, axis=0, keepdims=True)


def _rows_iota(t):
    return lax.broadcasted_iota(jnp.int32, (t, 1), 0)


def _scan(a, u, reverse):
    t, c = a.shape
    rows = _rows_iota(t)
    d = 1
    while d < t:
        if d < 8:
            valid = rows < t - d if reverse else rows >= d
            shift = t - d if reverse else d
            u = jnp.where(valid, u + a * pltpu.roll(u, shift, 0), u)
            a = jnp.where(valid, a * pltpu.roll(a, shift, 0), a)
        else:
            zeros, ones = jnp.zeros((d, c), F32), jnp.ones((d, c), F32)
            if reverse:
                u_far, a_far = jnp.concatenate([u[d:], zeros], axis=0), jnp.concatenate([a[d:], ones], axis=0)
            else:
                u_far, a_far = jnp.concatenate([zeros, u[:t - d]], axis=0), jnp.concatenate([ones, a[:t - d]], axis=0)
            u = u + a * u_far
            a = a * a_far
        d *= 2
    return a, u


def _cumsum_fwd(v):
    t = v.shape[0]
    rows = _rows_iota(t)
    d = 1
    while d < t:
        v = jnp.where(rows >= d, v + pltpu.roll(v, d, 0), v)
        d *= 2
    return v


def _cumsum_bwd(v):
    t = v.shape[0]
    rows = _rows_iota(t)
    d = 1
    while d < t:
        v = jnp.where(rows < t - d, v + pltpu.roll(v, t - d, 0), v)
        d *= 2
    return v


def _bias_lanes(v, at, ones_at):
    lane = lax.broadcasted_iota(jnp.int32, v.shape, 1)
    hi = v.astype(BF16).astype(F32)
    mid = (v - hi).astype(BF16).astype(F32)
    lo = ((v - hi) - mid).astype(BF16).astype(F32)
    out = jnp.where((lane >= ones_at) & (lane < ones_at + 3), 1.0, 0.0)
    for k, piece in enumerate((hi, mid, lo)):
        out = jnp.where(lane == at + k, piece, out)
    return out.astype(BF16)


def _shift_down(ext, k, t):
    return pltpu.roll(ext, k, 0)[8:, :] if k else ext[8:, :]


def _shift_up(ext, k, t):
    return pltpu.roll(ext, t + 8 - k, 0)[:t, :] if k else ext[:t, :]


def _exchange(name, arrs, kinds):
    n = len(arrs)
    out_shape = []
    for a, kind in zip(arrs, kinds):
        shp = a.shape if kind == "scatter" else (N_DEV,) + a.shape
        out_shape.append(jax.ShapeDtypeStruct(shp, a.dtype))

    def body(*refs):
        ins, outs = refs[:n], refs[n:2 * n]
        send_sems, recv_sems, local_sems = refs[2 * n:]
        x, y, c = lax.axis_index("x"), lax.axis_index("y"), lax.axis_index("c")
        me = 4 * x + 2 * y + c
        copies = []
        for i in range(n):
            scatter = kinds[i] == "scatter"
            mine = pltpu.make_async_copy(ins[i].at[me] if scatter else ins[i], outs[i].at[me], local_sems.at[i])
            mine.start()
            copies.append(mine)
            for m in range(1, N_DEV):
                px = 1 - x if m & 4 else x
                py = 1 - y if m & 2 else y
                pc = 1 - c if m & 1 else c
                peer = 4 * px + 2 * py + pc
                cp = pltpu.make_async_remote_copy(
                    src_ref=ins[i].at[peer] if scatter else ins[i],
                    dst_ref=outs[i].at[me],
                    send_sem=send_sems.at[i, m - 1],
                    recv_sem=recv_sems.at[i, m - 1],
                    device_id=(px, py, pc),
                    device_id_type=pl.DeviceIdType.MESH,
                )
                cp.start()
                copies.append(cp)
        for cp in copies:
            cp.wait()

    any_spec = pl.BlockSpec(memory_space=pl.ANY)
    return pl.pallas_call(
        body,
        name=name,
        out_shape=out_shape,
        in_specs=[any_spec] * n,
        out_specs=[any_spec] * n,
        scratch_shapes=[
            pltpu.SemaphoreType.DMA((n, N_DEV - 1)),
            pltpu.SemaphoreType.DMA((n, N_DEV - 1)),
            pltpu.SemaphoreType.DMA((n,)),
        ],
        compiler_params=pltpu.CompilerParams(has_side_effects=True),
    )(*arrs)


def _gather_two_level(name, arrs, pieces=1):
    n = len(arrs)
    items = []
    for i, a in enumerate(arrs):
        rows = a.shape[0]
        if pieces > 1 and rows >= 512:
            step = -(-rows // (16 * pieces)) * 16
            items += [(i, r0, min(step, rows - r0)) for r0 in range(0, rows, step)]
        else:
            items.append((i, 0, rows))
    n_items = len(items)

    def body(*refs):
        ins, outs = refs[:n], refs[n:2 * n]
        send_sems, recv_sems, local_sems = refs[2 * n:]
        x, y, c = lax.axis_index("x"), lax.axis_index("y"), lax.axis_index("c")
        me, sibling = (x, y, c), (x, y, 1 - c)
        chips = [(1 - x, y), (x, 1 - y), (1 - x, 1 - y)]

        def rows_of(ref, t):
            i, r0, rn = items[t]
            return ref if rn == arrs[i].shape[0] else ref.at[pl.ds(r0, rn)]

        def slot(t, dev):
            return rows_of(outs[items[t][0]].at[4 * dev[0] + 2 * dev[1] + dev[2]], t)

        def copy(t, k, block, to, from_input=False):
            return pltpu.make_async_remote_copy(
                src_ref=rows_of(ins[items[t][0]], t) if from_input else slot(t, block), dst_ref=slot(t, block),
                send_sem=send_sems.at[t, k], recv_sem=recv_sems.at[t, k],
                device_id=to, device_id_type=pl.DeviceIdType.MESH)

        own, sent = [], []
        for t in range(n_items):
            mine = pltpu.make_async_copy(rows_of(ins[items[t][0]], t), slot(t, me), local_sems.at[t])
            mine.start()
            own.append(mine)
            first = [copy(t, 1 + j, me, (*chip, c), from_input=True) for j, chip in enumerate(chips)]
            first.append(copy(t, 0, me, sibling, from_input=True))
            for cp in first:
                cp.start()
            sent += first
        for t in range(n_items):
            for j, chip in enumerate(chips):
                copy(t, 1 + j, (*chip, c), me).wait_recv()
                fwd = copy(t, 4 + j, (*chip, c), sibling)
                fwd.start()
                sent.append(fwd)
        for t in range(n_items):
            copy(t, 0, sibling, me).wait_recv()
            for j, chip in enumerate(chips):
                copy(t, 4 + j, (*chip, 1 - c), me).wait_recv()
        for cp in sent:
            cp.wait_send()
        for cp in own:
            cp.wait()

    any_spec = pl.BlockSpec(memory_space=pl.ANY)
    return pl.pallas_call(
        body, name=name,
        out_shape=[jax.ShapeDtypeStruct((N_DEV,) + a.shape, a.dtype) for a in arrs],
        in_specs=[any_spec] * n, out_specs=[any_spec] * n,
        scratch_shapes=[pltpu.SemaphoreType.DMA((n_items, 7)), pltpu.SemaphoreType.DMA((n_items, 7)),
                        pltpu.SemaphoreType.DMA((n_items,))],
        compiler_params=pltpu.CompilerParams(has_side_effects=True),
    )(*arrs)


def _peers(x, y, c):
    out = []
    for m in range(1, N_DEV):
        px = 1 - x if m & 4 else x
        py = 1 - y if m & 2 else y
        pc = 1 - c if m & 1 else c
        out.append((m, (px, py, pc), 4 * px + 2 * py + pc))
    return out


def _split_copies(kinds, src_refs, land_refs, send_sems, recv_sems):
    x, y, c = lax.axis_index("x"), lax.axis_index("y"), lax.axis_index("c")
    me = 4 * x + 2 * y + c
    copies = []
    for i, kind in enumerate(kinds):
        for m, peer, pidx in _peers(x, y, c):
            copies.append(pltpu.make_async_remote_copy(
                src_ref=src_refs[i].at[pidx] if kind == "scatter" else src_refs[i],
                dst_ref=land_refs[i].at[me],
                send_sem=send_sems.at[i * (N_DEV - 1) + m - 1],
                recv_sem=recv_sems.at[i * (N_DEV - 1) + m - 1],
                device_id=peer,
                device_id_type=pl.DeviceIdType.MESH,
            ))
    return copies


_HBM_SPEC = pl.BlockSpec(memory_space=pltpu.HBM)
_SEM_SPEC = pl.BlockSpec(memory_space=pltpu.SEMAPHORE)
_DATAFLOW = pltpu.SideEffectType.DATAFLOW_SIDE_EFFECTING


def _exchange_start(name, arrs, kinds, after=None):
    n = len(arrs)
    extra = [] if after is None else [after]
    lands = []
    for a, kind in zip(arrs, kinds):
        shp = a.shape if kind == "scatter" else (N_DEV,) + a.shape
        lands.append(lax.empty(shp, a.dtype))

    def body(*refs):
        src_refs, land_refs = refs[:n], refs[n:2 * n]
        send_sems, recv_sems = refs[2 * n + len(extra):2 * n + len(extra) + 2]
        token = refs[-1]
        for cp in _split_copies(kinds, src_refs, land_refs, send_sems, recv_sems):
            cp.start()
        token[...] = jnp.zeros_like(token)

    n_sem = n * (N_DEV - 1)
    hbm = lambda a: pltpu.HBM(a.shape, a.dtype)
    res = pl.pallas_call(
        body, name=name,
        out_shape=(pltpu.SemaphoreType.DMA((n_sem,)), pltpu.SemaphoreType.DMA((n_sem,)),
                   *[hbm(a) for a in arrs], *[hbm(a) for a in lands], jax.ShapeDtypeStruct((8, HD), F32)),
        in_specs=[_HBM_SPEC] * (2 * n) + [pl.BlockSpec(memory_space=pl.ANY)] * len(extra),
        out_specs=(_SEM_SPEC, _SEM_SPEC, *[_HBM_SPEC] * (2 * n), pl.BlockSpec(memory_space=pltpu.VMEM)),
        input_output_aliases={i: 2 + i for i in range(2 * n)},
        compiler_params=pltpu.CompilerParams(has_side_effects=_DATAFLOW),
    )(*[pltpu.with_memory_space_constraint(a, pltpu.HBM) for a in arrs],
      *[pltpu.with_memory_space_constraint(a, pltpu.HBM) for a in lands], *extra)
    return (kinds, res[0], res[1], res[2:2 + n], res[2 + n:2 + 2 * n]), res[-1]


def _exchange_wait(name, state, after, fill_own=True):
    kinds, send_sems, recv_sems, srcs, lands = state
    n = len(srcs)

    def body(*refs):
        src_refs, land_refs = refs[:n], refs[n:2 * n]
        send_sems_ref, recv_sems_ref = refs[2 * n:2 * n + 2]
        for cp in _split_copies(kinds, src_refs, land_refs, send_sems_ref, recv_sems_ref):
            cp.wait_send()
            cp.wait_recv()

    res = pl.pallas_call(
        body, name=name,
        out_shape=tuple(pltpu.HBM(a.shape, a.dtype) for a in (*srcs, *lands)),
        in_specs=[_HBM_SPEC] * (2 * n) + [_SEM_SPEC, _SEM_SPEC, pl.BlockSpec(memory_space=pl.ANY)],
        out_specs=tuple([_HBM_SPEC] * (2 * n)),
        input_output_aliases={i: i for i in range(2 * n)},
        compiler_params=pltpu.CompilerParams(has_side_effects=_DATAFLOW),
    )(*srcs, *lands, send_sems, recv_sems, after)
    me = 4 * lax.axis_index("x") + 2 * lax.axis_index("y") + lax.axis_index("c")
    outs = []
    for kind, src, land in zip(kinds, res[:n], res[n:]):
        own = lax.dynamic_index_in_dim(src, me, 0, keepdims=False) if kind == "scatter" else src
        outs.append(lax.dynamic_update_index_in_dim(land, own, me, 0) if fill_own else (land, own))
    return outs


def _pick(n, cands):
    for t in cands:
        if n % t == 0:
            return t
    raise ValueError(f"no tile for {n}")


def _mm(name, a, b, mode, out_dtype, after=None):
    if mode == "nn":
        (m, k), (k2, n) = a.shape, b.shape
    elif mode == "nt":
        (m, k), (n, k2) = a.shape, b.shape
    else:
        (k, m), (k2, n) = a.shape, b.shape
    assert k == k2, (name, a.shape, b.shape)
    if mode == "tn":
        tm = _pick(m, (1024, 640, 512, 256, 128))
        tn = _pick(n, (1024, 640, 512, 256, 128))
        tk = _pick(k, (2048, 1024, 512, 256))
    else:
        tm, tn, tk = _pick(m, (512, 256)), n, k
    nk = k // tk

    def body(a_ref, b_ref, *rest):
        o_ref = rest[-2] if nk > 1 else rest[-1]
        av = a_ref[...].astype(BF16)
        bv = b_ref[...].astype(BF16)
        if mode == "nn":
            part = jnp.dot(av, bv, preferred_element_type=F32)
        elif mode == "nt":
            part = lax.dot_general(av, bv, NT_DIMS, preferred_element_type=F32)
        else:
            part = lax.dot_general(av, bv, TN_DIMS, preferred_element_type=F32)
        if nk == 1:
            o_ref[...] = part.astype(out_dtype)
            return
        acc_ref = rest[-1]
        kk = pl.program_id(2)

        @pl.when(kk == 0)
        def _():
            acc_ref[...] = part

        @pl.when(kk > 0)
        def _():
            acc_ref[...] += part

        @pl.when(kk == nk - 1)
        def _():
            o_ref[...] = acc_ref[...].astype(out_dtype)

    if mode == "tn":
        a_spec = pl.BlockSpec((tk, tm), lambda j, i, kk: (kk, i))
    else:
        a_spec = pl.BlockSpec((tm, tk), lambda j, i, kk: (i, kk))
    if mode == "nt":
        b_spec = pl.BlockSpec((tn, tk), lambda j, i, kk: (j, kk))
    else:
        b_spec = pl.BlockSpec((tk, tn), lambda j, i, kk: (kk, j))
    in_specs, args = [a_spec, b_spec], [a, b]
    if after is not None:
        in_specs.append(pl.BlockSpec((8, HD), lambda j, i, kk: (0, 0)))
        args.append(after)
    return pl.pallas_call(
        body,
        name=name,
        grid=(n // tn, m // tm, nk),
        in_specs=in_specs,
        out_specs=pl.BlockSpec((tm, tn), lambda j, i, kk: (i, j)),
        out_shape=jax.ShapeDtypeStruct((m, n), out_dtype),
        scratch_shapes=[pltpu.VMEM((tm, tn), F32)] if nk > 1 else [],
        compiler_params=_params(dimension_semantics=("parallel", "parallel", "arbitrary")),
    )(*args)


def _mm_cat(name, a_list, b, out_dtype, after=None):
    m = a_list[0].shape[0]
    ks = [a.shape[1] for a in a_list]
    n = b.shape[1]
    assert sum(ks) == b.shape[0], (name, ks, b.shape)
    tm = _pick(m, (512, 256))
    na = len(a_list)

    def body(*refs):
        b_ref, o_ref = refs[na], refs[-1]
        k0, acc = 0, None
        for a_ref, kw in zip(refs[:na], ks):
            part = jnp.dot(a_ref[...].astype(BF16), b_ref[k0:k0 + kw, :], preferred_element_type=F32)
            acc = part if acc is None else acc + part
            k0 += kw
        o_ref[...] = acc.astype(out_dtype)

    in_specs = [pl.BlockSpec((tm, kw), lambda i: (i, 0)) for kw in ks] + [pl.BlockSpec(b.shape, lambda i: (0, 0))]
    args = [*a_list, b]
    if after is not None:
        in_specs.append(pl.BlockSpec((8, HD), lambda i: (0, 0)))
        args.append(after)
    return pl.pallas_call(
        body, name=name, grid=(m // tm,),
        in_specs=in_specs, out_specs=pl.BlockSpec((tm, n), lambda i: (i, 0)),
        out_shape=jax.ShapeDtypeStruct((m, n), out_dtype),
        compiler_params=_params(dimension_semantics=("parallel",)),
    )(*args)


def _row(c, col=0):
    return pl.BlockSpec((TS, c), lambda i: (i, col))


def _vec(r, c):
    return pl.BlockSpec((r, c), lambda i: (0, 0))


def _prenorm_proj(x, pre_gain, w_t, after):
    s = x.shape[0]
    n = w_t.shape[0]
    tm = 512

    def body(x_ref, g_ref, w_ref, after_ref, xn_ref, z_ref):
        xhat, _ = _rms_fwd(x_ref[...], None)
        xn = (xhat * g_ref[...]).astype(BF16)
        xn_ref[...] = xn
        z_ref[...] = lax.dot_general(xn, w_ref[...], NT_DIMS, preferred_element_type=F32).astype(BF16)

    return pl.pallas_call(
        body, name="prenorm_proj_qkv", grid=(s // tm,),
        in_specs=[pl.BlockSpec((tm, D), lambda i: (i, 0)), _vec(1, D), _vec(n, D), _vec(8, HD)],
        out_specs=[pl.BlockSpec((tm, D), lambda i: (i, 0)), pl.BlockSpec((tm, n), lambda i: (i, 0))],
        out_shape=[jax.ShapeDtypeStruct((s, D), BF16), jax.ShapeDtypeStruct((s, n), BF16)],
        compiler_params=_params(dimension_semantics=("parallel",)),
    )(x, pre_gain, w_t, after)


def _forget_fwd(zr, bf_pad):
    s = zr.shape[0]
    n = s // TQ

    def body(fl_ref, b_ref, kx_ref, c_buf, carry):
        i = pl.program_id(0)

        @pl.when(i == 0)
        def _():
            carry[...] = jnp.zeros_like(carry)

        fl = fl_ref[...] + b_ref[...]
        ls = jnp.minimum(fl, 0.0) - jnp.log(1.0 + jnp.exp(-jnp.abs(fl)))
        c_buf[...] = _cumsum_fwd(ls) + carry[0:1, :]
        carry[0:1, :] = c_buf[TQ - 1:TQ, :]
        cv = c_buf[...]
        for h in range(NH):
            kx_ref[h] = _bias_lanes(jnp.broadcast_to(cv[:, 8 * h:8 * h + 1], (TQ, HD)) * (-1.0 / SCALE), 0, 3)

    return pl.pallas_call(
        body, name="forget_fwd", grid=(n,),
        in_specs=[pl.BlockSpec((TQ, HD), lambda i: (i, FL_COL // HD)), _vec(1, HD)],
        out_specs=pl.BlockSpec((NH, TQ, HD), lambda i: (0, i, 0)),
        out_shape=jax.ShapeDtypeStruct((NH, s, HD), BF16),
        scratch_shapes=[pltpu.VMEM((TQ, HD), F32), pltpu.VMEM((8, HD), F32)],
        compiler_params=_params(dimension_semantics=("arbitrary",)),
    )(zr, bf_pad)


def _attn_fwd(zq, kx):
    s = zq.shape[0]
    n = s // TQ
    nb = TQ // HD

    def body(q_ref, k_ref, v_ref, kx_ref, o_ref, ax_ref):
        i = pl.program_id(1)
        lane = lax.broadcasted_iota(jnp.int32, (TQ, HD), 1)
        row = lax.broadcasted_iota(jnp.int32, (TQ, HD), 0)
        qa = jnp.concatenate([q_ref[...], jnp.where(lane < 3, 1.0, 0.0).astype(BF16)], axis=1)

        def step(j, carry, masked):
            m, l, acc = carry
            rows = pl.ds(pl.multiple_of(j * TQ, TQ), TQ)
            ka = jnp.concatenate([k_ref[rows, :], kx_ref[0, rows, :]], axis=1)
            v = v_ref[rows, :]
            rp = TQ // ROW_PARTS
            parts = [slice(rp * t, rp * (t + 1)) for t in range(ROW_PARTS)]
            u_parts = [lax.dot_general(qa[part], ka, NT_DIMS, preferred_element_type=F32) for part in parts]
            out = []
            for part, u in zip(parts, u_parts):
                us = [u[:, HD * b:HD * (b + 1)] for b in range(nb)]
                if masked:
                    us = [jnp.where(row[part] >= lane[part] + HD * b, us[b], NEG) for b in range(nb)]
                bm = functools.reduce(jnp.maximum, us)
                m_new = jnp.maximum(m[part], jnp.max(bm, axis=1, keepdims=True))
                alpha = jnp.exp2((m[part] - m_new) * EXP2_SCALE)
                shift = m_new * EXP2_SCALE
                ps = [jnp.exp2(ub * EXP2_SCALE - shift) for ub in us]
                l_new = alpha * l[part] + functools.reduce(jnp.add, ps)
                pr = jnp.concatenate(ps, axis=1).astype(BF16)
                out.append((m_new, l_new, alpha * acc[part] + jnp.dot(pr, v, preferred_element_type=F32)))
            return tuple(jnp.concatenate([o[t] for o in out], axis=0) for t in range(3))

        init = (jnp.full((TQ, HD), NEG, F32), jnp.zeros((TQ, HD), F32), jnp.zeros((TQ, HD), F32))
        carry = lax.fori_loop(0, i, lambda j, cr: step(j, cr, False), init)
        m, l, acc = step(i, carry, True)
        l_row = jnp.sum(l, axis=1, keepdims=True)
        o_ref[...] = acc / l_row
        ax_ref[0] = _bias_lanes(-(m + jnp.log(l_row) * (1.0 / SCALE)), 3, 0)

    return pl.pallas_call(
        body, name="attn_fwd", grid=(NH, n),
        in_specs=[
            pl.BlockSpec((TQ, HD), lambda h, i: (i, h)),
            pl.BlockSpec((s, HD), lambda h, i: (0, NH + h)),
            pl.BlockSpec((s, HD), lambda h, i: (0, 2 * NH + h)),
            pl.BlockSpec((1, s, HD), lambda h, i: (h, 0, 0)),
        ],
        out_specs=[pl.BlockSpec((TQ, HD), lambda h, i: (i, h)), pl.BlockSpec((1, TQ, HD), lambda h, i: (h, i, 0))],
        out_shape=[jax.ShapeDtypeStruct((s, D), F32), jax.ShapeDtypeStruct((NH, s, HD), BF16)],
        compiler_params=_params(dimension_semantics=("parallel", "parallel")),
    )(zq, zq, zq, kx)


def _attn_bwd(zq, do, ax, delta, kx, after):
    s = zq.shape[0]
    n = s // TQ
    nb = TQ // HD

    def body(k_ref, v_ref, kx_ref, q_ref, ax_ref, do_ref, dl_ref, after_ref, dq_ref, dk_ref, dv_ref, dcs_ref, drs_ref):
        j = pl.program_id(1)

        @pl.when(j == 0)
        def _():
            dq_ref[...] = jnp.zeros_like(dq_ref)
            drs_ref[...] = jnp.zeros_like(drs_ref)

        k = k_ref[...]
        v = v_ref[...]
        ka = jnp.concatenate([k, kx_ref[0]], axis=1)
        row = lax.broadcasted_iota(jnp.int32, (TQ, HD), 0)
        lane = lax.broadcasted_iota(jnp.int32, (TQ, HD), 1)

        def products(i):
            rows = pl.ds(pl.multiple_of(i * TQ, TQ), TQ)
            qa = jnp.concatenate([q_ref[rows, :], ax_ref[0, rows, :]], axis=1)
            return (lax.dot_general(qa, ka, NT_DIMS, preferred_element_type=F32),
                    lax.dot_general(do_ref[rows, :], v, NT_DIMS, preferred_element_type=F32))

        def step(i, carry, u, dp, masked):
            dk, dv, dcs = carry
            rows = pl.ds(pl.multiple_of(i * TQ, TQ), TQ)
            q = q_ref[rows, :]
            dout = do_ref[rows, :]
            dlv = dl_ref[0, rows, :]
            prs, dss = [], []
            for b in range(nb):
                cs = slice(HD * b, HD * (b + 1))
                ub = u[:, cs]
                if masked:
                    ub = jnp.where(row >= lane + HD * b, ub, NEG)
                pb = jnp.exp2(ub * EXP2_SCALE)
                prs.append(pb)
                dss.append(pb * (dp[:, cs] - dlv))
            drs_ref[0, rows, :] += functools.reduce(jnp.add, dss)
            ds = jnp.concatenate(dss, axis=1)
            dcs = dcs + jnp.sum(ds.reshape(TQ // 8, 8, TQ), axis=0)
            dsb = ds.astype(BF16)
            dv = dv + lax.dot_general(jnp.concatenate(prs, axis=1).astype(BF16), dout, TN_DIMS, preferred_element_type=F32)
            dk = dk + lax.dot_general(dsb, q, TN_DIMS, preferred_element_type=F32)
            dq_ref[rows, :] += jnp.dot(dsb, k, preferred_element_type=F32) * SCALE
            return dk, dv, dcs

        init = (jnp.zeros((TQ, HD), F32), jnp.zeros((TQ, HD), F32), jnp.zeros((8, TQ), F32))
        carry = step(j, init, *products(j), True)
        dk, dv, dcs = lax.fori_loop(j + 1, n, lambda i, cr: step(i, cr, *products(i), False), carry)
        dk_ref[...] = (dk * SCALE).astype(BF16)
        dv_ref[...] = dv.astype(BF16)
        dcs_ref[0] = jnp.broadcast_to(_colsum(dcs), (8, TQ))

    return pl.pallas_call(
        body, name="attn_bwd", grid=(NH, n),
        in_specs=[
            pl.BlockSpec((TQ, HD), lambda h, j: (j, NH + h)),
            pl.BlockSpec((TQ, HD), lambda h, j: (j, 2 * NH + h)),
            pl.BlockSpec((1, TQ, HD), lambda h, j: (h, j, 0)),
            pl.BlockSpec((s, HD), lambda h, j: (0, h)),
            pl.BlockSpec((1, s, HD), lambda h, j: (h, 0, 0)),
            pl.BlockSpec((s, HD), lambda h, j: (0, h)),
            pl.BlockSpec((1, s, HD), lambda h, j: (h, 0, 0)),
            pl.BlockSpec((8, HD), lambda h, j: (0, 0)),
        ],
        out_specs=[
            pl.BlockSpec((s, HD), lambda h, j: (0, h)),
            pl.BlockSpec((TQ, HD), lambda h, j: (j, h)),
            pl.BlockSpec((TQ, HD), lambda h, j: (j, h)),
            pl.BlockSpec((1, 8, TQ), lambda h, j: (j, h, 0)),
            pl.BlockSpec((1, s, HD), lambda h, j: (h, 0, 0)),
        ],
        out_shape=[
            jax.ShapeDtypeStruct((s, D), F32),
            jax.ShapeDtypeStruct((s, D), BF16),
            jax.ShapeDtypeStruct((s, D), BF16),
            jax.ShapeDtypeStruct((n, 8 * NH, TQ), F32),
            jax.ShapeDtypeStruct((NH, s, HD), F32),
        ],
        compiler_params=_params(dimension_semantics=("parallel", "arbitrary")),
    )(zq, zq, kx, zq, ax, do, delta, after)


def _forget_bwd(dcs, drs, zr, bf_pad):
    n = dcs.shape[0]
    s = n * TQ

    def body(dcs_ref, drs_ref, fl_ref, b_ref, dfl_ref, gb_ref, buf, carry):
        i = pl.program_id(0)

        @pl.when(i == 0)
        def _():
            carry[...] = jnp.zeros_like(carry)
            gb_ref[...] = jnp.zeros_like(gb_ref)

        dc_t = jnp.concatenate([dcs_ref[0], jnp.zeros((HD - 8 * NH, TQ), F32)], axis=0)
        lane = lax.broadcasted_iota(jnp.int32, (TQ, HD), 1)
        dc = -dc_t.T
        for hh in range(NH):
            dc = dc + jnp.where(lane == 8 * hh, jnp.sum(drs_ref[hh], axis=1, keepdims=True), 0.0)
        buf[...] = _cumsum_bwd(dc) + carry[0:1, :]
        carry[0:1, :] = buf[0:1, :]
        fl = fl_ref[...] + b_ref[...]
        dfl = buf[...] * _sigmoid_rel(-fl)
        dfl_ref[...] = dfl.astype(BF16)
        gb_ref[...] += _colsum(dfl)

    return pl.pallas_call(
        body, name="forget_bwd", grid=(n,),
        in_specs=[
            pl.BlockSpec((1, 8 * NH, TQ), lambda i: (n - 1 - i, 0, 0)),
            pl.BlockSpec((NH, TQ, HD), lambda i: (0, n - 1 - i, 0)),
            pl.BlockSpec((TQ, HD), lambda i: (n - 1 - i, FL_COL // HD)),
            _vec(1, HD),
        ],
        out_specs=[pl.BlockSpec((TQ, HD), lambda i: (n - 1 - i, 0)), _vec(1, HD)],
        out_shape=[jax.ShapeDtypeStruct((s, HD), BF16), jax.ShapeDtypeStruct((1, HD), F32)],
        scratch_shapes=[pltpu.VMEM((TQ, HD), F32), pltpu.VMEM((8, HD), F32)],
        compiler_params=_params(dimension_semantics=("arbitrary",)),
    )(dcs, drs, zr, bf_pad)


def _gates(xc, w_ref, b, sigmoid):
    xb = xc.astype(BF16)
    pre = jnp.concatenate(
        [jnp.dot(xb[:, HD * g:HD * (g + 1)], w_ref[g], preferred_element_type=F32) for g in range(NH)], axis=1)
    return sigmoid(pre + b)


def _lru_coeffs(r, lam):
    sp = jnp.maximum(-lam, 0.0) + jnp.log(1.0 + jnp.exp(-jnp.abs(lam)))
    log_a = -LRU_C * r * sp
    a = jnp.exp(log_a)
    y = 2.0 * log_a
    em1 = jnp.where(jnp.abs(y) < 0.01, y * (1.0 + y * (0.5 + y * (1.0 / 6.0))), jnp.exp(y) - 1.0)
    em = -em1
    inv_gam = lax.rsqrt(jnp.maximum(em, 1e-37))
    return sp, a, em * inv_gam, inv_gam


def _conv_taps(ext, t):
    return [_shift_down(ext, CONV_W - 1 - jj, t) for jj in range(CONV_W)]


def _lru_fwd(zr, conv_w8, conv_b, w_r, b_r, w_i, b_i, lam):
    s = zr.shape[0]
    n = s // TS
    xl_col = 1

    def body(xl_ref, halo_ref, cw_ref, cb_ref, wr_ref, br_ref, wi_ref, bi_ref, lam_ref, xc_ref, h_ref, carry):
        i = pl.program_id(0)

        @pl.when(i == 0)
        def _():
            carry[...] = jnp.zeros_like(carry)

        halo = jnp.where(i == 0, 0.0, halo_ref[...])
        taps = _conv_taps(jnp.concatenate([halo, xl_ref[...]], axis=0), TS)
        xc = cb_ref[...] + sum(cw_ref[jj:jj + 1, :] * taps[jj] for jj in range(CONV_W))
        xc_ref[...] = xc
        r = _gates(xc, wr_ref, br_ref[...], _sigmoid_rel)
        ig = _gates(xc, wi_ref, bi_ref[...], _sigmoid)
        _, a, gam, _ = _lru_coeffs(r, lam_ref[...])
        a_cum, h_loc = _scan(a, gam * (ig * xc), False)
        h_ref[...] = h_loc + a_cum * carry[0:1, :]
        carry[0:1, :] = h_ref[TS - 1:TS, :]

    return pl.pallas_call(
        body, name="lru_fwd", grid=(n,),
        in_specs=[
            _row(D, xl_col),
            pl.BlockSpec((8, D), lambda i: (jnp.maximum(i * (TS // 8) - 1, 0), xl_col)),
            _vec(8, D), _vec(1, D),
            pl.BlockSpec((NH, HD, HD), lambda i: (0, 0, 0)), _vec(1, D),
            pl.BlockSpec((NH, HD, HD), lambda i: (0, 0, 0)), _vec(1, D),
            _vec(1, D),
        ],
        out_specs=[_row(D), _row(D)],
        out_shape=[jax.ShapeDtypeStruct((s, D), F32), jax.ShapeDtypeStruct((s, D), F32)],
        scratch_shapes=[pltpu.VMEM((8, D), F32)],
        compiler_params=_params(dimension_semantics=("arbitrary",)),
    )(zr, zr, conv_w8, conv_b, w_r, b_r, w_i, b_i, lam)


def _lru_bwd(zr, xc, h, dh, conv_w8, w_r, b_r, w_i, b_i, lam):
    s = zr.shape[0]
    n = s // TS
    xl_col = 1

    def rev(i):
        return n - 1 - i

    def body(xl_ref, xlh_ref, xc_ref, h_ref, hh_ref, dh_ref, cw_ref, wr_ref, br_ref, wi_ref, bi_ref, lam_ref,
             dxl_ref, gwr_ref, gwi_ref, gbr_ref, gbi_ref, glam_ref, gcb_ref, gcw_ref, l_buf, dxc_buf, carry_g, carry_dxc):
        i = pl.program_id(0)
        first = rev(i) == 0

        @pl.when(i == 0)
        def _():
            carry_g[...] = jnp.zeros_like(carry_g)
            carry_dxc[...] = jnp.zeros_like(carry_dxc)
            for ref in (gwr_ref, gwi_ref, gbr_ref, gbi_ref, glam_ref, gcb_ref, gcw_ref):
                ref[...] = jnp.zeros_like(ref)

        rows = _rows_iota(TS)
        xc = xc_ref[...]
        lam = lam_ref[...]
        r = _gates(xc, wr_ref, br_ref[...], _sigmoid_rel)
        ig = _gates(xc, wi_ref, bi_ref[...], _sigmoid)
        sp, a, gam, inv_gam = _lru_coeffs(r, lam)
        g = dh_ref[...] + jnp.where(rows == TS - 1, carry_g[0:1, :], 0.0)
        b = jnp.where(rows == TS - 1, 0.0, pltpu.roll(a, TS - 1, 0))
        l_buf[...] = _scan(b, g, True)[1]
        lv = l_buf[...]
        carry_g[0:1, :] = l_buf[0:1, :] * a[0:1, :]
        h_prev_row = jnp.where(first, 0.0, hh_ref[7:8, :])
        h_prev = jnp.where(rows == 0, h_prev_row, pltpu.roll(h_ref[...], 1, 0))
        dgam = lv * ig * xc
        dig = lv * gam * xc
        dxc = lv * gam * ig
        dla = lv * h_prev * a - dgam * (a * a) * inv_gam
        dr = dla * (-LRU_C) * sp
        glam_ref[...] += _colsum(dla * r) * (LRU_C * _sigmoid_rel(-lam))
        dpr = dr * r * (1.0 - r)
        dpi = dig * ig * (1.0 - ig)
        gbr_ref[...] += _colsum(dpr)
        gbi_ref[...] += _colsum(dpi)
        xb = xc.astype(BF16)
        dprb = dpr.astype(BF16)
        dpib = dpi.astype(BF16)
        back = []
        for gi in range(NH):
            cs = slice(HD * gi, HD * (gi + 1))
            gwr_ref[gi] += lax.dot_general(xb[:, cs], dprb[:, cs], TN_DIMS, preferred_element_type=F32)
            gwi_ref[gi] += lax.dot_general(xb[:, cs], dpib[:, cs], TN_DIMS, preferred_element_type=F32)
            back.append(lax.dot_general(dprb[:, cs], wr_ref[gi], NT_DIMS, preferred_element_type=F32)
                        + lax.dot_general(dpib[:, cs], wi_ref[gi], NT_DIMS, preferred_element_type=F32))
        dxc = dxc + jnp.concatenate(back, axis=1)
        dxc_buf[...] = dxc
        gcb_ref[...] += _colsum(dxc)
        halo = jnp.where(first, 0.0, xlh_ref[...])
        taps = _conv_taps(jnp.concatenate([halo, xl_ref[...]], axis=0), TS)
        for jj in range(CONV_W):
            gcw_ref[jj:jj + 1, :] += _colsum(dxc * taps[jj])
        ext = jnp.concatenate([dxc, carry_dxc[...]], axis=0)
        dxl = sum(cw_ref[jj:jj + 1, :] * _shift_up(ext, CONV_W - 1 - jj, TS) for jj in range(CONV_W))
        dxl_ref[...] = dxl.astype(BF16)
        carry_dxc[...] = dxc_buf[0:8, :]

    rowr = lambda c, col=0: pl.BlockSpec((TS, c), lambda i: (rev(i), col))
    halo = lambda col: pl.BlockSpec((8, D), lambda i: (jnp.maximum(rev(i) * (TS // 8) - 1, 0), col))
    gate_w = pl.BlockSpec((NH, HD, HD), lambda i: (0, 0, 0))
    return pl.pallas_call(
        body, name="lru_bwd", grid=(n,),
        in_specs=[rowr(D, xl_col), halo(xl_col), rowr(D), rowr(D), halo(0), rowr(D),
                  _vec(8, D), gate_w, _vec(1, D), gate_w, _vec(1, D), _vec(1, D)],
        out_specs=[rowr(D), gate_w, gate_w, _vec(1, D), _vec(1, D), _vec(1, D), _vec(1, D), _vec(8, D)],
        out_shape=[
            jax.ShapeDtypeStruct((s, D), BF16),
            jax.ShapeDtypeStruct((NH, HD, HD), F32), jax.ShapeDtypeStruct((NH, HD, HD), F32),
            jax.ShapeDtypeStruct((1, D), F32), jax.ShapeDtypeStruct((1, D), F32), jax.ShapeDtypeStruct((1, D), F32),
            jax.ShapeDtypeStruct((1, D), F32), jax.ShapeDtypeStruct((8, D), F32),
        ],
        scratch_shapes=[pltpu.VMEM((TS, D), F32), pltpu.VMEM((TS, D), F32), pltpu.VMEM((8, D), F32), pltpu.VMEM((8, D), F32)],
        compiler_params=_params(dimension_semantics=("arbitrary",)),
    )(zr, zr, xc, h, h, dh, conv_w8, w_r, b_r, w_i, b_i, lam)


def _silu_parts(g):
    sg = _sigmoid(g)
    return g * sg, sg * (1.0 + g * (1.0 - sg))


def _branch_out(o, h, zr, gain_a, gain_l):
    s = o.shape[0]

    def body(o_ref, ga_ref, h_ref, gl_ref, ka_ref, kl_ref, y_ref):
        ohat, _ = _rms_fwd(o_ref[...], None)
        y_ref[:, 0:D] = (ohat * ka_ref[...] * _silu_parts(ga_ref[...])[0]).astype(BF16)
        hhat, _ = _rms_fwd(h_ref[...], None)
        y_ref[:, D:2 * D] = (hhat * kl_ref[...] * _silu_parts(gl_ref[...])[0]).astype(BF16)

    return pl.pallas_call(
        body, name="branch_out", grid=(s // TS,),
        in_specs=[_row(D), _row(D, 0), _row(D), _row(D, 2), _vec(1, D), _vec(1, D)],
        out_specs=_row(2 * D),
        out_shape=jax.ShapeDtypeStruct((s, 2 * D), BF16),
        compiler_params=_params(dimension_semantics=("parallel",)),
    )(o, zr, h, zr, gain_a, gain_l)


def _branch_out_bwd(o, h, zr, dmix, w_out, gain_a, gain_l):
    s = o.shape[0]

    def body(o_ref, ga_ref, h_ref, gl_ref, dm_ref, w_ref, ka_ref, kl_ref,
             do_ref, dl_ref, dga_ref, dh_ref, dgl_ref, gka_ref, gkl_ref):
        @pl.when(pl.program_id(0) == 0)
        def _():
            gka_ref[...] = jnp.zeros_like(gka_ref)
            gkl_ref[...] = jnp.zeros_like(gkl_ref)

        dycat = lax.dot_general(dm_ref[...], w_ref[...], NT_DIMS, preferred_element_type=F32)

        def one(v, g, dy, gain):
            vhat, rstd = _rms_fwd(v, None)
            sg, dsg = _silu_parts(g)
            dn = dy * sg
            dg = dy * (vhat * gain) * dsg
            return _rms_bwd(vhat, rstd, dn * gain), dg, _colsum(dn * vhat)

        o = o_ref[...]
        dout, dga, gka = one(o, ga_ref[...], dycat[:, :D], ka_ref[...])
        do_ref[...] = dout.astype(BF16)
        dga_ref[...] = dga.astype(BF16)
        gka_ref[...] += gka
        prod = dout * o
        for hh in range(NH):
            dl_ref[hh] = jnp.broadcast_to(jnp.sum(prod[:, HD * hh:HD * (hh + 1)], axis=1, keepdims=True), (TS, HD))
        dh, dgl, gkl = one(h_ref[...], gl_ref[...], dycat[:, D:], kl_ref[...])
        dh_ref[...] = dh
        dgl_ref[...] = dgl.astype(BF16)
        gkl_ref[...] += gkl

    return pl.pallas_call(
        body, name="branch_out_bwd", grid=(s // TS,),
        in_specs=[_row(D), _row(D, 0), _row(D), _row(D, 2), _row(D), _vec(2 * D, D), _vec(1, D), _vec(1, D)],
        out_specs=[_row(D), pl.BlockSpec((NH, TS, HD), lambda i: (0, i, 0)), _row(D), _row(D), _row(D), _vec(1, D), _vec(1, D)],
        out_shape=[
            jax.ShapeDtypeStruct((s, D), BF16), jax.ShapeDtypeStruct((NH, s, HD), F32), jax.ShapeDtypeStruct((s, D), BF16),
            jax.ShapeDtypeStruct((s, D), F32), jax.ShapeDtypeStruct((s, D), BF16),
            jax.ShapeDtypeStruct((1, D), F32), jax.ShapeDtypeStruct((1, D), F32),
        ],
        compiler_params=_params(dimension_semantics=("arbitrary",)),
    )(o, zr, h, zr, dmix, w_out, gain_a, gain_l)


def _residual(x, ycat, w_out, post_gain):
    s = x.shape[0]

    def body(x_ref, y_ref, w_ref, g_ref, m_ref, h_ref, hb_ref):
        mix = jnp.dot(y_ref[...], w_ref[...], preferred_element_type=F32)
        m_ref[...] = mix
        mhat, _ = _rms_fwd(mix, None)
        h1 = x_ref[...] + mhat * g_ref[...]
        h_ref[...] = h1
        hb_ref[...] = h1.astype(BF16)

    return pl.pallas_call(
        body, name="residual", grid=(s // TS,),
        in_specs=[_row(D), _row(2 * D), _vec(2 * D, D), _vec(1, D)], out_specs=[_row(D), _row(D), _row(D)],
        out_shape=[jax.ShapeDtypeStruct((s, D), F32), jax.ShapeDtypeStruct((s, D), F32), jax.ShapeDtypeStruct((s, D), BF16)],
        compiler_params=_params(dimension_semantics=("parallel",)),
    )(x, ycat, w_out, post_gain)


def _head(h1, p, tgt, mix, w_gate, w_ple, ple_gain, b_gate, post_gain):
    s = h1.shape[0]

    def body(h_ref, p_ref, t_ref, m_ref, wg_ref, wp_ref, kg_ref, b_ref, pg_ref,
             loss_ref, dgp_ref, dpe_ref, dh_ref, dm_ref, gk_ref, gb_ref, gg_ref):
        @pl.when(pl.program_id(0) == 0)
        def _():
            for ref in (loss_ref, gk_ref, gb_ref, gg_ref):
                ref[...] = jnp.zeros_like(ref)

        h1 = h_ref[...]
        pe = jnp.dot(p_ref[...].astype(BF16), wp_ref[...], preferred_element_type=F32)
        gp = jnp.dot(h1.astype(BF16), wg_ref[...], preferred_element_type=F32)
        ehat, rstd = _rms_fwd(pe, None)
        e = ehat * kg_ref[...]
        gate = _sigmoid(gp + b_ref[...])
        diff = (h1 + gate * e) - t_ref[...]
        per_row = jnp.mean(diff * diff, axis=-1, keepdims=True)
        loss_ref[...] += 0.5 * jnp.sum(per_row, axis=0, keepdims=True)
        dy = diff * (1.0 / D)
        dgp = dy * e * gate * (1.0 - gate)
        dgpb = dgp.astype(BF16)
        dgp_ref[...] = dgpb
        gb_ref[...] += _colsum(dgp)
        de = dy * gate
        gk_ref[...] += _colsum(de * ehat)
        dpe_ref[...] = _rms_bwd(ehat, rstd, de * kg_ref[...]).astype(BF16)
        dh1 = dy + lax.dot_general(dgpb, wg_ref[...], NT_DIMS, preferred_element_type=F32)
        dh_ref[...] = dh1
        mhat, rstd_m = _rms_fwd(m_ref[...], None)
        gg_ref[...] += _colsum(dh1 * mhat)
        dm_ref[...] = _rms_bwd(mhat, rstd_m, dh1 * pg_ref[...]).astype(BF16)

    return pl.pallas_call(
        body, name="head", grid=(s // TS,),
        in_specs=[_row(D), _row(D_PLE), _row(D), _row(D), _vec(D, D), _vec(D_PLE, D), _vec(1, D), _vec(1, D), _vec(1, D)],
        out_specs=[_vec(1, 1), _row(D), _row(D), _row(D), _row(D), _vec(1, D), _vec(1, D), _vec(1, D)],
        out_shape=[
            jax.ShapeDtypeStruct((1, 1), F32), jax.ShapeDtypeStruct((s, D), BF16), jax.ShapeDtypeStruct((s, D), BF16),
            jax.ShapeDtypeStruct((s, D), F32), jax.ShapeDtypeStruct((s, D), BF16),
            jax.ShapeDtypeStruct((1, D), F32), jax.ShapeDtypeStruct((1, D), F32), jax.ShapeDtypeStruct((1, D), F32),
        ],
        compiler_params=_params(dimension_semantics=("arbitrary",)),
    )(h1, p, tgt, mix, w_gate, w_ple, ple_gain, b_gate, post_gain)


def _prenorm_bwd(x, dxn_a, dxn_b, dh1, pre_gain):
    s = x.shape[0]

    def body(x_ref, da_ref, db_ref, dh_ref, g_ref, dx_ref, gg_ref):
        @pl.when(pl.program_id(0) == 0)
        def _():
            gg_ref[...] = jnp.zeros_like(gg_ref)

        xhat, rstd = _rms_fwd(x_ref[...], None)
        dxn = da_ref[...] + db_ref[...]
        gg_ref[...] += _colsum(dxn * xhat)
        dx_ref[...] = dh_ref[...] + _rms_bwd(xhat, rstd, dxn * g_ref[...])

    return pl.pallas_call(
        body, name="prenorm_bwd", grid=(s // TS,),
        in_specs=[_row(D), _row(D), _row(D), _row(D), _vec(1, D)], out_specs=[_row(D), _vec(1, D)],
        out_shape=[jax.ShapeDtypeStruct((s, D), F32), jax.ShapeDtypeStruct((1, D), F32)],
        compiler_params=_params(dimension_semantics=("arbitrary",)),
    )(x, dxn_a, dxn_b, dh1, pre_gain)


def _adamw(name, parts, w, m, v, own=None, me=None):
    r, c = w.shape
    if r % 8 == 0:
        tr = _pick(r, (256, 128, 16, 8))
        grid = (r // tr,)
        blk = pl.BlockSpec((tr, c), lambda i: (i, 0))
        parts_blk = pl.BlockSpec((N_DEV, tr, c), lambda i: (0, i, 0))
    else:
        tc = _pick(c, (256, 128))
        grid = (c // tc,)
        blk = pl.BlockSpec((r, tc), lambda i: (0, i))
        parts_blk = pl.BlockSpec((N_DEV, r, tc), lambda i: (0, 0, i))

    def body(*refs):
        p_ref, w_ref, m_ref, v_ref = refs[:4]
        g_ref, d_ref, nm_ref, nv_ref = refs[-4:]
        if own is None:
            g = p_ref[0].astype(F32)
            for j in range(1, N_DEV):
                g = g + p_ref[j].astype(F32)
            g_ref[...] = g
        else:
            own_ref, me_ref = refs[4:6]
            g_ref[...] = jnp.zeros_like(g_ref)
            for j in range(N_DEV):
                @pl.when(me_ref[0] == j)
                def _():
                    g_ref[...] += own_ref[...].astype(F32)

                @pl.when(me_ref[0] != j)
                def _():
                    g_ref[...] += p_ref[j].astype(F32)
            g = g_ref[...]
        nm = ADAM_B1 * m_ref[...] + (1.0 - ADAM_B1) * g
        nv = ADAM_B2 * v_ref[...] + (1.0 - ADAM_B2) * (g * g)
        nm_ref[...] = nm
        nv_ref[...] = nv
        m_hat = nm / (1.0 - ADAM_B1 ** ADAM_STEP)
        v_hat = nv / (1.0 - ADAM_B2 ** ADAM_STEP)
        d_ref[...] = -ADAM_LR * (m_hat / (jnp.sqrt(v_hat) + ADAM_EPS) + ADAM_WD * w_ref[...])

    in_specs, args = [parts_blk, blk, blk, blk], [parts, w, m, v]
    if own is not None:
        in_specs += [blk, pl.BlockSpec(memory_space=pltpu.SMEM)]
        args += [own, me]
    return pl.pallas_call(
        body, name=name, grid=grid,
        in_specs=in_specs,
        out_specs=[blk] * 4,
        out_shape=[jax.ShapeDtypeStruct((r, c), F32)] * 4,
        compiler_params=_params(dimension_semantics=("parallel",)),
    )(*args)


def _spread8(v):
    r = v.shape[0]
    return jnp.pad(jnp.pad(v[:, :, None], ((0, 0), (0, 0), (0, 7))).reshape(r, 8 * NH), ((0, 0), (0, HD - 8 * NH)))


def _gather8(v):
    return v[:, :8 * NH].reshape(v.shape[0], NH, 8)[:, :, 0]


def _cols_to_shards(g):
    r, c8 = g.shape
    return g.reshape(r, N_DEV, c8 // N_DEV).transpose(1, 0, 2)


def _shards_to_cols(g):
    n, r, c = g.shape
    return g.transpose(1, 0, 2).reshape(r, n * c)


def kernel(x, p, w_in, b_f, pre_gain, post_gain, conv_w, conv_b, w_rgate, b_rgate, w_igate, b_igate, lru_lambda, attn_out_gain, lru_out_gain, w_out, w_ple, ple_gain, w_ple_gate, b_ple_gate, loss_target, m_w_in, m_b_f, m_pre_gain, m_post_gain, m_conv_w, m_conv_b, m_w_rgate, m_b_rgate, m_w_igate, m_b_igate, m_lru_lambda, m_attn_out_gain, m_lru_out_gain, m_w_out, m_w_ple, m_ple_gain, m_w_ple_gate, m_b_ple_gate, v_w_in, v_b_f, v_pre_gain, v_post_gain, v_conv_w, v_conv_b, v_w_rgate, v_b_rgate, v_w_igate, v_b_igate, v_lru_lambda, v_attn_out_gain, v_lru_out_gain, v_w_out, v_w_ple, v_ple_gain, v_w_ple_gate, v_b_ple_gate):
    me = 4 * lax.axis_index("x") + 2 * lax.axis_index("y") + lax.axis_index("c")
    x2, p2, tgt = x[0], p[0, 0], loss_target[0]

    conv_w_shard8 = jnp.pad(conv_w[0], ((0, 8 - CONV_W), (0, 0)))
    wt, m_wt, v_wt = w_in[0].T, m_w_in[0].T, v_w_in[0].T
    g_wint, g_conv = _gather_two_level("gather_w_in", [wt.astype(BF16), conv_w_shard8])
    win_t = g_wint.reshape(D_IN, D)
    rest_state, rest_token = _exchange_start(
        "gather_rest_start", [w_out[0].astype(BF16), w_ple[0].astype(BF16), w_ple_gate[0].astype(BF16)], ["bcast"] * 3,
        after=g_conv)
    w_qkv_t = win_t[:D_QKV]
    w_rest_t = jnp.concatenate([win_t[D_QKV + NH:], _spread8(win_t[D_QKV:D_QKV + NH].T).T], axis=0)
    conv_w8 = _shards_to_cols(g_conv)
    bf_pad = _spread8(b_f)
    w_r, w_i = w_rgate[0].astype(BF16), w_igate[0].astype(BF16)

    xn, zq = _prenorm_proj(x2, pre_gain, w_qkv_t, rest_token)
    zr = _mm("proj_rest", xn, w_rest_t, "nt", F32)
    kx = _forget_fwd(zr, bf_pad)
    o, ax = _attn_fwd(zq, kx)
    xc, h = _lru_fwd(zr, conv_w8, conv_b, w_r, b_rgate, w_i, b_igate, lru_lambda)
    ycat = _branch_out(o, h, zr, attn_out_gain, lru_out_gain)
    g_wout, g_wple, g_wpg = _exchange_wait("gather_rest_wait", rest_state, ycat)
    wout_full = g_wout.reshape(2 * D, D)
    wple_full = _shards_to_cols(g_wple)
    wpg_full = g_wpg.reshape(D, D)
    mix, h1, h1b = _residual(x2, ycat, wout_full, post_gain)

    loss_part, dgp, dpe, dh1, dmix, g_ple_gain, g_b_gate, g_post_gain = _head(
        h1, p2, tgt, mix, wpg_full, wple_full, ple_gain, b_ple_gate, post_gain)
    loss = lax.psum(loss_part[0, 0], ("x", "y", "c"))
    gw_pg = _mm("bwd_gate_w", h1b, dgp, "tn", BF16)
    gw_ple = _mm("bwd_ple_w", p2, dpe, "tn", BF16)
    gw_out = _mm("bwd_out_w", ycat, dmix, "tn", BF16)
    do, delta, dga, dh, dgl, g_aog, g_log = _branch_out_bwd(o, h, zr, dmix, wout_full, attn_out_gain, lru_out_gain)
    dxl, g_wr, g_wi, g_br, g_bi, g_lam, g_cb, g_cw8 = _lru_bwd(
        zr, xc, h, dh, conv_w8, w_r, b_rgate, w_i, b_igate, lru_lambda)
    gates = jnp.concatenate([g_wr.reshape(D, HD), g_wi.reshape(D, HD)], axis=0).astype(BF16)
    outw_state, outw_token = _exchange_start(
        "exchange_outw_start",
        [gw_out.reshape(N_DEV, 2 * D // N_DEV, D), _cols_to_shards(gw_ple), gw_pg.reshape(N_DEV, D // N_DEV, D), gates],
        ["scatter"] * 3 + ["bcast"])
    dq, dk, dv, dcs, drs = _attn_bwd(zq, do, ax, delta, kx, outw_token)
    dfl, g_bf_pad = _forget_bwd(dcs, drs, zr, bf_pad)
    gw_pieces = [_mm("bwd_w_" + nm, dz, xn, "tn", BF16) for nm, dz in
                 (("q", dq), ("k", dk), ("v", dv), ("fl", dfl), ("ga", dga), ("xl", dxl), ("gl", dgl))]
    gw_pieces[3] = _gather8(gw_pieces[3].T).T
    gw_in_t = jnp.concatenate(gw_pieces, axis=0)
    inw_state, inw_token = _exchange_start(
        "exchange_inw_start", [gw_in_t.reshape(N_DEV, D_IN_SHARD, D)], ["scatter"])
    dxn_a = _mm_cat("bwd_qkv_x", [dq, dk, dv], w_qkv_t, F32, after=inw_token)
    dxn_b = _mm_cat("bwd_rest_x", [dga, dxl, dgl, dfl], w_rest_t, F32, after=inw_token)
    grad_x, g_pre_gain = _prenorm_bwd(x2, dxn_a, dxn_b, dh1, pre_gain)

    upd = {}
    me1 = me.reshape(1).astype(jnp.int32)
    r_wout, r_wple, r_wpg, r_gates = _exchange_wait("exchange_outw_wait", outw_state, grad_x, fill_own=False)
    upd["w_out"] = _adamw("adamw_w_out", r_wout[0], w_out[0], m_w_out[0], v_w_out[0], r_wout[1], me1)
    upd["w_ple"] = _adamw("adamw_w_ple", r_wple[0], w_ple[0], m_w_ple[0], v_w_ple[0], r_wple[1], me1)
    upd["w_ple_gate"] = _adamw("adamw_w_ple_gate", r_wpg[0], w_ple_gate[0], m_w_ple_gate[0], v_w_ple_gate[0], r_wpg[1], me1)
    gates_of = lambda a, b: jnp.concatenate([a[0].reshape(D, HD), b[0].reshape(D, HD)], axis=0)
    g_gates = _adamw("adamw_gates", r_gates[0], gates_of(w_rgate, w_igate), gates_of(m_w_rgate, m_w_igate),
                     gates_of(v_w_rgate, v_w_igate), r_gates[1], me1)
    upd["w_rgate"] = [a[:D].reshape(1, NH, HD, HD) for a in g_gates]
    upd["w_igate"] = [a[D:].reshape(1, NH, HD, HD) for a in g_gates]
    behind = upd["w_out"][0][0:1] + upd["w_ple_gate"][0][0:1] + jnp.pad(g_gates[0][0:1], ((0, 0), (0, D - HD)))
    small = jnp.concatenate(
        [jnp.pad(_gather8(g_bf_pad), ((0, 0), (0, D - NH))), g_pre_gain, g_post_gain, g_cb, g_br, g_bi, g_lam, g_aog, g_log,
         g_ple_gain, g_b_gate, g_cw8[:CONV_W], behind], axis=0)
    (r_small,) = _exchange("exchange_small", [small], ["bcast"])
    vec_names = ["b_f", "pre_gain", "post_gain", "conv_b", "b_rgate", "b_igate", "lru_lambda", "attn_out_gain",
                 "lru_out_gain", "ple_gain", "b_ple_gate"]
    vec_w = dict(b_f=(b_f, m_b_f, v_b_f), pre_gain=(pre_gain, m_pre_gain, v_pre_gain),
                 post_gain=(post_gain, m_post_gain, v_post_gain), conv_b=(conv_b, m_conv_b, v_conv_b),
                 b_rgate=(b_rgate, m_b_rgate, v_b_rgate), b_igate=(b_igate, m_b_igate, v_b_igate),
                 lru_lambda=(lru_lambda, m_lru_lambda, v_lru_lambda),
                 attn_out_gain=(attn_out_gain, m_attn_out_gain, v_attn_out_gain),
                 lru_out_gain=(lru_out_gain, m_lru_out_gain, v_lru_out_gain), ple_gain=(ple_gain, m_ple_gain, v_ple_gain),
                 b_ple_gate=(b_ple_gate, m_b_ple_gate, v_b_ple_gate))
    conv_mine = lambda a: lax.dynamic_slice_in_dim(a, me * HD, HD, axis=1)

    def small_rows(k):
        rows = [jnp.pad(vec_w[nm][k], ((0, 0), (0, D - vec_w[nm][k].shape[1]))) for nm in vec_names]
        cw = (conv_w, m_conv_w, v_conv_w)[k][0]
        full = lax.dynamic_update_slice_in_dim(jnp.ones((CONV_W, D), F32), cw, me * HD, axis=1)
        return jnp.concatenate(rows + [full, jnp.ones((1, D), F32)], axis=0)

    g_small = _adamw("adamw_small", r_small, small_rows(0), small_rows(1), small_rows(2))
    for idx, nm in enumerate(vec_names):
        width = vec_w[nm][0].shape[1]
        upd[nm] = [a[idx:idx + 1, :width] for a in g_small]
    base = len(vec_names)
    upd["conv_w"] = [conv_mine(a[base:base + CONV_W])[None] for a in g_small]
    (r_win,) = _exchange_wait("exchange_inw_wait", inw_state, g_small[0], fill_own=False)
    upd["w_in"] = [a.T for a in _adamw("adamw_w_in", r_win[0], wt, m_wt, v_wt, r_win[1], me1)]
    for nm in ("w_in", "w_out", "w_ple", "w_ple_gate"):
        upd[nm] = [a[None] for a in upd[nm]]

    order = ["w_in", "b_f", "pre_gain", "post_gain", "conv_w", "conv_b", "w_rgate", "b_rgate", "w_igate", "b_igate",
             "lru_lambda", "attn_out_gain", "lru_out_gain", "w_out", "w_ple", "ple_gain", "w_ple_gate", "b_ple_gate"]
    outs = [loss, grad_x[None]]
    for k in range(4):
        outs += [upd[nm][k] for nm in order]
    return tuple(outs)
```

```python
import functools

import jax
import jax.numpy as jnp
from jax import lax
from jax.experimental import pallas as pl
from jax.experimental.pallas import tpu as pltpu

F32 = jnp.float32
BF16 = jnp.bfloat16

N_DEV = 8
D = 1024
HD = 128
NH = 8
D_IN = 6152
D_IN_SHARD = D_IN // N_DEV
D_QKV = 3 * D
D_REST = 3 * D + HD
FL_COL = 3 * D
D_PLE = 256
CONV_W = 4
LRU_C = 8.0
RMS_EPS = 1e-6
SCALE = HD ** -0.5
EXP2_SCALE = SCALE * 1.4426950408889634
NEG = -1e30

ADAM_LR = 0.001
ADAM_B1 = 0.9
ADAM_B2 = 0.999
ADAM_EPS = 1e-08
ADAM_WD = 0.01
ADAM_STEP = 10

TS = 256
TQ = 1024
ROW_PARTS = 2
VMEM_LIMIT = 48 * 1024 * 1024

NT_DIMS = (((1,), (1,)), ((), ()))
TN_DIMS = (((0,), (0,)), ((), ()))


def _params(**kw):
    return pltpu.CompilerParams(vmem_limit_bytes=VMEM_LIMIT, **kw)


def _sigmoid(v):
    return 0.5 * jnp.tanh(0.5 * v) + 0.5


def _sigmoid_rel(v):
    return 1.0 / (1.0 + jnp.exp(-v))


def _rms_fwd(v, gain):
    rstd = lax.rsqrt(jnp.mean(v * v, axis=-1, keepdims=True) + RMS_EPS)
    return v * rstd, rstd


def _rms_bwd(vhat, rstd, dvhat):
    return rstd * (dvhat - vhat * jnp.mean(dvhat * vhat, axis=-1, keepdims=True))


def _colsum(v):
    return jnp.sum(v, axis=0, keepdims=True)


def _rows_iota(t):
    return lax.broadcasted_iota(jnp.int32, (t, 1), 0)


def _scan(a, u, reverse):
    t, c = a.shape
    rows = _rows_iota(t)
    d = 1
    while d < t:
        if d < 8:
            valid = rows < t - d if reverse else rows >= d
            shift = t - d if reverse else d
            u = jnp.where(valid, u + a * pltpu.roll(u, shift, 0), u)
            a = jnp.where(valid, a * pltpu.roll(a, shift, 0), a)
        else:
            zeros, ones = jnp.zeros((d, c), F32), jnp.ones((d, c), F32)
            if reverse:
                u_far, a_far = jnp.concatenate([u[d:], zeros], axis=0), jnp.concatenate([a[d:], ones], axis=0)
            else:
                u_far, a_far = jnp.concatenate([zeros, u[:t - d]], axis=0), jnp.concatenate([ones, a[:t - d]], axis=0)
            u = u + a * u_far
            a = a * a_far
        d *= 2
    return a, u


def _cumsum_fwd(v):
    t = v.shape[0]
    rows = _rows_iota(t)
    d = 1
    while d < t:
        v = jnp.where(rows >= d, v + pltpu.roll(v, d, 0), v)
        d *= 2
    return v


def _cumsum_bwd(v):
    t = v.shape[0]
    rows = _rows_iota(t)
    d = 1
    while d < t:
        v = jnp.where(rows < t - d, v + pltpu.roll(v, t - d, 0), v)
        d *= 2
    return v


def _bias_lanes(v, at, ones_at):
    lane = lax.broadcasted_iota(jnp.int32, v.shape, 1)
    hi = v.astype(BF16).astype(F32)
    mid = (v - hi).astype(BF16).astype(F32)
    lo = ((v - hi) - mid).astype(BF16).astype(F32)
    out = jnp.where((lane >= ones_at) & (lane < ones_at + 3), 1.0, 0.0)
    for k, piece in enumerate((hi, mid, lo)):
        out = jnp.where(lane == at + k, piece, out)
    return out.astype(BF16)


def _shift_down(ext, k, t):
    return pltpu.roll(ext, k, 0)[8:, :] if k else ext[8:, :]


def _shift_up(ext, k, t):
    return pltpu.roll(ext, t + 8 - k, 0)[:t, :] if k else ext[:t, :]


def _exchange(name, arrs, kinds):
    n = len(arrs)
    out_shape = []
    for a, kind in zip(arrs, kinds):
        shp = a.shape if kind == "scatter" else (N_DEV,) + a.shape
        out_shape.append(jax.ShapeDtypeStruct(shp, a.dtype))

    def body(*refs):
        ins, outs = refs[:n], refs[n:2 * n]
        send_sems, recv_sems, local_sems = refs[2 * n:]
        x, y, c = lax.axis_index("x"), lax.axis_index("y"), lax.axis_index("c")
        me = 4 * x + 2 * y + c
        copies = []
        for i in range(n):
            scatter = kinds[i] == "scatter"
            mine = pltpu.make_async_copy(ins[i].at[me] if scatter else ins[i], outs[i].at[me], local_sems.at[i])
            mine.start()
            copies.append(mine)
            for m in range(1, N_DEV):
                px = 1 - x if m & 4 else x
                py = 1 - y if m & 2 else y
                pc = 1 - c if m & 1 else c
                peer = 4 * px + 2 * py + pc
                cp = pltpu.make_async_remote_copy(
                    src_ref=ins[i].at[peer] if scatter else ins[i],
                    dst_ref=outs[i].at[me],
                    send_sem=send_sems.at[i, m - 1],
                    recv_sem=recv_sems.at[i, m - 1],
                    device_id=(px, py, pc),
                    device_id_type=pl.DeviceIdType.MESH,
                )
                cp.start()
                copies.append(cp)
        for cp in copies:
            cp.wait()

    any_spec = pl.BlockSpec(memory_space=pl.ANY)
    return pl.pallas_call(
        body,
        name=name,
        out_shape=out_shape,
        in_specs=[any_spec] * n,
        out_specs=[any_spec] * n,
        scratch_shapes=[
            pltpu.SemaphoreType.DMA((n, N_DEV - 1)),
            pltpu.SemaphoreType.DMA((n, N_DEV - 1)),
            pltpu.SemaphoreType.DMA((n,)),
        ],
        compiler_params=pltpu.CompilerParams(has_side_effects=True),
    )(*arrs)


def _gather_two_level(name, arrs, pieces=1):
    n = len(arrs)
    items = []
    for i, a in enumerate(arrs):
        rows = a.shape[0]
        if pieces > 1 and rows >= 512:
            step = -(-rows // (16 * pieces)) * 16
            items += [(i, r0, min(step, rows - r0)) for r0 in range(0, rows, step)]
        else:
            items.append((i, 0, rows))
    n_items = len(items)

    def body(*refs):
        ins, outs = refs[:n], refs[n:2 * n]
        send_sems, recv_sems, local_sems = refs[2 * n:]
        x, y, c = lax.axis_index("x"), lax.axis_index("y"), lax.axis_index("c")
        me, sibling = (x, y, c), (x, y, 1 - c)
        chips = [(1 - x, y), (x, 1 - y), (1 - x, 1 - y)]

        def rows_of(ref, t):
            i, r0, rn = items[t]
            return ref if rn == arrs[i].shape[0] else ref.at[pl.ds(r0, rn)]

        def slot(t, dev):
            return rows_of(outs[items[t][0]].at[4 * dev[0] + 2 * dev[1] + dev[2]], t)

        def copy(t, k, block, to, from_input=False):
            return pltpu.make_async_remote_copy(
                src_ref=rows_of(ins[items[t][0]], t) if from_input else slot(t, block), dst_ref=slot(t, block),
                send_sem=send_sems.at[t, k], recv_sem=recv_sems.at[t, k],
                device_id=to, device_id_type=pl.DeviceIdType.MESH)

        own, sent = [], []
        for t in range(n_items):
            mine = pltpu.make_async_copy(rows_of(ins[items[t][0]], t), slot(t, me), local_sems.at[t])
            mine.start()
            own.append(mine)
            first = [copy(t, 1 + j, me, (*chip, c), from_input=True) for j, chip in enumerate(chips)]
            first.append(copy(t, 0, me, sibling, from_input=True))
            for cp in first:
                cp.start()
            sent += first
        for t in range(n_items):
            for j, chip in enumerate(chips):
                copy(t, 1 + j, (*chip, c), me).wait_recv()
                fwd = copy(t, 4 + j, (*chip, c), sibling)
                fwd.start()
                sent.append(fwd)
        for t in range(n_items):
            copy(t, 0, sibling, me).wait_recv()
            for j, chip in enumerate(chips):
                copy(t, 4 + j, (*chip, 1 - c), me).wait_recv()
        for cp in sent:
            cp.wait_send()
        for cp in own:
            cp.wait()

    any_spec = pl.BlockSpec(memory_space=pl.ANY)
    return pl.pallas_call(
        body, name=name,
        out_shape=[jax.ShapeDtypeStruct((N_DEV,) + a.shape, a.dtype) for a in arrs],
        in_specs=[any_spec] * n, out_specs=[any_spec] * n,
        scratch_shapes=[pltpu.SemaphoreType.DMA((n_items, 7)), pltpu.SemaphoreType.DMA((n_items, 7)),
                        pltpu.SemaphoreType.DMA((n_items,))],
        compiler_params=pltpu.CompilerParams(has_side_effects=True),
    )(*arrs)


def _peers(x, y, c):
    out = []
    for m in range(1, N_DEV):
        px = 1 - x if m & 4 else x
        py = 1 - y if m & 2 else y
        pc = 1 - c if m & 1 else c
        out.append((m, (px, py, pc), 4 * px + 2 * py + pc))
    return out


def _split_copies(kinds, src_refs, land_refs, send_sems, recv_sems):
    x, y, c = lax.axis_index("x"), lax.axis_index("y"), lax.axis_index("c")
    me = 4 * x + 2 * y + c
    copies = []
    for i, kind in enumerate(kinds):
        for m, peer, pidx in _peers(x, y, c):
            copies.append(pltpu.make_async_remote_copy(
                src_ref=src_refs[i].at[pidx] if kind == "scatter" else src_refs[i],
                dst_ref=land_refs[i].at[me],
                send_sem=send_sems.at[i * (N_DEV - 1) + m - 1],
                recv_sem=recv_sems.at[i * (N_DEV - 1) + m - 1],
                device_id=peer,
                device_id_type=pl.DeviceIdType.MESH,
            ))
    return copies


_HBM_SPEC = pl.BlockSpec(memory_space=pltpu.HBM)
_SEM_SPEC = pl.BlockSpec(memory_space=pltpu.SEMAPHORE)
_DATAFLOW = pltpu.SideEffectType.DATAFLOW_SIDE_EFFECTING


def _exchange_start(name, arrs, kinds, after=None):
    n = len(arrs)
    extra = [] if after is None else [after]
    lands = []
    for a, kind in zip(arrs, kinds):
        shp = a.shape if kind == "scatter" else (N_DEV,) + a.shape
        lands.append(lax.empty(shp, a.dtype))

    def body(*refs):
        src_refs, land_refs = refs[:n], refs[n:2 * n]
        send_sems, recv_sems = refs[2 * n + len(extra):2 * n + len(extra) + 2]
        token = refs[-1]
        for cp in _split_copies(kinds, src_refs, land_refs, send_sems, recv_sems):
            cp.start()
        token[...] = jnp.zeros_like(token)

    n_sem = n * (N_DEV - 1)
    hbm = lambda a: pltpu.HBM(a.shape, a.dtype)
    res = pl.pallas_call(
        body, name=name,
        out_shape=(pltpu.SemaphoreType.DMA((n_sem,)), pltpu.SemaphoreType.DMA((n_sem,)),
                   *[hbm(a) for a in arrs], *[hbm(a) for a in lands], jax.ShapeDtypeStruct((8, HD), F32)),
        in_specs=[_HBM_SPEC] * (2 * n) + [pl.BlockSpec(memory_space=pl.ANY)] * len(extra),
        out_specs=(_SEM_SPEC, _SEM_SPEC, *[_HBM_SPEC] * (2 * n), pl.BlockSpec(memory_space=pltpu.VMEM)),
        input_output_aliases={i: 2 + i for i in range(2 * n)},
        compiler_params=pltpu.CompilerParams(has_side_effects=_DATAFLOW),
    )(*[pltpu.with_memory_space_constraint(a, pltpu.HBM) for a in arrs],
      *[pltpu.with_memory_space_constraint(a, pltpu.HBM) for a in lands], *extra)
    return (kinds, res[0], res[1], res[2:2 + n], res[2 + n:2 + 2 * n]), res[-1]


def _exchange_wait(name, state, after, fill_own=True):
    kinds, send_sems, recv_sems, srcs, lands = state
    n = len(srcs)

    def body(*refs):
        src_refs, land_refs = refs[:n], refs[n:2 * n]
        send_sems_ref, recv_sems_ref = refs[2 * n:2 * n + 2]
        for cp in _split_copies(kinds, src_refs, land_refs, send_sems_ref, recv_sems_ref):
            cp.wait_send()
            cp.wait_recv()

    res = pl.pallas_call(
        body, name=name,
        out_shape=tuple(pltpu.HBM(a.shape, a.dtype) for a in (*srcs, *lands)),
        in_specs=[_HBM_SPEC] * (2 * n) + [_SEM_SPEC, _SEM_SPEC, pl.BlockSpec(memory_space=pl.ANY)],
        out_specs=tuple([_HBM_SPEC] * (2 * n)),
        input_output_aliases={i: i for i in range(2 * n)},
        compiler_params=pltpu.CompilerParams(has_side_effects=_DATAFLOW),
    )(*srcs, *lands, send_sems, recv_sems, after)
    me = 4 * lax.axis_index("x") + 2 * lax.axis_index("y") + lax.axis_index("c")
    outs = []
    for kind, src, land in zip(kinds, res[:n], res[n:]):
        own = lax.dynamic_index_in_dim(src, me, 0, keepdims=False) if kind == "scatter" else src
        outs.append(lax.dynamic_update_index_in_dim(land, own, me, 0) if fill_own else (land, own))
    return outs


def _pick(n, cands):
    for t in cands:
        if n % t == 0:
            return t
    raise ValueError(f"no tile for {n}")


def _mm(name, a, b, mode, out_dtype, after=None):
    if mode == "nn":
        (m, k), (k2, n) = a.shape, b.shape
    elif mode == "nt":
        (m, k), (n, k2) = a.shape, b.shape
    else:
        (k, m), (k2, n) = a.shape, b.shape
    assert k == k2, (name, a.shape, b.shape)
    if mode == "tn":
        tm = _pick(m, (1024, 640, 512, 256, 128))
        tn = _pick(n, (1024, 640, 512, 256, 128))
        tk = _pick(k, (2048, 1024, 512, 256))
    else:
        tm, tn, tk = _pick(m, (512, 256)), n, k
    nk = k // tk

    def body(a_ref, b_ref, *rest):
        o_ref = rest[-2] if nk > 1 else rest[-1]
        av = a_ref[...].astype(BF16)
        bv = b_ref[...].astype(BF16)
        if mode == "nn":
            part = jnp.dot(av, bv, preferred_element_type=F32)
        elif mode == "nt":
            part = lax.dot_general(av, bv, NT_DIMS, preferred_element_type=F32)
        else:
            part = lax.dot_general(av, bv, TN_DIMS, preferred_element_type=F32)
        if nk == 1:
            o_ref[...] = part.astype(out_dtype)
            return
        acc_ref = rest[-1]
        kk = pl.program_id(2)

        @pl.when(kk == 0)
        def _():
            acc_ref[...] = part

        @pl.when(kk > 0)
        def _():
            acc_ref[...] += part

        @pl.when(kk == nk - 1)
        def _():
            o_ref[...] = acc_ref[...].astype(out_dtype)

    if mode == "tn":
        a_spec = pl.BlockSpec((tk, tm), lambda j, i, kk: (kk, i))
    else:
        a_spec = pl.BlockSpec((tm, tk), lambda j, i, kk: (i, kk))
    if mode == "nt":
        b_spec = pl.BlockSpec((tn, tk), lambda j, i, kk: (j, kk))
    else:
        b_spec = pl.BlockSpec((tk, tn), lambda j, i, kk: (kk, j))
    in_specs, args = [a_spec, b_spec], [a, b]
    if after is not None:
        in_specs.append(pl.BlockSpec((8, HD), lambda j, i, kk: (0, 0)))
        args.append(after)
    return pl.pallas_call(
        body,
        name=name,
        grid=(n // tn, m // tm, nk),
        in_specs=in_specs,
        out_specs=pl.BlockSpec((tm, tn), lambda j, i, kk: (i, j)),
        out_shape=jax.ShapeDtypeStruct((m, n), out_dtype),
        scratch_shapes=[pltpu.VMEM((tm, tn), F32)] if nk > 1 else [],
        compiler_params=_params(dimension_semantics=("parallel", "parallel", "arbitrary")),
    )(*args)


def _mm_cat(name, a_list, b, out_dtype, after=None):
    m = a_list[0].shape[0]
    ks = [a.shape[1] for a in a_list]
    n = b.shape[1]
    assert sum(ks) == b.shape[0], (name, ks, b.shape)
    tm = _pick(m, (512, 256))
    na = len(a_list)

    def body(*refs):
        b_ref, o_ref = refs[na], refs[-1]
        k0, acc = 0, None
        for a_ref, kw in zip(refs[:na], ks):
            part = jnp.dot(a_ref[...].astype(BF16), b_ref[k0:k0 + kw, :], preferred_element_type=F32)
            acc = part if acc is None else acc + part
            k0 += kw
        o_ref[...] = acc.astype(out_dtype)

    in_specs = [pl.BlockSpec((tm, kw), lambda i: (i, 0)) for kw in ks] + [pl.BlockSpec(b.shape, lambda i: (0, 0))]
    args = [*a_list, b]
    if after is not None:
        in_specs.append(pl.BlockSpec((8, HD), lambda i: (0, 0)))
        args.append(after)
    return pl.pallas_call(
        body, name=name, grid=(m // tm,),
        in_specs=in_specs, out_specs=pl.BlockSpec((tm, n), lambda i: (i, 0)),
        out_shape=jax.ShapeDtypeStruct((m, n), out_dtype),
        compiler_params=_params(dimension_semantics=("parallel",)),
    )(*args)


def _row(c, col=0):
    return pl.BlockSpec((TS, c), lambda i: (i, col))


def _vec(r, c):
    return pl.BlockSpec((r, c), lambda i: (0, 0))


def _prenorm_proj(x, pre_gain, w_t, after):
    s = x.shape[0]
    n = w_t.shape[0]
    tm = 512

    def body(x_ref, g_ref, w_ref, after_ref, xn_ref, z_ref):
        xhat, _ = _rms_fwd(x_ref[...], None)
        xn = (xhat * g_ref[...]).astype(BF16)
        xn_ref[...] = xn
        z_ref[...] = lax.dot_general(xn, w_ref[...], NT_DIMS, preferred_element_type=F32).astype(BF16)

    return pl.pallas_call(
        body, name="prenorm_proj_qkv", grid=(s // tm,),
        in_specs=[pl.BlockSpec((tm, D), lambda i: (i, 0)), _vec(1, D), _vec(n, D), _vec(8, HD)],
        out_specs=[pl.BlockSpec((tm, D), lambda i: (i, 0)), pl.BlockSpec((tm, n), lambda i: (i, 0))],
        out_shape=[jax.ShapeDtypeStruct((s, D), BF16), jax.ShapeDtypeStruct((s, n), BF16)],
        compiler_params=_params(dimension_semantics=("parallel",)),
    )(x, pre_gain, w_t, after)


def _forget_fwd(zr, bf_pad):
    s = zr.shape[0]
    n = s // TQ

    def body(fl_ref, b_ref, kx_ref, c_buf, carry):
        i = pl.program_id(0)

        @pl.when(i == 0)
        def _():
            carry[...] = jnp.zeros_like(carry)

        fl = fl_ref[...] + b_ref[...]
        ls = jnp.minimum(fl, 0.0) - jnp.log(1.0 + jnp.exp(-jnp.abs(fl)))
        c_buf[...] = _cumsum_fwd(ls) + carry[0:1, :]
        carry[0:1, :] = c_buf[TQ - 1:TQ, :]
        cv = c_buf[...]
        for h in range(NH):
            kx_ref[h] = _bias_lanes(jnp.broadcast_to(cv[:, 8 * h:8 * h + 1], (TQ, HD)) * (-1.0 / SCALE), 0, 3)

    return pl.pallas_call(
        body, name="forget_fwd", grid=(n,),
        in_specs=[pl.BlockSpec((TQ, HD), lambda i: (i, FL_COL // HD)), _vec(1, HD)],
        out_specs=pl.BlockSpec((NH, TQ, HD), lambda i: (0, i, 0)),
        out_shape=jax.ShapeDtypeStruct((NH, s, HD), BF16),
        scratch_shapes=[pltpu.VMEM((TQ, HD), F32), pltpu.VMEM((8, HD), F32)],
        compiler_params=_params(dimension_semantics=("arbitrary",)),
    )(zr, bf_pad)


def _attn_fwd(zq, kx):
    s = zq.shape[0]
    n = s // TQ
    nb = TQ // HD

    def body(q_ref, k_ref, v_ref, kx_ref, o_ref, ax_ref):
        i = pl.program_id(1)
        lane = lax.broadcasted_iota(jnp.int32, (TQ, HD), 1)
        row = lax.broadcasted_iota(jnp.int32, (TQ, HD), 0)
        qa = jnp.concatenate([q_ref[...], jnp.where(lane < 3, 1.0, 0.0).astype(BF16)], axis=1)

        def step(j, carry, masked):
            m, l, acc = carry
            rows = pl.ds(pl.multiple_of(j * TQ, TQ), TQ)
            ka = jnp.concatenate([k_ref[rows, :], kx_ref[0, rows, :]], axis=1)
            v_all = v_ref[rows, :]
            rp = TQ // ROW_PARTS
            parts = [slice(rp * t, rp * (t + 1)) for t in range(ROW_PARTS)]
            keys = [rp * (t + 1) if masked else TQ for t in range(ROW_PARTS)]
            u_parts = [lax.dot_general(qa[part], ka[:kn], NT_DIMS, preferred_element_type=F32)
                       for part, kn in zip(parts, keys)]
            out = []
            for t, (part, u, kn) in enumerate(zip(parts, u_parts, keys)):
                us = [u[:, HD * b:HD * (b + 1)] for b in range(kn // HD)]
                if masked:
                    us = [ub if HD * (b + 1) <= rp * t else jnp.where(row[part] >= lane[part] + HD * b, ub, NEG)
                          for b, ub in enumerate(us)]
                v = v_all[:kn]
                bm = functools.reduce(jnp.maximum, us)
                m_new = jnp.maximum(m[part], jnp.max(bm, axis=1, keepdims=True))
                alpha = jnp.exp2((m[part] - m_new) * EXP2_SCALE)
                shift = m_new * EXP2_SCALE
                ps = [jnp.exp2(ub * EXP2_SCALE - shift) for ub in us]
                l_new = alpha * l[part] + functools.reduce(jnp.add, ps)
                pr = jnp.concatenate(ps, axis=1).astype(BF16)
                out.append((m_new, l_new, alpha * acc[part] + jnp.dot(pr, v, preferred_element_type=F32)))
            return tuple(jnp.concatenate([o[t] for o in out], axis=0) for t in range(3))

        init = (jnp.full((TQ, HD), NEG, F32), jnp.zeros((TQ, HD), F32), jnp.zeros((TQ, HD), F32))
        carry = lax.fori_loop(0, i, lambda j, cr: step(j, cr, False), init)
        m, l, acc = step(i, carry, True)
        l_row = jnp.sum(l, axis=1, keepdims=True)
        o_ref[...] = acc / l_row
        ax_ref[0] = _bias_lanes(-(m + jnp.log(l_row) * (1.0 / SCALE)), 3, 0)

    return pl.pallas_call(
        body, name="attn_fwd", grid=(NH, n),
        in_specs=[
            pl.BlockSpec((TQ, HD), lambda h, i: (i, h)),
            pl.BlockSpec((s, HD), lambda h, i: (0, NH + h)),
            pl.BlockSpec((s, HD), lambda h, i: (0, 2 * NH + h)),
            pl.BlockSpec((1, s, HD), lambda h, i: (h, 0, 0)),
        ],
        out_specs=[pl.BlockSpec((TQ, HD), lambda h, i: (i, h)), pl.BlockSpec((1, TQ, HD), lambda h, i: (h, i, 0))],
        out_shape=[jax.ShapeDtypeStruct((s, D), F32), jax.ShapeDtypeStruct((NH, s, HD), BF16)],
        compiler_params=_params(dimension_semantics=("parallel", "parallel")),
    )(zq, zq, zq, kx)


def _attn_bwd(zq, do, ax, delta, kx, after):
    s = zq.shape[0]
    n = s // TQ
    nb = TQ // HD

    def body(k_ref, v_ref, kx_ref, q_ref, ax_ref, do_ref, dl_ref, after_ref, dq_ref, dk_ref, dv_ref, dcs_ref, drs_ref):
        j = pl.program_id(1)

        @pl.when(j == 0)
        def _():
            dq_ref[...] = jnp.zeros_like(dq_ref)
            drs_ref[...] = jnp.zeros_like(drs_ref)

        k = k_ref[...]
        v = v_ref[...]
        ka = jnp.concatenate([k, kx_ref[0]], axis=1)
        row = lax.broadcasted_iota(jnp.int32, (TQ, HD), 0)
        lane = lax.broadcasted_iota(jnp.int32, (TQ, HD), 1)

        def step(i, carry, r0, rn, kn, masked):
            dk, dv, dcs = carry
            rows = pl.ds(pl.multiple_of(i * TQ + r0, rn), rn)
            q = q_ref[rows, :]
            dout = do_ref[rows, :]
            dlv = dl_ref[0, rows, :]
            qa = jnp.concatenate([q, ax_ref[0, rows, :]], axis=1)
            u = lax.dot_general(qa, ka[:kn], NT_DIMS, preferred_element_type=F32)
            dp = lax.dot_general(dout, v[:kn], NT_DIMS, preferred_element_type=F32)
            prs, dss = [], []
            for b in range(kn // HD):
                cs = slice(HD * b, HD * (b + 1))
                ub = u[:, cs]
                if masked and HD * (b + 1) > r0:
                    ub = jnp.where(row[:rn] + r0 >= lane[:rn] + HD * b, ub, NEG)
                pb = jnp.exp2(ub * EXP2_SCALE)
                prs.append(pb)
                dss.append(pb * (dp[:, cs] - dlv))
            drs_ref[0, rows, :] += functools.reduce(jnp.add, dss)
            ds = jnp.concatenate(dss, axis=1)
            dsb = ds.astype(BF16)
            dcs_new = jnp.sum(ds.reshape(rn // 8, 8, kn), axis=0)
            dv_new = lax.dot_general(jnp.concatenate(prs, axis=1).astype(BF16), dout, TN_DIMS, preferred_element_type=F32)
            dk_new = lax.dot_general(dsb, q, TN_DIMS, preferred_element_type=F32)
            if kn < TQ:
                dcs_new = jnp.concatenate([dcs_new, jnp.zeros((8, TQ - kn), F32)], axis=1)
                dv_new = jnp.concatenate([dv_new, jnp.zeros((TQ - kn, HD), F32)], axis=0)
                dk_new = jnp.concatenate([dk_new, jnp.zeros((TQ - kn, HD), F32)], axis=0)
            dq_ref[rows, :] += jnp.dot(dsb, k[:kn], preferred_element_type=F32) * SCALE
            return dk + dk_new, dv + dv_new, dcs + dcs_new

        carry = (jnp.zeros((TQ, HD), F32), jnp.zeros((TQ, HD), F32), jnp.zeros((8, TQ), F32))
        rp = TQ // ROW_PARTS
        for t in range(ROW_PARTS):
            carry = step(j, carry, rp * t, rp, rp * (t + 1), True)
        dk, dv, dcs = lax.fori_loop(j + 1, n, lambda i, cr: step(i, cr, 0, TQ, TQ, False), carry)
        dk_ref[...] = (dk * SCALE).astype(BF16)
        dv_ref[...] = dv.astype(BF16)
        dcs_ref[0] = jnp.broadcast_to(_colsum(dcs), (8, TQ))

    return pl.pallas_call(
        body, name="attn_bwd", grid=(NH, n),
        in_specs=[
            pl.BlockSpec((TQ, HD), lambda h, j: (j, NH + h)),
            pl.BlockSpec((TQ, HD), lambda h, j: (j, 2 * NH + h)),
            pl.BlockSpec((1, TQ, HD), lambda h, j: (h, j, 0)),
            pl.BlockSpec((s, HD), lambda h, j: (0, h)),
            pl.BlockSpec((1, s, HD), lambda h, j: (h, 0, 0)),
            pl.BlockSpec((s, HD), lambda h, j: (0, h)),
            pl.BlockSpec((1, s, HD), lambda h, j: (h, 0, 0)),
            pl.BlockSpec((8, HD), lambda h, j: (0, 0)),
        ],
        out_specs=[
            pl.BlockSpec((s, HD), lambda h, j: (0, h)),
            pl.BlockSpec((TQ, HD), lambda h, j: (j, h)),
            pl.BlockSpec((TQ, HD), lambda h, j: (j, h)),
            pl.BlockSpec((1, 8, TQ), lambda h, j: (j, h, 0)),
            pl.BlockSpec((1, s, HD), lambda h, j: (h, 0, 0)),
        ],
        out_shape=[
            jax.ShapeDtypeStruct((s, D), F32),
            jax.ShapeDtypeStruct((s, D), BF16),
            jax.ShapeDtypeStruct((s, D), BF16),
            jax.ShapeDtypeStruct((n, 8 * NH, TQ), F32),
            jax.ShapeDtypeStruct((NH, s, HD), F32),
        ],
        compiler_params=_params(dimension_semantics=("parallel", "arbitrary")),
    )(zq, zq, kx, zq, ax, do, delta, after)


def _forget_bwd(dcs, drs, zr, bf_pad):
    n = dcs.shape[0]
    s = n * TQ

    def body(dcs_ref, drs_ref, fl_ref, b_ref, dfl_ref, gb_ref, buf, carry):
        i = pl.program_id(0)

        @pl.when(i == 0)
        def _():
            carry[...] = jnp.zeros_like(carry)
            gb_ref[...] = jnp.zeros_like(gb_ref)

        dc_t = jnp.concatenate([dcs_ref[0], jnp.zeros((HD - 8 * NH, TQ), F32)], axis=0)
        lane = lax.broadcasted_iota(jnp.int32, (TQ, HD), 1)
        dc = -dc_t.T
        for hh in range(NH):
            dc = dc + jnp.where(lane == 8 * hh, jnp.sum(drs_ref[hh], axis=1, keepdims=True), 0.0)
        buf[...] = _cumsum_bwd(dc) + carry[0:1, :]
        carry[0:1, :] = buf[0:1, :]
        fl = fl_ref[...] + b_ref[...]
        dfl = buf[...] * _sigmoid_rel(-fl)
        dfl_ref[...] = dfl.astype(BF16)
        gb_ref[...] += _colsum(dfl)

    return pl.pallas_call(
        body, name="forget_bwd", grid=(n,),
        in_specs=[
            pl.BlockSpec((1, 8 * NH, TQ), lambda i: (n - 1 - i, 0, 0)),
            pl.BlockSpec((NH, TQ, HD), lambda i: (0, n - 1 - i, 0)),
            pl.BlockSpec((TQ, HD), lambda i: (n - 1 - i, FL_COL // HD)),
            _vec(1, HD),
        ],
        out_specs=[pl.BlockSpec((TQ, HD), lambda i: (n - 1 - i, 0)), _vec(1, HD)],
        out_shape=[jax.ShapeDtypeStruct((s, HD), BF16), jax.ShapeDtypeStruct((1, HD), F32)],
        scratch_shapes=[pltpu.VMEM((TQ, HD), F32), pltpu.VMEM((8, HD), F32)],
        compiler_params=_params(dimension_semantics=("arbitrary",)),
    )(dcs, drs, zr, bf_pad)


def _gates(xc, w_ref, b, sigmoid):
    xb = xc.astype(BF16)
    pre = jnp.concatenate(
        [jnp.dot(xb[:, HD * g:HD * (g + 1)], w_ref[g], preferred_element_type=F32) for g in range(NH)], axis=1)
    return sigmoid(pre + b)


def _lru_coeffs(r, lam):
    sp = jnp.maximum(-lam, 0.0) + jnp.log(1.0 + jnp.exp(-jnp.abs(lam)))
    log_a = -LRU_C * r * sp
    a = jnp.exp(log_a)
    y = 2.0 * log_a
    em1 = jnp.where(jnp.abs(y) < 0.01, y * (1.0 + y * (0.5 + y * (1.0 / 6.0))), jnp.exp(y) - 1.0)
    em = -em1
    inv_gam = lax.rsqrt(jnp.maximum(em, 1e-37))
    return sp, a, em * inv_gam, inv_gam


def _conv_taps(ext, t):
    return [_shift_down(ext, CONV_W - 1 - jj, t) for jj in range(CONV_W)]


def _lru_fwd(zr, conv_w8, conv_b, w_r, b_r, w_i, b_i, lam):
    s = zr.shape[0]
    n = s // TS
    xl_col = 1

    def body(xl_ref, halo_ref, cw_ref, cb_ref, wr_ref, br_ref, wi_ref, bi_ref, lam_ref, xc_ref, h_ref, carry):
        i = pl.program_id(0)

        @pl.when(i == 0)
        def _():
            carry[...] = jnp.zeros_like(carry)

        halo = jnp.where(i == 0, 0.0, halo_ref[...])
        taps = _conv_taps(jnp.concatenate([halo, xl_ref[...]], axis=0), TS)
        xc = cb_ref[...] + sum(cw_ref[jj:jj + 1, :] * taps[jj] for jj in range(CONV_W))
        xc_ref[...] = xc
        r = _gates(xc, wr_ref, br_ref[...], _sigmoid_rel)
        ig = _gates(xc, wi_ref, bi_ref[...], _sigmoid)
        _, a, gam, _ = _lru_coeffs(r, lam_ref[...])
        a_cum, h_loc = _scan(a, gam * (ig * xc), False)
        h_ref[...] = h_loc + a_cum * carry[0:1, :]
        carry[0:1, :] = h_ref[TS - 1:TS, :]

    return pl.pallas_call(
        body, name="lru_fwd", grid=(n,),
        in_specs=[
            _row(D, xl_col),
            pl.BlockSpec((8, D), lambda i: (jnp.maximum(i * (TS // 8) - 1, 0), xl_col)),
            _vec(8, D), _vec(1, D),
            pl.BlockSpec((NH, HD, HD), lambda i: (0, 0, 0)), _vec(1, D),
            pl.BlockSpec((NH, HD, HD), lambda i: (0, 0, 0)), _vec(1, D),
            _vec(1, D),
        ],
        out_specs=[_row(D), _row(D)],
        out_shape=[jax.ShapeDtypeStruct((s, D), F32), jax.ShapeDtypeStruct((s, D), F32)],
        scratch_shapes=[pltpu.VMEM((8, D), F32)],
        compiler_params=_params(dimension_semantics=("arbitrary",)),
    )(zr, zr, conv_w8, conv_b, w_r, b_r, w_i, b_i, lam)


def _lru_bwd(zr, xc, h, dh, conv_w8, w_r, b_r, w_i, b_i, lam):
    s = zr.shape[0]
    n = s // TS
    xl_col = 1

    def rev(i):
        return n - 1 - i

    def body(xl_ref, xlh_ref, xc_ref, h_ref, hh_ref, dh_ref, cw_ref, wr_ref, br_ref, wi_ref, bi_ref, lam_ref,
             dxl_ref, gwr_ref, gwi_ref, gbr_ref, gbi_ref, glam_ref, gcb_ref, gcw_ref, l_buf, dxc_buf, carry_g, carry_dxc):
        i = pl.program_id(0)
        first = rev(i) == 0

        @pl.when(i == 0)
        def _():
            carry_g[...] = jnp.zeros_like(carry_g)
            carry_dxc[...] = jnp.zeros_like(carry_dxc)
            for ref in (gwr_ref, gwi_ref, gbr_ref, gbi_ref, glam_ref, gcb_ref, gcw_ref):
                ref[...] = jnp.zeros_like(ref)

        rows = _rows_iota(TS)
        xc = xc_ref[...]
        lam = lam_ref[...]
        r = _gates(xc, wr_ref, br_ref[...], _sigmoid_rel)
        ig = _gates(xc, wi_ref, bi_ref[...], _sigmoid)
        sp, a, gam, inv_gam = _lru_coeffs(r, lam)
        g = dh_ref[...] + jnp.where(rows == TS - 1, carry_g[0:1, :], 0.0)
        b = jnp.where(rows == TS - 1, 0.0, pltpu.roll(a, TS - 1, 0))
        l_buf[...] = _scan(b, g, True)[1]
        lv = l_buf[...]
        carry_g[0:1, :] = l_buf[0:1, :] * a[0:1, :]
        h_prev_row = jnp.where(first, 0.0, hh_ref[7:8, :])
        h_prev = jnp.where(rows == 0, h_prev_row, pltpu.roll(h_ref[...], 1, 0))
        dgam = lv * ig * xc
        dig = lv * gam * xc
        dxc = lv * gam * ig
        dla = lv * h_prev * a - dgam * (a * a) * inv_gam
        dr = dla * (-LRU_C) * sp
        glam_ref[...] += _colsum(dla * r) * (LRU_C * _sigmoid_rel(-lam))
        dpr = dr * r * (1.0 - r)
        dpi = dig * ig * (1.0 - ig)
        gbr_ref[...] += _colsum(dpr)
        gbi_ref[...] += _colsum(dpi)
        xb = xc.astype(BF16)
        dprb = dpr.astype(BF16)
        dpib = dpi.astype(BF16)
        back = []
        for gi in range(NH):
            cs = slice(HD * gi, HD * (gi + 1))
            gwr_ref[gi] += lax.dot_general(xb[:, cs], dprb[:, cs], TN_DIMS, preferred_element_type=F32)
            gwi_ref[gi] += lax.dot_general(xb[:, cs], dpib[:, cs], TN_DIMS, preferred_element_type=F32)
            back.append(lax.dot_general(dprb[:, cs], wr_ref[gi], NT_DIMS, preferred_element_type=F32)
                        + lax.dot_general(dpib[:, cs], wi_ref[gi], NT_DIMS, preferred_element_type=F32))
        dxc = dxc + jnp.concatenate(back, axis=1)
        dxc_buf[...] = dxc
        gcb_ref[...] += _colsum(dxc)
        halo = jnp.where(first, 0.0, xlh_ref[...])
        taps = _conv_taps(jnp.concatenate([halo, xl_ref[...]], axis=0), TS)
        for jj in range(CONV_W):
            gcw_ref[jj:jj + 1, :] += _colsum(dxc * taps[jj])
        ext = jnp.concatenate([dxc, carry_dxc[...]], axis=0)
        dxl = sum(cw_ref[jj:jj + 1, :] * _shift_up(ext, CONV_W - 1 - jj, TS) for jj in range(CONV_W))
        dxl_ref[...] = dxl.astype(BF16)
        carry_dxc[...] = dxc_buf[0:8, :]

    rowr = lambda c, col=0: pl.BlockSpec((TS, c), lambda i: (rev(i), col))
    halo = lambda col: pl.BlockSpec((8, D), lambda i: (jnp.maximum(rev(i) * (TS // 8) - 1, 0), col))
    gate_w = pl.BlockSpec((NH, HD, HD), lambda i: (0, 0, 0))
    return pl.pallas_call(
        body, name="lru_bwd", grid=(n,),
        in_specs=[rowr(D, xl_col), halo(xl_col), rowr(D), rowr(D), halo(0), rowr(D),
                  _vec(8, D), gate_w, _vec(1, D), gate_w, _vec(1, D), _vec(1, D)],
        out_specs=[rowr(D), gate_w, gate_w, _vec(1, D), _vec(1, D), _vec(1, D), _vec(1, D), _vec(8, D)],
        out_shape=[
            jax.ShapeDtypeStruct((s, D), BF16),
            jax.ShapeDtypeStruct((NH, HD, HD), F32), jax.ShapeDtypeStruct((NH, HD, HD), F32),
            jax.ShapeDtypeStruct((1, D), F32), jax.ShapeDtypeStruct((1, D), F32), jax.ShapeDtypeStruct((1, D), F32),
            jax.ShapeDtypeStruct((1, D), F32), jax.ShapeDtypeStruct((8, D), F32),
        ],
        scratch_shapes=[pltpu.VMEM((TS, D), F32), pltpu.VMEM((TS, D), F32), pltpu.VMEM((8, D), F32), pltpu.VMEM((8, D), F32)],
        compiler_params=_params(dimension_semantics=("arbitrary",)),
    )(zr, zr, xc, h, h, dh, conv_w8, w_r, b_r, w_i, b_i, lam)


def _silu_parts(g):
    sg = _sigmoid(g)
    return g * sg, sg * (1.0 + g * (1.0 - sg))


def _branch_out(o, h, zr, gain_a, gain_l):
    s = o.shape[0]

    def body(o_ref, ga_ref, h_ref, gl_ref, ka_ref, kl_ref, y_ref):
        ohat, _ = _rms_fwd(o_ref[...], None)
        y_ref[:, 0:D] = (ohat * ka_ref[...] * _silu_parts(ga_ref[...])[0]).astype(BF16)
        hhat, _ = _rms_fwd(h_ref[...], None)
        y_ref[:, D:2 * D] = (hhat * kl_ref[...] * _silu_parts(gl_ref[...])[0]).astype(BF16)

    return pl.pallas_call(
        body, name="branch_out", grid=(s // TS,),
        in_specs=[_row(D), _row(D, 0), _row(D), _row(D, 2), _vec(1, D), _vec(1, D)],
        out_specs=_row(2 * D),
        out_shape=jax.ShapeDtypeStruct((s, 2 * D), BF16),
        compiler_params=_params(dimension_semantics=("parallel",)),
    )(o, zr, h, zr, gain_a, gain_l)


def _branch_out_bwd(o, h, zr, dmix, w_out, gain_a, gain_l):
    s = o.shape[0]

    def body(o_ref, ga_ref, h_ref, gl_ref, dm_ref, w_ref, ka_ref, kl_ref,
             do_ref, dl_ref, dga_ref, dh_ref, dgl_ref, gka_ref, gkl_ref):
        @pl.when(pl.program_id(0) == 0)
        def _():
            gka_ref[...] = jnp.zeros_like(gka_ref)
            gkl_ref[...] = jnp.zeros_like(gkl_ref)

        dycat = lax.dot_general(dm_ref[...], w_ref[...], NT_DIMS, preferred_element_type=F32)

        def one(v, g, dy, gain):
            vhat, rstd = _rms_fwd(v, None)
            sg, dsg = _silu_parts(g)
            dn = dy * sg
            dg = dy * (vhat * gain) * dsg
            return _rms_bwd(vhat, rstd, dn * gain), dg, _colsum(dn * vhat)

        o = o_ref[...]
        dout, dga, gka = one(o, ga_ref[...], dycat[:, :D], ka_ref[...])
        do_ref[...] = dout.astype(BF16)
        dga_ref[...] = dga.astype(BF16)
        gka_ref[...] += gka
        prod = dout * o
        for hh in range(NH):
            dl_ref[hh] = jnp.broadcast_to(jnp.sum(prod[:, HD * hh:HD * (hh + 1)], axis=1, keepdims=True), (TS, HD))
        dh, dgl, gkl = one(h_ref[...], gl_ref[...], dycat[:, D:], kl_ref[...])
        dh_ref[...] = dh
        dgl_ref[...] = dgl.astype(BF16)
        gkl_ref[...] += gkl

    return pl.pallas_call(
        body, name="branch_out_bwd", grid=(s // TS,),
        in_specs=[_row(D), _row(D, 0), _row(D), _row(D, 2), _row(D), _vec(2 * D, D), _vec(1, D), _vec(1, D)],
        out_specs=[_row(D), pl.BlockSpec((NH, TS, HD), lambda i: (0, i, 0)), _row(D), _row(D), _row(D), _vec(1, D), _vec(1, D)],
        out_shape=[
            jax.ShapeDtypeStruct((s, D), BF16), jax.ShapeDtypeStruct((NH, s, HD), F32), jax.ShapeDtypeStruct((s, D), BF16),
            jax.ShapeDtypeStruct((s, D), F32), jax.ShapeDtypeStruct((s, D), BF16),
            jax.ShapeDtypeStruct((1, D), F32), jax.ShapeDtypeStruct((1, D), F32),
        ],
        compiler_params=_params(dimension_semantics=("arbitrary",)),
    )(o, zr, h, zr, dmix, w_out, gain_a, gain_l)


def _residual(x, ycat, w_out, post_gain):
    s = x.shape[0]

    def body(x_ref, y_ref, w_ref, g_ref, m_ref, h_ref, hb_ref):
        mix = jnp.dot(y_ref[...], w_ref[...], preferred_element_type=F32)
        m_ref[...] = mix
        mhat, _ = _rms_fwd(mix, None)
        h1 = x_ref[...] + mhat * g_ref[...]
        h_ref[...] = h1
        hb_ref[...] = h1.astype(BF16)

    return pl.pallas_call(
        body, name="residual", grid=(s // TS,),
        in_specs=[_row(D), _row(2 * D), _vec(2 * D, D), _vec(1, D)], out_specs=[_row(D), _row(D), _row(D)],
        out_shape=[jax.ShapeDtypeStruct((s, D), F32), jax.ShapeDtypeStruct((s, D), F32), jax.ShapeDtypeStruct((s, D), BF16)],
        compiler_params=_params(dimension_semantics=("parallel",)),
    )(x, ycat, w_out, post_gain)


def _head(h1, p, tgt, mix, w_gate, w_ple, ple_gain, b_gate, post_gain):
    s = h1.shape[0]

    def body(h_ref, p_ref, t_ref, m_ref, wg_ref, wp_ref, kg_ref, b_ref, pg_ref,
             loss_ref, dgp_ref, dpe_ref, dh_ref, dm_ref, gk_ref, gb_ref, gg_ref):
        @pl.when(pl.program_id(0) == 0)
        def _():
            for ref in (loss_ref, gk_ref, gb_ref, gg_ref):
                ref[...] = jnp.zeros_like(ref)

        h1 = h_ref[...]
        pe = jnp.dot(p_ref[...].astype(BF16), wp_ref[...], preferred_element_type=F32)
        gp = jnp.dot(h1.astype(BF16), wg_ref[...], preferred_element_type=F32)
        ehat, rstd = _rms_fwd(pe, None)
        e = ehat * kg_ref[...]
        gate = _sigmoid(gp + b_ref[...])
        diff = (h1 + gate * e) - t_ref[...]
        per_row = jnp.mean(diff * diff, axis=-1, keepdims=True)
        loss_ref[...] += 0.5 * jnp.sum(per_row, axis=0, keepdims=True)
        dy = diff * (1.0 / D)
        dgp = dy * e * gate * (1.0 - gate)
        dgpb = dgp.astype(BF16)
        dgp_ref[...] = dgpb
        gb_ref[...] += _colsum(dgp)
        de = dy * gate
        gk_ref[...] += _colsum(de * ehat)
        dpe_ref[...] = _rms_bwd(ehat, rstd, de * kg_ref[...]).astype(BF16)
        dh1 = dy + lax.dot_general(dgpb, wg_ref[...], NT_DIMS, preferred_element_type=F32)
        dh_ref[...] = dh1
        mhat, rstd_m = _rms_fwd(m_ref[...], None)
        gg_ref[...] += _colsum(dh1 * mhat)
        dm_ref[...] = _rms_bwd(mhat, rstd_m, dh1 * pg_ref[...]).astype(BF16)

    return pl.pallas_call(
        body, name="head", grid=(s // TS,),
        in_specs=[_row(D), _row(D_PLE), _row(D), _row(D), _vec(D, D), _vec(D_PLE, D), _vec(1, D), _vec(1, D), _vec(1, D)],
        out_specs=[_vec(1, 1), _row(D), _row(D), _row(D), _row(D), _vec(1, D), _vec(1, D), _vec(1, D)],
        out_shape=[
            jax.ShapeDtypeStruct((1, 1), F32), jax.ShapeDtypeStruct((s, D), BF16), jax.ShapeDtypeStruct((s, D), BF16),
            jax.ShapeDtypeStruct((s, D), F32), jax.ShapeDtypeStruct((s, D), BF16),
            jax.ShapeDtypeStruct((1, D), F32), jax.ShapeDtypeStruct((1, D), F32), jax.ShapeDtypeStruct((1, D), F32),
        ],
        compiler_params=_params(dimension_semantics=("arbitrary",)),
    )(h1, p, tgt, mix, w_gate, w_ple, ple_gain, b_gate, post_gain)


def _prenorm_bwd(x, dxn_a, dxn_b, dh1, pre_gain):
    s = x.shape[0]

    def body(x_ref, da_ref, db_ref, dh_ref, g_ref, dx_ref, gg_ref):
        @pl.when(pl.program_id(0) == 0)
        def _():
            gg_ref[...] = jnp.zeros_like(gg_ref)

        xhat, rstd = _rms_fwd(x_ref[...], None)
        dxn = da_ref[...] + db_ref[...]
        gg_ref[...] += _colsum(dxn * xhat)
        dx_ref[...] = dh_ref[...] + _rms_bwd(xhat, rstd, dxn * g_ref[...])

    return pl.pallas_call(
        body, name="prenorm_bwd", grid=(s // TS,),
        in_specs=[_row(D), _row(D), _row(D), _row(D), _vec(1, D)], out_specs=[_row(D), _vec(1, D)],
        out_shape=[jax.ShapeDtypeStruct((s, D), F32), jax.ShapeDtypeStruct((1, D), F32)],
        compiler_params=_params(dimension_semantics=("arbitrary",)),
    )(x, dxn_a, dxn_b, dh1, pre_gain)


def _adamw(name, parts, w, m, v, own=None, me=None):
    r, c = w.shape
    if r % 8 == 0:
        tr = _pick(r, (256, 128, 16, 8))
        grid = (r // tr,)
        blk = pl.BlockSpec((tr, c), lambda i: (i, 0))
        parts_blk = pl.BlockSpec((N_DEV, tr, c), lambda i: (0, i, 0))
    else:
        tc = _pick(c, (256, 128))
        grid = (c // tc,)
        blk = pl.BlockSpec((r, tc), lambda i: (0, i))
        parts_blk = pl.BlockSpec((N_DEV, r, tc), lambda i: (0, 0, i))

    def body(*refs):
        p_ref, w_ref, m_ref, v_ref = refs[:4]
        g_ref, d_ref, nm_ref, nv_ref = refs[-4:]
        if own is None:
            g = p_ref[0].astype(F32)
            for j in range(1, N_DEV):
                g = g + p_ref[j].astype(F32)
            g_ref[...] = g
        else:
            own_ref, me_ref = refs[4:6]
            g_ref[...] = jnp.zeros_like(g_ref)
            for j in range(N_DEV):
                @pl.when(me_ref[0] == j)
                def _():
                    g_ref[...] += own_ref[...].astype(F32)

                @pl.when(me_ref[0] != j)
                def _():
                    g_ref[...] += p_ref[j].astype(F32)
            g = g_ref[...]
        nm = ADAM_B1 * m_ref[...] + (1.0 - ADAM_B1) * g
        nv = ADAM_B2 * v_ref[...] + (1.0 - ADAM_B2) * (g * g)
        nm_ref[...] = nm
        nv_ref[...] = nv
        m_hat = nm / (1.0 - ADAM_B1 ** ADAM_STEP)
        v_hat = nv / (1.0 - ADAM_B2 ** ADAM_STEP)
        d_ref[...] = -ADAM_LR * (m_hat / (jnp.sqrt(v_hat) + ADAM_EPS) + ADAM_WD * w_ref[...])

    in_specs, args = [parts_blk, blk, blk, blk], [parts, w, m, v]
    if own is not None:
        in_specs += [blk, pl.BlockSpec(memory_space=pltpu.SMEM)]
        args += [own, me]
    return pl.pallas_call(
        body, name=name, grid=grid,
        in_specs=in_specs,
        out_specs=[blk] * 4,
        out_shape=[jax.ShapeDtypeStruct((r, c), F32)] * 4,
        compiler_params=_params(dimension_semantics=("parallel",)),
    )(*args)


def _spread8(v):
    r = v.shape[0]
    return jnp.pad(jnp.pad(v[:, :, None], ((0, 0), (0, 0), (0, 7))).reshape(r, 8 * NH), ((0, 0), (0, HD - 8 * NH)))


def _gather8(v):
    return v[:, :8 * NH].reshape(v.shape[0], NH, 8)[:, :, 0]


def _cols_to_shards(g):
    r, c8 = g.shape
    return g.reshape(r, N_DEV, c8 // N_DEV).transpose(1, 0, 2)


def _shards_to_cols(g):
    n, r, c = g.shape
    return g.transpose(1, 0, 2).reshape(r, n * c)


def kernel(x, p, w_in, b_f, pre_gain, post_gain, conv_w, conv_b, w_rgate, b_rgate, w_igate, b_igate, lru_lambda, attn_out_gain, lru_out_gain, w_out, w_ple, ple_gain, w_ple_gate, b_ple_gate, loss_target, m_w_in, m_b_f, m_pre_gain, m_post_gain, m_conv_w, m_conv_b, m_w_rgate, m_b_rgate, m_w_igate, m_b_igate, m_lru_lambda, m_attn_out_gain, m_lru_out_gain, m_w_out, m_w_ple, m_ple_gain, m_w_ple_gate, m_b_ple_gate, v_w_in, v_b_f, v_pre_gain, v_post_gain, v_conv_w, v_conv_b, v_w_rgate, v_b_rgate, v_w_igate, v_b_igate, v_lru_lambda, v_attn_out_gain, v_lru_out_gain, v_w_out, v_w_ple, v_ple_gain, v_w_ple_gate, v_b_ple_gate):
    me = 4 * lax.axis_index("x") + 2 * lax.axis_index("y") + lax.axis_index("c")
    x2, p2, tgt = x[0], p[0, 0], loss_target[0]

    conv_w_shard8 = jnp.pad(conv_w[0], ((0, 8 - CONV_W), (0, 0)))
    wt, m_wt, v_wt = w_in[0].T, m_w_in[0].T, v_w_in[0].T
    g_wint, g_conv = _gather_two_level("gather_w_in", [wt.astype(BF16), conv_w_shard8])
    win_t = g_wint.reshape(D_IN, D)
    rest_state, rest_token = _exchange_start(
        "gather_rest_start", [w_out[0].astype(BF16), w_ple[0].astype(BF16), w_ple_gate[0].astype(BF16)], ["bcast"] * 3,
        after=g_conv)
    w_qkv_t = win_t[:D_QKV]
    w_rest_t = jnp.concatenate([win_t[D_QKV + NH:], _spread8(win_t[D_QKV:D_QKV + NH].T).T], axis=0)
    conv_w8 = _shards_to_cols(g_conv)
    bf_pad = _spread8(b_f)
    w_r, w_i = w_rgate[0].astype(BF16), w_igate[0].astype(BF16)

    xn, zq = _prenorm_proj(x2, pre_gain, w_qkv_t, rest_token)
    zr = _mm("proj_rest", xn, w_rest_t, "nt", F32)
    kx = _forget_fwd(zr, bf_pad)
    o, ax = _attn_fwd(zq, kx)
    xc, h = _lru_fwd(zr, conv_w8, conv_b, w_r, b_rgate, w_i, b_igate, lru_lambda)
    ycat = _branch_out(o, h, zr, attn_out_gain, lru_out_gain)
    g_wout, g_wple, g_wpg = _exchange_wait("gather_rest_wait", rest_state, ycat)
    wout_full = g_wout.reshape(2 * D, D)
    wple_full = _shards_to_cols(g_wple)
    wpg_full = g_wpg.reshape(D, D)
    mix, h1, h1b = _residual(x2, ycat, wout_full, post_gain)

    loss_part, dgp, dpe, dh1, dmix, g_ple_gain, g_b_gate, g_post_gain = _head(
        h1, p2, tgt, mix, wpg_full, wple_full, ple_gain, b_ple_gate, post_gain)
    loss = lax.psum(loss_part[0, 0], ("x", "y", "c"))
    gw_pg = _mm("bwd_gate_w", h1b, dgp, "tn", BF16)
    gw_ple = _mm("bwd_ple_w", p2, dpe, "tn", BF16)
    gw_out = _mm("bwd_out_w", ycat, dmix, "tn", BF16)
    do, delta, dga, dh, dgl, g_aog, g_log = _branch_out_bwd(o, h, zr, dmix, wout_full, attn_out_gain, lru_out_gain)
    dxl, g_wr, g_wi, g_br, g_bi, g_lam, g_cb, g_cw8 = _lru_bwd(
        zr, xc, h, dh, conv_w8, w_r, b_rgate, w_i, b_igate, lru_lambda)
    gates = jnp.concatenate([g_wr.reshape(D, HD), g_wi.reshape(D, HD)], axis=0).astype(BF16)
    outw_state, outw_token = _exchange_start(
        "exchange_outw_start",
        [gw_out.reshape(N_DEV, 2 * D // N_DEV, D), _cols_to_shards(gw_ple), gw_pg.reshape(N_DEV, D // N_DEV, D), gates],
        ["scatter"] * 3 + ["bcast"])
    dq, dk, dv, dcs, drs = _attn_bwd(zq, do, ax, delta, kx, outw_token)
    dfl, g_bf_pad = _forget_bwd(dcs, drs, zr, bf_pad)
    gw_pieces = [_mm("bwd_w_" + nm, dz, xn, "tn", BF16) for nm, dz in
                 (("q", dq), ("k", dk), ("v", dv), ("fl", dfl), ("ga", dga), ("xl", dxl), ("gl", dgl))]
    gw_pieces[3] = _gather8(gw_pieces[3].T).T
    gw_in_t = jnp.concatenate(gw_pieces, axis=0)
    inw_state, inw_token = _exchange_start(
        "exchange_inw_start", [gw_in_t.reshape(N_DEV, D_IN_SHARD, D)], ["scatter"])
    dxn_a = _mm_cat("bwd_qkv_x", [dq, dk, dv], w_qkv_t, F32, after=inw_token)
    dxn_b = _mm_cat("bwd_rest_x", [dga, dxl, dgl, dfl], w_rest_t, F32, after=inw_token)
    grad_x, g_pre_gain = _prenorm_bwd(x2, dxn_a, dxn_b, dh1, pre_gain)

    upd = {}
    me1 = me.reshape(1).astype(jnp.int32)
    r_wout, r_wple, r_wpg, r_gates = _exchange_wait("exchange_outw_wait", outw_state, grad_x, fill_own=False)
    upd["w_out"] = _adamw("adamw_w_out", r_wout[0], w_out[0], m_w_out[0], v_w_out[0], r_wout[1], me1)
    upd["w_ple"] = _adamw("adamw_w_ple", r_wple[0], w_ple[0], m_w_ple[0], v_w_ple[0], r_wple[1], me1)
    upd["w_ple_gate"] = _adamw("adamw_w_ple_gate", r_wpg[0], w_ple_gate[0], m_w_ple_gate[0], v_w_ple_gate[0], r_wpg[1], me1)
    gates_of = lambda a, b: jnp.concatenate([a[0].reshape(D, HD), b[0].reshape(D, HD)], axis=0)
    g_gates = _adamw("adamw_gates", r_gates[0], gates_of(w_rgate, w_igate), gates_of(m_w_rgate, m_w_igate),
                     gates_of(v_w_rgate, v_w_igate), r_gates[1], me1)
    upd["w_rgate"] = [a[:D].reshape(1, NH, HD, HD) for a in g_gates]
    upd["w_igate"] = [a[D:].reshape(1, NH, HD, HD) for a in g_gates]
    behind = upd["w_out"][0][0:1] + upd["w_ple_gate"][0][0:1] + jnp.pad(g_gates[0][0:1], ((0, 0), (0, D - HD)))
    small = jnp.concatenate(
        [jnp.pad(_gather8(g_bf_pad), ((0, 0), (0, D - NH))), g_pre_gain, g_post_gain, g_cb, g_br, g_bi, g_lam, g_aog, g_log,
         g_ple_gain, g_b_gate, g_cw8[:CONV_W], behind], axis=0)
    (r_small,) = _exchange("exchange_small", [small], ["bcast"])
    vec_names = ["b_f", "pre_gain", "post_gain", "conv_b", "b_rgate", "b_igate", "lru_lambda", "attn_out_gain",
                 "lru_out_gain", "ple_gain", "b_ple_gate"]
    vec_w = dict(b_f=(b_f, m_b_f, v_b_f), pre_gain=(pre_gain, m_pre_gain, v_pre_gain),
                 post_gain=(post_gain, m_post_gain, v_post_gain), conv_b=(conv_b, m_conv_b, v_conv_b),
                 b_rgate=(b_rgate, m_b_rgate, v_b_rgate), b_igate=(b_igate, m_b_igate, v_b_igate),
                 lru_lambda=(lru_lambda, m_lru_lambda, v_lru_lambda),
                 attn_out_gain=(attn_out_gain, m_attn_out_gain, v_attn_out_gain),
                 lru_out_gain=(lru_out_gain, m_lru_out_gain, v_lru_out_gain), ple_gain=(ple_gain, m_ple_gain, v_ple_gain),
                 b_ple_gate=(b_ple_gate, m_b_ple_gate, v_b_ple_gate))
    conv_mine = lambda a: lax.dynamic_slice_in_dim(a, me * HD, HD, axis=1)

    def small_rows(k):
        rows = [jnp.pad(vec_w[nm][k], ((0, 0), (0, D - vec_w[nm][k].shape[1]))) for nm in vec_names]
        cw = (conv_w, m_conv_w, v_conv_w)[k][0]
        full = lax.dynamic_update_slice_in_dim(jnp.ones((CONV_W, D), F32), cw, me * HD, axis=1)
        return jnp.concatenate(rows + [full, jnp.ones((1, D), F32)], axis=0)

    g_small = _adamw("adamw_small", r_small, small_rows(0), small_rows(1), small_rows(2))
    for idx, nm in enumerate(vec_names):
        width = vec_w[nm][0].shape[1]
        upd[nm] = [a[idx:idx + 1, :width] for a in g_small]
    base = len(vec_names)
    upd["conv_w"] = [conv_mine(a[base:base + CONV_W])[None] for a in g_small]
    (r_win,) = _exchange_wait("exchange_inw_wait", inw_state, g_small[0], fill_own=False)
    upd["w_in"] = [a.T for a in _adamw("adamw_w_in", r_win[0], wt, m_wt, v_wt, r_win[1], me1)]
    for nm in ("w_in", "w_out", "w_ple", "w_ple_gate"):
        upd[nm] = [a[None] for a in upd[nm]]

    order = ["w_in", "b_f", "pre_gain", "post_gain", "conv_w", "conv_b", "w_rgate", "b_rgate", "w_igate", "b_igate",
             "lru_lambda", "attn_out_gain", "lru_out_gain", "w_out", "w_ple", "ple_gain", "w_ple_gate", "b_ple_gate"]
    outs = [loss, grad_x[None]]
    for k in range(4):
        outs += [upd[nm][k] for nm in order]
    return tuple(outs)
```

```python
import functools

import jax
import jax.numpy as jnp
from jax import lax
from jax.experimental import pallas as pl
from jax.experimental.pallas import tpu as pltpu

F32 = jnp.float32
BF16 = jnp.bfloat16

N_DEV = 8
D = 1024
HD = 128
NH = 8
D_IN = 6152
D_IN_SHARD = D_IN // N_DEV
D_QKV = 3 * D
D_REST = 3 * D + HD
FL_COL = 3 * D
D_PLE = 256
CONV_W = 4
LRU_C = 8.0
RMS_EPS = 1e-6
SCALE = HD ** -0.5
EXP2_SCALE = SCALE * 1.4426950408889634
NEG = -1e30

ADAM_LR = 0.001
ADAM_B1 = 0.9
ADAM_B2 = 0.999
ADAM_EPS = 1e-08
ADAM_WD = 0.01
ADAM_STEP = 10

TS = 256
TQ = 1024
ROW_PARTS = 2
VMEM_LIMIT = 48 * 1024 * 1024

NT_DIMS = (((1,), (1,)), ((), ()))
TN_DIMS = (((0,), (0,)), ((), ()))


def _params(**kw):
    return pltpu.CompilerParams(vmem_limit_bytes=VMEM_LIMIT, **kw)


def _sigmoid(v):
    return 0.5 * jnp.tanh(0.5 * v) + 0.5


def _sigmoid_rel(v):
    return 1.0 / (1.0 + jnp.exp(-v))


def _rms_fwd(v):
    rstd = lax.rsqrt(jnp.mean(v * v, axis=-1, keepdims=True) + RMS_EPS)
    return v * rstd, rstd


def _rms_bwd(vhat, rstd, dvhat):
    return rstd * (dvhat - vhat * jnp.mean(dvhat * vhat, axis=-1, keepdims=True))


def _colsum(v):
    return jnp.sum(v, axis=0, keepdims=True)


def _rows_iota(t):
    return lax.broadcasted_iota(jnp.int32, (t, 1), 0)


def _scan(a, u, reverse):
    t, c = a.shape
    rows = _rows_iota(t)
    d = 1
    while d < t:
        if d < 8:
            valid = rows < t - d if reverse else rows >= d
            shift = t - d if reverse else d
            u = jnp.where(valid, u + a * pltpu.roll(u, shift, 0), u)
            a = jnp.where(valid, a * pltpu.roll(a, shift, 0), a)
        else:
            zeros, ones = jnp.zeros((d, c), F32), jnp.ones((d, c), F32)
            if reverse:
                u_far, a_far = jnp.concatenate([u[d:], zeros], axis=0), jnp.concatenate([a[d:], ones], axis=0)
            else:
                u_far, a_far = jnp.concatenate([zeros, u[:t - d]], axis=0), jnp.concatenate([ones, a[:t - d]], axis=0)
            u = u + a * u_far
            a = a * a_far
        d *= 2
    return a, u


def _cumsum_fwd(v):
    t = v.shape[0]
    rows = _rows_iota(t)
    d = 1
    while d < t:
        v = jnp.where(rows >= d, v + pltpu.roll(v, d, 0), v)
        d *= 2
    return v


def _cumsum_bwd(v):
    t = v.shape[0]
    rows = _rows_iota(t)
    d = 1
    while d < t:
        v = jnp.where(rows < t - d, v + pltpu.roll(v, t - d, 0), v)
        d *= 2
    return v


def _bias_lanes(v, at, ones_at):
    lane = lax.broadcasted_iota(jnp.int32, v.shape, 1)
    hi = v.astype(BF16).astype(F32)
    mid = (v - hi).astype(BF16).astype(F32)
    lo = ((v - hi) - mid).astype(BF16).astype(F32)
    out = jnp.where((lane >= ones_at) & (lane < ones_at + 3), 1.0, 0.0)
    for k, piece in enumerate((hi, mid, lo)):
        out = jnp.where(lane == at + k, piece, out)
    return out.astype(BF16)


def _shift_down(ext, k, t):
    return pltpu.roll(ext, k, 0)[8:, :] if k else ext[8:, :]


def _shift_up(ext, k, t):
    return pltpu.roll(ext, t + 8 - k, 0)[:t, :] if k else ext[:t, :]


def _exchange(name, arrs, kinds):
    n = len(arrs)
    out_shape = []
    for a, kind in zip(arrs, kinds):
        shp = a.shape if kind == "scatter" else (N_DEV,) + a.shape
        out_shape.append(jax.ShapeDtypeStruct(shp, a.dtype))

    def body(*refs):
        ins, outs = refs[:n], refs[n:2 * n]
        send_sems, recv_sems, local_sems = refs[2 * n:]
        x, y, c = lax.axis_index("x"), lax.axis_index("y"), lax.axis_index("c")
        me = 4 * x + 2 * y + c
        copies = []
        for i in range(n):
            scatter = kinds[i] == "scatter"
            mine = pltpu.make_async_copy(ins[i].at[me] if scatter else ins[i], outs[i].at[me], local_sems.at[i])
            mine.start()
            copies.append(mine)
            for m in range(1, N_DEV):
                px = 1 - x if m & 4 else x
                py = 1 - y if m & 2 else y
                pc = 1 - c if m & 1 else c
                peer = 4 * px + 2 * py + pc
                cp = pltpu.make_async_remote_copy(
                    src_ref=ins[i].at[peer] if scatter else ins[i],
                    dst_ref=outs[i].at[me],
                    send_sem=send_sems.at[i, m - 1],
                    recv_sem=recv_sems.at[i, m - 1],
                    device_id=(px, py, pc),
                    device_id_type=pl.DeviceIdType.MESH,
                )
                cp.start()
                copies.append(cp)
        for cp in copies:
            cp.wait()

    any_spec = pl.BlockSpec(memory_space=pl.ANY)
    return pl.pallas_call(
        body,
        name=name,
        out_shape=out_shape,
        in_specs=[any_spec] * n,
        out_specs=[any_spec] * n,
        scratch_shapes=[
            pltpu.SemaphoreType.DMA((n, N_DEV - 1)),
            pltpu.SemaphoreType.DMA((n, N_DEV - 1)),
            pltpu.SemaphoreType.DMA((n,)),
        ],
        compiler_params=pltpu.CompilerParams(has_side_effects=True),
    )(*arrs)


def _gather_two_level(name, arrs, pieces=1):
    n = len(arrs)
    items = []
    for i, a in enumerate(arrs):
        rows = a.shape[0]
        if pieces > 1 and rows >= 512:
            step = -(-rows // (16 * pieces)) * 16
            items += [(i, r0, min(step, rows - r0)) for r0 in range(0, rows, step)]
        else:
            items.append((i, 0, rows))
    n_items = len(items)

    def body(*refs):
        ins, outs = refs[:n], refs[n:2 * n]
        send_sems, recv_sems, local_sems = refs[2 * n:]
        x, y, c = lax.axis_index("x"), lax.axis_index("y"), lax.axis_index("c")
        me, sibling = (x, y, c), (x, y, 1 - c)
        chips = [(1 - x, y), (x, 1 - y), (1 - x, 1 - y)]

        def rows_of(ref, t):
            i, r0, rn = items[t]
            return ref if rn == arrs[i].shape[0] else ref.at[pl.ds(r0, rn)]

        def slot(t, dev):
            return rows_of(outs[items[t][0]].at[4 * dev[0] + 2 * dev[1] + dev[2]], t)

        def copy(t, k, block, to, from_input=False):
            return pltpu.make_async_remote_copy(
                src_ref=rows_of(ins[items[t][0]], t) if from_input else slot(t, block), dst_ref=slot(t, block),
                send_sem=send_sems.at[t, k], recv_sem=recv_sems.at[t, k],
                device_id=to, device_id_type=pl.DeviceIdType.MESH)

        own, sent = [], []
        for t in range(n_items):
            mine = pltpu.make_async_copy(rows_of(ins[items[t][0]], t), slot(t, me), local_sems.at[t])
            mine.start()
            own.append(mine)
            first = [copy(t, 1 + j, me, (*chip, c), from_input=True) for j, chip in enumerate(chips)]
            first.append(copy(t, 0, me, sibling, from_input=True))
            for cp in first:
                cp.start()
            sent += first
        for t in range(n_items):
            for j, chip in enumerate(chips):
                copy(t, 1 + j, (*chip, c), me).wait_recv()
                fwd = copy(t, 4 + j, (*chip, c), sibling)
                fwd.start()
                sent.append(fwd)
        for t in range(n_items):
            copy(t, 0, sibling, me).wait_recv()
            for j, chip in enumerate(chips):
                copy(t, 4 + j, (*chip, 1 - c), me).wait_recv()
        for cp in sent:
            cp.wait_send()
        for cp in own:
            cp.wait()

    any_spec = pl.BlockSpec(memory_space=pl.ANY)
    return pl.pallas_call(
        body, name=name,
        out_shape=[jax.ShapeDtypeStruct((N_DEV,) + a.shape, a.dtype) for a in arrs],
        in_specs=[any_spec] * n, out_specs=[any_spec] * n,
        scratch_shapes=[pltpu.SemaphoreType.DMA((n_items, 7)), pltpu.SemaphoreType.DMA((n_items, 7)),
                        pltpu.SemaphoreType.DMA((n_items,))],
        compiler_params=pltpu.CompilerParams(has_side_effects=True),
    )(*arrs)


def _peers(x, y, c):
    out = []
    for m in range(1, N_DEV):
        px = 1 - x if m & 4 else x
        py = 1 - y if m & 2 else y
        pc = 1 - c if m & 1 else c
        out.append((m, (px, py, pc), 4 * px + 2 * py + pc))
    return out


def _split_copies(kinds, src_refs, land_refs, send_sems, recv_sems):
    x, y, c = lax.axis_index("x"), lax.axis_index("y"), lax.axis_index("c")
    me = 4 * x + 2 * y + c
    copies = []
    for i, kind in enumerate(kinds):
        for m, peer, pidx in _peers(x, y, c):
            copies.append(pltpu.make_async_remote_copy(
                src_ref=src_refs[i].at[pidx] if kind == "scatter" else src_refs[i],
                dst_ref=land_refs[i].at[me],
                send_sem=send_sems.at[i * (N_DEV - 1) + m - 1],
                recv_sem=recv_sems.at[i * (N_DEV - 1) + m - 1],
                device_id=peer,
                device_id_type=pl.DeviceIdType.MESH,
            ))
    return copies


_HBM_SPEC = pl.BlockSpec(memory_space=pltpu.HBM)
_SEM_SPEC = pl.BlockSpec(memory_space=pltpu.SEMAPHORE)
_DATAFLOW = pltpu.SideEffectType.DATAFLOW_SIDE_EFFECTING


def _exchange_start(name, arrs, kinds, after=None):
    n = len(arrs)
    extra = [] if after is None else [after]
    lands = []
    for a, kind in zip(arrs, kinds):
        shp = a.shape if kind == "scatter" else (N_DEV,) + a.shape
        lands.append(lax.empty(shp, a.dtype))

    def body(*refs):
        src_refs, land_refs = refs[:n], refs[n:2 * n]
        send_sems, recv_sems = refs[2 * n + len(extra):2 * n + len(extra) + 2]
        token = refs[-1]
        for cp in _split_copies(kinds, src_refs, land_refs, send_sems, recv_sems):
            cp.start()
        token[...] = jnp.zeros_like(token)

    n_sem = n * (N_DEV - 1)
    hbm = lambda a: pltpu.HBM(a.shape, a.dtype)
    res = pl.pallas_call(
        body, name=name,
        out_shape=(pltpu.SemaphoreType.DMA((n_sem,)), pltpu.SemaphoreType.DMA((n_sem,)),
                   *[hbm(a) for a in arrs], *[hbm(a) for a in lands], jax.ShapeDtypeStruct((8, HD), F32)),
        in_specs=[_HBM_SPEC] * (2 * n) + [pl.BlockSpec(memory_space=pl.ANY)] * len(extra),
        out_specs=(_SEM_SPEC, _SEM_SPEC, *[_HBM_SPEC] * (2 * n), pl.BlockSpec(memory_space=pltpu.VMEM)),
        input_output_aliases={i: 2 + i for i in range(2 * n)},
        compiler_params=pltpu.CompilerParams(has_side_effects=_DATAFLOW),
    )(*[pltpu.with_memory_space_constraint(a, pltpu.HBM) for a in arrs],
      *[pltpu.with_memory_space_constraint(a, pltpu.HBM) for a in lands], *extra)
    return (kinds, res[0], res[1], res[2:2 + n], res[2 + n:2 + 2 * n]), res[-1]


def _exchange_wait(name, state, after, fill_own=True):
    kinds, send_sems, recv_sems, srcs, lands = state
    n = len(srcs)

    def body(*refs):
        src_refs, land_refs = refs[:n], refs[n:2 * n]
        send_sems_ref, recv_sems_ref = refs[2 * n:2 * n + 2]
        for cp in _split_copies(kinds, src_refs, land_refs, send_sems_ref, recv_sems_ref):
            cp.wait_send()
            cp.wait_recv()

    res = pl.pallas_call(
        body, name=name,
        out_shape=tuple(pltpu.HBM(a.shape, a.dtype) for a in (*srcs, *lands)),
        in_specs=[_HBM_SPEC] * (2 * n) + [_SEM_SPEC, _SEM_SPEC, pl.BlockSpec(memory_space=pl.ANY)],
        out_specs=tuple([_HBM_SPEC] * (2 * n)),
        input_output_aliases={i: i for i in range(2 * n)},
        compiler_params=pltpu.CompilerParams(has_side_effects=_DATAFLOW),
    )(*srcs, *lands, send_sems, recv_sems, after)
    me = 4 * lax.axis_index("x") + 2 * lax.axis_index("y") + lax.axis_index("c")
    outs = []
    for kind, src, land in zip(kinds, res[:n], res[n:]):
        own = lax.dynamic_index_in_dim(src, me, 0, keepdims=False) if kind == "scatter" else src
        outs.append(lax.dynamic_update_index_in_dim(land, own, me, 0) if fill_own else (land, own))
    return outs


def _pick(n, cands):
    for t in cands:
        if n % t == 0:
            return t
    raise ValueError(f"no tile for {n}")


def _mm(name, a, b, mode, out_dtype, after=None):
    if mode == "nn":
        (m, k), (k2, n) = a.shape, b.shape
    elif mode == "nt":
        (m, k), (n, k2) = a.shape, b.shape
    else:
        (k, m), (k2, n) = a.shape, b.shape
    assert k == k2, (name, a.shape, b.shape)
    if mode == "tn":
        tm = _pick(m, (1024, 640, 512, 256, 128))
        tn = _pick(n, (1024, 640, 512, 256, 128))
        tk = _pick(k, (2048, 1024, 512, 256))
    else:
        tm, tn, tk = _pick(m, (512, 256)), n, k
    nk = k // tk

    def body(a_ref, b_ref, *rest):
        o_ref = rest[-2] if nk > 1 else rest[-1]
        av = a_ref[...].astype(BF16)
        bv = b_ref[...].astype(BF16)
        if mode == "nn":
            part = jnp.dot(av, bv, preferred_element_type=F32)
        elif mode == "nt":
            part = lax.dot_general(av, bv, NT_DIMS, preferred_element_type=F32)
        else:
            part = lax.dot_general(av, bv, TN_DIMS, preferred_element_type=F32)
        if nk == 1:
            o_ref[...] = part.astype(out_dtype)
            return
        acc_ref = rest[-1]
        kk = pl.program_id(2)

        @pl.when(kk == 0)
        def _():
            acc_ref[...] = part

        @pl.when(kk > 0)
        def _():
            acc_ref[...] += part

        @pl.when(kk == nk - 1)
        def _():
            o_ref[...] = acc_ref[...].astype(out_dtype)

    if mode == "tn":
        a_spec = pl.BlockSpec((tk, tm), lambda j, i, kk: (kk, i))
    else:
        a_spec = pl.BlockSpec((tm, tk), lambda j, i, kk: (i, kk))
    if mode == "nt":
        b_spec = pl.BlockSpec((tn, tk), lambda j, i, kk: (j, kk))
    else:
        b_spec = pl.BlockSpec((tk, tn), lambda j, i, kk: (kk, j))
    in_specs, args = [a_spec, b_spec], [a, b]
    if after is not None:
        in_specs.append(pl.BlockSpec((8, HD), lambda j, i, kk: (0, 0)))
        args.append(after)
    return pl.pallas_call(
        body,
        name=name,
        grid=(n // tn, m // tm, nk),
        in_specs=in_specs,
        out_specs=pl.BlockSpec((tm, tn), lambda j, i, kk: (i, j)),
        out_shape=jax.ShapeDtypeStruct((m, n), out_dtype),
        scratch_shapes=[pltpu.VMEM((tm, tn), F32)] if nk > 1 else [],
        compiler_params=_params(dimension_semantics=("parallel", "parallel", "arbitrary")),
    )(*args)


def _mm_cat(name, a_list, b, out_dtype, after=None):
    m = a_list[0].shape[0]
    ks = [a.shape[1] for a in a_list]
    n = b.shape[1]
    assert sum(ks) == b.shape[0], (name, ks, b.shape)
    tm = _pick(m, (512, 256))
    na = len(a_list)

    def body(*refs):
        b_ref, o_ref = refs[na], refs[-1]
        k0, acc = 0, None
        for a_ref, kw in zip(refs[:na], ks):
            part = jnp.dot(a_ref[...].astype(BF16), b_ref[k0:k0 + kw, :], preferred_element_type=F32)
            acc = part if acc is None else acc + part
            k0 += kw
        o_ref[...] = acc.astype(out_dtype)

    in_specs = [pl.BlockSpec((tm, kw), lambda i: (i, 0)) for kw in ks] + [pl.BlockSpec(b.shape, lambda i: (0, 0))]
    args = [*a_list, b]
    if after is not None:
        in_specs.append(pl.BlockSpec((8, HD), lambda i: (0, 0)))
        args.append(after)
    return pl.pallas_call(
        body, name=name, grid=(m // tm,),
        in_specs=in_specs, out_specs=pl.BlockSpec((tm, n), lambda i: (i, 0)),
        out_shape=jax.ShapeDtypeStruct((m, n), out_dtype),
        compiler_params=_params(dimension_semantics=("parallel",)),
    )(*args)


def _row(c, col=0):
    return pl.BlockSpec((TS, c), lambda i: (i, col))


def _vec(r, c):
    return pl.BlockSpec((r, c), lambda i: (0, 0))


def _prenorm_proj(x, pre_gain, w_t, after):
    s = x.shape[0]
    n = w_t.shape[0]
    tm = 512

    def body(x_ref, g_ref, w_ref, after_ref, xn_ref, z_ref):
        xhat, _ = _rms_fwd(x_ref[...])
        xn = (xhat * g_ref[...]).astype(BF16)
        xn_ref[...] = xn
        z_ref[...] = lax.dot_general(xn, w_ref[...], NT_DIMS, preferred_element_type=F32).astype(BF16)

    return pl.pallas_call(
        body, name="prenorm_proj_qkv", grid=(s // tm,),
        in_specs=[pl.BlockSpec((tm, D), lambda i: (i, 0)), _vec(1, D), _vec(n, D), _vec(8, HD)],
        out_specs=[pl.BlockSpec((tm, D), lambda i: (i, 0)), pl.BlockSpec((tm, n), lambda i: (i, 0))],
        out_shape=[jax.ShapeDtypeStruct((s, D), BF16), jax.ShapeDtypeStruct((s, n), BF16)],
        compiler_params=_params(dimension_semantics=("parallel",)),
    )(x, pre_gain, w_t, after)


def _forget_fwd(zr, bf_pad):
    s = zr.shape[0]
    n = s // TQ

    def body(fl_ref, b_ref, kx_ref, c_buf, carry):
        i = pl.program_id(0)

        @pl.when(i == 0)
        def _():
            carry[...] = jnp.zeros_like(carry)

        fl = fl_ref[...] + b_ref[...]
        ls = jnp.minimum(fl, 0.0) - jnp.log(1.0 + jnp.exp(-jnp.abs(fl)))
        c_buf[...] = _cumsum_fwd(ls) + carry[0:1, :]
        carry[0:1, :] = c_buf[TQ - 1:TQ, :]
        cv = c_buf[...]
        for h in range(NH):
            kx_ref[h] = _bias_lanes(jnp.broadcast_to(cv[:, 8 * h:8 * h + 1], (TQ, HD)) * (-1.0 / SCALE), 0, 3)

    return pl.pallas_call(
        body, name="forget_fwd", grid=(n,),
        in_specs=[pl.BlockSpec((TQ, HD), lambda i: (i, FL_COL // HD)), _vec(1, HD)],
        out_specs=pl.BlockSpec((NH, TQ, HD), lambda i: (0, i, 0)),
        out_shape=jax.ShapeDtypeStruct((NH, s, HD), BF16),
        scratch_shapes=[pltpu.VMEM((TQ, HD), F32), pltpu.VMEM((8, HD), F32)],
        compiler_params=_params(dimension_semantics=("arbitrary",)),
    )(zr, bf_pad)


def _attn_fwd(zq, kx):
    s = zq.shape[0]
    n = s // TQ
    nb = TQ // HD

    def body(q_ref, k_ref, v_ref, kx_ref, o_ref, ax_ref):
        i = pl.program_id(1)
        lane = lax.broadcasted_iota(jnp.int32, (TQ, HD), 1)
        row = lax.broadcasted_iota(jnp.int32, (TQ, HD), 0)
        qa = jnp.concatenate([q_ref[...], jnp.where(lane < 3, 1.0, 0.0).astype(BF16)], axis=1)

        def step(j, carry, masked):
            m, l, acc = carry
            rows = pl.ds(pl.multiple_of(j * TQ, TQ), TQ)
            ka = jnp.concatenate([k_ref[rows, :], kx_ref[0, rows, :]], axis=1)
            v_all = v_ref[rows, :]
            rp = TQ // ROW_PARTS
            parts = [slice(rp * t, rp * (t + 1)) for t in range(ROW_PARTS)]
            keys = [rp * (t + 1) if masked else TQ for t in range(ROW_PARTS)]
            u_parts = [lax.dot_general(qa[part], ka[:kn], NT_DIMS, preferred_element_type=F32)
                       for part, kn in zip(parts, keys)]
            out = []
            for t, (part, u, kn) in enumerate(zip(parts, u_parts, keys)):
                us = [u[:, HD * b:HD * (b + 1)] for b in range(kn // HD)]
                if masked:
                    us = [ub if HD * (b + 1) <= rp * t else jnp.where(row[part] >= lane[part] + HD * b, ub, NEG)
                          for b, ub in enumerate(us)]
                v = v_all[:kn]
                bm = functools.reduce(jnp.maximum, us)
                m_new = jnp.maximum(m[part], jnp.max(bm, axis=1, keepdims=True))
                alpha = jnp.exp2((m[part] - m_new) * EXP2_SCALE)
                shift = m_new * EXP2_SCALE
                ps = [jnp.exp2(ub * EXP2_SCALE - shift) for ub in us]
                l_new = alpha * l[part] + functools.reduce(jnp.add, ps)
                pr = jnp.concatenate(ps, axis=1).astype(BF16)
                out.append((m_new, l_new, alpha * acc[part] + jnp.dot(pr, v, preferred_element_type=F32)))
            return tuple(jnp.concatenate([o[t] for o in out], axis=0) for t in range(3))

        init = (jnp.full((TQ, HD), NEG, F32), jnp.zeros((TQ, HD), F32), jnp.zeros((TQ, HD), F32))
        carry = lax.fori_loop(0, i, lambda j, cr: step(j, cr, False), init)
        m, l, acc = step(i, carry, True)
        l_row = jnp.sum(l, axis=1, keepdims=True)
        o_ref[...] = acc / l_row
        ax_ref[0] = _bias_lanes(-(m + jnp.log(l_row) * (1.0 / SCALE)), 3, 0)

    return pl.pallas_call(
        body, name="attn_fwd", grid=(NH, n),
        in_specs=[
            pl.BlockSpec((TQ, HD), lambda h, i: (i, h)),
            pl.BlockSpec((s, HD), lambda h, i: (0, NH + h)),
            pl.BlockSpec((s, HD), lambda h, i: (0, 2 * NH + h)),
            pl.BlockSpec((1, s, HD), lambda h, i: (h, 0, 0)),
        ],
        out_specs=[pl.BlockSpec((TQ, HD), lambda h, i: (i, h)), pl.BlockSpec((1, TQ, HD), lambda h, i: (h, i, 0))],
        out_shape=[jax.ShapeDtypeStruct((s, D), F32), jax.ShapeDtypeStruct((NH, s, HD), BF16)],
        compiler_params=_params(dimension_semantics=("parallel", "parallel")),
    )(zq, zq, zq, kx)


def _attn_bwd(zq, do, ax, delta, kx, after):
    s = zq.shape[0]
    n = s // TQ
    nb = TQ // HD

    def body(k_ref, v_ref, kx_ref, q_ref, ax_ref, do_ref, dl_ref, after_ref, dq_out, dk_ref, dv_ref, dcs_ref, drs_ref,
             dq_ref):
        j = pl.program_id(1)

        @pl.when(j == 0)
        def _():
            dq_ref[...] = jnp.zeros_like(dq_ref)
            drs_ref[...] = jnp.zeros_like(drs_ref)

        k = k_ref[...]
        v = v_ref[...]
        ka = jnp.concatenate([k, kx_ref[0]], axis=1)
        row = lax.broadcasted_iota(jnp.int32, (TQ, HD), 0)
        lane = lax.broadcasted_iota(jnp.int32, (TQ, HD), 1)

        def step(i, carry, r0, rn, kn, masked):
            dk, dv, dcs = carry
            rows = pl.ds(pl.multiple_of(i * TQ + r0, rn), rn)
            q = q_ref[rows, :]
            dout = do_ref[rows, :]
            dlv = dl_ref[0, rows, :]
            qa = jnp.concatenate([q, ax_ref[0, rows, :]], axis=1)
            u = lax.dot_general(qa, ka[:kn], NT_DIMS, preferred_element_type=F32)
            dp = lax.dot_general(dout, v[:kn], NT_DIMS, preferred_element_type=F32)
            prs, dss = [], []
            for b in range(kn // HD):
                cs = slice(HD * b, HD * (b + 1))
                ub = u[:, cs]
                if masked and HD * (b + 1) > r0:
                    ub = jnp.where(row[:rn] + r0 >= lane[:rn] + HD * b, ub, NEG)
                pb = jnp.exp2(ub * EXP2_SCALE)
                prs.append(pb)
                dss.append(pb * (dp[:, cs] - dlv))
            drs_ref[0, rows, :] += functools.reduce(jnp.add, dss)
            ds = jnp.concatenate(dss, axis=1)
            dsb = ds.astype(BF16)
            dcs_new = jnp.sum(ds.reshape(rn // 8, 8, kn), axis=0)
            dv_new = lax.dot_general(jnp.concatenate(prs, axis=1).astype(BF16), dout, TN_DIMS, preferred_element_type=F32)
            dk_new = lax.dot_general(dsb, q, TN_DIMS, preferred_element_type=F32)
            if kn < TQ:
                dcs_new = jnp.concatenate([dcs_new, jnp.zeros((8, TQ - kn), F32)], axis=1)
                dv_new = jnp.concatenate([dv_new, jnp.zeros((TQ - kn, HD), F32)], axis=0)
                dk_new = jnp.concatenate([dk_new, jnp.zeros((TQ - kn, HD), F32)], axis=0)
            dq_ref[rows, :] += jnp.dot(dsb, k[:kn], preferred_element_type=F32) * SCALE
            return dk + dk_new, dv + dv_new, dcs + dcs_new

        carry = (jnp.zeros((TQ, HD), F32), jnp.zeros((TQ, HD), F32), jnp.zeros((8, TQ), F32))
        rp = TQ // ROW_PARTS
        for t in range(ROW_PARTS):
            carry = step(j, carry, rp * t, rp, rp * (t + 1), True)
        dk, dv, dcs = lax.fori_loop(j + 1, n, lambda i, cr: step(i, cr, 0, TQ, TQ, False), carry)
        dk_ref[...] = (dk * SCALE).astype(BF16)
        dv_ref[...] = dv.astype(BF16)
        dcs_ref[0] = jnp.broadcast_to(_colsum(dcs), (8, TQ))

        @pl.when(j == n - 1)
        def _():
            dq_out[...] = dq_ref[...].astype(BF16)

    return pl.pallas_call(
        body, name="attn_bwd", grid=(NH, n),
        in_specs=[
            pl.BlockSpec((TQ, HD), lambda h, j: (j, NH + h)),
            pl.BlockSpec((TQ, HD), lambda h, j: (j, 2 * NH + h)),
            pl.BlockSpec((1, TQ, HD), lambda h, j: (h, j, 0)),
            pl.BlockSpec((s, HD), lambda h, j: (0, h)),
            pl.BlockSpec((1, s, HD), lambda h, j: (h, 0, 0)),
            pl.BlockSpec((s, HD), lambda h, j: (0, h)),
            pl.BlockSpec((1, s, HD), lambda h, j: (h, 0, 0)),
            pl.BlockSpec((8, HD), lambda h, j: (0, 0)),
        ],
        out_specs=[
            pl.BlockSpec((s, HD), lambda h, j: (0, h)),
            pl.BlockSpec((TQ, HD), lambda h, j: (j, h)),
            pl.BlockSpec((TQ, HD), lambda h, j: (j, h)),
            pl.BlockSpec((1, 8, TQ), lambda h, j: (j, h, 0)),
            pl.BlockSpec((1, s, HD), lambda h, j: (h, 0, 0)),
        ],
        out_shape=[
            jax.ShapeDtypeStruct((s, D), BF16),
            jax.ShapeDtypeStruct((s, D), BF16),
            jax.ShapeDtypeStruct((s, D), BF16),
            jax.ShapeDtypeStruct((n, 8 * NH, TQ), F32),
            jax.ShapeDtypeStruct((NH, s, HD), F32),
        ],
        scratch_shapes=[pltpu.VMEM((s, HD), F32)],
        compiler_params=_params(dimension_semantics=("parallel", "arbitrary")),
    )(zq, zq, kx, zq, ax, do, delta, after)


def _forget_bwd(dcs, drs, zr, bf_pad):
    n = dcs.shape[0]
    s = n * TQ

    def body(dcs_ref, drs_ref, fl_ref, b_ref, dfl_ref, gb_ref, buf, carry):
        i = pl.program_id(0)

        @pl.when(i == 0)
        def _():
            carry[...] = jnp.zeros_like(carry)
            gb_ref[...] = jnp.zeros_like(gb_ref)

        dc_t = jnp.concatenate([dcs_ref[0], jnp.zeros((HD - 8 * NH, TQ), F32)], axis=0)
        lane = lax.broadcasted_iota(jnp.int32, (TQ, HD), 1)
        dc = -dc_t.T
        for hh in range(NH):
            dc = dc + jnp.where(lane == 8 * hh, jnp.sum(drs_ref[hh], axis=1, keepdims=True), 0.0)
        buf[...] = _cumsum_bwd(dc) + carry[0:1, :]
        carry[0:1, :] = buf[0:1, :]
        fl = fl_ref[...] + b_ref[...]
        dfl = buf[...] * _sigmoid_rel(-fl)
        dfl_ref[...] = dfl.astype(BF16)
        gb_ref[...] += _colsum(dfl)

    return pl.pallas_call(
        body, name="forget_bwd", grid=(n,),
        in_specs=[
            pl.BlockSpec((1, 8 * NH, TQ), lambda i: (n - 1 - i, 0, 0)),
            pl.BlockSpec((NH, TQ, HD), lambda i: (0, n - 1 - i, 0)),
            pl.BlockSpec((TQ, HD), lambda i: (n - 1 - i, FL_COL // HD)),
            _vec(1, HD),
        ],
        out_specs=[pl.BlockSpec((TQ, HD), lambda i: (n - 1 - i, 0)), _vec(1, HD)],
        out_shape=[jax.ShapeDtypeStruct((s, HD), BF16), jax.ShapeDtypeStruct((1, HD), F32)],
        scratch_shapes=[pltpu.VMEM((TQ, HD), F32), pltpu.VMEM((8, HD), F32)],
        compiler_params=_params(dimension_semantics=("arbitrary",)),
    )(dcs, drs, zr, bf_pad)


def _gates(xc, w_ref, b, sigmoid):
    xb = xc.astype(BF16)
    pre = jnp.concatenate(
        [jnp.dot(xb[:, HD * g:HD * (g + 1)], w_ref[g], preferred_element_type=F32) for g in range(NH)], axis=1)
    return sigmoid(pre + b)


def _lru_coeffs(r, lam):
    sp = jnp.maximum(-lam, 0.0) + jnp.log(1.0 + jnp.exp(-jnp.abs(lam)))
    log_a = -LRU_C * r * sp
    a = jnp.exp(log_a)
    y = 2.0 * log_a
    em1 = jnp.where(jnp.abs(y) < 0.01, y * (1.0 + y * (0.5 + y * (1.0 / 6.0))), jnp.exp(y) - 1.0)
    em = -em1
    inv_gam = lax.rsqrt(jnp.maximum(em, 1e-37))
    return sp, a, em * inv_gam, inv_gam


def _conv_taps(ext, t):
    return [_shift_down(ext, CONV_W - 1 - jj, t) for jj in range(CONV_W)]


def _lru_fwd(zr, conv_w8, conv_b, w_r, b_r, w_i, b_i, lam):
    s = zr.shape[0]
    n = s // TS
    xl_col = 1

    def body(xl_ref, halo_ref, cw_ref, cb_ref, wr_ref, br_ref, wi_ref, bi_ref, lam_ref, xc_ref, h_ref, carry):
        i = pl.program_id(0)

        @pl.when(i == 0)
        def _():
            carry[...] = jnp.zeros_like(carry)

        halo = jnp.where(i == 0, 0.0, halo_ref[...])
        taps = _conv_taps(jnp.concatenate([halo, xl_ref[...]], axis=0), TS)
        xc = cb_ref[...] + sum(cw_ref[jj:jj + 1, :] * taps[jj] for jj in range(CONV_W))
        xc_ref[...] = xc
        r = _gates(xc, wr_ref, br_ref[...], _sigmoid_rel)
        ig = _gates(xc, wi_ref, bi_ref[...], _sigmoid)
        _, a, gam, _ = _lru_coeffs(r, lam_ref[...])
        a_cum, h_loc = _scan(a, gam * (ig * xc), False)
        h_ref[...] = h_loc + a_cum * carry[0:1, :]
        carry[0:1, :] = h_ref[TS - 1:TS, :]

    return pl.pallas_call(
        body, name="lru_fwd", grid=(n,),
        in_specs=[
            _row(D, xl_col),
            pl.BlockSpec((8, D), lambda i: (jnp.maximum(i * (TS // 8) - 1, 0), xl_col)),
            _vec(8, D), _vec(1, D),
            pl.BlockSpec((NH, HD, HD), lambda i: (0, 0, 0)), _vec(1, D),
            pl.BlockSpec((NH, HD, HD), lambda i: (0, 0, 0)), _vec(1, D),
            _vec(1, D),
        ],
        out_specs=[_row(D), _row(D)],
        out_shape=[jax.ShapeDtypeStruct((s, D), F32), jax.ShapeDtypeStruct((s, D), F32)],
        scratch_shapes=[pltpu.VMEM((8, D), F32)],
        compiler_params=_params(dimension_semantics=("arbitrary",)),
    )(zr, zr, conv_w8, conv_b, w_r, b_r, w_i, b_i, lam)


def _lru_bwd(zr, xc, h, dh, conv_w8, w_r, b_r, w_i, b_i, lam):
    s = zr.shape[0]
    n = s // TS
    xl_col = 1

    def rev(i):
        return n - 1 - i

    def body(xl_ref, xlh_ref, xc_ref, h_ref, hh_ref, dh_ref, cw_ref, wr_ref, br_ref, wi_ref, bi_ref, lam_ref,
             dxl_ref, gwr_ref, gwi_ref, gbr_ref, gbi_ref, glam_ref, gcb_ref, gcw_ref, l_buf, dxc_buf, carry_g, carry_dxc):
        i = pl.program_id(0)
        first = rev(i) == 0

        @pl.when(i == 0)
        def _():
            carry_g[...] = jnp.zeros_like(carry_g)
            carry_dxc[...] = jnp.zeros_like(carry_dxc)
            for ref in (gwr_ref, gwi_ref, gbr_ref, gbi_ref, glam_ref, gcb_ref, gcw_ref):
                ref[...] = jnp.zeros_like(ref)

        rows = _rows_iota(TS)
        xc = xc_ref[...]
        lam = lam_ref[...]
        r = _gates(xc, wr_ref, br_ref[...], _sigmoid_rel)
        ig = _gates(xc, wi_ref, bi_ref[...], _sigmoid)
        sp, a, gam, inv_gam = _lru_coeffs(r, lam)
        g = dh_ref[...] + jnp.where(rows == TS - 1, carry_g[0:1, :], 0.0)
        b = jnp.where(rows == TS - 1, 0.0, pltpu.roll(a, TS - 1, 0))
        l_buf[...] = _scan(b, g, True)[1]
        lv = l_buf[...]
        carry_g[0:1, :] = l_buf[0:1, :] * a[0:1, :]
        h_prev_row = jnp.where(first, 0.0, hh_ref[7:8, :])
        h_prev = jnp.where(rows == 0, h_prev_row, pltpu.roll(h_ref[...], 1, 0))
        dgam = lv * ig * xc
        dig = lv * gam * xc
        dxc = lv * gam * ig
        dla = lv * h_prev * a - dgam * (a * a) * inv_gam
        dr = dla * (-LRU_C) * sp
        glam_ref[...] += _colsum(dla * r) * (LRU_C * _sigmoid_rel(-lam))
        dpr = dr * r * (1.0 - r)
        dpi = dig * ig * (1.0 - ig)
        gbr_ref[...] += _colsum(dpr)
        gbi_ref[...] += _colsum(dpi)
        xb = xc.astype(BF16)
        dprb = dpr.astype(BF16)
        dpib = dpi.astype(BF16)
        back = []
        for gi in range(NH):
            cs = slice(HD * gi, HD * (gi + 1))
            gwr_ref[gi] += lax.dot_general(xb[:, cs], dprb[:, cs], TN_DIMS, preferred_element_type=F32)
            gwi_ref[gi] += lax.dot_general(xb[:, cs], dpib[:, cs], TN_DIMS, preferred_element_type=F32)
            back.append(lax.dot_general(dprb[:, cs], wr_ref[gi], NT_DIMS, preferred_element_type=F32)
                        + lax.dot_general(dpib[:, cs], wi_ref[gi], NT_DIMS, preferred_element_type=F32))
        dxc = dxc + jnp.concatenate(back, axis=1)
        dxc_buf[...] = dxc
        gcb_ref[...] += _colsum(dxc)
        halo = jnp.where(first, 0.0, xlh_ref[...])
        taps = _conv_taps(jnp.concatenate([halo, xl_ref[...]], axis=0), TS)
        for jj in range(CONV_W):
            gcw_ref[jj:jj + 1, :] += _colsum(dxc * taps[jj])
        ext = jnp.concatenate([dxc, carry_dxc[...]], axis=0)
        dxl = sum(cw_ref[jj:jj + 1, :] * _shift_up(ext, CONV_W - 1 - jj, TS) for jj in range(CONV_W))
        dxl_ref[...] = dxl.astype(BF16)
        carry_dxc[...] = dxc_buf[0:8, :]

    rowr = lambda c, col=0: pl.BlockSpec((TS, c), lambda i: (rev(i), col))
    halo = lambda col: pl.BlockSpec((8, D), lambda i: (jnp.maximum(rev(i) * (TS // 8) - 1, 0), col))
    gate_w = pl.BlockSpec((NH, HD, HD), lambda i: (0, 0, 0))
    return pl.pallas_call(
        body, name="lru_bwd", grid=(n,),
        in_specs=[rowr(D, xl_col), halo(xl_col), rowr(D), rowr(D), halo(0), rowr(D),
                  _vec(8, D), gate_w, _vec(1, D), gate_w, _vec(1, D), _vec(1, D)],
        out_specs=[rowr(D), gate_w, gate_w, _vec(1, D), _vec(1, D), _vec(1, D), _vec(1, D), _vec(8, D)],
        out_shape=[
            jax.ShapeDtypeStruct((s, D), BF16),
            jax.ShapeDtypeStruct((NH, HD, HD), F32), jax.ShapeDtypeStruct((NH, HD, HD), F32),
            jax.ShapeDtypeStruct((1, D), F32), jax.ShapeDtypeStruct((1, D), F32), jax.ShapeDtypeStruct((1, D), F32),
            jax.ShapeDtypeStruct((1, D), F32), jax.ShapeDtypeStruct((8, D), F32),
        ],
        scratch_shapes=[pltpu.VMEM((TS, D), F32), pltpu.VMEM((TS, D), F32), pltpu.VMEM((8, D), F32), pltpu.VMEM((8, D), F32)],
        compiler_params=_params(dimension_semantics=("arbitrary",)),
    )(zr, zr, xc, h, h, dh, conv_w8, w_r, b_r, w_i, b_i, lam)


def _silu_parts(g):
    sg = _sigmoid(g)
    return g * sg, sg * (1.0 + g * (1.0 - sg))


def _branch_out(o, h, zr, gain_a, gain_l):
    s = o.shape[0]

    def body(o_ref, ga_ref, h_ref, gl_ref, ka_ref, kl_ref, y_ref):
        ohat, _ = _rms_fwd(o_ref[...])
        y_ref[:, 0:D] = (ohat * ka_ref[...] * _silu_parts(ga_ref[...])[0]).astype(BF16)
        hhat, _ = _rms_fwd(h_ref[...])
        y_ref[:, D:2 * D] = (hhat * kl_ref[...] * _silu_parts(gl_ref[...])[0]).astype(BF16)

    return pl.pallas_call(
        body, name="branch_out", grid=(s // TS,),
        in_specs=[_row(D), _row(D, 0), _row(D), _row(D, 2), _vec(1, D), _vec(1, D)],
        out_specs=_row(2 * D),
        out_shape=jax.ShapeDtypeStruct((s, 2 * D), BF16),
        compiler_params=_params(dimension_semantics=("parallel",)),
    )(o, zr, h, zr, gain_a, gain_l)


def _branch_out_bwd(o, h, zr, dmix, w_out, gain_a, gain_l):
    s = o.shape[0]

    def body(o_ref, ga_ref, h_ref, gl_ref, dm_ref, w_ref, ka_ref, kl_ref,
             do_ref, dl_ref, dga_ref, dh_ref, dgl_ref, gka_ref, gkl_ref):
        @pl.when(pl.program_id(0) == 0)
        def _():
            gka_ref[...] = jnp.zeros_like(gka_ref)
            gkl_ref[...] = jnp.zeros_like(gkl_ref)

        dycat = lax.dot_general(dm_ref[...], w_ref[...], NT_DIMS, preferred_element_type=F32)

        def one(v, g, dy, gain):
            vhat, rstd = _rms_fwd(v)
            sg, dsg = _silu_parts(g)
            dn = dy * sg
            dg = dy * (vhat * gain) * dsg
            return _rms_bwd(vhat, rstd, dn * gain), dg, _colsum(dn * vhat)

        o = o_ref[...]
        dout, dga, gka = one(o, ga_ref[...], dycat[:, :D], ka_ref[...])
        do_ref[...] = dout.astype(BF16)
        dga_ref[...] = dga.astype(BF16)
        gka_ref[...] += gka
        prod = dout * o
        for hh in range(NH):
            dl_ref[hh] = jnp.broadcast_to(jnp.sum(prod[:, HD * hh:HD * (hh + 1)], axis=1, keepdims=True), (TS, HD))
        dh, dgl, gkl = one(h_ref[...], gl_ref[...], dycat[:, D:], kl_ref[...])
        dh_ref[...] = dh
        dgl_ref[...] = dgl.astype(BF16)
        gkl_ref[...] += gkl

    return pl.pallas_call(
        body, name="branch_out_bwd", grid=(s // TS,),
        in_specs=[_row(D), _row(D, 0), _row(D), _row(D, 2), _row(D), _vec(2 * D, D), _vec(1, D), _vec(1, D)],
        out_specs=[_row(D), pl.BlockSpec((NH, TS, HD), lambda i: (0, i, 0)), _row(D), _row(D), _row(D), _vec(1, D), _vec(1, D)],
        out_shape=[
            jax.ShapeDtypeStruct((s, D), BF16), jax.ShapeDtypeStruct((NH, s, HD), F32), jax.ShapeDtypeStruct((s, D), BF16),
            jax.ShapeDtypeStruct((s, D), F32), jax.ShapeDtypeStruct((s, D), BF16),
            jax.ShapeDtypeStruct((1, D), F32), jax.ShapeDtypeStruct((1, D), F32),
        ],
        compiler_params=_params(dimension_semantics=("arbitrary",)),
    )(o, zr, h, zr, dmix, w_out, gain_a, gain_l)


def _residual(x, ycat, w_out, post_gain):
    s = x.shape[0]

    def body(x_ref, y_ref, w_ref, g_ref, m_ref, h_ref, hb_ref):
        mix = jnp.dot(y_ref[...], w_ref[...], preferred_element_type=F32)
        m_ref[...] = mix
        mhat, _ = _rms_fwd(mix)
        h1 = x_ref[...] + mhat * g_ref[...]
        h_ref[...] = h1
        hb_ref[...] = h1.astype(BF16)

    return pl.pallas_call(
        body, name="residual", grid=(s // TS,),
        in_specs=[_row(D), _row(2 * D), _vec(2 * D, D), _vec(1, D)], out_specs=[_row(D), _row(D), _row(D)],
        out_shape=[jax.ShapeDtypeStruct((s, D), F32), jax.ShapeDtypeStruct((s, D), F32), jax.ShapeDtypeStruct((s, D), BF16)],
        compiler_params=_params(dimension_semantics=("parallel",)),
    )(x, ycat, w_out, post_gain)


def _head(h1, p, tgt, mix, w_gate, w_ple, ple_gain, b_gate, post_gain):
    s = h1.shape[0]

    def body(h_ref, p_ref, t_ref, m_ref, wg_ref, wp_ref, kg_ref, b_ref, pg_ref,
             loss_ref, dgp_ref, dpe_ref, dh_ref, dm_ref, gk_ref, gb_ref, gg_ref):
        @pl.when(pl.program_id(0) == 0)
        def _():
            for ref in (loss_ref, gk_ref, gb_ref, gg_ref):
                ref[...] = jnp.zeros_like(ref)

        h1 = h_ref[...]
        pe = jnp.dot(p_ref[...].astype(BF16), wp_ref[...], preferred_element_type=F32)
        gp = jnp.dot(h1.astype(BF16), wg_ref[...], preferred_element_type=F32)
        ehat, rstd = _rms_fwd(pe)
        e = ehat * kg_ref[...]
        gate = _sigmoid(gp + b_ref[...])
        diff = (h1 + gate * e) - t_ref[...]
        per_row = jnp.mean(diff * diff, axis=-1, keepdims=True)
        loss_ref[...] += 0.5 * jnp.sum(per_row, axis=0, keepdims=True)
        dy = diff * (1.0 / D)
        dgp = dy * e * gate * (1.0 - gate)
        dgpb = dgp.astype(BF16)
        dgp_ref[...] = dgpb
        gb_ref[...] += _colsum(dgp)
        de = dy * gate
        gk_ref[...] += _colsum(de * ehat)
        dpe_ref[...] = _rms_bwd(ehat, rstd, de * kg_ref[...]).astype(BF16)
        dh1 = dy + lax.dot_general(dgpb, wg_ref[...], NT_DIMS, preferred_element_type=F32)
        dh_ref[...] = dh1
        mhat, rstd_m = _rms_fwd(m_ref[...])
        gg_ref[...] += _colsum(dh1 * mhat)
        dm_ref[...] = _rms_bwd(mhat, rstd_m, dh1 * pg_ref[...]).astype(BF16)

    return pl.pallas_call(
        body, name="head", grid=(s // TS,),
        in_specs=[_row(D), _row(D_PLE), _row(D), _row(D), _vec(D, D), _vec(D_PLE, D), _vec(1, D), _vec(1, D), _vec(1, D)],
        out_specs=[_vec(1, 1), _row(D), _row(D), _row(D), _row(D), _vec(1, D), _vec(1, D), _vec(1, D)],
        out_shape=[
            jax.ShapeDtypeStruct((1, 1), F32), jax.ShapeDtypeStruct((s, D), BF16), jax.ShapeDtypeStruct((s, D), BF16),
            jax.ShapeDtypeStruct((s, D), F32), jax.ShapeDtypeStruct((s, D), BF16),
            jax.ShapeDtypeStruct((1, D), F32), jax.ShapeDtypeStruct((1, D), F32), jax.ShapeDtypeStruct((1, D), F32),
        ],
        compiler_params=_params(dimension_semantics=("arbitrary",)),
    )(h1, p, tgt, mix, w_gate, w_ple, ple_gain, b_gate, post_gain)


def _prenorm_bwd(x, dxn_a, dxn_b, dh1, pre_gain):
    s = x.shape[0]

    def body(x_ref, da_ref, db_ref, dh_ref, g_ref, dx_ref, gg_ref):
        @pl.when(pl.program_id(0) == 0)
        def _():
            gg_ref[...] = jnp.zeros_like(gg_ref)

        xhat, rstd = _rms_fwd(x_ref[...])
        dxn = da_ref[...] + db_ref[...]
        gg_ref[...] += _colsum(dxn * xhat)
        dx_ref[...] = dh_ref[...] + _rms_bwd(xhat, rstd, dxn * g_ref[...])

    return pl.pallas_call(
        body, name="prenorm_bwd", grid=(s // TS,),
        in_specs=[_row(D), _row(D), _row(D), _row(D), _vec(1, D)], out_specs=[_row(D), _vec(1, D)],
        out_shape=[jax.ShapeDtypeStruct((s, D), F32), jax.ShapeDtypeStruct((1, D), F32)],
        compiler_params=_params(dimension_semantics=("arbitrary",)),
    )(x, dxn_a, dxn_b, dh1, pre_gain)


def _adamw(name, parts, w, m, v, own=None, me=None):
    r, c = w.shape
    if r % 8 == 0:
        tr = _pick(r, (256, 128, 16, 8))
        grid = (r // tr,)
        blk = pl.BlockSpec((tr, c), lambda i: (i, 0))
        parts_blk = pl.BlockSpec((N_DEV, tr, c), lambda i: (0, i, 0))
    else:
        tc = _pick(c, (256, 128))
        grid = (c // tc,)
        blk = pl.BlockSpec((r, tc), lambda i: (0, i))
        parts_blk = pl.BlockSpec((N_DEV, r, tc), lambda i: (0, 0, i))

    def body(*refs):
        p_ref, w_ref, m_ref, v_ref = refs[:4]
        g_ref, d_ref, nm_ref, nv_ref = refs[-4:]
        if own is None:
            g = p_ref[0].astype(F32)
            for j in range(1, N_DEV):
                g = g + p_ref[j].astype(F32)
            g_ref[...] = g
        else:
            own_ref, me_ref = refs[4:6]
            g_ref[...] = jnp.zeros_like(g_ref)
            for j in range(N_DEV):
                @pl.when(me_ref[0] == j)
                def _():
                    g_ref[...] += own_ref[...].astype(F32)

                @pl.when(me_ref[0] != j)
                def _():
                    g_ref[...] += p_ref[j].astype(F32)
            g = g_ref[...]
        nm = ADAM_B1 * m_ref[...] + (1.0 - ADAM_B1) * g
        nv = ADAM_B2 * v_ref[...] + (1.0 - ADAM_B2) * (g * g)
        nm_ref[...] = nm
        nv_ref[...] = nv
        m_hat = nm / (1.0 - ADAM_B1 ** ADAM_STEP)
        v_hat = nv / (1.0 - ADAM_B2 ** ADAM_STEP)
        d_ref[...] = -ADAM_LR * (m_hat / (jnp.sqrt(v_hat) + ADAM_EPS) + ADAM_WD * w_ref[...])

    in_specs, args = [parts_blk, blk, blk, blk], [parts, w, m, v]
    if own is not None:
        in_specs += [blk, pl.BlockSpec(memory_space=pltpu.SMEM)]
        args += [own, me]
    return pl.pallas_call(
        body, name=name, grid=grid,
        in_specs=in_specs,
        out_specs=[blk] * 4,
        out_shape=[jax.ShapeDtypeStruct((r, c), F32)] * 4,
        compiler_params=_params(dimension_semantics=("parallel",)),
    )(*args)


def _spread8(v):
    r = v.shape[0]
    return jnp.pad(jnp.pad(v[:, :, None], ((0, 0), (0, 0), (0, 7))).reshape(r, 8 * NH), ((0, 0), (0, HD - 8 * NH)))


def _gather8(v):
    return v[:, :8 * NH].reshape(v.shape[0], NH, 8)[:, :, 0]


def _cols_to_shards(g):
    r, c8 = g.shape
    return g.reshape(r, N_DEV, c8 // N_DEV).transpose(1, 0, 2)


def _shards_to_cols(g):
    n, r, c = g.shape
    return g.transpose(1, 0, 2).reshape(r, n * c)


def kernel(x, p, w_in, b_f, pre_gain, post_gain, conv_w, conv_b, w_rgate, b_rgate, w_igate, b_igate, lru_lambda, attn_out_gain, lru_out_gain, w_out, w_ple, ple_gain, w_ple_gate, b_ple_gate, loss_target, m_w_in, m_b_f, m_pre_gain, m_post_gain, m_conv_w, m_conv_b, m_w_rgate, m_b_rgate, m_w_igate, m_b_igate, m_lru_lambda, m_attn_out_gain, m_lru_out_gain, m_w_out, m_w_ple, m_ple_gain, m_w_ple_gate, m_b_ple_gate, v_w_in, v_b_f, v_pre_gain, v_post_gain, v_conv_w, v_conv_b, v_w_rgate, v_b_rgate, v_w_igate, v_b_igate, v_lru_lambda, v_attn_out_gain, v_lru_out_gain, v_w_out, v_w_ple, v_ple_gain, v_w_ple_gate, v_b_ple_gate):
    me = 4 * lax.axis_index("x") + 2 * lax.axis_index("y") + lax.axis_index("c")
    x2, p2, tgt = x[0], p[0, 0], loss_target[0]

    conv_w_shard8 = jnp.pad(conv_w[0], ((0, 8 - CONV_W), (0, 0)))
    wt, m_wt, v_wt = w_in[0].T, m_w_in[0].T, v_w_in[0].T
    g_wint, g_conv = _gather_two_level("gather_w_in", [wt.astype(BF16), conv_w_shard8])
    win_t = g_wint.reshape(D_IN, D)
    rest_state, rest_token = _exchange_start(
        "gather_rest_start", [w_out[0].astype(BF16), w_ple[0].astype(BF16), w_ple_gate[0].astype(BF16)], ["bcast"] * 3,
        after=g_conv)
    w_qkv_t = win_t[:D_QKV]
    w_rest_t = jnp.concatenate([win_t[D_QKV + NH:], _spread8(win_t[D_QKV:D_QKV + NH].T).T], axis=0)
    conv_w8 = _shards_to_cols(g_conv)
    bf_pad = _spread8(b_f)
    w_r, w_i = w_rgate[0].astype(BF16), w_igate[0].astype(BF16)

    xn, zq = _prenorm_proj(x2, pre_gain, w_qkv_t, rest_token)
    zr = _mm("proj_rest", xn, w_rest_t, "nt", F32)
    kx = _forget_fwd(zr, bf_pad)
    o, ax = _attn_fwd(zq, kx)
    xc, h = _lru_fwd(zr, conv_w8, conv_b, w_r, b_rgate, w_i, b_igate, lru_lambda)
    ycat = _branch_out(o, h, zr, attn_out_gain, lru_out_gain)
    g_wout, g_wple, g_wpg = _exchange_wait("gather_rest_wait", rest_state, ycat)
    wout_full = g_wout.reshape(2 * D, D)
    wple_full = _shards_to_cols(g_wple)
    wpg_full = g_wpg.reshape(D, D)
    mix, h1, h1b = _residual(x2, ycat, wout_full, post_gain)

    loss_part, dgp, dpe, dh1, dmix, g_ple_gain, g_b_gate, g_post_gain = _head(
        h1, p2, tgt, mix, wpg_full, wple_full, ple_gain, b_ple_gate, post_gain)
    loss = lax.psum(loss_part[0, 0], ("x", "y", "c"))
    gw_pg = _mm("bwd_gate_w", h1b, dgp, "tn", BF16)
    gw_ple = _mm("bwd_ple_w", p2, dpe, "tn", BF16)
    gw_out = _mm("bwd_out_w", ycat, dmix, "tn", BF16)
    do, delta, dga, dh, dgl, g_aog, g_log = _branch_out_bwd(o, h, zr, dmix, wout_full, attn_out_gain, lru_out_gain)
    dxl, g_wr, g_wi, g_br, g_bi, g_lam, g_cb, g_cw8 = _lru_bwd(
        zr, xc, h, dh, conv_w8, w_r, b_rgate, w_i, b_igate, lru_lambda)
    gates = jnp.concatenate([g_wr.reshape(D, HD), g_wi.reshape(D, HD)], axis=0).astype(BF16)
    outw_state, outw_token = _exchange_start(
        "exchange_outw_start",
        [gw_out.reshape(N_DEV, 2 * D // N_DEV, D), _cols_to_shards(gw_ple), gw_pg.reshape(N_DEV, D // N_DEV, D), gates],
        ["scatter"] * 3 + ["bcast"])
    dq, dk, dv, dcs, drs = _attn_bwd(zq, do, ax, delta, kx, outw_token)
    dfl, g_bf_pad = _forget_bwd(dcs, drs, zr, bf_pad)
    gw_pieces = [_mm("bwd_w_" + nm, dz, xn, "tn", BF16) for nm, dz in
                 (("q", dq), ("k", dk), ("v", dv), ("fl", dfl), ("ga", dga), ("xl", dxl), ("gl", dgl))]
    gw_pieces[3] = _gather8(gw_pieces[3].T).T
    gw_in_t = jnp.concatenate(gw_pieces, axis=0)
    inw_state, inw_token = _exchange_start(
        "exchange_inw_start", [gw_in_t.reshape(N_DEV, D_IN_SHARD, D)], ["scatter"])
    dxn_a = _mm_cat("bwd_qkv_x", [dq, dk, dv], w_qkv_t, F32, after=inw_token)
    dxn_b = _mm_cat("bwd_rest_x", [dga, dxl, dgl, dfl], w_rest_t, F32, after=inw_token)
    grad_x, g_pre_gain = _prenorm_bwd(x2, dxn_a, dxn_b, dh1, pre_gain)

    upd = {}
    me1 = me.reshape(1).astype(jnp.int32)
    r_wout, r_wple, r_wpg, r_gates = _exchange_wait("exchange_outw_wait", outw_state, grad_x, fill_own=False)
    upd["w_out"] = _adamw("adamw_w_out", r_wout[0], w_out[0], m_w_out[0], v_w_out[0], r_wout[1], me1)
    upd["w_ple"] = _adamw("adamw_w_ple", r_wple[0], w_ple[0], m_w_ple[0], v_w_ple[0], r_wple[1], me1)
    upd["w_ple_gate"] = _adamw("adamw_w_ple_gate", r_wpg[0], w_ple_gate[0], m_w_ple_gate[0], v_w_ple_gate[0], r_wpg[1], me1)
    gates_of = lambda a, b: jnp.concatenate([a[0].reshape(D, HD), b[0].reshape(D, HD)], axis=0)
    g_gates = _adamw("adamw_gates", r_gates[0], gates_of(w_rgate, w_igate), gates_of(m_w_rgate, m_w_igate),
                     gates_of(v_w_rgate, v_w_igate), r_gates[1], me1)
    upd["w_rgate"] = [a[:D].reshape(1, NH, HD, HD) for a in g_gates]
    upd["w_igate"] = [a[D:].reshape(1, NH, HD, HD) for a in g_gates]
    behind = upd["w_out"][0][0:1] + upd["w_ple_gate"][0][0:1] + jnp.pad(g_gates[0][0:1], ((0, 0), (0, D - HD)))
    small = jnp.concatenate(
        [jnp.pad(_gather8(g_bf_pad), ((0, 0), (0, D - NH))), g_pre_gain, g_post_gain, g_cb, g_br, g_bi, g_lam, g_aog, g_log,
         g_ple_gain, g_b_gate, g_cw8[:CONV_W], behind], axis=0)
    (r_small,) = _exchange("exchange_small", [small], ["bcast"])
    vec_names = ["b_f", "pre_gain", "post_gain", "conv_b", "b_rgate", "b_igate", "lru_lambda", "attn_out_gain",
                 "lru_out_gain", "ple_gain", "b_ple_gate"]
    vec_w = dict(b_f=(b_f, m_b_f, v_b_f), pre_gain=(pre_gain, m_pre_gain, v_pre_gain),
                 post_gain=(post_gain, m_post_gain, v_post_gain), conv_b=(conv_b, m_conv_b, v_conv_b),
                 b_rgate=(b_rgate, m_b_rgate, v_b_rgate), b_igate=(b_igate, m_b_igate, v_b_igate),
                 lru_lambda=(lru_lambda, m_lru_lambda, v_lru_lambda),
                 attn_out_gain=(attn_out_gain, m_attn_out_gain, v_attn_out_gain),
                 lru_out_gain=(lru_out_gain, m_lru_out_gain, v_lru_out_gain), ple_gain=(ple_gain, m_ple_gain, v_ple_gain),
                 b_ple_gate=(b_ple_gate, m_b_ple_gate, v_b_ple_gate))
    conv_mine = lambda a: lax.dynamic_slice_in_dim(a, me * HD, HD, axis=1)

    def small_rows(k):
        rows = [jnp.pad(vec_w[nm][k], ((0, 0), (0, D - vec_w[nm][k].shape[1]))) for nm in vec_names]
        cw = (conv_w, m_conv_w, v_conv_w)[k][0]
        full = lax.dynamic_update_slice_in_dim(jnp.ones((CONV_W, D), F32), cw, me * HD, axis=1)
        return jnp.concatenate(rows + [full, jnp.ones((1, D), F32)], axis=0)

    g_small = _adamw("adamw_small", r_small, small_rows(0), small_rows(1), small_rows(2))
    for idx, nm in enumerate(vec_names):
        width = vec_w[nm][0].shape[1]
        upd[nm] = [a[idx:idx + 1, :width] for a in g_small]
    base = len(vec_names)
    upd["conv_w"] = [conv_mine(a[base:base + CONV_W])[None] for a in g_small]
    (r_win,) = _exchange_wait("exchange_inw_wait", inw_state, g_small[0], fill_own=False)
    upd["w_in"] = [a.T for a in _adamw("adamw_w_in", r_win[0], wt, m_wt, v_wt, r_win[1], me1)]
    for nm in ("w_in", "w_out", "w_ple", "w_ple_gate"):
        upd[nm] = [a[None] for a in upd[nm]]

    order = ["w_in", "b_f", "pre_gain", "post_gain", "conv_w", "conv_b", "w_rgate", "b_rgate", "w_igate", "b_igate",
             "lru_lambda", "attn_out_gain", "lru_out_gain", "w_out", "w_ple", "ple_gain", "w_ple_gate", "b_ple_gate"]
    outs = [loss, grad_x[None]]
    for k in range(4):
        outs += [upd[nm][k] for nm in order]
    return tuple(outs)
```

```python
import functools

import jax
import jax.numpy as jnp
from jax import lax
from jax.experimental import pallas as pl
from jax.experimental.pallas import tpu as pltpu

F32 = jnp.float32
BF16 = jnp.bfloat16

N_DEV = 8
D = 1024
HD = 128
NH = 8
D_IN = 6152
D_IN_SHARD = D_IN // N_DEV
D_QKV = 3 * D
D_REST = 3 * D + HD
FL_COL = 3 * D
D_PLE = 256
CONV_W = 4
LRU_C = 8.0
RMS_EPS = 1e-6
SCALE = HD ** -0.5
EXP2_SCALE = SCALE * 1.4426950408889634
NEG = -1e30

ADAM_LR = 0.001
ADAM_B1 = 0.9
ADAM_B2 = 0.999
ADAM_EPS = 1e-08
ADAM_WD = 0.01
ADAM_STEP = 10

TS = 256
TQ = 1024
ROW_PARTS = 2
VMEM_LIMIT = 48 * 1024 * 1024

NT_DIMS = (((1,), (1,)), ((), ()))
TN_DIMS = (((0,), (0,)), ((), ()))


def _params(**kw):
    return pltpu.CompilerParams(vmem_limit_bytes=VMEM_LIMIT, **kw)


def _sigmoid(v):
    return 0.5 * jnp.tanh(0.5 * v) + 0.5


def _sigmoid_rel(v):
    return 1.0 / (1.0 + jnp.exp(-v))


def _rms_fwd(v):
    rstd = lax.rsqrt(jnp.mean(v * v, axis=-1, keepdims=True) + RMS_EPS)
    return v * rstd, rstd


def _rms_bwd(vhat, rstd, dvhat):
    return rstd * (dvhat - vhat * jnp.mean(dvhat * vhat, axis=-1, keepdims=True))


def _colsum(v):
    return jnp.sum(v, axis=0, keepdims=True)


def _rows_iota(t):
    return lax.broadcasted_iota(jnp.int32, (t, 1), 0)


def _scan(a, u, reverse):
    t, c = a.shape
    rows = _rows_iota(t)
    d = 1
    while d < t:
        if d < 8:
            valid = rows < t - d if reverse else rows >= d
            shift = t - d if reverse else d
            u = jnp.where(valid, u + a * pltpu.roll(u, shift, 0), u)
            a = jnp.where(valid, a * pltpu.roll(a, shift, 0), a)
        else:
            zeros, ones = jnp.zeros((d, c), F32), jnp.ones((d, c), F32)
            if reverse:
                u_far, a_far = jnp.concatenate([u[d:], zeros], axis=0), jnp.concatenate([a[d:], ones], axis=0)
            else:
                u_far, a_far = jnp.concatenate([zeros, u[:t - d]], axis=0), jnp.concatenate([ones, a[:t - d]], axis=0)
            u = u + a * u_far
            a = a * a_far
        d *= 2
    return a, u


def _cumsum_fwd(v):
    t = v.shape[0]
    rows = _rows_iota(t)
    d = 1
    while d < t:
        v = jnp.where(rows >= d, v + pltpu.roll(v, d, 0), v)
        d *= 2
    return v


def _cumsum_bwd(v):
    t = v.shape[0]
    rows = _rows_iota(t)
    d = 1
    while d < t:
        v = jnp.where(rows < t - d, v + pltpu.roll(v, t - d, 0), v)
        d *= 2
    return v


def _bias_lanes(v, at, ones_at):
    lane = lax.broadcasted_iota(jnp.int32, v.shape, 1)
    hi = v.astype(BF16).astype(F32)
    mid = (v - hi).astype(BF16).astype(F32)
    lo = ((v - hi) - mid).astype(BF16).astype(F32)
    out = jnp.where((lane >= ones_at) & (lane < ones_at + 3), 1.0, 0.0)
    for k, piece in enumerate((hi, mid, lo)):
        out = jnp.where(lane == at + k, piece, out)
    return out.astype(BF16)


def _shift_down(ext, k, t):
    return pltpu.roll(ext, k, 0)[8:, :] if k else ext[8:, :]


def _shift_up(ext, k, t):
    return pltpu.roll(ext, t + 8 - k, 0)[:t, :] if k else ext[:t, :]


def _exchange(name, arrs, kinds):
    n = len(arrs)
    out_shape = []
    for a, kind in zip(arrs, kinds):
        shp = a.shape if kind == "scatter" else (N_DEV,) + a.shape
        out_shape.append(jax.ShapeDtypeStruct(shp, a.dtype))

    def body(*refs):
        ins, outs = refs[:n], refs[n:2 * n]
        send_sems, recv_sems, local_sems = refs[2 * n:]
        x, y, c = lax.axis_index("x"), lax.axis_index("y"), lax.axis_index("c")
        me = 4 * x + 2 * y + c
        copies = []
        for i in range(n):
            scatter = kinds[i] == "scatter"
            mine = pltpu.make_async_copy(ins[i].at[me] if scatter else ins[i], outs[i].at[me], local_sems.at[i])
            mine.start()
            copies.append(mine)
            for m in range(1, N_DEV):
                px = 1 - x if m & 4 else x
                py = 1 - y if m & 2 else y
                pc = 1 - c if m & 1 else c
                peer = 4 * px + 2 * py + pc
                cp = pltpu.make_async_remote_copy(
                    src_ref=ins[i].at[peer] if scatter else ins[i],
                    dst_ref=outs[i].at[me],
                    send_sem=send_sems.at[i, m - 1],
                    recv_sem=recv_sems.at[i, m - 1],
                    device_id=(px, py, pc),
                    device_id_type=pl.DeviceIdType.MESH,
                )
                cp.start()
                copies.append(cp)
        for cp in copies:
            cp.wait()

    any_spec = pl.BlockSpec(memory_space=pl.ANY)
    return pl.pallas_call(
        body,
        name=name,
        out_shape=out_shape,
        in_specs=[any_spec] * n,
        out_specs=[any_spec] * n,
        scratch_shapes=[
            pltpu.SemaphoreType.DMA((n, N_DEV - 1)),
            pltpu.SemaphoreType.DMA((n, N_DEV - 1)),
            pltpu.SemaphoreType.DMA((n,)),
        ],
        compiler_params=pltpu.CompilerParams(has_side_effects=True),
    )(*arrs)


def _gather_two_level(name, arrs, pieces=1):
    n = len(arrs)
    items = []
    for i, a in enumerate(arrs):
        rows = a.shape[0]
        if pieces > 1 and rows >= 512:
            step = -(-rows // (16 * pieces)) * 16
            items += [(i, r0, min(step, rows - r0)) for r0 in range(0, rows, step)]
        else:
            items.append((i, 0, rows))
    n_items = len(items)

    def body(*refs):
        ins, outs = refs[:n], refs[n:2 * n]
        send_sems, recv_sems, local_sems = refs[2 * n:]
        x, y, c = lax.axis_index("x"), lax.axis_index("y"), lax.axis_index("c")
        me, sibling = (x, y, c), (x, y, 1 - c)
        chips = [(1 - x, y), (x, 1 - y), (1 - x, 1 - y)]

        def rows_of(ref, t):
            i, r0, rn = items[t]
            return ref if rn == arrs[i].shape[0] else ref.at[pl.ds(r0, rn)]

        def slot(t, dev):
            return rows_of(outs[items[t][0]].at[4 * dev[0] + 2 * dev[1] + dev[2]], t)

        def copy(t, k, block, to, from_input=False):
            return pltpu.make_async_remote_copy(
                src_ref=rows_of(ins[items[t][0]], t) if from_input else slot(t, block), dst_ref=slot(t, block),
                send_sem=send_sems.at[t, k], recv_sem=recv_sems.at[t, k],
                device_id=to, device_id_type=pl.DeviceIdType.MESH)

        own, sent = [], []
        for t in range(n_items):
            mine = pltpu.make_async_copy(rows_of(ins[items[t][0]], t), slot(t, me), local_sems.at[t])
            mine.start()
            own.append(mine)
            first = [copy(t, 1 + j, me, (*chip, c), from_input=True) for j, chip in enumerate(chips)]
            first.append(copy(t, 0, me, sibling, from_input=True))
            for cp in first:
                cp.start()
            sent += first
        for t in range(n_items):
            for j, chip in enumerate(chips):
                copy(t, 1 + j, (*chip, c), me).wait_recv()
                fwd = copy(t, 4 + j, (*chip, c), sibling)
                fwd.start()
                sent.append(fwd)
        for t in range(n_items):
            copy(t, 0, sibling, me).wait_recv()
            for j, chip in enumerate(chips):
                copy(t, 4 + j, (*chip, 1 - c), me).wait_recv()
        for cp in sent:
            cp.wait_send()
        for cp in own:
            cp.wait()

    any_spec = pl.BlockSpec(memory_space=pl.ANY)
    return pl.pallas_call(
        body, name=name,
        out_shape=[jax.ShapeDtypeStruct((N_DEV,) + a.shape, a.dtype) for a in arrs],
        in_specs=[any_spec] * n, out_specs=[any_spec] * n,
        scratch_shapes=[pltpu.SemaphoreType.DMA((n_items, 7)), pltpu.SemaphoreType.DMA((n_items, 7)),
                        pltpu.SemaphoreType.DMA((n_items,))],
        compiler_params=pltpu.CompilerParams(has_side_effects=True),
    )(*arrs)


def _peers(x, y, c):
    out = []
    for m in range(1, N_DEV):
        px = 1 - x if m & 4 else x
        py = 1 - y if m & 2 else y
        pc = 1 - c if m & 1 else c
        out.append((m, (px, py, pc), 4 * px + 2 * py + pc))
    return out


def _split_copies(kinds, src_refs, land_refs, send_sems, recv_sems):
    x, y, c = lax.axis_index("x"), lax.axis_index("y"), lax.axis_index("c")
    me = 4 * x + 2 * y + c
    copies = []
    for i, kind in enumerate(kinds):
        for m, peer, pidx in _peers(x, y, c):
            copies.append(pltpu.make_async_remote_copy(
                src_ref=src_refs[i].at[pidx] if kind == "scatter" else src_refs[i],
                dst_ref=land_refs[i].at[me],
                send_sem=send_sems.at[i * (N_DEV - 1) + m - 1],
                recv_sem=recv_sems.at[i * (N_DEV - 1) + m - 1],
                device_id=peer,
                device_id_type=pl.DeviceIdType.MESH,
            ))
    return copies


_HBM_SPEC = pl.BlockSpec(memory_space=pltpu.HBM)
_SEM_SPEC = pl.BlockSpec(memory_space=pltpu.SEMAPHORE)
_DATAFLOW = pltpu.SideEffectType.DATAFLOW_SIDE_EFFECTING


def _exchange_start(name, arrs, kinds, after=None):
    n = len(arrs)
    extra = [] if after is None else [after]
    lands = []
    for a, kind in zip(arrs, kinds):
        shp = a.shape if kind == "scatter" else (N_DEV,) + a.shape
        lands.append(lax.empty(shp, a.dtype))

    def body(*refs):
        src_refs, land_refs = refs[:n], refs[n:2 * n]
        send_sems, recv_sems = refs[2 * n + len(extra):2 * n + len(extra) + 2]
        token = refs[-1]
        for cp in _split_copies(kinds, src_refs, land_refs, send_sems, recv_sems):
            cp.start()
        token[...] = jnp.zeros_like(token)

    n_sem = n * (N_DEV - 1)
    hbm = lambda a: pltpu.HBM(a.shape, a.dtype)
    res = pl.pallas_call(
        body, name=name,
        out_shape=(pltpu.SemaphoreType.DMA((n_sem,)), pltpu.SemaphoreType.DMA((n_sem,)),
                   *[hbm(a) for a in arrs], *[hbm(a) for a in lands], jax.ShapeDtypeStruct((8, HD), F32)),
        in_specs=[_HBM_SPEC] * (2 * n) + [pl.BlockSpec(memory_space=pl.ANY)] * len(extra),
        out_specs=(_SEM_SPEC, _SEM_SPEC, *[_HBM_SPEC] * (2 * n), pl.BlockSpec(memory_space=pltpu.VMEM)),
        input_output_aliases={i: 2 + i for i in range(2 * n)},
        compiler_params=pltpu.CompilerParams(has_side_effects=_DATAFLOW),
    )(*[pltpu.with_memory_space_constraint(a, pltpu.HBM) for a in arrs],
      *[pltpu.with_memory_space_constraint(a, pltpu.HBM) for a in lands], *extra)
    return (kinds, res[0], res[1], res[2:2 + n], res[2 + n:2 + 2 * n]), res[-1]


def _exchange_wait(name, state, after, fill_own=True):
    kinds, send_sems, recv_sems, srcs, lands = state
    n = len(srcs)

    def body(*refs):
        src_refs, land_refs = refs[:n], refs[n:2 * n]
        send_sems_ref, recv_sems_ref = refs[2 * n:2 * n + 2]
        for cp in _split_copies(kinds, src_refs, land_refs, send_sems_ref, recv_sems_ref):
            cp.wait_send()
            cp.wait_recv()

    res = pl.pallas_call(
        body, name=name,
        out_shape=tuple(pltpu.HBM(a.shape, a.dtype) for a in (*srcs, *lands)),
        in_specs=[_HBM_SPEC] * (2 * n) + [_SEM_SPEC, _SEM_SPEC, pl.BlockSpec(memory_space=pl.ANY)],
        out_specs=tuple([_HBM_SPEC] * (2 * n)),
        input_output_aliases={i: i for i in range(2 * n)},
        compiler_params=pltpu.CompilerParams(has_side_effects=_DATAFLOW),
    )(*srcs, *lands, send_sems, recv_sems, after)
    me = 4 * lax.axis_index("x") + 2 * lax.axis_index("y") + lax.axis_index("c")
    outs = []
    for kind, src, land in zip(kinds, res[:n], res[n:]):
        own = lax.dynamic_index_in_dim(src, me, 0, keepdims=False) if kind == "scatter" else src
        outs.append(lax.dynamic_update_index_in_dim(land, own, me, 0) if fill_own else (land, own))
    return outs


def _pick(n, cands):
    for t in cands:
        if n % t == 0:
            return t
    raise ValueError(f"no tile for {n}")


def _mm(name, a, b, mode, out_dtype, after=None):
    if mode == "nn":
        (m, k), (k2, n) = a.shape, b.shape
    elif mode == "nt":
        (m, k), (n, k2) = a.shape, b.shape
    else:
        (k, m), (k2, n) = a.shape, b.shape
    assert k == k2, (name, a.shape, b.shape)
    if mode == "tn":
        tm = _pick(m, (1024, 640, 512, 256, 128))
        tn = _pick(n, (1024, 640, 512, 256, 128))
        tk = _pick(k, (2048, 1024, 512, 256))
    else:
        tm, tn, tk = _pick(m, (512, 256)), n, k
    nk = k // tk

    def body(a_ref, b_ref, *rest):
        o_ref = rest[-2] if nk > 1 else rest[-1]
        av = a_ref[...].astype(BF16)
        bv = b_ref[...].astype(BF16)
        if mode == "nn":
            part = jnp.dot(av, bv, preferred_element_type=F32)
        elif mode == "nt":
            part = lax.dot_general(av, bv, NT_DIMS, preferred_element_type=F32)
        else:
            part = lax.dot_general(av, bv, TN_DIMS, preferred_element_type=F32)
        if nk == 1:
            o_ref[...] = part.astype(out_dtype)
            return
        acc_ref = rest[-1]
        kk = pl.program_id(2)

        @pl.when(kk == 0)
        def _():
            acc_ref[...] = part

        @pl.when(kk > 0)
        def _():
            acc_ref[...] += part

        @pl.when(kk == nk - 1)
        def _():
            o_ref[...] = acc_ref[...].astype(out_dtype)

    if mode == "tn":
        a_spec = pl.BlockSpec((tk, tm), lambda j, i, kk: (kk, i))
    else:
        a_spec = pl.BlockSpec((tm, tk), lambda j, i, kk: (i, kk))
    if mode == "nt":
        b_spec = pl.BlockSpec((tn, tk), lambda j, i, kk: (j, kk))
    else:
        b_spec = pl.BlockSpec((tk, tn), lambda j, i, kk: (kk, j))
    in_specs, args = [a_spec, b_spec], [a, b]
    if after is not None:
        in_specs.append(pl.BlockSpec((8, HD), lambda j, i, kk: (0, 0)))
        args.append(after)
    return pl.pallas_call(
        body,
        name=name,
        grid=(n // tn, m // tm, nk),
        in_specs=in_specs,
        out_specs=pl.BlockSpec((tm, tn), lambda j, i, kk: (i, j)),
        out_shape=jax.ShapeDtypeStruct((m, n), out_dtype),
        scratch_shapes=[pltpu.VMEM((tm, tn), F32)] if nk > 1 else [],
        compiler_params=_params(dimension_semantics=("parallel", "parallel", "arbitrary")),
    )(*args)


def _mm_cat(name, a_list, b, out_dtype, after=None):
    m = a_list[0].shape[0]
    ks = [a.shape[1] for a in a_list]
    n = b.shape[1]
    assert sum(ks) <= b.shape[0], (name, ks, b.shape)
    tm = _pick(m, (512, 256))
    na = len(a_list)

    def body(*refs):
        b_ref, o_ref = refs[na], refs[-1]
        k0, acc = 0, None
        for a_ref, kw in zip(refs[:na], ks):
            part = jnp.dot(a_ref[...].astype(BF16), b_ref[k0:k0 + kw, :], preferred_element_type=F32)
            acc = part if acc is None else acc + part
            k0 += kw
        o_ref[...] = acc.astype(out_dtype)

    in_specs = [pl.BlockSpec((tm, kw), lambda i: (i, 0)) for kw in ks] + [pl.BlockSpec((sum(ks), n), lambda i: (0, 0))]
    args = [*a_list, b]
    if after is not None:
        in_specs.append(pl.BlockSpec((8, HD), lambda i: (0, 0)))
        args.append(after)
    return pl.pallas_call(
        body, name=name, grid=(m // tm,),
        in_specs=in_specs, out_specs=pl.BlockSpec((tm, n), lambda i: (i, 0)),
        out_shape=jax.ShapeDtypeStruct((m, n), out_dtype),
        compiler_params=_params(dimension_semantics=("parallel",)),
    )(*args)


def _row(c, col=0):
    return pl.BlockSpec((TS, c), lambda i: (i, col))


def _vec(r, c):
    return pl.BlockSpec((r, c), lambda i: (0, 0))


def _prenorm_proj(x, pre_gain, w_t, n, after):
    s = x.shape[0]
    tm = 512

    def body(x_ref, g_ref, w_ref, after_ref, xn_ref, z_ref):
        xhat, _ = _rms_fwd(x_ref[...])
        xn = (xhat * g_ref[...]).astype(BF16)
        xn_ref[...] = xn
        z_ref[...] = lax.dot_general(xn, w_ref[...], NT_DIMS, preferred_element_type=F32).astype(BF16)

    return pl.pallas_call(
        body, name="prenorm_proj_qkv", grid=(s // tm,),
        in_specs=[pl.BlockSpec((tm, D), lambda i: (i, 0)), _vec(1, D), _vec(n, D), _vec(8, HD)],
        out_specs=[pl.BlockSpec((tm, D), lambda i: (i, 0)), pl.BlockSpec((tm, n), lambda i: (i, 0))],
        out_shape=[jax.ShapeDtypeStruct((s, D), BF16), jax.ShapeDtypeStruct((s, n), BF16)],
        compiler_params=_params(dimension_semantics=("parallel",)),
    )(x, pre_gain, w_t, after)


def _forget_fwd(zr, bf_pad):
    s = zr.shape[0]
    n = s // TQ

    def body(fl_ref, b_ref, kx_ref, c_buf, carry):
        i = pl.program_id(0)

        @pl.when(i == 0)
        def _():
            carry[...] = jnp.zeros_like(carry)

        fl = fl_ref[...] + b_ref[...]
        ls = jnp.minimum(fl, 0.0) - jnp.log(1.0 + jnp.exp(-jnp.abs(fl)))
        c_buf[...] = _cumsum_fwd(ls) + carry[0:1, :]
        carry[0:1, :] = c_buf[TQ - 1:TQ, :]
        cv = c_buf[...]
        for h in range(NH):
            kx_ref[h] = _bias_lanes(jnp.broadcast_to(cv[:, 8 * h:8 * h + 1], (TQ, HD)) * (-1.0 / SCALE), 0, 3)

    return pl.pallas_call(
        body, name="forget_fwd", grid=(n,),
        in_specs=[pl.BlockSpec((TQ, HD), lambda i: (i, FL_COL // HD)), _vec(1, HD)],
        out_specs=pl.BlockSpec((NH, TQ, HD), lambda i: (0, i, 0)),
        out_shape=jax.ShapeDtypeStruct((NH, s, HD), BF16),
        scratch_shapes=[pltpu.VMEM((TQ, HD), F32), pltpu.VMEM((8, HD), F32)],
        compiler_params=_params(dimension_semantics=("arbitrary",)),
    )(zr, bf_pad)


def _attn_fwd(zq, kx):
    s = zq.shape[0]
    n = s // TQ
    nb = TQ // HD

    def body(q_ref, k_ref, v_ref, kx_ref, o_ref, ax_ref):
        i = pl.program_id(1)
        lane = lax.broadcasted_iota(jnp.int32, (TQ, HD), 1)
        row = lax.broadcasted_iota(jnp.int32, (TQ, HD), 0)
        qa = jnp.concatenate([q_ref[...], jnp.where(lane < 3, 1.0, 0.0).astype(BF16)], axis=1)

        def step(j, carry, masked):
            m, l, acc = carry
            rows = pl.ds(pl.multiple_of(j * TQ, TQ), TQ)
            ka = jnp.concatenate([k_ref[rows, :], kx_ref[0, rows, :]], axis=1)
            v_all = v_ref[rows, :]
            rp = TQ // ROW_PARTS
            parts = [slice(rp * t, rp * (t + 1)) for t in range(ROW_PARTS)]
            keys = [rp * (t + 1) if masked else TQ for t in range(ROW_PARTS)]
            u_parts = [lax.dot_general(qa[part], ka[:kn], NT_DIMS, preferred_element_type=F32)
                       for part, kn in zip(parts, keys)]
            out = []
            for t, (part, u, kn) in enumerate(zip(parts, u_parts, keys)):
                us = [u[:, HD * b:HD * (b + 1)] for b in range(kn // HD)]
                if masked:
                    us = [ub if HD * (b + 1) <= rp * t else jnp.where(row[part] >= lane[part] + HD * b, ub, NEG)
                          for b, ub in enumerate(us)]
                v = v_all[:kn]
                bm = functools.reduce(jnp.maximum, us)
                m_new = jnp.maximum(m[part], jnp.max(bm, axis=1, keepdims=True))
                alpha = jnp.exp2((m[part] - m_new) * EXP2_SCALE)
                shift = m_new * EXP2_SCALE
                ps = [jnp.exp2(ub * EXP2_SCALE - shift) for ub in us]
                l_new = alpha * l[part] + functools.reduce(jnp.add, ps)
                pr = jnp.concatenate(ps, axis=1).astype(BF16)
                out.append((m_new, l_new, alpha * acc[part] + jnp.dot(pr, v, preferred_element_type=F32)))
            return tuple(jnp.concatenate([o[t] for o in out], axis=0) for t in range(3))

        init = (jnp.full((TQ, HD), NEG, F32), jnp.zeros((TQ, HD), F32), jnp.zeros((TQ, HD), F32))
        carry = lax.fori_loop(0, i, lambda j, cr: step(j, cr, False), init)
        m, l, acc = step(i, carry, True)
        l_row = jnp.sum(l, axis=1, keepdims=True)
        o_ref[...] = acc / l_row
        ax_ref[0] = _bias_lanes(-(m + jnp.log(l_row) * (1.0 / SCALE)), 3, 0)

    return pl.pallas_call(
        body, name="attn_fwd", grid=(NH, n),
        in_specs=[
            pl.BlockSpec((TQ, HD), lambda h, i: (i, h)),
            pl.BlockSpec((s, HD), lambda h, i: (0, NH + h)),
            pl.BlockSpec((s, HD), lambda h, i: (0, 2 * NH + h)),
            pl.BlockSpec((1, s, HD), lambda h, i: (h, 0, 0)),
        ],
        out_specs=[pl.BlockSpec((TQ, HD), lambda h, i: (i, h)), pl.BlockSpec((1, TQ, HD), lambda h, i: (h, i, 0))],
        out_shape=[jax.ShapeDtypeStruct((s, D), F32), jax.ShapeDtypeStruct((NH, s, HD), BF16)],
        compiler_params=_params(dimension_semantics=("parallel", "parallel")),
    )(zq, zq, zq, kx)


def _attn_bwd(zq, do, ax, delta, kx, after):
    s = zq.shape[0]
    n = s // TQ
    nb = TQ // HD

    def body(k_ref, v_ref, kx_ref, q_ref, ax_ref, do_ref, dl_ref, after_ref, dq_out, dk_ref, dv_ref, dcs_ref, drs_ref,
             dq_ref):
        j = pl.program_id(1)

        @pl.when(j == 0)
        def _():
            dq_ref[...] = jnp.zeros_like(dq_ref)
            drs_ref[...] = jnp.zeros_like(drs_ref)

        k = k_ref[...]
        v = v_ref[...]
        ka = jnp.concatenate([k, kx_ref[0]], axis=1)
        row = lax.broadcasted_iota(jnp.int32, (TQ, HD), 0)
        lane = lax.broadcasted_iota(jnp.int32, (TQ, HD), 1)

        def step(i, carry, r0, rn, kn, masked):
            dk, dv, dcs = carry
            rows = pl.ds(pl.multiple_of(i * TQ + r0, rn), rn)
            q = q_ref[rows, :]
            dout = do_ref[rows, :]
            dlv = dl_ref[0, rows, :]
            qa = jnp.concatenate([q, ax_ref[0, rows, :]], axis=1)
            u = lax.dot_general(qa, ka[:kn], NT_DIMS, preferred_element_type=F32)
            dp = lax.dot_general(dout, v[:kn], NT_DIMS, preferred_element_type=F32)
            prs, dss = [], []
            for b in range(kn // HD):
                cs = slice(HD * b, HD * (b + 1))
                ub = u[:, cs]
                if masked and HD * (b + 1) > r0:
                    ub = jnp.where(row[:rn] + r0 >= lane[:rn] + HD * b, ub, NEG)
                pb = jnp.exp2(ub * EXP2_SCALE)
                prs.append(pb)
                dss.append(pb * (dp[:, cs] - dlv))
            drs_ref[0, rows, :] += functools.reduce(jnp.add, dss)
            ds = jnp.concatenate(dss, axis=1)
            dsb = ds.astype(BF16)
            dcs_new = jnp.sum(ds.reshape(rn // 8, 8, kn), axis=0)
            dv_new = lax.dot_general(jnp.concatenate(prs, axis=1).astype(BF16), dout, TN_DIMS, preferred_element_type=F32)
            dk_new = lax.dot_general(dsb, q, TN_DIMS, preferred_element_type=F32)
            if kn < TQ:
                dcs_new = jnp.concatenate([dcs_new, jnp.zeros((8, TQ - kn), F32)], axis=1)
                dv_new = jnp.concatenate([dv_new, jnp.zeros((TQ - kn, HD), F32)], axis=0)
                dk_new = jnp.concatenate([dk_new, jnp.zeros((TQ - kn, HD), F32)], axis=0)
            dq_ref[rows, :] += jnp.dot(dsb, k[:kn], preferred_element_type=F32) * SCALE
            return dk + dk_new, dv + dv_new, dcs + dcs_new

        carry = (jnp.zeros((TQ, HD), F32), jnp.zeros((TQ, HD), F32), jnp.zeros((8, TQ), F32))
        rp = TQ // ROW_PARTS
        for t in range(ROW_PARTS):
            carry = step(j, carry, rp * t, rp, rp * (t + 1), True)
        dk, dv, dcs = lax.fori_loop(j + 1, n, lambda i, cr: step(i, cr, 0, TQ, TQ, False), carry)
        dk_ref[...] = (dk * SCALE).astype(BF16)
        dv_ref[...] = dv.astype(BF16)
        dcs_ref[0] = jnp.broadcast_to(_colsum(dcs), (8, TQ))

        @pl.when(j == n - 1)
        def _():
            dq_out[...] = dq_ref[...].astype(BF16)

    return pl.pallas_call(
        body, name="attn_bwd", grid=(NH, n),
        in_specs=[
            pl.BlockSpec((TQ, HD), lambda h, j: (j, NH + h)),
            pl.BlockSpec((TQ, HD), lambda h, j: (j, 2 * NH + h)),
            pl.BlockSpec((1, TQ, HD), lambda h, j: (h, j, 0)),
            pl.BlockSpec((s, HD), lambda h, j: (0, h)),
            pl.BlockSpec((1, s, HD), lambda h, j: (h, 0, 0)),
            pl.BlockSpec((s, HD), lambda h, j: (0, h)),
            pl.BlockSpec((1, s, HD), lambda h, j: (h, 0, 0)),
            pl.BlockSpec((8, HD), lambda h, j: (0, 0)),
        ],
        out_specs=[
            pl.BlockSpec((s, HD), lambda h, j: (0, h)),
            pl.BlockSpec((TQ, HD), lambda h, j: (j, h)),
            pl.BlockSpec((TQ, HD), lambda h, j: (j, h)),
            pl.BlockSpec((1, 8, TQ), lambda h, j: (j, h, 0)),
            pl.BlockSpec((1, s, HD), lambda h, j: (h, 0, 0)),
        ],
        out_shape=[
            jax.ShapeDtypeStruct((s, D), BF16),
            jax.ShapeDtypeStruct((s, D), BF16),
            jax.ShapeDtypeStruct((s, D), BF16),
            jax.ShapeDtypeStruct((n, 8 * NH, TQ), F32),
            jax.ShapeDtypeStruct((NH, s, HD), F32),
        ],
        scratch_shapes=[pltpu.VMEM((s, HD), F32)],
        compiler_params=_params(dimension_semantics=("parallel", "arbitrary")),
    )(zq, zq, kx, zq, ax, do, delta, after)


def _forget_bwd(dcs, drs, zr, bf_pad):
    n = dcs.shape[0]
    s = n * TQ

    def body(dcs_ref, drs_ref, fl_ref, b_ref, dfl_ref, gb_ref, buf, carry):
        i = pl.program_id(0)

        @pl.when(i == 0)
        def _():
            carry[...] = jnp.zeros_like(carry)
            gb_ref[...] = jnp.zeros_like(gb_ref)

        dc_t = jnp.concatenate([dcs_ref[0], jnp.zeros((HD - 8 * NH, TQ), F32)], axis=0)
        lane = lax.broadcasted_iota(jnp.int32, (TQ, HD), 1)
        dc = -dc_t.T
        for hh in range(NH):
            dc = dc + jnp.where(lane == 8 * hh, jnp.sum(drs_ref[hh], axis=1, keepdims=True), 0.0)
        buf[...] = _cumsum_bwd(dc) + carry[0:1, :]
        carry[0:1, :] = buf[0:1, :]
        fl = fl_ref[...] + b_ref[...]
        dfl = buf[...] * _sigmoid_rel(-fl)
        dfl_ref[...] = dfl.astype(BF16)
        gb_ref[...] += _colsum(dfl)

    return pl.pallas_call(
        body, name="forget_bwd", grid=(n,),
        in_specs=[
            pl.BlockSpec((1, 8 * NH, TQ), lambda i: (n - 1 - i, 0, 0)),
            pl.BlockSpec((NH, TQ, HD), lambda i: (0, n - 1 - i, 0)),
            pl.BlockSpec((TQ, HD), lambda i: (n - 1 - i, FL_COL // HD)),
            _vec(1, HD),
        ],
        out_specs=[pl.BlockSpec((TQ, HD), lambda i: (n - 1 - i, 0)), _vec(1, HD)],
        out_shape=[jax.ShapeDtypeStruct((s, HD), BF16), jax.ShapeDtypeStruct((1, HD), F32)],
        scratch_shapes=[pltpu.VMEM((TQ, HD), F32), pltpu.VMEM((8, HD), F32)],
        compiler_params=_params(dimension_semantics=("arbitrary",)),
    )(dcs, drs, zr, bf_pad)


def _gates(xc, w_ref, b, sigmoid):
    xb = xc.astype(BF16)
    pre = jnp.concatenate(
        [jnp.dot(xb[:, HD * g:HD * (g + 1)], w_ref[g], preferred_element_type=F32) for g in range(NH)], axis=1)
    return sigmoid(pre + b)


def _lru_coeffs(r, lam):
    sp = jnp.maximum(-lam, 0.0) + jnp.log(1.0 + jnp.exp(-jnp.abs(lam)))
    log_a = -LRU_C * r * sp
    a = jnp.exp(log_a)
    y = 2.0 * log_a
    em1 = jnp.where(jnp.abs(y) < 0.01, y * (1.0 + y * (0.5 + y * (1.0 / 6.0))), jnp.exp(y) - 1.0)
    em = -em1
    inv_gam = lax.rsqrt(jnp.maximum(em, 1e-37))
    return sp, a, em * inv_gam, inv_gam


def _conv_taps(ext, t):
    return [_shift_down(ext, CONV_W - 1 - jj, t) for jj in range(CONV_W)]


def _lru_fwd(zr, conv_w8, conv_b, w_r, b_r, w_i, b_i, lam):
    s = zr.shape[0]
    n = s // TS
    xl_col = 1

    def body(xl_ref, halo_ref, cw_ref, cb_ref, wr_ref, br_ref, wi_ref, bi_ref, lam_ref, xc_ref, h_ref, carry):
        i = pl.program_id(0)

        @pl.when(i == 0)
        def _():
            carry[...] = jnp.zeros_like(carry)

        halo = jnp.where(i == 0, 0.0, halo_ref[...])
        taps = _conv_taps(jnp.concatenate([halo, xl_ref[...]], axis=0), TS)
        xc = cb_ref[...] + sum(cw_ref[jj:jj + 1, :] * taps[jj] for jj in range(CONV_W))
        xc_ref[...] = xc
        r = _gates(xc, wr_ref, br_ref[...], _sigmoid_rel)
        ig = _gates(xc, wi_ref, bi_ref[...], _sigmoid)
        _, a, gam, _ = _lru_coeffs(r, lam_ref[...])
        a_cum, h_loc = _scan(a, gam * (ig * xc), False)
        h_ref[...] = h_loc + a_cum * carry[0:1, :]
        carry[0:1, :] = h_ref[TS - 1:TS, :]

    return pl.pallas_call(
        body, name="lru_fwd", grid=(n,),
        in_specs=[
            _row(D, xl_col),
            pl.BlockSpec((8, D), lambda i: (jnp.maximum(i * (TS // 8) - 1, 0), xl_col)),
            _vec(8, D), _vec(1, D),
            pl.BlockSpec((NH, HD, HD), lambda i: (0, 0, 0)), _vec(1, D),
            pl.BlockSpec((NH, HD, HD), lambda i: (0, 0, 0)), _vec(1, D),
            _vec(1, D),
        ],
        out_specs=[_row(D), _row(D)],
        out_shape=[jax.ShapeDtypeStruct((s, D), F32), jax.ShapeDtypeStruct((s, D), F32)],
        scratch_shapes=[pltpu.VMEM((8, D), F32)],
        compiler_params=_params(dimension_semantics=("arbitrary",)),
    )(zr, zr, conv_w8, conv_b, w_r, b_r, w_i, b_i, lam)


def _lru_bwd(zr, xc, h, dh, conv_w8, w_r, b_r, w_i, b_i, lam):
    s = zr.shape[0]
    n = s // TS
    xl_col = 1

    def rev(i):
        return n - 1 - i

    def body(xl_ref, xlh_ref, xc_ref, h_ref, hh_ref, dh_ref, cw_ref, wr_ref, br_ref, wi_ref, bi_ref, lam_ref,
             dxl_ref, gwr_ref, gwi_ref, gbr_ref, gbi_ref, glam_ref, gcb_ref, gcw_ref, l_buf, dxc_buf, carry_g, carry_dxc):
        i = pl.program_id(0)
        first = rev(i) == 0

        @pl.when(i == 0)
        def _():
            carry_g[...] = jnp.zeros_like(carry_g)
            carry_dxc[...] = jnp.zeros_like(carry_dxc)
            for ref in (gwr_ref, gwi_ref, gbr_ref, gbi_ref, glam_ref, gcb_ref, gcw_ref):
                ref[...] = jnp.zeros_like(ref)

        rows = _rows_iota(TS)
        xc = xc_ref[...]
        lam = lam_ref[...]
        r = _gates(xc, wr_ref, br_ref[...], _sigmoid_rel)
        ig = _gates(xc, wi_ref, bi_ref[...], _sigmoid)
        sp, a, gam, inv_gam = _lru_coeffs(r, lam)
        g = dh_ref[...] + jnp.where(rows == TS - 1, carry_g[0:1, :], 0.0)
        b = jnp.where(rows == TS - 1, 0.0, pltpu.roll(a, TS - 1, 0))
        l_buf[...] = _scan(b, g, True)[1]
        lv = l_buf[...]
        carry_g[0:1, :] = l_buf[0:1, :] * a[0:1, :]
        h_prev_row = jnp.where(first, 0.0, hh_ref[7:8, :])
        h_prev = jnp.where(rows == 0, h_prev_row, pltpu.roll(h_ref[...], 1, 0))
        dgam = lv * ig * xc
        dig = lv * gam * xc
        dxc = lv * gam * ig
        dla = lv * h_prev * a - dgam * (a * a) * inv_gam
        dr = dla * (-LRU_C) * sp
        glam_ref[...] += _colsum(dla * r) * (LRU_C * _sigmoid_rel(-lam))
        dpr = dr * r * (1.0 - r)
        dpi = dig * ig * (1.0 - ig)
        gbr_ref[...] += _colsum(dpr)
        gbi_ref[...] += _colsum(dpi)
        xb = xc.astype(BF16)
        dprb = dpr.astype(BF16)
        dpib = dpi.astype(BF16)
        back = []
        for gi in range(NH):
            cs = slice(HD * gi, HD * (gi + 1))
            gwr_ref[gi] += lax.dot_general(xb[:, cs], dprb[:, cs], TN_DIMS, preferred_element_type=F32)
            gwi_ref[gi] += lax.dot_general(xb[:, cs], dpib[:, cs], TN_DIMS, preferred_element_type=F32)
            back.append(lax.dot_general(dprb[:, cs], wr_ref[gi], NT_DIMS, preferred_element_type=F32)
                        + lax.dot_general(dpib[:, cs], wi_ref[gi], NT_DIMS, preferred_element_type=F32))
        dxc = dxc + jnp.concatenate(back, axis=1)
        dxc_buf[...] = dxc
        gcb_ref[...] += _colsum(dxc)
        halo = jnp.where(first, 0.0, xlh_ref[...])
        taps = _conv_taps(jnp.concatenate([halo, xl_ref[...]], axis=0), TS)
        for jj in range(CONV_W):
            gcw_ref[jj:jj + 1, :] += _colsum(dxc * taps[jj])
        ext = jnp.concatenate([dxc, carry_dxc[...]], axis=0)
        dxl = sum(cw_ref[jj:jj + 1, :] * _shift_up(ext, CONV_W - 1 - jj, TS) for jj in range(CONV_W))
        dxl_ref[...] = dxl.astype(BF16)
        carry_dxc[...] = dxc_buf[0:8, :]

    rowr = lambda c, col=0: pl.BlockSpec((TS, c), lambda i: (rev(i), col))
    halo = lambda col: pl.BlockSpec((8, D), lambda i: (jnp.maximum(rev(i) * (TS // 8) - 1, 0), col))
    gate_w = pl.BlockSpec((NH, HD, HD), lambda i: (0, 0, 0))
    return pl.pallas_call(
        body, name="lru_bwd", grid=(n,),
        in_specs=[rowr(D, xl_col), halo(xl_col), rowr(D), rowr(D), halo(0), rowr(D),
                  _vec(8, D), gate_w, _vec(1, D), gate_w, _vec(1, D), _vec(1, D)],
        out_specs=[rowr(D), gate_w, gate_w, _vec(1, D), _vec(1, D), _vec(1, D), _vec(1, D), _vec(8, D)],
        out_shape=[
            jax.ShapeDtypeStruct((s, D), BF16),
            jax.ShapeDtypeStruct((NH, HD, HD), F32), jax.ShapeDtypeStruct((NH, HD, HD), F32),
            jax.ShapeDtypeStruct((1, D), F32), jax.ShapeDtypeStruct((1, D), F32), jax.ShapeDtypeStruct((1, D), F32),
            jax.ShapeDtypeStruct((1, D), F32), jax.ShapeDtypeStruct((8, D), F32),
        ],
        scratch_shapes=[pltpu.VMEM((TS, D), F32), pltpu.VMEM((TS, D), F32), pltpu.VMEM((8, D), F32), pltpu.VMEM((8, D), F32)],
        compiler_params=_params(dimension_semantics=("arbitrary",)),
    )(zr, zr, xc, h, h, dh, conv_w8, w_r, b_r, w_i, b_i, lam)


def _silu_parts(g):
    sg = _sigmoid(g)
    return g * sg, sg * (1.0 + g * (1.0 - sg))


def _branch_out(o, h, zr, gain_a, gain_l):
    s = o.shape[0]

    def body(o_ref, ga_ref, h_ref, gl_ref, ka_ref, kl_ref, y_ref):
        ohat, _ = _rms_fwd(o_ref[...])
        y_ref[:, 0:D] = (ohat * ka_ref[...] * _silu_parts(ga_ref[...])[0]).astype(BF16)
        hhat, _ = _rms_fwd(h_ref[...])
        y_ref[:, D:2 * D] = (hhat * kl_ref[...] * _silu_parts(gl_ref[...])[0]).astype(BF16)

    return pl.pallas_call(
        body, name="branch_out", grid=(s // TS,),
        in_specs=[_row(D), _row(D, 0), _row(D), _row(D, 2), _vec(1, D), _vec(1, D)],
        out_specs=_row(2 * D),
        out_shape=jax.ShapeDtypeStruct((s, 2 * D), BF16),
        compiler_params=_params(dimension_semantics=("parallel",)),
    )(o, zr, h, zr, gain_a, gain_l)


def _branch_out_bwd(o, h, zr, dmix, w_out, gain_a, gain_l):
    s = o.shape[0]

    def body(o_ref, ga_ref, h_ref, gl_ref, dm_ref, w_ref, ka_ref, kl_ref,
             do_ref, dl_ref, dga_ref, dh_ref, dgl_ref, gka_ref, gkl_ref):
        @pl.when(pl.program_id(0) == 0)
        def _():
            gka_ref[...] = jnp.zeros_like(gka_ref)
            gkl_ref[...] = jnp.zeros_like(gkl_ref)

        dycat = lax.dot_general(dm_ref[...], w_ref[...], NT_DIMS, preferred_element_type=F32)

        def one(v, g, dy, gain):
            vhat, rstd = _rms_fwd(v)
            sg, dsg = _silu_parts(g)
            dn = dy * sg
            dg = dy * (vhat * gain) * dsg
            return _rms_bwd(vhat, rstd, dn * gain), dg, _colsum(dn * vhat)

        o = o_ref[...]
        dout, dga, gka = one(o, ga_ref[...], dycat[:, :D], ka_ref[...])
        do_ref[...] = dout.astype(BF16)
        dga_ref[...] = dga.astype(BF16)
        gka_ref[...] += gka
        prod = dout * o
        for hh in range(NH):
            dl_ref[hh] = jnp.broadcast_to(jnp.sum(prod[:, HD * hh:HD * (hh + 1)], axis=1, keepdims=True), (TS, HD))
        dh, dgl, gkl = one(h_ref[...], gl_ref[...], dycat[:, D:], kl_ref[...])
        dh_ref[...] = dh
        dgl_ref[...] = dgl.astype(BF16)
        gkl_ref[...] += gkl

    return pl.pallas_call(
        body, name="branch_out_bwd", grid=(s // TS,),
        in_specs=[_row(D), _row(D, 0), _row(D), _row(D, 2), _row(D), _vec(2 * D, D), _vec(1, D), _vec(1, D)],
        out_specs=[_row(D), pl.BlockSpec((NH, TS, HD), lambda i: (0, i, 0)), _row(D), _row(D), _row(D), _vec(1, D), _vec(1, D)],
        out_shape=[
            jax.ShapeDtypeStruct((s, D), BF16), jax.ShapeDtypeStruct((NH, s, HD), F32), jax.ShapeDtypeStruct((s, D), BF16),
            jax.ShapeDtypeStruct((s, D), F32), jax.ShapeDtypeStruct((s, D), BF16),
            jax.ShapeDtypeStruct((1, D), F32), jax.ShapeDtypeStruct((1, D), F32),
        ],
        compiler_params=_params(dimension_semantics=("arbitrary",)),
    )(o, zr, h, zr, dmix, w_out, gain_a, gain_l)


def _residual(x, ycat, w_out, post_gain):
    s = x.shape[0]

    def body(x_ref, y_ref, w_ref, g_ref, m_ref, h_ref, hb_ref):
        mix = jnp.dot(y_ref[...], w_ref[...], preferred_element_type=F32)
        m_ref[...] = mix
        mhat, _ = _rms_fwd(mix)
        h1 = x_ref[...] + mhat * g_ref[...]
        h_ref[...] = h1
        hb_ref[...] = h1.astype(BF16)

    return pl.pallas_call(
        body, name="residual", grid=(s // TS,),
        in_specs=[_row(D), _row(2 * D), _vec(2 * D, D), _vec(1, D)], out_specs=[_row(D), _row(D), _row(D)],
        out_shape=[jax.ShapeDtypeStruct((s, D), F32), jax.ShapeDtypeStruct((s, D), F32), jax.ShapeDtypeStruct((s, D), BF16)],
        compiler_params=_params(dimension_semantics=("parallel",)),
    )(x, ycat, w_out, post_gain)


def _head(h1, p, tgt, mix, w_gate, w_ple, ple_gain, b_gate, post_gain):
    s = h1.shape[0]

    def body(h_ref, p_ref, t_ref, m_ref, wg_ref, wp_ref, kg_ref, b_ref, pg_ref,
             loss_ref, dgp_ref, dpe_ref, dh_ref, dm_ref, gk_ref, gb_ref, gg_ref):
        @pl.when(pl.program_id(0) == 0)
        def _():
            for ref in (loss_ref, gk_ref, gb_ref, gg_ref):
                ref[...] = jnp.zeros_like(ref)

        h1 = h_ref[...]
        pe = jnp.dot(p_ref[...].astype(BF16), wp_ref[...], preferred_element_type=F32)
        gp = jnp.dot(h1.astype(BF16), wg_ref[...], preferred_element_type=F32)
        ehat, rstd = _rms_fwd(pe)
        e = ehat * kg_ref[...]
        gate = _sigmoid(gp + b_ref[...])
        diff = (h1 + gate * e) - t_ref[...]
        per_row = jnp.mean(diff * diff, axis=-1, keepdims=True)
        loss_ref[...] += 0.5 * jnp.sum(per_row, axis=0, keepdims=True)
        dy = diff * (1.0 / D)
        dgp = dy * e * gate * (1.0 - gate)
        dgpb = dgp.astype(BF16)
        dgp_ref[...] = dgpb
        gb_ref[...] += _colsum(dgp)
        de = dy * gate
        gk_ref[...] += _colsum(de * ehat)
        dpe_ref[...] = _rms_bwd(ehat, rstd, de * kg_ref[...]).astype(BF16)
        dh1 = dy + lax.dot_general(dgpb, wg_ref[...], NT_DIMS, preferred_element_type=F32)
        dh_ref[...] = dh1
        mhat, rstd_m = _rms_fwd(m_ref[...])
        gg_ref[...] += _colsum(dh1 * mhat)
        dm_ref[...] = _rms_bwd(mhat, rstd_m, dh1 * pg_ref[...]).astype(BF16)

    return pl.pallas_call(
        body, name="head", grid=(s // TS,),
        in_specs=[_row(D), _row(D_PLE), _row(D), _row(D), _vec(D, D), _vec(D_PLE, D), _vec(1, D), _vec(1, D), _vec(1, D)],
        out_specs=[_vec(1, 1), _row(D), _row(D), _row(D), _row(D), _vec(1, D), _vec(1, D), _vec(1, D)],
        out_shape=[
            jax.ShapeDtypeStruct((1, 1), F32), jax.ShapeDtypeStruct((s, D), BF16), jax.ShapeDtypeStruct((s, D), BF16),
            jax.ShapeDtypeStruct((s, D), F32), jax.ShapeDtypeStruct((s, D), BF16),
            jax.ShapeDtypeStruct((1, D), F32), jax.ShapeDtypeStruct((1, D), F32), jax.ShapeDtypeStruct((1, D), F32),
        ],
        compiler_params=_params(dimension_semantics=("arbitrary",)),
    )(h1, p, tgt, mix, w_gate, w_ple, ple_gain, b_gate, post_gain)


def _prenorm_bwd(x, dxn_a, dxn_b, dh1, pre_gain):
    s = x.shape[0]

    def body(x_ref, da_ref, db_ref, dh_ref, g_ref, dx_ref, gg_ref):
        @pl.when(pl.program_id(0) == 0)
        def _():
            gg_ref[...] = jnp.zeros_like(gg_ref)

        xhat, rstd = _rms_fwd(x_ref[...])
        dxn = da_ref[...] + db_ref[...]
        gg_ref[...] += _colsum(dxn * xhat)
        dx_ref[...] = dh_ref[...] + _rms_bwd(xhat, rstd, dxn * g_ref[...])

    return pl.pallas_call(
        body, name="prenorm_bwd", grid=(s // TS,),
        in_specs=[_row(D), _row(D), _row(D), _row(D), _vec(1, D)], out_specs=[_row(D), _vec(1, D)],
        out_shape=[jax.ShapeDtypeStruct((s, D), F32), jax.ShapeDtypeStruct((1, D), F32)],
        compiler_params=_params(dimension_semantics=("arbitrary",)),
    )(x, dxn_a, dxn_b, dh1, pre_gain)


def _adamw(name, parts, w, m, v, own=None, me=None):
    r, c = w.shape
    if r % 8 == 0:
        tr = _pick(r, (256, 128, 16, 8))
        grid = (r // tr,)
        blk = pl.BlockSpec((tr, c), lambda i: (i, 0))
        parts_blk = pl.BlockSpec((N_DEV, tr, c), lambda i: (0, i, 0))
    else:
        tc = _pick(c, (256, 128))
        grid = (c // tc,)
        blk = pl.BlockSpec((r, tc), lambda i: (0, i))
        parts_blk = pl.BlockSpec((N_DEV, r, tc), lambda i: (0, 0, i))

    def body(*refs):
        p_ref, w_ref, m_ref, v_ref = refs[:4]
        g_ref, d_ref, nm_ref, nv_ref = refs[-4:]
        if own is None:
            g = p_ref[0].astype(F32)
            for j in range(1, N_DEV):
                g = g + p_ref[j].astype(F32)
            g_ref[...] = g
        else:
            own_ref, me_ref = refs[4:6]
            g_ref[...] = jnp.zeros_like(g_ref)
            for j in range(N_DEV):
                @pl.when(me_ref[0] == j)
                def _():
                    g_ref[...] += own_ref[...].astype(F32)

                @pl.when(me_ref[0] != j)
                def _():
                    g_ref[...] += p_ref[j].astype(F32)
            g = g_ref[...]
        nm = ADAM_B1 * m_ref[...] + (1.0 - ADAM_B1) * g
        nv = ADAM_B2 * v_ref[...] + (1.0 - ADAM_B2) * (g * g)
        nm_ref[...] = nm
        nv_ref[...] = nv
        m_hat = nm / (1.0 - ADAM_B1 ** ADAM_STEP)
        v_hat = nv / (1.0 - ADAM_B2 ** ADAM_STEP)
        d_ref[...] = -ADAM_LR * (m_hat / (jnp.sqrt(v_hat) + ADAM_EPS) + ADAM_WD * w_ref[...])

    in_specs, args = [parts_blk, blk, blk, blk], [parts, w, m, v]
    if own is not None:
        in_specs += [blk, pl.BlockSpec(memory_space=pltpu.SMEM)]
        args += [own, me]
    return pl.pallas_call(
        body, name=name, grid=grid,
        in_specs=in_specs,
        out_specs=[blk] * 4,
        out_shape=[jax.ShapeDtypeStruct((r, c), F32)] * 4,
        compiler_params=_params(dimension_semantics=("parallel",)),
    )(*args)


def _spread8(v):
    r = v.shape[0]
    return jnp.pad(jnp.pad(v[:, :, None], ((0, 0), (0, 0), (0, 7))).reshape(r, 8 * NH), ((0, 0), (0, HD - 8 * NH)))


def _gather8(v):
    return v[:, :8 * NH].reshape(v.shape[0], NH, 8)[:, :, 0]


def _cols_to_shards(g):
    r, c8 = g.shape
    return g.reshape(r, N_DEV, c8 // N_DEV).transpose(1, 0, 2)


def _shards_to_cols(g):
    n, r, c = g.shape
    return g.transpose(1, 0, 2).reshape(r, n * c)


def kernel(x, p, w_in, b_f, pre_gain, post_gain, conv_w, conv_b, w_rgate, b_rgate, w_igate, b_igate, lru_lambda, attn_out_gain, lru_out_gain, w_out, w_ple, ple_gain, w_ple_gate, b_ple_gate, loss_target, m_w_in, m_b_f, m_pre_gain, m_post_gain, m_conv_w, m_conv_b, m_w_rgate, m_b_rgate, m_w_igate, m_b_igate, m_lru_lambda, m_attn_out_gain, m_lru_out_gain, m_w_out, m_w_ple, m_ple_gain, m_w_ple_gate, m_b_ple_gate, v_w_in, v_b_f, v_pre_gain, v_post_gain, v_conv_w, v_conv_b, v_w_rgate, v_b_rgate, v_w_igate, v_b_igate, v_lru_lambda, v_attn_out_gain, v_lru_out_gain, v_w_out, v_w_ple, v_ple_gain, v_w_ple_gate, v_b_ple_gate):
    me = 4 * lax.axis_index("x") + 2 * lax.axis_index("y") + lax.axis_index("c")
    x2, p2, tgt = x[0], p[0, 0], loss_target[0]

    conv_w_shard8 = jnp.pad(conv_w[0], ((0, 8 - CONV_W), (0, 0)))
    wt, m_wt, v_wt = w_in[0].T, m_w_in[0].T, v_w_in[0].T
    g_wint, g_conv = _gather_two_level("gather_w_in", [wt.astype(BF16), conv_w_shard8])
    win_t = g_wint.reshape(D_IN, D)
    rest_state, rest_token = _exchange_start(
        "gather_rest_start", [w_out[0].astype(BF16), w_ple[0].astype(BF16), w_ple_gate[0].astype(BF16)], ["bcast"] * 3,
        after=g_conv)
    w_rest_t = jnp.concatenate([win_t[D_QKV + NH:], _spread8(win_t[D_QKV:D_QKV + NH].T).T], axis=0)
    conv_w8 = _shards_to_cols(g_conv)
    bf_pad = _spread8(b_f)
    w_r, w_i = w_rgate[0].astype(BF16), w_igate[0].astype(BF16)

    xn, zq = _prenorm_proj(x2, pre_gain, win_t, D_QKV, rest_token)
    zr = _mm("proj_rest", xn, w_rest_t, "nt", F32)
    kx = _forget_fwd(zr, bf_pad)
    o, ax = _attn_fwd(zq, kx)
    xc, h = _lru_fwd(zr, conv_w8, conv_b, w_r, b_rgate, w_i, b_igate, lru_lambda)
    ycat = _branch_out(o, h, zr, attn_out_gain, lru_out_gain)
    g_wout, g_wple, g_wpg = _exchange_wait("gather_rest_wait", rest_state, ycat)
    wout_full = g_wout.reshape(2 * D, D)
    wple_full = _shards_to_cols(g_wple)
    wpg_full = g_wpg.reshape(D, D)
    mix, h1, h1b = _residual(x2, ycat, wout_full, post_gain)

    loss_part, dgp, dpe, dh1, dmix, g_ple_gain, g_b_gate, g_post_gain = _head(
        h1, p2, tgt, mix, wpg_full, wple_full, ple_gain, b_ple_gate, post_gain)
    gw_pg = _mm("bwd_gate_w", h1b, dgp, "tn", BF16)
    gw_ple = _mm("bwd_ple_w", p2, dpe, "tn", BF16)
    gw_out = _mm("bwd_out_w", ycat, dmix, "tn", BF16)
    do, delta, dga, dh, dgl, g_aog, g_log = _branch_out_bwd(o, h, zr, dmix, wout_full, attn_out_gain, lru_out_gain)
    dxl, g_wr, g_wi, g_br, g_bi, g_lam, g_cb, g_cw8 = _lru_bwd(
        zr, xc, h, dh, conv_w8, w_r, b_rgate, w_i, b_igate, lru_lambda)
    gates = jnp.concatenate([g_wr.reshape(D, HD), g_wi.reshape(D, HD)], axis=0).astype(BF16)
    outw_state, outw_token = _exchange_start(
        "exchange_outw_start",
        [gw_out.reshape(N_DEV, 2 * D // N_DEV, D), _cols_to_shards(gw_ple), gw_pg.reshape(N_DEV, D // N_DEV, D), gates],
        ["scatter"] * 3 + ["bcast"])
    dq, dk, dv, dcs, drs = _attn_bwd(zq, do, ax, delta, kx, outw_token)
    dfl, g_bf_pad = _forget_bwd(dcs, drs, zr, bf_pad)
    gw_pieces = [_mm("bwd_w_" + nm, dz, xn, "tn", BF16) for nm, dz in
                 (("q", dq), ("k", dk), ("v", dv), ("fl", dfl), ("ga", dga), ("xl", dxl), ("gl", dgl))]
    gw_pieces[3] = _gather8(gw_pieces[3].T).T
    gw_in_t = jnp.concatenate(gw_pieces, axis=0)
    inw_state, inw_token = _exchange_start(
        "exchange_inw_start", [gw_in_t.reshape(N_DEV, D_IN_SHARD, D)], ["scatter"])
    dxn_a = _mm_cat("bwd_qkv_x", [dq, dk, dv], win_t, F32, after=inw_token)
    dxn_b = _mm_cat("bwd_rest_x", [dga, dxl, dgl, dfl], w_rest_t, F32, after=inw_token)
    grad_x, g_pre_gain = _prenorm_bwd(x2, dxn_a, dxn_b, dh1, pre_gain)

    upd = {}
    me1 = me.reshape(1).astype(jnp.int32)
    r_wout, r_wple, r_wpg, r_gates = _exchange_wait("exchange_outw_wait", outw_state, grad_x, fill_own=False)
    upd["w_out"] = _adamw("adamw_w_out", r_wout[0], w_out[0], m_w_out[0], v_w_out[0], r_wout[1], me1)
    upd["w_ple"] = _adamw("adamw_w_ple", r_wple[0], w_ple[0], m_w_ple[0], v_w_ple[0], r_wple[1], me1)
    upd["w_ple_gate"] = _adamw("adamw_w_ple_gate", r_wpg[0], w_ple_gate[0], m_w_ple_gate[0], v_w_ple_gate[0], r_wpg[1], me1)
    gates_of = lambda a, b: jnp.concatenate([a[0].reshape(D, HD), b[0].reshape(D, HD)], axis=0)
    g_gates = _adamw("adamw_gates", r_gates[0], gates_of(w_rgate, w_igate), gates_of(m_w_rgate, m_w_igate),
                     gates_of(v_w_rgate, v_w_igate), r_gates[1], me1)
    upd["w_rgate"] = [a[:D].reshape(1, NH, HD, HD) for a in g_gates]
    upd["w_igate"] = [a[D:].reshape(1, NH, HD, HD) for a in g_gates]
    behind = upd["w_out"][0][0:1] + upd["w_ple_gate"][0][0:1] + jnp.pad(g_gates[0][0:1], ((0, 0), (0, D - HD)))
    small = jnp.concatenate(
        [jnp.pad(_gather8(g_bf_pad), ((0, 0), (0, D - NH))), g_pre_gain, g_post_gain, g_cb, g_br, g_bi, g_lam, g_aog, g_log,
         g_ple_gain, g_b_gate, g_cw8[:CONV_W], behind, jnp.pad(loss_part, ((0, 7), (0, D - 1)))], axis=0)
    (r_small,) = _exchange("exchange_small", [small], ["bcast"])
    vec_names = ["b_f", "pre_gain", "post_gain", "conv_b", "b_rgate", "b_igate", "lru_lambda", "attn_out_gain",
                 "lru_out_gain", "ple_gain", "b_ple_gate"]
    vec_w = dict(b_f=(b_f, m_b_f, v_b_f), pre_gain=(pre_gain, m_pre_gain, v_pre_gain),
                 post_gain=(post_gain, m_post_gain, v_post_gain), conv_b=(conv_b, m_conv_b, v_conv_b),
                 b_rgate=(b_rgate, m_b_rgate, v_b_rgate), b_igate=(b_igate, m_b_igate, v_b_igate),
                 lru_lambda=(lru_lambda, m_lru_lambda, v_lru_lambda),
                 attn_out_gain=(attn_out_gain, m_attn_out_gain, v_attn_out_gain),
                 lru_out_gain=(lru_out_gain, m_lru_out_gain, v_lru_out_gain), ple_gain=(ple_gain, m_ple_gain, v_ple_gain),
                 b_ple_gate=(b_ple_gate, m_b_ple_gate, v_b_ple_gate))
    conv_mine = lambda a: lax.dynamic_slice_in_dim(a, me * HD, HD, axis=1)

    def small_rows(k):
        rows = [jnp.pad(vec_w[nm][k], ((0, 0), (0, D - vec_w[nm][k].shape[1]))) for nm in vec_names]
        cw = (conv_w, m_conv_w, v_conv_w)[k][0]
        full = lax.dynamic_update_slice_in_dim(jnp.ones((CONV_W, D), F32), cw, me * HD, axis=1)
        return jnp.concatenate(rows + [full, jnp.ones((9, D), F32)], axis=0)

    g_small = _adamw("adamw_small", r_small, small_rows(0), small_rows(1), small_rows(2))
    loss = g_small[0][16, 0]
    for idx, nm in enumerate(vec_names):
        width = vec_w[nm][0].shape[1]
        upd[nm] = [a[idx:idx + 1, :width] for a in g_small]
    base = len(vec_names)
    upd["conv_w"] = [conv_mine(a[base:base + CONV_W])[None] for a in g_small]
    (r_win,) = _exchange_wait("exchange_inw_wait", inw_state, g_small[0], fill_own=False)
    upd["w_in"] = [a.T for a in _adamw("adamw_w_in", r_win[0], wt, m_wt, v_wt, r_win[1], me1)]
    for nm in ("w_in", "w_out", "w_ple", "w_ple_gate"):
        upd[nm] = [a[None] for a in upd[nm]]

    order = ["w_in", "b_f", "pre_gain", "post_gain", "conv_w", "conv_b", "w_rgate", "b_rgate", "w_igate", "b_igate",
             "lru_lambda", "attn_out_gain", "lru_out_gain", "w_out", "w_ple", "ple_gain", "w_ple_gate", "b_ple_gate"]
    outs = [loss, grad_x[None]]
    for k in range(4):
        outs += [upd[nm][k] for nm in order]
    return tuple(outs)
```

```python
import functools

import jax
import jax.numpy as jnp
from jax import lax
from jax.experimental import pallas as pl
from jax.experimental.pallas import tpu as pltpu

F32 = jnp.float32
BF16 = jnp.bfloat16

N_DEV = 8
D = 1024
HD = 128
NH = 8
D_IN = 6152
D_IN_SHARD = D_IN // N_DEV
D_QKV = 3 * D
D_REST = 3 * D + HD
FL_COL = 3 * D
D_PLE = 256
CONV_W = 4
LRU_C = 8.0
RMS_EPS = 1e-6
SCALE = HD ** -0.5
EXP2_SCALE = SCALE * 1.4426950408889634
NEG = -1e30

ADAM_LR = 0.001
ADAM_B1 = 0.9
ADAM_B2 = 0.999
ADAM_EPS = 1e-08
ADAM_WD = 0.01
ADAM_STEP = 10

TS = 256
TQ = 1024
ROW_PARTS = 2
VMEM_LIMIT = 48 * 1024 * 1024

NT_DIMS = (((1,), (1,)), ((), ()))
TN_DIMS = (((0,), (0,)), ((), ()))


def _params(**kw):
    return pltpu.CompilerParams(vmem_limit_bytes=VMEM_LIMIT, **kw)


def _sigmoid(v):
    return 0.5 * jnp.tanh(0.5 * v) + 0.5


def _sigmoid_rel(v):
    return 1.0 / (1.0 + jnp.exp(-v))


def _rms_fwd(v):
    rstd = lax.rsqrt(jnp.mean(v * v, axis=-1, keepdims=True) + RMS_EPS)
    return v * rstd, rstd


def _rms_bwd(vhat, rstd, dvhat):
    return rstd * (dvhat - vhat * jnp.mean(dvhat * vhat, axis=-1, keepdims=True))


def _colsum(v):
    return jnp.sum(v, axis=0, keepdims=True)


def _rows_iota(t):
    return lax.broadcasted_iota(jnp.int32, (t, 1), 0)


def _scan(a, u, reverse):
    t, c = a.shape
    rows = _rows_iota(t)
    d = 1
    while d < t:
        if d < 8:
            valid = rows < t - d if reverse else rows >= d
            shift = t - d if reverse else d
            u = jnp.where(valid, u + a * pltpu.roll(u, shift, 0), u)
            a = jnp.where(valid, a * pltpu.roll(a, shift, 0), a)
        else:
            zeros, ones = jnp.zeros((d, c), F32), jnp.ones((d, c), F32)
            if reverse:
                u_far, a_far = jnp.concatenate([u[d:], zeros], axis=0), jnp.concatenate([a[d:], ones], axis=0)
            else:
                u_far, a_far = jnp.concatenate([zeros, u[:t - d]], axis=0), jnp.concatenate([ones, a[:t - d]], axis=0)
            u = u + a * u_far
            a = a * a_far
        d *= 2
    return a, u


def _cumsum_fwd(v):
    t = v.shape[0]
    rows = _rows_iota(t)
    d = 1
    while d < t:
        v = jnp.where(rows >= d, v + pltpu.roll(v, d, 0), v)
        d *= 2
    return v


def _cumsum_bwd(v):
    t = v.shape[0]
    rows = _rows_iota(t)
    d = 1
    while d < t:
        v = jnp.where(rows < t - d, v + pltpu.roll(v, t - d, 0), v)
        d *= 2
    return v


def _bias_lanes(v, at, ones_at):
    lane = lax.broadcasted_iota(jnp.int32, v.shape, 1)
    hi = v.astype(BF16).astype(F32)
    mid = (v - hi).astype(BF16).astype(F32)
    lo = ((v - hi) - mid).astype(BF16).astype(F32)
    out = jnp.where((lane >= ones_at) & (lane < ones_at + 3), 1.0, 0.0)
    for k, piece in enumerate((hi, mid, lo)):
        out = jnp.where(lane == at + k, piece, out)
    return out.astype(BF16)


def _shift_down(ext, k, t):
    return pltpu.roll(ext, k, 0)[8:, :] if k else ext[8:, :]


def _shift_up(ext, k, t):
    return pltpu.roll(ext, t + 8 - k, 0)[:t, :] if k else ext[:t, :]


def _exchange(name, arrs, kinds):
    n = len(arrs)
    out_shape = []
    for a, kind in zip(arrs, kinds):
        shp = a.shape if kind == "scatter" else (N_DEV,) + a.shape
        out_shape.append(jax.ShapeDtypeStruct(shp, a.dtype))

    def body(*refs):
        ins, outs = refs[:n], refs[n:2 * n]
        send_sems, recv_sems, local_sems = refs[2 * n:]
        x, y, c = lax.axis_index("x"), lax.axis_index("y"), lax.axis_index("c")
        me = 4 * x + 2 * y + c
        copies = []
        for i in range(n):
            scatter = kinds[i] == "scatter"
            mine = pltpu.make_async_copy(ins[i].at[me] if scatter else ins[i], outs[i].at[me], local_sems.at[i])
            mine.start()
            copies.append(mine)
            for m in range(1, N_DEV):
                px = 1 - x if m & 4 else x
                py = 1 - y if m & 2 else y
                pc = 1 - c if m & 1 else c
                peer = 4 * px + 2 * py + pc
                cp = pltpu.make_async_remote_copy(
                    src_ref=ins[i].at[peer] if scatter else ins[i],
                    dst_ref=outs[i].at[me],
                    send_sem=send_sems.at[i, m - 1],
                    recv_sem=recv_sems.at[i, m - 1],
                    device_id=(px, py, pc),
                    device_id_type=pl.DeviceIdType.MESH,
                )
                cp.start()
                copies.append(cp)
        for cp in copies:
            cp.wait()

    any_spec = pl.BlockSpec(memory_space=pl.ANY)
    return pl.pallas_call(
        body,
        name=name,
        out_shape=out_shape,
        in_specs=[any_spec] * n,
        out_specs=[any_spec] * n,
        scratch_shapes=[
            pltpu.SemaphoreType.DMA((n, N_DEV - 1)),
            pltpu.SemaphoreType.DMA((n, N_DEV - 1)),
            pltpu.SemaphoreType.DMA((n,)),
        ],
        compiler_params=pltpu.CompilerParams(has_side_effects=True),
    )(*arrs)


def _gather_two_level(name, arrs, pieces=1):
    n = len(arrs)
    items = []
    for i, a in enumerate(arrs):
        rows = a.shape[0]
        if pieces > 1 and rows >= 512:
            step = -(-rows // (16 * pieces)) * 16
            items += [(i, r0, min(step, rows - r0)) for r0 in range(0, rows, step)]
        else:
            items.append((i, 0, rows))
    n_items = len(items)

    def body(*refs):
        ins, outs = refs[:n], refs[n:2 * n]
        send_sems, recv_sems, local_sems = refs[2 * n:]
        x, y, c = lax.axis_index("x"), lax.axis_index("y"), lax.axis_index("c")
        me, sibling = (x, y, c), (x, y, 1 - c)
        chips = [(1 - x, y), (x, 1 - y), (1 - x, 1 - y)]

        def rows_of(ref, t):
            i, r0, rn = items[t]
            return ref if rn == arrs[i].shape[0] else ref.at[pl.ds(r0, rn)]

        def slot(t, dev):
            return rows_of(outs[items[t][0]].at[4 * dev[0] + 2 * dev[1] + dev[2]], t)

        def copy(t, k, block, to, from_input=False):
            return pltpu.make_async_remote_copy(
                src_ref=rows_of(ins[items[t][0]], t) if from_input else slot(t, block), dst_ref=slot(t, block),
                send_sem=send_sems.at[t, k], recv_sem=recv_sems.at[t, k],
                device_id=to, device_id_type=pl.DeviceIdType.MESH)

        own, sent = [], []
        for t in range(n_items):
            mine = pltpu.make_async_copy(rows_of(ins[items[t][0]], t), slot(t, me), local_sems.at[t])
            mine.start()
            own.append(mine)
            first = [copy(t, 1 + j, me, (*chip, c), from_input=True) for j, chip in enumerate(chips)]
            first.append(copy(t, 0, me, sibling, from_input=True))
            for cp in first:
                cp.start()
            sent += first
        for t in range(n_items):
            for j, chip in enumerate(chips):
                copy(t, 1 + j, (*chip, c), me).wait_recv()
                fwd = copy(t, 4 + j, (*chip, c), sibling)
                fwd.start()
                sent.append(fwd)
        for t in range(n_items):
            copy(t, 0, sibling, me).wait_recv()
            for j, chip in enumerate(chips):
                copy(t, 4 + j, (*chip, 1 - c), me).wait_recv()
        for cp in sent:
            cp.wait_send()
        for cp in own:
            cp.wait()

    any_spec = pl.BlockSpec(memory_space=pl.ANY)
    return pl.pallas_call(
        body, name=name,
        out_shape=[jax.ShapeDtypeStruct((N_DEV,) + a.shape, a.dtype) for a in arrs],
        in_specs=[any_spec] * n, out_specs=[any_spec] * n,
        scratch_shapes=[pltpu.SemaphoreType.DMA((n_items, 7)), pltpu.SemaphoreType.DMA((n_items, 7)),
                        pltpu.SemaphoreType.DMA((n_items,))],
        compiler_params=pltpu.CompilerParams(has_side_effects=True),
    )(*arrs)


def _peers(x, y, c):
    out = []
    for m in range(1, N_DEV):
        px = 1 - x if m & 4 else x
        py = 1 - y if m & 2 else y
        pc = 1 - c if m & 1 else c
        out.append((m, (px, py, pc), 4 * px + 2 * py + pc))
    return out


def _split_copies(kinds, src_refs, land_refs, send_sems, recv_sems):
    x, y, c = lax.axis_index("x"), lax.axis_index("y"), lax.axis_index("c")
    me = 4 * x + 2 * y + c
    copies = []
    for i, kind in enumerate(kinds):
        for m, peer, pidx in _peers(x, y, c):
            copies.append(pltpu.make_async_remote_copy(
                src_ref=src_refs[i].at[pidx] if kind == "scatter" else src_refs[i],
                dst_ref=land_refs[i].at[me],
                send_sem=send_sems.at[i * (N_DEV - 1) + m - 1],
                recv_sem=recv_sems.at[i * (N_DEV - 1) + m - 1],
                device_id=peer,
                device_id_type=pl.DeviceIdType.MESH,
            ))
    return copies


_HBM_SPEC = pl.BlockSpec(memory_space=pltpu.HBM)
_SEM_SPEC = pl.BlockSpec(memory_space=pltpu.SEMAPHORE)
_DATAFLOW = pltpu.SideEffectType.DATAFLOW_SIDE_EFFECTING


def _exchange_start(name, arrs, kinds, after=None):
    n = len(arrs)
    extra = [] if after is None else [after]
    lands = []
    for a, kind in zip(arrs, kinds):
        shp = a.shape if kind == "scatter" else (N_DEV,) + a.shape
        lands.append(lax.empty(shp, a.dtype))

    def body(*refs):
        src_refs, land_refs = refs[:n], refs[n:2 * n]
        send_sems, recv_sems = refs[2 * n + len(extra):2 * n + len(extra) + 2]
        token = refs[-1]
        for cp in _split_copies(kinds, src_refs, land_refs, send_sems, recv_sems):
            cp.start()
        token[...] = jnp.zeros_like(token)

    n_sem = n * (N_DEV - 1)
    hbm = lambda a: pltpu.HBM(a.shape, a.dtype)
    res = pl.pallas_call(
        body, name=name,
        out_shape=(pltpu.SemaphoreType.DMA((n_sem,)), pltpu.SemaphoreType.DMA((n_sem,)),
                   *[hbm(a) for a in arrs], *[hbm(a) for a in lands], jax.ShapeDtypeStruct((8, HD), F32)),
        in_specs=[_HBM_SPEC] * (2 * n) + [pl.BlockSpec(memory_space=pl.ANY)] * len(extra),
        out_specs=(_SEM_SPEC, _SEM_SPEC, *[_HBM_SPEC] * (2 * n), pl.BlockSpec(memory_space=pltpu.VMEM)),
        input_output_aliases={i: 2 + i for i in range(2 * n)},
        compiler_params=pltpu.CompilerParams(has_side_effects=_DATAFLOW),
    )(*[pltpu.with_memory_space_constraint(a, pltpu.HBM) for a in arrs],
      *[pltpu.with_memory_space_constraint(a, pltpu.HBM) for a in lands], *extra)
    return (kinds, res[0], res[1], res[2:2 + n], res[2 + n:2 + 2 * n]), res[-1]


def _exchange_wait(name, state, after, fill_own=True):
    kinds, send_sems, recv_sems, srcs, lands = state
    n = len(srcs)

    def body(*refs):
        src_refs, land_refs = refs[:n], refs[n:2 * n]
        send_sems_ref, recv_sems_ref = refs[2 * n:2 * n + 2]
        for cp in _split_copies(kinds, src_refs, land_refs, send_sems_ref, recv_sems_ref):
            cp.wait_send()
            cp.wait_recv()

    res = pl.pallas_call(
        body, name=name,
        out_shape=tuple(pltpu.HBM(a.shape, a.dtype) for a in (*srcs, *lands)),
        in_specs=[_HBM_SPEC] * (2 * n) + [_SEM_SPEC, _SEM_SPEC, pl.BlockSpec(memory_space=pl.ANY)],
        out_specs=tuple([_HBM_SPEC] * (2 * n)),
        input_output_aliases={i: i for i in range(2 * n)},
        compiler_params=pltpu.CompilerParams(has_side_effects=_DATAFLOW),
    )(*srcs, *lands, send_sems, recv_sems, after)
    me = 4 * lax.axis_index("x") + 2 * lax.axis_index("y") + lax.axis_index("c")
    outs = []
    for kind, src, land in zip(kinds, res[:n], res[n:]):
        own = lax.dynamic_index_in_dim(src, me, 0, keepdims=False) if kind == "scatter" else src
        outs.append(lax.dynamic_update_index_in_dim(land, own, me, 0) if fill_own else (land, own))
    return outs


def _pick(n, cands):
    for t in cands:
        if n % t == 0:
            return t
    raise ValueError(f"no tile for {n}")


def _mm(name, a, b, mode, out_dtype, after=None):
    if mode == "nn":
        (m, k), (k2, n) = a.shape, b.shape
    elif mode == "nt":
        (m, k), (n, k2) = a.shape, b.shape
    else:
        (k, m), (k2, n) = a.shape, b.shape
    assert k == k2, (name, a.shape, b.shape)
    if mode == "tn":
        tm = _pick(m, (1024, 640, 512, 256, 128))
        tn = _pick(n, (1024, 640, 512, 256, 128))
        tk = _pick(k, (2048, 1024, 512, 256))
    else:
        tm, tn, tk = _pick(m, (512, 256)), n, k
    nk = k // tk

    def body(a_ref, b_ref, *rest):
        o_ref = rest[-2] if nk > 1 else rest[-1]
        av = a_ref[...].astype(BF16)
        bv = b_ref[...].astype(BF16)
        if mode == "nn":
            part = jnp.dot(av, bv, preferred_element_type=F32)
        elif mode == "nt":
            part = lax.dot_general(av, bv, NT_DIMS, preferred_element_type=F32)
        else:
            part = lax.dot_general(av, bv, TN_DIMS, preferred_element_type=F32)
        if nk == 1:
            o_ref[...] = part.astype(out_dtype)
            return
        acc_ref = rest[-1]
        kk = pl.program_id(2)

        @pl.when(kk == 0)
        def _():
            acc_ref[...] = part

        @pl.when(kk > 0)
        def _():
            acc_ref[...] += part

        @pl.when(kk == nk - 1)
        def _():
            o_ref[...] = acc_ref[...].astype(out_dtype)

    if mode == "tn":
        a_spec = pl.BlockSpec((tk, tm), lambda j, i, kk: (kk, i))
    else:
        a_spec = pl.BlockSpec((tm, tk), lambda j, i, kk: (i, kk))
    if mode == "nt":
        b_spec = pl.BlockSpec((tn, tk), lambda j, i, kk: (j, kk))
    else:
        b_spec = pl.BlockSpec((tk, tn), lambda j, i, kk: (kk, j))
    in_specs, args = [a_spec, b_spec], [a, b]
    if after is not None:
        in_specs.append(pl.BlockSpec((8, HD), lambda j, i, kk: (0, 0)))
        args.append(after)
    return pl.pallas_call(
        body,
        name=name,
        grid=(n // tn, m // tm, nk),
        in_specs=in_specs,
        out_specs=pl.BlockSpec((tm, tn), lambda j, i, kk: (i, j)),
        out_shape=jax.ShapeDtypeStruct((m, n), out_dtype),
        scratch_shapes=[pltpu.VMEM((tm, tn), F32)] if nk > 1 else [],
        compiler_params=_params(dimension_semantics=("parallel", "parallel", "arbitrary")),
    )(*args)


def _mm_cat(name, a_list, b, out_dtype, after=None):
    m = a_list[0].shape[0]
    ks = [a.shape[1] for a in a_list]
    n = b.shape[1]
    assert sum(ks) <= b.shape[0], (name, ks, b.shape)
    tm = _pick(m, (512, 256))
    na = len(a_list)

    def body(*refs):
        b_ref, o_ref = refs[na], refs[-1]
        k0, acc = 0, None
        for a_ref, kw in zip(refs[:na], ks):
            part = jnp.dot(a_ref[...].astype(BF16), b_ref[k0:k0 + kw, :], preferred_element_type=F32)
            acc = part if acc is None else acc + part
            k0 += kw
        o_ref[...] = acc.astype(out_dtype)

    in_specs = [pl.BlockSpec((tm, kw), lambda i: (i, 0)) for kw in ks] + [pl.BlockSpec((sum(ks), n), lambda i: (0, 0))]
    args = [*a_list, b]
    if after is not None:
        in_specs.append(pl.BlockSpec((8, HD), lambda i: (0, 0)))
        args.append(after)
    return pl.pallas_call(
        body, name=name, grid=(m // tm,),
        in_specs=in_specs, out_specs=pl.BlockSpec((tm, n), lambda i: (i, 0)),
        out_shape=jax.ShapeDtypeStruct((m, n), out_dtype),
        compiler_params=_params(dimension_semantics=("parallel",)),
    )(*args)


def _row(c, col=0):
    return pl.BlockSpec((TS, c), lambda i: (i, col))


def _vec(r, c):
    return pl.BlockSpec((r, c), lambda i: (0, 0))


def _prenorm_proj(x, pre_gain, w_t, n, after):
    s = x.shape[0]
    tm = 512

    def body(x_ref, g_ref, w_ref, after_ref, xn_ref, z_ref):
        xhat, _ = _rms_fwd(x_ref[...])
        xn = (xhat * g_ref[...]).astype(BF16)
        xn_ref[...] = xn
        z_ref[...] = lax.dot_general(xn, w_ref[...], NT_DIMS, preferred_element_type=F32).astype(BF16)

    return pl.pallas_call(
        body, name="prenorm_proj_qkv", grid=(s // tm,),
        in_specs=[pl.BlockSpec((tm, D), lambda i: (i, 0)), _vec(1, D), _vec(n, D), _vec(8, HD)],
        out_specs=[pl.BlockSpec((tm, D), lambda i: (i, 0)), pl.BlockSpec((tm, n), lambda i: (i, 0))],
        out_shape=[jax.ShapeDtypeStruct((s, D), BF16), jax.ShapeDtypeStruct((s, n), BF16)],
        compiler_params=_params(dimension_semantics=("parallel",)),
    )(x, pre_gain, w_t, after)


def _proj_rest(xn, w_rest_t, bf_pad):
    s = xn.shape[0]
    tm = 512

    def body(x_ref, w_ref, b_ref, z_ref, kx_ref, c_buf, carry):
        @pl.when(pl.program_id(0) == 0)
        def _():
            carry[...] = jnp.zeros_like(carry)

        z = lax.dot_general(x_ref[...], w_ref[...], NT_DIMS, preferred_element_type=F32)
        z_ref[...] = z
        fl = z[:, FL_COL:] + b_ref[...]
        ls = jnp.minimum(fl, 0.0) - jnp.log(1.0 + jnp.exp(-jnp.abs(fl)))
        c_buf[...] = _cumsum_fwd(ls) + carry[0:1, :]
        carry[0:1, :] = c_buf[tm - 1:tm, :]
        cv = c_buf[...]
        for h in range(NH):
            kx_ref[h] = _bias_lanes(jnp.broadcast_to(cv[:, 8 * h:8 * h + 1], (tm, HD)) * (-1.0 / SCALE), 0, 3)

    return pl.pallas_call(
        body, name="proj_rest", grid=(s // tm,),
        in_specs=[pl.BlockSpec((tm, D), lambda i: (i, 0)), _vec(D_REST, D), _vec(1, HD)],
        out_specs=[pl.BlockSpec((tm, D_REST), lambda i: (i, 0)), pl.BlockSpec((NH, tm, HD), lambda i: (0, i, 0))],
        out_shape=[jax.ShapeDtypeStruct((s, D_REST), F32), jax.ShapeDtypeStruct((NH, s, HD), BF16)],
        scratch_shapes=[pltpu.VMEM((tm, HD), F32), pltpu.VMEM((8, HD), F32)],
        compiler_params=_params(dimension_semantics=("arbitrary",)),
    )(xn, w_rest_t, bf_pad)


def _attn_fwd(zq, kx):
    s = zq.shape[0]
    n = s // TQ
    nb = TQ // HD

    def body(q_ref, k_ref, v_ref, kx_ref, o_ref, ax_ref):
        i = pl.program_id(1)
        lane = lax.broadcasted_iota(jnp.int32, (TQ, HD), 1)
        row = lax.broadcasted_iota(jnp.int32, (TQ, HD), 0)
        qa = jnp.concatenate([q_ref[...], jnp.where(lane < 3, 1.0, 0.0).astype(BF16)], axis=1)

        def step(j, carry, masked):
            m, l, acc = carry
            rows = pl.ds(pl.multiple_of(j * TQ, TQ), TQ)
            ka = jnp.concatenate([k_ref[rows, :], kx_ref[0, rows, :]], axis=1)
            v_all = v_ref[rows, :]
            rp = TQ // ROW_PARTS
            parts = [slice(rp * t, rp * (t + 1)) for t in range(ROW_PARTS)]
            keys = [rp * (t + 1) if masked else TQ for t in range(ROW_PARTS)]
            u_parts = [lax.dot_general(qa[part], ka[:kn], NT_DIMS, preferred_element_type=F32)
                       for part, kn in zip(parts, keys)]
            out = []
            for t, (part, u, kn) in enumerate(zip(parts, u_parts, keys)):
                us = [u[:, HD * b:HD * (b + 1)] for b in range(kn // HD)]
                if masked:
                    us = [ub if HD * (b + 1) <= rp * t else jnp.where(row[part] >= lane[part] + HD * b, ub, NEG)
                          for b, ub in enumerate(us)]
                v = v_all[:kn]
                bm = functools.reduce(jnp.maximum, us)
                m_new = jnp.maximum(m[part], jnp.max(bm, axis=1, keepdims=True))
                alpha = jnp.exp2((m[part] - m_new) * EXP2_SCALE)
                shift = m_new * EXP2_SCALE
                ps = [jnp.exp2(ub * EXP2_SCALE - shift) for ub in us]
                l_new = alpha * l[part] + functools.reduce(jnp.add, ps)
                pr = jnp.concatenate(ps, axis=1).astype(BF16)
                out.append((m_new, l_new, alpha * acc[part] + jnp.dot(pr, v, preferred_element_type=F32)))
            return tuple(jnp.concatenate([o[t] for o in out], axis=0) for t in range(3))

        init = (jnp.full((TQ, HD), NEG, F32), jnp.zeros((TQ, HD), F32), jnp.zeros((TQ, HD), F32))
        carry = lax.fori_loop(0, i, lambda j, cr: step(j, cr, False), init)
        m, l, acc = step(i, carry, True)
        l_row = jnp.sum(l, axis=1, keepdims=True)
        o_ref[...] = acc / l_row
        ax_ref[0] = _bias_lanes(-(m + jnp.log(l_row) * (1.0 / SCALE)), 3, 0)

    return pl.pallas_call(
        body, name="attn_fwd", grid=(NH, n),
        in_specs=[
            pl.BlockSpec((TQ, HD), lambda h, i: (i, h)),
            pl.BlockSpec((s, HD), lambda h, i: (0, NH + h)),
            pl.BlockSpec((s, HD), lambda h, i: (0, 2 * NH + h)),
            pl.BlockSpec((1, s, HD), lambda h, i: (h, 0, 0)),
        ],
        out_specs=[pl.BlockSpec((TQ, HD), lambda h, i: (i, h)), pl.BlockSpec((1, TQ, HD), lambda h, i: (h, i, 0))],
        out_shape=[jax.ShapeDtypeStruct((s, D), F32), jax.ShapeDtypeStruct((NH, s, HD), BF16)],
        compiler_params=_params(dimension_semantics=("parallel", "parallel")),
    )(zq, zq, zq, kx)


def _attn_bwd(zq, do, ax, delta, kx, after):
    s = zq.shape[0]
    n = s // TQ
    nb = TQ // HD

    def body(k_ref, v_ref, kx_ref, q_ref, ax_ref, do_ref, dl_ref, after_ref, dq_out, dk_ref, dv_ref, dcs_ref, drs_ref,
             dq_ref):
        j = pl.program_id(1)

        @pl.when(j == 0)
        def _():
            dq_ref[...] = jnp.zeros_like(dq_ref)
            drs_ref[...] = jnp.zeros_like(drs_ref)

        k = k_ref[...]
        v = v_ref[...]
        ka = jnp.concatenate([k, kx_ref[0]], axis=1)
        row = lax.broadcasted_iota(jnp.int32, (TQ, HD), 0)
        lane = lax.broadcasted_iota(jnp.int32, (TQ, HD), 1)

        def step(i, carry, r0, rn, kn, masked):
            dk, dv, dcs = carry
            rows = pl.ds(pl.multiple_of(i * TQ + r0, rn), rn)
            q = q_ref[rows, :]
            dout = do_ref[rows, :]
            dlv = dl_ref[0, rows, :]
            qa = jnp.concatenate([q, ax_ref[0, rows, :]], axis=1)
            u = lax.dot_general(qa, ka[:kn], NT_DIMS, preferred_element_type=F32)
            dp = lax.dot_general(dout, v[:kn], NT_DIMS, preferred_element_type=F32)
            prs, dss = [], []
            for b in range(kn // HD):
                cs = slice(HD * b, HD * (b + 1))
                ub = u[:, cs]
                if masked and HD * (b + 1) > r0:
                    ub = jnp.where(row[:rn] + r0 >= lane[:rn] + HD * b, ub, NEG)
                pb = jnp.exp2(ub * EXP2_SCALE)
                prs.append(pb)
                dss.append(pb * (dp[:, cs] - dlv))
            drs_ref[0, rows, :] += functools.reduce(jnp.add, dss)
            ds = jnp.concatenate(dss, axis=1)
            dsb = ds.astype(BF16)
            dcs_new = jnp.sum(ds.reshape(rn // 8, 8, kn), axis=0)
            dv_new = lax.dot_general(jnp.concatenate(prs, axis=1).astype(BF16), dout, TN_DIMS, preferred_element_type=F32)
            dk_new = lax.dot_general(dsb, q, TN_DIMS, preferred_element_type=F32)
            if kn < TQ:
                dcs_new = jnp.concatenate([dcs_new, jnp.zeros((8, TQ - kn), F32)], axis=1)
                dv_new = jnp.concatenate([dv_new, jnp.zeros((TQ - kn, HD), F32)], axis=0)
                dk_new = jnp.concatenate([dk_new, jnp.zeros((TQ - kn, HD), F32)], axis=0)
            dq_ref[rows, :] += jnp.dot(dsb, k[:kn], preferred_element_type=F32) * SCALE
            return dk + dk_new, dv + dv_new, dcs + dcs_new

        carry = (jnp.zeros((TQ, HD), F32), jnp.zeros((TQ, HD), F32), jnp.zeros((8, TQ), F32))
        rp = TQ // ROW_PARTS
        for t in range(ROW_PARTS):
            carry = step(j, carry, rp * t, rp, rp * (t + 1), True)
        dk, dv, dcs = lax.fori_loop(j + 1, n, lambda i, cr: step(i, cr, 0, TQ, TQ, False), carry)
        dk_ref[...] = (dk * SCALE).astype(BF16)
        dv_ref[...] = dv.astype(BF16)
        dcs_ref[0] = jnp.broadcast_to(_colsum(dcs), (8, TQ))

        @pl.when(j == n - 1)
        def _():
            dq_out[...] = dq_ref[...].astype(BF16)

    return pl.pallas_call(
        body, name="attn_bwd", grid=(NH, n),
        in_specs=[
            pl.BlockSpec((TQ, HD), lambda h, j: (j, NH + h)),
            pl.BlockSpec((TQ, HD), lambda h, j: (j, 2 * NH + h)),
            pl.BlockSpec((1, TQ, HD), lambda h, j: (h, j, 0)),
            pl.BlockSpec((s, HD), lambda h, j: (0, h)),
            pl.BlockSpec((1, s, HD), lambda h, j: (h, 0, 0)),
            pl.BlockSpec((s, HD), lambda h, j: (0, h)),
            pl.BlockSpec((1, s, HD), lambda h, j: (h, 0, 0)),
            pl.BlockSpec((8, HD), lambda h, j: (0, 0)),
        ],
        out_specs=[
            pl.BlockSpec((s, HD), lambda h, j: (0, h)),
            pl.BlockSpec((TQ, HD), lambda h, j: (j, h)),
            pl.BlockSpec((TQ, HD), lambda h, j: (j, h)),
            pl.BlockSpec((1, 8, TQ), lambda h, j: (j, h, 0)),
            pl.BlockSpec((1, s, HD), lambda h, j: (h, 0, 0)),
        ],
        out_shape=[
            jax.ShapeDtypeStruct((s, D), BF16),
            jax.ShapeDtypeStruct((s, D), BF16),
            jax.ShapeDtypeStruct((s, D), BF16),
            jax.ShapeDtypeStruct((n, 8 * NH, TQ), F32),
            jax.ShapeDtypeStruct((NH, s, HD), F32),
        ],
        scratch_shapes=[pltpu.VMEM((s, HD), F32)],
        compiler_params=_params(dimension_semantics=("parallel", "arbitrary")),
    )(zq, zq, kx, zq, ax, do, delta, after)


def _forget_bwd(dcs, drs, zr, bf_pad):
    n = dcs.shape[0]
    s = n * TQ

    def body(dcs_ref, drs_ref, fl_ref, b_ref, dfl_ref, gb_ref, buf, carry):
        i = pl.program_id(0)

        @pl.when(i == 0)
        def _():
            carry[...] = jnp.zeros_like(carry)
            gb_ref[...] = jnp.zeros_like(gb_ref)

        dc_t = jnp.concatenate([dcs_ref[0], jnp.zeros((HD - 8 * NH, TQ), F32)], axis=0)
        lane = lax.broadcasted_iota(jnp.int32, (TQ, HD), 1)
        dc = -dc_t.T
        for hh in range(NH):
            dc = dc + jnp.where(lane == 8 * hh, jnp.sum(drs_ref[hh], axis=1, keepdims=True), 0.0)
        buf[...] = _cumsum_bwd(dc) + carry[0:1, :]
        carry[0:1, :] = buf[0:1, :]
        fl = fl_ref[...] + b_ref[...]
        dfl = buf[...] * _sigmoid_rel(-fl)
        dfl_ref[...] = dfl.astype(BF16)
        gb_ref[...] += _colsum(dfl)

    return pl.pallas_call(
        body, name="forget_bwd", grid=(n,),
        in_specs=[
            pl.BlockSpec((1, 8 * NH, TQ), lambda i: (n - 1 - i, 0, 0)),
            pl.BlockSpec((NH, TQ, HD), lambda i: (0, n - 1 - i, 0)),
            pl.BlockSpec((TQ, HD), lambda i: (n - 1 - i, FL_COL // HD)),
            _vec(1, HD),
        ],
        out_specs=[pl.BlockSpec((TQ, HD), lambda i: (n - 1 - i, 0)), _vec(1, HD)],
        out_shape=[jax.ShapeDtypeStruct((s, HD), BF16), jax.ShapeDtypeStruct((1, HD), F32)],
        scratch_shapes=[pltpu.VMEM((TQ, HD), F32), pltpu.VMEM((8, HD), F32)],
        compiler_params=_params(dimension_semantics=("arbitrary",)),
    )(dcs, drs, zr, bf_pad)


def _gates(xc, w_ref, b, sigmoid):
    xb = xc.astype(BF16)
    pre = jnp.concatenate(
        [jnp.dot(xb[:, HD * g:HD * (g + 1)], w_ref[g], preferred_element_type=F32) for g in range(NH)], axis=1)
    return sigmoid(pre + b)


def _lru_coeffs(r, lam):
    sp = jnp.maximum(-lam, 0.0) + jnp.log(1.0 + jnp.exp(-jnp.abs(lam)))
    log_a = -LRU_C * r * sp
    a = jnp.exp(log_a)
    y = 2.0 * log_a
    em1 = jnp.where(jnp.abs(y) < 0.01, y * (1.0 + y * (0.5 + y * (1.0 / 6.0))), jnp.exp(y) - 1.0)
    em = -em1
    inv_gam = lax.rsqrt(jnp.maximum(em, 1e-37))
    return sp, a, em * inv_gam, inv_gam


def _conv_taps(ext, t):
    return [_shift_down(ext, CONV_W - 1 - jj, t) for jj in range(CONV_W)]


def _lru_fwd(zr, conv_w8, conv_b, w_r, b_r, w_i, b_i, lam):
    s = zr.shape[0]
    n = s // TS
    xl_col = 1

    def body(xl_ref, halo_ref, cw_ref, cb_ref, wr_ref, br_ref, wi_ref, bi_ref, lam_ref, xc_ref, h_ref, carry):
        i = pl.program_id(0)

        @pl.when(i == 0)
        def _():
            carry[...] = jnp.zeros_like(carry)

        halo = jnp.where(i == 0, 0.0, halo_ref[...])
        taps = _conv_taps(jnp.concatenate([halo, xl_ref[...]], axis=0), TS)
        xc = cb_ref[...] + sum(cw_ref[jj:jj + 1, :] * taps[jj] for jj in range(CONV_W))
        xc_ref[...] = xc
        r = _gates(xc, wr_ref, br_ref[...], _sigmoid_rel)
        ig = _gates(xc, wi_ref, bi_ref[...], _sigmoid)
        _, a, gam, _ = _lru_coeffs(r, lam_ref[...])
        a_cum, h_loc = _scan(a, gam * (ig * xc), False)
        h_ref[...] = h_loc + a_cum * carry[0:1, :]
        carry[0:1, :] = h_ref[TS - 1:TS, :]

    return pl.pallas_call(
        body, name="lru_fwd", grid=(n,),
        in_specs=[
            _row(D, xl_col),
            pl.BlockSpec((8, D), lambda i: (jnp.maximum(i * (TS // 8) - 1, 0), xl_col)),
            _vec(8, D), _vec(1, D),
            pl.BlockSpec((NH, HD, HD), lambda i: (0, 0, 0)), _vec(1, D),
            pl.BlockSpec((NH, HD, HD), lambda i: (0, 0, 0)), _vec(1, D),
            _vec(1, D),
        ],
        out_specs=[_row(D), _row(D)],
        out_shape=[jax.ShapeDtypeStruct((s, D), F32), jax.ShapeDtypeStruct((s, D), F32)],
        scratch_shapes=[pltpu.VMEM((8, D), F32)],
        compiler_params=_params(dimension_semantics=("arbitrary",)),
    )(zr, zr, conv_w8, conv_b, w_r, b_r, w_i, b_i, lam)


def _lru_bwd(zr, xc, h, dh, conv_w8, w_r, b_r, w_i, b_i, lam):
    s = zr.shape[0]
    n = s // TS
    xl_col = 1

    def rev(i):
        return n - 1 - i

    def body(xl_ref, xlh_ref, xc_ref, h_ref, hh_ref, dh_ref, cw_ref, wr_ref, br_ref, wi_ref, bi_ref, lam_ref,
             dxl_ref, gwr_ref, gwi_ref, gbr_ref, gbi_ref, glam_ref, gcb_ref, gcw_ref, l_buf, dxc_buf, carry_g, carry_dxc):
        i = pl.program_id(0)
        first = rev(i) == 0

        @pl.when(i == 0)
        def _():
            carry_g[...] = jnp.zeros_like(carry_g)
            carry_dxc[...] = jnp.zeros_like(carry_dxc)
            for ref in (gwr_ref, gwi_ref, gbr_ref, gbi_ref, glam_ref, gcb_ref, gcw_ref):
                ref[...] = jnp.zeros_like(ref)

        rows = _rows_iota(TS)
        xc = xc_ref[...]
        lam = lam_ref[...]
        r = _gates(xc, wr_ref, br_ref[...], _sigmoid_rel)
        ig = _gates(xc, wi_ref, bi_ref[...], _sigmoid)
        sp, a, gam, inv_gam = _lru_coeffs(r, lam)
        g = dh_ref[...] + jnp.where(rows == TS - 1, carry_g[0:1, :], 0.0)
        b = jnp.where(rows == TS - 1, 0.0, pltpu.roll(a, TS - 1, 0))
        l_buf[...] = _scan(b, g, True)[1]
        lv = l_buf[...]
        carry_g[0:1, :] = l_buf[0:1, :] * a[0:1, :]
        h_prev_row = jnp.where(first, 0.0, hh_ref[7:8, :])
        h_prev = jnp.where(rows == 0, h_prev_row, pltpu.roll(h_ref[...], 1, 0))
        dgam = lv * ig * xc
        dig = lv * gam * xc
        dxc = lv * gam * ig
        dla = lv * h_prev * a - dgam * (a * a) * inv_gam
        dr = dla * (-LRU_C) * sp
        glam_ref[...] += _colsum(dla * r) * (LRU_C * _sigmoid_rel(-lam))
        dpr = dr * r * (1.0 - r)
        dpi = dig * ig * (1.0 - ig)
        gbr_ref[...] += _colsum(dpr)
        gbi_ref[...] += _colsum(dpi)
        xb = xc.astype(BF16)
        dprb = dpr.astype(BF16)
        dpib = dpi.astype(BF16)
        back = []
        for gi in range(NH):
            cs = slice(HD * gi, HD * (gi + 1))
            gwr_ref[gi] += lax.dot_general(xb[:, cs], dprb[:, cs], TN_DIMS, preferred_element_type=F32)
            gwi_ref[gi] += lax.dot_general(xb[:, cs], dpib[:, cs], TN_DIMS, preferred_element_type=F32)
            back.append(lax.dot_general(dprb[:, cs], wr_ref[gi], NT_DIMS, preferred_element_type=F32)
                        + lax.dot_general(dpib[:, cs], wi_ref[gi], NT_DIMS, preferred_element_type=F32))
        dxc = dxc + jnp.concatenate(back, axis=1)
        dxc_buf[...] = dxc
        gcb_ref[...] += _colsum(dxc)
        halo = jnp.where(first, 0.0, xlh_ref[...])
        taps = _conv_taps(jnp.concatenate([halo, xl_ref[...]], axis=0), TS)
        for jj in range(CONV_W):
            gcw_ref[jj:jj + 1, :] += _colsum(dxc * taps[jj])
        ext = jnp.concatenate([dxc, carry_dxc[...]], axis=0)
        dxl = sum(cw_ref[jj:jj + 1, :] * _shift_up(ext, CONV_W - 1 - jj, TS) for jj in range(CONV_W))
        dxl_ref[...] = dxl.astype(BF16)
        carry_dxc[...] = dxc_buf[0:8, :]

    rowr = lambda c, col=0: pl.BlockSpec((TS, c), lambda i: (rev(i), col))
    halo = lambda col: pl.BlockSpec((8, D), lambda i: (jnp.maximum(rev(i) * (TS // 8) - 1, 0), col))
    gate_w = pl.BlockSpec((NH, HD, HD), lambda i: (0, 0, 0))
    return pl.pallas_call(
        body, name="lru_bwd", grid=(n,),
        in_specs=[rowr(D, xl_col), halo(xl_col), rowr(D), rowr(D), halo(0), rowr(D),
                  _vec(8, D), gate_w, _vec(1, D), gate_w, _vec(1, D), _vec(1, D)],
        out_specs=[rowr(D), gate_w, gate_w, _vec(1, D), _vec(1, D), _vec(1, D), _vec(1, D), _vec(8, D)],
        out_shape=[
            jax.ShapeDtypeStruct((s, D), BF16),
            jax.ShapeDtypeStruct((NH, HD, HD), F32), jax.ShapeDtypeStruct((NH, HD, HD), F32),
            jax.ShapeDtypeStruct((1, D), F32), jax.ShapeDtypeStruct((1, D), F32), jax.ShapeDtypeStruct((1, D), F32),
            jax.ShapeDtypeStruct((1, D), F32), jax.ShapeDtypeStruct((8, D), F32),
        ],
        scratch_shapes=[pltpu.VMEM((TS, D), F32), pltpu.VMEM((TS, D), F32), pltpu.VMEM((8, D), F32), pltpu.VMEM((8, D), F32)],
        compiler_params=_params(dimension_semantics=("arbitrary",)),
    )(zr, zr, xc, h, h, dh, conv_w8, w_r, b_r, w_i, b_i, lam)


def _silu_parts(g):
    sg = _sigmoid(g)
    return g * sg, sg * (1.0 + g * (1.0 - sg))


def _branch_out_bwd(o, h, zr, dmix, w_out, gain_a, gain_l):
    s = o.shape[0]

    def body(o_ref, ga_ref, h_ref, gl_ref, dm_ref, w_ref, ka_ref, kl_ref,
             do_ref, dl_ref, dga_ref, dh_ref, dgl_ref, gka_ref, gkl_ref):
        @pl.when(pl.program_id(0) == 0)
        def _():
            gka_ref[...] = jnp.zeros_like(gka_ref)
            gkl_ref[...] = jnp.zeros_like(gkl_ref)

        dycat = lax.dot_general(dm_ref[...], w_ref[...], NT_DIMS, preferred_element_type=F32)

        def one(v, g, dy, gain):
            vhat, rstd = _rms_fwd(v)
            sg, dsg = _silu_parts(g)
            dn = dy * sg
            dg = dy * (vhat * gain) * dsg
            return _rms_bwd(vhat, rstd, dn * gain), dg, _colsum(dn * vhat)

        o = o_ref[...]
        dout, dga, gka = one(o, ga_ref[...], dycat[:, :D], ka_ref[...])
        do_ref[...] = dout.astype(BF16)
        dga_ref[...] = dga.astype(BF16)
        gka_ref[...] += gka
        prod = dout * o
        for hh in range(NH):
            dl_ref[hh] = jnp.broadcast_to(jnp.sum(prod[:, HD * hh:HD * (hh + 1)], axis=1, keepdims=True), (TS, HD))
        dh, dgl, gkl = one(h_ref[...], gl_ref[...], dycat[:, D:], kl_ref[...])
        dh_ref[...] = dh
        dgl_ref[...] = dgl.astype(BF16)
        gkl_ref[...] += gkl

    return pl.pallas_call(
        body, name="branch_out_bwd", grid=(s // TS,),
        in_specs=[_row(D), _row(D, 0), _row(D), _row(D, 2), _row(D), _vec(2 * D, D), _vec(1, D), _vec(1, D)],
        out_specs=[_row(D), pl.BlockSpec((NH, TS, HD), lambda i: (0, i, 0)), _row(D), _row(D), _row(D), _vec(1, D), _vec(1, D)],
        out_shape=[
            jax.ShapeDtypeStruct((s, D), BF16), jax.ShapeDtypeStruct((NH, s, HD), F32), jax.ShapeDtypeStruct((s, D), BF16),
            jax.ShapeDtypeStruct((s, D), F32), jax.ShapeDtypeStruct((s, D), BF16),
            jax.ShapeDtypeStruct((1, D), F32), jax.ShapeDtypeStruct((1, D), F32),
        ],
        compiler_params=_params(dimension_semantics=("arbitrary",)),
    )(o, zr, h, zr, dmix, w_out, gain_a, gain_l)


def _residual(x, o, h, zr, gain_a, gain_l, w_out, post_gain):
    s = x.shape[0]

    def body(x_ref, o_ref, ga_ref, h_ref, gl_ref, ka_ref, kl_ref, w_ref, g_ref, y_ref, m_ref, h1_ref, hb_ref):
        ohat, _ = _rms_fwd(o_ref[...])
        y_ref[:, 0:D] = (ohat * ka_ref[...] * _silu_parts(ga_ref[...])[0]).astype(BF16)
        hhat, _ = _rms_fwd(h_ref[...])
        y_ref[:, D:2 * D] = (hhat * kl_ref[...] * _silu_parts(gl_ref[...])[0]).astype(BF16)
        mix = jnp.dot(y_ref[...], w_ref[...], preferred_element_type=F32)
        m_ref[...] = mix
        mhat, _ = _rms_fwd(mix)
        h1 = x_ref[...] + mhat * g_ref[...]
        h1_ref[...] = h1
        hb_ref[...] = h1.astype(BF16)

    return pl.pallas_call(
        body, name="residual", grid=(s // TS,),
        in_specs=[_row(D), _row(D), _row(D, 0), _row(D), _row(D, 2), _vec(1, D), _vec(1, D), _vec(2 * D, D), _vec(1, D)],
        out_specs=[_row(2 * D), _row(D), _row(D), _row(D)],
        out_shape=[jax.ShapeDtypeStruct((s, 2 * D), BF16), jax.ShapeDtypeStruct((s, D), F32),
                   jax.ShapeDtypeStruct((s, D), F32), jax.ShapeDtypeStruct((s, D), BF16)],
        compiler_params=_params(dimension_semantics=("parallel",)),
    )(x, o, zr, h, zr, gain_a, gain_l, w_out, post_gain)


def _head(h1, p, tgt, mix, w_gate, w_ple, ple_gain, b_gate, post_gain):
    s = h1.shape[0]

    def body(h_ref, p_ref, t_ref, m_ref, wg_ref, wp_ref, kg_ref, b_ref, pg_ref,
             loss_ref, dgp_ref, dpe_ref, dh_ref, dm_ref, gk_ref, gb_ref, gg_ref):
        @pl.when(pl.program_id(0) == 0)
        def _():
            for ref in (loss_ref, gk_ref, gb_ref, gg_ref):
                ref[...] = jnp.zeros_like(ref)

        h1 = h_ref[...]
        pe = jnp.dot(p_ref[...].astype(BF16), wp_ref[...], preferred_element_type=F32)
        gp = jnp.dot(h1.astype(BF16), wg_ref[...], preferred_element_type=F32)
        ehat, rstd = _rms_fwd(pe)
        e = ehat * kg_ref[...]
        gate = _sigmoid(gp + b_ref[...])
        diff = (h1 + gate * e) - t_ref[...]
        per_row = jnp.mean(diff * diff, axis=-1, keepdims=True)
        loss_ref[...] += 0.5 * jnp.sum(per_row, axis=0, keepdims=True)
        dy = diff * (1.0 / D)
        dgp = dy * e * gate * (1.0 - gate)
        dgpb = dgp.astype(BF16)
        dgp_ref[...] = dgpb
        gb_ref[...] += _colsum(dgp)
        de = dy * gate
        gk_ref[...] += _colsum(de * ehat)
        dpe_ref[...] = _rms_bwd(ehat, rstd, de * kg_ref[...]).astype(BF16)
        dh1 = dy + lax.dot_general(dgpb, wg_ref[...], NT_DIMS, preferred_element_type=F32)
        dh_ref[...] = dh1
        mhat, rstd_m = _rms_fwd(m_ref[...])
        gg_ref[...] += _colsum(dh1 * mhat)
        dm_ref[...] = _rms_bwd(mhat, rstd_m, dh1 * pg_ref[...]).astype(BF16)

    return pl.pallas_call(
        body, name="head", grid=(s // TS,),
        in_specs=[_row(D), _row(D_PLE), _row(D), _row(D), _vec(D, D), _vec(D_PLE, D), _vec(1, D), _vec(1, D), _vec(1, D)],
        out_specs=[_vec(1, 1), _row(D), _row(D), _row(D), _row(D), _vec(1, D), _vec(1, D), _vec(1, D)],
        out_shape=[
            jax.ShapeDtypeStruct((1, 1), F32), jax.ShapeDtypeStruct((s, D), BF16), jax.ShapeDtypeStruct((s, D), BF16),
            jax.ShapeDtypeStruct((s, D), F32), jax.ShapeDtypeStruct((s, D), BF16),
            jax.ShapeDtypeStruct((1, D), F32), jax.ShapeDtypeStruct((1, D), F32), jax.ShapeDtypeStruct((1, D), F32),
        ],
        compiler_params=_params(dimension_semantics=("arbitrary",)),
    )(h1, p, tgt, mix, w_gate, w_ple, ple_gain, b_gate, post_gain)


def _prenorm_bwd(x, dxn_a, dz_rest, w_rest_t, dh1, pre_gain, after):
    s = x.shape[0]
    ks = [a.shape[1] for a in dz_rest]
    assert sum(ks) == w_rest_t.shape[0]
    nz = len(dz_rest)
    tm = 512
    rowm = lambda c: pl.BlockSpec((tm, c), lambda i: (i, 0))

    def body(*refs):
        x_ref, da_ref = refs[:2]
        w_ref, dh_ref, g_ref, after_ref, dx_ref, gg_ref = refs[2 + nz:]

        @pl.when(pl.program_id(0) == 0)
        def _():
            gg_ref[...] = jnp.zeros_like(gg_ref)

        dxn, k0 = da_ref[...], 0
        for dz_ref, kw in zip(refs[2:2 + nz], ks):
            dxn = dxn + jnp.dot(dz_ref[...], w_ref[k0:k0 + kw, :], preferred_element_type=F32)
            k0 += kw
        xhat, rstd = _rms_fwd(x_ref[...])
        gg_ref[...] += _colsum(dxn * xhat)
        dx_ref[...] = dh_ref[...] + _rms_bwd(xhat, rstd, dxn * g_ref[...])

    return pl.pallas_call(
        body, name="prenorm_bwd", grid=(s // tm,),
        in_specs=[rowm(D), rowm(D)] + [rowm(kw) for kw in ks] + [_vec(*w_rest_t.shape), rowm(D), _vec(1, D), _vec(8, HD)],
        out_specs=[rowm(D), _vec(1, D)],
        out_shape=[jax.ShapeDtypeStruct((s, D), F32), jax.ShapeDtypeStruct((1, D), F32)],
        compiler_params=_params(dimension_semantics=("arbitrary",)),
    )(x, dxn_a, *dz_rest, w_rest_t, dh1, pre_gain, after)


def _adamw(name, parts, w, m, v, own=None, me=None):
    r, c = w.shape
    if r % 8 == 0:
        tr = _pick(r, (256, 128, 16, 8))
        grid = (r // tr,)
        blk = pl.BlockSpec((tr, c), lambda i: (i, 0))
        parts_blk = pl.BlockSpec((N_DEV, tr, c), lambda i: (0, i, 0))
    else:
        tc = _pick(c, (256, 128))
        grid = (c // tc,)
        blk = pl.BlockSpec((r, tc), lambda i: (0, i))
        parts_blk = pl.BlockSpec((N_DEV, r, tc), lambda i: (0, 0, i))

    def body(*refs):
        p_ref, w_ref, m_ref, v_ref = refs[:4]
        g_ref, d_ref, nm_ref, nv_ref = refs[-4:]
        if own is None:
            g = p_ref[0].astype(F32)
            for j in range(1, N_DEV):
                g = g + p_ref[j].astype(F32)
            g_ref[...] = g
        else:
            own_ref, me_ref = refs[4:6]
            g_ref[...] = jnp.zeros_like(g_ref)
            for j in range(N_DEV):
                @pl.when(me_ref[0] == j)
                def _():
                    g_ref[...] += own_ref[...].astype(F32)

                @pl.when(me_ref[0] != j)
                def _():
                    g_ref[...] += p_ref[j].astype(F32)
            g = g_ref[...]
        nm = ADAM_B1 * m_ref[...] + (1.0 - ADAM_B1) * g
        nv = ADAM_B2 * v_ref[...] + (1.0 - ADAM_B2) * (g * g)
        nm_ref[...] = nm
        nv_ref[...] = nv
        m_hat = nm / (1.0 - ADAM_B1 ** ADAM_STEP)
        v_hat = nv / (1.0 - ADAM_B2 ** ADAM_STEP)
        d_ref[...] = -ADAM_LR * (m_hat / (jnp.sqrt(v_hat) + ADAM_EPS) + ADAM_WD * w_ref[...])

    in_specs, args = [parts_blk, blk, blk, blk], [parts, w, m, v]
    if own is not None:
        in_specs += [blk, pl.BlockSpec(memory_space=pltpu.SMEM)]
        args += [own, me]
    return pl.pallas_call(
        body, name=name, grid=grid,
        in_specs=in_specs,
        out_specs=[blk] * 4,
        out_shape=[jax.ShapeDtypeStruct((r, c), F32)] * 4,
        compiler_params=_params(dimension_semantics=("parallel",)),
    )(*args)


def _spread8(v):
    r = v.shape[0]
    return jnp.pad(jnp.pad(v[:, :, None], ((0, 0), (0, 0), (0, 7))).reshape(r, 8 * NH), ((0, 0), (0, HD - 8 * NH)))


def _gather8(v):
    return v[:, :8 * NH].reshape(v.shape[0], NH, 8)[:, :, 0]


def _cols_to_shards(g):
    r, c8 = g.shape
    return g.reshape(r, N_DEV, c8 // N_DEV).transpose(1, 0, 2)


def _shards_to_cols(g):
    n, r, c = g.shape
    return g.transpose(1, 0, 2).reshape(r, n * c)


def kernel(x, p, w_in, b_f, pre_gain, post_gain, conv_w, conv_b, w_rgate, b_rgate, w_igate, b_igate, lru_lambda, attn_out_gain, lru_out_gain, w_out, w_ple, ple_gain, w_ple_gate, b_ple_gate, loss_target, m_w_in, m_b_f, m_pre_gain, m_post_gain, m_conv_w, m_conv_b, m_w_rgate, m_b_rgate, m_w_igate, m_b_igate, m_lru_lambda, m_attn_out_gain, m_lru_out_gain, m_w_out, m_w_ple, m_ple_gain, m_w_ple_gate, m_b_ple_gate, v_w_in, v_b_f, v_pre_gain, v_post_gain, v_conv_w, v_conv_b, v_w_rgate, v_b_rgate, v_w_igate, v_b_igate, v_lru_lambda, v_attn_out_gain, v_lru_out_gain, v_w_out, v_w_ple, v_ple_gain, v_w_ple_gate, v_b_ple_gate):
    me = 4 * lax.axis_index("x") + 2 * lax.axis_index("y") + lax.axis_index("c")
    x2, p2, tgt = x[0], p[0, 0], loss_target[0]

    conv_w_shard8 = jnp.pad(conv_w[0], ((0, 8 - CONV_W), (0, 0)))
    wt, m_wt, v_wt = w_in[0].T, m_w_in[0].T, v_w_in[0].T
    g_wint, g_conv = _gather_two_level("gather_w_in", [wt.astype(BF16), conv_w_shard8])
    win_t = g_wint.reshape(D_IN, D)
    rest_state, rest_token = _exchange_start(
        "gather_rest_start", [w_out[0].astype(BF16), w_ple[0].astype(BF16), w_ple_gate[0].astype(BF16)], ["bcast"] * 3,
        after=g_conv)
    w_rest_t = jnp.concatenate([win_t[D_QKV + NH:], _spread8(win_t[D_QKV:D_QKV + NH].T).T], axis=0)
    conv_w8 = _shards_to_cols(g_conv)
    bf_pad = _spread8(b_f)
    w_r, w_i = w_rgate[0].astype(BF16), w_igate[0].astype(BF16)

    xn, zq = _prenorm_proj(x2, pre_gain, win_t, D_QKV, rest_token)
    zr, kx = _proj_rest(xn, w_rest_t, bf_pad)
    o, ax = _attn_fwd(zq, kx)
    xc, h = _lru_fwd(zr, conv_w8, conv_b, w_r, b_rgate, w_i, b_igate, lru_lambda)
    g_wout, g_wple, g_wpg = _exchange_wait("gather_rest_wait", rest_state, h)
    wout_full = g_wout.reshape(2 * D, D)
    wple_full = _shards_to_cols(g_wple)
    wpg_full = g_wpg.reshape(D, D)
    ycat, mix, h1, h1b = _residual(x2, o, h, zr, attn_out_gain, lru_out_gain, wout_full, post_gain)

    loss_part, dgp, dpe, dh1, dmix, g_ple_gain, g_b_gate, g_post_gain = _head(
        h1, p2, tgt, mix, wpg_full, wple_full, ple_gain, b_ple_gate, post_gain)
    gw_pg = _mm("bwd_gate_w", h1b, dgp, "tn", BF16)
    gw_ple = _mm("bwd_ple_w", p2, dpe, "tn", BF16)
    gw_out = _mm("bwd_out_w", ycat, dmix, "tn", BF16)
    do, delta, dga, dh, dgl, g_aog, g_log = _branch_out_bwd(o, h, zr, dmix, wout_full, attn_out_gain, lru_out_gain)
    dxl, g_wr, g_wi, g_br, g_bi, g_lam, g_cb, g_cw8 = _lru_bwd(
        zr, xc, h, dh, conv_w8, w_r, b_rgate, w_i, b_igate, lru_lambda)
    gates = jnp.concatenate([g_wr.reshape(D, HD), g_wi.reshape(D, HD)], axis=0).astype(BF16)
    outw_state, outw_token = _exchange_start(
        "exchange_outw_start",
        [gw_out.reshape(N_DEV, 2 * D // N_DEV, D), _cols_to_shards(gw_ple), gw_pg.reshape(N_DEV, D // N_DEV, D), gates],
        ["scatter"] * 3 + ["bcast"])
    dq, dk, dv, dcs, drs = _attn_bwd(zq, do, ax, delta, kx, outw_token)
    dfl, g_bf_pad = _forget_bwd(dcs, drs, zr, bf_pad)
    gw_pieces = [_mm("bwd_w_" + nm, dz, xn, "tn", BF16) for nm, dz in
                 (("q", dq), ("k", dk), ("v", dv), ("fl", dfl), ("ga", dga), ("xl", dxl), ("gl", dgl))]
    gw_pieces[3] = _gather8(gw_pieces[3].T).T
    gw_in_t = jnp.concatenate(gw_pieces, axis=0)
    inw_state, inw_token = _exchange_start(
        "exchange_inw_start", [gw_in_t.reshape(N_DEV, D_IN_SHARD, D)], ["scatter"])
    dxn_a = _mm_cat("bwd_qkv_x", [dq, dk, dv], win_t, F32, after=inw_token)
    grad_x, g_pre_gain = _prenorm_bwd(x2, dxn_a, [dga, dxl, dgl, dfl], w_rest_t, dh1, pre_gain, inw_token)

    upd = {}
    me1 = me.reshape(1).astype(jnp.int32)
    r_wout, r_wple, r_wpg, r_gates = _exchange_wait("exchange_outw_wait", outw_state, grad_x, fill_own=False)
    upd["w_out"] = _adamw("adamw_w_out", r_wout[0], w_out[0], m_w_out[0], v_w_out[0], r_wout[1], me1)
    upd["w_ple"] = _adamw("adamw_w_ple", r_wple[0], w_ple[0], m_w_ple[0], v_w_ple[0], r_wple[1], me1)
    upd["w_ple_gate"] = _adamw("adamw_w_ple_gate", r_wpg[0], w_ple_gate[0], m_w_ple_gate[0], v_w_ple_gate[0], r_wpg[1], me1)
    gates_of = lambda a, b: jnp.concatenate([a[0].reshape(D, HD), b[0].reshape(D, HD)], axis=0)
    g_gates = _adamw("adamw_gates", r_gates[0], gates_of(w_rgate, w_igate), gates_of(m_w_rgate, m_w_igate),
                     gates_of(v_w_rgate, v_w_igate), r_gates[1], me1)
    upd["w_rgate"] = [a[:D].reshape(1, NH, HD, HD) for a in g_gates]
    upd["w_igate"] = [a[D:].reshape(1, NH, HD, HD) for a in g_gates]
    behind = upd["w_out"][0][0:1] + upd["w_ple_gate"][0][0:1] + jnp.pad(g_gates[0][0:1], ((0, 0), (0, D - HD)))
    small = jnp.concatenate(
        [jnp.pad(_gather8(g_bf_pad), ((0, 0), (0, D - NH))), g_pre_gain, g_post_gain, g_cb, g_br, g_bi, g_lam, g_aog, g_log,
         g_ple_gain, g_b_gate, g_cw8[:CONV_W], behind, jnp.pad(loss_part, ((0, 7), (0, D - 1)))], axis=0)
    (r_small,) = _exchange("exchange_small", [small], ["bcast"])
    vec_names = ["b_f", "pre_gain", "post_gain", "conv_b", "b_rgate", "b_igate", "lru_lambda", "attn_out_gain",
                 "lru_out_gain", "ple_gain", "b_ple_gate"]
    vec_w = dict(b_f=(b_f, m_b_f, v_b_f), pre_gain=(pre_gain, m_pre_gain, v_pre_gain),
                 post_gain=(post_gain, m_post_gain, v_post_gain), conv_b=(conv_b, m_conv_b, v_conv_b),
                 b_rgate=(b_rgate, m_b_rgate, v_b_rgate), b_igate=(b_igate, m_b_igate, v_b_igate),
                 lru_lambda=(lru_lambda, m_lru_lambda, v_lru_lambda),
                 attn_out_gain=(attn_out_gain, m_attn_out_gain, v_attn_out_gain),
                 lru_out_gain=(lru_out_gain, m_lru_out_gain, v_lru_out_gain), ple_gain=(ple_gain, m_ple_gain, v_ple_gain),
                 b_ple_gate=(b_ple_gate, m_b_ple_gate, v_b_ple_gate))
    conv_mine = lambda a: lax.dynamic_slice_in_dim(a, me * HD, HD, axis=1)

    def small_rows(k):
        rows = [jnp.pad(vec_w[nm][k], ((0, 0), (0, D - vec_w[nm][k].shape[1]))) for nm in vec_names]
        cw = (conv_w, m_conv_w, v_conv_w)[k][0]
        full = lax.dynamic_update_slice_in_dim(jnp.ones((CONV_W, D), F32), cw, me * HD, axis=1)
        return jnp.concatenate(rows + [full, jnp.ones((9, D), F32)], axis=0)

    g_small = _adamw("adamw_small", r_small, small_rows(0), small_rows(1), small_rows(2))
    loss = g_small[0][16, 0]
    for idx, nm in enumerate(vec_names):
        width = vec_w[nm][0].shape[1]
        upd[nm] = [a[idx:idx + 1, :width] for a in g_small]
    base = len(vec_names)
    upd["conv_w"] = [conv_mine(a[base:base + CONV_W])[None] for a in g_small]
    (r_win,) = _exchange_wait("exchange_inw_wait", inw_state, g_small[0], fill_own=False)
    upd["w_in"] = [a.T for a in _adamw("adamw_w_in", r_win[0], wt, m_wt, v_wt, r_win[1], me1)]
    for nm in ("w_in", "w_out", "w_ple", "w_ple_gate"):
        upd[nm] = [a[None] for a in upd[nm]]

    order = ["w_in", "b_f", "pre_gain", "post_gain", "conv_w", "conv_b", "w_rgate", "b_rgate", "w_igate", "b_igate",
             "lru_lambda", "attn_out_gain", "lru_out_gain", "w_out", "w_ple", "ple_gain", "w_ple_gate", "b_ple_gate"]
    outs = [loss, grad_x[None]]
    for k in range(4):
        outs += [upd[nm][k] for nm in order]
    return tuple(outs)
```

```python
import functools

import jax
import jax.numpy as jnp
from jax import lax
from jax.experimental import pallas as pl
from jax.experimental.pallas import tpu as pltpu

F32 = jnp.float32
BF16 = jnp.bfloat16

N_DEV = 8
D = 1024
HD = 128
NH = 8
D_IN = 6152
D_IN_SHARD = D_IN // N_DEV
D_QKV = 3 * D
D_REST = 3 * D + HD
FL_COL = 3 * D
D_PLE = 256
CONV_W = 4
LRU_C = 8.0
RMS_EPS = 1e-6
SCALE = HD ** -0.5
EXP2_SCALE = SCALE * 1.4426950408889634
NEG = -1e30

ADAM_LR = 0.001
ADAM_B1 = 0.9
ADAM_B2 = 0.999
ADAM_EPS = 1e-08
ADAM_WD = 0.01
ADAM_STEP = 10

TS = 256
TQ = 1024
ROW_PARTS = 2
VMEM_LIMIT = 48 * 1024 * 1024

NT_DIMS = (((1,), (1,)), ((), ()))
TN_DIMS = (((0,), (0,)), ((), ()))


def _params(**kw):
    return pltpu.CompilerParams(vmem_limit_bytes=VMEM_LIMIT, **kw)


def _sigmoid(v):
    return 0.5 * jnp.tanh(0.5 * v) + 0.5


def _sigmoid_rel(v):
    return 1.0 / (1.0 + jnp.exp(-v))


def _rms_fwd(v):
    rstd = lax.rsqrt(jnp.mean(v * v, axis=-1, keepdims=True) + RMS_EPS)
    return v * rstd, rstd


def _rms_bwd(vhat, rstd, dvhat):
    return rstd * (dvhat - vhat * jnp.mean(dvhat * vhat, axis=-1, keepdims=True))


def _colsum(v):
    return jnp.sum(v, axis=0, keepdims=True)


def _rows_iota(t):
    return lax.broadcasted_iota(jnp.int32, (t, 1), 0)


def _scan(a, u, reverse):
    t, c = a.shape
    rows = _rows_iota(t)
    d = 1
    while d < t:
        if d < 8:
            valid = rows < t - d if reverse else rows >= d
            shift = t - d if reverse else d
            u = jnp.where(valid, u + a * pltpu.roll(u, shift, 0), u)
            a = jnp.where(valid, a * pltpu.roll(a, shift, 0), a)
        else:
            zeros, ones = jnp.zeros((d, c), F32), jnp.ones((d, c), F32)
            if reverse:
                u_far, a_far = jnp.concatenate([u[d:], zeros], axis=0), jnp.concatenate([a[d:], ones], axis=0)
            else:
                u_far, a_far = jnp.concatenate([zeros, u[:t - d]], axis=0), jnp.concatenate([ones, a[:t - d]], axis=0)
            u = u + a * u_far
            a = a * a_far
        d *= 2
    return a, u


def _cumsum_fwd(v):
    t = v.shape[0]
    rows = _rows_iota(t)
    d = 1
    while d < t:
        v = jnp.where(rows >= d, v + pltpu.roll(v, d, 0), v)
        d *= 2
    return v


def _cumsum_bwd(v):
    t = v.shape[0]
    rows = _rows_iota(t)
    d = 1
    while d < t:
        v = jnp.where(rows < t - d, v + pltpu.roll(v, t - d, 0), v)
        d *= 2
    return v


def _bias_lanes(v, at, ones_at):
    lane = lax.broadcasted_iota(jnp.int32, v.shape, 1)
    hi = v.astype(BF16).astype(F32)
    mid = (v - hi).astype(BF16).astype(F32)
    lo = ((v - hi) - mid).astype(BF16).astype(F32)
    out = jnp.where((lane >= ones_at) & (lane < ones_at + 3), 1.0, 0.0)
    for k, piece in enumerate((hi, mid, lo)):
        out = jnp.where(lane == at + k, piece, out)
    return out.astype(BF16)


def _shift_down(ext, k, t):
    return pltpu.roll(ext, k, 0)[8:, :] if k else ext[8:, :]


def _shift_up(ext, k, t):
    return pltpu.roll(ext, t + 8 - k, 0)[:t, :] if k else ext[:t, :]


def _exchange(name, arrs, kinds):
    n = len(arrs)
    out_shape = []
    for a, kind in zip(arrs, kinds):
        shp = a.shape if kind == "scatter" else (N_DEV,) + a.shape
        out_shape.append(jax.ShapeDtypeStruct(shp, a.dtype))

    def body(*refs):
        ins, outs = refs[:n], refs[n:2 * n]
        send_sems, recv_sems, local_sems = refs[2 * n:]
        x, y, c = lax.axis_index("x"), lax.axis_index("y"), lax.axis_index("c")
        me = 4 * x + 2 * y + c
        copies = []
        for i in range(n):
            scatter = kinds[i] == "scatter"
            mine = pltpu.make_async_copy(ins[i].at[me] if scatter else ins[i], outs[i].at[me], local_sems.at[i])
            mine.start()
            copies.append(mine)
            for m in range(1, N_DEV):
                px = 1 - x if m & 4 else x
                py = 1 - y if m & 2 else y
                pc = 1 - c if m & 1 else c
                peer = 4 * px + 2 * py + pc
                cp = pltpu.make_async_remote_copy(
                    src_ref=ins[i].at[peer] if scatter else ins[i],
                    dst_ref=outs[i].at[me],
                    send_sem=send_sems.at[i, m - 1],
                    recv_sem=recv_sems.at[i, m - 1],
                    device_id=(px, py, pc),
                    device_id_type=pl.DeviceIdType.MESH,
                )
                cp.start()
                copies.append(cp)
        for cp in copies:
            cp.wait()

    any_spec = pl.BlockSpec(memory_space=pl.ANY)
    return pl.pallas_call(
        body,
        name=name,
        out_shape=out_shape,
        in_specs=[any_spec] * n,
        out_specs=[any_spec] * n,
        scratch_shapes=[
            pltpu.SemaphoreType.DMA((n, N_DEV - 1)),
            pltpu.SemaphoreType.DMA((n, N_DEV - 1)),
            pltpu.SemaphoreType.DMA((n,)),
        ],
        compiler_params=pltpu.CompilerParams(has_side_effects=True),
    )(*arrs)


def _gather_two_level(name, arrs, pieces=1):
    n = len(arrs)
    items = []
    for i, a in enumerate(arrs):
        rows = a.shape[0]
        if pieces > 1 and rows >= 512:
            step = -(-rows // (16 * pieces)) * 16
            items += [(i, r0, min(step, rows - r0)) for r0 in range(0, rows, step)]
        else:
            items.append((i, 0, rows))
    n_items = len(items)

    def body(*refs):
        ins, outs = refs[:n], refs[n:2 * n]
        send_sems, recv_sems, local_sems = refs[2 * n:]
        x, y, c = lax.axis_index("x"), lax.axis_index("y"), lax.axis_index("c")
        me, sibling = (x, y, c), (x, y, 1 - c)
        chips = [(1 - x, y), (x, 1 - y), (1 - x, 1 - y)]

        def rows_of(ref, t):
            i, r0, rn = items[t]
            return ref if rn == arrs[i].shape[0] else ref.at[pl.ds(r0, rn)]

        def slot(t, dev):
            return rows_of(outs[items[t][0]].at[4 * dev[0] + 2 * dev[1] + dev[2]], t)

        def copy(t, k, block, to, from_input=False):
            return pltpu.make_async_remote_copy(
                src_ref=rows_of(ins[items[t][0]], t) if from_input else slot(t, block), dst_ref=slot(t, block),
                send_sem=send_sems.at[t, k], recv_sem=recv_sems.at[t, k],
                device_id=to, device_id_type=pl.DeviceIdType.MESH)

        own, sent = [], []
        for t in range(n_items):
            mine = pltpu.make_async_copy(rows_of(ins[items[t][0]], t), slot(t, me), local_sems.at[t])
            mine.start()
            own.append(mine)
            first = [copy(t, 1 + j, me, (*chip, c), from_input=True) for j, chip in enumerate(chips)]
            first.append(copy(t, 0, me, sibling, from_input=True))
            for cp in first:
                cp.start()
            sent += first
        for t in range(n_items):
            for j, chip in enumerate(chips):
                copy(t, 1 + j, (*chip, c), me).wait_recv()
                fwd = copy(t, 4 + j, (*chip, c), sibling)
                fwd.start()
                sent.append(fwd)
        for t in range(n_items):
            copy(t, 0, sibling, me).wait_recv()
            for j, chip in enumerate(chips):
                copy(t, 4 + j, (*chip, 1 - c), me).wait_recv()
        for cp in sent:
            cp.wait_send()
        for cp in own:
            cp.wait()

    any_spec = pl.BlockSpec(memory_space=pl.ANY)
    return pl.pallas_call(
        body, name=name,
        out_shape=[jax.ShapeDtypeStruct((N_DEV,) + a.shape, a.dtype) for a in arrs],
        in_specs=[any_spec] * n, out_specs=[any_spec] * n,
        scratch_shapes=[pltpu.SemaphoreType.DMA((n_items, 7)), pltpu.SemaphoreType.DMA((n_items, 7)),
                        pltpu.SemaphoreType.DMA((n_items,))],
        compiler_params=pltpu.CompilerParams(has_side_effects=True),
    )(*arrs)


def _peers(x, y, c):
    out = []
    for m in range(1, N_DEV):
        px = 1 - x if m & 4 else x
        py = 1 - y if m & 2 else y
        pc = 1 - c if m & 1 else c
        out.append((m, (px, py, pc), 4 * px + 2 * py + pc))
    return out


def _split_copies(kinds, src_refs, land_refs, send_sems, recv_sems):
    x, y, c = lax.axis_index("x"), lax.axis_index("y"), lax.axis_index("c")
    me = 4 * x + 2 * y + c
    copies = []
    for i, kind in enumerate(kinds):
        for m, peer, pidx in _peers(x, y, c):
            sends = receives = None
            if kind == "scatter":
                src = src_refs[i].at[pidx]
            elif kind[0] == "owners":
                first, count = kind[1:]
                src = src_refs[i].at[jnp.clip(pidx - first, 0, count - 1)]
                sends = (pidx >= first) & (pidx < first + count)
                receives = (me >= first) & (me < first + count)
            else:
                src = src_refs[i]
            copies.append((pltpu.make_async_remote_copy(
                src_ref=src,
                dst_ref=land_refs[i].at[me],
                send_sem=send_sems.at[i * (N_DEV - 1) + m - 1],
                recv_sem=recv_sems.at[i * (N_DEV - 1) + m - 1],
                device_id=peer,
                device_id_type=pl.DeviceIdType.MESH,
            ), sends, receives))
    return copies


def _when(cond, fn):
    if cond is None:
        fn()
    else:
        pl.when(cond)(fn)


_HBM_SPEC = pl.BlockSpec(memory_space=pltpu.HBM)
_SEM_SPEC = pl.BlockSpec(memory_space=pltpu.SEMAPHORE)
_DATAFLOW = pltpu.SideEffectType.DATAFLOW_SIDE_EFFECTING


def _exchange_start(name, arrs, kinds, after=None):
    n = len(arrs)
    extra = [] if after is None else [after]
    lands = []
    for a, kind in zip(arrs, kinds):
        shp = a.shape if kind == "scatter" else (N_DEV,) + (a.shape[1:] if kind[0] == "owners" else a.shape)
        lands.append(lax.empty(shp, a.dtype))

    def body(*refs):
        src_refs, land_refs = refs[:n], refs[n:2 * n]
        send_sems, recv_sems = refs[2 * n + len(extra):2 * n + len(extra) + 2]
        token = refs[-1]
        for cp, sends, _ in _split_copies(kinds, src_refs, land_refs, send_sems, recv_sems):
            _when(sends, cp.start)
        token[...] = jnp.zeros_like(token)

    n_sem = n * (N_DEV - 1)
    hbm = lambda a: pltpu.HBM(a.shape, a.dtype)
    res = pl.pallas_call(
        body, name=name,
        out_shape=(pltpu.SemaphoreType.DMA((n_sem,)), pltpu.SemaphoreType.DMA((n_sem,)),
                   *[hbm(a) for a in arrs], *[hbm(a) for a in lands], jax.ShapeDtypeStruct((8, HD), F32)),
        in_specs=[_HBM_SPEC] * (2 * n) + [pl.BlockSpec(memory_space=pl.ANY)] * len(extra),
        out_specs=(_SEM_SPEC, _SEM_SPEC, *[_HBM_SPEC] * (2 * n), pl.BlockSpec(memory_space=pltpu.VMEM)),
        input_output_aliases={i: 2 + i for i in range(2 * n)},
        compiler_params=pltpu.CompilerParams(has_side_effects=_DATAFLOW),
    )(*[pltpu.with_memory_space_constraint(a, pltpu.HBM) for a in arrs],
      *[pltpu.with_memory_space_constraint(a, pltpu.HBM) for a in lands], *extra)
    return (kinds, res[0], res[1], res[2:2 + n], res[2 + n:2 + 2 * n]), res[-1]


def _exchange_wait(name, state, after, fill_own=True):
    kinds, send_sems, recv_sems, srcs, lands = state
    n = len(srcs)

    def body(*refs):
        src_refs, land_refs = refs[:n], refs[n:2 * n]
        send_sems_ref, recv_sems_ref = refs[2 * n:2 * n + 2]
        for cp, sends, receives in _split_copies(kinds, src_refs, land_refs, send_sems_ref, recv_sems_ref):
            _when(sends, cp.wait_send)
            _when(receives, cp.wait_recv)

    res = pl.pallas_call(
        body, name=name,
        out_shape=tuple(pltpu.HBM(a.shape, a.dtype) for a in (*srcs, *lands)),
        in_specs=[_HBM_SPEC] * (2 * n) + [_SEM_SPEC, _SEM_SPEC, pl.BlockSpec(memory_space=pl.ANY)],
        out_specs=tuple([_HBM_SPEC] * (2 * n)),
        input_output_aliases={i: i for i in range(2 * n)},
        compiler_params=pltpu.CompilerParams(has_side_effects=_DATAFLOW),
    )(*srcs, *lands, send_sems, recv_sems, after)
    me = 4 * lax.axis_index("x") + 2 * lax.axis_index("y") + lax.axis_index("c")
    outs = []
    for kind, src, land in zip(kinds, res[:n], res[n:]):
        if kind == "scatter":
            own = lax.dynamic_index_in_dim(src, me, 0, keepdims=False)
        elif kind[0] == "owners":
            own = lax.dynamic_index_in_dim(src, jnp.clip(me - kind[1], 0, kind[2] - 1), 0, keepdims=False)
        else:
            own = src
        outs.append(lax.dynamic_update_index_in_dim(land, own, me, 0) if fill_own else (land, own))
    return outs


def _pick(n, cands):
    for t in cands:
        if n % t == 0:
            return t
    raise ValueError(f"no tile for {n}")


def _mm(name, a, b, mode, out_dtype, after=None):
    if mode == "nn":
        (m, k), (k2, n) = a.shape, b.shape
    elif mode == "nt":
        (m, k), (n, k2) = a.shape, b.shape
    else:
        (k, m), (k2, n) = a.shape, b.shape
    assert k == k2, (name, a.shape, b.shape)
    if mode == "tn":
        tm = _pick(m, (1024, 640, 512, 256, 128))
        tn = _pick(n, (1024, 640, 512, 256, 128))
        tk = _pick(k, (2048, 1024, 512, 256))
    else:
        tm, tn, tk = _pick(m, (512, 256)), n, k
    nk = k // tk

    def body(a_ref, b_ref, *rest):
        o_ref = rest[-2] if nk > 1 else rest[-1]
        av = a_ref[...].astype(BF16)
        bv = b_ref[...].astype(BF16)
        if mode == "nn":
            part = jnp.dot(av, bv, preferred_element_type=F32)
        elif mode == "nt":
            part = lax.dot_general(av, bv, NT_DIMS, preferred_element_type=F32)
        else:
            part = lax.dot_general(av, bv, TN_DIMS, preferred_element_type=F32)
        if nk == 1:
            o_ref[...] = part.astype(out_dtype)
            return
        acc_ref = rest[-1]
        kk = pl.program_id(2)

        @pl.when(kk == 0)
        def _():
            acc_ref[...] = part

        @pl.when(kk > 0)
        def _():
            acc_ref[...] += part

        @pl.when(kk == nk - 1)
        def _():
            o_ref[...] = acc_ref[...].astype(out_dtype)

    if mode == "tn":
        a_spec = pl.BlockSpec((tk, tm), lambda j, i, kk: (kk, i))
    else:
        a_spec = pl.BlockSpec((tm, tk), lambda j, i, kk: (i, kk))
    if mode == "nt":
        b_spec = pl.BlockSpec((tn, tk), lambda j, i, kk: (j, kk))
    else:
        b_spec = pl.BlockSpec((tk, tn), lambda j, i, kk: (kk, j))
    in_specs, args = [a_spec, b_spec], [a, b]
    if after is not None:
        in_specs.append(pl.BlockSpec((8, HD), lambda j, i, kk: (0, 0)))
        args.append(after)
    return pl.pallas_call(
        body,
        name=name,
        grid=(n // tn, m // tm, nk),
        in_specs=in_specs,
        out_specs=pl.BlockSpec((tm, tn), lambda j, i, kk: (i, j)),
        out_shape=jax.ShapeDtypeStruct((m, n), out_dtype),
        scratch_shapes=[pltpu.VMEM((tm, tn), F32)] if nk > 1 else [],
        compiler_params=_params(dimension_semantics=("parallel", "parallel", "arbitrary")),
    )(*args)


def _mm_cat(name, a_list, b, out_dtype, after=None):
    m = a_list[0].shape[0]
    ks = [a.shape[1] for a in a_list]
    n = b.shape[1]
    assert sum(ks) <= b.shape[0], (name, ks, b.shape)
    tm = _pick(m, (512, 256))
    na = len(a_list)

    def body(*refs):
        b_ref, o_ref = refs[na], refs[-1]
        k0, acc = 0, None
        for a_ref, kw in zip(refs[:na], ks):
            part = jnp.dot(a_ref[...].astype(BF16), b_ref[k0:k0 + kw, :], preferred_element_type=F32)
            acc = part if acc is None else acc + part
            k0 += kw
        o_ref[...] = acc.astype(out_dtype)

    in_specs = [pl.BlockSpec((tm, kw), lambda i: (i, 0)) for kw in ks] + [pl.BlockSpec((sum(ks), n), lambda i: (0, 0))]
    args = [*a_list, b]
    if after is not None:
        in_specs.append(pl.BlockSpec((8, HD), lambda i: (0, 0)))
        args.append(after)
    return pl.pallas_call(
        body, name=name, grid=(m // tm,),
        in_specs=in_specs, out_specs=pl.BlockSpec((tm, n), lambda i: (i, 0)),
        out_shape=jax.ShapeDtypeStruct((m, n), out_dtype),
        compiler_params=_params(dimension_semantics=("parallel",)),
    )(*args)


def _row(c, col=0):
    return pl.BlockSpec((TS, c), lambda i: (i, col))


def _vec(r, c):
    return pl.BlockSpec((r, c), lambda i: (0, 0))


def _prenorm_proj(x, pre_gain, w_t, n, after):
    s = x.shape[0]
    tm = 512

    def body(x_ref, g_ref, w_ref, after_ref, xn_ref, z_ref):
        xhat, _ = _rms_fwd(x_ref[...])
        xn = (xhat * g_ref[...]).astype(BF16)
        xn_ref[...] = xn
        z_ref[...] = lax.dot_general(xn, w_ref[...], NT_DIMS, preferred_element_type=F32).astype(BF16)

    return pl.pallas_call(
        body, name="prenorm_proj_qkv", grid=(s // tm,),
        in_specs=[pl.BlockSpec((tm, D), lambda i: (i, 0)), _vec(1, D), _vec(n, D), _vec(8, HD)],
        out_specs=[pl.BlockSpec((tm, D), lambda i: (i, 0)), pl.BlockSpec((tm, n), lambda i: (i, 0))],
        out_shape=[jax.ShapeDtypeStruct((s, D), BF16), jax.ShapeDtypeStruct((s, n), BF16)],
        compiler_params=_params(dimension_semantics=("parallel",)),
    )(x, pre_gain, w_t, after)


def _proj_rest(xn, w_rest_t, bf_pad):
    s = xn.shape[0]
    tm = 512

    def body(x_ref, w_ref, b_ref, z_ref, kx_ref, c_buf, carry):
        @pl.when(pl.program_id(0) == 0)
        def _():
            carry[...] = jnp.zeros_like(carry)

        z = lax.dot_general(x_ref[...], w_ref[...], NT_DIMS, preferred_element_type=F32)
        z_ref[...] = z
        fl = z[:, FL_COL:] + b_ref[...]
        ls = jnp.minimum(fl, 0.0) - jnp.log(1.0 + jnp.exp(-jnp.abs(fl)))
        c_buf[...] = _cumsum_fwd(ls) + carry[0:1, :]
        carry[0:1, :] = c_buf[tm - 1:tm, :]
        cv = c_buf[...]
        for h in range(NH):
            kx_ref[h] = _bias_lanes(jnp.broadcast_to(cv[:, 8 * h:8 * h + 1], (tm, HD)) * (-1.0 / SCALE), 0, 3)

    return pl.pallas_call(
        body, name="proj_rest", grid=(s // tm,),
        in_specs=[pl.BlockSpec((tm, D), lambda i: (i, 0)), _vec(D_REST, D), _vec(1, HD)],
        out_specs=[pl.BlockSpec((tm, D_REST), lambda i: (i, 0)), pl.BlockSpec((NH, tm, HD), lambda i: (0, i, 0))],
        out_shape=[jax.ShapeDtypeStruct((s, D_REST), F32), jax.ShapeDtypeStruct((NH, s, HD), BF16)],
        scratch_shapes=[pltpu.VMEM((tm, HD), F32), pltpu.VMEM((8, HD), F32)],
        compiler_params=_params(dimension_semantics=("arbitrary",)),
    )(xn, w_rest_t, bf_pad)


def _attn_fwd(zq, kx):
    s = zq.shape[0]
    n = s // TQ
    nb = TQ // HD

    def body(q_ref, k_ref, v_ref, kx_ref, o_ref, ax_ref):
        i = pl.program_id(1)
        lane = lax.broadcasted_iota(jnp.int32, (TQ, HD), 1)
        row = lax.broadcasted_iota(jnp.int32, (TQ, HD), 0)
        qa = jnp.concatenate([q_ref[...], jnp.where(lane < 3, 1.0, 0.0).astype(BF16)], axis=1)

        def step(j, carry, masked):
            m, l, acc = carry
            rows = pl.ds(pl.multiple_of(j * TQ, TQ), TQ)
            ka = jnp.concatenate([k_ref[rows, :], kx_ref[0, rows, :]], axis=1)
            v_all = v_ref[rows, :]
            rp = TQ // ROW_PARTS
            parts = [slice(rp * t, rp * (t + 1)) for t in range(ROW_PARTS)]
            keys = [rp * (t + 1) if masked else TQ for t in range(ROW_PARTS)]
            u_parts = [lax.dot_general(qa[part], ka[:kn], NT_DIMS, preferred_element_type=F32)
                       for part, kn in zip(parts, keys)]
            out = []
            for t, (part, u, kn) in enumerate(zip(parts, u_parts, keys)):
                us = [u[:, HD * b:HD * (b + 1)] for b in range(kn // HD)]
                if masked:
                    us = [ub if HD * (b + 1) <= rp * t else jnp.where(row[part] >= lane[part] + HD * b, ub, NEG)
                          for b, ub in enumerate(us)]
                v = v_all[:kn]
                bm = functools.reduce(jnp.maximum, us)
                m_new = jnp.maximum(m[part], jnp.max(bm, axis=1, keepdims=True))
                alpha = jnp.exp2((m[part] - m_new) * EXP2_SCALE)
                shift = m_new * EXP2_SCALE
                ps = [jnp.exp2(ub * EXP2_SCALE - shift) for ub in us]
                l_new = alpha * l[part] + functools.reduce(jnp.add, ps)
                pr = jnp.concatenate(ps, axis=1).astype(BF16)
                out.append((m_new, l_new, alpha * acc[part] + jnp.dot(pr, v, preferred_element_type=F32)))
            return tuple(jnp.concatenate([o[t] for o in out], axis=0) for t in range(3))

        init = (jnp.full((TQ, HD), NEG, F32), jnp.zeros((TQ, HD), F32), jnp.zeros((TQ, HD), F32))
        carry = lax.fori_loop(0, i, lambda j, cr: step(j, cr, False), init)
        m, l, acc = step(i, carry, True)
        l_row = jnp.sum(l, axis=1, keepdims=True)
        o_ref[...] = acc / l_row
        ax_ref[0] = _bias_lanes(-(m + jnp.log(l_row) * (1.0 / SCALE)), 3, 0)

    return pl.pallas_call(
        body, name="attn_fwd", grid=(NH, n),
        in_specs=[
            pl.BlockSpec((TQ, HD), lambda h, i: (i, h)),
            pl.BlockSpec((s, HD), lambda h, i: (0, NH + h)),
            pl.BlockSpec((s, HD), lambda h, i: (0, 2 * NH + h)),
            pl.BlockSpec((1, s, HD), lambda h, i: (h, 0, 0)),
        ],
        out_specs=[pl.BlockSpec((TQ, HD), lambda h, i: (i, h)), pl.BlockSpec((1, TQ, HD), lambda h, i: (h, i, 0))],
        out_shape=[jax.ShapeDtypeStruct((s, D), F32), jax.ShapeDtypeStruct((NH, s, HD), BF16)],
        compiler_params=_params(dimension_semantics=("parallel", "parallel")),
    )(zq, zq, zq, kx)


def _attn_bwd(zq, do, ax, delta, kx, after):
    s = zq.shape[0]
    n = s // TQ
    nb = TQ // HD

    def body(k_ref, v_ref, kx_ref, q_ref, ax_ref, do_ref, dl_ref, after_ref, dq_out, dk_ref, dv_ref, dcs_ref, drs_ref,
             dq_ref):
        j = pl.program_id(1)

        @pl.when(j == 0)
        def _():
            dq_ref[...] = jnp.zeros_like(dq_ref)
            drs_ref[...] = jnp.zeros_like(drs_ref)

        k = k_ref[...]
        v = v_ref[...]
        ka = jnp.concatenate([k, kx_ref[0]], axis=1)
        row = lax.broadcasted_iota(jnp.int32, (TQ, HD), 0)
        lane = lax.broadcasted_iota(jnp.int32, (TQ, HD), 1)

        def step(i, carry, r0, rn, kn, masked):
            dk, dv, dcs = carry
            rows = pl.ds(pl.multiple_of(i * TQ + r0, rn), rn)
            q = q_ref[rows, :]
            dout = do_ref[rows, :]
            dlv = dl_ref[0, rows, :]
            qa = jnp.concatenate([q, ax_ref[0, rows, :]], axis=1)
            u = lax.dot_general(qa, ka[:kn], NT_DIMS, preferred_element_type=F32)
            dp = lax.dot_general(dout, v[:kn], NT_DIMS, preferred_element_type=F32)
            prs, dss = [], []
            for b in range(kn // HD):
                cs = slice(HD * b, HD * (b + 1))
                ub = u[:, cs]
                if masked and HD * (b + 1) > r0:
                    ub = jnp.where(row[:rn] + r0 >= lane[:rn] + HD * b, ub, NEG)
                pb = jnp.exp2(ub * EXP2_SCALE)
                prs.append(pb)
                dss.append(pb * (dp[:, cs] - dlv))
            drs_ref[0, rows, :] += functools.reduce(jnp.add, dss)
            ds = jnp.concatenate(dss, axis=1)
            dsb = ds.astype(BF16)
            dcs_new = jnp.sum(ds.reshape(rn // 8, 8, kn), axis=0)
            dv_new = lax.dot_general(jnp.concatenate(prs, axis=1).astype(BF16), dout, TN_DIMS, preferred_element_type=F32)
            dk_new = lax.dot_general(dsb, q, TN_DIMS, preferred_element_type=F32)
            if kn < TQ:
                dcs_new = jnp.concatenate([dcs_new, jnp.zeros((8, TQ - kn), F32)], axis=1)
                dv_new = jnp.concatenate([dv_new, jnp.zeros((TQ - kn, HD), F32)], axis=0)
                dk_new = jnp.concatenate([dk_new, jnp.zeros((TQ - kn, HD), F32)], axis=0)
            dq_ref[rows, :] += jnp.dot(dsb, k[:kn], preferred_element_type=F32) * SCALE
            return dk + dk_new, dv + dv_new, dcs + dcs_new

        carry = (jnp.zeros((TQ, HD), F32), jnp.zeros((TQ, HD), F32), jnp.zeros((8, TQ), F32))
        rp = TQ // ROW_PARTS
        for t in range(ROW_PARTS):
            carry = step(j, carry, rp * t, rp, rp * (t + 1), True)
        dk, dv, dcs = lax.fori_loop(j + 1, n, lambda i, cr: step(i, cr, 0, TQ, TQ, False), carry)
        dk_ref[...] = (dk * SCALE).astype(BF16)
        dv_ref[...] = dv.astype(BF16)
        dcs_ref[0] = jnp.broadcast_to(_colsum(dcs), (8, TQ))

        @pl.when(j == n - 1)
        def _():
            dq_out[...] = dq_ref[...].astype(BF16)

    return pl.pallas_call(
        body, name="attn_bwd", grid=(NH, n),
        in_specs=[
            pl.BlockSpec((TQ, HD), lambda h, j: (j, NH + h)),
            pl.BlockSpec((TQ, HD), lambda h, j: (j, 2 * NH + h)),
            pl.BlockSpec((1, TQ, HD), lambda h, j: (h, j, 0)),
            pl.BlockSpec((s, HD), lambda h, j: (0, h)),
            pl.BlockSpec((1, s, HD), lambda h, j: (h, 0, 0)),
            pl.BlockSpec((s, HD), lambda h, j: (0, h)),
            pl.BlockSpec((1, s, HD), lambda h, j: (h, 0, 0)),
            pl.BlockSpec((8, HD), lambda h, j: (0, 0)),
        ],
        out_specs=[
            pl.BlockSpec((s, HD), lambda h, j: (0, h)),
            pl.BlockSpec((TQ, HD), lambda h, j: (j, h)),
            pl.BlockSpec((TQ, HD), lambda h, j: (j, h)),
            pl.BlockSpec((1, 8, TQ), lambda h, j: (j, h, 0)),
            pl.BlockSpec((1, s, HD), lambda h, j: (h, 0, 0)),
        ],
        out_shape=[
            jax.ShapeDtypeStruct((s, D), BF16),
            jax.ShapeDtypeStruct((s, D), BF16),
            jax.ShapeDtypeStruct((s, D), BF16),
            jax.ShapeDtypeStruct((n, 8 * NH, TQ), F32),
            jax.ShapeDtypeStruct((NH, s, HD), F32),
        ],
        scratch_shapes=[pltpu.VMEM((s, HD), F32)],
        compiler_params=_params(dimension_semantics=("parallel", "arbitrary")),
    )(zq, zq, kx, zq, ax, do, delta, after)


def _forget_bwd(dcs, drs, zr, bf_pad):
    n = dcs.shape[0]
    s = n * TQ

    def body(dcs_ref, drs_ref, fl_ref, b_ref, dfl_ref, gb_ref, buf, carry):
        i = pl.program_id(0)

        @pl.when(i == 0)
        def _():
            carry[...] = jnp.zeros_like(carry)
            gb_ref[...] = jnp.zeros_like(gb_ref)

        dc_t = jnp.concatenate([dcs_ref[0], jnp.zeros((HD - 8 * NH, TQ), F32)], axis=0)
        lane = lax.broadcasted_iota(jnp.int32, (TQ, HD), 1)
        dc = -dc_t.T
        for hh in range(NH):
            dc = dc + jnp.where(lane == 8 * hh, jnp.sum(drs_ref[hh], axis=1, keepdims=True), 0.0)
        buf[...] = _cumsum_bwd(dc) + carry[0:1, :]
        carry[0:1, :] = buf[0:1, :]
        fl = fl_ref[...] + b_ref[...]
        dfl = buf[...] * _sigmoid_rel(-fl)
        dfl_ref[...] = dfl.astype(BF16)
        gb_ref[...] += _colsum(dfl)

    return pl.pallas_call(
        body, name="forget_bwd", grid=(n,),
        in_specs=[
            pl.BlockSpec((1, 8 * NH, TQ), lambda i: (n - 1 - i, 0, 0)),
            pl.BlockSpec((NH, TQ, HD), lambda i: (0, n - 1 - i, 0)),
            pl.BlockSpec((TQ, HD), lambda i: (n - 1 - i, FL_COL // HD)),
            _vec(1, HD),
        ],
        out_specs=[pl.BlockSpec((TQ, HD), lambda i: (n - 1 - i, 0)), _vec(1, HD)],
        out_shape=[jax.ShapeDtypeStruct((s, HD), BF16), jax.ShapeDtypeStruct((1, HD), F32)],
        scratch_shapes=[pltpu.VMEM((TQ, HD), F32), pltpu.VMEM((8, HD), F32)],
        compiler_params=_params(dimension_semantics=("arbitrary",)),
    )(dcs, drs, zr, bf_pad)


def _gates(xc, w_ref, b, sigmoid):
    xb = xc.astype(BF16)
    pre = jnp.concatenate(
        [jnp.dot(xb[:, HD * g:HD * (g + 1)], w_ref[g], preferred_element_type=F32) for g in range(NH)], axis=1)
    return sigmoid(pre + b)


def _lru_coeffs(r, lam):
    sp = jnp.maximum(-lam, 0.0) + jnp.log(1.0 + jnp.exp(-jnp.abs(lam)))
    log_a = -LRU_C * r * sp
    a = jnp.exp(log_a)
    y = 2.0 * log_a
    em1 = jnp.where(jnp.abs(y) < 0.01, y * (1.0 + y * (0.5 + y * (1.0 / 6.0))), jnp.exp(y) - 1.0)
    em = -em1
    inv_gam = lax.rsqrt(jnp.maximum(em, 1e-37))
    return sp, a, em * inv_gam, inv_gam


def _conv_taps(ext, t):
    return [_shift_down(ext, CONV_W - 1 - jj, t) for jj in range(CONV_W)]


def _lru_fwd(zr, conv_w8, conv_b, w_r, b_r, w_i, b_i, lam):
    s = zr.shape[0]
    n = s // TS
    xl_col = 1

    def body(xl_ref, halo_ref, cw_ref, cb_ref, wr_ref, br_ref, wi_ref, bi_ref, lam_ref, xc_ref, h_ref, carry):
        i = pl.program_id(0)

        @pl.when(i == 0)
        def _():
            carry[...] = jnp.zeros_like(carry)

        halo = jnp.where(i == 0, 0.0, halo_ref[...])
        taps = _conv_taps(jnp.concatenate([halo, xl_ref[...]], axis=0), TS)
        xc = cb_ref[...] + sum(cw_ref[jj:jj + 1, :] * taps[jj] for jj in range(CONV_W))
        xc_ref[...] = xc
        r = _gates(xc, wr_ref, br_ref[...], _sigmoid_rel)
        ig = _gates(xc, wi_ref, bi_ref[...], _sigmoid)
        _, a, gam, _ = _lru_coeffs(r, lam_ref[...])
        a_cum, h_loc = _scan(a, gam * (ig * xc), False)
        h_ref[...] = h_loc + a_cum * carry[0:1, :]
        carry[0:1, :] = h_ref[TS - 1:TS, :]

    return pl.pallas_call(
        body, name="lru_fwd", grid=(n,),
        in_specs=[
            _row(D, xl_col),
            pl.BlockSpec((8, D), lambda i: (jnp.maximum(i * (TS // 8) - 1, 0), xl_col)),
            _vec(8, D), _vec(1, D),
            pl.BlockSpec((NH, HD, HD), lambda i: (0, 0, 0)), _vec(1, D),
            pl.BlockSpec((NH, HD, HD), lambda i: (0, 0, 0)), _vec(1, D),
            _vec(1, D),
        ],
        out_specs=[_row(D), _row(D)],
        out_shape=[jax.ShapeDtypeStruct((s, D), F32), jax.ShapeDtypeStruct((s, D), F32)],
        scratch_shapes=[pltpu.VMEM((8, D), F32)],
        compiler_params=_params(dimension_semantics=("arbitrary",)),
    )(zr, zr, conv_w8, conv_b, w_r, b_r, w_i, b_i, lam)


def _lru_bwd(zr, xc, h, dh, conv_w8, w_r, b_r, w_i, b_i, lam):
    s = zr.shape[0]
    n = s // TS
    xl_col = 1

    def rev(i):
        return n - 1 - i

    def body(xl_ref, xlh_ref, xc_ref, h_ref, hh_ref, dh_ref, cw_ref, wr_ref, br_ref, wi_ref, bi_ref, lam_ref,
             dxl_ref, gwr_ref, gwi_ref, gbr_ref, gbi_ref, glam_ref, gcb_ref, gcw_ref, l_buf, dxc_buf, carry_g, carry_dxc):
        i = pl.program_id(0)
        first = rev(i) == 0

        @pl.when(i == 0)
        def _():
            carry_g[...] = jnp.zeros_like(carry_g)
            carry_dxc[...] = jnp.zeros_like(carry_dxc)
            for ref in (gwr_ref, gwi_ref, gbr_ref, gbi_ref, glam_ref, gcb_ref, gcw_ref):
                ref[...] = jnp.zeros_like(ref)

        rows = _rows_iota(TS)
        xc = xc_ref[...]
        lam = lam_ref[...]
        r = _gates(xc, wr_ref, br_ref[...], _sigmoid_rel)
        ig = _gates(xc, wi_ref, bi_ref[...], _sigmoid)
        sp, a, gam, inv_gam = _lru_coeffs(r, lam)
        g = dh_ref[...] + jnp.where(rows == TS - 1, carry_g[0:1, :], 0.0)
        b = jnp.where(rows == TS - 1, 0.0, pltpu.roll(a, TS - 1, 0))
        l_buf[...] = _scan(b, g, True)[1]
        lv = l_buf[...]
        carry_g[0:1, :] = l_buf[0:1, :] * a[0:1, :]
        h_prev_row = jnp.where(first, 0.0, hh_ref[7:8, :])
        h_prev = jnp.where(rows == 0, h_prev_row, pltpu.roll(h_ref[...], 1, 0))
        dgam = lv * ig * xc
        dig = lv * gam * xc
        dxc = lv * gam * ig
        dla = lv * h_prev * a - dgam * (a * a) * inv_gam
        dr = dla * (-LRU_C) * sp
        glam_ref[...] += _colsum(dla * r) * (LRU_C * _sigmoid_rel(-lam))
        dpr = dr * r * (1.0 - r)
        dpi = dig * ig * (1.0 - ig)
        gbr_ref[...] += _colsum(dpr)
        gbi_ref[...] += _colsum(dpi)
        xb = xc.astype(BF16)
        dprb = dpr.astype(BF16)
        dpib = dpi.astype(BF16)
        back = []
        for gi in range(NH):
            cs = slice(HD * gi, HD * (gi + 1))
            gwr_ref[gi] += lax.dot_general(xb[:, cs], dprb[:, cs], TN_DIMS, preferred_element_type=F32)
            gwi_ref[gi] += lax.dot_general(xb[:, cs], dpib[:, cs], TN_DIMS, preferred_element_type=F32)
            back.append(lax.dot_general(dprb[:, cs], wr_ref[gi], NT_DIMS, preferred_element_type=F32)
                        + lax.dot_general(dpib[:, cs], wi_ref[gi], NT_DIMS, preferred_element_type=F32))
        dxc = dxc + jnp.concatenate(back, axis=1)
        dxc_buf[...] = dxc
        gcb_ref[...] += _colsum(dxc)
        halo = jnp.where(first, 0.0, xlh_ref[...])
        taps = _conv_taps(jnp.concatenate([halo, xl_ref[...]], axis=0), TS)
        for jj in range(CONV_W):
            gcw_ref[jj:jj + 1, :] += _colsum(dxc * taps[jj])
        ext = jnp.concatenate([dxc, carry_dxc[...]], axis=0)
        dxl = sum(cw_ref[jj:jj + 1, :] * _shift_up(ext, CONV_W - 1 - jj, TS) for jj in range(CONV_W))
        dxl_ref[...] = dxl.astype(BF16)
        carry_dxc[...] = dxc_buf[0:8, :]

    rowr = lambda c, col=0: pl.BlockSpec((TS, c), lambda i: (rev(i), col))
    halo = lambda col: pl.BlockSpec((8, D), lambda i: (jnp.maximum(rev(i) * (TS // 8) - 1, 0), col))
    gate_w = pl.BlockSpec((NH, HD, HD), lambda i: (0, 0, 0))
    return pl.pallas_call(
        body, name="lru_bwd", grid=(n,),
        in_specs=[rowr(D, xl_col), halo(xl_col), rowr(D), rowr(D), halo(0), rowr(D),
                  _vec(8, D), gate_w, _vec(1, D), gate_w, _vec(1, D), _vec(1, D)],
        out_specs=[rowr(D), gate_w, gate_w, _vec(1, D), _vec(1, D), _vec(1, D), _vec(1, D), _vec(8, D)],
        out_shape=[
            jax.ShapeDtypeStruct((s, D), BF16),
            jax.ShapeDtypeStruct((NH, HD, HD), F32), jax.ShapeDtypeStruct((NH, HD, HD), F32),
            jax.ShapeDtypeStruct((1, D), F32), jax.ShapeDtypeStruct((1, D), F32), jax.ShapeDtypeStruct((1, D), F32),
            jax.ShapeDtypeStruct((1, D), F32), jax.ShapeDtypeStruct((8, D), F32),
        ],
        scratch_shapes=[pltpu.VMEM((TS, D), F32), pltpu.VMEM((TS, D), F32), pltpu.VMEM((8, D), F32), pltpu.VMEM((8, D), F32)],
        compiler_params=_params(dimension_semantics=("arbitrary",)),
    )(zr, zr, xc, h, h, dh, conv_w8, w_r, b_r, w_i, b_i, lam)


def _silu_parts(g):
    sg = _sigmoid(g)
    return g * sg, sg * (1.0 + g * (1.0 - sg))


def _branch_out_bwd(o, h, zr, dmix, w_out, gain_a, gain_l):
    s = o.shape[0]

    def body(o_ref, ga_ref, h_ref, gl_ref, dm_ref, w_ref, ka_ref, kl_ref,
             do_ref, dl_ref, dga_ref, dh_ref, dgl_ref, gka_ref, gkl_ref):
        @pl.when(pl.program_id(0) == 0)
        def _():
            gka_ref[...] = jnp.zeros_like(gka_ref)
            gkl_ref[...] = jnp.zeros_like(gkl_ref)

        dycat = lax.dot_general(dm_ref[...], w_ref[...], NT_DIMS, preferred_element_type=F32)

        def one(v, g, dy, gain):
            vhat, rstd = _rms_fwd(v)
            sg, dsg = _silu_parts(g)
            dn = dy * sg
            dg = dy * (vhat * gain) * dsg
            return _rms_bwd(vhat, rstd, dn * gain), dg, _colsum(dn * vhat)

        o = o_ref[...]
        dout, dga, gka = one(o, ga_ref[...], dycat[:, :D], ka_ref[...])
        do_ref[...] = dout.astype(BF16)
        dga_ref[...] = dga.astype(BF16)
        gka_ref[...] += gka
        prod = dout * o
        for hh in range(NH):
            dl_ref[hh] = jnp.broadcast_to(jnp.sum(prod[:, HD * hh:HD * (hh + 1)], axis=1, keepdims=True), (TS, HD))
        dh, dgl, gkl = one(h_ref[...], gl_ref[...], dycat[:, D:], kl_ref[...])
        dh_ref[...] = dh
        dgl_ref[...] = dgl.astype(BF16)
        gkl_ref[...] += gkl

    return pl.pallas_call(
        body, name="branch_out_bwd", grid=(s // TS,),
        in_specs=[_row(D), _row(D, 0), _row(D), _row(D, 2), _row(D), _vec(2 * D, D), _vec(1, D), _vec(1, D)],
        out_specs=[_row(D), pl.BlockSpec((NH, TS, HD), lambda i: (0, i, 0)), _row(D), _row(D), _row(D), _vec(1, D), _vec(1, D)],
        out_shape=[
            jax.ShapeDtypeStruct((s, D), BF16), jax.ShapeDtypeStruct((NH, s, HD), F32), jax.ShapeDtypeStruct((s, D), BF16),
            jax.ShapeDtypeStruct((s, D), F32), jax.ShapeDtypeStruct((s, D), BF16),
            jax.ShapeDtypeStruct((1, D), F32), jax.ShapeDtypeStruct((1, D), F32),
        ],
        compiler_params=_params(dimension_semantics=("arbitrary",)),
    )(o, zr, h, zr, dmix, w_out, gain_a, gain_l)


def _residual(x, o, h, zr, gain_a, gain_l, w_out, post_gain):
    s = x.shape[0]

    def body(x_ref, o_ref, ga_ref, h_ref, gl_ref, ka_ref, kl_ref, w_ref, g_ref, y_ref, m_ref, h1_ref, hb_ref):
        ohat, _ = _rms_fwd(o_ref[...])
        y_ref[:, 0:D] = (ohat * ka_ref[...] * _silu_parts(ga_ref[...])[0]).astype(BF16)
        hhat, _ = _rms_fwd(h_ref[...])
        y_ref[:, D:2 * D] = (hhat * kl_ref[...] * _silu_parts(gl_ref[...])[0]).astype(BF16)
        mix = jnp.dot(y_ref[...], w_ref[...], preferred_element_type=F32)
        m_ref[...] = mix
        mhat, _ = _rms_fwd(mix)
        h1 = x_ref[...] + mhat * g_ref[...]
        h1_ref[...] = h1
        hb_ref[...] = h1.astype(BF16)

    return pl.pallas_call(
        body, name="residual", grid=(s // TS,),
        in_specs=[_row(D), _row(D), _row(D, 0), _row(D), _row(D, 2), _vec(1, D), _vec(1, D), _vec(2 * D, D), _vec(1, D)],
        out_specs=[_row(2 * D), _row(D), _row(D), _row(D)],
        out_shape=[jax.ShapeDtypeStruct((s, 2 * D), BF16), jax.ShapeDtypeStruct((s, D), F32),
                   jax.ShapeDtypeStruct((s, D), F32), jax.ShapeDtypeStruct((s, D), BF16)],
        compiler_params=_params(dimension_semantics=("parallel",)),
    )(x, o, zr, h, zr, gain_a, gain_l, w_out, post_gain)


def _head(h1, p, tgt, mix, w_gate, w_ple, ple_gain, b_gate, post_gain):
    s = h1.shape[0]

    def body(h_ref, p_ref, t_ref, m_ref, wg_ref, wp_ref, kg_ref, b_ref, pg_ref,
             loss_ref, dgp_ref, dpe_ref, dh_ref, dm_ref, gk_ref, gb_ref, gg_ref):
        @pl.when(pl.program_id(0) == 0)
        def _():
            for ref in (loss_ref, gk_ref, gb_ref, gg_ref):
                ref[...] = jnp.zeros_like(ref)

        h1 = h_ref[...]
        pe = jnp.dot(p_ref[...].astype(BF16), wp_ref[...], preferred_element_type=F32)
        gp = jnp.dot(h1.astype(BF16), wg_ref[...], preferred_element_type=F32)
        ehat, rstd = _rms_fwd(pe)
        e = ehat * kg_ref[...]
        gate = _sigmoid(gp + b_ref[...])
        diff = (h1 + gate * e) - t_ref[...]
        per_row = jnp.mean(diff * diff, axis=-1, keepdims=True)
        loss_ref[...] += 0.5 * jnp.sum(per_row, axis=0, keepdims=True)
        dy = diff * (1.0 / D)
        dgp = dy * e * gate * (1.0 - gate)
        dgpb = dgp.astype(BF16)
        dgp_ref[...] = dgpb
        gb_ref[...] += _colsum(dgp)
        de = dy * gate
        gk_ref[...] += _colsum(de * ehat)
        dpe_ref[...] = _rms_bwd(ehat, rstd, de * kg_ref[...]).astype(BF16)
        dh1 = dy + lax.dot_general(dgpb, wg_ref[...], NT_DIMS, preferred_element_type=F32)
        dh_ref[...] = dh1
        mhat, rstd_m = _rms_fwd(m_ref[...])
        gg_ref[...] += _colsum(dh1 * mhat)
        dm_ref[...] = _rms_bwd(mhat, rstd_m, dh1 * pg_ref[...]).astype(BF16)

    return pl.pallas_call(
        body, name="head", grid=(s // TS,),
        in_specs=[_row(D), _row(D_PLE), _row(D), _row(D), _vec(D, D), _vec(D_PLE, D), _vec(1, D), _vec(1, D), _vec(1, D)],
        out_specs=[_vec(1, 1), _row(D), _row(D), _row(D), _row(D), _vec(1, D), _vec(1, D), _vec(1, D)],
        out_shape=[
            jax.ShapeDtypeStruct((1, 1), F32), jax.ShapeDtypeStruct((s, D), BF16), jax.ShapeDtypeStruct((s, D), BF16),
            jax.ShapeDtypeStruct((s, D), F32), jax.ShapeDtypeStruct((s, D), BF16),
            jax.ShapeDtypeStruct((1, D), F32), jax.ShapeDtypeStruct((1, D), F32), jax.ShapeDtypeStruct((1, D), F32),
        ],
        compiler_params=_params(dimension_semantics=("arbitrary",)),
    )(h1, p, tgt, mix, w_gate, w_ple, ple_gain, b_gate, post_gain)


def _prenorm_bwd(x, dxn_a, dz_rest, w_rest_t, dh1, pre_gain, after):
    s = x.shape[0]
    ks = [a.shape[1] for a in dz_rest]
    assert sum(ks) == w_rest_t.shape[0]
    nz = len(dz_rest)
    tm = 512
    rowm = lambda c: pl.BlockSpec((tm, c), lambda i: (i, 0))

    def body(*refs):
        x_ref, da_ref = refs[:2]
        w_ref, dh_ref, g_ref, after_ref, dx_ref, gg_ref = refs[2 + nz:]

        @pl.when(pl.program_id(0) == 0)
        def _():
            gg_ref[...] = jnp.zeros_like(gg_ref)

        dxn, k0 = da_ref[...], 0
        for dz_ref, kw in zip(refs[2:2 + nz], ks):
            dxn = dxn + jnp.dot(dz_ref[...], w_ref[k0:k0 + kw, :], preferred_element_type=F32)
            k0 += kw
        xhat, rstd = _rms_fwd(x_ref[...])
        gg_ref[...] += _colsum(dxn * xhat)
        dx_ref[...] = dh_ref[...] + _rms_bwd(xhat, rstd, dxn * g_ref[...])

    return pl.pallas_call(
        body, name="prenorm_bwd", grid=(s // tm,),
        in_specs=[rowm(D), rowm(D)] + [rowm(kw) for kw in ks] + [_vec(*w_rest_t.shape), rowm(D), _vec(1, D), _vec(8, HD)],
        out_specs=[rowm(D), _vec(1, D)],
        out_shape=[jax.ShapeDtypeStruct((s, D), F32), jax.ShapeDtypeStruct((1, D), F32)],
        compiler_params=_params(dimension_semantics=("arbitrary",)),
    )(x, dxn_a, *dz_rest, w_rest_t, dh1, pre_gain, after)


def _adamw(name, parts, w, m, v, own=None, me=None, parts_hi=None, hi_from=None):
    r, c = w.shape
    if r % 8 == 0:
        tr = _pick(r, (256, 128, 16, 8))
        grid = (r // tr,)
        blk = pl.BlockSpec((tr, c), lambda i: (i, 0))
        parts_blk = pl.BlockSpec((N_DEV, tr, c), lambda i: (0, i, 0))
    else:
        tc = _pick(c, (256, 128))
        grid = (c // tc,)
        blk = pl.BlockSpec((r, tc), lambda i: (0, i))
        parts_blk = pl.BlockSpec((N_DEV, r, tc), lambda i: (0, 0, i))

    def body(*refs):
        p_ref, w_ref, m_ref, v_ref = refs[:4]
        g_ref, d_ref, nm_ref, nv_ref = refs[-4:]
        if own is None:
            g = p_ref[0].astype(F32)
            for j in range(1, N_DEV):
                g = g + p_ref[j].astype(F32)
            g_ref[...] = g
        else:
            own_ref, me_ref = refs[4:6]
            g_ref[...] = jnp.zeros_like(g_ref)
            for j in range(N_DEV):
                @pl.when(me_ref[0] == j)
                def _():
                    g_ref[...] += own_ref[...].astype(F32)

                if parts_hi is None:
                    @pl.when(me_ref[0] != j)
                    def _():
                        g_ref[...] += p_ref[j].astype(F32)
                else:
                    @pl.when((me_ref[0] != j) & (me_ref[0] < hi_from))
                    def _():
                        g_ref[...] += p_ref[j].astype(F32)

                    @pl.when((me_ref[0] != j) & (me_ref[0] >= hi_from))
                    def _():
                        g_ref[...] += refs[6][j].astype(F32)
            g = g_ref[...]
        nm = ADAM_B1 * m_ref[...] + (1.0 - ADAM_B1) * g
        nv = ADAM_B2 * v_ref[...] + (1.0 - ADAM_B2) * (g * g)
        nm_ref[...] = nm
        nv_ref[...] = nv
        m_hat = nm / (1.0 - ADAM_B1 ** ADAM_STEP)
        v_hat = nv / (1.0 - ADAM_B2 ** ADAM_STEP)
        d_ref[...] = -ADAM_LR * (m_hat / (jnp.sqrt(v_hat) + ADAM_EPS) + ADAM_WD * w_ref[...])

    in_specs, args = [parts_blk, blk, blk, blk], [parts, w, m, v]
    if own is not None:
        in_specs += [blk, pl.BlockSpec(memory_space=pltpu.SMEM)]
        args += [own, me]
    if parts_hi is not None:
        in_specs.append(parts_blk)
        args.append(parts_hi)
    return pl.pallas_call(
        body, name=name, grid=grid,
        in_specs=in_specs,
        out_specs=[blk] * 4,
        out_shape=[jax.ShapeDtypeStruct((r, c), F32)] * 4,
        compiler_params=_params(dimension_semantics=("parallel",)),
    )(*args)


def _spread8(v):
    r = v.shape[0]
    return jnp.pad(jnp.pad(v[:, :, None], ((0, 0), (0, 0), (0, 7))).reshape(r, 8 * NH), ((0, 0), (0, HD - 8 * NH)))


def _gather8(v):
    return v[:, :8 * NH].reshape(v.shape[0], NH, 8)[:, :, 0]


def _cols_to_shards(g):
    r, c8 = g.shape
    return g.reshape(r, N_DEV, c8 // N_DEV).transpose(1, 0, 2)


def _shards_to_cols(g):
    n, r, c = g.shape
    return g.transpose(1, 0, 2).reshape(r, n * c)


def kernel(x, p, w_in, b_f, pre_gain, post_gain, conv_w, conv_b, w_rgate, b_rgate, w_igate, b_igate, lru_lambda, attn_out_gain, lru_out_gain, w_out, w_ple, ple_gain, w_ple_gate, b_ple_gate, loss_target, m_w_in, m_b_f, m_pre_gain, m_post_gain, m_conv_w, m_conv_b, m_w_rgate, m_b_rgate, m_w_igate, m_b_igate, m_lru_lambda, m_attn_out_gain, m_lru_out_gain, m_w_out, m_w_ple, m_ple_gain, m_w_ple_gate, m_b_ple_gate, v_w_in, v_b_f, v_pre_gain, v_post_gain, v_conv_w, v_conv_b, v_w_rgate, v_b_rgate, v_w_igate, v_b_igate, v_lru_lambda, v_attn_out_gain, v_lru_out_gain, v_w_out, v_w_ple, v_ple_gain, v_w_ple_gate, v_b_ple_gate):
    me = 4 * lax.axis_index("x") + 2 * lax.axis_index("y") + lax.axis_index("c")
    x2, p2, tgt = x[0], p[0, 0], loss_target[0]

    conv_w_shard8 = jnp.pad(conv_w[0], ((0, 8 - CONV_W), (0, 0)))
    wt, m_wt, v_wt = w_in[0].T, m_w_in[0].T, v_w_in[0].T
    g_wint, g_conv = _gather_two_level("gather_w_in", [wt.astype(BF16), conv_w_shard8])
    win_t = g_wint.reshape(D_IN, D)
    rest_state, rest_token = _exchange_start(
        "gather_rest_start", [w_out[0].astype(BF16), w_ple[0].astype(BF16), w_ple_gate[0].astype(BF16)], ["bcast"] * 3,
        after=g_conv)
    w_rest_t = jnp.concatenate([win_t[D_QKV + NH:], _spread8(win_t[D_QKV:D_QKV + NH].T).T], axis=0)
    conv_w8 = _shards_to_cols(g_conv)
    bf_pad = _spread8(b_f)
    w_r, w_i = w_rgate[0].astype(BF16), w_igate[0].astype(BF16)

    xn, zq = _prenorm_proj(x2, pre_gain, win_t, D_QKV, rest_token)
    zr, kx = _proj_rest(xn, w_rest_t, bf_pad)
    o, ax = _attn_fwd(zq, kx)
    xc, h = _lru_fwd(zr, conv_w8, conv_b, w_r, b_rgate, w_i, b_igate, lru_lambda)
    g_wout, g_wple, g_wpg = _exchange_wait("gather_rest_wait", rest_state, h)
    wout_full = g_wout.reshape(2 * D, D)
    wple_full = _shards_to_cols(g_wple)
    wpg_full = g_wpg.reshape(D, D)
    ycat, mix, h1, h1b = _residual(x2, o, h, zr, attn_out_gain, lru_out_gain, wout_full, post_gain)

    loss_part, dgp, dpe, dh1, dmix, g_ple_gain, g_b_gate, g_post_gain = _head(
        h1, p2, tgt, mix, wpg_full, wple_full, ple_gain, b_ple_gate, post_gain)
    gw_pg = _mm("bwd_gate_w", h1b, dgp, "tn", BF16)
    gw_ple = _mm("bwd_ple_w", p2, dpe, "tn", BF16)
    gw_out = _mm("bwd_out_w", ycat, dmix, "tn", BF16)
    do, delta, dga, dh, dgl, g_aog, g_log = _branch_out_bwd(o, h, zr, dmix, wout_full, attn_out_gain, lru_out_gain)
    dxl, g_wr, g_wi, g_br, g_bi, g_lam, g_cb, g_cw8 = _lru_bwd(
        zr, xc, h, dh, conv_w8, w_r, b_rgate, w_i, b_igate, lru_lambda)
    gates = jnp.concatenate([g_wr.reshape(D, HD), g_wi.reshape(D, HD)], axis=0).astype(BF16)
    gw_rest = jnp.concatenate([_mm("bwd_w_" + nm, dz, xn, "tn", BF16) for nm, dz in
                               (("ga", dga), ("xl", dxl), ("gl", dgl))], axis=0)
    n_late = (D_QKV + NH + D_IN_SHARD - 1) // D_IN_SHARD
    rest_late = n_late * D_IN_SHARD - (D_QKV + NH)
    outw_state, outw_token = _exchange_start(
        "exchange_outw_start",
        [gw_out.reshape(N_DEV, 2 * D // N_DEV, D), _cols_to_shards(gw_ple), gw_pg.reshape(N_DEV, D // N_DEV, D), gates,
         gw_rest[rest_late:].reshape(N_DEV - n_late, D_IN_SHARD, D)],
        ["scatter"] * 3 + ["bcast", ("owners", n_late, N_DEV - n_late)])
    dq, dk, dv, dcs, drs = _attn_bwd(zq, do, ax, delta, kx, outw_token)
    dfl, g_bf_pad = _forget_bwd(dcs, drs, zr, bf_pad)
    gw_qkv = [_mm("bwd_w_" + nm, dz, xn, "tn", BF16) for nm, dz in (("q", dq), ("k", dk), ("v", dv), ("fl", dfl))]
    gw_qkv[3] = _gather8(gw_qkv[3].T).T
    gw_late = jnp.concatenate(gw_qkv + [gw_rest[:rest_late]], axis=0)
    inw_state, inw_token = _exchange_start(
        "exchange_inw_start", [gw_late.reshape(n_late, D_IN_SHARD, D)], [("owners", 0, n_late)])
    dxn_a = _mm_cat("bwd_qkv_x", [dq, dk, dv], win_t, F32, after=inw_token)
    grad_x, g_pre_gain = _prenorm_bwd(x2, dxn_a, [dga, dxl, dgl, dfl], w_rest_t, dh1, pre_gain, inw_token)

    upd = {}
    me1 = me.reshape(1).astype(jnp.int32)
    r_wout, r_wple, r_wpg, r_gates, r_win_hi = _exchange_wait("exchange_outw_wait", outw_state, grad_x, fill_own=False)
    upd["w_out"] = _adamw("adamw_w_out", r_wout[0], w_out[0], m_w_out[0], v_w_out[0], r_wout[1], me1)
    upd["w_ple"] = _adamw("adamw_w_ple", r_wple[0], w_ple[0], m_w_ple[0], v_w_ple[0], r_wple[1], me1)
    upd["w_ple_gate"] = _adamw("adamw_w_ple_gate", r_wpg[0], w_ple_gate[0], m_w_ple_gate[0], v_w_ple_gate[0], r_wpg[1], me1)
    gates_of = lambda a, b: jnp.concatenate([a[0].reshape(D, HD), b[0].reshape(D, HD)], axis=0)
    g_gates = _adamw("adamw_gates", r_gates[0], gates_of(w_rgate, w_igate), gates_of(m_w_rgate, m_w_igate),
                     gates_of(v_w_rgate, v_w_igate), r_gates[1], me1)
    upd["w_rgate"] = [a[:D].reshape(1, NH, HD, HD) for a in g_gates]
    upd["w_igate"] = [a[D:].reshape(1, NH, HD, HD) for a in g_gates]
    behind = upd["w_out"][0][0:1] + upd["w_ple_gate"][0][0:1] + jnp.pad(g_gates[0][0:1], ((0, 0), (0, D - HD)))
    small = jnp.concatenate(
        [jnp.pad(_gather8(g_bf_pad), ((0, 0), (0, D - NH))), g_pre_gain, g_post_gain, g_cb, g_br, g_bi, g_lam, g_aog, g_log,
         g_ple_gain, g_b_gate, g_cw8[:CONV_W], behind, jnp.pad(loss_part, ((0, 7), (0, D - 1)))], axis=0)
    (r_small,) = _exchange("exchange_small", [small], ["bcast"])
    vec_names = ["b_f", "pre_gain", "post_gain", "conv_b", "b_rgate", "b_igate", "lru_lambda", "attn_out_gain",
                 "lru_out_gain", "ple_gain", "b_ple_gate"]
    vec_w = dict(b_f=(b_f, m_b_f, v_b_f), pre_gain=(pre_gain, m_pre_gain, v_pre_gain),
                 post_gain=(post_gain, m_post_gain, v_post_gain), conv_b=(conv_b, m_conv_b, v_conv_b),
                 b_rgate=(b_rgate, m_b_rgate, v_b_rgate), b_igate=(b_igate, m_b_igate, v_b_igate),
                 lru_lambda=(lru_lambda, m_lru_lambda, v_lru_lambda),
                 attn_out_gain=(attn_out_gain, m_attn_out_gain, v_attn_out_gain),
                 lru_out_gain=(lru_out_gain, m_lru_out_gain, v_lru_out_gain), ple_gain=(ple_gain, m_ple_gain, v_ple_gain),
                 b_ple_gate=(b_ple_gate, m_b_ple_gate, v_b_ple_gate))
    conv_mine = lambda a: lax.dynamic_slice_in_dim(a, me * HD, HD, axis=1)

    def small_rows(k):
        rows = [jnp.pad(vec_w[nm][k], ((0, 0), (0, D - vec_w[nm][k].shape[1]))) for nm in vec_names]
        cw = (conv_w, m_conv_w, v_conv_w)[k][0]
        full = lax.dynamic_update_slice_in_dim(jnp.ones((CONV_W, D), F32), cw, me * HD, axis=1)
        return jnp.concatenate(rows + [full, jnp.ones((9, D), F32)], axis=0)

    g_small = _adamw("adamw_small", r_small, small_rows(0), small_rows(1), small_rows(2))
    loss = g_small[0][16, 0]
    for idx, nm in enumerate(vec_names):
        width = vec_w[nm][0].shape[1]
        upd[nm] = [a[idx:idx + 1, :width] for a in g_small]
    base = len(vec_names)
    upd["conv_w"] = [conv_mine(a[base:base + CONV_W])[None] for a in g_small]
    (r_win_lo,) = _exchange_wait("exchange_inw_wait", inw_state, g_small[0], fill_own=False)
    own_win = jnp.where(me < n_late, r_win_lo[1], r_win_hi[1])
    upd["w_in"] = [a.T for a in _adamw("adamw_w_in", r_win_lo[0], wt, m_wt, v_wt, own_win, me1,
                                       parts_hi=r_win_hi[0], hi_from=n_late)]
    for nm in ("w_in", "w_out", "w_ple", "w_ple_gate"):
        upd[nm] = [a[None] for a in upd[nm]]

    order = ["w_in", "b_f", "pre_gain", "post_gain", "conv_w", "conv_b", "w_rgate", "b_rgate", "w_igate", "b_igate",
             "lru_lambda", "attn_out_gain", "lru_out_gain", "w_out", "w_ple", "ple_gain", "w_ple_gate", "b_ple_gate"]
    outs = [loss, grad_x[None]]
    for k in range(4):
        outs += [upd[nm][k] for nm in order]
    return tuple(outs)
```

```python
import functools

import jax
import jax.numpy as jnp
from jax import lax
from jax.experimental import pallas as pl
from jax.experimental.pallas import tpu as pltpu

F32 = jnp.float32
BF16 = jnp.bfloat16

N_DEV = 8
D = 1024
HD = 128
NH = 8
D_IN = 6152
D_IN_SHARD = D_IN // N_DEV
D_QKV = 3 * D
D_REST = 3 * D + HD
FL_COL = 3 * D
D_PLE = 256
CONV_W = 4
LRU_C = 8.0
RMS_EPS = 1e-6
SCALE = HD ** -0.5
LOG2E = 1.4426950408889634
Q_SCALE = SCALE * LOG2E
NEG = -1e30

ADAM_LR = 0.001
ADAM_B1 = 0.9
ADAM_B2 = 0.999
ADAM_EPS = 1e-08
ADAM_WD = 0.01
ADAM_STEP = 10

TS = 256
TQ = 1024
ROW_PARTS = 2
VMEM_LIMIT = 48 * 1024 * 1024

NT_DIMS = (((1,), (1,)), ((), ()))
TN_DIMS = (((0,), (0,)), ((), ()))


def _params(**kw):
    return pltpu.CompilerParams(vmem_limit_bytes=VMEM_LIMIT, **kw)


def _sigmoid(v):
    return 0.5 * jnp.tanh(0.5 * v) + 0.5


def _sigmoid_rel(v):
    return 1.0 / (1.0 + jnp.exp(-v))


def _rms_fwd(v):
    rstd = lax.rsqrt(jnp.mean(v * v, axis=-1, keepdims=True) + RMS_EPS)
    return v * rstd, rstd


def _rms_bwd(vhat, rstd, dvhat):
    return rstd * (dvhat - vhat * jnp.mean(dvhat * vhat, axis=-1, keepdims=True))


def _colsum(v):
    return jnp.sum(v, axis=0, keepdims=True)


def _rows_iota(t):
    return lax.broadcasted_iota(jnp.int32, (t, 1), 0)


def _scan(a, u, reverse):
    t, c = a.shape
    rows = _rows_iota(t)
    d = 1
    while d < t:
        if d < 8:
            valid = rows < t - d if reverse else rows >= d
            shift = t - d if reverse else d
            u = jnp.where(valid, u + a * pltpu.roll(u, shift, 0), u)
            a = jnp.where(valid, a * pltpu.roll(a, shift, 0), a)
        else:
            zeros, ones = jnp.zeros((d, c), F32), jnp.ones((d, c), F32)
            if reverse:
                u_far, a_far = jnp.concatenate([u[d:], zeros], axis=0), jnp.concatenate([a[d:], ones], axis=0)
            else:
                u_far, a_far = jnp.concatenate([zeros, u[:t - d]], axis=0), jnp.concatenate([ones, a[:t - d]], axis=0)
            u = u + a * u_far
            a = a * a_far
        d *= 2
    return a, u


def _cumsum_fwd(v):
    t = v.shape[0]
    rows = _rows_iota(t)
    d = 1
    while d < t:
        v = jnp.where(rows >= d, v + pltpu.roll(v, d, 0), v)
        d *= 2
    return v


def _cumsum_bwd(v):
    t = v.shape[0]
    rows = _rows_iota(t)
    d = 1
    while d < t:
        v = jnp.where(rows < t - d, v + pltpu.roll(v, t - d, 0), v)
        d *= 2
    return v


def _bias_lanes(v, at, ones_at):
    lane = lax.broadcasted_iota(jnp.int32, v.shape, 1)
    hi = v.astype(BF16).astype(F32)
    mid = (v - hi).astype(BF16).astype(F32)
    lo = ((v - hi) - mid).astype(BF16).astype(F32)
    out = jnp.where((lane >= ones_at) & (lane < ones_at + 3), 1.0, 0.0)
    for k, piece in enumerate((hi, mid, lo)):
        out = jnp.where(lane == at + k, piece, out)
    return out.astype(BF16)


def _shift_down(ext, k, t):
    return pltpu.roll(ext, k, 0)[8:, :] if k else ext[8:, :]


def _shift_up(ext, k, t):
    return pltpu.roll(ext, t + 8 - k, 0)[:t, :] if k else ext[:t, :]


def _exchange(name, arrs, kinds):
    n = len(arrs)
    out_shape = []
    for a, kind in zip(arrs, kinds):
        shp = a.shape if kind == "scatter" else (N_DEV,) + a.shape
        out_shape.append(jax.ShapeDtypeStruct(shp, a.dtype))

    def body(*refs):
        ins, outs = refs[:n], refs[n:2 * n]
        send_sems, recv_sems, local_sems = refs[2 * n:]
        x, y, c = lax.axis_index("x"), lax.axis_index("y"), lax.axis_index("c")
        me = 4 * x + 2 * y + c
        copies = []
        for i in range(n):
            scatter = kinds[i] == "scatter"
            mine = pltpu.make_async_copy(ins[i].at[me] if scatter else ins[i], outs[i].at[me], local_sems.at[i])
            mine.start()
            copies.append(mine)
            for m in range(1, N_DEV):
                px = 1 - x if m & 4 else x
                py = 1 - y if m & 2 else y
                pc = 1 - c if m & 1 else c
                peer = 4 * px + 2 * py + pc
                cp = pltpu.make_async_remote_copy(
                    src_ref=ins[i].at[peer] if scatter else ins[i],
                    dst_ref=outs[i].at[me],
                    send_sem=send_sems.at[i, m - 1],
                    recv_sem=recv_sems.at[i, m - 1],
                    device_id=(px, py, pc),
                    device_id_type=pl.DeviceIdType.MESH,
                )
                cp.start()
                copies.append(cp)
        for cp in copies:
            cp.wait()

    any_spec = pl.BlockSpec(memory_space=pl.ANY)
    return pl.pallas_call(
        body,
        name=name,
        out_shape=out_shape,
        in_specs=[any_spec] * n,
        out_specs=[any_spec] * n,
        scratch_shapes=[
            pltpu.SemaphoreType.DMA((n, N_DEV - 1)),
            pltpu.SemaphoreType.DMA((n, N_DEV - 1)),
            pltpu.SemaphoreType.DMA((n,)),
        ],
        compiler_params=pltpu.CompilerParams(has_side_effects=True),
    )(*arrs)


def _gather_two_level(name, arrs, pieces=1):
    n = len(arrs)
    items = []
    for i, a in enumerate(arrs):
        rows = a.shape[0]
        if pieces > 1 and rows >= 512:
            step = -(-rows // (16 * pieces)) * 16
            items += [(i, r0, min(step, rows - r0)) for r0 in range(0, rows, step)]
        else:
            items.append((i, 0, rows))
    n_items = len(items)

    def body(*refs):
        ins, outs = refs[:n], refs[n:2 * n]
        send_sems, recv_sems, local_sems = refs[2 * n:]
        x, y, c = lax.axis_index("x"), lax.axis_index("y"), lax.axis_index("c")
        me, sibling = (x, y, c), (x, y, 1 - c)
        chips = [(1 - x, y), (x, 1 - y), (1 - x, 1 - y)]

        def rows_of(ref, t):
            i, r0, rn = items[t]
            return ref if rn == arrs[i].shape[0] else ref.at[pl.ds(r0, rn)]

        def slot(t, dev):
            return rows_of(outs[items[t][0]].at[4 * dev[0] + 2 * dev[1] + dev[2]], t)

        def copy(t, k, block, to, from_input=False):
            return pltpu.make_async_remote_copy(
                src_ref=rows_of(ins[items[t][0]], t) if from_input else slot(t, block), dst_ref=slot(t, block),
                send_sem=send_sems.at[t, k], recv_sem=recv_sems.at[t, k],
                device_id=to, device_id_type=pl.DeviceIdType.MESH)

        own, sent = [], []
        for t in range(n_items):
            mine = pltpu.make_async_copy(rows_of(ins[items[t][0]], t), slot(t, me), local_sems.at[t])
            mine.start()
            own.append(mine)
            first = [copy(t, 1 + j, me, (*chip, c), from_input=True) for j, chip in enumerate(chips)]
            first.append(copy(t, 0, me, sibling, from_input=True))
            for cp in first:
                cp.start()
            sent += first
        for t in range(n_items):
            for j, chip in enumerate(chips):
                copy(t, 1 + j, (*chip, c), me).wait_recv()
                fwd = copy(t, 4 + j, (*chip, c), sibling)
                fwd.start()
                sent.append(fwd)
        for t in range(n_items):
            copy(t, 0, sibling, me).wait_recv()
            for j, chip in enumerate(chips):
                copy(t, 4 + j, (*chip, 1 - c), me).wait_recv()
        for cp in sent:
            cp.wait_send()
        for cp in own:
            cp.wait()

    any_spec = pl.BlockSpec(memory_space=pl.ANY)
    return pl.pallas_call(
        body, name=name,
        out_shape=[jax.ShapeDtypeStruct((N_DEV,) + a.shape, a.dtype) for a in arrs],
        in_specs=[any_spec] * n, out_specs=[any_spec] * n,
        scratch_shapes=[pltpu.SemaphoreType.DMA((n_items, 7)), pltpu.SemaphoreType.DMA((n_items, 7)),
                        pltpu.SemaphoreType.DMA((n_items,))],
        compiler_params=pltpu.CompilerParams(has_side_effects=True),
    )(*arrs)


def _peers(x, y, c):
    out = []
    for m in range(1, N_DEV):
        px = 1 - x if m & 4 else x
        py = 1 - y if m & 2 else y
        pc = 1 - c if m & 1 else c
        out.append((m, (px, py, pc), 4 * px + 2 * py + pc))
    return out


def _split_copies(kinds, src_refs, land_refs, send_sems, recv_sems):
    x, y, c = lax.axis_index("x"), lax.axis_index("y"), lax.axis_index("c")
    me = 4 * x + 2 * y + c
    copies = []
    for i, kind in enumerate(kinds):
        for m, peer, pidx in _peers(x, y, c):
            copies.append(pltpu.make_async_remote_copy(
                src_ref=src_refs[i].at[pidx] if kind == "scatter" else src_refs[i],
                dst_ref=land_refs[i].at[me],
                send_sem=send_sems.at[i * (N_DEV - 1) + m - 1],
                recv_sem=recv_sems.at[i * (N_DEV - 1) + m - 1],
                device_id=peer,
                device_id_type=pl.DeviceIdType.MESH,
            ))
    return copies


_HBM_SPEC = pl.BlockSpec(memory_space=pltpu.HBM)
_SEM_SPEC = pl.BlockSpec(memory_space=pltpu.SEMAPHORE)
_DATAFLOW = pltpu.SideEffectType.DATAFLOW_SIDE_EFFECTING


def _exchange_start(name, arrs, kinds, after=None):
    n = len(arrs)
    extra = [] if after is None else [after]
    lands = []
    for a, kind in zip(arrs, kinds):
        shp = a.shape if kind == "scatter" else (N_DEV,) + a.shape
        lands.append(lax.empty(shp, a.dtype))

    def body(*refs):
        src_refs, land_refs = refs[:n], refs[n:2 * n]
        send_sems, recv_sems = refs[2 * n + len(extra):2 * n + len(extra) + 2]
        token = refs[-1]
        for cp in _split_copies(kinds, src_refs, land_refs, send_sems, recv_sems):
            cp.start()
        token[...] = jnp.zeros_like(token)

    n_sem = n * (N_DEV - 1)
    hbm = lambda a: pltpu.HBM(a.shape, a.dtype)
    res = pl.pallas_call(
        body, name=name,
        out_shape=(pltpu.SemaphoreType.DMA((n_sem,)), pltpu.SemaphoreType.DMA((n_sem,)),
                   *[hbm(a) for a in arrs], *[hbm(a) for a in lands], jax.ShapeDtypeStruct((8, HD), F32)),
        in_specs=[_HBM_SPEC] * (2 * n) + [pl.BlockSpec(memory_space=pl.ANY)] * len(extra),
        out_specs=(_SEM_SPEC, _SEM_SPEC, *[_HBM_SPEC] * (2 * n), pl.BlockSpec(memory_space=pltpu.VMEM)),
        input_output_aliases={i: 2 + i for i in range(2 * n)},
        compiler_params=pltpu.CompilerParams(has_side_effects=_DATAFLOW),
    )(*[pltpu.with_memory_space_constraint(a, pltpu.HBM) for a in arrs],
      *[pltpu.with_memory_space_constraint(a, pltpu.HBM) for a in lands], *extra)
    return (kinds, res[0], res[1], res[2:2 + n], res[2 + n:2 + 2 * n]), res[-1]


def _exchange_wait(name, state, after, fill_own=True):
    kinds, send_sems, recv_sems, srcs, lands = state
    n = len(srcs)

    def body(*refs):
        src_refs, land_refs = refs[:n], refs[n:2 * n]
        send_sems_ref, recv_sems_ref = refs[2 * n:2 * n + 2]
        for cp in _split_copies(kinds, src_refs, land_refs, send_sems_ref, recv_sems_ref):
            cp.wait_send()
            cp.wait_recv()

    res = pl.pallas_call(
        body, name=name,
        out_shape=tuple(pltpu.HBM(a.shape, a.dtype) for a in (*srcs, *lands)),
        in_specs=[_HBM_SPEC] * (2 * n) + [_SEM_SPEC, _SEM_SPEC, pl.BlockSpec(memory_space=pl.ANY)],
        out_specs=tuple([_HBM_SPEC] * (2 * n)),
        input_output_aliases={i: i for i in range(2 * n)},
        compiler_params=pltpu.CompilerParams(has_side_effects=_DATAFLOW),
    )(*srcs, *lands, send_sems, recv_sems, after)
    me = 4 * lax.axis_index("x") + 2 * lax.axis_index("y") + lax.axis_index("c")
    outs = []
    for kind, src, land in zip(kinds, res[:n], res[n:]):
        own = lax.dynamic_index_in_dim(src, me, 0, keepdims=False) if kind == "scatter" else src
        outs.append(lax.dynamic_update_index_in_dim(land, own, me, 0) if fill_own else (land, own))
    return outs


def _pick(n, cands):
    for t in cands:
        if n % t == 0:
            return t
    raise ValueError(f"no tile for {n}")


def _mm(name, a, b, mode, out_dtype, after=None):
    if mode == "nn":
        (m, k), (k2, n) = a.shape, b.shape
    elif mode == "nt":
        (m, k), (n, k2) = a.shape, b.shape
    else:
        (k, m), (k2, n) = a.shape, b.shape
    assert k == k2, (name, a.shape, b.shape)
    if mode == "tn":
        tm = _pick(m, (1024, 640, 512, 256, 128))
        tn = _pick(n, (1024, 640, 512, 256, 128))
        tk = _pick(k, (2048, 1024, 512, 256))
    else:
        tm, tn, tk = _pick(m, (512, 256)), n, k
    nk = k // tk

    def body(a_ref, b_ref, *rest):
        o_ref = rest[-2] if nk > 1 else rest[-1]
        av = a_ref[...].astype(BF16)
        bv = b_ref[...].astype(BF16)
        if mode == "nn":
            part = jnp.dot(av, bv, preferred_element_type=F32)
        elif mode == "nt":
            part = lax.dot_general(av, bv, NT_DIMS, preferred_element_type=F32)
        else:
            part = lax.dot_general(av, bv, TN_DIMS, preferred_element_type=F32)
        if nk == 1:
            o_ref[...] = part.astype(out_dtype)
            return
        acc_ref = rest[-1]
        kk = pl.program_id(2)

        @pl.when(kk == 0)
        def _():
            acc_ref[...] = part

        @pl.when(kk > 0)
        def _():
            acc_ref[...] += part

        @pl.when(kk == nk - 1)
        def _():
            o_ref[...] = acc_ref[...].astype(out_dtype)

    if mode == "tn":
        a_spec = pl.BlockSpec((tk, tm), lambda j, i, kk: (kk, i))
    else:
        a_spec = pl.BlockSpec((tm, tk), lambda j, i, kk: (i, kk))
    if mode == "nt":
        b_spec = pl.BlockSpec((tn, tk), lambda j, i, kk: (j, kk))
    else:
        b_spec = pl.BlockSpec((tk, tn), lambda j, i, kk: (kk, j))
    in_specs, args = [a_spec, b_spec], [a, b]
    if after is not None:
        in_specs.append(pl.BlockSpec((8, HD), lambda j, i, kk: (0, 0)))
        args.append(after)
    return pl.pallas_call(
        body,
        name=name,
        grid=(n // tn, m // tm, nk),
        in_specs=in_specs,
        out_specs=pl.BlockSpec((tm, tn), lambda j, i, kk: (i, j)),
        out_shape=jax.ShapeDtypeStruct((m, n), out_dtype),
        scratch_shapes=[pltpu.VMEM((tm, tn), F32)] if nk > 1 else [],
        compiler_params=_params(dimension_semantics=("parallel", "parallel", "arbitrary")),
    )(*args)


def _mm_cat(name, a_list, b, out_dtype, after=None):
    m = a_list[0].shape[0]
    ks = [a.shape[1] for a in a_list]
    n = b.shape[1]
    assert sum(ks) <= b.shape[0], (name, ks, b.shape)
    tm = _pick(m, (512, 256))
    na = len(a_list)

    def body(*refs):
        b_ref, o_ref = refs[na], refs[-1]
        k0, acc = 0, None
        for a_ref, kw in zip(refs[:na], ks):
            part = jnp.dot(a_ref[...].astype(BF16), b_ref[k0:k0 + kw, :], preferred_element_type=F32)
            acc = part if acc is None else acc + part
            k0 += kw
        o_ref[...] = acc.astype(out_dtype)

    in_specs = [pl.BlockSpec((tm, kw), lambda i: (i, 0)) for kw in ks] + [pl.BlockSpec((sum(ks), n), lambda i: (0, 0))]
    args = [*a_list, b]
    if after is not None:
        in_specs.append(pl.BlockSpec((8, HD), lambda i: (0, 0)))
        args.append(after)
    return pl.pallas_call(
        body, name=name, grid=(m // tm,),
        in_specs=in_specs, out_specs=pl.BlockSpec((tm, n), lambda i: (i, 0)),
        out_shape=jax.ShapeDtypeStruct((m, n), out_dtype),
        compiler_params=_params(dimension_semantics=("parallel",)),
    )(*args)


def _row(c, col=0):
    return pl.BlockSpec((TS, c), lambda i: (i, col))


def _vec(r, c):
    return pl.BlockSpec((r, c), lambda i: (0, 0))


def _prenorm_proj(x, pre_gain, w_t, n, after):
    s = x.shape[0]
    tm = 512

    def body(x_ref, g_ref, w_ref, after_ref, xn_ref, z_ref):
        xhat, _ = _rms_fwd(x_ref[...])
        xn = (xhat * g_ref[...]).astype(BF16)
        xn_ref[...] = xn
        z = lax.dot_general(xn, w_ref[...], NT_DIMS, preferred_element_type=F32)
        z_ref[:, :D] = (z[:, :D] * Q_SCALE).astype(BF16)
        z_ref[:, D:] = z[:, D:].astype(BF16)

    return pl.pallas_call(
        body, name="prenorm_proj_qkv", grid=(s // tm,),
        in_specs=[pl.BlockSpec((tm, D), lambda i: (i, 0)), _vec(1, D), _vec(n, D), _vec(8, HD)],
        out_specs=[pl.BlockSpec((tm, D), lambda i: (i, 0)), pl.BlockSpec((tm, n), lambda i: (i, 0))],
        out_shape=[jax.ShapeDtypeStruct((s, D), BF16), jax.ShapeDtypeStruct((s, n), BF16)],
        compiler_params=_params(dimension_semantics=("parallel",)),
    )(x, pre_gain, w_t, after)


def _proj_rest(xn, w_rest_t, bf_pad):
    s = xn.shape[0]
    tm = 512

    def body(x_ref, w_ref, b_ref, z_ref, kx_ref, c_buf, carry):
        @pl.when(pl.program_id(0) == 0)
        def _():
            carry[...] = jnp.zeros_like(carry)

        z = lax.dot_general(x_ref[...], w_ref[...], NT_DIMS, preferred_element_type=F32)
        z_ref[...] = z
        fl = z[:, FL_COL:] + b_ref[...]
        ls = jnp.minimum(fl, 0.0) - jnp.log(1.0 + jnp.exp(-jnp.abs(fl)))
        c_buf[...] = _cumsum_fwd(ls) + carry[0:1, :]
        carry[0:1, :] = c_buf[tm - 1:tm, :]
        cv = c_buf[...]
        for h in range(NH):
            kx_ref[h] = _bias_lanes(jnp.broadcast_to(cv[:, 8 * h:8 * h + 1], (tm, HD)) * (-LOG2E), 0, 3)

    return pl.pallas_call(
        body, name="proj_rest", grid=(s // tm,),
        in_specs=[pl.BlockSpec((tm, D), lambda i: (i, 0)), _vec(D_REST, D), _vec(1, HD)],
        out_specs=[pl.BlockSpec((tm, D_REST), lambda i: (i, 0)), pl.BlockSpec((NH, tm, HD), lambda i: (0, i, 0))],
        out_shape=[jax.ShapeDtypeStruct((s, D_REST), F32), jax.ShapeDtypeStruct((NH, s, HD), BF16)],
        scratch_shapes=[pltpu.VMEM((tm, HD), F32), pltpu.VMEM((8, HD), F32)],
        compiler_params=_params(dimension_semantics=("arbitrary",)),
    )(xn, w_rest_t, bf_pad)


def _attn_fwd(zq, kx):
    s = zq.shape[0]
    n = s // TQ
    nb = TQ // HD

    def body(q_ref, k_ref, v_ref, kx_ref, o_ref, ax_ref):
        i = pl.program_id(1)
        lane = lax.broadcasted_iota(jnp.int32, (TQ, HD), 1)
        row = lax.broadcasted_iota(jnp.int32, (TQ, HD), 0)
        qa = jnp.concatenate([q_ref[...], jnp.where(lane < 3, 1.0, 0.0).astype(BF16)], axis=1)

        def step(j, carry, masked):
            m, l, acc = carry
            rows = pl.ds(pl.multiple_of(j * TQ, TQ), TQ)
            ka = jnp.concatenate([k_ref[rows, :], kx_ref[0, rows, :]], axis=1)
            v_all = v_ref[rows, :]
            rp = TQ // ROW_PARTS
            parts = [slice(rp * t, rp * (t + 1)) for t in range(ROW_PARTS)]
            keys = [rp * (t + 1) if masked else TQ for t in range(ROW_PARTS)]
            u_parts = [lax.dot_general(qa[part], ka[:kn], NT_DIMS, preferred_element_type=F32)
                       for part, kn in zip(parts, keys)]
            out = []
            for t, (part, u, kn) in enumerate(zip(parts, u_parts, keys)):
                us = [u[:, HD * b:HD * (b + 1)] for b in range(kn // HD)]
                if masked:
                    us = [ub if HD * (b + 1) <= rp * t else jnp.where(row[part] >= lane[part] + HD * b, ub, NEG)
                          for b, ub in enumerate(us)]
                v = v_all[:kn]
                bm = functools.reduce(jnp.maximum, us)
                m_new = jnp.maximum(m[part], jnp.max(bm, axis=1, keepdims=True))
                alpha = jnp.exp2(m[part] - m_new)
                ps = [jnp.exp2(ub - m_new) for ub in us]
                l_new = alpha * l[part] + functools.reduce(jnp.add, ps)
                pr = jnp.concatenate(ps, axis=1).astype(BF16)
                out.append((m_new, l_new, alpha * acc[part] + jnp.dot(pr, v, preferred_element_type=F32)))
            return tuple(jnp.concatenate([o[t] for o in out], axis=0) for t in range(3))

        init = (jnp.full((TQ, HD), NEG, F32), jnp.zeros((TQ, HD), F32), jnp.zeros((TQ, HD), F32))
        carry = lax.fori_loop(0, i, lambda j, cr: step(j, cr, False), init)
        m, l, acc = step(i, carry, True)
        l_row = jnp.sum(l, axis=1, keepdims=True)
        o_ref[...] = acc / l_row
        ax_ref[0] = _bias_lanes(-(m + jnp.log(l_row) * LOG2E), 3, 0)

    return pl.pallas_call(
        body, name="attn_fwd", grid=(NH, n),
        in_specs=[
            pl.BlockSpec((TQ, HD), lambda h, i: (i, h)),
            pl.BlockSpec((s, HD), lambda h, i: (0, NH + h)),
            pl.BlockSpec((s, HD), lambda h, i: (0, 2 * NH + h)),
            pl.BlockSpec((1, s, HD), lambda h, i: (h, 0, 0)),
        ],
        out_specs=[pl.BlockSpec((TQ, HD), lambda h, i: (i, h)), pl.BlockSpec((1, TQ, HD), lambda h, i: (h, i, 0))],
        out_shape=[jax.ShapeDtypeStruct((s, D), F32), jax.ShapeDtypeStruct((NH, s, HD), BF16)],
        compiler_params=_params(dimension_semantics=("parallel", "parallel")),
    )(zq, zq, zq, kx)


def _attn_bwd(zq, do, ax, delta, kx, after):
    s = zq.shape[0]
    n = s // TQ
    nb = TQ // HD

    def body(k_ref, v_ref, kx_ref, q_ref, ax_ref, do_ref, dl_ref, after_ref, dq_out, dk_ref, dv_ref, dcs_ref, drs_ref,
             dq_ref):
        j = pl.program_id(1)

        @pl.when(j == 0)
        def _():
            dq_ref[...] = jnp.zeros_like(dq_ref)
            drs_ref[...] = jnp.zeros_like(drs_ref)

        k = k_ref[...]
        v = v_ref[...]
        ka = jnp.concatenate([k, kx_ref[0]], axis=1)
        row = lax.broadcasted_iota(jnp.int32, (TQ, HD), 0)
        lane = lax.broadcasted_iota(jnp.int32, (TQ, HD), 1)

        def step(i, carry, r0, rn, kn, masked):
            dk, dv, dcs = carry
            rows = pl.ds(pl.multiple_of(i * TQ + r0, rn), rn)
            q = q_ref[rows, :]
            dout = do_ref[rows, :]
            dlv = dl_ref[0, rows, :]
            qa = jnp.concatenate([q, ax_ref[0, rows, :]], axis=1)
            u = lax.dot_general(qa, ka[:kn], NT_DIMS, preferred_element_type=F32)
            dp = lax.dot_general(dout, v[:kn], NT_DIMS, preferred_element_type=F32)
            prs, dss = [], []
            for b in range(kn // HD):
                cs = slice(HD * b, HD * (b + 1))
                ub = u[:, cs]
                if masked and HD * (b + 1) > r0:
                    ub = jnp.where(row[:rn] + r0 >= lane[:rn] + HD * b, ub, NEG)
                pb = jnp.exp2(ub)
                prs.append(pb)
                dss.append(pb * (dp[:, cs] - dlv))
            drs_ref[0, rows, :] += functools.reduce(jnp.add, dss)
            ds = jnp.concatenate(dss, axis=1)
            dsb = ds.astype(BF16)
            dcs_new = jnp.sum(ds.reshape(rn // 8, 8, kn), axis=0)
            dv_new = lax.dot_general(jnp.concatenate(prs, axis=1).astype(BF16), dout, TN_DIMS, preferred_element_type=F32)
            dk_new = lax.dot_general(dsb, q, TN_DIMS, preferred_element_type=F32)
            if kn < TQ:
                dcs_new = jnp.concatenate([dcs_new, jnp.zeros((8, TQ - kn), F32)], axis=1)
                dv_new = jnp.concatenate([dv_new, jnp.zeros((TQ - kn, HD), F32)], axis=0)
                dk_new = jnp.concatenate([dk_new, jnp.zeros((TQ - kn, HD), F32)], axis=0)
            dq_ref[rows, :] += jnp.dot(dsb, k[:kn], preferred_element_type=F32) * SCALE
            return dk + dk_new, dv + dv_new, dcs + dcs_new

        carry = (jnp.zeros((TQ, HD), F32), jnp.zeros((TQ, HD), F32), jnp.zeros((8, TQ), F32))
        rp = TQ // ROW_PARTS
        for t in range(ROW_PARTS):
            carry = step(j, carry, rp * t, rp, rp * (t + 1), True)
        dk, dv, dcs = lax.fori_loop(j + 1, n, lambda i, cr: step(i, cr, 0, TQ, TQ, False), carry)
        dk_ref[...] = (dk * (SCALE / Q_SCALE)).astype(BF16)
        dv_ref[...] = dv.astype(BF16)
        dcs_ref[0] = jnp.broadcast_to(_colsum(dcs), (8, TQ))

        @pl.when(j == n - 1)
        def _():
            dq_out[...] = dq_ref[...].astype(BF16)

    return pl.pallas_call(
        body, name="attn_bwd", grid=(NH, n),
        in_specs=[
            pl.BlockSpec((TQ, HD), lambda h, j: (j, NH + h)),
            pl.BlockSpec((TQ, HD), lambda h, j: (j, 2 * NH + h)),
            pl.BlockSpec((1, TQ, HD), lambda h, j: (h, j, 0)),
            pl.BlockSpec((s, HD), lambda h, j: (0, h)),
            pl.BlockSpec((1, s, HD), lambda h, j: (h, 0, 0)),
            pl.BlockSpec((s, HD), lambda h, j: (0, h)),
            pl.BlockSpec((1, s, HD), lambda h, j: (h, 0, 0)),
            pl.BlockSpec((8, HD), lambda h, j: (0, 0)),
        ],
        out_specs=[
            pl.BlockSpec((s, HD), lambda h, j: (0, h)),
            pl.BlockSpec((TQ, HD), lambda h, j: (j, h)),
            pl.BlockSpec((TQ, HD), lambda h, j: (j, h)),
            pl.BlockSpec((1, 8, TQ), lambda h, j: (j, h, 0)),
            pl.BlockSpec((1, s, HD), lambda h, j: (h, 0, 0)),
        ],
        out_shape=[
            jax.ShapeDtypeStruct((s, D), BF16),
            jax.ShapeDtypeStruct((s, D), BF16),
            jax.ShapeDtypeStruct((s, D), BF16),
            jax.ShapeDtypeStruct((n, 8 * NH, TQ), F32),
            jax.ShapeDtypeStruct((NH, s, HD), F32),
        ],
        scratch_shapes=[pltpu.VMEM((s, HD), F32)],
        compiler_params=_params(dimension_semantics=("parallel", "arbitrary")),
    )(zq, zq, kx, zq, ax, do, delta, after)


def _forget_bwd(dcs, drs, zr, bf_pad):
    n = dcs.shape[0]
    s = n * TQ

    def body(dcs_ref, drs_ref, fl_ref, b_ref, dfl_ref, gb_ref, buf, carry):
        i = pl.program_id(0)

        @pl.when(i == 0)
        def _():
            carry[...] = jnp.zeros_like(carry)
            gb_ref[...] = jnp.zeros_like(gb_ref)

        dc_t = jnp.concatenate([dcs_ref[0], jnp.zeros((HD - 8 * NH, TQ), F32)], axis=0)
        lane = lax.broadcasted_iota(jnp.int32, (TQ, HD), 1)
        dc = -dc_t.T
        for hh in range(NH):
            dc = dc + jnp.where(lane == 8 * hh, jnp.sum(drs_ref[hh], axis=1, keepdims=True), 0.0)
        buf[...] = _cumsum_bwd(dc) + carry[0:1, :]
        carry[0:1, :] = buf[0:1, :]
        fl = fl_ref[...] + b_ref[...]
        dfl = buf[...] * _sigmoid_rel(-fl)
        dfl_ref[...] = dfl.astype(BF16)
        gb_ref[...] += _colsum(dfl)

    return pl.pallas_call(
        body, name="forget_bwd", grid=(n,),
        in_specs=[
            pl.BlockSpec((1, 8 * NH, TQ), lambda i: (n - 1 - i, 0, 0)),
            pl.BlockSpec((NH, TQ, HD), lambda i: (0, n - 1 - i, 0)),
            pl.BlockSpec((TQ, HD), lambda i: (n - 1 - i, FL_COL // HD)),
            _vec(1, HD),
        ],
        out_specs=[pl.BlockSpec((TQ, HD), lambda i: (n - 1 - i, 0)), _vec(1, HD)],
        out_shape=[jax.ShapeDtypeStruct((s, HD), BF16), jax.ShapeDtypeStruct((1, HD), F32)],
        scratch_shapes=[pltpu.VMEM((TQ, HD), F32), pltpu.VMEM((8, HD), F32)],
        compiler_params=_params(dimension_semantics=("arbitrary",)),
    )(dcs, drs, zr, bf_pad)


def _gates(xc, w_ref, b, sigmoid):
    xb = xc.astype(BF16)
    pre = jnp.concatenate(
        [jnp.dot(xb[:, HD * g:HD * (g + 1)], w_ref[g], preferred_element_type=F32) for g in range(NH)], axis=1)
    return sigmoid(pre + b)


def _lru_coeffs(r, lam):
    sp = jnp.maximum(-lam, 0.0) + jnp.log(1.0 + jnp.exp(-jnp.abs(lam)))
    log_a = -LRU_C * r * sp
    a = jnp.exp(log_a)
    y = 2.0 * log_a
    em1 = jnp.where(jnp.abs(y) < 0.01, y * (1.0 + y * (0.5 + y * (1.0 / 6.0))), jnp.exp(y) - 1.0)
    em = -em1
    inv_gam = lax.rsqrt(jnp.maximum(em, 1e-37))
    return sp, a, em * inv_gam, inv_gam


def _conv_taps(ext, t):
    return [_shift_down(ext, CONV_W - 1 - jj, t) for jj in range(CONV_W)]


def _lru_fwd(zr, conv_w8, conv_b, w_r, b_r, w_i, b_i, lam):
    s = zr.shape[0]
    n = s // TS
    xl_col = 1

    def body(xl_ref, halo_ref, cw_ref, cb_ref, wr_ref, br_ref, wi_ref, bi_ref, lam_ref, xc_ref, h_ref, carry):
        i = pl.program_id(0)

        @pl.when(i == 0)
        def _():
            carry[...] = jnp.zeros_like(carry)

        halo = jnp.where(i == 0, 0.0, halo_ref[...])
        taps = _conv_taps(jnp.concatenate([halo, xl_ref[...]], axis=0), TS)
        xc = cb_ref[...] + sum(cw_ref[jj:jj + 1, :] * taps[jj] for jj in range(CONV_W))
        xc_ref[...] = xc
        r = _gates(xc, wr_ref, br_ref[...], _sigmoid_rel)
        ig = _gates(xc, wi_ref, bi_ref[...], _sigmoid)
        _, a, gam, _ = _lru_coeffs(r, lam_ref[...])
        a_cum, h_loc = _scan(a, gam * (ig * xc), False)
        h_ref[...] = h_loc + a_cum * carry[0:1, :]
        carry[0:1, :] = h_ref[TS - 1:TS, :]

    return pl.pallas_call(
        body, name="lru_fwd", grid=(n,),
        in_specs=[
            _row(D, xl_col),
            pl.BlockSpec((8, D), lambda i: (jnp.maximum(i * (TS // 8) - 1, 0), xl_col)),
            _vec(8, D), _vec(1, D),
            pl.BlockSpec((NH, HD, HD), lambda i: (0, 0, 0)), _vec(1, D),
            pl.BlockSpec((NH, HD, HD), lambda i: (0, 0, 0)), _vec(1, D),
            _vec(1, D),
        ],
        out_specs=[_row(D), _row(D)],
        out_shape=[jax.ShapeDtypeStruct((s, D), F32), jax.ShapeDtypeStruct((s, D), F32)],
        scratch_shapes=[pltpu.VMEM((8, D), F32)],
        compiler_params=_params(dimension_semantics=("arbitrary",)),
    )(zr, zr, conv_w8, conv_b, w_r, b_r, w_i, b_i, lam)


def _lru_bwd(zr, xc, h, dh, conv_w8, w_r, b_r, w_i, b_i, lam):
    s = zr.shape[0]
    n = s // TS
    xl_col = 1

    def rev(i):
        return n - 1 - i

    def body(xl_ref, xlh_ref, xc_ref, h_ref, hh_ref, dh_ref, cw_ref, wr_ref, br_ref, wi_ref, bi_ref, lam_ref,
             dxl_ref, gwr_ref, gwi_ref, gbr_ref, gbi_ref, glam_ref, gcb_ref, gcw_ref, l_buf, dxc_buf, carry_g, carry_dxc):
        i = pl.program_id(0)
        first = rev(i) == 0

        @pl.when(i == 0)
        def _():
            carry_g[...] = jnp.zeros_like(carry_g)
            carry_dxc[...] = jnp.zeros_like(carry_dxc)
            for ref in (gwr_ref, gwi_ref, gbr_ref, gbi_ref, glam_ref, gcb_ref, gcw_ref):
                ref[...] = jnp.zeros_like(ref)

        rows = _rows_iota(TS)
        xc = xc_ref[...]
        lam = lam_ref[...]
        r = _gates(xc, wr_ref, br_ref[...], _sigmoid_rel)
        ig = _gates(xc, wi_ref, bi_ref[...], _sigmoid)
        sp, a, gam, inv_gam = _lru_coeffs(r, lam)
        g = dh_ref[...] + jnp.where(rows == TS - 1, carry_g[0:1, :], 0.0)
        b = jnp.where(rows == TS - 1, 0.0, pltpu.roll(a, TS - 1, 0))
        l_buf[...] = _scan(b, g, True)[1]
        lv = l_buf[...]
        carry_g[0:1, :] = l_buf[0:1, :] * a[0:1, :]
        h_prev_row = jnp.where(first, 0.0, hh_ref[7:8, :])
        h_prev = jnp.where(rows == 0, h_prev_row, pltpu.roll(h_ref[...], 1, 0))
        dgam = lv * ig * xc
        dig = lv * gam * xc
        dxc = lv * gam * ig
        dla = lv * h_prev * a - dgam * (a * a) * inv_gam
        dr = dla * (-LRU_C) * sp
        glam_ref[...] += _colsum(dla * r) * (LRU_C * _sigmoid_rel(-lam))
        dpr = dr * r * (1.0 - r)
        dpi = dig * ig * (1.0 - ig)
        gbr_ref[...] += _colsum(dpr)
        gbi_ref[...] += _colsum(dpi)
        xb = xc.astype(BF16)
        dprb = dpr.astype(BF16)
        dpib = dpi.astype(BF16)
        back = []
        for gi in range(NH):
            cs = slice(HD * gi, HD * (gi + 1))
            gwr_ref[gi] += lax.dot_general(xb[:, cs], dprb[:, cs], TN_DIMS, preferred_element_type=F32)
            gwi_ref[gi] += lax.dot_general(xb[:, cs], dpib[:, cs], TN_DIMS, preferred_element_type=F32)
            back.append(lax.dot_general(dprb[:, cs], wr_ref[gi], NT_DIMS, preferred_element_type=F32)
                        + lax.dot_general(dpib[:, cs], wi_ref[gi], NT_DIMS, preferred_element_type=F32))
        dxc = dxc + jnp.concatenate(back, axis=1)
        dxc_buf[...] = dxc
        gcb_ref[...] += _colsum(dxc)
        halo = jnp.where(first, 0.0, xlh_ref[...])
        taps = _conv_taps(jnp.concatenate([halo, xl_ref[...]], axis=0), TS)
        for jj in range(CONV_W):
            gcw_ref[jj:jj + 1, :] += _colsum(dxc * taps[jj])
        ext = jnp.concatenate([dxc, carry_dxc[...]], axis=0)
        dxl = sum(cw_ref[jj:jj + 1, :] * _shift_up(ext, CONV_W - 1 - jj, TS) for jj in range(CONV_W))
        dxl_ref[...] = dxl.astype(BF16)
        carry_dxc[...] = dxc_buf[0:8, :]

    rowr = lambda c, col=0: pl.BlockSpec((TS, c), lambda i: (rev(i), col))
    halo = lambda col: pl.BlockSpec((8, D), lambda i: (jnp.maximum(rev(i) * (TS // 8) - 1, 0), col))
    gate_w = pl.BlockSpec((NH, HD, HD), lambda i: (0, 0, 0))
    return pl.pallas_call(
        body, name="lru_bwd", grid=(n,),
        in_specs=[rowr(D, xl_col), halo(xl_col), rowr(D), rowr(D), halo(0), rowr(D),
                  _vec(8, D), gate_w, _vec(1, D), gate_w, _vec(1, D), _vec(1, D)],
        out_specs=[rowr(D), gate_w, gate_w, _vec(1, D), _vec(1, D), _vec(1, D), _vec(1, D), _vec(8, D)],
        out_shape=[
            jax.ShapeDtypeStruct((s, D), BF16),
            jax.ShapeDtypeStruct((NH, HD, HD), F32), jax.ShapeDtypeStruct((NH, HD, HD), F32),
            jax.ShapeDtypeStruct((1, D), F32), jax.ShapeDtypeStruct((1, D), F32), jax.ShapeDtypeStruct((1, D), F32),
            jax.ShapeDtypeStruct((1, D), F32), jax.ShapeDtypeStruct((8, D), F32),
        ],
        scratch_shapes=[pltpu.VMEM((TS, D), F32), pltpu.VMEM((TS, D), F32), pltpu.VMEM((8, D), F32), pltpu.VMEM((8, D), F32)],
        compiler_params=_params(dimension_semantics=("arbitrary",)),
    )(zr, zr, xc, h, h, dh, conv_w8, w_r, b_r, w_i, b_i, lam)


def _silu_parts(g):
    sg = _sigmoid(g)
    return g * sg, sg * (1.0 + g * (1.0 - sg))


def _branch_out_bwd(o, h, zr, dmix, w_out, gain_a, gain_l):
    s = o.shape[0]

    def body(o_ref, ga_ref, h_ref, gl_ref, dm_ref, w_ref, ka_ref, kl_ref,
             do_ref, dl_ref, dga_ref, dh_ref, dgl_ref, gka_ref, gkl_ref):
        @pl.when(pl.program_id(0) == 0)
        def _():
            gka_ref[...] = jnp.zeros_like(gka_ref)
            gkl_ref[...] = jnp.zeros_like(gkl_ref)

        dycat = lax.dot_general(dm_ref[...], w_ref[...], NT_DIMS, preferred_element_type=F32)

        def one(v, g, dy, gain):
            vhat, rstd = _rms_fwd(v)
            sg, dsg = _silu_parts(g)
            dn = dy * sg
            dg = dy * (vhat * gain) * dsg
            return _rms_bwd(vhat, rstd, dn * gain), dg, _colsum(dn * vhat)

        o = o_ref[...]
        dout, dga, gka = one(o, ga_ref[...], dycat[:, :D], ka_ref[...])
        do_ref[...] = dout.astype(BF16)
        dga_ref[...] = dga.astype(BF16)
        gka_ref[...] += gka
        prod = dout * o
        for hh in range(NH):
            dl_ref[hh] = jnp.broadcast_to(jnp.sum(prod[:, HD * hh:HD * (hh + 1)], axis=1, keepdims=True), (TS, HD))
        dh, dgl, gkl = one(h_ref[...], gl_ref[...], dycat[:, D:], kl_ref[...])
        dh_ref[...] = dh
        dgl_ref[...] = dgl.astype(BF16)
        gkl_ref[...] += gkl

    return pl.pallas_call(
        body, name="branch_out_bwd", grid=(s // TS,),
        in_specs=[_row(D), _row(D, 0), _row(D), _row(D, 2), _row(D), _vec(2 * D, D), _vec(1, D), _vec(1, D)],
        out_specs=[_row(D), pl.BlockSpec((NH, TS, HD), lambda i: (0, i, 0)), _row(D), _row(D), _row(D), _vec(1, D), _vec(1, D)],
        out_shape=[
            jax.ShapeDtypeStruct((s, D), BF16), jax.ShapeDtypeStruct((NH, s, HD), F32), jax.ShapeDtypeStruct((s, D), BF16),
            jax.ShapeDtypeStruct((s, D), F32), jax.ShapeDtypeStruct((s, D), BF16),
            jax.ShapeDtypeStruct((1, D), F32), jax.ShapeDtypeStruct((1, D), F32),
        ],
        compiler_params=_params(dimension_semantics=("arbitrary",)),
    )(o, zr, h, zr, dmix, w_out, gain_a, gain_l)


def _residual(x, o, h, zr, gain_a, gain_l, w_out, post_gain):
    s = x.shape[0]

    def body(x_ref, o_ref, ga_ref, h_ref, gl_ref, ka_ref, kl_ref, w_ref, g_ref, y_ref, m_ref, h1_ref, hb_ref):
        ohat, _ = _rms_fwd(o_ref[...])
        y_ref[:, 0:D] = (ohat * ka_ref[...] * _silu_parts(ga_ref[...])[0]).astype(BF16)
        hhat, _ = _rms_fwd(h_ref[...])
        y_ref[:, D:2 * D] = (hhat * kl_ref[...] * _silu_parts(gl_ref[...])[0]).astype(BF16)
        mix = jnp.dot(y_ref[...], w_ref[...], preferred_element_type=F32)
        m_ref[...] = mix
        mhat, _ = _rms_fwd(mix)
        h1 = x_ref[...] + mhat * g_ref[...]
        h1_ref[...] = h1
        hb_ref[...] = h1.astype(BF16)

    return pl.pallas_call(
        body, name="residual", grid=(s // TS,),
        in_specs=[_row(D), _row(D), _row(D, 0), _row(D), _row(D, 2), _vec(1, D), _vec(1, D), _vec(2 * D, D), _vec(1, D)],
        out_specs=[_row(2 * D), _row(D), _row(D), _row(D)],
        out_shape=[jax.ShapeDtypeStruct((s, 2 * D), BF16), jax.ShapeDtypeStruct((s, D), F32),
                   jax.ShapeDtypeStruct((s, D), F32), jax.ShapeDtypeStruct((s, D), BF16)],
        compiler_params=_params(dimension_semantics=("parallel",)),
    )(x, o, zr, h, zr, gain_a, gain_l, w_out, post_gain)


def _head(h1, p, tgt, mix, w_gate, w_ple, ple_gain, b_gate, post_gain):
    s = h1.shape[0]

    def body(h_ref, p_ref, t_ref, m_ref, wg_ref, wp_ref, kg_ref, b_ref, pg_ref,
             loss_ref, dgp_ref, dpe_ref, dh_ref, dm_ref, gk_ref, gb_ref, gg_ref):
        @pl.when(pl.program_id(0) == 0)
        def _():
            for ref in (loss_ref, gk_ref, gb_ref, gg_ref):
                ref[...] = jnp.zeros_like(ref)

        h1 = h_ref[...]
        pe = jnp.dot(p_ref[...].astype(BF16), wp_ref[...], preferred_element_type=F32)
        gp = jnp.dot(h1.astype(BF16), wg_ref[...], preferred_element_type=F32)
        ehat, rstd = _rms_fwd(pe)
        e = ehat * kg_ref[...]
        gate = _sigmoid(gp + b_ref[...])
        diff = (h1 + gate * e) - t_ref[...]
        per_row = jnp.mean(diff * diff, axis=-1, keepdims=True)
        loss_ref[...] += 0.5 * jnp.sum(per_row, axis=0, keepdims=True)
        dy = diff * (1.0 / D)
        dgp = dy * e * gate * (1.0 - gate)
        dgpb = dgp.astype(BF16)
        dgp_ref[...] = dgpb
        gb_ref[...] += _colsum(dgp)
        de = dy * gate
        gk_ref[...] += _colsum(de * ehat)
        dpe_ref[...] = _rms_bwd(ehat, rstd, de * kg_ref[...]).astype(BF16)
        dh1 = dy + lax.dot_general(dgpb, wg_ref[...], NT_DIMS, preferred_element_type=F32)
        dh_ref[...] = dh1
        mhat, rstd_m = _rms_fwd(m_ref[...])
        gg_ref[...] += _colsum(dh1 * mhat)
        dm_ref[...] = _rms_bwd(mhat, rstd_m, dh1 * pg_ref[...]).astype(BF16)

    return pl.pallas_call(
        body, name="head", grid=(s // TS,),
        in_specs=[_row(D), _row(D_PLE), _row(D), _row(D), _vec(D, D), _vec(D_PLE, D), _vec(1, D), _vec(1, D), _vec(1, D)],
        out_specs=[_vec(1, 1), _row(D), _row(D), _row(D), _row(D), _vec(1, D), _vec(1, D), _vec(1, D)],
        out_shape=[
            jax.ShapeDtypeStruct((1, 1), F32), jax.ShapeDtypeStruct((s, D), BF16), jax.ShapeDtypeStruct((s, D), BF16),
            jax.ShapeDtypeStruct((s, D), F32), jax.ShapeDtypeStruct((s, D), BF16),
            jax.ShapeDtypeStruct((1, D), F32), jax.ShapeDtypeStruct((1, D), F32), jax.ShapeDtypeStruct((1, D), F32),
        ],
        compiler_params=_params(dimension_semantics=("arbitrary",)),
    )(h1, p, tgt, mix, w_gate, w_ple, ple_gain, b_gate, post_gain)


def _prenorm_bwd(x, dxn_a, dz_rest, w_rest_t, dh1, pre_gain, after):
    s = x.shape[0]
    ks = [a.shape[1] for a in dz_rest]
    assert sum(ks) == w_rest_t.shape[0]
    nz = len(dz_rest)
    tm = 512
    rowm = lambda c: pl.BlockSpec((tm, c), lambda i: (i, 0))

    def body(*refs):
        x_ref, da_ref = refs[:2]
        w_ref, dh_ref, g_ref, after_ref, dx_ref, gg_ref = refs[2 + nz:]

        @pl.when(pl.program_id(0) == 0)
        def _():
            gg_ref[...] = jnp.zeros_like(gg_ref)

        dxn, k0 = da_ref[...], 0
        for dz_ref, kw in zip(refs[2:2 + nz], ks):
            dxn = dxn + jnp.dot(dz_ref[...], w_ref[k0:k0 + kw, :], preferred_element_type=F32)
            k0 += kw
        xhat, rstd = _rms_fwd(x_ref[...])
        gg_ref[...] += _colsum(dxn * xhat)
        dx_ref[...] = dh_ref[...] + _rms_bwd(xhat, rstd, dxn * g_ref[...])

    return pl.pallas_call(
        body, name="prenorm_bwd", grid=(s // tm,),
        in_specs=[rowm(D), rowm(D)] + [rowm(kw) for kw in ks] + [_vec(*w_rest_t.shape), rowm(D), _vec(1, D), _vec(8, HD)],
        out_specs=[rowm(D), _vec(1, D)],
        out_shape=[jax.ShapeDtypeStruct((s, D), F32), jax.ShapeDtypeStruct((1, D), F32)],
        compiler_params=_params(dimension_semantics=("arbitrary",)),
    )(x, dxn_a, *dz_rest, w_rest_t, dh1, pre_gain, after)


def _adamw(name, parts, w, m, v, own=None, me=None):
    r, c = w.shape
    if r % 8 == 0:
        tr = _pick(r, (256, 128, 16, 8))
        grid = (r // tr,)
        blk = pl.BlockSpec((tr, c), lambda i: (i, 0))
        parts_blk = pl.BlockSpec((N_DEV, tr, c), lambda i: (0, i, 0))
    else:
        tc = _pick(c, (256, 128))
        grid = (c // tc,)
        blk = pl.BlockSpec((r, tc), lambda i: (0, i))
        parts_blk = pl.BlockSpec((N_DEV, r, tc), lambda i: (0, 0, i))

    def body(*refs):
        p_ref, w_ref, m_ref, v_ref = refs[:4]
        g_ref, d_ref, nm_ref, nv_ref = refs[-4:]
        if own is None:
            g = p_ref[0].astype(F32)
            for j in range(1, N_DEV):
                g = g + p_ref[j].astype(F32)
            g_ref[...] = g
        else:
            own_ref, me_ref = refs[4:6]
            g_ref[...] = jnp.zeros_like(g_ref)
            for j in range(N_DEV):
                @pl.when(me_ref[0] == j)
                def _():
                    g_ref[...] += own_ref[...].astype(F32)

                @pl.when(me_ref[0] != j)
                def _():
                    g_ref[...] += p_ref[j].astype(F32)
            g = g_ref[...]
        nm = ADAM_B1 * m_ref[...] + (1.0 - ADAM_B1) * g
        nv = ADAM_B2 * v_ref[...] + (1.0 - ADAM_B2) * (g * g)
        nm_ref[...] = nm
        nv_ref[...] = nv
        m_hat = nm / (1.0 - ADAM_B1 ** ADAM_STEP)
        v_hat = nv / (1.0 - ADAM_B2 ** ADAM_STEP)
        d_ref[...] = -ADAM_LR * (m_hat / (jnp.sqrt(v_hat) + ADAM_EPS) + ADAM_WD * w_ref[...])

    in_specs, args = [parts_blk, blk, blk, blk], [parts, w, m, v]
    if own is not None:
        in_specs += [blk, pl.BlockSpec(memory_space=pltpu.SMEM)]
        args += [own, me]
    return pl.pallas_call(
        body, name=name, grid=grid,
        in_specs=in_specs,
        out_specs=[blk] * 4,
        out_shape=[jax.ShapeDtypeStruct((r, c), F32)] * 4,
        compiler_params=_params(dimension_semantics=("parallel",)),
    )(*args)


def _spread8(v):
    r = v.shape[0]
    return jnp.pad(jnp.pad(v[:, :, None], ((0, 0), (0, 0), (0, 7))).reshape(r, 8 * NH), ((0, 0), (0, HD - 8 * NH)))


def _gather8(v):
    return v[:, :8 * NH].reshape(v.shape[0], NH, 8)[:, :, 0]


def _cols_to_shards(g):
    r, c8 = g.shape
    return g.reshape(r, N_DEV, c8 // N_DEV).transpose(1, 0, 2)


def _shards_to_cols(g):
    n, r, c = g.shape
    return g.transpose(1, 0, 2).reshape(r, n * c)


def kernel(x, p, w_in, b_f, pre_gain, post_gain, conv_w, conv_b, w_rgate, b_rgate, w_igate, b_igate, lru_lambda, attn_out_gain, lru_out_gain, w_out, w_ple, ple_gain, w_ple_gate, b_ple_gate, loss_target, m_w_in, m_b_f, m_pre_gain, m_post_gain, m_conv_w, m_conv_b, m_w_rgate, m_b_rgate, m_w_igate, m_b_igate, m_lru_lambda, m_attn_out_gain, m_lru_out_gain, m_w_out, m_w_ple, m_ple_gain, m_w_ple_gate, m_b_ple_gate, v_w_in, v_b_f, v_pre_gain, v_post_gain, v_conv_w, v_conv_b, v_w_rgate, v_b_rgate, v_w_igate, v_b_igate, v_lru_lambda, v_attn_out_gain, v_lru_out_gain, v_w_out, v_w_ple, v_ple_gain, v_w_ple_gate, v_b_ple_gate):
    me = 4 * lax.axis_index("x") + 2 * lax.axis_index("y") + lax.axis_index("c")
    x2, p2, tgt = x[0], p[0, 0], loss_target[0]

    conv_w_shard8 = jnp.pad(conv_w[0], ((0, 8 - CONV_W), (0, 0)))
    wt, m_wt, v_wt = w_in[0].T, m_w_in[0].T, v_w_in[0].T
    g_wint, g_conv = _gather_two_level("gather_w_in", [wt.astype(BF16), conv_w_shard8])
    win_t = g_wint.reshape(D_IN, D)
    rest_state, rest_token = _exchange_start(
        "gather_rest_start", [w_out[0].astype(BF16), w_ple[0].astype(BF16), w_ple_gate[0].astype(BF16)], ["bcast"] * 3,
        after=g_conv)
    w_rest_t = jnp.concatenate([win_t[D_QKV + NH:], _spread8(win_t[D_QKV:D_QKV + NH].T).T], axis=0)
    conv_w8 = _shards_to_cols(g_conv)
    bf_pad = _spread8(b_f)
    w_r, w_i = w_rgate[0].astype(BF16), w_igate[0].astype(BF16)

    xn, zq = _prenorm_proj(x2, pre_gain, win_t, D_QKV, rest_token)
    zr, kx = _proj_rest(xn, w_rest_t, bf_pad)
    o, ax = _attn_fwd(zq, kx)
    xc, h = _lru_fwd(zr, conv_w8, conv_b, w_r, b_rgate, w_i, b_igate, lru_lambda)
    g_wout, g_wple, g_wpg = _exchange_wait("gather_rest_wait", rest_state, h)
    wout_full = g_wout.reshape(2 * D, D)
    wple_full = _shards_to_cols(g_wple)
    wpg_full = g_wpg.reshape(D, D)
    ycat, mix, h1, h1b = _residual(x2, o, h, zr, attn_out_gain, lru_out_gain, wout_full, post_gain)

    loss_part, dgp, dpe, dh1, dmix, g_ple_gain, g_b_gate, g_post_gain = _head(
        h1, p2, tgt, mix, wpg_full, wple_full, ple_gain, b_ple_gate, post_gain)
    gw_pg = _mm("bwd_gate_w", h1b, dgp, "tn", BF16)
    gw_ple = _mm("bwd_ple_w", p2, dpe, "tn", BF16)
    gw_out = _mm("bwd_out_w", ycat, dmix, "tn", BF16)
    do, delta, dga, dh, dgl, g_aog, g_log = _branch_out_bwd(o, h, zr, dmix, wout_full, attn_out_gain, lru_out_gain)
    dxl, g_wr, g_wi, g_br, g_bi, g_lam, g_cb, g_cw8 = _lru_bwd(
        zr, xc, h, dh, conv_w8, w_r, b_rgate, w_i, b_igate, lru_lambda)
    gates = jnp.concatenate([g_wr.reshape(D, HD), g_wi.reshape(D, HD)], axis=0).astype(BF16)
    outw_state, outw_token = _exchange_start(
        "exchange_outw_start",
        [gw_out.reshape(N_DEV, 2 * D // N_DEV, D), _cols_to_shards(gw_ple), gw_pg.reshape(N_DEV, D // N_DEV, D), gates],
        ["scatter"] * 3 + ["bcast"])
    dq, dk, dv, dcs, drs = _attn_bwd(zq, do, ax, delta, kx, outw_token)
    dfl, g_bf_pad = _forget_bwd(dcs, drs, zr, bf_pad)
    gw_pieces = [_mm("bwd_w_" + nm, dz, xn, "tn", BF16) for nm, dz in
                 (("q", dq), ("k", dk), ("v", dv), ("fl", dfl), ("ga", dga), ("xl", dxl), ("gl", dgl))]
    gw_pieces[3] = _gather8(gw_pieces[3].T).T
    gw_in_t = jnp.concatenate(gw_pieces, axis=0)
    inw_state, inw_token = _exchange_start(
        "exchange_inw_start", [gw_in_t.reshape(N_DEV, D_IN_SHARD, D)], ["scatter"])
    dxn_a = _mm_cat("bwd_qkv_x", [dq, dk, dv], win_t, F32, after=inw_token)
    grad_x, g_pre_gain = _prenorm_bwd(x2, dxn_a, [dga, dxl, dgl, dfl], w_rest_t, dh1, pre_gain, inw_token)

    upd = {}
    me1 = me.reshape(1).astype(jnp.int32)
    r_wout, r_wple, r_wpg, r_gates = _exchange_wait("exchange_outw_wait", outw_state, grad_x, fill_own=False)
    upd["w_out"] = _adamw("adamw_w_out", r_wout[0], w_out[0], m_w_out[0], v_w_out[0], r_wout[1], me1)
    upd["w_ple"] = _adamw("adamw_w_ple", r_wple[0], w_ple[0], m_w_ple[0], v_w_ple[0], r_wple[1], me1)
    upd["w_ple_gate"] = _adamw("adamw_w_ple_gate", r_wpg[0], w_ple_gate[0], m_w_ple_gate[0], v_w_ple_gate[0], r_wpg[1], me1)
    gates_of = lambda a, b: jnp.concatenate([a[0].reshape(D, HD), b[0].reshape(D, HD)], axis=0)
    g_gates = _adamw("adamw_gates", r_gates[0], gates_of(w_rgate, w_igate), gates_of(m_w_rgate, m_w_igate),
                     gates_of(v_w_rgate, v_w_igate), r_gates[1], me1)
    upd["w_rgate"] = [a[:D].reshape(1, NH, HD, HD) for a in g_gates]
    upd["w_igate"] = [a[D:].reshape(1, NH, HD, HD) for a in g_gates]
    behind = upd["w_out"][0][0:1] + upd["w_ple_gate"][0][0:1] + jnp.pad(g_gates[0][0:1], ((0, 0), (0, D - HD)))
    small = jnp.concatenate(
        [jnp.pad(_gather8(g_bf_pad), ((0, 0), (0, D - NH))), g_pre_gain, g_post_gain, g_cb, g_br, g_bi, g_lam, g_aog, g_log,
         g_ple_gain, g_b_gate, g_cw8[:CONV_W], behind, jnp.pad(loss_part, ((0, 7), (0, D - 1)))], axis=0)
    (r_small,) = _exchange("exchange_small", [small], ["bcast"])
    vec_names = ["b_f", "pre_gain", "post_gain", "conv_b", "b_rgate", "b_igate", "lru_lambda", "attn_out_gain",
                 "lru_out_gain", "ple_gain", "b_ple_gate"]
    vec_w = dict(b_f=(b_f, m_b_f, v_b_f), pre_gain=(pre_gain, m_pre_gain, v_pre_gain),
                 post_gain=(post_gain, m_post_gain, v_post_gain), conv_b=(conv_b, m_conv_b, v_conv_b),
                 b_rgate=(b_rgate, m_b_rgate, v_b_rgate), b_igate=(b_igate, m_b_igate, v_b_igate),
                 lru_lambda=(lru_lambda, m_lru_lambda, v_lru_lambda),
                 attn_out_gain=(attn_out_gain, m_attn_out_gain, v_attn_out_gain),
                 lru_out_gain=(lru_out_gain, m_lru_out_gain, v_lru_out_gain), ple_gain=(ple_gain, m_ple_gain, v_ple_gain),
                 b_ple_gate=(b_ple_gate, m_b_ple_gate, v_b_ple_gate))
    conv_mine = lambda a: lax.dynamic_slice_in_dim(a, me * HD, HD, axis=1)

    def small_rows(k):
        rows = [jnp.pad(vec_w[nm][k], ((0, 0), (0, D - vec_w[nm][k].shape[1]))) for nm in vec_names]
        cw = (conv_w, m_conv_w, v_conv_w)[k][0]
        full = lax.dynamic_update_slice_in_dim(jnp.ones((CONV_W, D), F32), cw, me * HD, axis=1)
        return jnp.concatenate(rows + [full, jnp.ones((9, D), F32)], axis=0)

    g_small = _adamw("adamw_small", r_small, small_rows(0), small_rows(1), small_rows(2))
    loss = g_small[0][16, 0]
    for idx, nm in enumerate(vec_names):
        width = vec_w[nm][0].shape[1]
        upd[nm] = [a[idx:idx + 1, :width] for a in g_small]
    base = len(vec_names)
    upd["conv_w"] = [conv_mine(a[base:base + CONV_W])[None] for a in g_small]
    (r_win,) = _exchange_wait("exchange_inw_wait", inw_state, g_small[0], fill_own=False)
    upd["w_in"] = [a.T for a in _adamw("adamw_w_in", r_win[0], wt, m_wt, v_wt, r_win[1], me1)]
    for nm in ("w_in", "w_out", "w_ple", "w_ple_gate"):
        upd[nm] = [a[None] for a in upd[nm]]

    order = ["w_in", "b_f", "pre_gain", "post_gain", "conv_w", "conv_b", "w_rgate", "b_rgate", "w_igate", "b_igate",
             "lru_lambda", "attn_out_gain", "lru_out_gain", "w_out", "w_ple", "ple_gain", "w_ple_gate", "b_ple_gate"]
    outs = [loss, grad_x[None]]
    for k in range(4):
        outs += [upd[nm][k] for nm in order]
    return tuple(outs)
```

```python
import functools

import jax
import jax.numpy as jnp
from jax import lax
from jax.experimental import pallas as pl
from jax.experimental.pallas import tpu as pltpu

F32 = jnp.float32
BF16 = jnp.bfloat16

N_DEV = 8
D = 1024
HD = 128
NH = 8
D_IN = 6152
D_IN_SHARD = D_IN // N_DEV
D_QKV = 3 * D
D_REST = 3 * D + HD
FL_COL = 3 * D
D_PLE = 256
CONV_W = 4
LRU_C = 8.0
RMS_EPS = 1e-6
SCALE = HD ** -0.5
LOG2E = 1.4426950408889634
Q_SCALE = SCALE * LOG2E
NEG = -1e30

ADAM_LR = 0.001
ADAM_B1 = 0.9
ADAM_B2 = 0.999
ADAM_EPS = 1e-08
ADAM_WD = 0.01
ADAM_STEP = 10

TS = 256
TQ = 1024
ROW_PARTS = 2
VMEM_LIMIT = 48 * 1024 * 1024

NT_DIMS = (((1,), (1,)), ((), ()))
TN_DIMS = (((0,), (0,)), ((), ()))


def _params(**kw):
    return pltpu.CompilerParams(vmem_limit_bytes=VMEM_LIMIT, **kw)


def _sigmoid(v):
    return 0.5 * jnp.tanh(0.5 * v) + 0.5


def _sigmoid_rel(v):
    return 1.0 / (1.0 + jnp.exp(-v))


def _rms_fwd(v):
    rstd = lax.rsqrt(jnp.mean(v * v, axis=-1, keepdims=True) + RMS_EPS)
    return v * rstd, rstd


def _rms_bwd(vhat, rstd, dvhat):
    return rstd * (dvhat - vhat * jnp.mean(dvhat * vhat, axis=-1, keepdims=True))


def _colsum(v):
    return jnp.sum(v, axis=0, keepdims=True)


def _rows_iota(t):
    return lax.broadcasted_iota(jnp.int32, (t, 1), 0)


def _scan(a, u, reverse):
    t, c = a.shape
    rows = _rows_iota(t)
    d = 1
    while d < t:
        if d < 8:
            valid = rows < t - d if reverse else rows >= d
            shift = t - d if reverse else d
            u = jnp.where(valid, u + a * pltpu.roll(u, shift, 0), u)
            a = jnp.where(valid, a * pltpu.roll(a, shift, 0), a)
        else:
            zeros, ones = jnp.zeros((d, c), F32), jnp.ones((d, c), F32)
            if reverse:
                u_far, a_far = jnp.concatenate([u[d:], zeros], axis=0), jnp.concatenate([a[d:], ones], axis=0)
            else:
                u_far, a_far = jnp.concatenate([zeros, u[:t - d]], axis=0), jnp.concatenate([ones, a[:t - d]], axis=0)
            u = u + a * u_far
            a = a * a_far
        d *= 2
    return a, u


def _cumsum_fwd(v):
    t = v.shape[0]
    rows = _rows_iota(t)
    d = 1
    while d < t:
        v = jnp.where(rows >= d, v + pltpu.roll(v, d, 0), v)
        d *= 2
    return v


def _cumsum_bwd(v):
    t = v.shape[0]
    rows = _rows_iota(t)
    d = 1
    while d < t:
        v = jnp.where(rows < t - d, v + pltpu.roll(v, t - d, 0), v)
        d *= 2
    return v


def _bias_lanes(v, at, ones_at):
    lane = lax.broadcasted_iota(jnp.int32, v.shape, 1)
    hi = v.astype(BF16).astype(F32)
    mid = (v - hi).astype(BF16).astype(F32)
    lo = ((v - hi) - mid).astype(BF16).astype(F32)
    out = jnp.where((lane >= ones_at) & (lane < ones_at + 3), 1.0, 0.0)
    for k, piece in enumerate((hi, mid, lo)):
        out = jnp.where(lane == at + k, piece, out)
    return out.astype(BF16)


def _shift_down(ext, k, t):
    return pltpu.roll(ext, k, 0)[8:, :] if k else ext[8:, :]


def _shift_up(ext, k, t):
    return pltpu.roll(ext, t + 8 - k, 0)[:t, :] if k else ext[:t, :]


def _exchange(name, arrs, kinds):
    n = len(arrs)
    out_shape = []
    for a, kind in zip(arrs, kinds):
        shp = a.shape if kind == "scatter" else (N_DEV,) + a.shape
        out_shape.append(jax.ShapeDtypeStruct(shp, a.dtype))

    def body(*refs):
        ins, outs = refs[:n], refs[n:2 * n]
        send_sems, recv_sems, local_sems = refs[2 * n:]
        x, y, c = lax.axis_index("x"), lax.axis_index("y"), lax.axis_index("c")
        me = 4 * x + 2 * y + c
        copies = []
        for i in range(n):
            scatter = kinds[i] == "scatter"
            mine = pltpu.make_async_copy(ins[i].at[me] if scatter else ins[i], outs[i].at[me], local_sems.at[i])
            mine.start()
            copies.append(mine)
            for m in range(1, N_DEV):
                px = 1 - x if m & 4 else x
                py = 1 - y if m & 2 else y
                pc = 1 - c if m & 1 else c
                peer = 4 * px + 2 * py + pc
                cp = pltpu.make_async_remote_copy(
                    src_ref=ins[i].at[peer] if scatter else ins[i],
                    dst_ref=outs[i].at[me],
                    send_sem=send_sems.at[i, m - 1],
                    recv_sem=recv_sems.at[i, m - 1],
                    device_id=(px, py, pc),
                    device_id_type=pl.DeviceIdType.MESH,
                )
                cp.start()
                copies.append(cp)
        for cp in copies:
            cp.wait()

    any_spec = pl.BlockSpec(memory_space=pl.ANY)
    return pl.pallas_call(
        body,
        name=name,
        out_shape=out_shape,
        in_specs=[any_spec] * n,
        out_specs=[any_spec] * n,
        scratch_shapes=[
            pltpu.SemaphoreType.DMA((n, N_DEV - 1)),
            pltpu.SemaphoreType.DMA((n, N_DEV - 1)),
            pltpu.SemaphoreType.DMA((n,)),
        ],
        compiler_params=pltpu.CompilerParams(has_side_effects=True),
    )(*arrs)


def _gather_two_level(name, arrs, pieces=1):
    n = len(arrs)
    items = []
    for i, a in enumerate(arrs):
        rows = a.shape[0]
        if pieces > 1 and rows >= 512:
            step = -(-rows // (16 * pieces)) * 16
            items += [(i, r0, min(step, rows - r0)) for r0 in range(0, rows, step)]
        else:
            items.append((i, 0, rows))
    n_items = len(items)

    def body(*refs):
        ins, outs = refs[:n], refs[n:2 * n]
        send_sems, recv_sems, local_sems = refs[2 * n:]
        x, y, c = lax.axis_index("x"), lax.axis_index("y"), lax.axis_index("c")
        me, sibling = (x, y, c), (x, y, 1 - c)
        chips = [(1 - x, y), (x, 1 - y), (1 - x, 1 - y)]

        def rows_of(ref, t):
            i, r0, rn = items[t]
            return ref if rn == arrs[i].shape[0] else ref.at[pl.ds(r0, rn)]

        def slot(t, dev):
            return rows_of(outs[items[t][0]].at[4 * dev[0] + 2 * dev[1] + dev[2]], t)

        def copy(t, k, block, to, from_input=False):
            return pltpu.make_async_remote_copy(
                src_ref=rows_of(ins[items[t][0]], t) if from_input else slot(t, block), dst_ref=slot(t, block),
                send_sem=send_sems.at[t, k], recv_sem=recv_sems.at[t, k],
                device_id=to, device_id_type=pl.DeviceIdType.MESH)

        own, sent = [], []
        for t in range(n_items):
            mine = pltpu.make_async_copy(rows_of(ins[items[t][0]], t), slot(t, me), local_sems.at[t])
            mine.start()
            own.append(mine)
            first = [copy(t, 1 + j, me, (*chip, c), from_input=True) for j, chip in enumerate(chips)]
            first.append(copy(t, 0, me, sibling, from_input=True))
            for cp in first:
                cp.start()
            sent += first
        for t in range(n_items):
            for j, chip in enumerate(chips):
                copy(t, 1 + j, (*chip, c), me).wait_recv()
                fwd = copy(t, 4 + j, (*chip, c), sibling)
                fwd.start()
                sent.append(fwd)
        for t in range(n_items):
            copy(t, 0, sibling, me).wait_recv()
            for j, chip in enumerate(chips):
                copy(t, 4 + j, (*chip, 1 - c), me).wait_recv()
        for cp in sent:
            cp.wait_send()
        for cp in own:
            cp.wait()

    any_spec = pl.BlockSpec(memory_space=pl.ANY)
    return pl.pallas_call(
        body, name=name,
        out_shape=[jax.ShapeDtypeStruct((N_DEV,) + a.shape, a.dtype) for a in arrs],
        in_specs=[any_spec] * n, out_specs=[any_spec] * n,
        scratch_shapes=[pltpu.SemaphoreType.DMA((n_items, 7)), pltpu.SemaphoreType.DMA((n_items, 7)),
                        pltpu.SemaphoreType.DMA((n_items,))],
        compiler_params=pltpu.CompilerParams(has_side_effects=True),
    )(*arrs)


def _peers(x, y, c):
    out = []
    for m in range(1, N_DEV):
        px = 1 - x if m & 4 else x
        py = 1 - y if m & 2 else y
        pc = 1 - c if m & 1 else c
        out.append((m, (px, py, pc), 4 * px + 2 * py + pc))
    return out


def _split_copies(kinds, src_refs, land_refs, send_sems, recv_sems):
    x, y, c = lax.axis_index("x"), lax.axis_index("y"), lax.axis_index("c")
    me = 4 * x + 2 * y + c
    copies = []
    for i, kind in enumerate(kinds):
        for m, peer, pidx in _peers(x, y, c):
            copies.append(pltpu.make_async_remote_copy(
                src_ref=src_refs[i].at[pidx] if kind == "scatter" else src_refs[i],
                dst_ref=land_refs[i].at[me],
                send_sem=send_sems.at[i * (N_DEV - 1) + m - 1],
                recv_sem=recv_sems.at[i * (N_DEV - 1) + m - 1],
                device_id=peer,
                device_id_type=pl.DeviceIdType.MESH,
            ))
    return copies


_HBM_SPEC = pl.BlockSpec(memory_space=pltpu.HBM)
_SEM_SPEC = pl.BlockSpec(memory_space=pltpu.SEMAPHORE)
_DATAFLOW = pltpu.SideEffectType.DATAFLOW_SIDE_EFFECTING


def _exchange_start(name, arrs, kinds, after=None):
    n = len(arrs)
    extra = [] if after is None else [after]
    lands = []
    for a, kind in zip(arrs, kinds):
        shp = a.shape if kind == "scatter" else (N_DEV,) + a.shape
        lands.append(lax.empty(shp, a.dtype))

    def body(*refs):
        src_refs, land_refs = refs[:n], refs[n:2 * n]
        send_sems, recv_sems = refs[2 * n + len(extra):2 * n + len(extra) + 2]
        token = refs[-1]
        for cp in _split_copies(kinds, src_refs, land_refs, send_sems, recv_sems):
            cp.start()
        token[...] = jnp.zeros_like(token)

    n_sem = n * (N_DEV - 1)
    hbm = lambda a: pltpu.HBM(a.shape, a.dtype)
    res = pl.pallas_call(
        body, name=name,
        out_shape=(pltpu.SemaphoreType.DMA((n_sem,)), pltpu.SemaphoreType.DMA((n_sem,)),
                   *[hbm(a) for a in arrs], *[hbm(a) for a in lands], jax.ShapeDtypeStruct((8, HD), F32)),
        in_specs=[_HBM_SPEC] * (2 * n) + [pl.BlockSpec(memory_space=pl.ANY)] * len(extra),
        out_specs=(_SEM_SPEC, _SEM_SPEC, *[_HBM_SPEC] * (2 * n), pl.BlockSpec(memory_space=pltpu.VMEM)),
        input_output_aliases={i: 2 + i for i in range(2 * n)},
        compiler_params=pltpu.CompilerParams(has_side_effects=_DATAFLOW),
    )(*[pltpu.with_memory_space_constraint(a, pltpu.HBM) for a in arrs],
      *[pltpu.with_memory_space_constraint(a, pltpu.HBM) for a in lands], *extra)
    return (kinds, res[0], res[1], res[2:2 + n], res[2 + n:2 + 2 * n]), res[-1]


def _exchange_wait(name, state, after, fill_own=True):
    kinds, send_sems, recv_sems, srcs, lands = state
    n = len(srcs)

    def body(*refs):
        src_refs, land_refs = refs[:n], refs[n:2 * n]
        send_sems_ref, recv_sems_ref = refs[2 * n:2 * n + 2]
        for cp in _split_copies(kinds, src_refs, land_refs, send_sems_ref, recv_sems_ref):
            cp.wait_send()
            cp.wait_recv()

    res = pl.pallas_call(
        body, name=name,
        out_shape=tuple(pltpu.HBM(a.shape, a.dtype) for a in (*srcs, *lands)),
        in_specs=[_HBM_SPEC] * (2 * n) + [_SEM_SPEC, _SEM_SPEC, pl.BlockSpec(memory_space=pl.ANY)],
        out_specs=tuple([_HBM_SPEC] * (2 * n)),
        input_output_aliases={i: i for i in range(2 * n)},
        compiler_params=pltpu.CompilerParams(has_side_effects=_DATAFLOW),
    )(*srcs, *lands, send_sems, recv_sems, after)
    me = 4 * lax.axis_index("x") + 2 * lax.axis_index("y") + lax.axis_index("c")
    outs = []
    for kind, src, land in zip(kinds, res[:n], res[n:]):
        own = lax.dynamic_index_in_dim(src, me, 0, keepdims=False) if kind == "scatter" else src
        outs.append(lax.dynamic_update_index_in_dim(land, own, me, 0) if fill_own else (land, own))
    return outs


def _pick(n, cands):
    for t in cands:
        if n % t == 0:
            return t
    raise ValueError(f"no tile for {n}")


def _mm(name, a, b, mode, out_dtype, after=None):
    if mode == "nn":
        (m, k), (k2, n) = a.shape, b.shape
    elif mode == "nt":
        (m, k), (n, k2) = a.shape, b.shape
    else:
        (k, m), (k2, n) = a.shape, b.shape
    assert k == k2, (name, a.shape, b.shape)
    if mode == "tn":
        tm = _pick(m, (1024, 640, 512, 256, 128))
        tn = _pick(n, (1024, 640, 512, 256, 128))
        tk = _pick(k, (2048, 1024, 512, 256))
    else:
        tm, tn, tk = _pick(m, (512, 256)), n, k
    nk = k // tk

    def body(a_ref, b_ref, *rest):
        o_ref = rest[-2] if nk > 1 else rest[-1]
        av = a_ref[...].astype(BF16)
        bv = b_ref[...].astype(BF16)
        if mode == "nn":
            part = jnp.dot(av, bv, preferred_element_type=F32)
        elif mode == "nt":
            part = lax.dot_general(av, bv, NT_DIMS, preferred_element_type=F32)
        else:
            part = lax.dot_general(av, bv, TN_DIMS, preferred_element_type=F32)
        if nk == 1:
            o_ref[...] = part.astype(out_dtype)
            return
        acc_ref = rest[-1]
        kk = pl.program_id(2)

        @pl.when(kk == 0)
        def _():
            acc_ref[...] = part

        @pl.when(kk > 0)
        def _():
            acc_ref[...] += part

        @pl.when(kk == nk - 1)
        def _():
            o_ref[...] = acc_ref[...].astype(out_dtype)

    if mode == "tn":
        a_spec = pl.BlockSpec((tk, tm), lambda j, i, kk: (kk, i))
    else:
        a_spec = pl.BlockSpec((tm, tk), lambda j, i, kk: (i, kk))
    if mode == "nt":
        b_spec = pl.BlockSpec((tn, tk), lambda j, i, kk: (j, kk))
    else:
        b_spec = pl.BlockSpec((tk, tn), lambda j, i, kk: (kk, j))
    in_specs, args = [a_spec, b_spec], [a, b]
    if after is not None:
        in_specs.append(pl.BlockSpec((8, HD), lambda j, i, kk: (0, 0)))
        args.append(after)
    return pl.pallas_call(
        body,
        name=name,
        grid=(n // tn, m // tm, nk),
        in_specs=in_specs,
        out_specs=pl.BlockSpec((tm, tn), lambda j, i, kk: (i, j)),
        out_shape=jax.ShapeDtypeStruct((m, n), out_dtype),
        scratch_shapes=[pltpu.VMEM((tm, tn), F32)] if nk > 1 else [],
        compiler_params=_params(dimension_semantics=("parallel", "parallel", "arbitrary")),
    )(*args)


def _mm_cat(name, a_list, b, out_dtype, after=None):
    m = a_list[0].shape[0]
    ks = [a.shape[1] for a in a_list]
    n = b.shape[1]
    assert sum(ks) <= b.shape[0], (name, ks, b.shape)
    tm = _pick(m, (512, 256))
    na = len(a_list)

    def body(*refs):
        b_ref, o_ref = refs[na], refs[-1]
        k0, acc = 0, None
        for a_ref, kw in zip(refs[:na], ks):
            part = jnp.dot(a_ref[...].astype(BF16), b_ref[k0:k0 + kw, :], preferred_element_type=F32)
            acc = part if acc is None else acc + part
            k0 += kw
        o_ref[...] = acc.astype(out_dtype)

    in_specs = [pl.BlockSpec((tm, kw), lambda i: (i, 0)) for kw in ks] + [pl.BlockSpec((sum(ks), n), lambda i: (0, 0))]
    args = [*a_list, b]
    if after is not None:
        in_specs.append(pl.BlockSpec((8, HD), lambda i: (0, 0)))
        args.append(after)
    return pl.pallas_call(
        body, name=name, grid=(m // tm,),
        in_specs=in_specs, out_specs=pl.BlockSpec((tm, n), lambda i: (i, 0)),
        out_shape=jax.ShapeDtypeStruct((m, n), out_dtype),
        compiler_params=_params(dimension_semantics=("parallel",)),
    )(*args)


def _row(c, col=0):
    return pl.BlockSpec((TS, c), lambda i: (i, col))


def _vec(r, c):
    return pl.BlockSpec((r, c), lambda i: (0, 0))


def _prenorm_proj(x, pre_gain, w_t, n, after):
    s = x.shape[0]
    tm = 512

    def body(x_ref, g_ref, w_ref, after_ref, xn_ref, z_ref):
        xhat, _ = _rms_fwd(x_ref[...])
        xn = (xhat * g_ref[...]).astype(BF16)
        xn_ref[...] = xn
        z = lax.dot_general(xn, w_ref[...], NT_DIMS, preferred_element_type=F32)
        z_ref[:, :D] = (z[:, :D] * Q_SCALE).astype(BF16)
        z_ref[:, D:] = z[:, D:].astype(BF16)

    return pl.pallas_call(
        body, name="prenorm_proj_qkv", grid=(s // tm,),
        in_specs=[pl.BlockSpec((tm, D), lambda i: (i, 0)), _vec(1, D), _vec(n, D), _vec(8, HD)],
        out_specs=[pl.BlockSpec((tm, D), lambda i: (i, 0)), pl.BlockSpec((tm, n), lambda i: (i, 0))],
        out_shape=[jax.ShapeDtypeStruct((s, D), BF16), jax.ShapeDtypeStruct((s, n), BF16)],
        compiler_params=_params(dimension_semantics=("parallel",)),
    )(x, pre_gain, w_t, after)


def _proj_rest(xn, w_rest_t, bf_pad):
    s = xn.shape[0]
    tm = 512

    def body(x_ref, w_ref, b_ref, z_ref, kx_ref, c_buf, carry):
        @pl.when(pl.program_id(0) == 0)
        def _():
            carry[...] = jnp.zeros_like(carry)

        z = lax.dot_general(x_ref[...], w_ref[...], NT_DIMS, preferred_element_type=F32)
        z_ref[...] = z
        fl = z[:, FL_COL:] + b_ref[...]
        ls = jnp.minimum(fl, 0.0) - jnp.log(1.0 + jnp.exp(-jnp.abs(fl)))
        c_buf[...] = _cumsum_fwd(ls) + carry[0:1, :]
        carry[0:1, :] = c_buf[tm - 1:tm, :]
        cv = c_buf[...]
        for h in range(NH):
            kx_ref[h] = _bias_lanes(jnp.broadcast_to(cv[:, 8 * h:8 * h + 1], (tm, HD)) * (-LOG2E), 0, 3)

    return pl.pallas_call(
        body, name="proj_rest", grid=(s // tm,),
        in_specs=[pl.BlockSpec((tm, D), lambda i: (i, 0)), _vec(D_REST, D), _vec(1, HD)],
        out_specs=[pl.BlockSpec((tm, D_REST), lambda i: (i, 0)), pl.BlockSpec((NH, tm, HD), lambda i: (0, i, 0))],
        out_shape=[jax.ShapeDtypeStruct((s, D_REST), F32), jax.ShapeDtypeStruct((NH, s, HD), BF16)],
        scratch_shapes=[pltpu.VMEM((tm, HD), F32), pltpu.VMEM((8, HD), F32)],
        compiler_params=_params(dimension_semantics=("arbitrary",)),
    )(xn, w_rest_t, bf_pad)


def _attn_fwd(zq, kx):
    s = zq.shape[0]
    n = s // TQ
    nb = TQ // HD

    def body(q_ref, k_ref, v_ref, kx_ref, o_ref, ax_ref):
        i = pl.program_id(1)
        lane = lax.broadcasted_iota(jnp.int32, (TQ, HD), 1)
        row = lax.broadcasted_iota(jnp.int32, (TQ, HD), 0)
        qa = jnp.concatenate([q_ref[...], jnp.where(lane < 3, 1.0, 0.0).astype(BF16)], axis=1)

        def step(j, carry, masked):
            m, l, acc = carry
            rows = pl.ds(pl.multiple_of(j * TQ, TQ), TQ)
            ka = jnp.concatenate([k_ref[rows, :], kx_ref[0, rows, :]], axis=1)
            v_all = v_ref[rows, :]
            rp = TQ // ROW_PARTS
            parts = [slice(rp * t, rp * (t + 1)) for t in range(ROW_PARTS)]
            keys = [rp * (t + 1) if masked else TQ for t in range(ROW_PARTS)]
            u_parts = [lax.dot_general(qa[part], ka[:kn], NT_DIMS, preferred_element_type=F32)
                       for part, kn in zip(parts, keys)]
            out = []
            for t, (part, u, kn) in enumerate(zip(parts, u_parts, keys)):
                us = [u[:, HD * b:HD * (b + 1)] for b in range(kn // HD)]
                if masked:
                    us = [ub if HD * (b + 1) <= rp * t else jnp.where(row[part] >= lane[part] + HD * b, ub, NEG)
                          for b, ub in enumerate(us)]
                v = v_all[:kn]
                bm = functools.reduce(jnp.maximum, us)
                m_new = jnp.maximum(m[part], jnp.max(bm, axis=1, keepdims=True))
                alpha = jnp.exp2(m[part] - m_new)
                ps = [jnp.exp2(ub - m_new) for ub in us]
                l_new = alpha * l[part] + functools.reduce(jnp.add, ps)
                pr = jnp.concatenate(ps, axis=1).astype(BF16)
                out.append((m_new, l_new, alpha * acc[part] + jnp.dot(pr, v, preferred_element_type=F32)))
            return tuple(jnp.concatenate([o[t] for o in out], axis=0) for t in range(3))

        init = (jnp.full((TQ, HD), NEG, F32), jnp.zeros((TQ, HD), F32), jnp.zeros((TQ, HD), F32))
        carry = lax.fori_loop(0, i, lambda j, cr: step(j, cr, False), init)
        m, l, acc = step(i, carry, True)
        l_row = jnp.sum(l, axis=1, keepdims=True)
        o_ref[...] = acc / l_row
        ax_ref[0] = _bias_lanes(-(m + jnp.log(l_row) * LOG2E), 3, 0)

    return pl.pallas_call(
        body, name="attn_fwd", grid=(NH, n),
        in_specs=[
            pl.BlockSpec((TQ, HD), lambda h, i: (i, h)),
            pl.BlockSpec((s, HD), lambda h, i: (0, NH + h)),
            pl.BlockSpec((s, HD), lambda h, i: (0, 2 * NH + h)),
            pl.BlockSpec((1, s, HD), lambda h, i: (h, 0, 0)),
        ],
        out_specs=[pl.BlockSpec((TQ, HD), lambda h, i: (i, h)), pl.BlockSpec((1, TQ, HD), lambda h, i: (h, i, 0))],
        out_shape=[jax.ShapeDtypeStruct((s, D), F32), jax.ShapeDtypeStruct((NH, s, HD), BF16)],
        compiler_params=_params(dimension_semantics=("parallel", "parallel")),
    )(zq, zq, zq, kx)


def _attn_bwd(zq, do, ax, delta, kx, after):
    s = zq.shape[0]
    n = s // TQ
    nb = TQ // HD

    def body(k_ref, v_ref, kx_ref, q_ref, ax_ref, do_ref, dl_ref, after_ref, dq_out, dk_ref, dv_ref, dcs_ref, drs_ref,
             dq_ref):
        j = pl.program_id(1)

        @pl.when(j == 0)
        def _():
            dq_ref[...] = jnp.zeros_like(dq_ref)
            drs_ref[...] = jnp.zeros_like(drs_ref)

        k = k_ref[...]
        v = v_ref[...]
        ka = jnp.concatenate([k, kx_ref[0]], axis=1)
        row = lax.broadcasted_iota(jnp.int32, (TQ, HD), 0)
        lane = lax.broadcasted_iota(jnp.int32, (TQ, HD), 1)

        def step(i, carry, r0, rn, kn, masked):
            dk, dv, dcs = carry
            rows = pl.ds(pl.multiple_of(i * TQ + r0, rn), rn)
            q = q_ref[rows, :]
            dout = do_ref[rows, :]
            dlv = dl_ref[0, rows, :]
            qa = jnp.concatenate([q, ax_ref[0, rows, :]], axis=1)
            u = lax.dot_general(qa, ka[:kn], NT_DIMS, preferred_element_type=F32)
            dp = lax.dot_general(dout, v[:kn], NT_DIMS, preferred_element_type=F32)
            prs, dss = [], []
            for b in range(kn // HD):
                cs = slice(HD * b, HD * (b + 1))
                ub = u[:, cs]
                if masked and HD * (b + 1) > r0:
                    ub = jnp.where(row[:rn] + r0 >= lane[:rn] + HD * b, ub, NEG)
                pb = jnp.exp2(ub)
                prs.append(pb)
                dss.append(pb * (dp[:, cs] - dlv))
            drs_ref[0, rows, :] += functools.reduce(jnp.add, dss)
            ds = jnp.concatenate(dss, axis=1)
            dsb = ds.astype(BF16)
            dcs_new = jnp.sum(ds.reshape(rn // 8, 8, kn), axis=0)
            dv_new = lax.dot_general(jnp.concatenate(prs, axis=1).astype(BF16), dout, TN_DIMS, preferred_element_type=F32)
            dk_new = lax.dot_general(dsb, q, TN_DIMS, preferred_element_type=F32)
            if kn < TQ:
                dcs_new = jnp.concatenate([dcs_new, jnp.zeros((8, TQ - kn), F32)], axis=1)
                dv_new = jnp.concatenate([dv_new, jnp.zeros((TQ - kn, HD), F32)], axis=0)
                dk_new = jnp.concatenate([dk_new, jnp.zeros((TQ - kn, HD), F32)], axis=0)
            dq_ref[rows, :] += jnp.dot(dsb, k[:kn], preferred_element_type=F32) * SCALE
            return dk + dk_new, dv + dv_new, dcs + dcs_new

        carry = (jnp.zeros((TQ, HD), F32), jnp.zeros((TQ, HD), F32), jnp.zeros((8, TQ), F32))
        rp = TQ // ROW_PARTS
        for t in range(ROW_PARTS):
            carry = step(j, carry, rp * t, rp, rp * (t + 1), True)
        dk, dv, dcs = lax.fori_loop(j + 1, n, lambda i, cr: step(i, cr, 0, TQ, TQ, False), carry)
        dk_ref[...] = (dk * (SCALE / Q_SCALE)).astype(BF16)
        dv_ref[...] = dv.astype(BF16)
        dcs_ref[0] = jnp.broadcast_to(_colsum(dcs), (8, TQ))

        @pl.when(j == n - 1)
        def _():
            dq_out[...] = dq_ref[...].astype(BF16)

    return pl.pallas_call(
        body, name="attn_bwd", grid=(NH, n),
        in_specs=[
            pl.BlockSpec((TQ, HD), lambda h, j: (j, NH + h)),
            pl.BlockSpec((TQ, HD), lambda h, j: (j, 2 * NH + h)),
            pl.BlockSpec((1, TQ, HD), lambda h, j: (h, j, 0)),
            pl.BlockSpec((s, HD), lambda h, j: (0, h)),
            pl.BlockSpec((1, s, HD), lambda h, j: (h, 0, 0)),
            pl.BlockSpec((s, HD), lambda h, j: (0, h)),
            pl.BlockSpec((1, s, HD), lambda h, j: (h, 0, 0)),
            pl.BlockSpec((8, HD), lambda h, j: (0, 0)),
        ],
        out_specs=[
            pl.BlockSpec((s, HD), lambda h, j: (0, h)),
            pl.BlockSpec((TQ, HD), lambda h, j: (j, h)),
            pl.BlockSpec((TQ, HD), lambda h, j: (j, h)),
            pl.BlockSpec((1, 8, TQ), lambda h, j: (j, h, 0)),
            pl.BlockSpec((1, s, HD), lambda h, j: (h, 0, 0)),
        ],
        out_shape=[
            jax.ShapeDtypeStruct((s, D), BF16),
            jax.ShapeDtypeStruct((s, D), BF16),
            jax.ShapeDtypeStruct((s, D), BF16),
            jax.ShapeDtypeStruct((n, 8 * NH, TQ), F32),
            jax.ShapeDtypeStruct((NH, s, HD), F32),
        ],
        scratch_shapes=[pltpu.VMEM((s, HD), F32)],
        compiler_params=_params(dimension_semantics=("parallel", "arbitrary")),
    )(zq, zq, kx, zq, ax, do, delta, after)


def _forget_bwd(dcs, drs, zr, bf_pad):
    n = dcs.shape[0]
    s = n * TQ

    def body(dcs_ref, drs_ref, fl_ref, b_ref, dfl_ref, gb_ref, buf, carry):
        i = pl.program_id(0)

        @pl.when(i == 0)
        def _():
            carry[...] = jnp.zeros_like(carry)
            gb_ref[...] = jnp.zeros_like(gb_ref)

        dc_t = jnp.concatenate([dcs_ref[0], jnp.zeros((HD - 8 * NH, TQ), F32)], axis=0)
        lane = lax.broadcasted_iota(jnp.int32, (TQ, HD), 1)
        dc = -dc_t.T
        for hh in range(NH):
            dc = dc + jnp.where(lane == 8 * hh, jnp.sum(drs_ref[hh], axis=1, keepdims=True), 0.0)
        buf[...] = _cumsum_bwd(dc) + carry[0:1, :]
        carry[0:1, :] = buf[0:1, :]
        fl = fl_ref[...] + b_ref[...]
        dfl = buf[...] * _sigmoid_rel(-fl)
        dfl_ref[...] = dfl.astype(BF16)
        gb_ref[...] += _colsum(dfl)

    return pl.pallas_call(
        body, name="forget_bwd", grid=(n,),
        in_specs=[
            pl.BlockSpec((1, 8 * NH, TQ), lambda i: (n - 1 - i, 0, 0)),
            pl.BlockSpec((NH, TQ, HD), lambda i: (0, n - 1 - i, 0)),
            pl.BlockSpec((TQ, HD), lambda i: (n - 1 - i, FL_COL // HD)),
            _vec(1, HD),
        ],
        out_specs=[pl.BlockSpec((TQ, HD), lambda i: (n - 1 - i, 0)), _vec(1, HD)],
        out_shape=[jax.ShapeDtypeStruct((s, HD), BF16), jax.ShapeDtypeStruct((1, HD), F32)],
        scratch_shapes=[pltpu.VMEM((TQ, HD), F32), pltpu.VMEM((8, HD), F32)],
        compiler_params=_params(dimension_semantics=("arbitrary",)),
    )(dcs, drs, zr, bf_pad)


def _gates(xc, w_ref, b, sigmoid):
    xb = xc.astype(BF16)
    pre = jnp.concatenate(
        [jnp.dot(xb[:, HD * g:HD * (g + 1)], w_ref[g], preferred_element_type=F32) for g in range(NH)], axis=1)
    return sigmoid(pre + b)


def _lru_coeffs(r, lam):
    sp = jnp.maximum(-lam, 0.0) + jnp.log(1.0 + jnp.exp(-jnp.abs(lam)))
    log_a = -LRU_C * r * sp
    a = jnp.exp(log_a)
    y = 2.0 * log_a
    em1 = jnp.where(jnp.abs(y) < 0.01, y * (1.0 + y * (0.5 + y * (1.0 / 6.0))), jnp.exp(y) - 1.0)
    em = -em1
    inv_gam = lax.rsqrt(jnp.maximum(em, 1e-37))
    return sp, a, em * inv_gam, inv_gam


def _conv_taps(ext, t):
    return [_shift_down(ext, CONV_W - 1 - jj, t) for jj in range(CONV_W)]


def _lru_fwd(zr, conv_w8, conv_b, w_r, b_r, w_i, b_i, lam):
    s = zr.shape[0]
    n = s // TS
    xl_col = 1

    def body(xl_ref, halo_ref, cw_ref, cb_ref, wr_ref, br_ref, wi_ref, bi_ref, lam_ref, xc_ref, h_ref, carry):
        i = pl.program_id(0)

        @pl.when(i == 0)
        def _():
            carry[...] = jnp.zeros_like(carry)

        halo = jnp.where(i == 0, 0.0, halo_ref[...])
        taps = _conv_taps(jnp.concatenate([halo, xl_ref[...]], axis=0), TS)
        xc = cb_ref[...] + sum(cw_ref[jj:jj + 1, :] * taps[jj] for jj in range(CONV_W))
        xc_ref[...] = xc
        r = _gates(xc, wr_ref, br_ref[...], _sigmoid_rel)
        ig = _gates(xc, wi_ref, bi_ref[...], _sigmoid)
        _, a, gam, _ = _lru_coeffs(r, lam_ref[...])
        a_cum, h_loc = _scan(a, gam * (ig * xc), False)
        h_ref[...] = h_loc + a_cum * carry[0:1, :]
        carry[0:1, :] = h_ref[TS - 1:TS, :]

    return pl.pallas_call(
        body, name="lru_fwd", grid=(n,),
        in_specs=[
            _row(D, xl_col),
            pl.BlockSpec((8, D), lambda i: (jnp.maximum(i * (TS // 8) - 1, 0), xl_col)),
            _vec(8, D), _vec(1, D),
            pl.BlockSpec((NH, HD, HD), lambda i: (0, 0, 0)), _vec(1, D),
            pl.BlockSpec((NH, HD, HD), lambda i: (0, 0, 0)), _vec(1, D),
            _vec(1, D),
        ],
        out_specs=[_row(D), _row(D)],
        out_shape=[jax.ShapeDtypeStruct((s, D), F32), jax.ShapeDtypeStruct((s, D), F32)],
        scratch_shapes=[pltpu.VMEM((8, D), F32)],
        compiler_params=_params(dimension_semantics=("arbitrary",)),
    )(zr, zr, conv_w8, conv_b, w_r, b_r, w_i, b_i, lam)


def _lru_bwd(zr, xc, h, dh, conv_w8, w_r, b_r, w_i, b_i, lam):
    s = zr.shape[0]
    n = s // TS
    xl_col = 1

    def rev(i):
        return n - 1 - i

    def body(xl_ref, xlh_ref, xc_ref, h_ref, hh_ref, dh_ref, cw_ref, wr_ref, br_ref, wi_ref, bi_ref, lam_ref,
             dxl_ref, gwr_ref, gwi_ref, gbr_ref, gbi_ref, glam_ref, gcb_ref, gcw_ref, l_buf, dxc_buf, carry_g, carry_dxc):
        i = pl.program_id(0)
        first = rev(i) == 0

        @pl.when(i == 0)
        def _():
            carry_g[...] = jnp.zeros_like(carry_g)
            carry_dxc[...] = jnp.zeros_like(carry_dxc)
            for ref in (gwr_ref, gwi_ref, gbr_ref, gbi_ref, glam_ref, gcb_ref, gcw_ref):
                ref[...] = jnp.zeros_like(ref)

        rows = _rows_iota(TS)
        xc = xc_ref[...]
        lam = lam_ref[...]
        r = _gates(xc, wr_ref, br_ref[...], _sigmoid_rel)
        ig = _gates(xc, wi_ref, bi_ref[...], _sigmoid)
        sp, a, gam, inv_gam = _lru_coeffs(r, lam)
        g = dh_ref[...] + jnp.where(rows == TS - 1, carry_g[0:1, :], 0.0)
        b = jnp.where(rows == TS - 1, 0.0, pltpu.roll(a, TS - 1, 0))
        l_buf[...] = _scan(b, g, True)[1]
        lv = l_buf[...]
        carry_g[0:1, :] = l_buf[0:1, :] * a[0:1, :]
        h_prev_row = jnp.where(first, 0.0, hh_ref[7:8, :])
        h_prev = jnp.where(rows == 0, h_prev_row, pltpu.roll(h_ref[...], 1, 0))
        dgam = lv * ig * xc
        dig = lv * gam * xc
        dxc = lv * gam * ig
        dla = lv * h_prev * a - dgam * (a * a) * inv_gam
        dr = dla * (-LRU_C) * sp
        glam_ref[...] += _colsum(dla * r) * (LRU_C * _sigmoid_rel(-lam))
        dpr = dr * r * (1.0 - r)
        dpi = dig * ig * (1.0 - ig)
        gbr_ref[...] += _colsum(dpr)
        gbi_ref[...] += _colsum(dpi)
        xb = xc.astype(BF16)
        dprb = dpr.astype(BF16)
        dpib = dpi.astype(BF16)
        back = []
        for gi in range(NH):
            cs = slice(HD * gi, HD * (gi + 1))
            gwr_ref[gi] += lax.dot_general(xb[:, cs], dprb[:, cs], TN_DIMS, preferred_element_type=F32)
            gwi_ref[gi] += lax.dot_general(xb[:, cs], dpib[:, cs], TN_DIMS, preferred_element_type=F32)
            back.append(lax.dot_general(dprb[:, cs], wr_ref[gi], NT_DIMS, preferred_element_type=F32)
                        + lax.dot_general(dpib[:, cs], wi_ref[gi], NT_DIMS, preferred_element_type=F32))
        dxc = dxc + jnp.concatenate(back, axis=1)
        dxc_buf[...] = dxc
        gcb_ref[...] += _colsum(dxc)
        halo = jnp.where(first, 0.0, xlh_ref[...])
        taps = _conv_taps(jnp.concatenate([halo, xl_ref[...]], axis=0), TS)
        for jj in range(CONV_W):
            gcw_ref[jj:jj + 1, :] += _colsum(dxc * taps[jj])
        ext = jnp.concatenate([dxc, carry_dxc[...]], axis=0)
        dxl = sum(cw_ref[jj:jj + 1, :] * _shift_up(ext, CONV_W - 1 - jj, TS) for jj in range(CONV_W))
        dxl_ref[...] = dxl.astype(BF16)
        carry_dxc[...] = dxc_buf[0:8, :]

    rowr = lambda c, col=0: pl.BlockSpec((TS, c), lambda i: (rev(i), col))
    halo = lambda col: pl.BlockSpec((8, D), lambda i: (jnp.maximum(rev(i) * (TS // 8) - 1, 0), col))
    gate_w = pl.BlockSpec((NH, HD, HD), lambda i: (0, 0, 0))
    return pl.pallas_call(
        body, name="lru_bwd", grid=(n,),
        in_specs=[rowr(D, xl_col), halo(xl_col), rowr(D), rowr(D), halo(0), rowr(D),
                  _vec(8, D), gate_w, _vec(1, D), gate_w, _vec(1, D), _vec(1, D)],
        out_specs=[rowr(D), gate_w, gate_w, _vec(1, D), _vec(1, D), _vec(1, D), _vec(1, D), _vec(8, D)],
        out_shape=[
            jax.ShapeDtypeStruct((s, D), BF16),
            jax.ShapeDtypeStruct((NH, HD, HD), F32), jax.ShapeDtypeStruct((NH, HD, HD), F32),
            jax.ShapeDtypeStruct((1, D), F32), jax.ShapeDtypeStruct((1, D), F32), jax.ShapeDtypeStruct((1, D), F32),
            jax.ShapeDtypeStruct((1, D), F32), jax.ShapeDtypeStruct((8, D), F32),
        ],
        scratch_shapes=[pltpu.VMEM((TS, D), F32), pltpu.VMEM((TS, D), F32), pltpu.VMEM((8, D), F32), pltpu.VMEM((8, D), F32)],
        compiler_params=_params(dimension_semantics=("arbitrary",)),
    )(zr, zr, xc, h, h, dh, conv_w8, w_r, b_r, w_i, b_i, lam)


def _silu_parts(g):
    sg = _sigmoid(g)
    return g * sg, sg * (1.0 + g * (1.0 - sg))


def _branch_out_bwd(o, h, zr, dmix, w_out, gain_a, gain_l):
    s = o.shape[0]

    def body(o_ref, ga_ref, h_ref, gl_ref, dm_ref, w_ref, ka_ref, kl_ref,
             do_ref, dl_ref, dga_ref, dh_ref, dgl_ref, gka_ref, gkl_ref):
        @pl.when(pl.program_id(0) == 0)
        def _():
            gka_ref[...] = jnp.zeros_like(gka_ref)
            gkl_ref[...] = jnp.zeros_like(gkl_ref)

        dycat = lax.dot_general(dm_ref[...], w_ref[...], NT_DIMS, preferred_element_type=F32)

        def one(v, g, dy, gain):
            vhat, rstd = _rms_fwd(v)
            sg, dsg = _silu_parts(g)
            dn = dy * sg
            dg = dy * (vhat * gain) * dsg
            return _rms_bwd(vhat, rstd, dn * gain), dg, _colsum(dn * vhat)

        o = o_ref[...]
        dout, dga, gka = one(o, ga_ref[...], dycat[:, :D], ka_ref[...])
        do_ref[...] = dout.astype(BF16)
        dga_ref[...] = dga.astype(BF16)
        gka_ref[...] += gka
        prod = dout * o
        for hh in range(NH):
            dl_ref[hh] = jnp.broadcast_to(jnp.sum(prod[:, HD * hh:HD * (hh + 1)], axis=1, keepdims=True), (TS, HD))
        dh, dgl, gkl = one(h_ref[...], gl_ref[...], dycat[:, D:], kl_ref[...])
        dh_ref[...] = dh
        dgl_ref[...] = dgl.astype(BF16)
        gkl_ref[...] += gkl

    return pl.pallas_call(
        body, name="branch_out_bwd", grid=(s // TS,),
        in_specs=[_row(D), _row(D, 0), _row(D), _row(D, 2), _row(D), _vec(2 * D, D), _vec(1, D), _vec(1, D)],
        out_specs=[_row(D), pl.BlockSpec((NH, TS, HD), lambda i: (0, i, 0)), _row(D), _row(D), _row(D), _vec(1, D), _vec(1, D)],
        out_shape=[
            jax.ShapeDtypeStruct((s, D), BF16), jax.ShapeDtypeStruct((NH, s, HD), F32), jax.ShapeDtypeStruct((s, D), BF16),
            jax.ShapeDtypeStruct((s, D), F32), jax.ShapeDtypeStruct((s, D), BF16),
            jax.ShapeDtypeStruct((1, D), F32), jax.ShapeDtypeStruct((1, D), F32),
        ],
        compiler_params=_params(dimension_semantics=("arbitrary",)),
    )(o, zr, h, zr, dmix, w_out, gain_a, gain_l)


def _residual(x, o, h, zr, gain_a, gain_l, w_out, post_gain):
    s = x.shape[0]

    def body(x_ref, o_ref, ga_ref, h_ref, gl_ref, ka_ref, kl_ref, w_ref, g_ref, y_ref, m_ref, h1_ref, hb_ref):
        ohat, _ = _rms_fwd(o_ref[...])
        y_ref[:, 0:D] = (ohat * ka_ref[...] * _silu_parts(ga_ref[...])[0]).astype(BF16)
        hhat, _ = _rms_fwd(h_ref[...])
        y_ref[:, D:2 * D] = (hhat * kl_ref[...] * _silu_parts(gl_ref[...])[0]).astype(BF16)
        mix = jnp.dot(y_ref[...], w_ref[...], preferred_element_type=F32)
        m_ref[...] = mix
        mhat, _ = _rms_fwd(mix)
        h1 = x_ref[...] + mhat * g_ref[...]
        h1_ref[...] = h1
        hb_ref[...] = h1.astype(BF16)

    return pl.pallas_call(
        body, name="residual", grid=(s // TS,),
        in_specs=[_row(D), _row(D), _row(D, 0), _row(D), _row(D, 2), _vec(1, D), _vec(1, D), _vec(2 * D, D), _vec(1, D)],
        out_specs=[_row(2 * D), _row(D), _row(D), _row(D)],
        out_shape=[jax.ShapeDtypeStruct((s, 2 * D), BF16), jax.ShapeDtypeStruct((s, D), F32),
                   jax.ShapeDtypeStruct((s, D), F32), jax.ShapeDtypeStruct((s, D), BF16)],
        compiler_params=_params(dimension_semantics=("parallel",)),
    )(x, o, zr, h, zr, gain_a, gain_l, w_out, post_gain)


def _head(h1, p, tgt, mix, w_gate, w_ple, ple_gain, b_gate, post_gain):
    s = h1.shape[0]

    def body(h_ref, p_ref, t_ref, m_ref, wg_ref, wp_ref, kg_ref, b_ref, pg_ref,
             loss_ref, dgp_ref, dpe_ref, dh_ref, dm_ref, gk_ref, gb_ref, gg_ref):
        @pl.when(pl.program_id(0) == 0)
        def _():
            for ref in (loss_ref, gk_ref, gb_ref, gg_ref):
                ref[...] = jnp.zeros_like(ref)

        h1 = h_ref[...]
        pe = jnp.dot(p_ref[...].astype(BF16), wp_ref[...], preferred_element_type=F32)
        gp = jnp.dot(h1.astype(BF16), wg_ref[...], preferred_element_type=F32)
        ehat, rstd = _rms_fwd(pe)
        e = ehat * kg_ref[...]
        gate = _sigmoid(gp + b_ref[...])
        diff = (h1 + gate * e) - t_ref[...]
        per_row = jnp.mean(diff * diff, axis=-1, keepdims=True)
        loss_ref[...] += 0.5 * jnp.sum(per_row, axis=0, keepdims=True)
        dy = diff * (1.0 / D)
        dgp = dy * e * gate * (1.0 - gate)
        dgpb = dgp.astype(BF16)
        dgp_ref[...] = dgpb
        gb_ref[...] += _colsum(dgp)
        de = dy * gate
        gk_ref[...] += _colsum(de * ehat)
        dpe_ref[...] = _rms_bwd(ehat, rstd, de * kg_ref[...]).astype(BF16)
        dh1 = dy + lax.dot_general(dgpb, wg_ref[...], NT_DIMS, preferred_element_type=F32)
        dh_ref[...] = dh1
        mhat, rstd_m = _rms_fwd(m_ref[...])
        gg_ref[...] += _colsum(dh1 * mhat)
        dm_ref[...] = _rms_bwd(mhat, rstd_m, dh1 * pg_ref[...]).astype(BF16)

    return pl.pallas_call(
        body, name="head", grid=(s // TS,),
        in_specs=[_row(D), _row(D_PLE), _row(D), _row(D), _vec(D, D), _vec(D_PLE, D), _vec(1, D), _vec(1, D), _vec(1, D)],
        out_specs=[_vec(1, 1), _row(D), _row(D), _row(D), _row(D), _vec(1, D), _vec(1, D), _vec(1, D)],
        out_shape=[
            jax.ShapeDtypeStruct((1, 1), F32), jax.ShapeDtypeStruct((s, D), BF16), jax.ShapeDtypeStruct((s, D), BF16),
            jax.ShapeDtypeStruct((s, D), F32), jax.ShapeDtypeStruct((s, D), BF16),
            jax.ShapeDtypeStruct((1, D), F32), jax.ShapeDtypeStruct((1, D), F32), jax.ShapeDtypeStruct((1, D), F32),
        ],
        compiler_params=_params(dimension_semantics=("arbitrary",)),
    )(h1, p, tgt, mix, w_gate, w_ple, ple_gain, b_gate, post_gain)


def _prenorm_bwd(x, dxn_a, dz_rest, w_rest_t, dh1, pre_gain, after):
    s = x.shape[0]
    ks = [a.shape[1] for a in dz_rest]
    assert sum(ks) == w_rest_t.shape[0]
    nz = len(dz_rest)
    tm = 512
    rowm = lambda c: pl.BlockSpec((tm, c), lambda i: (i, 0))

    def body(*refs):
        x_ref, da_ref = refs[:2]
        w_ref, dh_ref, g_ref, after_ref, dx_ref, gg_ref = refs[2 + nz:]

        @pl.when(pl.program_id(0) == 0)
        def _():
            gg_ref[...] = jnp.zeros_like(gg_ref)

        dxn, k0 = da_ref[...], 0
        for dz_ref, kw in zip(refs[2:2 + nz], ks):
            dxn = dxn + jnp.dot(dz_ref[...], w_ref[k0:k0 + kw, :], preferred_element_type=F32)
            k0 += kw
        xhat, rstd = _rms_fwd(x_ref[...])
        gg_ref[...] += _colsum(dxn * xhat)
        dx_ref[...] = dh_ref[...] + _rms_bwd(xhat, rstd, dxn * g_ref[...])

    return pl.pallas_call(
        body, name="prenorm_bwd", grid=(s // tm,),
        in_specs=[rowm(D), rowm(D)] + [rowm(kw) for kw in ks] + [_vec(*w_rest_t.shape), rowm(D), _vec(1, D), _vec(8, HD)],
        out_specs=[rowm(D), _vec(1, D)],
        out_shape=[jax.ShapeDtypeStruct((s, D), F32), jax.ShapeDtypeStruct((1, D), F32)],
        compiler_params=_params(dimension_semantics=("arbitrary",)),
    )(x, dxn_a, *dz_rest, w_rest_t, dh1, pre_gain, after)


def _adamw(name, parts, w, m, v, own=None, me=None):
    r, c = w.shape
    if r % 8 == 0:
        tr = _pick(r, (256, 128, 16, 8))
        grid = (r // tr,)
        blk = pl.BlockSpec((tr, c), lambda i: (i, 0))
        parts_blk = pl.BlockSpec((N_DEV, tr, c), lambda i: (0, i, 0))
    else:
        tc = _pick(c, (256, 128))
        grid = (c // tc,)
        blk = pl.BlockSpec((r, tc), lambda i: (0, i))
        parts_blk = pl.BlockSpec((N_DEV, r, tc), lambda i: (0, 0, i))

    def body(*refs):
        p_ref, w_ref, m_ref, v_ref = refs[:4]
        g_ref, d_ref, nm_ref, nv_ref = refs[-4:]
        if own is None:
            g = p_ref[0].astype(F32)
            for j in range(1, N_DEV):
                g = g + p_ref[j].astype(F32)
            g_ref[...] = g
        else:
            own_ref, me_ref = refs[4:6]
            g_ref[...] = jnp.zeros_like(g_ref)
            for j in range(N_DEV):
                @pl.when(me_ref[0] == j)
                def _():
                    g_ref[...] += own_ref[...].astype(F32)

                @pl.when(me_ref[0] != j)
                def _():
                    g_ref[...] += p_ref[j].astype(F32)
            g = g_ref[...]
        nm = ADAM_B1 * m_ref[...] + (1.0 - ADAM_B1) * g
        nv = ADAM_B2 * v_ref[...] + (1.0 - ADAM_B2) * (g * g)
        nm_ref[...] = nm
        nv_ref[...] = nv
        m_hat = nm / (1.0 - ADAM_B1 ** ADAM_STEP)
        v_hat = nv / (1.0 - ADAM_B2 ** ADAM_STEP)
        d_ref[...] = -ADAM_LR * (m_hat / (jnp.sqrt(v_hat) + ADAM_EPS) + ADAM_WD * w_ref[...])

    in_specs, args = [parts_blk, blk, blk, blk], [parts, w, m, v]
    if own is not None:
        in_specs += [blk, pl.BlockSpec(memory_space=pltpu.SMEM)]
        args += [own, me]
    return pl.pallas_call(
        body, name=name, grid=grid,
        in_specs=in_specs,
        out_specs=[blk] * 4,
        out_shape=[jax.ShapeDtypeStruct((r, c), F32)] * 4,
        compiler_params=_params(dimension_semantics=("parallel",)),
    )(*args)


def _spread8(v):
    r = v.shape[0]
    return jnp.pad(jnp.pad(v[:, :, None], ((0, 0), (0, 0), (0, 7))).reshape(r, 8 * NH), ((0, 0), (0, HD - 8 * NH)))


def _gather8(v):
    return v[:, :8 * NH].reshape(v.shape[0], NH, 8)[:, :, 0]


def _cols_to_shards(g):
    r, c8 = g.shape
    return g.reshape(r, N_DEV, c8 // N_DEV).transpose(1, 0, 2)


def _shards_to_cols(g):
    n, r, c = g.shape
    return g.transpose(1, 0, 2).reshape(r, n * c)


def kernel(x, p, w_in, b_f, pre_gain, post_gain, conv_w, conv_b, w_rgate, b_rgate, w_igate, b_igate, lru_lambda, attn_out_gain, lru_out_gain, w_out, w_ple, ple_gain, w_ple_gate, b_ple_gate, loss_target, m_w_in, m_b_f, m_pre_gain, m_post_gain, m_conv_w, m_conv_b, m_w_rgate, m_b_rgate, m_w_igate, m_b_igate, m_lru_lambda, m_attn_out_gain, m_lru_out_gain, m_w_out, m_w_ple, m_ple_gain, m_w_ple_gate, m_b_ple_gate, v_w_in, v_b_f, v_pre_gain, v_post_gain, v_conv_w, v_conv_b, v_w_rgate, v_b_rgate, v_w_igate, v_b_igate, v_lru_lambda, v_attn_out_gain, v_lru_out_gain, v_w_out, v_w_ple, v_ple_gain, v_w_ple_gate, v_b_ple_gate):
    me = 4 * lax.axis_index("x") + 2 * lax.axis_index("y") + lax.axis_index("c")
    x2, p2, tgt = x[0], p[0, 0], loss_target[0]

    conv_w_shard8 = jnp.pad(conv_w[0], ((0, 8 - CONV_W), (0, 0)))
    wt, m_wt, v_wt = w_in[0].T, m_w_in[0].T, v_w_in[0].T
    g_wint, g_conv = _gather_two_level("gather_w_in", [wt.astype(BF16), conv_w_shard8])
    win_t = g_wint.reshape(D_IN, D)
    rest_state, rest_token = _exchange_start(
        "gather_rest_start", [w_out[0].astype(BF16), w_ple[0].astype(BF16), w_ple_gate[0].astype(BF16)], ["bcast"] * 3,
        after=g_conv)
    w_rest_t = jnp.concatenate([win_t[D_QKV + NH:], _spread8(win_t[D_QKV:D_QKV + NH].T).T], axis=0)
    conv_w8 = _shards_to_cols(g_conv)
    bf_pad = _spread8(b_f)
    w_r, w_i = w_rgate[0].astype(BF16), w_igate[0].astype(BF16)

    xn, zq = _prenorm_proj(x2, pre_gain, win_t, D_QKV, rest_token)
    zr, kx = _proj_rest(xn, w_rest_t, bf_pad)
    o, ax = _attn_fwd(zq, kx)
    xc, h = _lru_fwd(zr, conv_w8, conv_b, w_r, b_rgate, w_i, b_igate, lru_lambda)
    g_wout, g_wple, g_wpg = _exchange_wait("gather_rest_wait", rest_state, h)
    wout_full = g_wout.reshape(2 * D, D)
    wple_full = _shards_to_cols(g_wple)
    wpg_full = g_wpg.reshape(D, D)
    ycat, mix, h1, h1b = _residual(x2, o, h, zr, attn_out_gain, lru_out_gain, wout_full, post_gain)

    loss_part, dgp, dpe, dh1, dmix, g_ple_gain, g_b_gate, g_post_gain = _head(
        h1, p2, tgt, mix, wpg_full, wple_full, ple_gain, b_ple_gate, post_gain)
    gw_pg = _mm("bwd_gate_w", h1b, dgp, "tn", BF16)
    gw_ple = _mm("bwd_ple_w", p2, dpe, "tn", BF16)
    gw_out = _mm("bwd_out_w", ycat, dmix, "tn", BF16)
    do, delta, dga, dh, dgl, g_aog, g_log = _branch_out_bwd(o, h, zr, dmix, wout_full, attn_out_gain, lru_out_gain)
    dxl, g_wr, g_wi, g_br, g_bi, g_lam, g_cb, g_cw8 = _lru_bwd(
        zr, xc, h, dh, conv_w8, w_r, b_rgate, w_i, b_igate, lru_lambda)
    gates = jnp.concatenate([g_wr.reshape(D, HD), g_wi.reshape(D, HD)], axis=0).astype(BF16)
    outw_state, outw_token = _exchange_start(
        "exchange_outw_start",
        [gw_out.reshape(N_DEV, 2 * D // N_DEV, D), _cols_to_shards(gw_ple), gw_pg.reshape(N_DEV, D // N_DEV, D), gates],
        ["scatter"] * 3 + ["bcast"])
    dq, dk, dv, dcs, drs = _attn_bwd(zq, do, ax, delta, kx, outw_token)
    tn = lambda nm, dz: _mm("bwd_w_" + nm, dz, xn, "tn", BF16)
    gw_in_t = jnp.concatenate([tn("q", dq), tn("k", dk), tn("v", dv), jnp.zeros((NH, D), BF16),
                               tn("ga", dga), tn("xl", dxl), tn("gl", dgl)], axis=0)
    inw_state, inw_token = _exchange_start(
        "exchange_inw_start", [gw_in_t.reshape(N_DEV, D_IN_SHARD, D)], ["scatter"])
    dfl, g_bf_pad = _forget_bwd(dcs, drs, zr, bf_pad + inw_token[0:1, :])
    gw_fl = _gather8(_mm("bwd_w_fl", dfl, xn, "tn", F32).T).T
    dxn_a = _mm_cat("bwd_qkv_x", [dq, dk, dv], win_t, F32, after=inw_token)
    grad_x, g_pre_gain = _prenorm_bwd(x2, dxn_a, [dga, dxl, dgl, dfl], w_rest_t, dh1, pre_gain, inw_token)

    upd = {}
    me1 = me.reshape(1).astype(jnp.int32)
    r_wout, r_wple, r_wpg, r_gates = _exchange_wait("exchange_outw_wait", outw_state, grad_x, fill_own=False)
    upd["w_out"] = _adamw("adamw_w_out", r_wout[0], w_out[0], m_w_out[0], v_w_out[0], r_wout[1], me1)
    upd["w_ple"] = _adamw("adamw_w_ple", r_wple[0], w_ple[0], m_w_ple[0], v_w_ple[0], r_wple[1], me1)
    upd["w_ple_gate"] = _adamw("adamw_w_ple_gate", r_wpg[0], w_ple_gate[0], m_w_ple_gate[0], v_w_ple_gate[0], r_wpg[1], me1)
    gates_of = lambda a, b: jnp.concatenate([a[0].reshape(D, HD), b[0].reshape(D, HD)], axis=0)
    g_gates = _adamw("adamw_gates", r_gates[0], gates_of(w_rgate, w_igate), gates_of(m_w_rgate, m_w_igate),
                     gates_of(v_w_rgate, v_w_igate), r_gates[1], me1)
    upd["w_rgate"] = [a[:D].reshape(1, NH, HD, HD) for a in g_gates]
    upd["w_igate"] = [a[D:].reshape(1, NH, HD, HD) for a in g_gates]
    behind = upd["w_out"][0][0:1] + upd["w_ple_gate"][0][0:1] + jnp.pad(g_gates[0][0:1], ((0, 0), (0, D - HD)))
    small = jnp.concatenate(
        [jnp.pad(_gather8(g_bf_pad), ((0, 0), (0, D - NH))), g_pre_gain, g_post_gain, g_cb, g_br, g_bi, g_lam, g_aog, g_log,
         g_ple_gain, g_b_gate, g_cw8[:CONV_W], behind, jnp.pad(loss_part, ((0, 7), (0, D - 1))), gw_fl], axis=0)
    (r_small,) = _exchange("exchange_small", [small], ["bcast"])
    vec_names = ["b_f", "pre_gain", "post_gain", "conv_b", "b_rgate", "b_igate", "lru_lambda", "attn_out_gain",
                 "lru_out_gain", "ple_gain", "b_ple_gate"]
    vec_w = dict(b_f=(b_f, m_b_f, v_b_f), pre_gain=(pre_gain, m_pre_gain, v_pre_gain),
                 post_gain=(post_gain, m_post_gain, v_post_gain), conv_b=(conv_b, m_conv_b, v_conv_b),
                 b_rgate=(b_rgate, m_b_rgate, v_b_rgate), b_igate=(b_igate, m_b_igate, v_b_igate),
                 lru_lambda=(lru_lambda, m_lru_lambda, v_lru_lambda),
                 attn_out_gain=(attn_out_gain, m_attn_out_gain, v_attn_out_gain),
                 lru_out_gain=(lru_out_gain, m_lru_out_gain, v_lru_out_gain), ple_gain=(ple_gain, m_ple_gain, v_ple_gain),
                 b_ple_gate=(b_ple_gate, m_b_ple_gate, v_b_ple_gate))
    conv_mine = lambda a: lax.dynamic_slice_in_dim(a, me * HD, HD, axis=1)

    def small_rows(k):
        rows = [jnp.pad(vec_w[nm][k], ((0, 0), (0, D - vec_w[nm][k].shape[1]))) for nm in vec_names]
        cw = (conv_w, m_conv_w, v_conv_w)[k][0]
        full = lax.dynamic_update_slice_in_dim(jnp.ones((CONV_W, D), F32), cw, me * HD, axis=1)
        return jnp.concatenate(rows + [full, jnp.ones((17, D), F32)], axis=0)

    g_small = _adamw("adamw_small", r_small, small_rows(0), small_rows(1), small_rows(2))
    loss = g_small[0][16, 0]
    for idx, nm in enumerate(vec_names):
        width = vec_w[nm][0].shape[1]
        upd[nm] = [a[idx:idx + 1, :width] for a in g_small]
    base = len(vec_names)
    upd["conv_w"] = [conv_mine(a[base:base + CONV_W])[None] for a in g_small]
    (r_win,) = _exchange_wait("exchange_inw_wait", inw_state, g_small[0], fill_own=False)
    fl_sum = g_small[0][24:24 + NH]
    dev_a, row_a = divmod(D_QKV, D_IN_SHARD)
    n_a = D_IN_SHARD - row_a
    assert 2 * n_a == NH, "the forget rows are taken to straddle two row blocks evenly"
    own_win = r_win[1].astype(F32)
    at = jnp.where(me == dev_a, row_a, 0)
    rows_new = jnp.where(me == dev_a, fl_sum[:n_a], fl_sum[n_a:])
    rows_new = jnp.where((me == dev_a) | (me == dev_a + 1), rows_new, lax.dynamic_slice_in_dim(own_win, at, n_a, axis=0))
    own_win = lax.dynamic_update_slice_in_dim(own_win, rows_new, at, axis=0)
    upd["w_in"] = [a.T for a in _adamw("adamw_w_in", r_win[0], wt, m_wt, v_wt, own_win, me1)]
    for nm in ("w_in", "w_out", "w_ple", "w_ple_gate"):
        upd[nm] = [a[None] for a in upd[nm]]

    order = ["w_in", "b_f", "pre_gain", "post_gain", "conv_w", "conv_b", "w_rgate", "b_rgate", "w_igate", "b_igate",
             "lru_lambda", "attn_out_gain", "lru_out_gain", "w_out", "w_ple", "ple_gain", "w_ple_gate", "b_ple_gate"]
    outs = [loss, grad_x[None]]
    for k in range(4):
        outs += [upd[nm][k] for nm in order]
    return tuple(outs)
```

```python
import functools

import jax
import jax.numpy as jnp
from jax import lax
from jax.experimental import pallas as pl
from jax.experimental.pallas import tpu as pltpu

F32 = jnp.float32
BF16 = jnp.bfloat16

N_DEV = 8
D = 1024
HD = 128
NH = 8
D_IN = 6152
D_IN_SHARD = D_IN // N_DEV
D_QKV = 3 * D
D_REST = 3 * D + HD
FL_COL = 3 * D
D_PLE = 256
CONV_W = 4
LRU_C = 8.0
RMS_EPS = 1e-6
SCALE = HD ** -0.5
LOG2E = 1.4426950408889634
Q_SCALE = SCALE * LOG2E
NEG = -1e30

ADAM_LR = 0.001
ADAM_B1 = 0.9
ADAM_B2 = 0.999
ADAM_EPS = 1e-08
ADAM_WD = 0.01
ADAM_STEP = 10

TS = 256
TQ = 1024
ROW_PARTS = 2
VMEM_LIMIT = 48 * 1024 * 1024

NT_DIMS = (((1,), (1,)), ((), ()))
TN_DIMS = (((0,), (0,)), ((), ()))


def _params(**kw):
    return pltpu.CompilerParams(vmem_limit_bytes=VMEM_LIMIT, **kw)


def _sigmoid(v):
    return 0.5 * jnp.tanh(0.5 * v) + 0.5


def _sigmoid_rel(v):
    return 1.0 / (1.0 + jnp.exp(-v))


def _rms_fwd(v):
    rstd = lax.rsqrt(jnp.mean(v * v, axis=-1, keepdims=True) + RMS_EPS)
    return v * rstd, rstd


def _rms_bwd(vhat, rstd, dvhat):
    return rstd * (dvhat - vhat * jnp.mean(dvhat * vhat, axis=-1, keepdims=True))


def _colsum(v):
    return jnp.sum(v, axis=0, keepdims=True)


def _rows_iota(t):
    return lax.broadcasted_iota(jnp.int32, (t, 1), 0)


def _scan(a, u, reverse):
    t, c = a.shape
    rows = _rows_iota(t)
    d = 1
    while d < t:
        if d < 8:
            valid = rows < t - d if reverse else rows >= d
            shift = t - d if reverse else d
            u = jnp.where(valid, u + a * pltpu.roll(u, shift, 0), u)
            a = jnp.where(valid, a * pltpu.roll(a, shift, 0), a)
        else:
            zeros, ones = jnp.zeros((d, c), F32), jnp.ones((d, c), F32)
            if reverse:
                u_far, a_far = jnp.concatenate([u[d:], zeros], axis=0), jnp.concatenate([a[d:], ones], axis=0)
            else:
                u_far, a_far = jnp.concatenate([zeros, u[:t - d]], axis=0), jnp.concatenate([ones, a[:t - d]], axis=0)
            u = u + a * u_far
            a = a * a_far
        d *= 2
    return a, u


def _cumsum_fwd(v):
    t = v.shape[0]
    rows = _rows_iota(t)
    d = 1
    while d < t:
        v = jnp.where(rows >= d, v + pltpu.roll(v, d, 0), v)
        d *= 2
    return v


def _cumsum_bwd(v):
    t = v.shape[0]
    rows = _rows_iota(t)
    d = 1
    while d < t:
        v = jnp.where(rows < t - d, v + pltpu.roll(v, t - d, 0), v)
        d *= 2
    return v


def _bias_lanes(v, at, ones_at):
    lane = lax.broadcasted_iota(jnp.int32, v.shape, 1)
    hi = v.astype(BF16).astype(F32)
    mid = (v - hi).astype(BF16).astype(F32)
    lo = ((v - hi) - mid).astype(BF16).astype(F32)
    out = jnp.where((lane >= ones_at) & (lane < ones_at + 3), 1.0, 0.0)
    for k, piece in enumerate((hi, mid, lo)):
        out = jnp.where(lane == at + k, piece, out)
    return out.astype(BF16)


def _shift_down(ext, k, t):
    return pltpu.roll(ext, k, 0)[8:, :] if k else ext[8:, :]


def _shift_up(ext, k, t):
    return pltpu.roll(ext, t + 8 - k, 0)[:t, :] if k else ext[:t, :]


def _exchange(name, arrs, kinds, before=()):
    n = len(arrs)
    nb = len(before)
    out_shape = []
    for a, kind in zip(arrs, kinds):
        shp = a.shape if kind == "scatter" else (N_DEV,) + a.shape
        out_shape.append(jax.ShapeDtypeStruct(shp, a.dtype))

    def body(*refs):
        ins, outs = refs[:n], refs[n + nb:2 * n + nb]
        send_sems, recv_sems, local_sems = refs[2 * n + nb:]
        x, y, c = lax.axis_index("x"), lax.axis_index("y"), lax.axis_index("c")
        me = 4 * x + 2 * y + c
        copies = []
        for i in range(n):
            scatter = kinds[i] == "scatter"
            mine = pltpu.make_async_copy(ins[i].at[me] if scatter else ins[i], outs[i].at[me], local_sems.at[i])
            mine.start()
            copies.append(mine)
            for m in range(1, N_DEV):
                px = 1 - x if m & 4 else x
                py = 1 - y if m & 2 else y
                pc = 1 - c if m & 1 else c
                peer = 4 * px + 2 * py + pc
                cp = pltpu.make_async_remote_copy(
                    src_ref=ins[i].at[peer] if scatter else ins[i],
                    dst_ref=outs[i].at[me],
                    send_sem=send_sems.at[i, m - 1],
                    recv_sem=recv_sems.at[i, m - 1],
                    device_id=(px, py, pc),
                    device_id_type=pl.DeviceIdType.MESH,
                )
                cp.start()
                copies.append(cp)
        for cp in copies:
            cp.wait()

    any_spec = pl.BlockSpec(memory_space=pl.ANY)
    return pl.pallas_call(
        body,
        name=name,
        out_shape=out_shape,
        in_specs=[any_spec] * (n + nb),
        out_specs=[any_spec] * n,
        scratch_shapes=[
            pltpu.SemaphoreType.DMA((n, N_DEV - 1)),
            pltpu.SemaphoreType.DMA((n, N_DEV - 1)),
            pltpu.SemaphoreType.DMA((n,)),
        ],
        compiler_params=pltpu.CompilerParams(has_side_effects=True),
    )(*arrs, *before)


def _gather_two_level(name, arrs, pieces=1):
    n = len(arrs)
    items = []
    for i, a in enumerate(arrs):
        rows = a.shape[0]
        if pieces > 1 and rows >= 512:
            step = -(-rows // (16 * pieces)) * 16
            items += [(i, r0, min(step, rows - r0)) for r0 in range(0, rows, step)]
        else:
            items.append((i, 0, rows))
    n_items = len(items)

    def body(*refs):
        ins, outs = refs[:n], refs[n:2 * n]
        send_sems, recv_sems, local_sems = refs[2 * n:]
        x, y, c = lax.axis_index("x"), lax.axis_index("y"), lax.axis_index("c")
        me, sibling = (x, y, c), (x, y, 1 - c)
        chips = [(1 - x, y), (x, 1 - y), (1 - x, 1 - y)]

        def rows_of(ref, t):
            i, r0, rn = items[t]
            return ref if rn == arrs[i].shape[0] else ref.at[pl.ds(r0, rn)]

        def slot(t, dev):
            return rows_of(outs[items[t][0]].at[4 * dev[0] + 2 * dev[1] + dev[2]], t)

        def copy(t, k, block, to, from_input=False):
            return pltpu.make_async_remote_copy(
                src_ref=rows_of(ins[items[t][0]], t) if from_input else slot(t, block), dst_ref=slot(t, block),
                send_sem=send_sems.at[t, k], recv_sem=recv_sems.at[t, k],
                device_id=to, device_id_type=pl.DeviceIdType.MESH)

        own, sent = [], []
        for t in range(n_items):
            mine = pltpu.make_async_copy(rows_of(ins[items[t][0]], t), slot(t, me), local_sems.at[t])
            mine.start()
            own.append(mine)
            first = [copy(t, 1 + j, me, (*chip, c), from_input=True) for j, chip in enumerate(chips)]
            first.append(copy(t, 0, me, sibling, from_input=True))
            for cp in first:
                cp.start()
            sent += first
        for t in range(n_items):
            for j, chip in enumerate(chips):
                copy(t, 1 + j, (*chip, c), me).wait_recv()
                fwd = copy(t, 4 + j, (*chip, c), sibling)
                fwd.start()
                sent.append(fwd)
        for t in range(n_items):
            copy(t, 0, sibling, me).wait_recv()
            for j, chip in enumerate(chips):
                copy(t, 4 + j, (*chip, 1 - c), me).wait_recv()
        for cp in sent:
            cp.wait_send()
        for cp in own:
            cp.wait()

    any_spec = pl.BlockSpec(memory_space=pl.ANY)
    return pl.pallas_call(
        body, name=name,
        out_shape=[jax.ShapeDtypeStruct((N_DEV,) + a.shape, a.dtype) for a in arrs],
        in_specs=[any_spec] * n, out_specs=[any_spec] * n,
        scratch_shapes=[pltpu.SemaphoreType.DMA((n_items, 7)), pltpu.SemaphoreType.DMA((n_items, 7)),
                        pltpu.SemaphoreType.DMA((n_items,))],
        compiler_params=pltpu.CompilerParams(has_side_effects=True),
    )(*arrs)


def _peers(x, y, c):
    out = []
    for m in range(1, N_DEV):
        px = 1 - x if m & 4 else x
        py = 1 - y if m & 2 else y
        pc = 1 - c if m & 1 else c
        out.append((m, (px, py, pc), 4 * px + 2 * py + pc))
    return out


def _split_copies(kinds, src_refs, land_refs, send_sems, recv_sems):
    x, y, c = lax.axis_index("x"), lax.axis_index("y"), lax.axis_index("c")
    me = 4 * x + 2 * y + c
    copies = []
    for i, kind in enumerate(kinds):
        for m, peer, pidx in _peers(x, y, c):
            copies.append(pltpu.make_async_remote_copy(
                src_ref=src_refs[i].at[pidx] if kind == "scatter" else src_refs[i],
                dst_ref=land_refs[i].at[me],
                send_sem=send_sems.at[i * (N_DEV - 1) + m - 1],
                recv_sem=recv_sems.at[i * (N_DEV - 1) + m - 1],
                device_id=peer,
                device_id_type=pl.DeviceIdType.MESH,
            ))
    return copies


_HBM_SPEC = pl.BlockSpec(memory_space=pltpu.HBM)
_SEM_SPEC = pl.BlockSpec(memory_space=pltpu.SEMAPHORE)
_DATAFLOW = pltpu.SideEffectType.DATAFLOW_SIDE_EFFECTING


def _exchange_start(name, arrs, kinds, after=None):
    n = len(arrs)
    extra = [] if after is None else [after]
    lands = []
    for a, kind in zip(arrs, kinds):
        shp = a.shape if kind == "scatter" else (N_DEV,) + a.shape
        lands.append(lax.empty(shp, a.dtype))

    def body(*refs):
        src_refs, land_refs = refs[:n], refs[n:2 * n]
        send_sems, recv_sems = refs[2 * n + len(extra):2 * n + len(extra) + 2]
        token = refs[-1]
        for cp in _split_copies(kinds, src_refs, land_refs, send_sems, recv_sems):
            cp.start()
        token[...] = jnp.zeros_like(token)

    n_sem = n * (N_DEV - 1)
    hbm = lambda a: pltpu.HBM(a.shape, a.dtype)
    res = pl.pallas_call(
        body, name=name,
        out_shape=(pltpu.SemaphoreType.DMA((n_sem,)), pltpu.SemaphoreType.DMA((n_sem,)),
                   *[hbm(a) for a in arrs], *[hbm(a) for a in lands], jax.ShapeDtypeStruct((8, HD), F32)),
        in_specs=[_HBM_SPEC] * (2 * n) + [pl.BlockSpec(memory_space=pl.ANY)] * len(extra),
        out_specs=(_SEM_SPEC, _SEM_SPEC, *[_HBM_SPEC] * (2 * n), pl.BlockSpec(memory_space=pltpu.VMEM)),
        input_output_aliases={i: 2 + i for i in range(2 * n)},
        compiler_params=pltpu.CompilerParams(has_side_effects=_DATAFLOW),
    )(*[pltpu.with_memory_space_constraint(a, pltpu.HBM) for a in arrs],
      *[pltpu.with_memory_space_constraint(a, pltpu.HBM) for a in lands], *extra)
    return (kinds, res[0], res[1], res[2:2 + n], res[2 + n:2 + 2 * n]), res[-1]


def _exchange_wait(name, state, after, fill_own=True):
    kinds, send_sems, recv_sems, srcs, lands = state
    n = len(srcs)

    def body(*refs):
        src_refs, land_refs = refs[:n], refs[n:2 * n]
        send_sems_ref, recv_sems_ref = refs[2 * n:2 * n + 2]
        for cp in _split_copies(kinds, src_refs, land_refs, send_sems_ref, recv_sems_ref):
            cp.wait_send()
            cp.wait_recv()

    res = pl.pallas_call(
        body, name=name,
        out_shape=tuple(pltpu.HBM(a.shape, a.dtype) for a in (*srcs, *lands)),
        in_specs=[_HBM_SPEC] * (2 * n) + [_SEM_SPEC, _SEM_SPEC, pl.BlockSpec(memory_space=pl.ANY)],
        out_specs=tuple([_HBM_SPEC] * (2 * n)),
        input_output_aliases={i: i for i in range(2 * n)},
        compiler_params=pltpu.CompilerParams(has_side_effects=_DATAFLOW),
    )(*srcs, *lands, send_sems, recv_sems, after)
    me = 4 * lax.axis_index("x") + 2 * lax.axis_index("y") + lax.axis_index("c")
    outs = []
    for kind, src, land in zip(kinds, res[:n], res[n:]):
        own = lax.dynamic_index_in_dim(src, me, 0, keepdims=False) if kind == "scatter" else src
        outs.append(lax.dynamic_update_index_in_dim(land, own, me, 0) if fill_own else (land, own))
    return outs


def _pick(n, cands):
    for t in cands:
        if n % t == 0:
            return t
    raise ValueError(f"no tile for {n}")


def _mm(name, a, b, mode, out_dtype, after=None):
    if mode == "nn":
        (m, k), (k2, n) = a.shape, b.shape
    elif mode == "nt":
        (m, k), (n, k2) = a.shape, b.shape
    else:
        (k, m), (k2, n) = a.shape, b.shape
    assert k == k2, (name, a.shape, b.shape)
    if mode == "tn":
        tm = _pick(m, (1024, 640, 512, 256, 128))
        tn = _pick(n, (1024, 640, 512, 256, 128))
        tk = _pick(k, (2048, 1024, 512, 256))
    else:
        tm, tn, tk = _pick(m, (512, 256)), n, k
    nk = k // tk

    def body(a_ref, b_ref, *rest):
        o_ref = rest[-2] if nk > 1 else rest[-1]
        av = a_ref[...].astype(BF16)
        bv = b_ref[...].astype(BF16)
        if mode == "nn":
            part = jnp.dot(av, bv, preferred_element_type=F32)
        elif mode == "nt":
            part = lax.dot_general(av, bv, NT_DIMS, preferred_element_type=F32)
        else:
            part = lax.dot_general(av, bv, TN_DIMS, preferred_element_type=F32)
        if nk == 1:
            o_ref[...] = part.astype(out_dtype)
            return
        acc_ref = rest[-1]
        kk = pl.program_id(2)

        @pl.when(kk == 0)
        def _():
            acc_ref[...] = part

        @pl.when(kk > 0)
        def _():
            acc_ref[...] += part

        @pl.when(kk == nk - 1)
        def _():
            o_ref[...] = acc_ref[...].astype(out_dtype)

    if mode == "tn":
        a_spec = pl.BlockSpec((tk, tm), lambda j, i, kk: (kk, i))
    else:
        a_spec = pl.BlockSpec((tm, tk), lambda j, i, kk: (i, kk))
    if mode == "nt":
        b_spec = pl.BlockSpec((tn, tk), lambda j, i, kk: (j, kk))
    else:
        b_spec = pl.BlockSpec((tk, tn), lambda j, i, kk: (kk, j))
    in_specs, args = [a_spec, b_spec], [a, b]
    if after is not None:
        in_specs.append(pl.BlockSpec((8, HD), lambda j, i, kk: (0, 0)))
        args.append(after)
    return pl.pallas_call(
        body,
        name=name,
        grid=(n // tn, m // tm, nk),
        in_specs=in_specs,
        out_specs=pl.BlockSpec((tm, tn), lambda j, i, kk: (i, j)),
        out_shape=jax.ShapeDtypeStruct((m, n), out_dtype),
        scratch_shapes=[pltpu.VMEM((tm, tn), F32)] if nk > 1 else [],
        compiler_params=_params(dimension_semantics=("parallel", "parallel", "arbitrary")),
    )(*args)


def _mm_cat(name, a_list, b, out_dtype, after=None):
    m = a_list[0].shape[0]
    ks = [a.shape[1] for a in a_list]
    n = b.shape[1]
    assert sum(ks) <= b.shape[0], (name, ks, b.shape)
    tm = _pick(m, (512, 256))
    na = len(a_list)

    def body(*refs):
        b_ref, o_ref = refs[na], refs[-1]
        k0, acc = 0, None
        for a_ref, kw in zip(refs[:na], ks):
            part = jnp.dot(a_ref[...].astype(BF16), b_ref[k0:k0 + kw, :], preferred_element_type=F32)
            acc = part if acc is None else acc + part
            k0 += kw
        o_ref[...] = acc.astype(out_dtype)

    in_specs = [pl.BlockSpec((tm, kw), lambda i: (i, 0)) for kw in ks] + [pl.BlockSpec((sum(ks), n), lambda i: (0, 0))]
    args = [*a_list, b]
    if after is not None:
        in_specs.append(pl.BlockSpec((8, HD), lambda i: (0, 0)))
        args.append(after)
    return pl.pallas_call(
        body, name=name, grid=(m // tm,),
        in_specs=in_specs, out_specs=pl.BlockSpec((tm, n), lambda i: (i, 0)),
        out_shape=jax.ShapeDtypeStruct((m, n), out_dtype),
        compiler_params=_params(dimension_semantics=("parallel",)),
    )(*args)


def _row(c, col=0):
    return pl.BlockSpec((TS, c), lambda i: (i, col))


def _vec(r, c):
    return pl.BlockSpec((r, c), lambda i: (0, 0))


def _prenorm_proj(x, pre_gain, w_t, n, after):
    s = x.shape[0]
    tm = 512

    def body(x_ref, g_ref, w_ref, after_ref, xn_ref, z_ref):
        xhat, _ = _rms_fwd(x_ref[...])
        xn = (xhat * g_ref[...]).astype(BF16)
        xn_ref[...] = xn
        z = lax.dot_general(xn, w_ref[...], NT_DIMS, preferred_element_type=F32)
        z_ref[:, :D] = (z[:, :D] * Q_SCALE).astype(BF16)
        z_ref[:, D:] = z[:, D:].astype(BF16)

    return pl.pallas_call(
        body, name="prenorm_proj_qkv", grid=(s // tm,),
        in_specs=[pl.BlockSpec((tm, D), lambda i: (i, 0)), _vec(1, D), _vec(n, D), _vec(8, HD)],
        out_specs=[pl.BlockSpec((tm, D), lambda i: (i, 0)), pl.BlockSpec((tm, n), lambda i: (i, 0))],
        out_shape=[jax.ShapeDtypeStruct((s, D), BF16), jax.ShapeDtypeStruct((s, n), BF16)],
        compiler_params=_params(dimension_semantics=("parallel",)),
    )(x, pre_gain, w_t, after)


def _proj_rest(xn, w_rest_t, bf_pad):
    s = xn.shape[0]
    tm = 512

    def body(x_ref, w_ref, b_ref, z_ref, kx_ref, c_buf, carry):
        @pl.when(pl.program_id(0) == 0)
        def _():
            carry[...] = jnp.zeros_like(carry)

        z = lax.dot_general(x_ref[...], w_ref[...], NT_DIMS, preferred_element_type=F32)
        z_ref[...] = z
        fl = z[:, FL_COL:] + b_ref[...]
        ls = jnp.minimum(fl, 0.0) - jnp.log(1.0 + jnp.exp(-jnp.abs(fl)))
        c_buf[...] = _cumsum_fwd(ls) + carry[0:1, :]
        carry[0:1, :] = c_buf[tm - 1:tm, :]
        cv = c_buf[...]
        for h in range(NH):
            kx_ref[h] = _bias_lanes(jnp.broadcast_to(cv[:, 8 * h:8 * h + 1], (tm, HD)) * (-LOG2E), 0, 3)

    return pl.pallas_call(
        body, name="proj_rest", grid=(s // tm,),
        in_specs=[pl.BlockSpec((tm, D), lambda i: (i, 0)), _vec(D_REST, D), _vec(1, HD)],
        out_specs=[pl.BlockSpec((tm, D_REST), lambda i: (i, 0)), pl.BlockSpec((NH, tm, HD), lambda i: (0, i, 0))],
        out_shape=[jax.ShapeDtypeStruct((s, D_REST), F32), jax.ShapeDtypeStruct((NH, s, HD), BF16)],
        scratch_shapes=[pltpu.VMEM((tm, HD), F32), pltpu.VMEM((8, HD), F32)],
        compiler_params=_params(dimension_semantics=("arbitrary",)),
    )(xn, w_rest_t, bf_pad)


def _attn_fwd(zq, kx):
    s = zq.shape[0]
    n = s // TQ
    nb = TQ // HD

    def body(q_ref, k_ref, v_ref, kx_ref, o_ref, ax_ref):
        i = pl.program_id(1)
        lane = lax.broadcasted_iota(jnp.int32, (TQ, HD), 1)
        row = lax.broadcasted_iota(jnp.int32, (TQ, HD), 0)
        qa = jnp.concatenate([q_ref[...], jnp.where(lane < 3, 1.0, 0.0).astype(BF16)], axis=1)

        def step(j, carry, masked):
            m, l, acc = carry
            rows = pl.ds(pl.multiple_of(j * TQ, TQ), TQ)
            ka = jnp.concatenate([k_ref[rows, :], kx_ref[0, rows, :]], axis=1)
            v_all = v_ref[rows, :]
            rp = TQ // ROW_PARTS
            parts = [slice(rp * t, rp * (t + 1)) for t in range(ROW_PARTS)]
            keys = [rp * (t + 1) if masked else TQ for t in range(ROW_PARTS)]
            u_parts = [lax.dot_general(qa[part], ka[:kn], NT_DIMS, preferred_element_type=F32)
                       for part, kn in zip(parts, keys)]
            out = []
            for t, (part, u, kn) in enumerate(zip(parts, u_parts, keys)):
                us = [u[:, HD * b:HD * (b + 1)] for b in range(kn // HD)]
                if masked:
                    us = [ub if HD * (b + 1) <= rp * t else jnp.where(row[part] >= lane[part] + HD * b, ub, NEG)
                          for b, ub in enumerate(us)]
                v = v_all[:kn]
                bm = functools.reduce(jnp.maximum, us)
                m_new = jnp.maximum(m[part], jnp.max(bm, axis=1, keepdims=True))
                alpha = jnp.exp2(m[part] - m_new)
                ps = [jnp.exp2(ub - m_new) for ub in us]
                l_new = alpha * l[part] + functools.reduce(jnp.add, ps)
                pr = jnp.concatenate(ps, axis=1).astype(BF16)
                out.append((m_new, l_new, alpha * acc[part] + jnp.dot(pr, v, preferred_element_type=F32)))
            return tuple(jnp.concatenate([o[t] for o in out], axis=0) for t in range(3))

        init = (jnp.full((TQ, HD), NEG, F32), jnp.zeros((TQ, HD), F32), jnp.zeros((TQ, HD), F32))
        carry = lax.fori_loop(0, i, lambda j, cr: step(j, cr, False), init)
        m, l, acc = step(i, carry, True)
        l_row = jnp.sum(l, axis=1, keepdims=True)
        o_ref[...] = acc / l_row
        ax_ref[0] = _bias_lanes(-(m + jnp.log(l_row) * LOG2E), 3, 0)

    return pl.pallas_call(
        body, name="attn_fwd", grid=(NH, n),
        in_specs=[
            pl.BlockSpec((TQ, HD), lambda h, i: (i, h)),
            pl.BlockSpec((s, HD), lambda h, i: (0, NH + h)),
            pl.BlockSpec((s, HD), lambda h, i: (0, 2 * NH + h)),
            pl.BlockSpec((1, s, HD), lambda h, i: (h, 0, 0)),
        ],
        out_specs=[pl.BlockSpec((TQ, HD), lambda h, i: (i, h)), pl.BlockSpec((1, TQ, HD), lambda h, i: (h, i, 0))],
        out_shape=[jax.ShapeDtypeStruct((s, D), F32), jax.ShapeDtypeStruct((NH, s, HD), BF16)],
        compiler_params=_params(dimension_semantics=("parallel", "parallel")),
    )(zq, zq, zq, kx)


def _attn_bwd(zq, do, ax, delta, kx, after):
    s = zq.shape[0]
    n = s // TQ
    nb = TQ // HD

    def body(k_ref, v_ref, kx_ref, q_ref, ax_ref, do_ref, dl_ref, after_ref, dq_out, dk_ref, dv_ref, dcs_ref, drs_ref,
             dq_ref):
        j = pl.program_id(1)

        @pl.when(j == 0)
        def _():
            dq_ref[...] = jnp.zeros_like(dq_ref)
            drs_ref[...] = jnp.zeros_like(drs_ref)

        k = k_ref[...]
        v = v_ref[...]
        ka = jnp.concatenate([k, kx_ref[0]], axis=1)
        row = lax.broadcasted_iota(jnp.int32, (TQ, HD), 0)
        lane = lax.broadcasted_iota(jnp.int32, (TQ, HD), 1)

        def step(i, carry, r0, rn, kn, masked):
            dk, dv, dcs = carry
            rows = pl.ds(pl.multiple_of(i * TQ + r0, rn), rn)
            q = q_ref[rows, :]
            dout = do_ref[rows, :]
            dlv = dl_ref[0, rows, :]
            qa = jnp.concatenate([q, ax_ref[0, rows, :]], axis=1)
            u = lax.dot_general(qa, ka[:kn], NT_DIMS, preferred_element_type=F32)
            dp = lax.dot_general(dout, v[:kn], NT_DIMS, preferred_element_type=F32)
            prs, dss = [], []
            for b in range(kn // HD):
                cs = slice(HD * b, HD * (b + 1))
                ub = u[:, cs]
                if masked and HD * (b + 1) > r0:
                    ub = jnp.where(row[:rn] + r0 >= lane[:rn] + HD * b, ub, NEG)
                pb = jnp.exp2(ub)
                prs.append(pb)
                dss.append(pb * (dp[:, cs] - dlv))
            drs_ref[0, rows, :] += functools.reduce(jnp.add, dss)
            ds = jnp.concatenate(dss, axis=1)
            dsb = ds.astype(BF16)
            dcs_new = jnp.sum(ds.reshape(rn // 8, 8, kn), axis=0)
            dv_new = lax.dot_general(jnp.concatenate(prs, axis=1).astype(BF16), dout, TN_DIMS, preferred_element_type=F32)
            dk_new = lax.dot_general(dsb, q, TN_DIMS, preferred_element_type=F32)
            if kn < TQ:
                dcs_new = jnp.concatenate([dcs_new, jnp.zeros((8, TQ - kn), F32)], axis=1)
                dv_new = jnp.concatenate([dv_new, jnp.zeros((TQ - kn, HD), F32)], axis=0)
                dk_new = jnp.concatenate([dk_new, jnp.zeros((TQ - kn, HD), F32)], axis=0)
            dq_ref[rows, :] += jnp.dot(dsb, k[:kn], preferred_element_type=F32) * SCALE
            return dk + dk_new, dv + dv_new, dcs + dcs_new

        carry = (jnp.zeros((TQ, HD), F32), jnp.zeros((TQ, HD), F32), jnp.zeros((8, TQ), F32))
        rp = TQ // ROW_PARTS
        for t in range(ROW_PARTS):
            carry = step(j, carry, rp * t, rp, rp * (t + 1), True)
        dk, dv, dcs = lax.fori_loop(j + 1, n, lambda i, cr: step(i, cr, 0, TQ, TQ, False), carry)
        dk_ref[...] = (dk * (SCALE / Q_SCALE)).astype(BF16)
        dv_ref[...] = dv.astype(BF16)
        dcs_ref[0] = jnp.broadcast_to(_colsum(dcs), (8, TQ))

        @pl.when(j == n - 1)
        def _():
            dq_out[...] = dq_ref[...].astype(BF16)

    return pl.pallas_call(
        body, name="attn_bwd", grid=(NH, n),
        in_specs=[
            pl.BlockSpec((TQ, HD), lambda h, j: (j, NH + h)),
            pl.BlockSpec((TQ, HD), lambda h, j: (j, 2 * NH + h)),
            pl.BlockSpec((1, TQ, HD), lambda h, j: (h, j, 0)),
            pl.BlockSpec((s, HD), lambda h, j: (0, h)),
            pl.BlockSpec((1, s, HD), lambda h, j: (h, 0, 0)),
            pl.BlockSpec((s, HD), lambda h, j: (0, h)),
            pl.BlockSpec((1, s, HD), lambda h, j: (h, 0, 0)),
            pl.BlockSpec((8, HD), lambda h, j: (0, 0)),
        ],
        out_specs=[
            pl.BlockSpec((s, HD), lambda h, j: (0, h)),
            pl.BlockSpec((TQ, HD), lambda h, j: (j, h)),
            pl.BlockSpec((TQ, HD), lambda h, j: (j, h)),
            pl.BlockSpec((1, 8, TQ), lambda h, j: (j, h, 0)),
            pl.BlockSpec((1, s, HD), lambda h, j: (h, 0, 0)),
        ],
        out_shape=[
            jax.ShapeDtypeStruct((s, D), BF16),
            jax.ShapeDtypeStruct((s, D), BF16),
            jax.ShapeDtypeStruct((s, D), BF16),
            jax.ShapeDtypeStruct((n, 8 * NH, TQ), F32),
            jax.ShapeDtypeStruct((NH, s, HD), F32),
        ],
        scratch_shapes=[pltpu.VMEM((s, HD), F32)],
        compiler_params=_params(dimension_semantics=("parallel", "arbitrary")),
    )(zq, zq, kx, zq, ax, do, delta, after)


def _forget_bwd(dcs, drs, zr, bf_pad):
    n = dcs.shape[0]
    s = n * TQ

    def body(dcs_ref, drs_ref, fl_ref, b_ref, dfl_ref, gb_ref, buf, carry):
        i = pl.program_id(0)

        @pl.when(i == 0)
        def _():
            carry[...] = jnp.zeros_like(carry)
            gb_ref[...] = jnp.zeros_like(gb_ref)

        dc_t = jnp.concatenate([dcs_ref[0], jnp.zeros((HD - 8 * NH, TQ), F32)], axis=0)
        lane = lax.broadcasted_iota(jnp.int32, (TQ, HD), 1)
        dc = -dc_t.T
        for hh in range(NH):
            dc = dc + jnp.where(lane == 8 * hh, jnp.sum(drs_ref[hh], axis=1, keepdims=True), 0.0)
        buf[...] = _cumsum_bwd(dc) + carry[0:1, :]
        carry[0:1, :] = buf[0:1, :]
        fl = fl_ref[...] + b_ref[...]
        dfl = buf[...] * _sigmoid_rel(-fl)
        dfl_ref[...] = dfl.astype(BF16)
        gb_ref[...] += _colsum(dfl)

    return pl.pallas_call(
        body, name="forget_bwd", grid=(n,),
        in_specs=[
            pl.BlockSpec((1, 8 * NH, TQ), lambda i: (n - 1 - i, 0, 0)),
            pl.BlockSpec((NH, TQ, HD), lambda i: (0, n - 1 - i, 0)),
            pl.BlockSpec((TQ, HD), lambda i: (n - 1 - i, FL_COL // HD)),
            _vec(1, HD),
        ],
        out_specs=[pl.BlockSpec((TQ, HD), lambda i: (n - 1 - i, 0)), _vec(1, HD)],
        out_shape=[jax.ShapeDtypeStruct((s, HD), BF16), jax.ShapeDtypeStruct((1, HD), F32)],
        scratch_shapes=[pltpu.VMEM((TQ, HD), F32), pltpu.VMEM((8, HD), F32)],
        compiler_params=_params(dimension_semantics=("arbitrary",)),
    )(dcs, drs, zr, bf_pad)


def _gates(xc, w_ref, b, sigmoid):
    xb = xc.astype(BF16)
    pre = jnp.concatenate(
        [jnp.dot(xb[:, HD * g:HD * (g + 1)], w_ref[g], preferred_element_type=F32) for g in range(NH)], axis=1)
    return sigmoid(pre + b)


def _lru_coeffs(r, lam):
    sp = jnp.maximum(-lam, 0.0) + jnp.log(1.0 + jnp.exp(-jnp.abs(lam)))
    log_a = -LRU_C * r * sp
    a = jnp.exp(log_a)
    y = 2.0 * log_a
    em1 = jnp.where(jnp.abs(y) < 0.01, y * (1.0 + y * (0.5 + y * (1.0 / 6.0))), jnp.exp(y) - 1.0)
    em = -em1
    inv_gam = lax.rsqrt(jnp.maximum(em, 1e-37))
    return sp, a, em * inv_gam, inv_gam


def _conv_taps(ext, t):
    return [_shift_down(ext, CONV_W - 1 - jj, t) for jj in range(CONV_W)]


def _lru_fwd(zr, conv_w8, conv_b, w_r, b_r, w_i, b_i, lam):
    s = zr.shape[0]
    n = s // TS
    xl_col = 1

    def body(xl_ref, halo_ref, cw_ref, cb_ref, wr_ref, br_ref, wi_ref, bi_ref, lam_ref, xc_ref, h_ref, carry):
        i = pl.program_id(0)

        @pl.when(i == 0)
        def _():
            carry[...] = jnp.zeros_like(carry)

        halo = jnp.where(i == 0, 0.0, halo_ref[...])
        taps = _conv_taps(jnp.concatenate([halo, xl_ref[...]], axis=0), TS)
        xc = cb_ref[...] + sum(cw_ref[jj:jj + 1, :] * taps[jj] for jj in range(CONV_W))
        xc_ref[...] = xc
        r = _gates(xc, wr_ref, br_ref[...], _sigmoid_rel)
        ig = _gates(xc, wi_ref, bi_ref[...], _sigmoid)
        _, a, gam, _ = _lru_coeffs(r, lam_ref[...])
        a_cum, h_loc = _scan(a, gam * (ig * xc), False)
        h_ref[...] = h_loc + a_cum * carry[0:1, :]
        carry[0:1, :] = h_ref[TS - 1:TS, :]

    return pl.pallas_call(
        body, name="lru_fwd", grid=(n,),
        in_specs=[
            _row(D, xl_col),
            pl.BlockSpec((8, D), lambda i: (jnp.maximum(i * (TS // 8) - 1, 0), xl_col)),
            _vec(8, D), _vec(1, D),
            pl.BlockSpec((NH, HD, HD), lambda i: (0, 0, 0)), _vec(1, D),
            pl.BlockSpec((NH, HD, HD), lambda i: (0, 0, 0)), _vec(1, D),
            _vec(1, D),
        ],
        out_specs=[_row(D), _row(D)],
        out_shape=[jax.ShapeDtypeStruct((s, D), F32), jax.ShapeDtypeStruct((s, D), F32)],
        scratch_shapes=[pltpu.VMEM((8, D), F32)],
        compiler_params=_params(dimension_semantics=("arbitrary",)),
    )(zr, zr, conv_w8, conv_b, w_r, b_r, w_i, b_i, lam)


def _lru_bwd(zr, xc, h, dh, conv_w8, w_r, b_r, w_i, b_i, lam):
    s = zr.shape[0]
    n = s // TS
    xl_col = 1

    def rev(i):
        return n - 1 - i

    def body(xl_ref, xlh_ref, xc_ref, h_ref, hh_ref, dh_ref, cw_ref, wr_ref, br_ref, wi_ref, bi_ref, lam_ref,
             dxl_ref, gwr_ref, gwi_ref, gbr_ref, gbi_ref, glam_ref, gcb_ref, gcw_ref, l_buf, dxc_buf, carry_g, carry_dxc):
        i = pl.program_id(0)
        first = rev(i) == 0

        @pl.when(i == 0)
        def _():
            carry_g[...] = jnp.zeros_like(carry_g)
            carry_dxc[...] = jnp.zeros_like(carry_dxc)
            for ref in (gwr_ref, gwi_ref, gbr_ref, gbi_ref, glam_ref, gcb_ref, gcw_ref):
                ref[...] = jnp.zeros_like(ref)

        rows = _rows_iota(TS)
        xc = xc_ref[...]
        lam = lam_ref[...]
        r = _gates(xc, wr_ref, br_ref[...], _sigmoid_rel)
        ig = _gates(xc, wi_ref, bi_ref[...], _sigmoid)
        sp, a, gam, inv_gam = _lru_coeffs(r, lam)
        g = dh_ref[...] + jnp.where(rows == TS - 1, carry_g[0:1, :], 0.0)
        b = jnp.where(rows == TS - 1, 0.0, pltpu.roll(a, TS - 1, 0))
        l_buf[...] = _scan(b, g, True)[1]
        lv = l_buf[...]
        carry_g[0:1, :] = l_buf[0:1, :] * a[0:1, :]
        h_prev_row = jnp.where(first, 0.0, hh_ref[7:8, :])
        h_prev = jnp.where(rows == 0, h_prev_row, pltpu.roll(h_ref[...], 1, 0))
        dgam = lv * ig * xc
        dig = lv * gam * xc
        dxc = lv * gam * ig
        dla = lv * h_prev * a - dgam * (a * a) * inv_gam
        dr = dla * (-LRU_C) * sp
        glam_ref[...] += _colsum(dla * r) * (LRU_C * _sigmoid_rel(-lam))
        dpr = dr * r * (1.0 - r)
        dpi = dig * ig * (1.0 - ig)
        gbr_ref[...] += _colsum(dpr)
        gbi_ref[...] += _colsum(dpi)
        xb = xc.astype(BF16)
        dprb = dpr.astype(BF16)
        dpib = dpi.astype(BF16)
        back = []
        for gi in range(NH):
            cs = slice(HD * gi, HD * (gi + 1))
            gwr_ref[gi] += lax.dot_general(xb[:, cs], dprb[:, cs], TN_DIMS, preferred_element_type=F32)
            gwi_ref[gi] += lax.dot_general(xb[:, cs], dpib[:, cs], TN_DIMS, preferred_element_type=F32)
            back.append(lax.dot_general(dprb[:, cs], wr_ref[gi], NT_DIMS, preferred_element_type=F32)
                        + lax.dot_general(dpib[:, cs], wi_ref[gi], NT_DIMS, preferred_element_type=F32))
        dxc = dxc + jnp.concatenate(back, axis=1)
        dxc_buf[...] = dxc
        gcb_ref[...] += _colsum(dxc)
        halo = jnp.where(first, 0.0, xlh_ref[...])
        taps = _conv_taps(jnp.concatenate([halo, xl_ref[...]], axis=0), TS)
        for jj in range(CONV_W):
            gcw_ref[jj:jj + 1, :] += _colsum(dxc * taps[jj])
        ext = jnp.concatenate([dxc, carry_dxc[...]], axis=0)
        dxl = sum(cw_ref[jj:jj + 1, :] * _shift_up(ext, CONV_W - 1 - jj, TS) for jj in range(CONV_W))
        dxl_ref[...] = dxl.astype(BF16)
        carry_dxc[...] = dxc_buf[0:8, :]

    rowr = lambda c, col=0: pl.BlockSpec((TS, c), lambda i: (rev(i), col))
    halo = lambda col: pl.BlockSpec((8, D), lambda i: (jnp.maximum(rev(i) * (TS // 8) - 1, 0), col))
    gate_w = pl.BlockSpec((NH, HD, HD), lambda i: (0, 0, 0))
    return pl.pallas_call(
        body, name="lru_bwd", grid=(n,),
        in_specs=[rowr(D, xl_col), halo(xl_col), rowr(D), rowr(D), halo(0), rowr(D),
                  _vec(8, D), gate_w, _vec(1, D), gate_w, _vec(1, D), _vec(1, D)],
        out_specs=[rowr(D), gate_w, gate_w, _vec(1, D), _vec(1, D), _vec(1, D), _vec(1, D), _vec(8, D)],
        out_shape=[
            jax.ShapeDtypeStruct((s, D), BF16),
            jax.ShapeDtypeStruct((NH, HD, HD), F32), jax.ShapeDtypeStruct((NH, HD, HD), F32),
            jax.ShapeDtypeStruct((1, D), F32), jax.ShapeDtypeStruct((1, D), F32), jax.ShapeDtypeStruct((1, D), F32),
            jax.ShapeDtypeStruct((1, D), F32), jax.ShapeDtypeStruct((8, D), F32),
        ],
        scratch_shapes=[pltpu.VMEM((TS, D), F32), pltpu.VMEM((TS, D), F32), pltpu.VMEM((8, D), F32), pltpu.VMEM((8, D), F32)],
        compiler_params=_params(dimension_semantics=("arbitrary",)),
    )(zr, zr, xc, h, h, dh, conv_w8, w_r, b_r, w_i, b_i, lam)


def _silu_parts(g):
    sg = _sigmoid(g)
    return g * sg, sg * (1.0 + g * (1.0 - sg))


def _branch_out_bwd(o, h, zr, dmix, w_out, gain_a, gain_l):
    s = o.shape[0]

    def body(o_ref, ga_ref, h_ref, gl_ref, dm_ref, w_ref, ka_ref, kl_ref,
             do_ref, dl_ref, dga_ref, dh_ref, dgl_ref, gka_ref, gkl_ref):
        @pl.when(pl.program_id(0) == 0)
        def _():
            gka_ref[...] = jnp.zeros_like(gka_ref)
            gkl_ref[...] = jnp.zeros_like(gkl_ref)

        dycat = lax.dot_general(dm_ref[...], w_ref[...], NT_DIMS, preferred_element_type=F32)

        def one(v, g, dy, gain):
            vhat, rstd = _rms_fwd(v)
            sg, dsg = _silu_parts(g)
            dn = dy * sg
            dg = dy * (vhat * gain) * dsg
            return _rms_bwd(vhat, rstd, dn * gain), dg, _colsum(dn * vhat)

        o = o_ref[...]
        dout, dga, gka = one(o, ga_ref[...], dycat[:, :D], ka_ref[...])
        do_ref[...] = dout.astype(BF16)
        dga_ref[...] = dga.astype(BF16)
        gka_ref[...] += gka
        prod = dout * o
        for hh in range(NH):
            dl_ref[hh] = jnp.broadcast_to(jnp.sum(prod[:, HD * hh:HD * (hh + 1)], axis=1, keepdims=True), (TS, HD))
        dh, dgl, gkl = one(h_ref[...], gl_ref[...], dycat[:, D:], kl_ref[...])
        dh_ref[...] = dh
        dgl_ref[...] = dgl.astype(BF16)
        gkl_ref[...] += gkl

    return pl.pallas_call(
        body, name="branch_out_bwd", grid=(s // TS,),
        in_specs=[_row(D), _row(D, 0), _row(D), _row(D, 2), _row(D), _vec(2 * D, D), _vec(1, D), _vec(1, D)],
        out_specs=[_row(D), pl.BlockSpec((NH, TS, HD), lambda i: (0, i, 0)), _row(D), _row(D), _row(D), _vec(1, D), _vec(1, D)],
        out_shape=[
            jax.ShapeDtypeStruct((s, D), BF16), jax.ShapeDtypeStruct((NH, s, HD), F32), jax.ShapeDtypeStruct((s, D), BF16),
            jax.ShapeDtypeStruct((s, D), F32), jax.ShapeDtypeStruct((s, D), BF16),
            jax.ShapeDtypeStruct((1, D), F32), jax.ShapeDtypeStruct((1, D), F32),
        ],
        compiler_params=_params(dimension_semantics=("arbitrary",)),
    )(o, zr, h, zr, dmix, w_out, gain_a, gain_l)


def _residual(x, o, h, zr, gain_a, gain_l, w_out, post_gain):
    s = x.shape[0]

    def body(x_ref, o_ref, ga_ref, h_ref, gl_ref, ka_ref, kl_ref, w_ref, g_ref, y_ref, m_ref, h1_ref, hb_ref):
        ohat, _ = _rms_fwd(o_ref[...])
        y_ref[:, 0:D] = (ohat * ka_ref[...] * _silu_parts(ga_ref[...])[0]).astype(BF16)
        hhat, _ = _rms_fwd(h_ref[...])
        y_ref[:, D:2 * D] = (hhat * kl_ref[...] * _silu_parts(gl_ref[...])[0]).astype(BF16)
        mix = jnp.dot(y_ref[...], w_ref[...], preferred_element_type=F32)
        m_ref[...] = mix
        mhat, _ = _rms_fwd(mix)
        h1 = x_ref[...] + mhat * g_ref[...]
        h1_ref[...] = h1
        hb_ref[...] = h1.astype(BF16)

    return pl.pallas_call(
        body, name="residual", grid=(s // TS,),
        in_specs=[_row(D), _row(D), _row(D, 0), _row(D), _row(D, 2), _vec(1, D), _vec(1, D), _vec(2 * D, D), _vec(1, D)],
        out_specs=[_row(2 * D), _row(D), _row(D), _row(D)],
        out_shape=[jax.ShapeDtypeStruct((s, 2 * D), BF16), jax.ShapeDtypeStruct((s, D), F32),
                   jax.ShapeDtypeStruct((s, D), F32), jax.ShapeDtypeStruct((s, D), BF16)],
        compiler_params=_params(dimension_semantics=("parallel",)),
    )(x, o, zr, h, zr, gain_a, gain_l, w_out, post_gain)


def _head(h1, p, tgt, mix, w_gate, w_ple, ple_gain, b_gate, post_gain):
    s = h1.shape[0]

    def body(h_ref, p_ref, t_ref, m_ref, wg_ref, wp_ref, kg_ref, b_ref, pg_ref,
             loss_ref, dgp_ref, dpe_ref, dh_ref, dm_ref, gk_ref, gb_ref, gg_ref):
        @pl.when(pl.program_id(0) == 0)
        def _():
            for ref in (loss_ref, gk_ref, gb_ref, gg_ref):
                ref[...] = jnp.zeros_like(ref)

        h1 = h_ref[...]
        pe = jnp.dot(p_ref[...].astype(BF16), wp_ref[...], preferred_element_type=F32)
        gp = jnp.dot(h1.astype(BF16), wg_ref[...], preferred_element_type=F32)
        ehat, rstd = _rms_fwd(pe)
        e = ehat * kg_ref[...]
        gate = _sigmoid(gp + b_ref[...])
        diff = (h1 + gate * e) - t_ref[...]
        per_row = jnp.mean(diff * diff, axis=-1, keepdims=True)
        loss_ref[...] += 0.5 * jnp.sum(per_row, axis=0, keepdims=True)
        dy = diff * (1.0 / D)
        dgp = dy * e * gate * (1.0 - gate)
        dgpb = dgp.astype(BF16)
        dgp_ref[...] = dgpb
        gb_ref[...] += _colsum(dgp)
        de = dy * gate
        gk_ref[...] += _colsum(de * ehat)
        dpe_ref[...] = _rms_bwd(ehat, rstd, de * kg_ref[...]).astype(BF16)
        dh1 = dy + lax.dot_general(dgpb, wg_ref[...], NT_DIMS, preferred_element_type=F32)
        dh_ref[...] = dh1
        mhat, rstd_m = _rms_fwd(m_ref[...])
        gg_ref[...] += _colsum(dh1 * mhat)
        dm_ref[...] = _rms_bwd(mhat, rstd_m, dh1 * pg_ref[...]).astype(BF16)

    return pl.pallas_call(
        body, name="head", grid=(s // TS,),
        in_specs=[_row(D), _row(D_PLE), _row(D), _row(D), _vec(D, D), _vec(D_PLE, D), _vec(1, D), _vec(1, D), _vec(1, D)],
        out_specs=[_vec(1, 1), _row(D), _row(D), _row(D), _row(D), _vec(1, D), _vec(1, D), _vec(1, D)],
        out_shape=[
            jax.ShapeDtypeStruct((1, 1), F32), jax.ShapeDtypeStruct((s, D), BF16), jax.ShapeDtypeStruct((s, D), BF16),
            jax.ShapeDtypeStruct((s, D), F32), jax.ShapeDtypeStruct((s, D), BF16),
            jax.ShapeDtypeStruct((1, D), F32), jax.ShapeDtypeStruct((1, D), F32), jax.ShapeDtypeStruct((1, D), F32),
        ],
        compiler_params=_params(dimension_semantics=("arbitrary",)),
    )(h1, p, tgt, mix, w_gate, w_ple, ple_gain, b_gate, post_gain)


def _prenorm_bwd(x, dxn_a, dz_rest, w_rest_t, dh1, pre_gain, after):
    s = x.shape[0]
    ks = [a.shape[1] for a in dz_rest]
    assert sum(ks) == w_rest_t.shape[0]
    nz = len(dz_rest)
    tm = 512
    rowm = lambda c: pl.BlockSpec((tm, c), lambda i: (i, 0))

    def body(*refs):
        x_ref, da_ref = refs[:2]
        w_ref, dh_ref, g_ref, after_ref, dx_ref, gg_ref = refs[2 + nz:]

        @pl.when(pl.program_id(0) == 0)
        def _():
            gg_ref[...] = jnp.zeros_like(gg_ref)

        dxn, k0 = da_ref[...], 0
        for dz_ref, kw in zip(refs[2:2 + nz], ks):
            dxn = dxn + jnp.dot(dz_ref[...], w_ref[k0:k0 + kw, :], preferred_element_type=F32)
            k0 += kw
        xhat, rstd = _rms_fwd(x_ref[...])
        gg_ref[...] += _colsum(dxn * xhat)
        dx_ref[...] = dh_ref[...] + _rms_bwd(xhat, rstd, dxn * g_ref[...])

    return pl.pallas_call(
        body, name="prenorm_bwd", grid=(s // tm,),
        in_specs=[rowm(D), rowm(D)] + [rowm(kw) for kw in ks] + [_vec(*w_rest_t.shape), rowm(D), _vec(1, D), _vec(8, HD)],
        out_specs=[rowm(D), _vec(1, D)],
        out_shape=[jax.ShapeDtypeStruct((s, D), F32), jax.ShapeDtypeStruct((1, D), F32)],
        compiler_params=_params(dimension_semantics=("arbitrary",)),
    )(x, dxn_a, *dz_rest, w_rest_t, dh1, pre_gain, after)


def _adamw(name, parts, w, m, v, own=None, me=None):
    r, c = w.shape
    if r % 8 == 0:
        tr = _pick(r, (256, 128, 16, 8))
        grid = (r // tr,)
        blk = pl.BlockSpec((tr, c), lambda i: (i, 0))
        parts_blk = pl.BlockSpec((N_DEV, tr, c), lambda i: (0, i, 0))
    else:
        tc = _pick(c, (256, 128))
        grid = (c // tc,)
        blk = pl.BlockSpec((r, tc), lambda i: (0, i))
        parts_blk = pl.BlockSpec((N_DEV, r, tc), lambda i: (0, 0, i))

    def body(*refs):
        p_ref, w_ref, m_ref, v_ref = refs[:4]
        g_ref, d_ref, nm_ref, nv_ref = refs[-4:]
        if own is None:
            g = p_ref[0].astype(F32)
            for j in range(1, N_DEV):
                g = g + p_ref[j].astype(F32)
            g_ref[...] = g
        else:
            own_ref, me_ref = refs[4:6]
            g_ref[...] = jnp.zeros_like(g_ref)
            for j in range(N_DEV):
                @pl.when(me_ref[0] == j)
                def _():
                    g_ref[...] += own_ref[...].astype(F32)

                @pl.when(me_ref[0] != j)
                def _():
                    g_ref[...] += p_ref[j].astype(F32)
            g = g_ref[...]
        nm = ADAM_B1 * m_ref[...] + (1.0 - ADAM_B1) * g
        nv = ADAM_B2 * v_ref[...] + (1.0 - ADAM_B2) * (g * g)
        nm_ref[...] = nm
        nv_ref[...] = nv
        m_hat = nm / (1.0 - ADAM_B1 ** ADAM_STEP)
        v_hat = nv / (1.0 - ADAM_B2 ** ADAM_STEP)
        d_ref[...] = -ADAM_LR * (m_hat / (jnp.sqrt(v_hat) + ADAM_EPS) + ADAM_WD * w_ref[...])

    in_specs, args = [parts_blk, blk, blk, blk], [parts, w, m, v]
    if own is not None:
        in_specs += [blk, pl.BlockSpec(memory_space=pltpu.SMEM)]
        args += [own, me]
    return pl.pallas_call(
        body, name=name, grid=grid,
        in_specs=in_specs,
        out_specs=[blk] * 4,
        out_shape=[jax.ShapeDtypeStruct((r, c), F32)] * 4,
        compiler_params=_params(dimension_semantics=("parallel",)),
    )(*args)


def _spread8(v):
    r = v.shape[0]
    return jnp.pad(jnp.pad(v[:, :, None], ((0, 0), (0, 0), (0, 7))).reshape(r, 8 * NH), ((0, 0), (0, HD - 8 * NH)))


def _gather8(v):
    return v[:, :8 * NH].reshape(v.shape[0], NH, 8)[:, :, 0]


def _cols_to_shards(g):
    r, c8 = g.shape
    return g.reshape(r, N_DEV, c8 // N_DEV).transpose(1, 0, 2)


def _shards_to_cols(g):
    n, r, c = g.shape
    return g.transpose(1, 0, 2).reshape(r, n * c)


def kernel(x, p, w_in, b_f, pre_gain, post_gain, conv_w, conv_b, w_rgate, b_rgate, w_igate, b_igate, lru_lambda, attn_out_gain, lru_out_gain, w_out, w_ple, ple_gain, w_ple_gate, b_ple_gate, loss_target, m_w_in, m_b_f, m_pre_gain, m_post_gain, m_conv_w, m_conv_b, m_w_rgate, m_b_rgate, m_w_igate, m_b_igate, m_lru_lambda, m_attn_out_gain, m_lru_out_gain, m_w_out, m_w_ple, m_ple_gain, m_w_ple_gate, m_b_ple_gate, v_w_in, v_b_f, v_pre_gain, v_post_gain, v_conv_w, v_conv_b, v_w_rgate, v_b_rgate, v_w_igate, v_b_igate, v_lru_lambda, v_attn_out_gain, v_lru_out_gain, v_w_out, v_w_ple, v_ple_gain, v_w_ple_gate, v_b_ple_gate):
    me = 4 * lax.axis_index("x") + 2 * lax.axis_index("y") + lax.axis_index("c")
    x2, p2, tgt = x[0], p[0, 0], loss_target[0]

    conv_w_shard8 = jnp.pad(conv_w[0], ((0, 8 - CONV_W), (0, 0)))
    wt, m_wt, v_wt = w_in[0].T, m_w_in[0].T, v_w_in[0].T
    g_wint, g_conv = _gather_two_level("gather_w_in", [wt.astype(BF16), conv_w_shard8])
    win_t = g_wint.reshape(D_IN, D)
    rest_state, rest_token = _exchange_start(
        "gather_rest_start", [w_out[0].astype(BF16), w_ple[0].astype(BF16), w_ple_gate[0].astype(BF16)], ["bcast"] * 3,
        after=g_conv)
    w_rest_t = jnp.concatenate([win_t[D_QKV + NH:], _spread8(win_t[D_QKV:D_QKV + NH].T).T], axis=0)
    conv_w8 = _shards_to_cols(g_conv)
    bf_pad = _spread8(b_f)
    w_r, w_i = w_rgate[0].astype(BF16), w_igate[0].astype(BF16)

    xn, zq = _prenorm_proj(x2, pre_gain, win_t, D_QKV, rest_token)
    zr, kx = _proj_rest(xn, w_rest_t, bf_pad)
    o, ax = _attn_fwd(zq, kx)
    xc, h = _lru_fwd(zr, conv_w8, conv_b, w_r, b_rgate, w_i, b_igate, lru_lambda)
    g_wout, g_wple, g_wpg = _exchange_wait("gather_rest_wait", rest_state, h)
    wout_full = g_wout.reshape(2 * D, D)
    wple_full = _shards_to_cols(g_wple)
    wpg_full = g_wpg.reshape(D, D)
    ycat, mix, h1, h1b = _residual(x2, o, h, zr, attn_out_gain, lru_out_gain, wout_full, post_gain)

    loss_part, dgp, dpe, dh1, dmix, g_ple_gain, g_b_gate, g_post_gain = _head(
        h1, p2, tgt, mix, wpg_full, wple_full, ple_gain, b_ple_gate, post_gain)
    gw_pg = _mm("bwd_gate_w", h1b, dgp, "tn", BF16)
    gw_ple = _mm("bwd_ple_w", p2, dpe, "tn", BF16)
    gw_out = _mm("bwd_out_w", ycat, dmix, "tn", BF16)
    do, delta, dga, dh, dgl, g_aog, g_log = _branch_out_bwd(o, h, zr, dmix, wout_full, attn_out_gain, lru_out_gain)
    dxl, g_wr, g_wi, g_br, g_bi, g_lam, g_cb, g_cw8 = _lru_bwd(
        zr, xc, h, dh, conv_w8, w_r, b_rgate, w_i, b_igate, lru_lambda)
    gates = jnp.concatenate([g_wr.reshape(D, HD), g_wi.reshape(D, HD)], axis=0).astype(BF16)
    outw_state, outw_token = _exchange_start(
        "exchange_outw_start",
        [gw_out.reshape(N_DEV, 2 * D // N_DEV, D), _cols_to_shards(gw_ple), gw_pg.reshape(N_DEV, D // N_DEV, D), gates],
        ["scatter"] * 3 + ["bcast"])
    dq, dk, dv, dcs, drs = _attn_bwd(zq, do, ax, delta, kx, outw_token)
    tn = lambda nm, dz: _mm("bwd_w_" + nm, dz, xn, "tn", BF16)
    gw_in_t = jnp.concatenate([tn("q", dq), tn("k", dk), tn("v", dv), jnp.zeros((NH, D), BF16),
                               tn("ga", dga), tn("xl", dxl), tn("gl", dgl)], axis=0)
    inw_state, inw_token = _exchange_start(
        "exchange_inw_start", [gw_in_t.reshape(N_DEV, D_IN_SHARD, D)], ["scatter"])
    dfl, g_bf_pad = _forget_bwd(dcs, drs, zr, bf_pad + inw_token[0:1, :])
    gw_fl = _gather8(_mm("bwd_w_fl", dfl, xn, "tn", F32).T).T
    dxn_a = _mm_cat("bwd_qkv_x", [dq, dk, dv], win_t, F32, after=inw_token)
    grad_x, g_pre_gain = _prenorm_bwd(x2, dxn_a, [dga, dxl, dgl, dfl], w_rest_t, dh1, pre_gain, inw_token)

    upd = {}
    me1 = me.reshape(1).astype(jnp.int32)
    r_wout, r_wple, r_wpg, r_gates = _exchange_wait("exchange_outw_wait", outw_state, grad_x, fill_own=False)
    upd["w_out"] = _adamw("adamw_w_out", r_wout[0], w_out[0], m_w_out[0], v_w_out[0], r_wout[1], me1)
    upd["w_ple"] = _adamw("adamw_w_ple", r_wple[0], w_ple[0], m_w_ple[0], v_w_ple[0], r_wple[1], me1)
    upd["w_ple_gate"] = _adamw("adamw_w_ple_gate", r_wpg[0], w_ple_gate[0], m_w_ple_gate[0], v_w_ple_gate[0], r_wpg[1], me1)
    gates_of = lambda a, b: jnp.concatenate([a[0].reshape(D, HD), b[0].reshape(D, HD)], axis=0)
    g_gates = _adamw("adamw_gates", r_gates[0], gates_of(w_rgate, w_igate), gates_of(m_w_rgate, m_w_igate),
                     gates_of(v_w_rgate, v_w_igate), r_gates[1], me1)
    upd["w_rgate"] = [a[:D].reshape(1, NH, HD, HD) for a in g_gates]
    upd["w_igate"] = [a[D:].reshape(1, NH, HD, HD) for a in g_gates]
    behind = upd["w_out"][0][0:1] + upd["w_ple_gate"][0][0:1] + jnp.pad(g_gates[0][0:1], ((0, 0), (0, D - HD)))
    small = jnp.concatenate(
        [jnp.pad(_gather8(g_bf_pad), ((0, 0), (0, D - NH))), g_pre_gain, g_post_gain, g_cb, g_br, g_bi, g_lam, g_aog, g_log,
         g_ple_gain, g_b_gate, g_cw8[:CONV_W], behind, jnp.pad(loss_part, ((0, 7), (0, D - 1))), gw_fl], axis=0)
    (r_small,) = _exchange("exchange_small", [small], ["bcast"], before=(wt, m_wt, v_wt))
    vec_names = ["b_f", "pre_gain", "post_gain", "conv_b", "b_rgate", "b_igate", "lru_lambda", "attn_out_gain",
                 "lru_out_gain", "ple_gain", "b_ple_gate"]
    vec_w = dict(b_f=(b_f, m_b_f, v_b_f), pre_gain=(pre_gain, m_pre_gain, v_pre_gain),
                 post_gain=(post_gain, m_post_gain, v_post_gain), conv_b=(conv_b, m_conv_b, v_conv_b),
                 b_rgate=(b_rgate, m_b_rgate, v_b_rgate), b_igate=(b_igate, m_b_igate, v_b_igate),
                 lru_lambda=(lru_lambda, m_lru_lambda, v_lru_lambda),
                 attn_out_gain=(attn_out_gain, m_attn_out_gain, v_attn_out_gain),
                 lru_out_gain=(lru_out_gain, m_lru_out_gain, v_lru_out_gain), ple_gain=(ple_gain, m_ple_gain, v_ple_gain),
                 b_ple_gate=(b_ple_gate, m_b_ple_gate, v_b_ple_gate))
    conv_mine = lambda a: lax.dynamic_slice_in_dim(a, me * HD, HD, axis=1)

    def small_rows(k):
        rows = [jnp.pad(vec_w[nm][k], ((0, 0), (0, D - vec_w[nm][k].shape[1]))) for nm in vec_names]
        cw = (conv_w, m_conv_w, v_conv_w)[k][0]
        full = lax.dynamic_update_slice_in_dim(jnp.ones((CONV_W, D), F32), cw, me * HD, axis=1)
        return jnp.concatenate(rows + [full, jnp.ones((17, D), F32)], axis=0)

    g_small = _adamw("adamw_small", r_small, small_rows(0), small_rows(1), small_rows(2))
    loss = g_small[0][16, 0]
    for idx, nm in enumerate(vec_names):
        width = vec_w[nm][0].shape[1]
        upd[nm] = [a[idx:idx + 1, :width] for a in g_small]
    base = len(vec_names)
    upd["conv_w"] = [conv_mine(a[base:base + CONV_W])[None] for a in g_small]
    (r_win,) = _exchange_wait("exchange_inw_wait", inw_state, g_small[0], fill_own=False)
    fl_sum = g_small[0][24:24 + NH]
    dev_a, row_a = divmod(D_QKV, D_IN_SHARD)
    n_a = D_IN_SHARD - row_a
    assert 2 * n_a == NH, "the forget rows are taken to straddle two row blocks evenly"
    own_win = r_win[1].astype(F32)
    at = jnp.where(me == dev_a, row_a, 0)
    rows_new = jnp.where(me == dev_a, fl_sum[:n_a], fl_sum[n_a:])
    rows_new = jnp.where((me == dev_a) | (me == dev_a + 1), rows_new, lax.dynamic_slice_in_dim(own_win, at, n_a, axis=0))
    own_win = lax.dynamic_update_slice_in_dim(own_win, rows_new, at, axis=0)
    upd["w_in"] = [a.T for a in _adamw("adamw_w_in", r_win[0], wt, m_wt, v_wt, own_win, me1)]
    for nm in ("w_in", "w_out", "w_ple", "w_ple_gate"):
        upd[nm] = [a[None] for a in upd[nm]]

    order = ["w_in", "b_f", "pre_gain", "post_gain", "conv_w", "conv_b", "w_rgate", "b_rgate", "w_igate", "b_igate",
             "lru_lambda", "attn_out_gain", "lru_out_gain", "w_out", "w_ple", "ple_gain", "w_ple_gate", "b_ple_gate"]
    outs = [loss, grad_x[None]]
    for k in range(4):
        outs += [upd[nm][k] for nm in order]
    return tuple(outs)
```

```python
import functools

import jax
import jax.numpy as jnp
from jax import lax
from jax.experimental import pallas as pl
from jax.experimental.pallas import tpu as pltpu

F32 = jnp.float32
BF16 = jnp.bfloat16

N_DEV = 8
D = 1024
HD = 128
NH = 8
D_IN = 6152
D_IN_SHARD = D_IN // N_DEV
D_QKV = 3 * D
D_REST = 3 * D + HD
FL_COL = 3 * D
D_PLE = 256
CONV_W = 4
LRU_C = 8.0
RMS_EPS = 1e-6
SCALE = HD ** -0.5
LOG2E = 1.4426950408889634
Q_SCALE = SCALE * LOG2E
NEG = -1e30

ADAM_LR = 0.001
ADAM_B1 = 0.9
ADAM_B2 = 0.999
ADAM_EPS = 1e-08
ADAM_WD = 0.01
ADAM_STEP = 10

TS = 256
TQ = 1024
ROW_PARTS = 2
FWD_HEADS = 2
FWD_ROW_PARTS = 2
VMEM_LIMIT = 48 * 1024 * 1024

NT_DIMS = (((1,), (1,)), ((), ()))
TN_DIMS = (((0,), (0,)), ((), ()))


def _params(**kw):
    return pltpu.CompilerParams(vmem_limit_bytes=VMEM_LIMIT, **kw)


def _sigmoid(v):
    return 0.5 * jnp.tanh(0.5 * v) + 0.5


def _sigmoid_rel(v):
    return 1.0 / (1.0 + jnp.exp(-v))


def _rms_fwd(v):
    rstd = lax.rsqrt(jnp.mean(v * v, axis=-1, keepdims=True) + RMS_EPS)
    return v * rstd, rstd


def _rms_bwd(vhat, rstd, dvhat):
    return rstd * (dvhat - vhat * jnp.mean(dvhat * vhat, axis=-1, keepdims=True))


def _colsum(v):
    return jnp.sum(v, axis=0, keepdims=True)


def _rows_iota(t):
    return lax.broadcasted_iota(jnp.int32, (t, 1), 0)


def _scan(a, u, reverse):
    t, c = a.shape
    rows = _rows_iota(t)
    d = 1
    while d < t:
        if d < 8:
            valid = rows < t - d if reverse else rows >= d
            shift = t - d if reverse else d
            u = jnp.where(valid, u + a * pltpu.roll(u, shift, 0), u)
            a = jnp.where(valid, a * pltpu.roll(a, shift, 0), a)
        else:
            zeros, ones = jnp.zeros((d, c), F32), jnp.ones((d, c), F32)
            if reverse:
                u_far, a_far = jnp.concatenate([u[d:], zeros], axis=0), jnp.concatenate([a[d:], ones], axis=0)
            else:
                u_far, a_far = jnp.concatenate([zeros, u[:t - d]], axis=0), jnp.concatenate([ones, a[:t - d]], axis=0)
            u = u + a * u_far
            a = a * a_far
        d *= 2
    return a, u


def _cumsum_fwd(v):
    t = v.shape[0]
    rows = _rows_iota(t)
    d = 1
    while d < t:
        v = jnp.where(rows >= d, v + pltpu.roll(v, d, 0), v)
        d *= 2
    return v


def _cumsum_bwd(v):
    t = v.shape[0]
    rows = _rows_iota(t)
    d = 1
    while d < t:
        v = jnp.where(rows < t - d, v + pltpu.roll(v, t - d, 0), v)
        d *= 2
    return v


def _bias_lanes(v, at, ones_at):
    lane = lax.broadcasted_iota(jnp.int32, v.shape, 1)
    hi = v.astype(BF16).astype(F32)
    mid = (v - hi).astype(BF16).astype(F32)
    lo = ((v - hi) - mid).astype(BF16).astype(F32)
    out = jnp.where((lane >= ones_at) & (lane < ones_at + 3), 1.0, 0.0)
    for k, piece in enumerate((hi, mid, lo)):
        out = jnp.where(lane == at + k, piece, out)
    return out.astype(BF16)


def _shift_down(ext, k, t):
    return pltpu.roll(ext, k, 0)[8:, :] if k else ext[8:, :]


def _shift_up(ext, k, t):
    return pltpu.roll(ext, t + 8 - k, 0)[:t, :] if k else ext[:t, :]


def _exchange(name, arrs, kinds, before=()):
    n = len(arrs)
    nb = len(before)
    out_shape = []
    for a, kind in zip(arrs, kinds):
        shp = a.shape if kind == "scatter" else (N_DEV,) + a.shape
        out_shape.append(jax.ShapeDtypeStruct(shp, a.dtype))

    def body(*refs):
        ins, outs = refs[:n], refs[n + nb:2 * n + nb]
        send_sems, recv_sems, local_sems = refs[2 * n + nb:]
        x, y, c = lax.axis_index("x"), lax.axis_index("y"), lax.axis_index("c")
        me = 4 * x + 2 * y + c
        copies = []
        for i in range(n):
            scatter = kinds[i] == "scatter"
            mine = pltpu.make_async_copy(ins[i].at[me] if scatter else ins[i], outs[i].at[me], local_sems.at[i])
            mine.start()
            copies.append(mine)
            for m in range(1, N_DEV):
                px = 1 - x if m & 4 else x
                py = 1 - y if m & 2 else y
                pc = 1 - c if m & 1 else c
                peer = 4 * px + 2 * py + pc
                cp = pltpu.make_async_remote_copy(
                    src_ref=ins[i].at[peer] if scatter else ins[i],
                    dst_ref=outs[i].at[me],
                    send_sem=send_sems.at[i, m - 1],
                    recv_sem=recv_sems.at[i, m - 1],
                    device_id=(px, py, pc),
                    device_id_type=pl.DeviceIdType.MESH,
                )
                cp.start()
                copies.append(cp)
        for cp in copies:
            cp.wait()

    any_spec = pl.BlockSpec(memory_space=pl.ANY)
    return pl.pallas_call(
        body,
        name=name,
        out_shape=out_shape,
        in_specs=[any_spec] * (n + nb),
        out_specs=[any_spec] * n,
        scratch_shapes=[
            pltpu.SemaphoreType.DMA((n, N_DEV - 1)),
            pltpu.SemaphoreType.DMA((n, N_DEV - 1)),
            pltpu.SemaphoreType.DMA((n,)),
        ],
        compiler_params=pltpu.CompilerParams(has_side_effects=True),
    )(*arrs, *before)


def _gather_two_level(name, arrs, pieces=1):
    n = len(arrs)
    items = []
    for i, a in enumerate(arrs):
        rows = a.shape[0]
        if pieces > 1 and rows >= 512:
            step = -(-rows // (16 * pieces)) * 16
            items += [(i, r0, min(step, rows - r0)) for r0 in range(0, rows, step)]
        else:
            items.append((i, 0, rows))
    n_items = len(items)

    def body(*refs):
        ins, outs = refs[:n], refs[n:2 * n]
        send_sems, recv_sems, local_sems = refs[2 * n:]
        x, y, c = lax.axis_index("x"), lax.axis_index("y"), lax.axis_index("c")
        me, sibling = (x, y, c), (x, y, 1 - c)
        chips = [(1 - x, y), (x, 1 - y), (1 - x, 1 - y)]

        def rows_of(ref, t):
            i, r0, rn = items[t]
            return ref if rn == arrs[i].shape[0] else ref.at[pl.ds(r0, rn)]

        def slot(t, dev):
            return rows_of(outs[items[t][0]].at[4 * dev[0] + 2 * dev[1] + dev[2]], t)

        def copy(t, k, block, to, from_input=False):
            return pltpu.make_async_remote_copy(
                src_ref=rows_of(ins[items[t][0]], t) if from_input else slot(t, block), dst_ref=slot(t, block),
                send_sem=send_sems.at[t, k], recv_sem=recv_sems.at[t, k],
                device_id=to, device_id_type=pl.DeviceIdType.MESH)

        own, sent = [], []
        for t in range(n_items):
            mine = pltpu.make_async_copy(rows_of(ins[items[t][0]], t), slot(t, me), local_sems.at[t])
            mine.start()
            own.append(mine)
            first = [copy(t, 1 + j, me, (*chip, c), from_input=True) for j, chip in enumerate(chips)]
            first.append(copy(t, 0, me, sibling, from_input=True))
            for cp in first:
                cp.start()
            sent += first
        for t in range(n_items):
            for j, chip in enumerate(chips):
                copy(t, 1 + j, (*chip, c), me).wait_recv()
                fwd = copy(t, 4 + j, (*chip, c), sibling)
                fwd.start()
                sent.append(fwd)
        for t in range(n_items):
            copy(t, 0, sibling, me).wait_recv()
            for j, chip in enumerate(chips):
                copy(t, 4 + j, (*chip, 1 - c), me).wait_recv()
        for cp in sent:
            cp.wait_send()
        for cp in own:
            cp.wait()

    any_spec = pl.BlockSpec(memory_space=pl.ANY)
    return pl.pallas_call(
        body, name=name,
        out_shape=[jax.ShapeDtypeStruct((N_DEV,) + a.shape, a.dtype) for a in arrs],
        in_specs=[any_spec] * n, out_specs=[any_spec] * n,
        scratch_shapes=[pltpu.SemaphoreType.DMA((n_items, 7)), pltpu.SemaphoreType.DMA((n_items, 7)),
                        pltpu.SemaphoreType.DMA((n_items,))],
        compiler_params=pltpu.CompilerParams(has_side_effects=True),
    )(*arrs)


def _peers(x, y, c):
    out = []
    for m in range(1, N_DEV):
        px = 1 - x if m & 4 else x
        py = 1 - y if m & 2 else y
        pc = 1 - c if m & 1 else c
        out.append((m, (px, py, pc), 4 * px + 2 * py + pc))
    return out


def _split_copies(kinds, src_refs, land_refs, send_sems, recv_sems):
    x, y, c = lax.axis_index("x"), lax.axis_index("y"), lax.axis_index("c")
    me = 4 * x + 2 * y + c
    copies = []
    for i, kind in enumerate(kinds):
        for m, peer, pidx in _peers(x, y, c):
            copies.append(pltpu.make_async_remote_copy(
                src_ref=src_refs[i].at[pidx] if kind == "scatter" else src_refs[i],
                dst_ref=land_refs[i].at[me],
                send_sem=send_sems.at[i * (N_DEV - 1) + m - 1],
                recv_sem=recv_sems.at[i * (N_DEV - 1) + m - 1],
                device_id=peer,
                device_id_type=pl.DeviceIdType.MESH,
            ))
    return copies


_HBM_SPEC = pl.BlockSpec(memory_space=pltpu.HBM)
_SEM_SPEC = pl.BlockSpec(memory_space=pltpu.SEMAPHORE)
_DATAFLOW = pltpu.SideEffectType.DATAFLOW_SIDE_EFFECTING


def _exchange_start(name, arrs, kinds, after=None):
    n = len(arrs)
    extra = [] if after is None else [after]
    lands = []
    for a, kind in zip(arrs, kinds):
        shp = a.shape if kind == "scatter" else (N_DEV,) + a.shape
        lands.append(lax.empty(shp, a.dtype))

    def body(*refs):
        src_refs, land_refs = refs[:n], refs[n:2 * n]
        send_sems, recv_sems = refs[2 * n + len(extra):2 * n + len(extra) + 2]
        token = refs[-1]
        for cp in _split_copies(kinds, src_refs, land_refs, send_sems, recv_sems):
            cp.start()
        token[...] = jnp.zeros_like(token)

    n_sem = n * (N_DEV - 1)
    hbm = lambda a: pltpu.HBM(a.shape, a.dtype)
    res = pl.pallas_call(
        body, name=name,
        out_shape=(pltpu.SemaphoreType.DMA((n_sem,)), pltpu.SemaphoreType.DMA((n_sem,)),
                   *[hbm(a) for a in arrs], *[hbm(a) for a in lands], jax.ShapeDtypeStruct((8, HD), F32)),
        in_specs=[_HBM_SPEC] * (2 * n) + [pl.BlockSpec(memory_space=pl.ANY)] * len(extra),
        out_specs=(_SEM_SPEC, _SEM_SPEC, *[_HBM_SPEC] * (2 * n), pl.BlockSpec(memory_space=pltpu.VMEM)),
        input_output_aliases={i: 2 + i for i in range(2 * n)},
        compiler_params=pltpu.CompilerParams(has_side_effects=_DATAFLOW),
    )(*[pltpu.with_memory_space_constraint(a, pltpu.HBM) for a in arrs],
      *[pltpu.with_memory_space_constraint(a, pltpu.HBM) for a in lands], *extra)
    return (kinds, res[0], res[1], res[2:2 + n], res[2 + n:2 + 2 * n]), res[-1]


def _exchange_wait(name, state, after, fill_own=True):
    kinds, send_sems, recv_sems, srcs, lands = state
    n = len(srcs)

    def body(*refs):
        src_refs, land_refs = refs[:n], refs[n:2 * n]
        send_sems_ref, recv_sems_ref = refs[2 * n:2 * n + 2]
        for cp in _split_copies(kinds, src_refs, land_refs, send_sems_ref, recv_sems_ref):
            cp.wait_send()
            cp.wait_recv()

    res = pl.pallas_call(
        body, name=name,
        out_shape=tuple(pltpu.HBM(a.shape, a.dtype) for a in (*srcs, *lands)),
        in_specs=[_HBM_SPEC] * (2 * n) + [_SEM_SPEC, _SEM_SPEC, pl.BlockSpec(memory_space=pl.ANY)],
        out_specs=tuple([_HBM_SPEC] * (2 * n)),
        input_output_aliases={i: i for i in range(2 * n)},
        compiler_params=pltpu.CompilerParams(has_side_effects=_DATAFLOW),
    )(*srcs, *lands, send_sems, recv_sems, after)
    me = 4 * lax.axis_index("x") + 2 * lax.axis_index("y") + lax.axis_index("c")
    outs = []
    for kind, src, land in zip(kinds, res[:n], res[n:]):
        own = lax.dynamic_index_in_dim(src, me, 0, keepdims=False) if kind == "scatter" else src
        outs.append(lax.dynamic_update_index_in_dim(land, own, me, 0) if fill_own else (land, own))
    return outs


def _pick(n, cands):
    for t in cands:
        if n % t == 0:
            return t
    raise ValueError(f"no tile for {n}")


def _mm(name, a, b, mode, out_dtype, after=None):
    if mode == "nn":
        (m, k), (k2, n) = a.shape, b.shape
    elif mode == "nt":
        (m, k), (n, k2) = a.shape, b.shape
    else:
        (k, m), (k2, n) = a.shape, b.shape
    assert k == k2, (name, a.shape, b.shape)
    if mode == "tn":
        tm = _pick(m, (1024, 640, 512, 256, 128))
        tn = _pick(n, (1024, 640, 512, 256, 128))
        tk = _pick(k, (2048, 1024, 512, 256))
    else:
        tm, tn, tk = _pick(m, (512, 256)), n, k
    nk = k // tk

    def body(a_ref, b_ref, *rest):
        o_ref = rest[-2] if nk > 1 else rest[-1]
        av = a_ref[...].astype(BF16)
        bv = b_ref[...].astype(BF16)
        if mode == "nn":
            part = jnp.dot(av, bv, preferred_element_type=F32)
        elif mode == "nt":
            part = lax.dot_general(av, bv, NT_DIMS, preferred_element_type=F32)
        else:
            part = lax.dot_general(av, bv, TN_DIMS, preferred_element_type=F32)
        if nk == 1:
            o_ref[...] = part.astype(out_dtype)
            return
        acc_ref = rest[-1]
        kk = pl.program_id(2)

        @pl.when(kk == 0)
        def _():
            acc_ref[...] = part

        @pl.when(kk > 0)
        def _():
            acc_ref[...] += part

        @pl.when(kk == nk - 1)
        def _():
            o_ref[...] = acc_ref[...].astype(out_dtype)

    if mode == "tn":
        a_spec = pl.BlockSpec((tk, tm), lambda j, i, kk: (kk, i))
    else:
        a_spec = pl.BlockSpec((tm, tk), lambda j, i, kk: (i, kk))
    if mode == "nt":
        b_spec = pl.BlockSpec((tn, tk), lambda j, i, kk: (j, kk))
    else:
        b_spec = pl.BlockSpec((tk, tn), lambda j, i, kk: (kk, j))
    in_specs, args = [a_spec, b_spec], [a, b]
    if after is not None:
        in_specs.append(pl.BlockSpec((8, HD), lambda j, i, kk: (0, 0)))
        args.append(after)
    return pl.pallas_call(
        body,
        name=name,
        grid=(n // tn, m // tm, nk),
        in_specs=in_specs,
        out_specs=pl.BlockSpec((tm, tn), lambda j, i, kk: (i, j)),
        out_shape=jax.ShapeDtypeStruct((m, n), out_dtype),
        scratch_shapes=[pltpu.VMEM((tm, tn), F32)] if nk > 1 else [],
        compiler_params=_params(dimension_semantics=("parallel", "parallel", "arbitrary")),
    )(*args)


def _mm_cat(name, a_list, b, out_dtype, after=None):
    m = a_list[0].shape[0]
    ks = [a.shape[1] for a in a_list]
    n = b.shape[1]
    assert sum(ks) <= b.shape[0], (name, ks, b.shape)
    tm = _pick(m, (512, 256))
    na = len(a_list)

    def body(*refs):
        b_ref, o_ref = refs[na], refs[-1]
        k0, acc = 0, None
        for a_ref, kw in zip(refs[:na], ks):
            part = jnp.dot(a_ref[...].astype(BF16), b_ref[k0:k0 + kw, :], preferred_element_type=F32)
            acc = part if acc is None else acc + part
            k0 += kw
        o_ref[...] = acc.astype(out_dtype)

    in_specs = [pl.BlockSpec((tm, kw), lambda i: (i, 0)) for kw in ks] + [pl.BlockSpec((sum(ks), n), lambda i: (0, 0))]
    args = [*a_list, b]
    if after is not None:
        in_specs.append(pl.BlockSpec((8, HD), lambda i: (0, 0)))
        args.append(after)
    return pl.pallas_call(
        body, name=name, grid=(m // tm,),
        in_specs=in_specs, out_specs=pl.BlockSpec((tm, n), lambda i: (i, 0)),
        out_shape=jax.ShapeDtypeStruct((m, n), out_dtype),
        compiler_params=_params(dimension_semantics=("parallel",)),
    )(*args)


def _row(c, col=0):
    return pl.BlockSpec((TS, c), lambda i: (i, col))


def _vec(r, c):
    return pl.BlockSpec((r, c), lambda i: (0, 0))


def _prenorm_proj(x, pre_gain, w_t, n, after):
    s = x.shape[0]
    tm = 512

    def body(x_ref, g_ref, w_ref, after_ref, xn_ref, z_ref):
        xhat, _ = _rms_fwd(x_ref[...])
        xn = (xhat * g_ref[...]).astype(BF16)
        xn_ref[...] = xn
        z = lax.dot_general(xn, w_ref[...], NT_DIMS, preferred_element_type=F32)
        z_ref[:, :D] = (z[:, :D] * Q_SCALE).astype(BF16)
        z_ref[:, D:] = z[:, D:].astype(BF16)

    return pl.pallas_call(
        body, name="prenorm_proj_qkv", grid=(s // tm,),
        in_specs=[pl.BlockSpec((tm, D), lambda i: (i, 0)), _vec(1, D), _vec(n, D), _vec(8, HD)],
        out_specs=[pl.BlockSpec((tm, D), lambda i: (i, 0)), pl.BlockSpec((tm, n), lambda i: (i, 0))],
        out_shape=[jax.ShapeDtypeStruct((s, D), BF16), jax.ShapeDtypeStruct((s, n), BF16)],
        compiler_params=_params(dimension_semantics=("parallel",)),
    )(x, pre_gain, w_t, after)


def _proj_rest(xn, w_rest_t, bf_pad):
    s = xn.shape[0]
    tm = 512

    def body(x_ref, w_ref, b_ref, z_ref, kx_ref, c_buf, carry):
        @pl.when(pl.program_id(0) == 0)
        def _():
            carry[...] = jnp.zeros_like(carry)

        z = lax.dot_general(x_ref[...], w_ref[...], NT_DIMS, preferred_element_type=F32)
        z_ref[...] = z
        fl = z[:, FL_COL:] + b_ref[...]
        ls = jnp.minimum(fl, 0.0) - jnp.log(1.0 + jnp.exp(-jnp.abs(fl)))
        c_buf[...] = _cumsum_fwd(ls) + carry[0:1, :]
        carry[0:1, :] = c_buf[tm - 1:tm, :]
        cv = c_buf[...]
        for h in range(NH):
            kx_ref[h] = _bias_lanes(jnp.broadcast_to(cv[:, 8 * h:8 * h + 1], (tm, HD)) * (-LOG2E), 0, 3)

    return pl.pallas_call(
        body, name="proj_rest", grid=(s // tm,),
        in_specs=[pl.BlockSpec((tm, D), lambda i: (i, 0)), _vec(D_REST, D), _vec(1, HD)],
        out_specs=[pl.BlockSpec((tm, D_REST), lambda i: (i, 0)), pl.BlockSpec((NH, tm, HD), lambda i: (0, i, 0))],
        out_shape=[jax.ShapeDtypeStruct((s, D_REST), F32), jax.ShapeDtypeStruct((NH, s, HD), BF16)],
        scratch_shapes=[pltpu.VMEM((tm, HD), F32), pltpu.VMEM((8, HD), F32)],
        compiler_params=_params(dimension_semantics=("arbitrary",)),
    )(xn, w_rest_t, bf_pad)


def _attn_fwd(zq, kx):
    s = zq.shape[0]
    n = s // TQ
    nb = TQ // HD

    hp = FWD_HEADS

    def body(q_ref, k_ref, v_ref, kx_ref, o_ref, ax_ref):
        i = pl.program_id(1)
        lane = lax.broadcasted_iota(jnp.int32, (TQ, HD), 1)
        row = lax.broadcasted_iota(jnp.int32, (TQ, HD), 0)
        ones = jnp.where(lane < 3, 1.0, 0.0).astype(BF16)
        qas = [jnp.concatenate([q_ref[:, HD * hh:HD * (hh + 1)], ones], axis=1) for hh in range(hp)]
        rp = TQ // FWD_ROW_PARTS

        def step(j, carry, masked):
            m, l, acc = carry
            rows = pl.ds(pl.multiple_of(j * TQ, TQ), TQ)
            kas = [jnp.concatenate([k_ref[rows, HD * hh:HD * (hh + 1)], kx_ref[hh, rows, :]], axis=1) for hh in range(hp)]
            vs = [v_ref[rows, HD * hh:HD * (hh + 1)] for hh in range(hp)]
            units = [(hh, t) for hh in range(hp) for t in range(FWD_ROW_PARTS)]
            keys = [rp * (t + 1) if masked else TQ for _, t in units]
            u_parts = [lax.dot_general(qas[hh][rp * t:rp * (t + 1)], kas[hh][:kn], NT_DIMS, preferred_element_type=F32)
                       for (hh, t), kn in zip(units, keys)]
            out = []
            for (hh, t), u, kn in zip(units, u_parts, keys):
                local = slice(rp * t, rp * (t + 1))
                part = slice(hh * TQ + rp * t, hh * TQ + rp * (t + 1))
                us = [u[:, HD * b:HD * (b + 1)] for b in range(kn // HD)]
                if masked:
                    us = [ub if HD * (b + 1) <= rp * t else jnp.where(row[local] >= lane[local] + HD * b, ub, NEG)
                          for b, ub in enumerate(us)]
                bm = functools.reduce(jnp.maximum, us)
                m_new = jnp.maximum(m[part], jnp.max(bm, axis=1, keepdims=True))
                alpha = jnp.exp2(m[part] - m_new)
                ps = [jnp.exp2(ub - m_new) for ub in us]
                l_new = alpha * l[part] + functools.reduce(jnp.add, ps)
                pr = jnp.concatenate(ps, axis=1).astype(BF16)
                out.append((m_new, l_new, alpha * acc[part] + jnp.dot(pr, vs[hh][:kn], preferred_element_type=F32)))
            return tuple(jnp.concatenate([o[t] for o in out], axis=0) for t in range(3))

        init = (jnp.full((hp * TQ, HD), NEG, F32), jnp.zeros((hp * TQ, HD), F32), jnp.zeros((hp * TQ, HD), F32))
        carry = lax.fori_loop(0, i, lambda j, cr: step(j, cr, False), init)
        m, l, acc = step(i, carry, True)
        l_row = jnp.sum(l, axis=1, keepdims=True)
        o_all = acc / l_row
        ax_all = _bias_lanes(-(m + jnp.log(l_row) * LOG2E), 3, 0)
        for hh in range(hp):
            o_ref[:, HD * hh:HD * (hh + 1)] = o_all[hh * TQ:(hh + 1) * TQ]
            ax_ref[hh] = ax_all[hh * TQ:(hh + 1) * TQ]

    return pl.pallas_call(
        body, name="attn_fwd", grid=(NH // hp, n),
        in_specs=[
            pl.BlockSpec((TQ, hp * HD), lambda h, i: (i, h)),
            pl.BlockSpec((s, hp * HD), lambda h, i: (0, NH // hp + h)),
            pl.BlockSpec((s, hp * HD), lambda h, i: (0, 2 * (NH // hp) + h)),
            pl.BlockSpec((hp, s, HD), lambda h, i: (h, 0, 0)),
        ],
        out_specs=[pl.BlockSpec((TQ, hp * HD), lambda h, i: (i, h)), pl.BlockSpec((hp, TQ, HD), lambda h, i: (h, i, 0))],
        out_shape=[jax.ShapeDtypeStruct((s, D), F32), jax.ShapeDtypeStruct((NH, s, HD), BF16)],
        compiler_params=_params(dimension_semantics=("parallel", "parallel")),
    )(zq, zq, zq, kx)


def _attn_bwd(zq, do, ax, delta, kx, after):
    s = zq.shape[0]
    n = s // TQ
    nb = TQ // HD

    def body(k_ref, v_ref, kx_ref, q_ref, ax_ref, do_ref, dl_ref, after_ref, dq_out, dk_ref, dv_ref, dcs_ref, drs_ref,
             dq_ref):
        j = pl.program_id(1)

        @pl.when(j == 0)
        def _():
            dq_ref[...] = jnp.zeros_like(dq_ref)
            drs_ref[...] = jnp.zeros_like(drs_ref)

        k = k_ref[...]
        v = v_ref[...]
        ka = jnp.concatenate([k, kx_ref[0]], axis=1)
        row = lax.broadcasted_iota(jnp.int32, (TQ, HD), 0)
        lane = lax.broadcasted_iota(jnp.int32, (TQ, HD), 1)

        def step(i, carry, r0, rn, kn, masked):
            dk, dv, dcs = carry
            rows = pl.ds(pl.multiple_of(i * TQ + r0, rn), rn)
            q = q_ref[rows, :]
            dout = do_ref[rows, :]
            dlv = dl_ref[0, rows, :]
            qa = jnp.concatenate([q, ax_ref[0, rows, :]], axis=1)
            u = lax.dot_general(qa, ka[:kn], NT_DIMS, preferred_element_type=F32)
            dp = lax.dot_general(dout, v[:kn], NT_DIMS, preferred_element_type=F32)
            prs, dss = [], []
            for b in range(kn // HD):
                cs = slice(HD * b, HD * (b + 1))
                ub = u[:, cs]
                if masked and HD * (b + 1) > r0:
                    ub = jnp.where(row[:rn] + r0 >= lane[:rn] + HD * b, ub, NEG)
                pb = jnp.exp2(ub)
                prs.append(pb)
                dss.append(pb * (dp[:, cs] - dlv))
            drs_ref[0, rows, :] += functools.reduce(jnp.add, dss)
            ds = jnp.concatenate(dss, axis=1)
            dsb = ds.astype(BF16)
            dcs_new = jnp.sum(ds.reshape(rn // 8, 8, kn), axis=0)
            dv_new = lax.dot_general(jnp.concatenate(prs, axis=1).astype(BF16), dout, TN_DIMS, preferred_element_type=F32)
            dk_new = lax.dot_general(dsb, q, TN_DIMS, preferred_element_type=F32)
            if kn < TQ:
                dcs_new = jnp.concatenate([dcs_new, jnp.zeros((8, TQ - kn), F32)], axis=1)
                dv_new = jnp.concatenate([dv_new, jnp.zeros((TQ - kn, HD), F32)], axis=0)
                dk_new = jnp.concatenate([dk_new, jnp.zeros((TQ - kn, HD), F32)], axis=0)
            dq_ref[rows, :] += jnp.dot(dsb, k[:kn], preferred_element_type=F32) * SCALE
            return dk + dk_new, dv + dv_new, dcs + dcs_new

        carry = (jnp.zeros((TQ, HD), F32), jnp.zeros((TQ, HD), F32), jnp.zeros((8, TQ), F32))
        rp = TQ // ROW_PARTS
        for t in range(ROW_PARTS):
            carry = step(j, carry, rp * t, rp, rp * (t + 1), True)
        dk, dv, dcs = lax.fori_loop(j + 1, n, lambda i, cr: step(i, cr, 0, TQ, TQ, False), carry)
        dk_ref[...] = (dk * (SCALE / Q_SCALE)).astype(BF16)
        dv_ref[...] = dv.astype(BF16)
        dcs_ref[0] = jnp.broadcast_to(_colsum(dcs), (8, TQ))

        @pl.when(j == n - 1)
        def _():
            dq_out[...] = dq_ref[...].astype(BF16)

    return pl.pallas_call(
        body, name="attn_bwd", grid=(NH, n),
        in_specs=[
            pl.BlockSpec((TQ, HD), lambda h, j: (j, NH + h)),
            pl.BlockSpec((TQ, HD), lambda h, j: (j, 2 * NH + h)),
            pl.BlockSpec((1, TQ, HD), lambda h, j: (h, j, 0)),
            pl.BlockSpec((s, HD), lambda h, j: (0, h)),
            pl.BlockSpec((1, s, HD), lambda h, j: (h, 0, 0)),
            pl.BlockSpec((s, HD), lambda h, j: (0, h)),
            pl.BlockSpec((1, s, HD), lambda h, j: (h, 0, 0)),
            pl.BlockSpec((8, HD), lambda h, j: (0, 0)),
        ],
        out_specs=[
            pl.BlockSpec((s, HD), lambda h, j: (0, h)),
            pl.BlockSpec((TQ, HD), lambda h, j: (j, h)),
            pl.BlockSpec((TQ, HD), lambda h, j: (j, h)),
            pl.BlockSpec((1, 8, TQ), lambda h, j: (j, h, 0)),
            pl.BlockSpec((1, s, HD), lambda h, j: (h, 0, 0)),
        ],
        out_shape=[
            jax.ShapeDtypeStruct((s, D), BF16),
            jax.ShapeDtypeStruct((s, D), BF16),
            jax.ShapeDtypeStruct((s, D), BF16),
            jax.ShapeDtypeStruct((n, 8 * NH, TQ), F32),
            jax.ShapeDtypeStruct((NH, s, HD), F32),
        ],
        scratch_shapes=[pltpu.VMEM((s, HD), F32)],
        compiler_params=_params(dimension_semantics=("parallel", "arbitrary")),
    )(zq, zq, kx, zq, ax, do, delta, after)


def _forget_bwd(dcs, drs, zr, bf_pad):
    n = dcs.shape[0]
    s = n * TQ

    def body(dcs_ref, drs_ref, fl_ref, b_ref, dfl_ref, gb_ref, buf, carry):
        i = pl.program_id(0)

        @pl.when(i == 0)
        def _():
            carry[...] = jnp.zeros_like(carry)
            gb_ref[...] = jnp.zeros_like(gb_ref)

        dc_t = jnp.concatenate([dcs_ref[0], jnp.zeros((HD - 8 * NH, TQ), F32)], axis=0)
        lane = lax.broadcasted_iota(jnp.int32, (TQ, HD), 1)
        dc = -dc_t.T
        for hh in range(NH):
            dc = dc + jnp.where(lane == 8 * hh, jnp.sum(drs_ref[hh], axis=1, keepdims=True), 0.0)
        buf[...] = _cumsum_bwd(dc) + carry[0:1, :]
        carry[0:1, :] = buf[0:1, :]
        fl = fl_ref[...] + b_ref[...]
        dfl = buf[...] * _sigmoid_rel(-fl)
        dfl_ref[...] = dfl.astype(BF16)
        gb_ref[...] += _colsum(dfl)

    return pl.pallas_call(
        body, name="forget_bwd", grid=(n,),
        in_specs=[
            pl.BlockSpec((1, 8 * NH, TQ), lambda i: (n - 1 - i, 0, 0)),
            pl.BlockSpec((NH, TQ, HD), lambda i: (0, n - 1 - i, 0)),
            pl.BlockSpec((TQ, HD), lambda i: (n - 1 - i, FL_COL // HD)),
            _vec(1, HD),
        ],
        out_specs=[pl.BlockSpec((TQ, HD), lambda i: (n - 1 - i, 0)), _vec(1, HD)],
        out_shape=[jax.ShapeDtypeStruct((s, HD), BF16), jax.ShapeDtypeStruct((1, HD), F32)],
        scratch_shapes=[pltpu.VMEM((TQ, HD), F32), pltpu.VMEM((8, HD), F32)],
        compiler_params=_params(dimension_semantics=("arbitrary",)),
    )(dcs, drs, zr, bf_pad)


def _gates(xc, w_ref, b, sigmoid):
    xb = xc.astype(BF16)
    pre = jnp.concatenate(
        [jnp.dot(xb[:, HD * g:HD * (g + 1)], w_ref[g], preferred_element_type=F32) for g in range(NH)], axis=1)
    return sigmoid(pre + b)


def _lru_coeffs(r, lam):
    sp = jnp.maximum(-lam, 0.0) + jnp.log(1.0 + jnp.exp(-jnp.abs(lam)))
    log_a = -LRU_C * r * sp
    a = jnp.exp(log_a)
    y = 2.0 * log_a
    em1 = jnp.where(jnp.abs(y) < 0.01, y * (1.0 + y * (0.5 + y * (1.0 / 6.0))), jnp.exp(y) - 1.0)
    em = -em1
    inv_gam = lax.rsqrt(jnp.maximum(em, 1e-37))
    return sp, a, em * inv_gam, inv_gam


def _conv_taps(ext, t):
    return [_shift_down(ext, CONV_W - 1 - jj, t) for jj in range(CONV_W)]


def _lru_fwd(zr, conv_w8, conv_b, w_r, b_r, w_i, b_i, lam):
    s = zr.shape[0]
    n = s // TS
    xl_col = 1

    def body(xl_ref, halo_ref, cw_ref, cb_ref, wr_ref, br_ref, wi_ref, bi_ref, lam_ref, xc_ref, h_ref, carry):
        i = pl.program_id(0)

        @pl.when(i == 0)
        def _():
            carry[...] = jnp.zeros_like(carry)

        halo = jnp.where(i == 0, 0.0, halo_ref[...])
        taps = _conv_taps(jnp.concatenate([halo, xl_ref[...]], axis=0), TS)
        xc = cb_ref[...] + sum(cw_ref[jj:jj + 1, :] * taps[jj] for jj in range(CONV_W))
        xc_ref[...] = xc
        r = _gates(xc, wr_ref, br_ref[...], _sigmoid_rel)
        ig = _gates(xc, wi_ref, bi_ref[...], _sigmoid)
        _, a, gam, _ = _lru_coeffs(r, lam_ref[...])
        a_cum, h_loc = _scan(a, gam * (ig * xc), False)
        h_ref[...] = h_loc + a_cum * carry[0:1, :]
        carry[0:1, :] = h_ref[TS - 1:TS, :]

    return pl.pallas_call(
        body, name="lru_fwd", grid=(n,),
        in_specs=[
            _row(D, xl_col),
            pl.BlockSpec((8, D), lambda i: (jnp.maximum(i * (TS // 8) - 1, 0), xl_col)),
            _vec(8, D), _vec(1, D),
            pl.BlockSpec((NH, HD, HD), lambda i: (0, 0, 0)), _vec(1, D),
            pl.BlockSpec((NH, HD, HD), lambda i: (0, 0, 0)), _vec(1, D),
            _vec(1, D),
        ],
        out_specs=[_row(D), _row(D)],
        out_shape=[jax.ShapeDtypeStruct((s, D), F32), jax.ShapeDtypeStruct((s, D), F32)],
        scratch_shapes=[pltpu.VMEM((8, D), F32)],
        compiler_params=_params(dimension_semantics=("arbitrary",)),
    )(zr, zr, conv_w8, conv_b, w_r, b_r, w_i, b_i, lam)


def _lru_bwd(zr, xc, h, dh, conv_w8, w_r, b_r, w_i, b_i, lam):
    s = zr.shape[0]
    n = s // TS
    xl_col = 1

    def rev(i):
        return n - 1 - i

    def body(xl_ref, xlh_ref, xc_ref, h_ref, hh_ref, dh_ref, cw_ref, wr_ref, br_ref, wi_ref, bi_ref, lam_ref,
             dxl_ref, gwr_ref, gwi_ref, gbr_ref, gbi_ref, glam_ref, gcb_ref, gcw_ref, l_buf, dxc_buf, carry_g, carry_dxc):
        i = pl.program_id(0)
        first = rev(i) == 0

        @pl.when(i == 0)
        def _():
            carry_g[...] = jnp.zeros_like(carry_g)
            carry_dxc[...] = jnp.zeros_like(carry_dxc)
            for ref in (gwr_ref, gwi_ref, gbr_ref, gbi_ref, glam_ref, gcb_ref, gcw_ref):
                ref[...] = jnp.zeros_like(ref)

        rows = _rows_iota(TS)
        xc = xc_ref[...]
        lam = lam_ref[...]
        r = _gates(xc, wr_ref, br_ref[...], _sigmoid_rel)
        ig = _gates(xc, wi_ref, bi_ref[...], _sigmoid)
        sp, a, gam, inv_gam = _lru_coeffs(r, lam)
        g = dh_ref[...] + jnp.where(rows == TS - 1, carry_g[0:1, :], 0.0)
        b = jnp.where(rows == TS - 1, 0.0, pltpu.roll(a, TS - 1, 0))
        l_buf[...] = _scan(b, g, True)[1]
        lv = l_buf[...]
        carry_g[0:1, :] = l_buf[0:1, :] * a[0:1, :]
        h_prev_row = jnp.where(first, 0.0, hh_ref[7:8, :])
        h_prev = jnp.where(rows == 0, h_prev_row, pltpu.roll(h_ref[...], 1, 0))
        dgam = lv * ig * xc
        dig = lv * gam * xc
        dxc = lv * gam * ig
        dla = lv * h_prev * a - dgam * (a * a) * inv_gam
        dr = dla * (-LRU_C) * sp
        glam_ref[...] += _colsum(dla * r) * (LRU_C * _sigmoid_rel(-lam))
        dpr = dr * r * (1.0 - r)
        dpi = dig * ig * (1.0 - ig)
        gbr_ref[...] += _colsum(dpr)
        gbi_ref[...] += _colsum(dpi)
        xb = xc.astype(BF16)
        dprb = dpr.astype(BF16)
        dpib = dpi.astype(BF16)
        back = []
        for gi in range(NH):
            cs = slice(HD * gi, HD * (gi + 1))
            gwr_ref[gi] += lax.dot_general(xb[:, cs], dprb[:, cs], TN_DIMS, preferred_element_type=F32)
            gwi_ref[gi] += lax.dot_general(xb[:, cs], dpib[:, cs], TN_DIMS, preferred_element_type=F32)
            back.append(lax.dot_general(dprb[:, cs], wr_ref[gi], NT_DIMS, preferred_element_type=F32)
                        + lax.dot_general(dpib[:, cs], wi_ref[gi], NT_DIMS, preferred_element_type=F32))
        dxc = dxc + jnp.concatenate(back, axis=1)
        dxc_buf[...] = dxc
        gcb_ref[...] += _colsum(dxc)
        halo = jnp.where(first, 0.0, xlh_ref[...])
        taps = _conv_taps(jnp.concatenate([halo, xl_ref[...]], axis=0), TS)
        for jj in range(CONV_W):
            gcw_ref[jj:jj + 1, :] += _colsum(dxc * taps[jj])
        ext = jnp.concatenate([dxc, carry_dxc[...]], axis=0)
        dxl = sum(cw_ref[jj:jj + 1, :] * _shift_up(ext, CONV_W - 1 - jj, TS) for jj in range(CONV_W))
        dxl_ref[...] = dxl.astype(BF16)
        carry_dxc[...] = dxc_buf[0:8, :]

    rowr = lambda c, col=0: pl.BlockSpec((TS, c), lambda i: (rev(i), col))
    halo = lambda col: pl.BlockSpec((8, D), lambda i: (jnp.maximum(rev(i) * (TS // 8) - 1, 0), col))
    gate_w = pl.BlockSpec((NH, HD, HD), lambda i: (0, 0, 0))
    return pl.pallas_call(
        body, name="lru_bwd", grid=(n,),
        in_specs=[rowr(D, xl_col), halo(xl_col), rowr(D), rowr(D), halo(0), rowr(D),
                  _vec(8, D), gate_w, _vec(1, D), gate_w, _vec(1, D), _vec(1, D)],
        out_specs=[rowr(D), gate_w, gate_w, _vec(1, D), _vec(1, D), _vec(1, D), _vec(1, D), _vec(8, D)],
        out_shape=[
            jax.ShapeDtypeStruct((s, D), BF16),
            jax.ShapeDtypeStruct((NH, HD, HD), F32), jax.ShapeDtypeStruct((NH, HD, HD), F32),
            jax.ShapeDtypeStruct((1, D), F32), jax.ShapeDtypeStruct((1, D), F32), jax.ShapeDtypeStruct((1, D), F32),
            jax.ShapeDtypeStruct((1, D), F32), jax.ShapeDtypeStruct((8, D), F32),
        ],
        scratch_shapes=[pltpu.VMEM((TS, D), F32), pltpu.VMEM((TS, D), F32), pltpu.VMEM((8, D), F32), pltpu.VMEM((8, D), F32)],
        compiler_params=_params(dimension_semantics=("arbitrary",)),
    )(zr, zr, xc, h, h, dh, conv_w8, w_r, b_r, w_i, b_i, lam)


def _silu_parts(g):
    sg = _sigmoid(g)
    return g * sg, sg * (1.0 + g * (1.0 - sg))


def _branch_out_bwd(o, h, zr, dmix, w_out, gain_a, gain_l):
    s = o.shape[0]

    def body(o_ref, ga_ref, h_ref, gl_ref, dm_ref, w_ref, ka_ref, kl_ref,
             do_ref, dl_ref, dga_ref, dh_ref, dgl_ref, gka_ref, gkl_ref):
        @pl.when(pl.program_id(0) == 0)
        def _():
            gka_ref[...] = jnp.zeros_like(gka_ref)
            gkl_ref[...] = jnp.zeros_like(gkl_ref)

        dycat = lax.dot_general(dm_ref[...], w_ref[...], NT_DIMS, preferred_element_type=F32)

        def one(v, g, dy, gain):
            vhat, rstd = _rms_fwd(v)
            sg, dsg = _silu_parts(g)
            dn = dy * sg
            dg = dy * (vhat * gain) * dsg
            return _rms_bwd(vhat, rstd, dn * gain), dg, _colsum(dn * vhat)

        o = o_ref[...]
        dout, dga, gka = one(o, ga_ref[...], dycat[:, :D], ka_ref[...])
        do_ref[...] = dout.astype(BF16)
        dga_ref[...] = dga.astype(BF16)
        gka_ref[...] += gka
        prod = dout * o
        for hh in range(NH):
            dl_ref[hh] = jnp.broadcast_to(jnp.sum(prod[:, HD * hh:HD * (hh + 1)], axis=1, keepdims=True), (TS, HD))
        dh, dgl, gkl = one(h_ref[...], gl_ref[...], dycat[:, D:], kl_ref[...])
        dh_ref[...] = dh
        dgl_ref[...] = dgl.astype(BF16)
        gkl_ref[...] += gkl

    return pl.pallas_call(
        body, name="branch_out_bwd", grid=(s // TS,),
        in_specs=[_row(D), _row(D, 0), _row(D), _row(D, 2), _row(D), _vec(2 * D, D), _vec(1, D), _vec(1, D)],
        out_specs=[_row(D), pl.BlockSpec((NH, TS, HD), lambda i: (0, i, 0)), _row(D), _row(D), _row(D), _vec(1, D), _vec(1, D)],
        out_shape=[
            jax.ShapeDtypeStruct((s, D), BF16), jax.ShapeDtypeStruct((NH, s, HD), F32), jax.ShapeDtypeStruct((s, D), BF16),
            jax.ShapeDtypeStruct((s, D), F32), jax.ShapeDtypeStruct((s, D), BF16),
            jax.ShapeDtypeStruct((1, D), F32), jax.ShapeDtypeStruct((1, D), F32),
        ],
        compiler_params=_params(dimension_semantics=("arbitrary",)),
    )(o, zr, h, zr, dmix, w_out, gain_a, gain_l)


def _residual(x, o, h, zr, gain_a, gain_l, w_out, post_gain):
    s = x.shape[0]

    def body(x_ref, o_ref, ga_ref, h_ref, gl_ref, ka_ref, kl_ref, w_ref, g_ref, y_ref, m_ref, h1_ref, hb_ref):
        ohat, _ = _rms_fwd(o_ref[...])
        y_ref[:, 0:D] = (ohat * ka_ref[...] * _silu_parts(ga_ref[...])[0]).astype(BF16)
        hhat, _ = _rms_fwd(h_ref[...])
        y_ref[:, D:2 * D] = (hhat * kl_ref[...] * _silu_parts(gl_ref[...])[0]).astype(BF16)
        mix = jnp.dot(y_ref[...], w_ref[...], preferred_element_type=F32)
        m_ref[...] = mix
        mhat, _ = _rms_fwd(mix)
        h1 = x_ref[...] + mhat * g_ref[...]
        h1_ref[...] = h1
        hb_ref[...] = h1.astype(BF16)

    return pl.pallas_call(
        body, name="residual", grid=(s // TS,),
        in_specs=[_row(D), _row(D), _row(D, 0), _row(D), _row(D, 2), _vec(1, D), _vec(1, D), _vec(2 * D, D), _vec(1, D)],
        out_specs=[_row(2 * D), _row(D), _row(D), _row(D)],
        out_shape=[jax.ShapeDtypeStruct((s, 2 * D), BF16), jax.ShapeDtypeStruct((s, D), F32),
                   jax.ShapeDtypeStruct((s, D), F32), jax.ShapeDtypeStruct((s, D), BF16)],
        compiler_params=_params(dimension_semantics=("parallel",)),
    )(x, o, zr, h, zr, gain_a, gain_l, w_out, post_gain)


def _head(h1, p, tgt, mix, w_gate, w_ple, ple_gain, b_gate, post_gain):
    s = h1.shape[0]

    def body(h_ref, p_ref, t_ref, m_ref, wg_ref, wp_ref, kg_ref, b_ref, pg_ref,
             loss_ref, dgp_ref, dpe_ref, dh_ref, dm_ref, gk_ref, gb_ref, gg_ref):
        @pl.when(pl.program_id(0) == 0)
        def _():
            for ref in (loss_ref, gk_ref, gb_ref, gg_ref):
                ref[...] = jnp.zeros_like(ref)

        h1 = h_ref[...]
        pe = jnp.dot(p_ref[...].astype(BF16), wp_ref[...], preferred_element_type=F32)
        gp = jnp.dot(h1.astype(BF16), wg_ref[...], preferred_element_type=F32)
        ehat, rstd = _rms_fwd(pe)
        e = ehat * kg_ref[...]
        gate = _sigmoid(gp + b_ref[...])
        diff = (h1 + gate * e) - t_ref[...]
        per_row = jnp.mean(diff * diff, axis=-1, keepdims=True)
        loss_ref[...] += 0.5 * jnp.sum(per_row, axis=0, keepdims=True)
        dy = diff * (1.0 / D)
        dgp = dy * e * gate * (1.0 - gate)
        dgpb = dgp.astype(BF16)
        dgp_ref[...] = dgpb
        gb_ref[...] += _colsum(dgp)
        de = dy * gate
        gk_ref[...] += _colsum(de * ehat)
        dpe_ref[...] = _rms_bwd(ehat, rstd, de * kg_ref[...]).astype(BF16)
        dh1 = dy + lax.dot_general(dgpb, wg_ref[...], NT_DIMS, preferred_element_type=F32)
        dh_ref[...] = dh1
        mhat, rstd_m = _rms_fwd(m_ref[...])
        gg_ref[...] += _colsum(dh1 * mhat)
        dm_ref[...] = _rms_bwd(mhat, rstd_m, dh1 * pg_ref[...]).astype(BF16)

    return pl.pallas_call(
        body, name="head", grid=(s // TS,),
        in_specs=[_row(D), _row(D_PLE), _row(D), _row(D), _vec(D, D), _vec(D_PLE, D), _vec(1, D), _vec(1, D), _vec(1, D)],
        out_specs=[_vec(1, 1), _row(D), _row(D), _row(D), _row(D), _vec(1, D), _vec(1, D), _vec(1, D)],
        out_shape=[
            jax.ShapeDtypeStruct((1, 1), F32), jax.ShapeDtypeStruct((s, D), BF16), jax.ShapeDtypeStruct((s, D), BF16),
            jax.ShapeDtypeStruct((s, D), F32), jax.ShapeDtypeStruct((s, D), BF16),
            jax.ShapeDtypeStruct((1, D), F32), jax.ShapeDtypeStruct((1, D), F32), jax.ShapeDtypeStruct((1, D), F32),
        ],
        compiler_params=_params(dimension_semantics=("arbitrary",)),
    )(h1, p, tgt, mix, w_gate, w_ple, ple_gain, b_gate, post_gain)


def _prenorm_bwd(x, dxn_a, dz_rest, w_rest_t, dh1, pre_gain, after):
    s = x.shape[0]
    ks = [a.shape[1] for a in dz_rest]
    assert sum(ks) == w_rest_t.shape[0]
    nz = len(dz_rest)
    tm = 512
    rowm = lambda c: pl.BlockSpec((tm, c), lambda i: (i, 0))

    def body(*refs):
        x_ref, da_ref = refs[:2]
        w_ref, dh_ref, g_ref, after_ref, dx_ref, gg_ref = refs[2 + nz:]

        @pl.when(pl.program_id(0) == 0)
        def _():
            gg_ref[...] = jnp.zeros_like(gg_ref)

        dxn, k0 = da_ref[...], 0
        for dz_ref, kw in zip(refs[2:2 + nz], ks):
            dxn = dxn + jnp.dot(dz_ref[...], w_ref[k0:k0 + kw, :], preferred_element_type=F32)
            k0 += kw
        xhat, rstd = _rms_fwd(x_ref[...])
        gg_ref[...] += _colsum(dxn * xhat)
        dx_ref[...] = dh_ref[...] + _rms_bwd(xhat, rstd, dxn * g_ref[...])

    return pl.pallas_call(
        body, name="prenorm_bwd", grid=(s // tm,),
        in_specs=[rowm(D), rowm(D)] + [rowm(kw) for kw in ks] + [_vec(*w_rest_t.shape), rowm(D), _vec(1, D), _vec(8, HD)],
        out_specs=[rowm(D), _vec(1, D)],
        out_shape=[jax.ShapeDtypeStruct((s, D), F32), jax.ShapeDtypeStruct((1, D), F32)],
        compiler_params=_params(dimension_semantics=("arbitrary",)),
    )(x, dxn_a, *dz_rest, w_rest_t, dh1, pre_gain, after)


def _adamw(name, parts, w, m, v, own=None, me=None):
    r, c = w.shape
    if r % 8 == 0:
        tr = _pick(r, (256, 128, 16, 8))
        grid = (r // tr,)
        blk = pl.BlockSpec((tr, c), lambda i: (i, 0))
        parts_blk = pl.BlockSpec((N_DEV, tr, c), lambda i: (0, i, 0))
    else:
        tc = _pick(c, (256, 128))
        grid = (c // tc,)
        blk = pl.BlockSpec((r, tc), lambda i: (0, i))
        parts_blk = pl.BlockSpec((N_DEV, r, tc), lambda i: (0, 0, i))

    def body(*refs):
        p_ref, w_ref, m_ref, v_ref = refs[:4]
        g_ref, d_ref, nm_ref, nv_ref = refs[-4:]
        if own is None:
            g = p_ref[0].astype(F32)
            for j in range(1, N_DEV):
                g = g + p_ref[j].astype(F32)
            g_ref[...] = g
        else:
            own_ref, me_ref = refs[4:6]
            g_ref[...] = jnp.zeros_like(g_ref)
            for j in range(N_DEV):
                @pl.when(me_ref[0] == j)
                def _():
                    g_ref[...] += own_ref[...].astype(F32)

                @pl.when(me_ref[0] != j)
                def _():
                    g_ref[...] += p_ref[j].astype(F32)
            g = g_ref[...]
        nm = ADAM_B1 * m_ref[...] + (1.0 - ADAM_B1) * g
        nv = ADAM_B2 * v_ref[...] + (1.0 - ADAM_B2) * (g * g)
        nm_ref[...] = nm
        nv_ref[...] = nv
        m_hat = nm / (1.0 - ADAM_B1 ** ADAM_STEP)
        v_hat = nv / (1.0 - ADAM_B2 ** ADAM_STEP)
        d_ref[...] = -ADAM_LR * (m_hat / (jnp.sqrt(v_hat) + ADAM_EPS) + ADAM_WD * w_ref[...])

    in_specs, args = [parts_blk, blk, blk, blk], [parts, w, m, v]
    if own is not None:
        in_specs += [blk, pl.BlockSpec(memory_space=pltpu.SMEM)]
        args += [own, me]
    return pl.pallas_call(
        body, name=name, grid=grid,
        in_specs=in_specs,
        out_specs=[blk] * 4,
        out_shape=[jax.ShapeDtypeStruct((r, c), F32)] * 4,
        compiler_params=_params(dimension_semantics=("parallel",)),
    )(*args)


def _spread8(v):
    r = v.shape[0]
    return jnp.pad(jnp.pad(v[:, :, None], ((0, 0), (0, 0), (0, 7))).reshape(r, 8 * NH), ((0, 0), (0, HD - 8 * NH)))


def _gather8(v):
    return v[:, :8 * NH].reshape(v.shape[0], NH, 8)[:, :, 0]


def _cols_to_shards(g):
    r, c8 = g.shape
    return g.reshape(r, N_DEV, c8 // N_DEV).transpose(1, 0, 2)


def _shards_to_cols(g):
    n, r, c = g.shape
    return g.transpose(1, 0, 2).reshape(r, n * c)


def kernel(x, p, w_in, b_f, pre_gain, post_gain, conv_w, conv_b, w_rgate, b_rgate, w_igate, b_igate, lru_lambda, attn_out_gain, lru_out_gain, w_out, w_ple, ple_gain, w_ple_gate, b_ple_gate, loss_target, m_w_in, m_b_f, m_pre_gain, m_post_gain, m_conv_w, m_conv_b, m_w_rgate, m_b_rgate, m_w_igate, m_b_igate, m_lru_lambda, m_attn_out_gain, m_lru_out_gain, m_w_out, m_w_ple, m_ple_gain, m_w_ple_gate, m_b_ple_gate, v_w_in, v_b_f, v_pre_gain, v_post_gain, v_conv_w, v_conv_b, v_w_rgate, v_b_rgate, v_w_igate, v_b_igate, v_lru_lambda, v_attn_out_gain, v_lru_out_gain, v_w_out, v_w_ple, v_ple_gain, v_w_ple_gate, v_b_ple_gate):
    me = 4 * lax.axis_index("x") + 2 * lax.axis_index("y") + lax.axis_index("c")
    x2, p2, tgt = x[0], p[0, 0], loss_target[0]

    conv_w_shard8 = jnp.pad(conv_w[0], ((0, 8 - CONV_W), (0, 0)))
    wt, m_wt, v_wt = w_in[0].T, m_w_in[0].T, v_w_in[0].T
    g_wint, g_conv = _gather_two_level("gather_w_in", [wt.astype(BF16), conv_w_shard8])
    win_t = g_wint.reshape(D_IN, D)
    rest_state, rest_token = _exchange_start(
        "gather_rest_start", [w_out[0].astype(BF16), w_ple[0].astype(BF16), w_ple_gate[0].astype(BF16)], ["bcast"] * 3,
        after=g_conv)
    w_rest_t = jnp.concatenate([win_t[D_QKV + NH:], _spread8(win_t[D_QKV:D_QKV + NH].T).T], axis=0)
    conv_w8 = _shards_to_cols(g_conv)
    bf_pad = _spread8(b_f)
    w_r, w_i = w_rgate[0].astype(BF16), w_igate[0].astype(BF16)

    xn, zq = _prenorm_proj(x2, pre_gain, win_t, D_QKV, rest_token)
    zr, kx = _proj_rest(xn, w_rest_t, bf_pad)
    o, ax = _attn_fwd(zq, kx)
    xc, h = _lru_fwd(zr, conv_w8, conv_b, w_r, b_rgate, w_i, b_igate, lru_lambda)
    g_wout, g_wple, g_wpg = _exchange_wait("gather_rest_wait", rest_state, h)
    wout_full = g_wout.reshape(2 * D, D)
    wple_full = _shards_to_cols(g_wple)
    wpg_full = g_wpg.reshape(D, D)
    ycat, mix, h1, h1b = _residual(x2, o, h, zr, attn_out_gain, lru_out_gain, wout_full, post_gain)

    loss_part, dgp, dpe, dh1, dmix, g_ple_gain, g_b_gate, g_post_gain = _head(
        h1, p2, tgt, mix, wpg_full, wple_full, ple_gain, b_ple_gate, post_gain)
    gw_pg = _mm("bwd_gate_w", h1b, dgp, "tn", BF16)
    gw_ple = _mm("bwd_ple_w", p2, dpe, "tn", BF16)
    gw_out = _mm("bwd_out_w", ycat, dmix, "tn", BF16)
    do, delta, dga, dh, dgl, g_aog, g_log = _branch_out_bwd(o, h, zr, dmix, wout_full, attn_out_gain, lru_out_gain)
    dxl, g_wr, g_wi, g_br, g_bi, g_lam, g_cb, g_cw8 = _lru_bwd(
        zr, xc, h, dh, conv_w8, w_r, b_rgate, w_i, b_igate, lru_lambda)
    gates = jnp.concatenate([g_wr.reshape(D, HD), g_wi.reshape(D, HD)], axis=0).astype(BF16)
    outw_state, outw_token = _exchange_start(
        "exchange_outw_start",
        [gw_out.reshape(N_DEV, 2 * D // N_DEV, D), _cols_to_shards(gw_ple), gw_pg.reshape(N_DEV, D // N_DEV, D), gates],
        ["scatter"] * 3 + ["bcast"])
    dq, dk, dv, dcs, drs = _attn_bwd(zq, do, ax, delta, kx, outw_token)
    tn = lambda nm, dz: _mm("bwd_w_" + nm, dz, xn, "tn", BF16)
    gw_in_t = jnp.concatenate([tn("q", dq), tn("k", dk), tn("v", dv), jnp.zeros((NH, D), BF16),
                               tn("ga", dga), tn("xl", dxl), tn("gl", dgl)], axis=0)
    inw_state, inw_token = _exchange_start(
        "exchange_inw_start", [gw_in_t.reshape(N_DEV, D_IN_SHARD, D)], ["scatter"])
    dfl, g_bf_pad = _forget_bwd(dcs, drs, zr, bf_pad + inw_token[0:1, :])
    gw_fl = _gather8(_mm("bwd_w_fl", dfl, xn, "tn", F32).T).T
    dxn_a = _mm_cat("bwd_qkv_x", [dq, dk, dv], win_t, F32, after=inw_token)
    grad_x, g_pre_gain = _prenorm_bwd(x2, dxn_a, [dga, dxl, dgl, dfl], w_rest_t, dh1, pre_gain, inw_token)

    upd = {}
    me1 = me.reshape(1).astype(jnp.int32)
    r_wout, r_wple, r_wpg, r_gates = _exchange_wait("exchange_outw_wait", outw_state, grad_x, fill_own=False)
    upd["w_out"] = _adamw("adamw_w_out", r_wout[0], w_out[0], m_w_out[0], v_w_out[0], r_wout[1], me1)
    upd["w_ple"] = _adamw("adamw_w_ple", r_wple[0], w_ple[0], m_w_ple[0], v_w_ple[0], r_wple[1], me1)
    upd["w_ple_gate"] = _adamw("adamw_w_ple_gate", r_wpg[0], w_ple_gate[0], m_w_ple_gate[0], v_w_ple_gate[0], r_wpg[1], me1)
    gates_of = lambda a, b: jnp.concatenate([a[0].reshape(D, HD), b[0].reshape(D, HD)], axis=0)
    g_gates = _adamw("adamw_gates", r_gates[0], gates_of(w_rgate, w_igate), gates_of(m_w_rgate, m_w_igate),
                     gates_of(v_w_rgate, v_w_igate), r_gates[1], me1)
    upd["w_rgate"] = [a[:D].reshape(1, NH, HD, HD) for a in g_gates]
    upd["w_igate"] = [a[D:].reshape(1, NH, HD, HD) for a in g_gates]
    behind = upd["w_out"][0][0:1] + upd["w_ple_gate"][0][0:1] + jnp.pad(g_gates[0][0:1], ((0, 0), (0, D - HD)))
    small = jnp.concatenate(
        [jnp.pad(_gather8(g_bf_pad), ((0, 0), (0, D - NH))), g_pre_gain, g_post_gain, g_cb, g_br, g_bi, g_lam, g_aog, g_log,
         g_ple_gain, g_b_gate, g_cw8[:CONV_W], behind, jnp.pad(loss_part, ((0, 7), (0, D - 1))), gw_fl], axis=0)
    (r_small,) = _exchange("exchange_small", [small], ["bcast"], before=(wt, m_wt, v_wt))
    vec_names = ["b_f", "pre_gain", "post_gain", "conv_b", "b_rgate", "b_igate", "lru_lambda", "attn_out_gain",
                 "lru_out_gain", "ple_gain", "b_ple_gate"]
    vec_w = dict(b_f=(b_f, m_b_f, v_b_f), pre_gain=(pre_gain, m_pre_gain, v_pre_gain),
                 post_gain=(post_gain, m_post_gain, v_post_gain), conv_b=(conv_b, m_conv_b, v_conv_b),
                 b_rgate=(b_rgate, m_b_rgate, v_b_rgate), b_igate=(b_igate, m_b_igate, v_b_igate),
                 lru_lambda=(lru_lambda, m_lru_lambda, v_lru_lambda),
                 attn_out_gain=(attn_out_gain, m_attn_out_gain, v_attn_out_gain),
                 lru_out_gain=(lru_out_gain, m_lru_out_gain, v_lru_out_gain), ple_gain=(ple_gain, m_ple_gain, v_ple_gain),
                 b_ple_gate=(b_ple_gate, m_b_ple_gate, v_b_ple_gate))
    conv_mine = lambda a: lax.dynamic_slice_in_dim(a, me * HD, HD, axis=1)

    def small_rows(k):
        rows = [jnp.pad(vec_w[nm][k], ((0, 0), (0, D - vec_w[nm][k].shape[1]))) for nm in vec_names]
        cw = (conv_w, m_conv_w, v_conv_w)[k][0]
        full = lax.dynamic_update_slice_in_dim(jnp.ones((CONV_W, D), F32), cw, me * HD, axis=1)
        return jnp.concatenate(rows + [full, jnp.ones((17, D), F32)], axis=0)

    g_small = _adamw("adamw_small", r_small, small_rows(0), small_rows(1), small_rows(2))
    loss = g_small[0][16, 0]
    for idx, nm in enumerate(vec_names):
        width = vec_w[nm][0].shape[1]
        upd[nm] = [a[idx:idx + 1, :width] for a in g_small]
    base = len(vec_names)
    upd["conv_w"] = [conv_mine(a[base:base + CONV_W])[None] for a in g_small]
    (r_win,) = _exchange_wait("exchange_inw_wait", inw_state, g_small[0], fill_own=False)
    fl_sum = g_small[0][24:24 + NH]
    dev_a, row_a = divmod(D_QKV, D_IN_SHARD)
    n_a = D_IN_SHARD - row_a
    assert 2 * n_a == NH, "the forget rows are taken to straddle two row blocks evenly"
    own_win = r_win[1].astype(F32)
    at = jnp.where(me == dev_a, row_a, 0)
    rows_new = jnp.where(me == dev_a, fl_sum[:n_a], fl_sum[n_a:])
    rows_new = jnp.where((me == dev_a) | (me == dev_a + 1), rows_new, lax.dynamic_slice_in_dim(own_win, at, n_a, axis=0))
    own_win = lax.dynamic_update_slice_in_dim(own_win, rows_new, at, axis=0)
    upd["w_in"] = [a.T for a in _adamw("adamw_w_in", r_win[0], wt, m_wt, v_wt, own_win, me1)]
    for nm in ("w_in", "w_out", "w_ple", "w_ple_gate"):
        upd[nm] = [a[None] for a in upd[nm]]

    order = ["w_in", "b_f", "pre_gain", "post_gain", "conv_w", "conv_b", "w_rgate", "b_rgate", "w_igate", "b_igate",
             "lru_lambda", "attn_out_gain", "lru_out_gain", "w_out", "w_ple", "ple_gain", "w_ple_gate", "b_ple_gate"]
    outs = [loss, grad_x[None]]
    for k in range(4):
        outs += [upd[nm][k] for nm in order]
    return tuple(outs)
```

```python
import functools

import jax
import jax.numpy as jnp
from jax import lax
from jax.experimental import pallas as pl
from jax.experimental.pallas import tpu as pltpu

F32 = jnp.float32
BF16 = jnp.bfloat16

N_DEV = 8
D = 1024
HD = 128
NH = 8
D_IN = 6152
D_IN_SHARD = D_IN // N_DEV
D_QKV = 3 * D
D_REST = 3 * D + HD
FL_COL = 3 * D
D_PLE = 256
CONV_W = 4
LRU_C = 8.0
RMS_EPS = 1e-6
SCALE = HD ** -0.5
LOG2E = 1.4426950408889634
Q_SCALE = SCALE * LOG2E
NEG = -1e30

ADAM_LR = 0.001
ADAM_B1 = 0.9
ADAM_B2 = 0.999
ADAM_EPS = 1e-08
ADAM_WD = 0.01
ADAM_STEP = 10

TS = 256
TQ = 1024
ROW_PARTS = 2
FWD_HEADS = 2
FWD_ROW_PARTS = 2
VMEM_LIMIT = 48 * 1024 * 1024

NT_DIMS = (((1,), (1,)), ((), ()))
TN_DIMS = (((0,), (0,)), ((), ()))


def _params(**kw):
    return pltpu.CompilerParams(vmem_limit_bytes=VMEM_LIMIT, **kw)


def _sigmoid(v):
    return 0.5 * jnp.tanh(0.5 * v) + 0.5


def _sigmoid_rel(v):
    return 1.0 / (1.0 + jnp.exp(-v))


def _rms_fwd(v):
    rstd = lax.rsqrt(jnp.mean(v * v, axis=-1, keepdims=True) + RMS_EPS)
    return v * rstd, rstd


def _rms_bwd(vhat, rstd, dvhat):
    return rstd * (dvhat - vhat * jnp.mean(dvhat * vhat, axis=-1, keepdims=True))


def _colsum(v):
    return jnp.sum(v, axis=0, keepdims=True)


def _rows_iota(t):
    return lax.broadcasted_iota(jnp.int32, (t, 1), 0)


def _scan(a, u, reverse):
    t, c = a.shape
    rows = _rows_iota(t)
    d = 1
    while d < t:
        if d < 8:
            valid = rows < t - d if reverse else rows >= d
            shift = t - d if reverse else d
            u = jnp.where(valid, u + a * pltpu.roll(u, shift, 0), u)
            a = jnp.where(valid, a * pltpu.roll(a, shift, 0), a)
        else:
            zeros, ones = jnp.zeros((d, c), F32), jnp.ones((d, c), F32)
            if reverse:
                u_far, a_far = jnp.concatenate([u[d:], zeros], axis=0), jnp.concatenate([a[d:], ones], axis=0)
            else:
                u_far, a_far = jnp.concatenate([zeros, u[:t - d]], axis=0), jnp.concatenate([ones, a[:t - d]], axis=0)
            u = u + a * u_far
            a = a * a_far
        d *= 2
    return a, u


def _cumsum_fwd(v):
    t = v.shape[0]
    rows = _rows_iota(t)
    d = 1
    while d < t:
        v = jnp.where(rows >= d, v + pltpu.roll(v, d, 0), v)
        d *= 2
    return v


def _cumsum_bwd(v):
    t = v.shape[0]
    rows = _rows_iota(t)
    d = 1
    while d < t:
        v = jnp.where(rows < t - d, v + pltpu.roll(v, t - d, 0), v)
        d *= 2
    return v


def _bias_lanes(v, at, ones_at):
    lane = lax.broadcasted_iota(jnp.int32, v.shape, 1)
    hi = v.astype(BF16).astype(F32)
    mid = (v - hi).astype(BF16).astype(F32)
    lo = ((v - hi) - mid).astype(BF16).astype(F32)
    out = jnp.where((lane >= ones_at) & (lane < ones_at + 3), 1.0, 0.0)
    for k, piece in enumerate((hi, mid, lo)):
        out = jnp.where(lane == at + k, piece, out)
    return out.astype(BF16)


def _shift_down(ext, k, t):
    return pltpu.roll(ext, k, 0)[8:, :] if k else ext[8:, :]


def _shift_up(ext, k, t):
    return pltpu.roll(ext, t + 8 - k, 0)[:t, :] if k else ext[:t, :]


def _exchange(name, arrs, kinds, before=()):
    n = len(arrs)
    nb = len(before)
    out_shape = []
    for a, kind in zip(arrs, kinds):
        shp = a.shape if kind == "scatter" else (N_DEV,) + a.shape
        out_shape.append(jax.ShapeDtypeStruct(shp, a.dtype))

    def body(*refs):
        ins, outs = refs[:n], refs[n + nb:2 * n + nb]
        send_sems, recv_sems, local_sems = refs[2 * n + nb:]
        x, y, c = lax.axis_index("x"), lax.axis_index("y"), lax.axis_index("c")
        me = 4 * x + 2 * y + c
        copies = []
        for i in range(n):
            scatter = kinds[i] == "scatter"
            mine = pltpu.make_async_copy(ins[i].at[me] if scatter else ins[i], outs[i].at[me], local_sems.at[i])
            mine.start()
            copies.append(mine)
            for m in range(1, N_DEV):
                px = 1 - x if m & 4 else x
                py = 1 - y if m & 2 else y
                pc = 1 - c if m & 1 else c
                peer = 4 * px + 2 * py + pc
                cp = pltpu.make_async_remote_copy(
                    src_ref=ins[i].at[peer] if scatter else ins[i],
                    dst_ref=outs[i].at[me],
                    send_sem=send_sems.at[i, m - 1],
                    recv_sem=recv_sems.at[i, m - 1],
                    device_id=(px, py, pc),
                    device_id_type=pl.DeviceIdType.MESH,
                )
                cp.start()
                copies.append(cp)
        for cp in copies:
            cp.wait()

    any_spec = pl.BlockSpec(memory_space=pl.ANY)
    return pl.pallas_call(
        body,
        name=name,
        out_shape=out_shape,
        in_specs=[any_spec] * (n + nb),
        out_specs=[any_spec] * n,
        scratch_shapes=[
            pltpu.SemaphoreType.DMA((n, N_DEV - 1)),
            pltpu.SemaphoreType.DMA((n, N_DEV - 1)),
            pltpu.SemaphoreType.DMA((n,)),
        ],
        compiler_params=pltpu.CompilerParams(has_side_effects=True),
    )(*arrs, *before)


def _gather_two_level(name, arrs, pieces=1):
    n = len(arrs)
    items = []
    for i, a in enumerate(arrs):
        rows = a.shape[0]
        if pieces > 1 and rows >= 512:
            step = -(-rows // (16 * pieces)) * 16
            items += [(i, r0, min(step, rows - r0)) for r0 in range(0, rows, step)]
        else:
            items.append((i, 0, rows))
    n_items = len(items)

    def body(*refs):
        ins, outs = refs[:n], refs[n:2 * n]
        send_sems, recv_sems, local_sems = refs[2 * n:]
        x, y, c = lax.axis_index("x"), lax.axis_index("y"), lax.axis_index("c")
        me, sibling = (x, y, c), (x, y, 1 - c)
        chips = [(1 - x, y), (x, 1 - y), (1 - x, 1 - y)]

        def rows_of(ref, t):
            i, r0, rn = items[t]
            return ref if rn == arrs[i].shape[0] else ref.at[pl.ds(r0, rn)]

        def slot(t, dev):
            return rows_of(outs[items[t][0]].at[4 * dev[0] + 2 * dev[1] + dev[2]], t)

        def copy(t, k, block, to, from_input=False):
            return pltpu.make_async_remote_copy(
                src_ref=rows_of(ins[items[t][0]], t) if from_input else slot(t, block), dst_ref=slot(t, block),
                send_sem=send_sems.at[t, k], recv_sem=recv_sems.at[t, k],
                device_id=to, device_id_type=pl.DeviceIdType.MESH)

        own, sent = [], []
        for t in range(n_items):
            mine = pltpu.make_async_copy(rows_of(ins[items[t][0]], t), slot(t, me), local_sems.at[t])
            mine.start()
            own.append(mine)
            first = [copy(t, 1 + j, me, (*chip, c), from_input=True) for j, chip in enumerate(chips)]
            first.append(copy(t, 0, me, sibling, from_input=True))
            for cp in first:
                cp.start()
            sent += first
        for t in range(n_items):
            for j, chip in enumerate(chips):
                copy(t, 1 + j, (*chip, c), me).wait_recv()
                fwd = copy(t, 4 + j, (*chip, c), sibling)
                fwd.start()
                sent.append(fwd)
        for t in range(n_items):
            copy(t, 0, sibling, me).wait_recv()
            for j, chip in enumerate(chips):
                copy(t, 4 + j, (*chip, 1 - c), me).wait_recv()
        for cp in sent:
            cp.wait_send()
        for cp in own:
            cp.wait()

    any_spec = pl.BlockSpec(memory_space=pl.ANY)
    return pl.pallas_call(
        body, name=name,
        out_shape=[jax.ShapeDtypeStruct((N_DEV,) + a.shape, a.dtype) for a in arrs],
        in_specs=[any_spec] * n, out_specs=[any_spec] * n,
        scratch_shapes=[pltpu.SemaphoreType.DMA((n_items, 7)), pltpu.SemaphoreType.DMA((n_items, 7)),
                        pltpu.SemaphoreType.DMA((n_items,))],
        compiler_params=pltpu.CompilerParams(has_side_effects=True),
    )(*arrs)


def _peers(x, y, c):
    out = []
    for m in range(1, N_DEV):
        px = 1 - x if m & 4 else x
        py = 1 - y if m & 2 else y
        pc = 1 - c if m & 1 else c
        out.append((m, (px, py, pc), 4 * px + 2 * py + pc))
    return out


def _split_copies(kinds, src_refs, land_refs, send_sems, recv_sems):
    x, y, c = lax.axis_index("x"), lax.axis_index("y"), lax.axis_index("c")
    me = 4 * x + 2 * y + c
    copies = []
    for i, kind in enumerate(kinds):
        for m, peer, pidx in _peers(x, y, c):
            copies.append(pltpu.make_async_remote_copy(
                src_ref=src_refs[i].at[pidx] if kind == "scatter" else src_refs[i],
                dst_ref=land_refs[i].at[me],
                send_sem=send_sems.at[i * (N_DEV - 1) + m - 1],
                recv_sem=recv_sems.at[i * (N_DEV - 1) + m - 1],
                device_id=peer,
                device_id_type=pl.DeviceIdType.MESH,
            ))
    return copies


_HBM_SPEC = pl.BlockSpec(memory_space=pltpu.HBM)
_SEM_SPEC = pl.BlockSpec(memory_space=pltpu.SEMAPHORE)
_DATAFLOW = pltpu.SideEffectType.DATAFLOW_SIDE_EFFECTING


def _exchange_start(name, arrs, kinds, after=None):
    n = len(arrs)
    extra = [] if after is None else [after]
    lands = []
    for a, kind in zip(arrs, kinds):
        shp = a.shape if kind == "scatter" else (N_DEV,) + a.shape
        lands.append(lax.empty(shp, a.dtype))

    def body(*refs):
        src_refs, land_refs = refs[:n], refs[n:2 * n]
        send_sems, recv_sems = refs[2 * n + len(extra):2 * n + len(extra) + 2]
        token = refs[-1]
        for cp in _split_copies(kinds, src_refs, land_refs, send_sems, recv_sems):
            cp.start()
        token[...] = jnp.zeros_like(token)

    n_sem = n * (N_DEV - 1)
    hbm = lambda a: pltpu.HBM(a.shape, a.dtype)
    res = pl.pallas_call(
        body, name=name,
        out_shape=(pltpu.SemaphoreType.DMA((n_sem,)), pltpu.SemaphoreType.DMA((n_sem,)),
                   *[hbm(a) for a in arrs], *[hbm(a) for a in lands], jax.ShapeDtypeStruct((8, HD), F32)),
        in_specs=[_HBM_SPEC] * (2 * n) + [pl.BlockSpec(memory_space=pl.ANY)] * len(extra),
        out_specs=(_SEM_SPEC, _SEM_SPEC, *[_HBM_SPEC] * (2 * n), pl.BlockSpec(memory_space=pltpu.VMEM)),
        input_output_aliases={i: 2 + i for i in range(2 * n)},
        compiler_params=pltpu.CompilerParams(has_side_effects=_DATAFLOW),
    )(*[pltpu.with_memory_space_constraint(a, pltpu.HBM) for a in arrs],
      *[pltpu.with_memory_space_constraint(a, pltpu.HBM) for a in lands], *extra)
    return (kinds, res[0], res[1], res[2:2 + n], res[2 + n:2 + 2 * n]), res[-1]


def _exchange_wait(name, state, after, fill_own=True):
    kinds, send_sems, recv_sems, srcs, lands = state
    n = len(srcs)

    def body(*refs):
        src_refs, land_refs = refs[:n], refs[n:2 * n]
        send_sems_ref, recv_sems_ref = refs[2 * n:2 * n + 2]
        for cp in _split_copies(kinds, src_refs, land_refs, send_sems_ref, recv_sems_ref):
            cp.wait_send()
            cp.wait_recv()

    res = pl.pallas_call(
        body, name=name,
        out_shape=tuple(pltpu.HBM(a.shape, a.dtype) for a in (*srcs, *lands)),
        in_specs=[_HBM_SPEC] * (2 * n) + [_SEM_SPEC, _SEM_SPEC, pl.BlockSpec(memory_space=pl.ANY)],
        out_specs=tuple([_HBM_SPEC] * (2 * n)),
        input_output_aliases={i: i for i in range(2 * n)},
        compiler_params=pltpu.CompilerParams(has_side_effects=_DATAFLOW),
    )(*srcs, *lands, send_sems, recv_sems, after)
    me = 4 * lax.axis_index("x") + 2 * lax.axis_index("y") + lax.axis_index("c")
    outs = []
    for kind, src, land in zip(kinds, res[:n], res[n:]):
        own = lax.dynamic_index_in_dim(src, me, 0, keepdims=False) if kind == "scatter" else src
        outs.append(lax.dynamic_update_index_in_dim(land, own, me, 0) if fill_own else (land, own))
    return outs


def _pick(n, cands):
    for t in cands:
        if n % t == 0:
            return t
    raise ValueError(f"no tile for {n}")


def _mm(name, a, b, mode, out_dtype, after=None):
    if mode == "nn":
        (m, k), (k2, n) = a.shape, b.shape
    elif mode == "nt":
        (m, k), (n, k2) = a.shape, b.shape
    else:
        (k, m), (k2, n) = a.shape, b.shape
    assert k == k2, (name, a.shape, b.shape)
    if mode == "tn":
        tm = _pick(m, (1024, 640, 512, 256, 128))
        tn = _pick(n, (1024, 640, 512, 256, 128))
        tk = _pick(k, (2048, 1024, 512, 256))
    else:
        tm, tn, tk = _pick(m, (512, 256)), n, k
    nk = k // tk

    def body(a_ref, b_ref, *rest):
        o_ref = rest[-2] if nk > 1 else rest[-1]
        av = a_ref[...].astype(BF16)
        bv = b_ref[...].astype(BF16)
        if mode == "nn":
            part = jnp.dot(av, bv, preferred_element_type=F32)
        elif mode == "nt":
            part = lax.dot_general(av, bv, NT_DIMS, preferred_element_type=F32)
        else:
            part = lax.dot_general(av, bv, TN_DIMS, preferred_element_type=F32)
        if nk == 1:
            o_ref[...] = part.astype(out_dtype)
            return
        acc_ref = rest[-1]
        kk = pl.program_id(2)

        @pl.when(kk == 0)
        def _():
            acc_ref[...] = part

        @pl.when(kk > 0)
        def _():
            acc_ref[...] += part

        @pl.when(kk == nk - 1)
        def _():
            o_ref[...] = acc_ref[...].astype(out_dtype)

    if mode == "tn":
        a_spec = pl.BlockSpec((tk, tm), lambda j, i, kk: (kk, i))
    else:
        a_spec = pl.BlockSpec((tm, tk), lambda j, i, kk: (i, kk))
    if mode == "nt":
        b_spec = pl.BlockSpec((tn, tk), lambda j, i, kk: (j, kk))
    else:
        b_spec = pl.BlockSpec((tk, tn), lambda j, i, kk: (kk, j))
    in_specs, args = [a_spec, b_spec], [a, b]
    if after is not None:
        in_specs.append(pl.BlockSpec((8, HD), lambda j, i, kk: (0, 0)))
        args.append(after)
    return pl.pallas_call(
        body,
        name=name,
        grid=(n // tn, m // tm, nk),
        in_specs=in_specs,
        out_specs=pl.BlockSpec((tm, tn), lambda j, i, kk: (i, j)),
        out_shape=jax.ShapeDtypeStruct((m, n), out_dtype),
        scratch_shapes=[pltpu.VMEM((tm, tn), F32)] if nk > 1 else [],
        compiler_params=_params(dimension_semantics=("parallel", "parallel", "arbitrary")),
    )(*args)


def _mm_cat(name, a_list, b, out_dtype, after=None):
    m = a_list[0].shape[0]
    ks = [a.shape[1] for a in a_list]
    n = b.shape[1]
    assert sum(ks) <= b.shape[0], (name, ks, b.shape)
    tm = _pick(m, (512, 256))
    na = len(a_list)

    def body(*refs):
        b_ref, o_ref = refs[na], refs[-1]
        k0, acc = 0, None
        for a_ref, kw in zip(refs[:na], ks):
            part = jnp.dot(a_ref[...].astype(BF16), b_ref[k0:k0 + kw, :], preferred_element_type=F32)
            acc = part if acc is None else acc + part
            k0 += kw
        o_ref[...] = acc.astype(out_dtype)

    in_specs = [pl.BlockSpec((tm, kw), lambda i: (i, 0)) for kw in ks] + [pl.BlockSpec((sum(ks), n), lambda i: (0, 0))]
    args = [*a_list, b]
    if after is not None:
        in_specs.append(pl.BlockSpec((8, HD), lambda i: (0, 0)))
        args.append(after)
    return pl.pallas_call(
        body, name=name, grid=(m // tm,),
        in_specs=in_specs, out_specs=pl.BlockSpec((tm, n), lambda i: (i, 0)),
        out_shape=jax.ShapeDtypeStruct((m, n), out_dtype),
        compiler_params=_params(dimension_semantics=("parallel",)),
    )(*args)


def _row(c, col=0):
    return pl.BlockSpec((TS, c), lambda i: (i, col))


def _vec(r, c):
    return pl.BlockSpec((r, c), lambda i: (0, 0))


def _prenorm_proj(x, pre_gain, w_t, n, after):
    s = x.shape[0]
    tm = 512

    def body(x_ref, g_ref, w_ref, after_ref, xn_ref, z_ref):
        xhat, _ = _rms_fwd(x_ref[...])
        xn = (xhat * g_ref[...]).astype(BF16)
        xn_ref[...] = xn
        z = lax.dot_general(xn, w_ref[...], NT_DIMS, preferred_element_type=F32)
        z_ref[:, :D] = (z[:, :D] * Q_SCALE).astype(BF16)
        z_ref[:, D:] = z[:, D:].astype(BF16)

    return pl.pallas_call(
        body, name="prenorm_proj_qkv", grid=(s // tm,),
        in_specs=[pl.BlockSpec((tm, D), lambda i: (i, 0)), _vec(1, D), _vec(n, D), _vec(8, HD)],
        out_specs=[pl.BlockSpec((tm, D), lambda i: (i, 0)), pl.BlockSpec((tm, n), lambda i: (i, 0))],
        out_shape=[jax.ShapeDtypeStruct((s, D), BF16), jax.ShapeDtypeStruct((s, n), BF16)],
        compiler_params=_params(dimension_semantics=("parallel",)),
    )(x, pre_gain, w_t, after)


def _proj_rest(xn, w_rest_t, bf_pad):
    s = xn.shape[0]
    tm = 512

    def body(x_ref, w_ref, b_ref, z_ref, kx_ref, c_buf, carry):
        @pl.when(pl.program_id(0) == 0)
        def _():
            carry[...] = jnp.zeros_like(carry)

        z = lax.dot_general(x_ref[...], w_ref[...], NT_DIMS, preferred_element_type=F32)
        z_ref[...] = z
        fl = z[:, FL_COL:] + b_ref[...]
        ls = jnp.minimum(fl, 0.0) - jnp.log(1.0 + jnp.exp(-jnp.abs(fl)))
        c_buf[...] = _cumsum_fwd(ls) + carry[0:1, :]
        carry[0:1, :] = c_buf[tm - 1:tm, :]
        cv = c_buf[...]
        for h in range(NH):
            kx_ref[h] = _bias_lanes(jnp.broadcast_to(cv[:, 8 * h:8 * h + 1], (tm, HD)) * (-LOG2E), 0, 3)

    return pl.pallas_call(
        body, name="proj_rest", grid=(s // tm,),
        in_specs=[pl.BlockSpec((tm, D), lambda i: (i, 0)), _vec(D_REST, D), _vec(1, HD)],
        out_specs=[pl.BlockSpec((tm, D_REST), lambda i: (i, 0)), pl.BlockSpec((NH, tm, HD), lambda i: (0, i, 0))],
        out_shape=[jax.ShapeDtypeStruct((s, D_REST), F32), jax.ShapeDtypeStruct((NH, s, HD), BF16)],
        scratch_shapes=[pltpu.VMEM((tm, HD), F32), pltpu.VMEM((8, HD), F32)],
        compiler_params=_params(dimension_semantics=("arbitrary",)),
    )(xn, w_rest_t, bf_pad)


def _attn_fwd(zq, kx):
    s = zq.shape[0]
    n = s // TQ
    nb = TQ // HD

    hp = FWD_HEADS

    def body(q_ref, k_ref, v_ref, kx_ref, o_ref, ax_ref):
        i = pl.program_id(1)
        lane = lax.broadcasted_iota(jnp.int32, (TQ, HD), 1)
        row = lax.broadcasted_iota(jnp.int32, (TQ, HD), 0)
        ones = jnp.where(lane < 3, 1.0, 0.0).astype(BF16)
        qas = [jnp.concatenate([q_ref[:, HD * hh:HD * (hh + 1)], ones], axis=1) for hh in range(hp)]
        rp = TQ // FWD_ROW_PARTS

        def step(j, carry, masked):
            m, l, acc = carry
            rows = pl.ds(pl.multiple_of(j * TQ, TQ), TQ)
            kas = [jnp.concatenate([k_ref[rows, HD * hh:HD * (hh + 1)], kx_ref[hh, rows, :]], axis=1) for hh in range(hp)]
            vs = [v_ref[rows, HD * hh:HD * (hh + 1)] for hh in range(hp)]
            units = [(hh, t) for hh in range(hp) for t in range(FWD_ROW_PARTS)]
            keys = [rp * (t + 1) if masked else TQ for _, t in units]
            u_parts = [lax.dot_general(qas[hh][rp * t:rp * (t + 1)], kas[hh][:kn], NT_DIMS, preferred_element_type=F32)
                       for (hh, t), kn in zip(units, keys)]
            out = []
            for (hh, t), u, kn in zip(units, u_parts, keys):
                local = slice(rp * t, rp * (t + 1))
                part = slice(hh * TQ + rp * t, hh * TQ + rp * (t + 1))
                us = [u[:, HD * b:HD * (b + 1)] for b in range(kn // HD)]
                if masked:
                    us = [ub if HD * (b + 1) <= rp * t else jnp.where(row[local] >= lane[local] + HD * b, ub, NEG)
                          for b, ub in enumerate(us)]
                bm = functools.reduce(jnp.maximum, us)
                m_new = jnp.maximum(m[part], jnp.max(bm, axis=1, keepdims=True))
                alpha = jnp.exp2(m[part] - m_new)
                ps = [jnp.exp2(ub - m_new) for ub in us]
                l_new = alpha * l[part] + functools.reduce(jnp.add, ps)
                pr = jnp.concatenate(ps, axis=1).astype(BF16)
                out.append((m_new, l_new, alpha * acc[part] + jnp.dot(pr, vs[hh][:kn], preferred_element_type=F32)))
            return tuple(jnp.concatenate([o[t] for o in out], axis=0) for t in range(3))

        init = (jnp.full((hp * TQ, HD), NEG, F32), jnp.zeros((hp * TQ, HD), F32), jnp.zeros((hp * TQ, HD), F32))
        carry = lax.fori_loop(0, i, lambda j, cr: step(j, cr, False), init)
        m, l, acc = step(i, carry, True)
        l_row = jnp.sum(l, axis=1, keepdims=True)
        o_all = acc / l_row
        ax_all = _bias_lanes(-(m + jnp.log(l_row) * LOG2E), 3, 0)
        for hh in range(hp):
            o_ref[:, HD * hh:HD * (hh + 1)] = o_all[hh * TQ:(hh + 1) * TQ]
            ax_ref[hh] = ax_all[hh * TQ:(hh + 1) * TQ]

    return pl.pallas_call(
        body, name="attn_fwd", grid=(NH // hp, n),
        in_specs=[
            pl.BlockSpec((TQ, hp * HD), lambda h, i: (i, h)),
            pl.BlockSpec((s, hp * HD), lambda h, i: (0, NH // hp + h)),
            pl.BlockSpec((s, hp * HD), lambda h, i: (0, 2 * (NH // hp) + h)),
            pl.BlockSpec((hp, s, HD), lambda h, i: (h, 0, 0)),
        ],
        out_specs=[pl.BlockSpec((TQ, hp * HD), lambda h, i: (i, h)), pl.BlockSpec((hp, TQ, HD), lambda h, i: (h, i, 0))],
        out_shape=[jax.ShapeDtypeStruct((s, D), F32), jax.ShapeDtypeStruct((NH, s, HD), BF16)],
        compiler_params=_params(dimension_semantics=("parallel", "parallel")),
    )(zq, zq, zq, kx)


def _attn_bwd(zq, do, ax, delta, kx, after):
    s = zq.shape[0]
    n = s // TQ
    nb = TQ // HD

    def body(k_ref, v_ref, kx_ref, q_ref, ax_ref, do_ref, dl_ref, after_ref, dq_out, dk_ref, dv_ref, dcs_ref, drs_ref,
             dq_ref):
        j = pl.program_id(1)

        @pl.when(j == 0)
        def _():
            dq_ref[...] = jnp.zeros_like(dq_ref)
            drs_ref[...] = jnp.zeros_like(drs_ref)

        k = k_ref[...]
        v = v_ref[...]
        ka = jnp.concatenate([k, kx_ref[0]], axis=1)
        row = lax.broadcasted_iota(jnp.int32, (TQ, HD), 0)
        lane = lax.broadcasted_iota(jnp.int32, (TQ, HD), 1)

        def step(i, carry, r0, rn, kn, masked):
            dk, dv, dcs = carry
            rows = pl.ds(pl.multiple_of(i * TQ + r0, rn), rn)
            q = q_ref[rows, :]
            dout = do_ref[rows, :]
            dlv = dl_ref[0, rows, :]
            qa = jnp.concatenate([q, ax_ref[0, rows, :]], axis=1)
            u = lax.dot_general(qa, ka[:kn], NT_DIMS, preferred_element_type=F32)
            dp = lax.dot_general(dout, v[:kn], NT_DIMS, preferred_element_type=F32)
            prs, dss = [], []
            for b in range(kn // HD):
                cs = slice(HD * b, HD * (b + 1))
                ub = u[:, cs]
                if masked and HD * (b + 1) > r0:
                    ub = jnp.where(row[:rn] + r0 >= lane[:rn] + HD * b, ub, NEG)
                pb = jnp.exp2(ub)
                prs.append(pb)
                dss.append(pb * (dp[:, cs] - dlv))
            drs_ref[0, rows, :] += functools.reduce(jnp.add, dss)
            ds = jnp.concatenate(dss, axis=1)
            dsb = ds.astype(BF16)
            dcs_new = jnp.sum(ds.reshape(rn // 8, 8, kn), axis=0)
            dv_new = lax.dot_general(jnp.concatenate(prs, axis=1).astype(BF16), dout, TN_DIMS, preferred_element_type=F32)
            dk_new = lax.dot_general(dsb, q, TN_DIMS, preferred_element_type=F32)
            if kn < TQ:
                dcs_new = jnp.concatenate([dcs_new, jnp.zeros((8, TQ - kn), F32)], axis=1)
                dv_new = jnp.concatenate([dv_new, jnp.zeros((TQ - kn, HD), F32)], axis=0)
                dk_new = jnp.concatenate([dk_new, jnp.zeros((TQ - kn, HD), F32)], axis=0)
            dq_ref[rows, :] += jnp.dot(dsb, k[:kn], preferred_element_type=F32) * SCALE
            return dk + dk_new, dv + dv_new, dcs + dcs_new

        carry = (jnp.zeros((TQ, HD), F32), jnp.zeros((TQ, HD), F32), jnp.zeros((8, TQ), F32))
        rp = TQ // ROW_PARTS
        for t in range(ROW_PARTS):
            carry = step(j, carry, rp * t, rp, rp * (t + 1), True)
        dk, dv, dcs = lax.fori_loop(j + 1, n, lambda i, cr: step(i, cr, 0, TQ, TQ, False), carry)
        dk_ref[...] = (dk * (SCALE / Q_SCALE)).astype(BF16)
        dv_ref[...] = dv.astype(BF16)
        dcs_ref[0] = jnp.broadcast_to(_colsum(dcs), (8, TQ))

        @pl.when(j == n - 1)
        def _():
            dq_out[...] = dq_ref[...].astype(BF16)

    return pl.pallas_call(
        body, name="attn_bwd", grid=(NH, n),
        in_specs=[
            pl.BlockSpec((TQ, HD), lambda h, j: (j, NH + h)),
            pl.BlockSpec((TQ, HD), lambda h, j: (j, 2 * NH + h)),
            pl.BlockSpec((1, TQ, HD), lambda h, j: (h, j, 0)),
            pl.BlockSpec((s, HD), lambda h, j: (0, h)),
            pl.BlockSpec((1, s, HD), lambda h, j: (h, 0, 0)),
            pl.BlockSpec((s, HD), lambda h, j: (0, h)),
            pl.BlockSpec((1, s, HD), lambda h, j: (h, 0, 0)),
            pl.BlockSpec((8, HD), lambda h, j: (0, 0)),
        ],
        out_specs=[
            pl.BlockSpec((s, HD), lambda h, j: (0, h)),
            pl.BlockSpec((TQ, HD), lambda h, j: (j, h)),
            pl.BlockSpec((TQ, HD), lambda h, j: (j, h)),
            pl.BlockSpec((1, 8, TQ), lambda h, j: (j, h, 0)),
            pl.BlockSpec((1, s, HD), lambda h, j: (h, 0, 0)),
        ],
        out_shape=[
            jax.ShapeDtypeStruct((s, D), BF16),
            jax.ShapeDtypeStruct((s, D), BF16),
            jax.ShapeDtypeStruct((s, D), BF16),
            jax.ShapeDtypeStruct((n, 8 * NH, TQ), F32),
            jax.ShapeDtypeStruct((NH, s, HD), F32),
        ],
        scratch_shapes=[pltpu.VMEM((s, HD), F32)],
        compiler_params=_params(dimension_semantics=("parallel", "arbitrary")),
    )(zq, zq, kx, zq, ax, do, delta, after)


def _forget_bwd(dcs, drs, zr, bf_pad):
    n = dcs.shape[0]
    s = n * TQ

    def body(dcs_ref, drs_ref, fl_ref, b_ref, dfl_ref, gb_ref, buf, carry):
        i = pl.program_id(0)

        @pl.when(i == 0)
        def _():
            carry[...] = jnp.zeros_like(carry)
            gb_ref[...] = jnp.zeros_like(gb_ref)

        dc_t = jnp.concatenate([dcs_ref[0], jnp.zeros((HD - 8 * NH, TQ), F32)], axis=0)
        lane = lax.broadcasted_iota(jnp.int32, (TQ, HD), 1)
        dc = -dc_t.T
        for hh in range(NH):
            dc = dc + jnp.where(lane == 8 * hh, jnp.sum(drs_ref[hh], axis=1, keepdims=True), 0.0)
        buf[...] = _cumsum_bwd(dc) + carry[0:1, :]
        carry[0:1, :] = buf[0:1, :]
        fl = fl_ref[...] + b_ref[...]
        dfl = buf[...] * _sigmoid_rel(-fl)
        dfl_ref[...] = dfl.astype(BF16)
        gb_ref[...] += _colsum(dfl)

    return pl.pallas_call(
        body, name="forget_bwd", grid=(n,),
        in_specs=[
            pl.BlockSpec((1, 8 * NH, TQ), lambda i: (n - 1 - i, 0, 0)),
            pl.BlockSpec((NH, TQ, HD), lambda i: (0, n - 1 - i, 0)),
            pl.BlockSpec((TQ, HD), lambda i: (n - 1 - i, FL_COL // HD)),
            _vec(1, HD),
        ],
        out_specs=[pl.BlockSpec((TQ, HD), lambda i: (n - 1 - i, 0)), _vec(1, HD)],
        out_shape=[jax.ShapeDtypeStruct((s, HD), BF16), jax.ShapeDtypeStruct((1, HD), F32)],
        scratch_shapes=[pltpu.VMEM((TQ, HD), F32), pltpu.VMEM((8, HD), F32)],
        compiler_params=_params(dimension_semantics=("arbitrary",)),
    )(dcs, drs, zr, bf_pad)


def _gates(xc, w_ref, b, sigmoid):
    xb = xc.astype(BF16)
    pre = jnp.concatenate(
        [jnp.dot(xb[:, HD * g:HD * (g + 1)], w_ref[g], preferred_element_type=F32) for g in range(NH)], axis=1)
    return sigmoid(pre + b)


def _lru_coeffs(r, lam):
    sp = jnp.maximum(-lam, 0.0) + jnp.log(1.0 + jnp.exp(-jnp.abs(lam)))
    log_a = -LRU_C * r * sp
    a = jnp.exp(log_a)
    y = 2.0 * log_a
    em1 = jnp.where(jnp.abs(y) < 0.01, y * (1.0 + y * (0.5 + y * (1.0 / 6.0))), jnp.exp(y) - 1.0)
    em = -em1
    inv_gam = lax.rsqrt(jnp.maximum(em, 1e-37))
    return sp, a, em * inv_gam, inv_gam


def _conv_taps(ext, t):
    return [_shift_down(ext, CONV_W - 1 - jj, t) for jj in range(CONV_W)]


def _lru_fwd(zr, conv_w8, conv_b, w_r, b_r, w_i, b_i, lam):
    s = zr.shape[0]
    n = s // TS
    xl_col = 1

    def body(xl_ref, halo_ref, cw_ref, cb_ref, wr_ref, br_ref, wi_ref, bi_ref, lam_ref, xc_ref, h_ref, carry):
        i = pl.program_id(0)

        @pl.when(i == 0)
        def _():
            carry[...] = jnp.zeros_like(carry)

        halo = jnp.where(i == 0, 0.0, halo_ref[...])
        taps = _conv_taps(jnp.concatenate([halo, xl_ref[...]], axis=0), TS)
        xc = cb_ref[...] + sum(cw_ref[jj:jj + 1, :] * taps[jj] for jj in range(CONV_W))
        xc_ref[...] = xc
        r = _gates(xc, wr_ref, br_ref[...], _sigmoid_rel)
        ig = _gates(xc, wi_ref, bi_ref[...], _sigmoid)
        _, a, gam, _ = _lru_coeffs(r, lam_ref[...])
        a_cum, h_loc = _scan(a, gam * (ig * xc), False)
        h_ref[...] = h_loc + a_cum * carry[0:1, :]
        carry[0:1, :] = h_ref[TS - 1:TS, :]

    return pl.pallas_call(
        body, name="lru_fwd", grid=(n,),
        in_specs=[
            _row(D, xl_col),
            pl.BlockSpec((8, D), lambda i: (jnp.maximum(i * (TS // 8) - 1, 0), xl_col)),
            _vec(8, D), _vec(1, D),
            pl.BlockSpec((NH, HD, HD), lambda i: (0, 0, 0)), _vec(1, D),
            pl.BlockSpec((NH, HD, HD), lambda i: (0, 0, 0)), _vec(1, D),
            _vec(1, D),
        ],
        out_specs=[_row(D), _row(D)],
        out_shape=[jax.ShapeDtypeStruct((s, D), F32), jax.ShapeDtypeStruct((s, D), F32)],
        scratch_shapes=[pltpu.VMEM((8, D), F32)],
        compiler_params=_params(dimension_semantics=("arbitrary",)),
    )(zr, zr, conv_w8, conv_b, w_r, b_r, w_i, b_i, lam)


def _lru_bwd(zr, xc, h, dh, conv_w8, w_r, b_r, w_i, b_i, lam):
    s = zr.shape[0]
    n = s // TS
    xl_col = 1

    def rev(i):
        return n - 1 - i

    def body(xl_ref, xlh_ref, xc_ref, h_ref, hh_ref, dh_ref, cw_ref, wr_ref, br_ref, wi_ref, bi_ref, lam_ref,
             dxl_ref, gwr_ref, gwi_ref, gbr_ref, gbi_ref, glam_ref, gcb_ref, gcw_ref, l_buf, dxc_buf, carry_g, carry_dxc):
        i = pl.program_id(0)
        first = rev(i) == 0

        @pl.when(i == 0)
        def _():
            carry_g[...] = jnp.zeros_like(carry_g)
            carry_dxc[...] = jnp.zeros_like(carry_dxc)
            for ref in (gwr_ref, gwi_ref, gbr_ref, gbi_ref, glam_ref, gcb_ref, gcw_ref):
                ref[...] = jnp.zeros_like(ref)

        rows = _rows_iota(TS)
        xc = xc_ref[...]
        lam = lam_ref[...]
        r = _gates(xc, wr_ref, br_ref[...], _sigmoid_rel)
        ig = _gates(xc, wi_ref, bi_ref[...], _sigmoid)
        sp, a, gam, inv_gam = _lru_coeffs(r, lam)
        g = dh_ref[...] + jnp.where(rows == TS - 1, carry_g[0:1, :], 0.0)
        b = jnp.where(rows == TS - 1, 0.0, pltpu.roll(a, TS - 1, 0))
        l_buf[...] = _scan(b, g, True)[1]
        lv = l_buf[...]
        carry_g[0:1, :] = l_buf[0:1, :] * a[0:1, :]
        h_prev_row = jnp.where(first, 0.0, hh_ref[7:8, :])
        h_prev = jnp.where(rows == 0, h_prev_row, pltpu.roll(h_ref[...], 1, 0))
        dgam = lv * ig * xc
        dig = lv * gam * xc
        dxc = lv * gam * ig
        dla = lv * h_prev * a - dgam * (a * a) * inv_gam
        dr = dla * (-LRU_C) * sp
        glam_ref[...] += _colsum(dla * r) * (LRU_C * _sigmoid_rel(-lam))
        dpr = dr * r * (1.0 - r)
        dpi = dig * ig * (1.0 - ig)
        gbr_ref[...] += _colsum(dpr)
        gbi_ref[...] += _colsum(dpi)
        xb = xc.astype(BF16)
        dprb = dpr.astype(BF16)
        dpib = dpi.astype(BF16)
        back = []
        for gi in range(NH):
            cs = slice(HD * gi, HD * (gi + 1))
            gwr_ref[gi] += lax.dot_general(xb[:, cs], dprb[:, cs], TN_DIMS, preferred_element_type=F32)
            gwi_ref[gi] += lax.dot_general(xb[:, cs], dpib[:, cs], TN_DIMS, preferred_element_type=F32)
            back.append(lax.dot_general(dprb[:, cs], wr_ref[gi], NT_DIMS, preferred_element_type=F32)
                        + lax.dot_general(dpib[:, cs], wi_ref[gi], NT_DIMS, preferred_element_type=F32))
        dxc = dxc + jnp.concatenate(back, axis=1)
        dxc_buf[...] = dxc
        gcb_ref[...] += _colsum(dxc)
        halo = jnp.where(first, 0.0, xlh_ref[...])
        taps = _conv_taps(jnp.concatenate([halo, xl_ref[...]], axis=0), TS)
        for jj in range(CONV_W):
            gcw_ref[jj:jj + 1, :] += _colsum(dxc * taps[jj])
        ext = jnp.concatenate([dxc, carry_dxc[...]], axis=0)
        dxl = sum(cw_ref[jj:jj + 1, :] * _shift_up(ext, CONV_W - 1 - jj, TS) for jj in range(CONV_W))
        dxl_ref[...] = dxl.astype(BF16)
        carry_dxc[...] = dxc_buf[0:8, :]

    rowr = lambda c, col=0: pl.BlockSpec((TS, c), lambda i: (rev(i), col))
    halo = lambda col: pl.BlockSpec((8, D), lambda i: (jnp.maximum(rev(i) * (TS // 8) - 1, 0), col))
    gate_w = pl.BlockSpec((NH, HD, HD), lambda i: (0, 0, 0))
    return pl.pallas_call(
        body, name="lru_bwd", grid=(n,),
        in_specs=[rowr(D, xl_col), halo(xl_col), rowr(D), rowr(D), halo(0), rowr(D),
                  _vec(8, D), gate_w, _vec(1, D), gate_w, _vec(1, D), _vec(1, D)],
        out_specs=[rowr(D), gate_w, gate_w, _vec(1, D), _vec(1, D), _vec(1, D), _vec(1, D), _vec(8, D)],
        out_shape=[
            jax.ShapeDtypeStruct((s, D), BF16),
            jax.ShapeDtypeStruct((NH, HD, HD), F32), jax.ShapeDtypeStruct((NH, HD, HD), F32),
            jax.ShapeDtypeStruct((1, D), F32), jax.ShapeDtypeStruct((1, D), F32), jax.ShapeDtypeStruct((1, D), F32),
            jax.ShapeDtypeStruct((1, D), F32), jax.ShapeDtypeStruct((8, D), F32),
        ],
        scratch_shapes=[pltpu.VMEM((TS, D), F32), pltpu.VMEM((TS, D), F32), pltpu.VMEM((8, D), F32), pltpu.VMEM((8, D), F32)],
        compiler_params=_params(dimension_semantics=("arbitrary",)),
    )(zr, zr, xc, h, h, dh, conv_w8, w_r, b_r, w_i, b_i, lam)


def _silu_parts(g):
    sg = _sigmoid(g)
    return g * sg, sg * (1.0 + g * (1.0 - sg))


def _branch_out_bwd(o, h, zr, dmix, w_out, gain_a, gain_l):
    s = o.shape[0]

    def body(o_ref, ga_ref, h_ref, gl_ref, dm_ref, w_ref, ka_ref, kl_ref,
             do_ref, dl_ref, dga_ref, dh_ref, dgl_ref, gka_ref, gkl_ref):
        @pl.when(pl.program_id(0) == 0)
        def _():
            gka_ref[...] = jnp.zeros_like(gka_ref)
            gkl_ref[...] = jnp.zeros_like(gkl_ref)

        dycat = lax.dot_general(dm_ref[...], w_ref[...], NT_DIMS, preferred_element_type=F32)

        def one(v, g, dy, gain):
            vhat, rstd = _rms_fwd(v)
            sg, dsg = _silu_parts(g)
            dn = dy * sg
            dg = dy * (vhat * gain) * dsg
            return _rms_bwd(vhat, rstd, dn * gain), dg, _colsum(dn * vhat)

        o = o_ref[...]
        dout, dga, gka = one(o, ga_ref[...], dycat[:, :D], ka_ref[...])
        do_ref[...] = dout.astype(BF16)
        dga_ref[...] = dga.astype(BF16)
        gka_ref[...] += gka
        prod = dout * o
        for hh in range(NH):
            dl_ref[hh] = jnp.broadcast_to(jnp.sum(prod[:, HD * hh:HD * (hh + 1)], axis=1, keepdims=True), (TS, HD))
        dh, dgl, gkl = one(h_ref[...], gl_ref[...], dycat[:, D:], kl_ref[...])
        dh_ref[...] = dh
        dgl_ref[...] = dgl.astype(BF16)
        gkl_ref[...] += gkl

    return pl.pallas_call(
        body, name="branch_out_bwd", grid=(s // TS,),
        in_specs=[_row(D), _row(D, 0), _row(D), _row(D, 2), _row(D), _vec(2 * D, D), _vec(1, D), _vec(1, D)],
        out_specs=[_row(D), pl.BlockSpec((NH, TS, HD), lambda i: (0, i, 0)), _row(D), _row(D), _row(D), _vec(1, D), _vec(1, D)],
        out_shape=[
            jax.ShapeDtypeStruct((s, D), BF16), jax.ShapeDtypeStruct((NH, s, HD), F32), jax.ShapeDtypeStruct((s, D), BF16),
            jax.ShapeDtypeStruct((s, D), F32), jax.ShapeDtypeStruct((s, D), BF16),
            jax.ShapeDtypeStruct((1, D), F32), jax.ShapeDtypeStruct((1, D), F32),
        ],
        compiler_params=_params(dimension_semantics=("arbitrary",)),
    )(o, zr, h, zr, dmix, w_out, gain_a, gain_l)


def _residual(x, o, h, zr, gain_a, gain_l, w_out, post_gain):
    s = x.shape[0]

    def body(x_ref, o_ref, ga_ref, h_ref, gl_ref, ka_ref, kl_ref, w_ref, g_ref, y_ref, m_ref, h1_ref, hb_ref):
        ohat, _ = _rms_fwd(o_ref[...])
        y_ref[:, 0:D] = (ohat * ka_ref[...] * _silu_parts(ga_ref[...])[0]).astype(BF16)
        hhat, _ = _rms_fwd(h_ref[...])
        y_ref[:, D:2 * D] = (hhat * kl_ref[...] * _silu_parts(gl_ref[...])[0]).astype(BF16)
        mix = jnp.dot(y_ref[...], w_ref[...], preferred_element_type=F32)
        m_ref[...] = mix
        mhat, _ = _rms_fwd(mix)
        h1 = x_ref[...] + mhat * g_ref[...]
        h1_ref[...] = h1
        hb_ref[...] = h1.astype(BF16)

    return pl.pallas_call(
        body, name="residual", grid=(s // TS,),
        in_specs=[_row(D), _row(D), _row(D, 0), _row(D), _row(D, 2), _vec(1, D), _vec(1, D), _vec(2 * D, D), _vec(1, D)],
        out_specs=[_row(2 * D), _row(D), _row(D), _row(D)],
        out_shape=[jax.ShapeDtypeStruct((s, 2 * D), BF16), jax.ShapeDtypeStruct((s, D), F32),
                   jax.ShapeDtypeStruct((s, D), F32), jax.ShapeDtypeStruct((s, D), BF16)],
        compiler_params=_params(dimension_semantics=("parallel",)),
    )(x, o, zr, h, zr, gain_a, gain_l, w_out, post_gain)


def _head(h1, p, tgt, mix, w_gate, w_ple, ple_gain, b_gate, post_gain):
    s = h1.shape[0]

    def body(h_ref, p_ref, t_ref, m_ref, wg_ref, wp_ref, kg_ref, b_ref, pg_ref,
             loss_ref, dgp_ref, dpe_ref, dh_ref, dm_ref, gk_ref, gb_ref, gg_ref):
        @pl.when(pl.program_id(0) == 0)
        def _():
            for ref in (loss_ref, gk_ref, gb_ref, gg_ref):
                ref[...] = jnp.zeros_like(ref)

        h1 = h_ref[...]
        pe = jnp.dot(p_ref[...].astype(BF16), wp_ref[...], preferred_element_type=F32)
        gp = jnp.dot(h1.astype(BF16), wg_ref[...], preferred_element_type=F32)
        ehat, rstd = _rms_fwd(pe)
        e = ehat * kg_ref[...]
        gate = _sigmoid(gp + b_ref[...])
        diff = (h1 + gate * e) - t_ref[...]
        per_row = jnp.mean(diff * diff, axis=-1, keepdims=True)
        loss_ref[...] += 0.5 * jnp.sum(per_row, axis=0, keepdims=True)
        dy = diff * (1.0 / D)
        dgp = dy * e * gate * (1.0 - gate)
        dgpb = dgp.astype(BF16)
        dgp_ref[...] = dgpb
        gb_ref[...] += _colsum(dgp)
        de = dy * gate
        gk_ref[...] += _colsum(de * ehat)
        dpe_ref[...] = _rms_bwd(ehat, rstd, de * kg_ref[...]).astype(BF16)
        dh1 = dy + lax.dot_general(dgpb, wg_ref[...], NT_DIMS, preferred_element_type=F32)
        dh_ref[...] = dh1
        mhat, rstd_m = _rms_fwd(m_ref[...])
        gg_ref[...] += _colsum(dh1 * mhat)
        dm_ref[...] = _rms_bwd(mhat, rstd_m, dh1 * pg_ref[...]).astype(BF16)

    return pl.pallas_call(
        body, name="head", grid=(s // TS,),
        in_specs=[_row(D), _row(D_PLE), _row(D), _row(D), _vec(D, D), _vec(D_PLE, D), _vec(1, D), _vec(1, D), _vec(1, D)],
        out_specs=[_vec(1, 1), _row(D), _row(D), _row(D), _row(D), _vec(1, D), _vec(1, D), _vec(1, D)],
        out_shape=[
            jax.ShapeDtypeStruct((1, 1), F32), jax.ShapeDtypeStruct((s, D), BF16), jax.ShapeDtypeStruct((s, D), BF16),
            jax.ShapeDtypeStruct((s, D), F32), jax.ShapeDtypeStruct((s, D), BF16),
            jax.ShapeDtypeStruct((1, D), F32), jax.ShapeDtypeStruct((1, D), F32), jax.ShapeDtypeStruct((1, D), F32),
        ],
        compiler_params=_params(dimension_semantics=("arbitrary",)),
    )(h1, p, tgt, mix, w_gate, w_ple, ple_gain, b_gate, post_gain)


def _prenorm_bwd(x, dxn_a, dz_rest, w_rest_t, dh1, pre_gain, after):
    s = x.shape[0]
    ks = [a.shape[1] for a in dz_rest]
    assert sum(ks) == w_rest_t.shape[0]
    nz = len(dz_rest)
    tm = 512
    rowm = lambda c: pl.BlockSpec((tm, c), lambda i: (i, 0))

    def body(*refs):
        x_ref, da_ref = refs[:2]
        w_ref, dh_ref, g_ref, after_ref, dx_ref, gg_ref = refs[2 + nz:]

        @pl.when(pl.program_id(0) == 0)
        def _():
            gg_ref[...] = jnp.zeros_like(gg_ref)

        dxn, k0 = da_ref[...], 0
        for dz_ref, kw in zip(refs[2:2 + nz], ks):
            dxn = dxn + jnp.dot(dz_ref[...], w_ref[k0:k0 + kw, :], preferred_element_type=F32)
            k0 += kw
        xhat, rstd = _rms_fwd(x_ref[...])
        gg_ref[...] += _colsum(dxn * xhat)
        dx_ref[...] = dh_ref[...] + _rms_bwd(xhat, rstd, dxn * g_ref[...])

    return pl.pallas_call(
        body, name="prenorm_bwd", grid=(s // tm,),
        in_specs=[rowm(D), rowm(D)] + [rowm(kw) for kw in ks] + [_vec(*w_rest_t.shape), rowm(D), _vec(1, D), _vec(8, HD)],
        out_specs=[rowm(D), _vec(1, D)],
        out_shape=[jax.ShapeDtypeStruct((s, D), F32), jax.ShapeDtypeStruct((1, D), F32)],
        compiler_params=_params(dimension_semantics=("arbitrary",)),
    )(x, dxn_a, *dz_rest, w_rest_t, dh1, pre_gain, after)


def _adamw(name, parts, w, m, v, own=None, me=None, patch=None):
    r, c = w.shape
    if r % 8 == 0:
        tr = _pick(r, (256, 128, 16, 8))
        grid = (r // tr,)
        blk = pl.BlockSpec((tr, c), lambda i: (i, 0))
        parts_blk = pl.BlockSpec((N_DEV, tr, c), lambda i: (0, i, 0))
    else:
        tc = _pick(c, (256, 128))
        grid = (c // tc,)
        blk = pl.BlockSpec((r, tc), lambda i: (0, i))
        parts_blk = pl.BlockSpec((N_DEV, r, tc), lambda i: (0, 0, i))

    def body(*refs):
        p_ref, w_ref, m_ref, v_ref = refs[:4]
        g_ref, d_ref, nm_ref, nv_ref = refs[-4:]
        if own is None:
            g = p_ref[0].astype(F32)
            for j in range(1, N_DEV):
                g = g + p_ref[j].astype(F32)
            g_ref[...] = g
        else:
            own_ref, me_ref = refs[4:6]
            g_ref[...] = jnp.zeros_like(g_ref)
            for j in range(N_DEV):
                @pl.when(me_ref[0] == j)
                def _():
                    g_ref[...] += own_ref[...].astype(F32)

                @pl.when(me_ref[0] != j)
                def _():
                    g_ref[...] += p_ref[j].astype(F32)
            if patch is not None:
                for dev, row, cnt, src in patch[1]:
                    @pl.when(me_ref[0] == dev)
                    def _():
                        g_ref[row:row + cnt, :] += refs[6][src:src + cnt, :]
            g = g_ref[...]
        nm = ADAM_B1 * m_ref[...] + (1.0 - ADAM_B1) * g
        nv = ADAM_B2 * v_ref[...] + (1.0 - ADAM_B2) * (g * g)
        nm_ref[...] = nm
        nv_ref[...] = nv
        m_hat = nm / (1.0 - ADAM_B1 ** ADAM_STEP)
        v_hat = nv / (1.0 - ADAM_B2 ** ADAM_STEP)
        d_ref[...] = -ADAM_LR * (m_hat / (jnp.sqrt(v_hat) + ADAM_EPS) + ADAM_WD * w_ref[...])

    in_specs, args = [parts_blk, blk, blk, blk], [parts, w, m, v]
    if own is not None:
        in_specs += [blk, pl.BlockSpec(memory_space=pltpu.SMEM)]
        args += [own, me]
        if patch is not None:
            assert r % 8 != 0, "the patch rows are taken whole per column tile"
            in_specs += [pl.BlockSpec((patch[0].shape[0], tc), lambda i: (0, i))]
            args += [patch[0]]
    return pl.pallas_call(
        body, name=name, grid=grid,
        in_specs=in_specs,
        out_specs=[blk] * 4,
        out_shape=[jax.ShapeDtypeStruct((r, c), F32)] * 4,
        compiler_params=_params(dimension_semantics=("parallel",)),
    )(*args)


def _spread8(v):
    r = v.shape[0]
    return jnp.pad(jnp.pad(v[:, :, None], ((0, 0), (0, 0), (0, 7))).reshape(r, 8 * NH), ((0, 0), (0, HD - 8 * NH)))


def _gather8(v):
    return v[:, :8 * NH].reshape(v.shape[0], NH, 8)[:, :, 0]


def _cols_to_shards(g):
    r, c8 = g.shape
    return g.reshape(r, N_DEV, c8 // N_DEV).transpose(1, 0, 2)


def _shards_to_cols(g):
    n, r, c = g.shape
    return g.transpose(1, 0, 2).reshape(r, n * c)


def kernel(x, p, w_in, b_f, pre_gain, post_gain, conv_w, conv_b, w_rgate, b_rgate, w_igate, b_igate, lru_lambda, attn_out_gain, lru_out_gain, w_out, w_ple, ple_gain, w_ple_gate, b_ple_gate, loss_target, m_w_in, m_b_f, m_pre_gain, m_post_gain, m_conv_w, m_conv_b, m_w_rgate, m_b_rgate, m_w_igate, m_b_igate, m_lru_lambda, m_attn_out_gain, m_lru_out_gain, m_w_out, m_w_ple, m_ple_gain, m_w_ple_gate, m_b_ple_gate, v_w_in, v_b_f, v_pre_gain, v_post_gain, v_conv_w, v_conv_b, v_w_rgate, v_b_rgate, v_w_igate, v_b_igate, v_lru_lambda, v_attn_out_gain, v_lru_out_gain, v_w_out, v_w_ple, v_ple_gain, v_w_ple_gate, v_b_ple_gate):
    me = 4 * lax.axis_index("x") + 2 * lax.axis_index("y") + lax.axis_index("c")
    x2, p2, tgt = x[0], p[0, 0], loss_target[0]

    conv_w_shard8 = jnp.pad(conv_w[0], ((0, 8 - CONV_W), (0, 0)))
    wt, m_wt, v_wt = w_in[0].T, m_w_in[0].T, v_w_in[0].T
    g_wint, g_conv = _gather_two_level("gather_w_in", [wt.astype(BF16), conv_w_shard8])
    win_t = g_wint.reshape(D_IN, D)
    rest_state, rest_token = _exchange_start(
        "gather_rest_start", [w_out[0].astype(BF16), w_ple[0].astype(BF16), w_ple_gate[0].astype(BF16)], ["bcast"] * 3,
        after=g_conv)
    w_rest_t = jnp.concatenate([win_t[D_QKV + NH:], _spread8(win_t[D_QKV:D_QKV + NH].T).T], axis=0)
    conv_w8 = _shards_to_cols(g_conv)
    bf_pad = _spread8(b_f)
    w_r, w_i = w_rgate[0].astype(BF16), w_igate[0].astype(BF16)

    xn, zq = _prenorm_proj(x2, pre_gain, win_t, D_QKV, rest_token)
    zr, kx = _proj_rest(xn, w_rest_t, bf_pad)
    o, ax = _attn_fwd(zq, kx)
    xc, h = _lru_fwd(zr, conv_w8, conv_b, w_r, b_rgate, w_i, b_igate, lru_lambda)
    g_wout, g_wple, g_wpg = _exchange_wait("gather_rest_wait", rest_state, h)
    wout_full = g_wout.reshape(2 * D, D)
    wple_full = _shards_to_cols(g_wple)
    wpg_full = g_wpg.reshape(D, D)
    ycat, mix, h1, h1b = _residual(x2, o, h, zr, attn_out_gain, lru_out_gain, wout_full, post_gain)

    loss_part, dgp, dpe, dh1, dmix, g_ple_gain, g_b_gate, g_post_gain = _head(
        h1, p2, tgt, mix, wpg_full, wple_full, ple_gain, b_ple_gate, post_gain)
    gw_pg = _mm("bwd_gate_w", h1b, dgp, "tn", BF16)
    gw_ple = _mm("bwd_ple_w", p2, dpe, "tn", BF16)
    gw_out = _mm("bwd_out_w", ycat, dmix, "tn", BF16)
    do, delta, dga, dh, dgl, g_aog, g_log = _branch_out_bwd(o, h, zr, dmix, wout_full, attn_out_gain, lru_out_gain)
    dxl, g_wr, g_wi, g_br, g_bi, g_lam, g_cb, g_cw8 = _lru_bwd(
        zr, xc, h, dh, conv_w8, w_r, b_rgate, w_i, b_igate, lru_lambda)
    gates = jnp.concatenate([g_wr.reshape(D, HD), g_wi.reshape(D, HD)], axis=0).astype(BF16)
    outw_state, outw_token = _exchange_start(
        "exchange_outw_start",
        [gw_out.reshape(N_DEV, 2 * D // N_DEV, D), _cols_to_shards(gw_ple), gw_pg.reshape(N_DEV, D // N_DEV, D), gates],
        ["scatter"] * 3 + ["bcast"])
    dq, dk, dv, dcs, drs = _attn_bwd(zq, do, ax, delta, kx, outw_token)
    tn = lambda nm, dz: _mm("bwd_w_" + nm, dz, xn, "tn", BF16)
    gw_in_t = jnp.concatenate([tn("q", dq), tn("k", dk), tn("v", dv), jnp.zeros((NH, D), BF16),
                               tn("ga", dga), tn("xl", dxl), tn("gl", dgl)], axis=0)
    inw_state, inw_token = _exchange_start(
        "exchange_inw_start", [gw_in_t.reshape(N_DEV, D_IN_SHARD, D)], ["scatter"])
    dfl, g_bf_pad = _forget_bwd(dcs, drs, zr, bf_pad + inw_token[0:1, :])
    gw_fl = _gather8(_mm("bwd_w_fl", dfl, xn, "tn", F32).T).T
    dxn_a = _mm_cat("bwd_qkv_x", [dq, dk, dv], win_t, F32, after=inw_token)
    grad_x, g_pre_gain = _prenorm_bwd(x2, dxn_a, [dga, dxl, dgl, dfl], w_rest_t, dh1, pre_gain, inw_token)

    upd = {}
    me1 = me.reshape(1).astype(jnp.int32)
    r_wout, r_wple, r_wpg, r_gates = _exchange_wait("exchange_outw_wait", outw_state, grad_x, fill_own=False)
    upd["w_out"] = _adamw("adamw_w_out", r_wout[0], w_out[0], m_w_out[0], v_w_out[0], r_wout[1], me1)
    upd["w_ple"] = _adamw("adamw_w_ple", r_wple[0], w_ple[0], m_w_ple[0], v_w_ple[0], r_wple[1], me1)
    upd["w_ple_gate"] = _adamw("adamw_w_ple_gate", r_wpg[0], w_ple_gate[0], m_w_ple_gate[0], v_w_ple_gate[0], r_wpg[1], me1)
    gates_of = lambda a, b: jnp.concatenate([a[0].reshape(D, HD), b[0].reshape(D, HD)], axis=0)
    g_gates = _adamw("adamw_gates", r_gates[0], gates_of(w_rgate, w_igate), gates_of(m_w_rgate, m_w_igate),
                     gates_of(v_w_rgate, v_w_igate), r_gates[1], me1)
    upd["w_rgate"] = [a[:D].reshape(1, NH, HD, HD) for a in g_gates]
    upd["w_igate"] = [a[D:].reshape(1, NH, HD, HD) for a in g_gates]
    behind = upd["w_out"][0][0:1] + upd["w_ple_gate"][0][0:1] + jnp.pad(g_gates[0][0:1], ((0, 0), (0, D - HD)))
    small = jnp.concatenate(
        [jnp.pad(_gather8(g_bf_pad), ((0, 0), (0, D - NH))), g_pre_gain, g_post_gain, g_cb, g_br, g_bi, g_lam, g_aog, g_log,
         g_ple_gain, g_b_gate, g_cw8[:CONV_W], behind, jnp.pad(loss_part, ((0, 7), (0, D - 1))), gw_fl], axis=0)
    (r_small,) = _exchange("exchange_small", [small], ["bcast"], before=(wt, m_wt, v_wt))
    vec_names = ["b_f", "pre_gain", "post_gain", "conv_b", "b_rgate", "b_igate", "lru_lambda", "attn_out_gain",
                 "lru_out_gain", "ple_gain", "b_ple_gate"]
    vec_w = dict(b_f=(b_f, m_b_f, v_b_f), pre_gain=(pre_gain, m_pre_gain, v_pre_gain),
                 post_gain=(post_gain, m_post_gain, v_post_gain), conv_b=(conv_b, m_conv_b, v_conv_b),
                 b_rgate=(b_rgate, m_b_rgate, v_b_rgate), b_igate=(b_igate, m_b_igate, v_b_igate),
                 lru_lambda=(lru_lambda, m_lru_lambda, v_lru_lambda),
                 attn_out_gain=(attn_out_gain, m_attn_out_gain, v_attn_out_gain),
                 lru_out_gain=(lru_out_gain, m_lru_out_gain, v_lru_out_gain), ple_gain=(ple_gain, m_ple_gain, v_ple_gain),
                 b_ple_gate=(b_ple_gate, m_b_ple_gate, v_b_ple_gate))
    conv_mine = lambda a: lax.dynamic_slice_in_dim(a, me * HD, HD, axis=1)

    def small_rows(k):
        rows = [jnp.pad(vec_w[nm][k], ((0, 0), (0, D - vec_w[nm][k].shape[1]))) for nm in vec_names]
        cw = (conv_w, m_conv_w, v_conv_w)[k][0]
        full = lax.dynamic_update_slice_in_dim(jnp.ones((CONV_W, D), F32), cw, me * HD, axis=1)
        return jnp.concatenate(rows + [full, jnp.ones((17, D), F32)], axis=0)

    g_small = _adamw("adamw_small", r_small, small_rows(0), small_rows(1), small_rows(2))
    loss = g_small[0][16, 0]
    for idx, nm in enumerate(vec_names):
        width = vec_w[nm][0].shape[1]
        upd[nm] = [a[idx:idx + 1, :width] for a in g_small]
    base = len(vec_names)
    upd["conv_w"] = [conv_mine(a[base:base + CONV_W])[None] for a in g_small]
    (r_win,) = _exchange_wait("exchange_inw_wait", inw_state, g_small[0], fill_own=False)
    fl_sum = g_small[0][24:24 + NH]
    dev_a, row_a = divmod(D_QKV, D_IN_SHARD)
    n_a = D_IN_SHARD - row_a
    assert 2 * n_a == NH, "the forget rows are taken to straddle two row blocks evenly"
    patch = (fl_sum, [(dev_a, row_a, n_a, 0), (dev_a + 1, 0, n_a, n_a)])
    upd["w_in"] = [a.T for a in _adamw("adamw_w_in", r_win[0], wt, m_wt, v_wt, r_win[1], me1, patch)]
    for nm in ("w_in", "w_out", "w_ple", "w_ple_gate"):
        upd[nm] = [a[None] for a in upd[nm]]

    order = ["w_in", "b_f", "pre_gain", "post_gain", "conv_w", "conv_b", "w_rgate", "b_rgate", "w_igate", "b_igate",
             "lru_lambda", "attn_out_gain", "lru_out_gain", "w_out", "w_ple", "ple_gain", "w_ple_gate", "b_ple_gate"]
    outs = [loss, grad_x[None]]
    for k in range(4):
        outs += [upd[nm][k] for nm in order]
    return tuple(outs)
```
